```python
import math
import jax
import jax.numpy as jnp
from jax import lax
import numpy as np

D_MODEL = 1024
BATCH = 2
SEQ = 16384
DEPTH = 2

GRID_W = 64
CTX_LEN = 256

HEAD_DIM = 64
MIX_WIDTH = D_MODEL
ATT_WIDTH = MIX_WIDTH // 2
RWKV_WIDTH = MIX_WIDTH // 4
HY_WIDTH = MIX_WIDTH - ATT_WIDTH - RWKV_WIDTH

ATT_HEADS = ATT_WIDTH // HEAD_DIM
ATT_KV_HEADS = 2
ATT_REP = ATT_HEADS // ATT_KV_HEADS
ATT_KV_WIDTH = ATT_KV_HEADS * HEAD_DIM
ROPE_THETA = 10000.0
QK_EPS = 1e-6
Q_BLOCK = 128

RWKV_HEADS = RWKV_WIDTH // HEAD_DIM
W_LORA = 16
A_LORA = 16
G_LORA = 32
RWKV_GN_EPS = 64e-5

HY_ORDER = 2
HY_EMB = 33
HY_FFN = 64
HY_SHORT = 3

FF_DENSE = 2816
N_EXPERTS = 8
TOP_K = 2
FF_EXPERT = 2816
LN_EPS = 1e-6

IN_ATT = ATT_WIDTH + 2 * ATT_KV_WIDTH
RWKV_LORA = 2 * W_LORA + 2 * A_LORA + G_LORA
IN_RWKV = 3 * RWKV_WIDTH + RWKV_LORA
IN_HY = (HY_ORDER + 1) * HY_WIDTH
IN_WIDTH = IN_ATT + IN_RWKV + IN_HY

kernel_name = 'hybrid_diffusion_trunk_attn_rwkv7_hyena_moe'


def layer_norm(x):
    xf = x.astype(jnp.float32)
    xc = xf - jnp.mean(xf, -1, keepdims=True)
    var = jnp.mean(xc * xc, -1, keepdims=True)
    return (xc * lax.rsqrt(var + LN_EPS)).astype(x.dtype)


def modulate(x, shift, scale):
    return layer_norm(x) * (1 + scale) + shift


def post_norm(x, g, b):
    return layer_norm(x) * g + b


def rms_norm(x, g):
    xf = x.astype(jnp.float32)
    return (xf * lax.rsqrt(jnp.mean(xf * xf, -1, keepdims=True) + QK_EPS)).astype(x.dtype) * g


def axial_rope_tables(L):
    rows = L // GRID_W
    row = jnp.repeat(jnp.arange(rows, dtype=jnp.float32), GRID_W)
    col = jnp.tile(jnp.arange(GRID_W, dtype=jnp.float32), rows)
    n_freq = HEAD_DIM // 4
    inv_freq = ROPE_THETA ** (-jnp.arange(n_freq, dtype=jnp.float32) / n_freq)
    ang = jnp.concatenate([row[:, None] * inv_freq, col[:, None] * inv_freq], -1)
    return jnp.cos(ang), jnp.sin(ang)


def apply_rope(x, cos, sin):
    B, L, H, _ = x.shape
    xp = x.astype(jnp.float32).reshape(B, L, H, HEAD_DIM // 2, 2)
    x0, x1 = xp[..., 0], xp[..., 1]
    c, s = cos[None, :, None, :], sin[None, :, None, :]
    return jnp.stack([x0 * c - x1 * s, x0 * s + x1 * c], -1).reshape(x.shape).astype(x.dtype)


def qkv_heads(u, q_gain, k_gain):
    B, L, _ = u.shape
    q = u[..., :ATT_WIDTH].reshape(B, L, ATT_HEADS, HEAD_DIM)
    k = u[..., ATT_WIDTH:ATT_WIDTH + ATT_KV_WIDTH].reshape(B, L, ATT_KV_HEADS, HEAD_DIM)
    v = u[..., ATT_WIDTH + ATT_KV_WIDTH:IN_ATT].reshape(B, L, ATT_KV_HEADS, HEAD_DIM)
    return rms_norm(q, q_gain), rms_norm(k, k_gain), v


def gqa_attend(q, k, v):
    s = jnp.einsum('bqgrd,bkgd->bgrqk', q, k, preferred_element_type=jnp.float32) * (HEAD_DIM ** -0.5)
    p = jax.nn.softmax(s, axis=-1).astype(v.dtype)
    return jnp.einsum('bgrqk,bkgd->bqgrd', p, v)


def attention_group(u_lat, u_ctx, lp, rope, need_ctx):
    ql, kl, vl = qkv_heads(u_lat, lp['q_gain'], lp['k_gain'])
    qc, kc, vc = qkv_heads(u_ctx, lp['q_gain'], lp['k_gain'])
    cos, sin = rope
    ql = apply_rope(ql, cos, sin)
    kl = apply_rope(kl, cos, sin)
    k_all = jnp.concatenate([kl, kc], 1)
    v_all = jnp.concatenate([vl, vc], 1)
    B, L = ql.shape[:2]
    n_blk = L // Q_BLOCK
    qb = ql.reshape(B, n_blk, Q_BLOCK, ATT_KV_HEADS, ATT_REP, HEAD_DIM).swapaxes(0, 1)
    ob = lax.map(lambda q: gqa_attend(q, k_all, v_all), qb)
    out_lat = ob.swapaxes(0, 1).reshape(B, L, ATT_WIDTH)
    if not need_ctx:
        return out_lat, None
    Lc = qc.shape[1]
    out_ctx = gqa_attend(qc.reshape(B, Lc, ATT_KV_HEADS, ATT_REP, HEAD_DIM), kc, vc).reshape(B, Lc, ATT_WIDTH)
    return out_lat, out_ctx


def token_shift(u, mu):
    prev = jnp.pad(u[:, :-1], ((0, 0), (1, 0), (0, 0)))
    nxt = jnp.pad(u[:, 1:], ((0, 0), (0, 1), (0, 0)))
    return u + mu[0] * (prev - u) + mu[1] * (nxt - u)


def rwkv_prep(u, lp):
    B, L, _ = u.shape
    C, H, N = RWKV_WIDTH, RWKV_HEADS, HEAD_DIM
    u = token_shift(u, lp['rwkv_mu']).astype(jnp.float32)
    r, k, v = u[..., :C], u[..., C:2 * C], u[..., 2 * C:3 * C]
    o = 3 * C
    lw = u[..., o:o + 2 * W_LORA].reshape(B, L, 2, W_LORA)
    o += 2 * W_LORA
    la = u[..., o:o + 2 * A_LORA].reshape(B, L, 2, A_LORA)
    o += 2 * A_LORA
    lg = u[..., o:o + G_LORA]
    w_raw = lp['rwkv_w0'] + jnp.einsum('bldr,drc->bldc', jnp.tanh(lw), lp['rwkv_wB'])
    w = jnp.exp(-jnp.exp(-jax.nn.softplus(-w_raw) - 0.5))
    a = jax.nn.sigmoid(lp['rwkv_a0'] + jnp.einsum('bldr,drc->bldc', la, lp['rwkv_aB']))
    g = jax.nn.sigmoid(lg) @ lp['rwkv_gB']
    kk = (k * lp['rwkv_kk']).reshape(B, L, H, N)
    kk = kk * lax.rsqrt(jnp.sum(kk * kk, -1, keepdims=True) + 1e-12)
    kd = k[:, :, None, :] * (1 + (a - 1) * lp['rwkv_ka'])
    return {'r': r.reshape(B, L, H, N), 'v': v.reshape(B, L, H, N), 'kk': kk, 'g': g,
            'w': w.reshape(B, L, 2, H, N), 'a': a.reshape(B, L, 2, H, N), 'kd': kd.reshape(B, L, 2, H, N)}


def wkv_scan(S0, pr, d, reverse):
    def step(S, inp):
        r_t, w_t, k_t, v_t, kk_t, a_t = inp
        sa = jnp.einsum('bhvk,bhk->bhv', S, kk_t)
        S = S * w_t[:, :, None, :] - sa[..., None] * (kk_t * a_t)[:, :, None, :] + v_t[..., None] * k_t[:, :, None, :]
        return S, jnp.einsum('bhvk,bhk->bhv', S, r_t)
    xs = tuple(jnp.moveaxis(t, 1, 0) for t in
               (pr['r'], pr['w'][:, :, d], pr['kd'][:, :, d], pr['v'], pr['kk'], pr['a'][:, :, d]))
    S, y = lax.scan(step, S0, xs, reverse=reverse)
    return S, jnp.moveaxis(y, 0, 1)


def rwkv_readout(pr, y_f, y_b, lp):
    bonus = jnp.sum(pr['r'][:, :, None] * pr['kd'] * lp['rwkv_rk'], -1, keepdims=True) * pr['v'][:, :, None]
    y = y_f + y_b + jnp.sum(bonus, 2)
    mu = jnp.mean(y, -1, keepdims=True)
    yc = y - mu
    yn = yc * lax.rsqrt(jnp.mean(yc * yc, -1, keepdims=True) + RWKV_GN_EPS)
    B, L = y.shape[:2]
    yn = yn.reshape(B, L, RWKV_WIDTH) * lp['rwkv_gn_g'] + lp['rwkv_gn_b']
    return yn * pr['g']


def rwkv_group(u_lat, u_ctx, lp, need_ctx):
    pl = rwkv_prep(u_lat, lp)
    pc = rwkv_prep(u_ctx, lp)
    S0 = jnp.zeros((u_lat.shape[0], RWKV_HEADS, HEAD_DIM, HEAD_DIM), jnp.float32)
    S_f, yc_f = wkv_scan(S0, pc, 0, False)
    S_b, yc_b = wkv_scan(S0, pc, 1, True)
    _, yl_f = wkv_scan(S_f, pl, 0, False)
    _, yl_b = wkv_scan(S_b, pl, 1, True)
    out_lat = rwkv_readout(pl, yl_f, yl_b, lp).astype(u_lat.dtype)
    if not need_ctx:
        return out_lat, None
    return out_lat, rwkv_readout(pc, yc_f, yc_b, lp).astype(u_ctx.dtype)


def short_conv3(u, w, b):
    up = jnp.pad(u, ((0, 0), (1, 1), (0, 0)))
    return up[:, :-2] * w[0] + up[:, 1:-1] * w[1] + up[:, 2:] * w[2] + b


def hyena_filter_spectrum(L, lp):
    bands = (HY_EMB - 1) // 2
    t = jnp.linspace(0.0, 1.0, L, dtype=jnp.float32)[:, None]
    f = jnp.linspace(1e-4, bands - 1, bands, dtype=jnp.float32)[None, :]
    wt = 2.0 * math.pi * jnp.arange(L, dtype=jnp.float32)[:, None] / L
    z = jnp.concatenate([t, jnp.cos(f * wt), -jnp.sin(f * wt)], -1)
    h = jnp.sin(lp['hy_freq1'] * (z @ lp['hy_w1'] + lp['hy_b1']))
    h = jnp.sin(lp['hy_freq2'] * (h @ lp['hy_w2'] + lp['hy_b2']))
    h = ((h @ lp['hy_w3']) * jnp.exp(-t * lp['hy_decay'])).astype(jnp.float32)
    h = h.reshape(L, HY_ORDER, 2, HY_WIDTH)
    fwd, bwd = h[:, :, 0], h[:, :, 1]
    kern = jnp.concatenate([fwd, jnp.zeros_like(fwd[:1]), bwd[:0:-1]], 0)
    kern = kern / jnp.sum(jnp.abs(kern), axis=0, keepdims=True)
    return jnp.fft.rfft(kern, axis=0)


def fft_long_conv(z, kf, d):
    L = z.shape[1]
    zf = z.astype(jnp.float32)
    y = jnp.fft.irfft(jnp.fft.rfft(zf, n=2 * L, axis=1) * kf[None], n=2 * L, axis=1)[:, :L]
    return (y + zf * d).astype(z.dtype)


def hyena_group(u, lp):
    L = u.shape[1]
    u = short_conv3(u, lp['hy_short_w'], lp['hy_short_b'])
    C = HY_WIDTH
    x1, x2, v = u[..., :C], u[..., C:2 * C], u[..., 2 * C:]
    kf = hyena_filter_spectrum(L, lp)
    z = x1 * fft_long_conv(v, kf[:, 0], lp['hy_bias'][0])
    return x2 * fft_long_conv(z, kf[:, 1], lp['hy_bias'][1])


def token_mixer(h_lat, h_ctx, lp, rope, need_ctx):
    u_lat = h_lat @ lp['w_in']
    u_ctx = h_ctx @ lp['w_in']
    s_att = slice(0, IN_ATT)
    s_rwkv = slice(IN_ATT, IN_ATT + IN_RWKV)
    s_hy = slice(IN_ATT + IN_RWKV, IN_WIDTH)
    att_l, att_c = attention_group(u_lat[..., s_att], u_ctx[..., s_att], lp, rope, need_ctx)
    rw_l, rw_c = rwkv_group(u_lat[..., s_rwkv], u_ctx[..., s_rwkv], lp, need_ctx)
    hy_l = hyena_group(u_lat[..., s_hy], lp)
    o_lat = jnp.concatenate([att_l, rw_l, hy_l], -1) @ lp['w_out']
    if not need_ctx:
        return o_lat, None
    hy_c = hyena_group(u_ctx[..., s_hy], lp)
    o_ctx = jnp.concatenate([att_c, rw_c, hy_c], -1) @ lp['w_out']
    return o_lat, o_ctx


def swiglu(h, w1, w3, w2):
    return (jax.nn.silu(h @ w1) * (h @ w3)) @ w2


def moe_swiglu(h, w_router, w1, w3, w2):
    logits = jnp.einsum('bld,de->ble', h, w_router).astype(jnp.float32)
    top_v, top_i = lax.top_k(logits, TOP_K)
    gates = jax.nn.softmax(top_v, axis=-1)
    dense_gate = jnp.sum(jax.nn.one_hot(top_i, N_EXPERTS, dtype=jnp.float32) * gates[..., None], -2)
    dense_gate = dense_gate.astype(h.dtype)
    out = jnp.zeros_like(h)
    for e in range(N_EXPERTS):
        out = out + dense_gate[..., e:e + 1] * swiglu(h, w1[e], w3[e], w2[e])
    return out


def setup_inputs(seed: int = 0) -> dict:
    key = jax.random.key(seed)
    keys = iter(jax.random.split(key, 64))

    def nrm(shape, scale=1.0):
        return scale * jax.random.normal(next(keys), shape, jnp.float32)

    def uni(shape, lo, hi):
        return jax.random.uniform(next(keys), shape, jnp.float32, lo, hi)

    def gain(shape):
        return 1.0 + nrm(shape, 0.02)

    D = D_MODEL
    beta = (8 * DEPTH) ** -0.25
    n_dense = (DEPTH + 1) // 2
    n_moe = DEPTH // 2
    n_filt = HY_ORDER * 2 * HY_WIDTH
    return {
        'x': nrm((BATCH, SEQ, D)),
        'c': nrm((BATCH, D)),
        'ctx': nrm((BATCH, CTX_LEN, D)),
        'c_ctx': nrm((D,)),
        'ada_w': nrm((DEPTH, D, 6 * D), 0.5 * D ** -0.5),
        'ada_b': nrm((DEPTH, 6 * D), 0.02),
        'w_in': nrm((DEPTH, D, IN_WIDTH), D ** -0.5),
        'w_out': nrm((DEPTH, MIX_WIDTH, D), beta * MIX_WIDTH ** -0.5),
        'q_gain': gain((DEPTH, HEAD_DIM)),
        'k_gain': gain((DEPTH, HEAD_DIM)),
        'rwkv_mu': uni((DEPTH, 2, IN_RWKV), 0.0, 0.5),
        'rwkv_w0': uni((DEPTH, 2, RWKV_WIDTH), -5.0, 1.0),
        'rwkv_wB': nrm((DEPTH, 2, W_LORA, RWKV_WIDTH), W_LORA ** -0.5),
        'rwkv_a0': nrm((DEPTH, 2, RWKV_WIDTH), 0.1),
        'rwkv_aB': nrm((DEPTH, 2, A_LORA, RWKV_WIDTH), A_LORA ** -0.5),
        'rwkv_gB': nrm((DEPTH, G_LORA, RWKV_WIDTH), G_LORA ** -0.5),
        'rwkv_kk': 0.85 + nrm((DEPTH, RWKV_WIDTH), 0.02),
        'rwkv_ka': gain((DEPTH, RWKV_WIDTH)),
        'rwkv_rk': nrm((DEPTH, RWKV_HEADS, HEAD_DIM), 0.1),
        'rwkv_gn_g': gain((DEPTH, RWKV_WIDTH)),
        'rwkv_gn_b': nrm((DEPTH, RWKV_WIDTH), 0.02),
        'hy_short_w': nrm((DEPTH, HY_SHORT, IN_HY), HY_SHORT ** -0.5),
        'hy_short_b': nrm((DEPTH, IN_HY), 0.02),
        'hy_w1': nrm((DEPTH, HY_EMB, HY_FFN), HY_EMB ** -0.5),
        'hy_b1': nrm((DEPTH, HY_FFN), 0.1),
        'hy_freq1': gain((DEPTH, HY_FFN)),
        'hy_w2': nrm((DEPTH, HY_FFN, HY_FFN), HY_FFN ** -0.5),
        'hy_b2': nrm((DEPTH, HY_FFN), 0.1),
        'hy_freq2': gain((DEPTH, HY_FFN)),
        'hy_w3': nrm((DEPTH, HY_FFN, n_filt), HY_FFN ** -0.5),
        'hy_decay': uni((DEPTH, n_filt), 3.07, 15.35),
        'hy_bias': nrm((DEPTH, HY_ORDER, HY_WIDTH)),
        'ln1_g': gain((DEPTH, D)),
        'ln1_b': nrm((DEPTH, D), 0.02),
        'ln2_g': gain((DEPTH, D)),
        'ln2_b': nrm((DEPTH, D), 0.02),
        'ffn_w1': nrm((n_dense, D, FF_DENSE), D ** -0.5),
        'ffn_w3': nrm((n_dense, D, FF_DENSE), D ** -0.5),
        'ffn_w2': nrm((n_dense, FF_DENSE, D), beta * FF_DENSE ** -0.5),
        'moe_router': nrm((n_moe, D, N_EXPERTS), D ** -0.5),
        'moe_w1': nrm((n_moe, N_EXPERTS, D, FF_EXPERT), D ** -0.5),
        'moe_w3': nrm((n_moe, N_EXPERTS, D, FF_EXPERT), D ** -0.5),
        'moe_w2': nrm((n_moe, N_EXPERTS, FF_EXPERT, D), beta * FF_EXPERT ** -0.5),
    }


def reference(x, c, ctx, c_ctx, ada_w, ada_b, w_in, w_out, q_gain, k_gain,
              rwkv_mu, rwkv_w0, rwkv_wB, rwkv_a0, rwkv_aB, rwkv_gB, rwkv_kk, rwkv_ka, rwkv_rk,
              rwkv_gn_g, rwkv_gn_b, hy_short_w, hy_short_b, hy_w1, hy_b1, hy_freq1, hy_w2, hy_b2,
              hy_freq2, hy_w3, hy_decay, hy_bias, ln1_g, ln1_b, ln2_g, ln2_b,
              ffn_w1, ffn_w3, ffn_w2, moe_router, moe_w1, moe_w3, moe_w2):
    alpha = float((2 * DEPTH) ** 0.25)
    rope = axial_rope_tables(x.shape[1])
    cond_lat = jax.nn.silu(c)
    cond_ctx = jax.nn.silu(c_ctx)
    xc = ctx
    for l in range(DEPTH):
        need_ctx = l < DEPTH - 1
        mod_l = (cond_lat @ ada_w[l] + ada_b[l])[:, None, :]
        mod_c = (cond_ctx @ ada_w[l] + ada_b[l])[None, None, :]
        sh1, sc1, g1, sh2, sc2, g2 = jnp.split(mod_l, 6, axis=-1)
        csh1, csc1, cg1, csh2, csc2, cg2 = jnp.split(mod_c, 6, axis=-1)
        lp = {
            'w_in': w_in[l], 'w_out': w_out[l], 'q_gain': q_gain[l], 'k_gain': k_gain[l],
            'rwkv_mu': rwkv_mu[l], 'rwkv_w0': rwkv_w0[l], 'rwkv_wB': rwkv_wB[l],
            'rwkv_a0': rwkv_a0[l], 'rwkv_aB': rwkv_aB[l], 'rwkv_gB': rwkv_gB[l],
            'rwkv_kk': rwkv_kk[l], 'rwkv_ka': rwkv_ka[l], 'rwkv_rk': rwkv_rk[l],
            'rwkv_gn_g': rwkv_gn_g[l], 'rwkv_gn_b': rwkv_gn_b[l],
            'hy_short_w': hy_short_w[l], 'hy_short_b': hy_short_b[l],
            'hy_w1': hy_w1[l], 'hy_b1': hy_b1[l], 'hy_freq1': hy_freq1[l],
            'hy_w2': hy_w2[l], 'hy_b2': hy_b2[l], 'hy_freq2': hy_freq2[l],
            'hy_w3': hy_w3[l], 'hy_decay': hy_decay[l], 'hy_bias': hy_bias[l],
        }
        o_lat, o_ctx = token_mixer(modulate(x, sh1, sc1), modulate(xc, csh1, csc1), lp, rope, need_ctx)
        x = post_norm(alpha * x + g1 * o_lat, ln1_g[l], ln1_b[l])
        if need_ctx:
            xc = post_norm(alpha * xc + cg1 * o_ctx, ln1_g[l], ln1_b[l])

        j = l // 2
        if l % 2 == 0:
            f_lat = swiglu(modulate(x, sh2, sc2), ffn_w1[j], ffn_w3[j], ffn_w2[j])
        else:
            f_lat = moe_swiglu(modulate(x, sh2, sc2), moe_router[j], moe_w1[j], moe_w3[j], moe_w2[j])
        if need_ctx:
            if l % 2 == 0:
                f_ctx = swiglu(modulate(xc, csh2, csc2), ffn_w1[j], ffn_w3[j], ffn_w2[j])
            else:
                f_ctx = moe_swiglu(modulate(xc, csh2, csc2), moe_router[j], moe_w1[j], moe_w3[j], moe_w2[j])
            xc = post_norm(alpha * xc + cg2 * f_ctx, ln2_g[l], ln2_b[l])
        x = post_norm(alpha * x + g2 * f_lat, ln2_g[l], ln2_b[l])
    return x
```

```python
import functools
import math

import jax
import jax.numpy as jnp
from jax import lax
from jax.experimental import pallas as pl
from jax.experimental.pallas import tpu as pltpu

F32 = jnp.float32
BF16 = jnp.bfloat16
HI = lax.Precision.HIGHEST

D_MODEL = 1024
DEPTH = 2
GRID_W = 64
HEAD_DIM = 64
HALF_HD = HEAD_DIM // 2
ATT_WIDTH = 512
RWKV_WIDTH = 256
HY_WIDTH = 256
ATT_HEADS = 8
ATT_KV_HEADS = 2
ATT_REP = 4
ATT_KV_WIDTH = 128
ROPE_THETA = 10000.0
QK_EPS = 1e-6
RWKV_HEADS = 4
W_LORA = 16
A_LORA = 16
G_LORA = 32
RWKV_GN_EPS = 64e-5
HY_ORDER = 2
HY_EMB = 33
HY_FFN = 64
N_EXPERTS = 8
LN_EPS = 1e-6
IN_ATT = ATT_WIDTH + 2 * ATT_KV_WIDTH
IN_RWKV = 3 * RWKV_WIDTH + 2 * W_LORA + 2 * A_LORA + G_LORA
IN_RWKV_PAD = 896
IN_HY = 3 * HY_WIDTH
ALPHA = float((2 * DEPTH) ** 0.25)

LANES = 128
SUBLANES = 8
TOK_TILE = 256
WKV_CHUNK = 64
DFT_N2 = 256
VMEM_LIMIT = 48 * 1024 * 1024


def _cparams(sem):
    return pltpu.CompilerParams(dimension_semantics=sem, vmem_limit_bytes=VMEM_LIMIT)


def _pick_tile(n, cands):
    for c in cands:
        if n % c == 0:
            return c
    raise ValueError(f"no tile for {n} in {cands}")


def _layer_norm(x):
    mu = jnp.mean(x, -1, keepdims=True)
    xc = x - mu
    var = jnp.mean(xc * xc, -1, keepdims=True)
    return xc * lax.rsqrt(var + LN_EPS)


def _sel_mod(mod_ref, j, row0, tm, lc):
    rid = row0 + lax.broadcasted_iota(jnp.int32, (tm, 1), 0)
    return jnp.where(rid < lc, mod_ref[0, 0, j:j + 1, :], mod_ref[0, 1, j:j + 1, :])


def _silu(x):
    return x * jax.nn.sigmoid(x)


def _ada_kernel(c_ref, w_ref, b_ref, o_ref):
    s = _silu(c_ref[...])
    o_ref[0] = jnp.dot(s, w_ref[0], precision=HI, preferred_element_type=F32) + b_ref[0]


def _ada_mod(cond8, ada_w, ada_b):
    depth, d, n = ada_w.shape
    tn = _pick_tile(n, (1536, 1024, 512, 256, 128))
    return pl.pallas_call(
        _ada_kernel,
        grid=(depth, n // tn),
        in_specs=[pl.BlockSpec((SUBLANES, d), lambda l, j: (0, 0)),
                  pl.BlockSpec((1, d, tn), lambda l, j: (l, 0, j)),
                  pl.BlockSpec((1, 1, tn), lambda l, j: (l, 0, j))],
        out_specs=pl.BlockSpec((1, SUBLANES, tn), lambda l, j: (l, 0, j)),
        out_shape=jax.ShapeDtypeStruct((depth, SUBLANES, n), F32),
        compiler_params=_cparams(("parallel", "parallel")),
        name="ada_mod",
    )(cond8, ada_w, ada_b.reshape(depth, 1, n))


def _inproj_kernel(x_ref, mod_ref, w_ref, oa_ref, or_ref, oh_ref, *, lc, tm):
    row0 = pl.program_id(1) * tm
    sh = _sel_mod(mod_ref, 0, row0, tm, lc)
    sc = _sel_mod(mod_ref, 1, row0, tm, lc)
    h = (_layer_norm(x_ref[0]) * (1.0 + sc) + sh).astype(BF16)
    u = jnp.dot(h, w_ref[...], preferred_element_type=F32)
    oa_ref[0] = u[:, :IN_ATT]
    or_ref[0] = u[:, IN_ATT:IN_ATT + IN_RWKV_PAD]
    oh_ref[0] = u[:, IN_ATT + IN_RWKV_PAD:]


def _inproj(xx, mod, w_pad, lc):
    b, lt, d = xx.shape
    tm = TOK_TILE
    n = w_pad.shape[1]
    return pl.pallas_call(
        functools.partial(_inproj_kernel, lc=lc, tm=tm),
        grid=(b, lt // tm),
        in_specs=[pl.BlockSpec((1, tm, d), lambda bi, i: (bi, i, 0)),
                  pl.BlockSpec((1, 2, 6, d), lambda bi, i: (bi, 0, 0, 0)),
                  pl.BlockSpec((d, n), lambda bi, i: (0, 0))],
        out_specs=[pl.BlockSpec((1, tm, IN_ATT), lambda bi, i: (bi, i, 0)),
                   pl.BlockSpec((1, tm, IN_RWKV_PAD), lambda bi, i: (bi, i, 0)),
                   pl.BlockSpec((1, tm, IN_HY), lambda bi, i: (bi, i, 0))],
        out_shape=[jax.ShapeDtypeStruct((b, lt, IN_ATT), F32),
                   jax.ShapeDtypeStruct((b, lt, IN_RWKV_PAD), F32),
                   jax.ShapeDtypeStruct((b, lt, IN_HY), F32)],
        compiler_params=_cparams(("parallel", "parallel")),
        name="inproj",
    )(xx, mod, w_pad)


def _attn_prep_kernel(u_ref, cos_ref, sin_ref, qg_ref, kg_ref, q_ref, kt_ref, v_ref):
    u = u_ref[0]
    cos = cos_ref[...]
    sin = sin_ref[...]

    def norm_rope(xh, g):
        ms = jnp.mean(xh * xh, -1, keepdims=True)
        xn = xh * lax.rsqrt(ms + QK_EPS) * g
        sw = jnp.concatenate([xn[:, HALF_HD:], xn[:, :HALF_HD]], -1)
        return xn * cos + sw * sin

    qg = qg_ref[...]
    kg = kg_ref[...]
    for h in range(ATT_HEADS):
        qh = norm_rope(u[:, h * HEAD_DIM:(h + 1) * HEAD_DIM], qg)
        q_ref[0, h] = (qh * (HEAD_DIM ** -0.5)).astype(BF16)
    ks = [norm_rope(u[:, ATT_WIDTH + g * HEAD_DIM:ATT_WIDTH + (g + 1) * HEAD_DIM], kg)
          for g in range(ATT_KV_HEADS)]
    kt = jnp.concatenate(ks, -1).T
    for g in range(ATT_KV_HEADS):
        kt_ref[0, g] = kt[g * HEAD_DIM:(g + 1) * HEAD_DIM].astype(BF16)
        v0 = ATT_WIDTH + ATT_KV_WIDTH + g * HEAD_DIM
        v_ref[0, g] = u[:, v0:v0 + HEAD_DIM].astype(BF16)


def _attn_prep(ua, cos, sin, qg, kg):
    b, lt, _ = ua.shape
    tm = TOK_TILE
    return pl.pallas_call(
        _attn_prep_kernel,
        grid=(b, lt // tm),
        in_specs=[pl.BlockSpec((1, tm, IN_ATT), lambda bi, i: (bi, i, 0)),
                  pl.BlockSpec((tm, HEAD_DIM), lambda bi, i: (i, 0)),
                  pl.BlockSpec((tm, HEAD_DIM), lambda bi, i: (i, 0)),
                  pl.BlockSpec((1, HEAD_DIM), lambda bi, i: (0, 0)),
                  pl.BlockSpec((1, HEAD_DIM), lambda bi, i: (0, 0))],
        out_specs=[pl.BlockSpec((1, ATT_HEADS, tm, HEAD_DIM), lambda bi, i: (bi, 0, i, 0)),
                   pl.BlockSpec((1, ATT_KV_HEADS, HEAD_DIM, tm), lambda bi, i: (bi, 0, 0, i)),
                   pl.BlockSpec((1, ATT_KV_HEADS, tm, HEAD_DIM), lambda bi, i: (bi, 0, i, 0))],
        out_shape=[jax.ShapeDtypeStruct((b, ATT_HEADS, lt, HEAD_DIM), BF16),
                   jax.ShapeDtypeStruct((b, ATT_KV_HEADS, HEAD_DIM, lt), BF16),
                   jax.ShapeDtypeStruct((b, ATT_KV_HEADS, lt, HEAD_DIM), BF16)],
        compiler_params=_cparams(("parallel", "parallel")),
        name="attn_prep",
    )(ua, cos, sin, qg, kg)


def _flash_kernel(q_ref, kt_ref, v_ref, o_ref, m_scr, l_scr, acc_scr, *, nk, tq):
    j = pl.program_id(3)

    @pl.when(j == 0)
    def _():
        m_scr[...] = jnp.full(m_scr.shape, -jnp.inf, F32)
        l_scr[...] = jnp.zeros(l_scr.shape, F32)
        acc_scr[...] = jnp.zeros(acc_scr.shape, F32)

    q = q_ref[0].reshape(ATT_REP * tq, HEAD_DIM)
    s = jnp.dot(q, kt_ref[0, 0], preferred_element_type=F32)
    m_prev = m_scr[...]
    m_new = jnp.maximum(m_prev, jnp.max(s, -1, keepdims=True))
    a = jnp.exp(m_prev - m_new)
    p = jnp.exp(s - m_new)
    l_scr[...] = a * l_scr[...] + jnp.sum(p, -1, keepdims=True)
    acc_scr[...] = a * acc_scr[...] + jnp.dot(p.astype(BF16), v_ref[0, 0], preferred_element_type=F32)
    m_scr[...] = m_new

    @pl.when(j == nk - 1)
    def _():
        o = acc_scr[...] / l_scr[...]
        o_ref[0] = jnp.concatenate([o[r * tq:(r + 1) * tq] for r in range(ATT_REP)], -1).astype(o_ref.dtype)


def _flash(q, kt, v, tq, tk, q_tile0, nq, nk):
    b = q.shape[0]
    lq = nq * tq
    return pl.pallas_call(
        functools.partial(_flash_kernel, nk=nk, tq=tq),
        grid=(b, ATT_KV_HEADS, nq, nk),
        in_specs=[pl.BlockSpec((1, ATT_REP, tq, HEAD_DIM), lambda bi, g, i, j: (bi, g, i + q_tile0, 0)),
                  pl.BlockSpec((1, 1, HEAD_DIM, tk), lambda bi, g, i, j: (bi, g, 0, j)),
                  pl.BlockSpec((1, 1, tk, HEAD_DIM), lambda bi, g, i, j: (bi, g, j, 0))],
        out_specs=pl.BlockSpec((1, tq, ATT_REP * HEAD_DIM), lambda bi, g, i, j: (bi, i, g)),
        out_shape=jax.ShapeDtypeStruct((b, lq, ATT_WIDTH), BF16),
        scratch_shapes=[pltpu.VMEM((ATT_REP * tq, 1), F32),
                        pltpu.VMEM((ATT_REP * tq, 1), F32),
                        pltpu.VMEM((ATT_REP * tq, HEAD_DIM), F32)],
        compiler_params=_cparams(("parallel", "parallel", "parallel", "arbitrary")),
        name="flash",
    )(q, kt, v)


def _prev_next(u, up8, un8, i, tm, lc, lt):
    start = i * tm
    p_ok = jnp.logical_and(start != 0, start != lc)
    n_ok = jnp.logical_and(start + tm != lc, start + tm != lt)
    prow = jnp.where(p_ok, up8[SUBLANES - 1:SUBLANES], 0.0)
    nrow = jnp.where(n_ok, un8[0:1], 0.0)
    rid = lax.broadcasted_iota(jnp.int32, u.shape, 0)
    prev = jnp.where(rid == 0, prow, pltpu.roll(u, 1, 0))
    nxt = jnp.where(rid == tm - 1, nrow, pltpu.roll(u, tm - 1, 0))
    return prev, nxt


def _halo_specs(tm, c, lt):
    r = tm // SUBLANES
    last = lt // SUBLANES - 1
    return [pl.BlockSpec((1, tm, c), lambda bi, i: (bi, i, 0)),
            pl.BlockSpec((1, SUBLANES, c), lambda bi, i: (bi, jnp.maximum(i * r - 1, 0), 0)),
            pl.BlockSpec((1, SUBLANES, c), lambda bi, i: (bi, jnp.minimum((i + 1) * r, last), 0))]


def _softplus(z):
    return jnp.maximum(z, 0.0) + jnp.log1p(jnp.exp(-jnp.abs(z)))


def _rwkv_prep_kernel(u_ref, up_ref, un_ref, mu_ref, wl_ref, bd_ref, w0_ref, a0_ref, kkw_ref, ka_ref, rk_ref,
                      r_o, v_o, kk_o, g_o, bon_o, lw_o, be_o, kd_o, *, lc, lt, tm):
    i = pl.program_id(1)
    u = u_ref[0]
    prev, nxt = _prev_next(u, up_ref[0], un_ref[0], i, tm, lc, lt)
    us = u + mu_ref[0:1] * (prev - u) + mu_ref[1:2] * (nxt - u)
    c = RWKV_WIDTH
    r = us[:, 0:c]
    k = us[:, c:2 * c]
    v = us[:, 2 * c:3 * c]
    slab = us[:, 3 * c:3 * c + LANES]
    lane = lax.broadcasted_iota(jnp.int32, slab.shape, 1)
    o_a = 2 * W_LORA
    o_g = o_a + 2 * A_LORA
    act = jnp.where(lane < o_a, jnp.tanh(slab),
                    jnp.where(lane < o_g, slab,
                              jnp.where(lane < o_g + G_LORA, jax.nn.sigmoid(slab), 0.0)))
    lo = jnp.dot(act, wl_ref[...], precision=HI, preferred_element_type=F32)
    bd = bd_ref[...]
    kk0 = k * kkw_ref[...]
    kk = kk0 * lax.rsqrt(jnp.dot(kk0 * kk0, bd, precision=HI, preferred_element_type=F32) + 1e-12)
    r_o[0] = r
    v_o[0] = v
    kk_o[0] = kk
    g_o[0] = lo[:, 4 * c:5 * c]
    bon = jnp.zeros_like(r)
    for d in range(2):
        w_raw = w0_ref[d:d + 1] + lo[:, d * c:(d + 1) * c]
        lw = -jnp.exp(-_softplus(-w_raw) - 0.5)
        a = jax.nn.sigmoid(a0_ref[d:d + 1] + lo[:, (2 + d) * c:(3 + d) * c])
        kd = k * (1.0 + (a - 1.0) * ka_ref[...])
        lw_o[0, d] = lw
        be_o[0, d] = a * kk
        kd_o[0, d] = kd
        bon = bon + r * kd * rk_ref[...]
    bon_o[0] = jnp.dot(bon, bd, precision=HI, preferred_element_type=F32) * v


def _rwkv_prep(ur, mu, wl, bd, w0, a0, kkw, ka, rk, lc):
    b, lt, cp = ur.shape
    tm = TOK_TILE
    c = RWKV_WIDTH
    full = lambda shape: pl.BlockSpec(shape, lambda bi, i: (0,) * len(shape))
    tok = pl.BlockSpec((1, tm, c), lambda bi, i: (bi, i, 0))
    tok2 = pl.BlockSpec((1, 2, tm, c), lambda bi, i: (bi, 0, i, 0))
    s1 = jax.ShapeDtypeStruct((b, lt, c), F32)
    s2 = jax.ShapeDtypeStruct((b, 2, lt, c), F32)
    return pl.pallas_call(
        functools.partial(_rwkv_prep_kernel, lc=lc, lt=lt, tm=tm),
        grid=(b, lt // tm),
        in_specs=_halo_specs(tm, cp, lt) + [full((2, cp)), full((LANES, 5 * c)), full((c, c)), full((2, c)),
                                            full((2, c)), full((1, c)), full((1, c)), full((1, c))],
        out_specs=[tok, tok, tok, tok, tok, tok2, tok2, tok2],
        out_shape=[s1, s1, s1, s1, s1, s2, s2, s2],
        compiler_params=_cparams(("parallel", "parallel")),
        name="rwkv_prep",
    )(ur, ur, ur, mu, wl, bd, w0, a0, kkw, ka, rk)


def _mm(a, b):
    return jnp.dot(a, b, precision=HI, preferred_element_type=F32)


def _unit_tri_inv(a_mat, row, col, eye):
    t = a_mat.shape[0]
    eye_f = eye.astype(F32)
    base = SUBLANES
    same = (row // base) == (col // base)
    n1 = -jnp.where(same, a_mat, 0.0)
    n2 = _mm(n1, n1)
    n4 = _mm(n2, n2)
    x = _mm(_mm(eye_f + n1, eye_f + n2), eye_f + n4)
    m = base
    while m < t:
        off = jnp.logical_and((row // (2 * m)) == (col // (2 * m)), (row // m) != (col // m))
        x = x - _mm(x, _mm(jnp.where(off, a_mat, 0.0), x))
        m *= 2
    return x


def _wkv_kernel(r_ref, v_ref, kk_ref, lw_ref, be_ref, kd_ref, y_ref, h_scr):
    t = WKV_CHUNK
    n = HEAD_DIM
    fwd = (pl.program_id(0) % 2) == 0

    @pl.when(pl.program_id(1) == 0)
    def _():
        h_scr[...] = jnp.zeros(h_scr.shape, F32)

    row = lax.broadcasted_iota(jnp.int32, (t, t), 0)
    col = lax.broadcasted_iota(jnp.int32, (t, t), 1)
    eye = row == col
    strict = (row - col) * jnp.where(fwd, 1, -1) > 0
    incl = jnp.logical_or(strict, eye)
    lw = lw_ref[0, 0]
    cum = _mm(incl.astype(F32), lw)
    ctot = jnp.sum(lw, 0, keepdims=True)
    e_pos = jnp.exp(cum)
    e_neg = jnp.exp(-cum)
    e_end = jnp.exp(ctot - cum)
    g_end = jnp.exp(ctot)
    r = r_ref[0]
    v = v_ref[0]
    be = be_ref[0, 0]
    kd = kd_ref[0, 0]
    kap_t = kk_ref[0] * jnp.exp(cum - lw)
    r_t = r * e_pos
    be_t = be * e_neg
    kd_t = kd * e_neg
    be_h = be * e_end
    kd_h = kd * e_end
    ys = []
    for h in range(RWKV_HEADS):
        sl = slice(h * n, (h + 1) * n)
        lm = jnp.concatenate([kap_t[:, sl], r_t[:, sl]], 0)
        rm = jnp.concatenate([be_t[:, sl], kd_t[:, sl]], 0)
        m = lax.dot_general(lm, rm, (((1,), (1,)), ((), ())), precision=HI, preferred_element_type=F32)
        a_mat = jnp.where(strict, m[:t, :t], 0.0)
        b_mat = jnp.where(strict, m[:t, t:], 0.0)
        a_r = jnp.where(incl, m[t:, :t], 0.0)
        b_r = jnp.where(incl, m[t:, t:], 0.0)
        tinv = _unit_tri_inv(a_mat, row, col, eye)
        vh = v[:, sl]
        w = _mm(tinv, jnp.concatenate([kap_t[:, sl], _mm(b_mat, vh)], 1))
        h0 = h_scr[h]
        u = -(_mm(w[:, :n], h0) + w[:, n:])
        stack = jnp.concatenate([h0, u, vh], 0)
        ys.append(_mm(jnp.concatenate([r_t[:, sl], a_r, b_r], 1), stack))
        gd = jnp.where(eye, jnp.broadcast_to(g_end[:, sl], (n, n)), 0.0)
        lh = jnp.concatenate([gd, be_h[:, sl], kd_h[:, sl]], 0)
        h_scr[h] = lax.dot_general(lh, stack, (((0,), (0,)), ((), ())), precision=HI,
                                   preferred_element_type=F32)
    y_ref[0, 0] = jnp.concatenate(ys, -1)


def _wkv_scan(r, v, kk, lw, be, kd, lc):
    b, lt, c = r.shape
    t = WKV_CHUNK
    nc = lt // t
    ncc = lc // t

    def chunk(bd, i):
        j_b = jnp.where(i < ncc, ncc - 1 - i, nc - 1 - (i - ncc))
        return jnp.where(bd % 2 == 0, i, j_b)

    one = pl.BlockSpec((1, t, c), lambda bd, i: (bd // 2, chunk(bd, i), 0))
    two = pl.BlockSpec((1, 1, t, c), lambda bd, i: (bd // 2, bd % 2, chunk(bd, i), 0))
    return pl.pallas_call(
        _wkv_kernel,
        grid=(2 * b, nc),
        in_specs=[one, one, one, two, two, two],
        out_specs=two,
        out_shape=jax.ShapeDtypeStruct((b, 2, lt, c), F32),
        scratch_shapes=[pltpu.VMEM((RWKV_HEADS, HEAD_DIM, HEAD_DIM), F32)],
        compiler_params=_cparams(("parallel", "arbitrary")),
        name="wkv_scan",
    )(r, v, kk, lw, be, kd)


def _rwkv_out_kernel(y_ref, bon_ref, g_ref, bd_ref, gg_ref, gb_ref, o_ref):
    y = y_ref[0, 0] + y_ref[0, 1] + bon_ref[0]
    bd = bd_ref[...]
    mu = _mm(y, bd) * (1.0 / HEAD_DIM)
    yc = y - mu
    var = _mm(yc * yc, bd) * (1.0 / HEAD_DIM)
    yn = yc * lax.rsqrt(var + RWKV_GN_EPS) * gg_ref[...] + gb_ref[...]
    o_ref[0] = (yn * g_ref[0]).astype(o_ref.dtype)


def _rwkv_out(y, bon, g, bd, gg, gb):
    b, _, lt, c = y.shape
    tm = TOK_TILE
    tok = pl.BlockSpec((1, tm, c), lambda bi, i: (bi, i, 0))
    full = lambda shape: pl.BlockSpec(shape, lambda bi, i: (0,) * len(shape))
    return pl.pallas_call(
        _rwkv_out_kernel,
        grid=(b, lt // tm),
        in_specs=[pl.BlockSpec((1, 2, tm, c), lambda bi, i: (bi, 0, i, 0)), tok, tok,
                  full((c, c)), full((1, c)), full((1, c))],
        out_specs=tok,
        out_shape=jax.ShapeDtypeStruct((b, lt, c), BF16),
        compiler_params=_cparams(("parallel", "parallel")),
        name="rwkv_out",
    )(y, bon, g, bd, gg, gb)


def _hy_prep_kernel(u_ref, up_ref, un_ref, w_ref, b_ref, x1_o, x2_o, v_o, *, lc, lt, tm):
    i = pl.program_id(1)
    u = u_ref[0]
    prev, nxt = _prev_next(u, up_ref[0], un_ref[0], i, tm, lc, lt)
    y = prev * w_ref[0:1] + u * w_ref[1:2] + nxt * w_ref[2:3] + b_ref[...]
    c = HY_WIDTH
    x1_o[0] = y[:, :c]
    x2_o[0] = y[:, c:2 * c]
    v_o[0] = y[:, 2 * c:]


def _hy_prep(uh, w, bias, lc):
    b, lt, cin = uh.shape
    tm = TOK_TILE
    c = HY_WIDTH
    full = lambda shape: pl.BlockSpec(shape, lambda bi, i: (0,) * len(shape))
    tok = pl.BlockSpec((1, tm, c), lambda bi, i: (bi, i, 0))
    s1 = jax.ShapeDtypeStruct((b, lt, c), F32)
    return pl.pallas_call(
        functools.partial(_hy_prep_kernel, lc=lc, lt=lt, tm=tm),
        grid=(b, lt // tm),
        in_specs=_halo_specs(tm, cin, lt) + [full((3, cin)), full((1, cin))],
        out_specs=[tok, tok, tok],
        out_shape=[s1, s1, s1],
        compiler_params=_cparams(("parallel", "parallel")),
        name="hy_prep",
    )(uh, uh, uh, w, bias)


def _hy_filter_kernel(z_ref, w1_ref, b1_ref, f1_ref, w2_ref, b2_ref, f2_ref, w3_ref, dec_ref,
                      fw_o, bw_o, nrm_o, *, tl):
    i = pl.program_id(0)
    z = z_ref[...]
    h = jnp.sin(f1_ref[...] * (_mm(z, w1_ref[...]) + b1_ref[...]))
    h = jnp.sin(f2_ref[...] * (_mm(h, w2_ref[...]) + b2_ref[...]))
    h = _mm(h, w3_ref[...]) * jnp.exp(-z[:, 0:1] * dec_ref[...])
    c = HY_WIDTH
    fw = jnp.concatenate([h[:, 0:c], h[:, 2 * c:3 * c]], 1)
    bw = jnp.concatenate([h[:, c:2 * c], h[:, 3 * c:4 * c]], 1)
    rid = i * tl + lax.broadcasted_iota(jnp.int32, (tl, 1), 0)
    bw = jnp.where(rid == 0, 0.0, bw)
    fw_o[...] = fw
    bw_o[...] = bw

    @pl.when(i == 0)
    def _():
        nrm_o[...] = jnp.zeros(nrm_o.shape, F32)

    nrm_o[...] += jnp.sum(jnp.abs(fw) + jnp.abs(bw), 0, keepdims=True)


def _hy_filter(feat, w1p, b1, f1, w2, b2, f2, w3, dec):
    l, fe = feat.shape
    tl = _pick_tile(l, (512, 256))
    c2 = HY_ORDER * HY_WIDTH
    full = lambda shape: pl.BlockSpec(shape, lambda i: (0,) * len(shape))
    return pl.pallas_call(
        functools.partial(_hy_filter_kernel, tl=tl),
        grid=(l // tl,),
        in_specs=[pl.BlockSpec((tl, fe), lambda i: (i, 0)), full(w1p.shape), full(b1.shape), full(f1.shape),
                  full(w2.shape), full(b2.shape), full(f2.shape), full(w3.shape), full(dec.shape)],
        out_specs=[pl.BlockSpec((tl, c2), lambda i: (i, 0)), pl.BlockSpec((tl, c2), lambda i: (i, 0)),
                   pl.BlockSpec((1, c2), lambda i: (0, 0))],
        out_shape=[jax.ShapeDtypeStruct((l, c2), F32), jax.ShapeDtypeStruct((l, c2), F32),
                   jax.ShapeDtypeStruct((1, c2), F32)],
        compiler_params=_cparams(("arbitrary",)),
        name="hy_filter",
    )(feat, w1p, b1, f1, w2, b2, f2, w3, dec)


def _dft_cols_kernel(f_ref, xa_ref, xb_ref, o_ref, *, n1, pair):
    f = f_ref[...]
    pa = _mm(f, xa_ref[0])
    pb = _mm(f, xb_ref[0])
    if pair:
        o_ref[0] = pa[:n1]
        o_ref[1] = pa[n1:]
        o_ref[2] = pb[:n1]
        o_ref[3] = pb[n1:]
    else:
        o_ref[0] = pa[:n1] - pb[n1:]
        o_ref[1] = pb[:n1] + pa[n1:]


def _dft_cols(fstack, xa, ia, xb, ib, pair):
    n1 = fstack.shape[0] // 2
    _, nh, cols = xa.shape
    tc = _pick_tile(cols, (4096, 2048, 1024, 512, 256, 128))
    no = 4 if pair else 2
    return pl.pallas_call(
        functools.partial(_dft_cols_kernel, n1=n1, pair=pair),
        grid=(cols // tc,),
        in_specs=[pl.BlockSpec(fstack.shape, lambda j: (0, 0)),
                  pl.BlockSpec((1, nh, tc), lambda j: (ia, 0, j)),
                  pl.BlockSpec((1, nh, tc), lambda j: (ib, 0, j))],
        out_specs=pl.BlockSpec((no, n1, tc), lambda j: (0, 0, j)),
        out_shape=jax.ShapeDtypeStruct((no, n1, cols), F32),
        compiler_params=_cparams(("parallel",)),
        name="dft_cols",
    )(fstack, xa, xb)


def _cplx_left(gs, zr, zi, n):
    c = zr.shape[1]
    p = _mm(gs, jnp.concatenate([zr, zi], 1))
    return p[:n, :c] - p[n:, c:], p[:n, c:] + p[n:, :c]


def _spec_kernel(a_ref, g_ref, nrm_ref, o_ref, *, n_total):
    n2 = DFT_N2
    gs = jnp.concatenate([g_ref[0, 0], g_ref[0, 1]], 0)
    fr, fi = _cplx_left(gs, a_ref[0, 0], a_ref[1, 0], n2)
    br, bi = _cplx_left(gs, a_ref[2, 0], a_ref[3, 0], n2)
    s = 1.0 / (nrm_ref[...] * n_total)
    o_ref[0, 0] = (fr + br) * s
    o_ref[1, 0] = (fi - bi) * s


def _spec(a4, g, nrm, n_total):
    _, n1, n2, c2 = a4.shape
    return pl.pallas_call(
        functools.partial(_spec_kernel, n_total=float(n_total)),
        grid=(n1,),
        in_specs=[pl.BlockSpec((4, 1, n2, c2), lambda k: (0, k, 0, 0)),
                  pl.BlockSpec((1, 2, n2, n2), lambda k: (k, 0, 0, 0)),
                  pl.BlockSpec((1, c2), lambda k: (0, 0))],
        out_specs=pl.BlockSpec((2, 1, n2, c2), lambda k: (0, k, 0, 0)),
        out_shape=jax.ShapeDtypeStruct((2, n1, n2, c2), F32),
        compiler_params=_cparams(("parallel",)),
        name="hy_spec",
    )(a4, g, nrm)


def _conv_mid_kernel(a_ref, g_ref, gh_ref, k_ref, o_ref):
    n2 = DFT_N2
    gs = jnp.concatenate([g_ref[0, 0], g_ref[0, 1]], 0)
    xr, xi = _cplx_left(gs, a_ref[0, 0], a_ref[1, 0], n2)
    kr = k_ref[0, 0]
    ki = k_ref[1, 0]
    zr = xr * kr - xi * ki
    zi = xr * ki + xi * kr
    ghs = jnp.concatenate([gh_ref[0, 0], gh_ref[0, 1]], 0)
    yr, yi = _cplx_left(ghs, zr, zi, n2)
    o_ref[0, 0] = yr
    o_ref[1, 0] = yi


def _conv_mid(a, g, gh, kspec, order):
    _, n1, n2, c = a.shape
    return pl.pallas_call(
        _conv_mid_kernel,
        grid=(n1,),
        in_specs=[pl.BlockSpec((2, 1, n2, c), lambda k: (0, k, 0, 0)),
                  pl.BlockSpec((1, 2, n2, n2), lambda k: (k, 0, 0, 0)),
                  pl.BlockSpec((1, 2, n2, n2), lambda k: (k, 0, 0, 0)),
                  pl.BlockSpec((2, 1, n2, c), lambda k: (0, k, 0, order))],
        out_specs=pl.BlockSpec((2, 1, n2, c), lambda k: (0, k, 0, 0)),
        out_shape=jax.ShapeDtypeStruct((2, n1, n2, c), F32),
        compiler_params=_cparams(("parallel",)),
        name="hy_conv_mid",
    )(a, g, gh, kspec)


def _idft_cols_kernel(c_ref, b_ref, g0_ref, g1_ref, x0_ref, x1_ref, bias_ref, o_ref, *, nh):
    cs = c_ref[...]
    pr = _mm(cs, b_ref[0])
    pi = _mm(cs, b_ref[1])
    yr = pr[:nh] - pi[nh:]
    yi = pi[:nh] + pr[nh:]
    bias = bias_ref[...]
    o_ref[0] = g0_ref[0] * (yr + x0_ref[0] * bias)
    o_ref[1] = g1_ref[0] * (yi + x1_ref[0] * bias)


def _idft_cols(cstack, bv, gate, xin, bias_cols):
    nh2, n1 = cstack.shape
    nh = nh2 // 2
    cols = bv.shape[-1]
    tc = _pick_tile(cols, (4096, 2048, 1024, 512, 256, 128))
    row = lambda bi: pl.BlockSpec((1, nh, tc), lambda j: (bi, 0, j))
    return pl.pallas_call(
        functools.partial(_idft_cols_kernel, nh=nh),
        grid=(cols // tc,),
        in_specs=[pl.BlockSpec((nh2, n1), lambda j: (0, 0)),
                  pl.BlockSpec((2, n1, tc), lambda j: (0, 0, j)),
                  row(0), row(1), row(0), row(1),
                  pl.BlockSpec((1, tc), lambda j: (0, j))],
        out_specs=pl.BlockSpec((2, nh, tc), lambda j: (0, 0, j)),
        out_shape=jax.ShapeDtypeStruct((2, nh, cols), F32),
        compiler_params=_cparams(("parallel",)),
        name="idft_cols",
    )(cstack, bv, gate, gate, xin, xin, bias_cols)


def _hy_ctx_kernel(x1_ref, x2_ref, v_ref, fw_ref, bw_ref, nrm_ref, bias_ref, ff_ref, ci_ref, o_ref, *, lc):
    n = 2 * lc
    c = HY_WIDTH
    ff = ff_ref[...]
    ci = ci_ref[...]
    pf = _mm(ff, fw_ref[...])
    pb = _mm(ff, bw_ref[...])
    s = 1.0 / (nrm_ref[...] * float(n))
    kr = (pf[:n] + pb[:n]) * s
    ki = (pf[n:] - pb[n:]) * s

    def conv(z0, z1, o):
        xr, xi = _cplx_left(ff, z0, z1, n)
        krr = kr[:, o * c:(o + 1) * c]
        kii = ki[:, o * c:(o + 1) * c]
        return _cplx_left(ci, xr * krr - xi * kii, xr * kii + xi * krr, lc)

    v0 = v_ref[0]
    v1 = v_ref[1]
    y0, y1 = conv(v0, v1, 0)
    z0 = x1_ref[0] * (y0 + v0 * bias_ref[0:1])
    z1 = x1_ref[1] * (y1 + v1 * bias_ref[0:1])
    y0, y1 = conv(z0, z1, 1)
    o_ref[0] = x2_ref[0] * (y0 + z0 * bias_ref[1:2])
    o_ref[1] = x2_ref[1] * (y1 + z1 * bias_ref[1:2])


def _hy_ctx(x1, x2, v, fw, bw, nrm, bias, ff, ci):
    b, lc, c = v.shape
    vm = pl.BlockSpec(memory_space=pltpu.VMEM)
    return pl.pallas_call(
        functools.partial(_hy_ctx_kernel, lc=lc),
        in_specs=[vm] * 9,
        out_specs=vm,
        out_shape=jax.ShapeDtypeStruct((b, lc, c), F32),
        compiler_params=pltpu.CompilerParams(vmem_limit_bytes=VMEM_LIMIT),
        name="hy_ctx",
    )(x1, x2, v, fw, bw, nrm, bias, ff, ci)


def _outproj_kernel(att_ref, rw_ref, hy_ref, x_ref, mod_ref, w_ref, lg_ref, lb_ref, o_ref, *, lc, tm):
    row0 = pl.program_id(1) * tm
    a0 = ATT_WIDTH
    a1 = ATT_WIDTH + RWKV_WIDTH
    o = jnp.dot(att_ref[0], w_ref[:a0], preferred_element_type=F32)
    o += jnp.dot(rw_ref[0], w_ref[a0:a1], preferred_element_type=F32)
    o += jnp.dot(hy_ref[0].astype(BF16), w_ref[a1:], preferred_element_type=F32)
    g = _sel_mod(mod_ref, 2, row0, tm, lc)
    y = ALPHA * x_ref[0] + g * o
    o_ref[0] = _layer_norm(y) * lg_ref[...] + lb_ref[...]


def _outproj(att, rw, hy, xx, mod, w, lg, lb, lc):
    b, lt, d = xx.shape
    tm = TOK_TILE
    tok = lambda c: pl.BlockSpec((1, tm, c), lambda bi, i: (bi, i, 0))
    full = lambda shape: pl.BlockSpec(shape, lambda bi, i: (0,) * len(shape))
    return pl.pallas_call(
        functools.partial(_outproj_kernel, lc=lc, tm=tm),
        grid=(b, lt // tm),
        in_specs=[tok(ATT_WIDTH), tok(RWKV_WIDTH), tok(HY_WIDTH), tok(d),
                  pl.BlockSpec((1, 2, 6, d), lambda bi, i: (bi, 0, 0, 0)),
                  full(w.shape), full((1, d)), full((1, d))],
        out_specs=tok(d),
        out_shape=jax.ShapeDtypeStruct((b, lt, d), F32),
        compiler_params=_cparams(("parallel", "parallel")),
        name="outproj",
    )(att, rw, hy, xx, mod, w, lg, lb)


def _ffn_kernel(x_ref, mod_ref, w1_ref, w3_ref, w2_ref, lg_ref, lb_ref, o_ref, h_scr, acc_scr, *, lc, tm, nf):
    row0 = pl.program_id(1) * tm
    f = pl.program_id(2)

    @pl.when(f == 0)
    def _():
        sh = _sel_mod(mod_ref, 3, row0, tm, lc)
        sc = _sel_mod(mod_ref, 4, row0, tm, lc)
        h_scr[...] = (_layer_norm(x_ref[0]) * (1.0 + sc) + sh).astype(BF16)
        acc_scr[...] = jnp.zeros(acc_scr.shape, F32)

    h = h_scr[...]
    a = jnp.dot(h, w1_ref[...], preferred_element_type=F32)
    g = jnp.dot(h, w3_ref[...], preferred_element_type=F32)
    acc_scr[...] += jnp.dot((_silu(a) * g).astype(BF16), w2_ref[...], preferred_element_type=F32)

    @pl.when(f == nf - 1)
    def _():
        gate = _sel_mod(mod_ref, 5, row0, tm, lc)
        y = ALPHA * x_ref[0] + gate * acc_scr[...]
        o_ref[0] = _layer_norm(y) * lg_ref[...] + lb_ref[...]


def _ffn(xx, mod, w1, w3, w2, lg, lb, lc):
    b, lt, d = xx.shape
    ff = w1.shape[1]
    tm = _pick_tile(lt, (1280, 768, 512, 256))
    tf = _pick_tile(ff, (256, 128))
    nf = ff // tf
    return pl.pallas_call(
        functools.partial(_ffn_kernel, lc=lc, tm=tm, nf=nf),
        grid=(b, lt // tm, nf),
        in_specs=[pl.BlockSpec((1, tm, d), lambda bi, i, f: (bi, i, 0)),
                  pl.BlockSpec((1, 2, 6, d), lambda bi, i, f: (bi, 0, 0, 0)),
                  pl.BlockSpec((d, tf), lambda bi, i, f: (0, f)),
                  pl.BlockSpec((d, tf), lambda bi, i, f: (0, f)),
                  pl.BlockSpec((tf, d), lambda bi, i, f: (f, 0)),
                  pl.BlockSpec((1, d), lambda bi, i, f: (0, 0)),
                  pl.BlockSpec((1, d), lambda bi, i, f: (0, 0))],
        out_specs=pl.BlockSpec((1, tm, d), lambda bi, i, f: (bi, i, 0)),
        out_shape=jax.ShapeDtypeStruct((b, lt, d), F32),
        scratch_shapes=[pltpu.VMEM((tm, d), BF16), pltpu.VMEM((tm, d), F32)],
        compiler_params=_cparams(("parallel", "parallel", "arbitrary")),
        name="ffn",
    )(xx, mod, w1, w3, w2, lg, lb)


def _moe_kernel(x_ref, mod_ref, wr_ref, w1_ref, w3_ref, w2_ref, lg_ref, lb_ref, o_ref,
                h_scr, acc_scr, gate_scr, *, lc, tm, nf, ns):
    row0 = pl.program_id(1) * tm
    s = pl.program_id(2)

    @pl.when(s == 0)
    def _():
        sh = _sel_mod(mod_ref, 3, row0, tm, lc)
        sc = _sel_mod(mod_ref, 4, row0, tm, lc)
        h = _layer_norm(x_ref[0]) * (1.0 + sc) + sh
        h_scr[...] = h.astype(BF16)
        acc_scr[...] = jnp.zeros(acc_scr.shape, F32)
        logits = _mm(h, wr_ref[...])
        lane = lax.broadcasted_iota(jnp.int32, logits.shape, 1)
        neg = jnp.float32(-jnp.inf)
        lg = jnp.where(lane < N_EXPERTS, logits, neg)
        m1 = jnp.max(lg, -1, keepdims=True)
        i1 = jnp.min(jnp.where(lg == m1, lane, LANES), -1, keepdims=True)
        lg2 = jnp.where(lane == i1, neg, lg)
        m2 = jnp.max(lg2, -1, keepdims=True)
        i2 = jnp.min(jnp.where(lg2 == m2, lane, LANES), -1, keepdims=True)
        e2 = jnp.exp(m2 - m1)
        g1 = 1.0 / (1.0 + e2)
        g2 = e2 / (1.0 + e2)
        for e in range(N_EXPERTS):
            gate_scr[e] = jnp.where(i1 == e, g1, 0.0) + jnp.where(i2 == e, g2, 0.0)

    h = h_scr[...]
    a = jnp.dot(h, w1_ref[0], preferred_element_type=F32)
    g = jnp.dot(h, w3_ref[0], preferred_element_type=F32)
    act = _silu(a) * g * gate_scr[s // nf]
    acc_scr[...] += jnp.dot(act.astype(BF16), w2_ref[0], preferred_element_type=F32)

    @pl.when(s == ns - 1)
    def _():
        gate = _sel_mod(mod_ref, 5, row0, tm, lc)
        y = ALPHA * x_ref[0] + gate * acc_scr[...]
        o_ref[0] = _layer_norm(y) * lg_ref[...] + lb_ref[...]


def _moe(xx, mod, wr, w1, w3, w2, lg, lb, lc):
    b, lt, d = xx.shape
    ne, _, ff = w1.shape
    tm = _pick_tile(lt, (1280, 768, 512, 256))
    tf = _pick_tile(ff, (256, 128))
    nf = ff // tf
    ns = ne * nf
    return pl.pallas_call(
        functools.partial(_moe_kernel, lc=lc, tm=tm, nf=nf, ns=ns),
        grid=(b, lt // tm, ns),
        in_specs=[pl.BlockSpec((1, tm, d), lambda bi, i, s: (bi, i, 0)),
                  pl.BlockSpec((1, 2, 6, d), lambda bi, i, s: (bi, 0, 0, 0)),
                  pl.BlockSpec((d, LANES), lambda bi, i, s: (0, 0)),
                  pl.BlockSpec((1, d, tf), lambda bi, i, s: (s // nf, 0, s % nf)),
                  pl.BlockSpec((1, d, tf), lambda bi, i, s: (s // nf, 0, s % nf)),
                  pl.BlockSpec((1, tf, d), lambda bi, i, s: (s // nf, s % nf, 0)),
                  pl.BlockSpec((1, d), lambda bi, i, s: (0, 0)),
                  pl.BlockSpec((1, d), lambda bi, i, s: (0, 0))],
        out_specs=pl.BlockSpec((1, tm, d), lambda bi, i, s: (bi, i, 0)),
        out_shape=jax.ShapeDtypeStruct((b, lt, d), F32),
        scratch_shapes=[pltpu.VMEM((tm, d), BF16), pltpu.VMEM((tm, d), F32),
                        pltpu.VMEM((ne, tm, 1), F32)],
        compiler_params=_cparams(("parallel", "parallel", "arbitrary")),
        name="moe",
    )(xx, mod, wr, w1, w3, w2, lg, lb)


def _rope_tables(l, lc):
    rows = l // GRID_W
    row = jnp.repeat(jnp.arange(rows, dtype=F32), GRID_W)
    col = jnp.tile(jnp.arange(GRID_W, dtype=F32), rows)
    n_freq = HEAD_DIM // 4
    inv_freq = ROPE_THETA ** (-jnp.arange(n_freq, dtype=F32) / n_freq)
    ang = jnp.concatenate([row[:, None] * inv_freq, col[:, None] * inv_freq], -1)
    cos, sin = jnp.cos(ang), jnp.sin(ang)
    cos64 = jnp.concatenate([jnp.ones((lc, HEAD_DIM), F32), jnp.concatenate([cos, cos], -1)], 0)
    sin64 = jnp.concatenate([jnp.zeros((lc, HEAD_DIM), F32), jnp.concatenate([-sin, sin], -1)], 0)
    return cos64, sin64


def _hy_features(l):
    bands = (HY_EMB - 1) // 2
    t = jnp.linspace(0.0, 1.0, l, dtype=F32)[:, None]
    f = jnp.linspace(1e-4, bands - 1, bands, dtype=F32)[None, :]
    wt = 2.0 * math.pi * jnp.arange(l, dtype=F32)[:, None] / l
    z = jnp.concatenate([t, jnp.cos(f * wt), -jnp.sin(f * wt)], -1)
    return jnp.pad(z, ((0, 0), (0, LANES - HY_EMB)))


def _angle(idx, n):
    return (2.0 * math.pi / n) * (idx % n).astype(F32)


def _dft_tables(n1):
    nh = n1 // 2
    n2 = DFT_N2
    n = n1 * n2
    k1 = jnp.arange(n1, dtype=jnp.int32)
    a1 = _angle(k1[:, None] * jnp.arange(nh, dtype=jnp.int32)[None, :], n1)
    fstack = jnp.concatenate([jnp.cos(a1), -jnp.sin(a1)], 0)
    cstack = jnp.concatenate([jnp.cos(a1.T), jnp.sin(a1.T)], 0)
    k2 = jnp.arange(n2, dtype=jnp.int32)
    ag = _angle(k2[None, None, :] * (k1[:, None, None] + n1 * k2[None, :, None]), n)
    g = jnp.stack([jnp.cos(ag), -jnp.sin(ag)], 1)
    agt = jnp.swapaxes(ag, 1, 2)
    gh = jnp.stack([jnp.cos(agt), jnp.sin(agt)], 1)
    return fstack, cstack, g, gh


def _dense_dft_tables(lc):
    n = 2 * lc
    a = _angle(jnp.arange(n, dtype=jnp.int32)[:, None] * jnp.arange(lc, dtype=jnp.int32)[None, :], n)
    ff = jnp.concatenate([jnp.cos(a), -jnp.sin(a)], 0)
    ci = jnp.concatenate([jnp.cos(a.T), jnp.sin(a.T)], 0)
    return ff, ci


def kernel(x, c, ctx, c_ctx, ada_w, ada_b, w_in, w_out, q_gain, k_gain, rwkv_mu, rwkv_w0, rwkv_wB, rwkv_a0, rwkv_aB, rwkv_gB, rwkv_kk, rwkv_ka, rwkv_rk, rwkv_gn_g, rwkv_gn_b, hy_short_w, hy_short_b, hy_w1, hy_b1, hy_freq1, hy_w2, hy_b2, hy_freq2, hy_w3, hy_decay, hy_bias, ln1_g, ln1_b, ln2_g, ln2_b, ffn_w1, ffn_w3, ffn_w2, moe_router, moe_w1, moe_w3, moe_w2):
    b, l, d = x.shape
    lc = ctx.shape[1]
    lt = lc + l
    depth = ada_w.shape[0]
    assert b == 2, "the long convolution packs the two batch rows as one complex signal"
    assert d == D_MODEL and lc % TOK_TILE == 0 and l % TOK_TILE == 0 and (2 * l) % (2 * DFT_N2) == 0
    cw = RWKV_WIDTH

    xx = jnp.concatenate([ctx, x], 1)
    cond8 = jnp.zeros((SUBLANES, d), F32).at[:b].set(c).at[b].set(c_ctx)
    mod_all = _ada_mod(cond8, ada_w, ada_b)

    cos64, sin64 = _rope_tables(l, lc)
    n1 = 2 * l // DFT_N2
    nh = n1 // 2
    cols = DFT_N2 * HY_WIDTH
    fstack, cstack, g_tab, gh_tab = _dft_tables(n1)
    ff_c, ci_c = _dense_dft_tables(lc)
    feat_l = _hy_features(l)
    feat_c = _hy_features(lc)
    blk = jnp.arange(cw) // HEAD_DIM
    bd = (blk[:, None] == blk[None, :]).astype(F32)
    perm64 = jnp.concatenate([jnp.arange(0, HEAD_DIM, 2), jnp.arange(1, HEAD_DIM, 2)])
    perm_att = jnp.concatenate([h * HEAD_DIM + perm64 for h in range(ATT_HEADS + ATT_KV_HEADS)]
                               + [jnp.arange(ATT_WIDTH + ATT_KV_WIDTH, IN_ATT)])
    tq = TOK_TILE
    tk = _pick_tile(lt, (1280, 1024, 768, 512, 256))

    for li in range(depth):
        ml = mod_all[li]
        mod = jnp.stack([jnp.broadcast_to(ml[b].reshape(1, 6, d), (b, 6, d)), ml[:b].reshape(b, 6, d)], 1)
        wi = w_in[li]
        w_pad = jnp.concatenate([wi[:, :IN_ATT][:, perm_att], wi[:, IN_ATT:IN_ATT + IN_RWKV],
                                 jnp.zeros((d, IN_RWKV_PAD - IN_RWKV), F32), wi[:, IN_ATT + IN_RWKV:]],
                                1).astype(BF16)
        ua, ur, uh = _inproj(xx, mod, w_pad, lc)

        q, kt, v = _attn_prep(ua, cos64, sin64, q_gain[li][perm64][None], k_gain[li][perm64][None])
        att_c = _flash(q, kt, v, tq, TOK_TILE, 0, lc // tq, lc // TOK_TILE)
        att_l = _flash(q, kt, v, tq, tk, lc // tq, l // tq, lt // tk)
        att = jnp.concatenate([att_c, att_l], 1)

        wl = jnp.zeros((LANES, 5 * cw), F32)
        wl = wl.at[0:W_LORA, 0:cw].set(rwkv_wB[li, 0]).at[W_LORA:2 * W_LORA, cw:2 * cw].set(rwkv_wB[li, 1])
        o_a = 2 * W_LORA
        wl = wl.at[o_a:o_a + A_LORA, 2 * cw:3 * cw].set(rwkv_aB[li, 0])
        wl = wl.at[o_a + A_LORA:o_a + 2 * A_LORA, 3 * cw:4 * cw].set(rwkv_aB[li, 1])
        o_g = o_a + 2 * A_LORA
        wl = wl.at[o_g:o_g + G_LORA, 4 * cw:5 * cw].set(rwkv_gB[li])
        mu = jnp.pad(rwkv_mu[li], ((0, 0), (0, IN_RWKV_PAD - IN_RWKV)))
        r_, v_, kk_, g_, bon_, lw_, be_, kd_ = _rwkv_prep(
            ur, mu, wl, bd, rwkv_w0[li], rwkv_a0[li], rwkv_kk[li][None], rwkv_ka[li][None],
            rwkv_rk[li].reshape(1, cw), lc)
        y = _wkv_scan(r_, v_, kk_, lw_, be_, kd_, lc)
        rw = _rwkv_out(y, bon_, g_, bd, rwkv_gn_g[li][None], rwkv_gn_b[li][None])

        x1, x2, vv = _hy_prep(uh, hy_short_w[li], hy_short_b[li][None], lc)
        w1p = jnp.pad(hy_w1[li], ((0, LANES - HY_EMB), (0, 0)))
        fargs = (w1p, hy_b1[li][None], hy_freq1[li][None], hy_w2[li], hy_b2[li][None], hy_freq2[li][None],
                 hy_w3[li], hy_decay[li][None])
        fw, bw, nrm = _hy_filter(feat_l, *fargs)
        c2 = HY_ORDER * HY_WIDTH
        a4 = _dft_cols(fstack, fw.reshape(1, nh, DFT_N2 * c2), 0, bw.reshape(1, nh, DFT_N2 * c2), 0, True)
        kspec = _spec(a4.reshape(4, n1, DFT_N2, c2), g_tab, nrm, n1 * DFT_N2)
        lat = lambda t: t[:, lc:].reshape(b, nh, cols)
        x1l, x2l, zin = lat(x1), lat(x2), lat(vv)
        for o, gate in enumerate((x1l, x2l)):
            a = _dft_cols(fstack, zin, 0, zin, 1, False)
            bv = _conv_mid(a.reshape(2, n1, DFT_N2, HY_WIDTH), g_tab, gh_tab, kspec, o)
            bias_cols = jnp.tile(hy_bias[li, o], DFT_N2)[None]
            zin = _idft_cols(cstack, bv.reshape(2, n1, cols), gate, zin, bias_cols)
        hy_l = zin.reshape(b, l, HY_WIDTH)
        fw_c, bw_c, nrm_c = _hy_filter(feat_c, *fargs)
        hy_c = _hy_ctx(x1[:, :lc], x2[:, :lc], vv[:, :lc], fw_c, bw_c, nrm_c, hy_bias[li], ff_c, ci_c)
        hy = jnp.concatenate([hy_c, hy_l], 1)

        xx = _outproj(att, rw, hy, xx, mod, w_out[li].astype(BF16), ln1_g[li][None], ln1_b[li][None], lc)

        j = li // 2
        if li % 2 == 0:
            xx = _ffn(xx, mod, ffn_w1[j].astype(BF16), ffn_w3[j].astype(BF16), ffn_w2[j].astype(BF16),
                      ln2_g[li][None], ln2_b[li][None], lc)
        else:
            wr = jnp.pad(moe_router[j], ((0, 0), (0, LANES - N_EXPERTS)))
            xx = _moe(xx, mod, wr, moe_w1[j].astype(BF16), moe_w3[j].astype(BF16), moe_w2[j].astype(BF16),
                      ln2_g[li][None], ln2_b[li][None], lc)
    return xx[:, lc:]
```

```python
import functools
import math

import jax
import jax.numpy as jnp
from jax import lax
from jax.experimental import pallas as pl
from jax.experimental.pallas import tpu as pltpu

F32 = jnp.float32
BF16 = jnp.bfloat16
HI = lax.Precision.HIGHEST

D_MODEL = 1024
DEPTH = 2
GRID_W = 64
HEAD_DIM = 64
HALF_HD = HEAD_DIM // 2
ATT_WIDTH = 512
RWKV_WIDTH = 256
HY_WIDTH = 256
ATT_HEADS = 8
ATT_KV_HEADS = 2
ATT_REP = 4
ATT_KV_WIDTH = 128
ROPE_THETA = 10000.0
QK_EPS = 1e-6
RWKV_HEADS = 4
W_LORA = 16
A_LORA = 16
G_LORA = 32
RWKV_GN_EPS = 64e-5
HY_ORDER = 2
HY_EMB = 33
HY_FFN = 64
N_EXPERTS = 8
LN_EPS = 1e-6
IN_ATT = ATT_WIDTH + 2 * ATT_KV_WIDTH
IN_RWKV = 3 * RWKV_WIDTH + 2 * W_LORA + 2 * A_LORA + G_LORA
IN_RWKV_PAD = 896
IN_HY = 3 * HY_WIDTH
ALPHA = float((2 * DEPTH) ** 0.25)
LOG2E = 1.4426950408889634

LANES = 128
SUBLANES = 8
TOK_TILE = 256
WKV_CHUNK = 64
WKV_TILE = 256
DFT_N2 = 256
VMEM_LIMIT = 48 * 1024 * 1024


def _cparams(sem):
    return pltpu.CompilerParams(dimension_semantics=sem, vmem_limit_bytes=VMEM_LIMIT)


def _pick_tile(n, cands):
    for c in cands:
        if n % c == 0:
            return c
    raise ValueError(f"no tile for {n} in {cands}")


def _layer_norm(x):
    mu = jnp.mean(x, -1, keepdims=True)
    xc = x - mu
    var = jnp.mean(xc * xc, -1, keepdims=True)
    return xc * lax.rsqrt(var + LN_EPS)


def _sel_mod(mod_ref, j, row0, tm, lc):
    rid = row0 + lax.broadcasted_iota(jnp.int32, (tm, 1), 0)
    return jnp.where(rid < lc, mod_ref[0, 0, j:j + 1, :], mod_ref[0, 1, j:j + 1, :])


def _silu(x):
    return x * jax.nn.sigmoid(x)


def _ada_kernel(c_ref, w_ref, b_ref, o_ref):
    s = _silu(c_ref[...])
    o_ref[0] = jnp.dot(s, w_ref[0], precision=HI, preferred_element_type=F32) + b_ref[0]


def _ada_mod(cond8, ada_w, ada_b):
    depth, d, n = ada_w.shape
    tn = _pick_tile(n, (1536, 1024, 512, 256, 128))
    return pl.pallas_call(
        _ada_kernel,
        grid=(depth, n // tn),
        in_specs=[pl.BlockSpec((SUBLANES, d), lambda l, j: (0, 0)),
                  pl.BlockSpec((1, d, tn), lambda l, j: (l, 0, j)),
                  pl.BlockSpec((1, 1, tn), lambda l, j: (l, 0, j))],
        out_specs=pl.BlockSpec((1, SUBLANES, tn), lambda l, j: (l, 0, j)),
        out_shape=jax.ShapeDtypeStruct((depth, SUBLANES, n), F32),
        compiler_params=_cparams(("parallel", "parallel")),
        name="ada_mod",
    )(cond8, ada_w, ada_b.reshape(depth, 1, n))


def _inproj_kernel(x_ref, mod_ref, w_ref, oa_ref, or_ref, oh_ref, *, lc, tm):
    row0 = pl.program_id(1) * tm
    sh = _sel_mod(mod_ref, 0, row0, tm, lc)
    sc = _sel_mod(mod_ref, 1, row0, tm, lc)
    h = (_layer_norm(x_ref[0]) * (1.0 + sc) + sh).astype(BF16)
    u = jnp.dot(h, w_ref[...], preferred_element_type=F32)
    oa_ref[0] = u[:, :IN_ATT]
    or_ref[0] = u[:, IN_ATT:IN_ATT + IN_RWKV_PAD]
    oh_ref[0] = u[:, IN_ATT + IN_RWKV_PAD:]


def _inproj(xx, mod, w_pad, lc):
    b, lt, d = xx.shape
    tm = TOK_TILE
    n = w_pad.shape[1]
    return pl.pallas_call(
        functools.partial(_inproj_kernel, lc=lc, tm=tm),
        grid=(b, lt // tm),
        in_specs=[pl.BlockSpec((1, tm, d), lambda bi, i: (bi, i, 0)),
                  pl.BlockSpec((1, 2, 6, d), lambda bi, i: (bi, 0, 0, 0)),
                  pl.BlockSpec((d, n), lambda bi, i: (0, 0))],
        out_specs=[pl.BlockSpec((1, tm, IN_ATT), lambda bi, i: (bi, i, 0)),
                   pl.BlockSpec((1, tm, IN_RWKV_PAD), lambda bi, i: (bi, i, 0)),
                   pl.BlockSpec((1, tm, IN_HY), lambda bi, i: (bi, i, 0))],
        out_shape=[jax.ShapeDtypeStruct((b, lt, IN_ATT), F32),
                   jax.ShapeDtypeStruct((b, lt, IN_RWKV_PAD), F32),
                   jax.ShapeDtypeStruct((b, lt, IN_HY), F32)],
        compiler_params=_cparams(("parallel", "parallel")),
        name="inproj",
    )(xx, mod, w_pad)


def _attn_prep_kernel(u_ref, cos_ref, sin_ref, qg_ref, kg_ref, qt_ref, k_ref, vt_ref):
    u = u_ref[0]
    cos = cos_ref[...]
    sin = sin_ref[...]

    def norm_rope(xh, g):
        ms = jnp.mean(xh * xh, -1, keepdims=True)
        xn = xh * lax.rsqrt(ms + QK_EPS) * g
        sw = jnp.concatenate([xn[:, HALF_HD:], xn[:, :HALF_HD]], -1)
        return xn * cos + sw * sin

    qg = qg_ref[...]
    kg = kg_ref[...]
    tm = u.shape[0]
    qs = [norm_rope(u[:, h * HEAD_DIM:(h + 1) * HEAD_DIM], qg) * (LOG2E * HEAD_DIM ** -0.5)
          for h in range(ATT_HEADS)]
    qt = jnp.concatenate(qs, -1).T
    for g in range(ATT_KV_HEADS):
        base = g * ATT_REP * HEAD_DIM
        qt_ref[0, g, 0] = jnp.concatenate(
            [qt[base + r * HEAD_DIM:base + (r + 1) * HEAD_DIM] for r in range(ATT_REP)], -1).astype(BF16)
        k0 = ATT_WIDTH + g * HEAD_DIM
        k_ref[0, g] = norm_rope(u[:, k0:k0 + HEAD_DIM], kg).astype(BF16)
    v0 = ATT_WIDTH + ATT_KV_WIDTH
    vt = u[:, v0:v0 + ATT_KV_WIDTH].T
    for g in range(ATT_KV_HEADS):
        vt_ref[0, g] = vt[g * HEAD_DIM:(g + 1) * HEAD_DIM].astype(BF16)


def _attn_prep(ua, cos, sin, qg, kg):
    b, lt, _ = ua.shape
    tm = TOK_TILE
    return pl.pallas_call(
        _attn_prep_kernel,
        grid=(b, lt // tm),
        in_specs=[pl.BlockSpec((1, tm, IN_ATT), lambda bi, i: (bi, i, 0)),
                  pl.BlockSpec((tm, HEAD_DIM), lambda bi, i: (i, 0)),
                  pl.BlockSpec((tm, HEAD_DIM), lambda bi, i: (i, 0)),
                  pl.BlockSpec((1, HEAD_DIM), lambda bi, i: (0, 0)),
                  pl.BlockSpec((1, HEAD_DIM), lambda bi, i: (0, 0))],
        out_specs=[pl.BlockSpec((1, ATT_KV_HEADS, 1, HEAD_DIM, ATT_REP * tm), lambda bi, i: (bi, 0, i, 0, 0)),
                   pl.BlockSpec((1, ATT_KV_HEADS, tm, HEAD_DIM), lambda bi, i: (bi, 0, i, 0)),
                   pl.BlockSpec((1, ATT_KV_HEADS, HEAD_DIM, tm), lambda bi, i: (bi, 0, 0, i))],
        out_shape=[jax.ShapeDtypeStruct((b, ATT_KV_HEADS, lt // tm, HEAD_DIM, ATT_REP * tm), BF16),
                   jax.ShapeDtypeStruct((b, ATT_KV_HEADS, lt, HEAD_DIM), BF16),
                   jax.ShapeDtypeStruct((b, ATT_KV_HEADS, HEAD_DIM, lt), BF16)],
        compiler_params=_cparams(("parallel", "parallel")),
        name="attn_prep",
    )(ua, cos, sin, qg, kg)


def _flash_kernel(qt_ref, k_ref, vt_ref, o_ref, m_scr, l_scr, acc_scr, *, nk, tq, sub):
    j = pl.program_id(3)

    @pl.when(j == 0)
    def _():
        m_scr[...] = jnp.full(m_scr.shape, -jnp.inf, F32)
        l_scr[...] = jnp.zeros(l_scr.shape, F32)
        acc_scr[...] = jnp.zeros(acc_scr.shape, F32)

    qt = qt_ref[0, 0, 0]
    nsub = k_ref.shape[2] // sub
    m = m_scr[...]
    l = l_scr[...]
    acc = acc_scr[...]
    scores = lambda c: jnp.dot(k_ref[0, 0, c * sub:(c + 1) * sub, :], qt, preferred_element_type=F32)
    s_next = scores(0)
    for c in range(nsub):
        s = s_next
        if c + 1 < nsub:
            s_next = scores(c + 1)
        m_new = jnp.maximum(m, jnp.max(s, 0, keepdims=True))
        a = jnp.exp2(m - m_new)
        p = jnp.exp2(s - m_new)
        l = a * l + jnp.sum(p, 0, keepdims=True)
        acc = a * acc + jnp.dot(vt_ref[0, 0, :, c * sub:(c + 1) * sub], p.astype(BF16),
                                preferred_element_type=F32)
        m = m_new
    m_scr[...] = m
    l_scr[...] = l
    acc_scr[...] = acc

    @pl.when(j == nk - 1)
    def _():
        o = (acc / l).T
        o_ref[0] = jnp.concatenate([o[r * tq:(r + 1) * tq] for r in range(ATT_REP)], -1).astype(o_ref.dtype)


def _flash(qt, k, vt, tq, tk, q_tile0, nq, nk):
    b = qt.shape[0]
    lq = nq * tq
    sub = _pick_tile(tk, (256, 128))
    return pl.pallas_call(
        functools.partial(_flash_kernel, nk=nk, tq=tq, sub=sub),
        grid=(b, ATT_KV_HEADS, nq, nk),
        in_specs=[pl.BlockSpec((1, 1, 1, HEAD_DIM, ATT_REP * tq), lambda bi, g, i, j: (bi, g, i + q_tile0, 0, 0)),
                  pl.BlockSpec((1, 1, tk, HEAD_DIM), lambda bi, g, i, j: (bi, g, j, 0)),
                  pl.BlockSpec((1, 1, HEAD_DIM, tk), lambda bi, g, i, j: (bi, g, 0, j))],
        out_specs=pl.BlockSpec((1, tq, ATT_REP * HEAD_DIM), lambda bi, g, i, j: (bi, i, g)),
        out_shape=jax.ShapeDtypeStruct((b, lq, ATT_WIDTH), BF16),
        scratch_shapes=[pltpu.VMEM((1, ATT_REP * tq), F32),
                        pltpu.VMEM((1, ATT_REP * tq), F32),
                        pltpu.VMEM((HEAD_DIM, ATT_REP * tq), F32)],
        compiler_params=_cparams(("parallel", "parallel", "parallel", "arbitrary")),
        name="flash",
    )(qt, k, vt)


def _prev_next(u, up8, un8, i, tm, lc, lt):
    start = i * tm
    p_ok = jnp.logical_and(start != 0, start != lc)
    n_ok = jnp.logical_and(start + tm != lc, start + tm != lt)
    prow = jnp.where(p_ok, up8[SUBLANES - 1:SUBLANES], 0.0)
    nrow = jnp.where(n_ok, un8[0:1], 0.0)
    rid = lax.broadcasted_iota(jnp.int32, u.shape, 0)
    prev = jnp.where(rid == 0, prow, pltpu.roll(u, 1, 0))
    nxt = jnp.where(rid == tm - 1, nrow, pltpu.roll(u, tm - 1, 0))
    return prev, nxt


def _halo_specs(tm, c, lt):
    r = tm // SUBLANES
    last = lt // SUBLANES - 1
    return [pl.BlockSpec((1, tm, c), lambda bi, i: (bi, i, 0)),
            pl.BlockSpec((1, SUBLANES, c), lambda bi, i: (bi, jnp.maximum(i * r - 1, 0), 0)),
            pl.BlockSpec((1, SUBLANES, c), lambda bi, i: (bi, jnp.minimum((i + 1) * r, last), 0))]


def _softplus(z):
    return jnp.maximum(z, 0.0) + jnp.log1p(jnp.exp(-jnp.abs(z)))


def _rwkv_prep_kernel(u_ref, up_ref, un_ref, mu_ref, wl_ref, bd_ref, w0_ref, a0_ref, kkw_ref, ka_ref, rk_ref,
                      r_o, v_o, kk_o, g_o, bon_o, lw_o, be_o, kd_o, *, lc, lt, tm):
    i = pl.program_id(1)
    u = u_ref[0]
    prev, nxt = _prev_next(u, up_ref[0], un_ref[0], i, tm, lc, lt)
    us = u + mu_ref[0:1] * (prev - u) + mu_ref[1:2] * (nxt - u)
    c = RWKV_WIDTH
    r = us[:, 0:c]
    k = us[:, c:2 * c]
    v = us[:, 2 * c:3 * c]
    slab = us[:, 3 * c:3 * c + LANES]
    lane = lax.broadcasted_iota(jnp.int32, slab.shape, 1)
    o_a = 2 * W_LORA
    o_g = o_a + 2 * A_LORA
    act = jnp.where(lane < o_a, jnp.tanh(slab),
                    jnp.where(lane < o_g, slab,
                              jnp.where(lane < o_g + G_LORA, jax.nn.sigmoid(slab), 0.0)))
    lo = jnp.dot(act, wl_ref[...], precision=HI, preferred_element_type=F32)
    bd = bd_ref[...]
    kk0 = k * kkw_ref[...]
    kk = kk0 * lax.rsqrt(jnp.dot(kk0 * kk0, bd, precision=HI, preferred_element_type=F32) + 1e-12)
    r_o[0] = r
    v_o[0] = v
    kk_o[0] = kk
    g_o[0] = lo[:, 4 * c:5 * c]
    bon = jnp.zeros_like(r)
    for d in range(2):
        w_raw = w0_ref[d:d + 1] + lo[:, d * c:(d + 1) * c]
        lw = -jnp.exp(-_softplus(-w_raw) - 0.5)
        a = jax.nn.sigmoid(a0_ref[d:d + 1] + lo[:, (2 + d) * c:(3 + d) * c])
        kd = k * (1.0 + (a - 1.0) * ka_ref[...])
        lw_o[0, d] = lw
        be_o[0, d] = a * kk
        kd_o[0, d] = kd
        bon = bon + r * kd * rk_ref[...]
    bon_o[0] = jnp.dot(bon, bd, precision=HI, preferred_element_type=F32) * v


def _rwkv_prep(ur, mu, wl, bd, w0, a0, kkw, ka, rk, lc):
    b, lt, cp = ur.shape
    tm = TOK_TILE
    c = RWKV_WIDTH
    full = lambda shape: pl.BlockSpec(shape, lambda bi, i: (0,) * len(shape))
    tok = pl.BlockSpec((1, tm, c), lambda bi, i: (bi, i, 0))
    tok2 = pl.BlockSpec((1, 2, tm, c), lambda bi, i: (bi, 0, i, 0))
    s1 = jax.ShapeDtypeStruct((b, lt, c), F32)
    s2 = jax.ShapeDtypeStruct((b, 2, lt, c), F32)
    return pl.pallas_call(
        functools.partial(_rwkv_prep_kernel, lc=lc, lt=lt, tm=tm),
        grid=(b, lt // tm),
        in_specs=_halo_specs(tm, cp, lt) + [full((2, cp)), full((LANES, 5 * c)), full((c, c)), full((2, c)),
                                            full((2, c)), full((1, c)), full((1, c)), full((1, c))],
        out_specs=[tok, tok, tok, tok, tok, tok2, tok2, tok2],
        out_shape=[s1, s1, s1, s1, s1, s2, s2, s2],
        compiler_params=_cparams(("parallel", "parallel")),
        name="rwkv_prep",
    )(ur, ur, ur, mu, wl, bd, w0, a0, kkw, ka, rk)


def _mm(a, b):
    return jnp.dot(a, b, precision=HI, preferred_element_type=F32)


_NN = ((1,), (0,))
_NT = ((1,), (1,))
_TN = ((0,), (0,))


def _split2(a):
    hi = a.astype(BF16)
    return hi, (a - hi.astype(F32)).astype(BF16)


def _dotp(a, b, passes, dims=_NN):
    if a.ndim == 3:
        dn = (((dims[0][0] + 1,), (dims[1][0] + 1,)), ((0,), (0,)))
    else:
        dn = (dims, ((), ()))
    dg = lambda p, q: lax.dot_general(p, q, dn, preferred_element_type=F32)
    if passes == 1:
        return dg(a.astype(BF16), b.astype(BF16))
    ah, al = _split2(a)
    bh, bl = _split2(b)
    return dg(ah, bh) + dg(ah, bl) + dg(al, bh)


P_M = 1
P_INV = 1
P_W = 1
P_Z = 1
P_STATE = 3


def _unit_tri_inv(a_mat, row, col, eye, passes):
    t = a_mat.shape[-1]
    eye_f = eye.astype(F32)
    base = SUBLANES
    same = (row // base) == (col // base)
    n1 = -jnp.where(same, a_mat, 0.0)
    n2 = _dotp(n1, n1, passes)
    n4 = _dotp(n2, n2, passes)
    x = _dotp(_dotp(eye_f + n1, eye_f + n2, passes), eye_f + n4, passes)
    m = base
    while m < t:
        off = jnp.logical_and((row // (2 * m)) == (col // (2 * m)), (row // m) != (col // m))
        x = x - _dotp(x, _dotp(jnp.where(off, a_mat, 0.0), x, passes), passes)
        m *= 2
    return x


def _wkv_kernel(r_ref, v_ref, kk_ref, lw_ref, be_ref, kd_ref, y_ref, h_scr, pq_scr, ry_scr, *, fwd):
    t = WKV_CHUNK
    n = HEAD_DIM
    g = WKV_TILE // WKV_CHUNK
    tt = WKV_TILE
    order = list(range(g)) if fwd else list(range(g - 1, -1, -1))

    @pl.when(pl.program_id(1) == 0)
    def _():
        h_scr[...] = jnp.zeros(h_scr.shape, F32)
        pq_scr[...] = jnp.zeros(pq_scr.shape, F32)
        ry_scr[...] = jnp.zeros(ry_scr.shape, F32)

    nh = RWKV_HEADS
    hm = h_scr[...]
    for p in range(g):
        ry = ry_scr[p * nh:(p + 1) * nh]
        pq = pq_scr[p * nh:(p + 1) * nh]
        y = _dotp(ry[:, :, :n], hm, P_STATE) + ry[:, :, n:]
        hm = _dotp(pq[:, :, :n], hm, P_STATE) + pq[:, :, n:]
        c = order[p]
        y_ref[0, c * t:(c + 1) * t, :] = jnp.concatenate([y[h] for h in range(nh)], -1)
    h_scr[...] = hm

    row = lax.broadcasted_iota(jnp.int32, (tt, tt), 0)
    col = lax.broadcasted_iota(jnp.int32, (tt, tt), 1)
    same = (row // t) == (col // t)
    tri = jnp.logical_and(same, (row >= col) if fwd else (row <= col))
    sums = jnp.concatenate([jnp.where(tri, 1.0, 0.0), jnp.where(same, 1.0, 0.0)], 0).astype(BF16)
    lw = lw_ref[0, 0]
    l1 = lw.astype(BF16)
    rem = lw - l1.astype(F32)
    l2 = rem.astype(BF16)
    l3 = (rem - l2.astype(F32)).astype(BF16)
    cc = (jnp.dot(sums, l1, preferred_element_type=F32) + jnp.dot(sums, l2, preferred_element_type=F32)
          + jnp.dot(sums, l3, preferred_element_type=F32))
    cum = cc[:tt]
    ctot = cc[tt:]
    e_neg = jnp.exp(-cum)
    e_end = jnp.exp(ctot - cum)
    g_end = jnp.exp(ctot)
    v = v_ref[0]
    be = be_ref[0, 0]
    kd = kd_ref[0, 0]
    kap_t = kk_ref[0] * jnp.exp(cum - lw)
    r_t = r_ref[0] * jnp.exp(cum)
    be_t = be * e_neg
    kd_t = kd * e_neg
    be_h = be * e_end
    kd_h = kd * e_end
    r64 = lax.broadcasted_iota(jnp.int32, (t, t), 0)
    c64 = lax.broadcasted_iota(jnp.int32, (t, t), 1)
    eye = r64 == c64
    strict = (r64 > c64) if fwd else (r64 < c64)
    incl = jnp.logical_or(strict, eye)

    def units(x, rows=t):
        return jnp.stack([x[order[p] * t:order[p] * t + rows, h * n:(h + 1) * n]
                          for p in range(g) for h in range(nh)], 0)

    kap = units(kap_t)
    rt = units(r_t)
    vh = units(v)
    nu = g * nh
    m = _dotp(jnp.concatenate([kap, rt], 1), jnp.concatenate([units(be_t), units(kd_t)], 1), P_M, _NT)
    a_mat = jnp.where(strict, m[:, :t, :t], 0.0)
    b_mat = jnp.where(strict, m[:, :t, t:], 0.0)
    ab_r = jnp.concatenate([jnp.where(incl, m[:, t:, :t], 0.0), jnp.where(incl, m[:, t:, t:], 0.0)], 2)
    tinv = _unit_tri_inv(a_mat, r64, c64, eye, P_INV)
    w = _dotp(tinv, jnp.concatenate([kap, _dotp(b_mat, vh, P_W)], 2), P_W)
    z = jnp.concatenate([-w, jnp.concatenate([jnp.zeros((nu, t, n), F32), vh], 2)], 1)
    ry = _dotp(ab_r, z, P_Z)
    ry_scr[:, :, :n] = ry[:, :, :n] + rt
    ry_scr[:, :, n:] = ry[:, :, n:]
    pq = _dotp(jnp.concatenate([units(be_h), units(kd_h)], 1), z, P_Z, _TN)
    gd = jnp.where(eye, jnp.broadcast_to(units(g_end, 1), (nu, n, n)), 0.0)
    pq_scr[:, :, :n] = pq[:, :, :n] + gd
    pq_scr[:, :, n:] = pq[:, :, n:]


def _wkv_scan(r, v, kk, lw, be, kd, lc, fwd):
    b, lt, c = r.shape
    tt = WKV_TILE
    nt = lt // tt
    ntc = lc // tt
    d = 0 if fwd else 1

    def tile(i):
        if fwd:
            return i
        return jnp.where(i < ntc, ntc - 1 - i, nt - 1 - (i - ntc))

    t_in = lambda i: tile(jnp.minimum(i, nt - 1))
    t_out = lambda i: tile(jnp.maximum(i - 1, 0))
    one = pl.BlockSpec((1, tt, c), lambda bi, i: (bi, t_in(i), 0))
    two = pl.BlockSpec((1, 1, tt, c), lambda bi, i: (bi, d, t_in(i), 0))
    g = tt // WKV_CHUNK
    return pl.pallas_call(
        functools.partial(_wkv_kernel, fwd=fwd),
        grid=(b, nt + 1),
        in_specs=[one, one, one, two, two, two],
        out_specs=pl.BlockSpec((1, tt, c), lambda bi, i: (bi, t_out(i), 0)),
        out_shape=jax.ShapeDtypeStruct((b, lt, c), F32),
        scratch_shapes=[pltpu.VMEM((RWKV_HEADS, HEAD_DIM, HEAD_DIM), F32),
                        pltpu.VMEM((g * RWKV_HEADS, HEAD_DIM, 2 * HEAD_DIM), F32),
                        pltpu.VMEM((g * RWKV_HEADS, WKV_CHUNK, 2 * HEAD_DIM), F32)],
        compiler_params=_cparams(("parallel", "arbitrary")),
        name="wkv_fwd" if fwd else "wkv_bwd",
    )(r, v, kk, lw, be, kd)


def _rwkv_out_kernel(yf_ref, yb_ref, bon_ref, g_ref, bd_ref, gg_ref, gb_ref, o_ref):
    y = yf_ref[0] + yb_ref[0] + bon_ref[0]
    bd = bd_ref[...]
    mu = _mm(y, bd) * (1.0 / HEAD_DIM)
    yc = y - mu
    var = _mm(yc * yc, bd) * (1.0 / HEAD_DIM)
    yn = yc * lax.rsqrt(var + RWKV_GN_EPS) * gg_ref[...] + gb_ref[...]
    o_ref[0] = (yn * g_ref[0]).astype(o_ref.dtype)


def _rwkv_out(yf, yb, bon, g, bd, gg, gb):
    b, lt, c = yf.shape
    tm = TOK_TILE
    tok = pl.BlockSpec((1, tm, c), lambda bi, i: (bi, i, 0))
    full = lambda shape: pl.BlockSpec(shape, lambda bi, i: (0,) * len(shape))
    return pl.pallas_call(
        _rwkv_out_kernel,
        grid=(b, lt // tm),
        in_specs=[tok, tok, tok, tok, full((c, c)), full((1, c)), full((1, c))],
        out_specs=tok,
        out_shape=jax.ShapeDtypeStruct((b, lt, c), BF16),
        compiler_params=_cparams(("parallel", "parallel")),
        name="rwkv_out",
    )(yf, yb, bon, g, bd, gg, gb)


def _hy_prep_kernel(u_ref, up_ref, un_ref, w_ref, b_ref, x1_o, x2_o, v_o, *, lc, lt, tm):
    i = pl.program_id(1)
    u = u_ref[0]
    prev, nxt = _prev_next(u, up_ref[0], un_ref[0], i, tm, lc, lt)
    y = prev * w_ref[0:1] + u * w_ref[1:2] + nxt * w_ref[2:3] + b_ref[...]
    c = HY_WIDTH
    x1_o[0] = y[:, :c]
    x2_o[0] = y[:, c:2 * c]
    v_o[0] = y[:, 2 * c:]


def _hy_prep(uh, w, bias, lc):
    b, lt, cin = uh.shape
    tm = TOK_TILE
    c = HY_WIDTH
    full = lambda shape: pl.BlockSpec(shape, lambda bi, i: (0,) * len(shape))
    tok = pl.BlockSpec((1, tm, c), lambda bi, i: (bi, i, 0))
    s1 = jax.ShapeDtypeStruct((b, lt, c), F32)
    return pl.pallas_call(
        functools.partial(_hy_prep_kernel, lc=lc, lt=lt, tm=tm),
        grid=(b, lt // tm),
        in_specs=_halo_specs(tm, cin, lt) + [full((3, cin)), full((1, cin))],
        out_specs=[tok, tok, tok],
        out_shape=[s1, s1, s1],
        compiler_params=_cparams(("parallel", "parallel")),
        name="hy_prep",
    )(uh, uh, uh, w, bias)


def _hy_filter_kernel(z_ref, w1_ref, b1_ref, f1_ref, w2_ref, b2_ref, f2_ref, w3_ref, dec_ref,
                      fw_o, bw_o, nrm_o, *, tl):
    i = pl.program_id(0)
    z = z_ref[...]
    h = jnp.sin(f1_ref[...] * (_mm(z, w1_ref[...]) + b1_ref[...]))
    h = jnp.sin(f2_ref[...] * (_mm(h, w2_ref[...]) + b2_ref[...]))
    h = _mm(h, w3_ref[...]) * jnp.exp(-z[:, 0:1] * dec_ref[...])
    c = HY_WIDTH
    fw = jnp.concatenate([h[:, 0:c], h[:, 2 * c:3 * c]], 1)
    bw = jnp.concatenate([h[:, c:2 * c], h[:, 3 * c:4 * c]], 1)
    rid = i * tl + lax.broadcasted_iota(jnp.int32, (tl, 1), 0)
    bw = jnp.where(rid == 0, 0.0, bw)
    fw_o[...] = fw
    bw_o[...] = bw

    @pl.when(i == 0)
    def _():
        nrm_o[...] = jnp.zeros(nrm_o.shape, F32)

    nrm_o[...] += jnp.sum(jnp.abs(fw) + jnp.abs(bw), 0, keepdims=True)


def _hy_filter(feat, w1p, b1, f1, w2, b2, f2, w3, dec):
    l, fe = feat.shape
    tl = _pick_tile(l, (512, 256))
    c2 = HY_ORDER * HY_WIDTH
    full = lambda shape: pl.BlockSpec(shape, lambda i: (0,) * len(shape))
    return pl.pallas_call(
        functools.partial(_hy_filter_kernel, tl=tl),
        grid=(l // tl,),
        in_specs=[pl.BlockSpec((tl, fe), lambda i: (i, 0)), full(w1p.shape), full(b1.shape), full(f1.shape),
                  full(w2.shape), full(b2.shape), full(f2.shape), full(w3.shape), full(dec.shape)],
        out_specs=[pl.BlockSpec((tl, c2), lambda i: (i, 0)), pl.BlockSpec((tl, c2), lambda i: (i, 0)),
                   pl.BlockSpec((1, c2), lambda i: (0, 0))],
        out_shape=[jax.ShapeDtypeStruct((l, c2), F32), jax.ShapeDtypeStruct((l, c2), F32),
                   jax.ShapeDtypeStruct((1, c2), F32)],
        compiler_params=_cparams(("arbitrary",)),
        name="hy_filter",
    )(feat, w1p, b1, f1, w2, b2, f2, w3, dec)


def _dft_cols_kernel(f_ref, xa_ref, xb_ref, o_ref, *, n1, pair):
    f = f_ref[...]
    pa = _mm(f, xa_ref[0])
    pb = _mm(f, xb_ref[0])
    if pair:
        o_ref[0] = pa[:n1]
        o_ref[1] = pa[n1:]
        o_ref[2] = pb[:n1]
        o_ref[3] = pb[n1:]
    else:
        o_ref[0] = pa[:n1] - pb[n1:]
        o_ref[1] = pb[:n1] + pa[n1:]


def _dft_cols(fstack, xa, ia, xb, ib, pair):
    n1 = fstack.shape[0] // 2
    _, nh, cols = xa.shape
    tc = _pick_tile(cols, (4096, 2048, 1024, 512, 256, 128))
    no = 4 if pair else 2
    return pl.pallas_call(
        functools.partial(_dft_cols_kernel, n1=n1, pair=pair),
        grid=(cols // tc,),
        in_specs=[pl.BlockSpec(fstack.shape, lambda j: (0, 0)),
                  pl.BlockSpec((1, nh, tc), lambda j: (ia, 0, j)),
                  pl.BlockSpec((1, nh, tc), lambda j: (ib, 0, j))],
        out_specs=pl.BlockSpec((no, n1, tc), lambda j: (0, 0, j)),
        out_shape=jax.ShapeDtypeStruct((no, n1, cols), F32),
        compiler_params=_cparams(("parallel",)),
        name="dft_cols",
    )(fstack, xa, xb)


def _cplx_left(gs, zr, zi, n):
    c = zr.shape[1]
    p = _mm(gs, jnp.concatenate([zr, zi], 1))
    return p[:n, :c] - p[n:, c:], p[:n, c:] + p[n:, :c]


def _spec_kernel(a_ref, g_ref, nrm_ref, o_ref, *, n_total):
    n2 = DFT_N2
    gs = jnp.concatenate([g_ref[0, 0], g_ref[0, 1]], 0)
    fr, fi = _cplx_left(gs, a_ref[0, 0], a_ref[1, 0], n2)
    br, bi = _cplx_left(gs, a_ref[2, 0], a_ref[3, 0], n2)
    s = 1.0 / (nrm_ref[...] * n_total)
    o_ref[0, 0] = (fr + br) * s
    o_ref[1, 0] = (fi - bi) * s


def _spec(a4, g, nrm, n_total):
    _, n1, n2, c2 = a4.shape
    return pl.pallas_call(
        functools.partial(_spec_kernel, n_total=float(n_total)),
        grid=(n1,),
        in_specs=[pl.BlockSpec((4, 1, n2, c2), lambda k: (0, k, 0, 0)),
                  pl.BlockSpec((1, 2, n2, n2), lambda k: (k, 0, 0, 0)),
                  pl.BlockSpec((1, c2), lambda k: (0, 0))],
        out_specs=pl.BlockSpec((2, 1, n2, c2), lambda k: (0, k, 0, 0)),
        out_shape=jax.ShapeDtypeStruct((2, n1, n2, c2), F32),
        compiler_params=_cparams(("parallel",)),
        name="hy_spec",
    )(a4, g, nrm)


def _conv_mid_kernel(a_ref, g_ref, gh_ref, k_ref, o_ref):
    n2 = DFT_N2
    gs = jnp.concatenate([g_ref[0, 0], g_ref[0, 1]], 0)
    xr, xi = _cplx_left(gs, a_ref[0, 0], a_ref[1, 0], n2)
    kr = k_ref[0, 0]
    ki = k_ref[1, 0]
    zr = xr * kr - xi * ki
    zi = xr * ki + xi * kr
    ghs = jnp.concatenate([gh_ref[0, 0], gh_ref[0, 1]], 0)
    yr, yi = _cplx_left(ghs, zr, zi, n2)
    o_ref[0, 0] = yr
    o_ref[1, 0] = yi


def _conv_mid(a, g, gh, kspec, order):
    _, n1, n2, c = a.shape
    return pl.pallas_call(
        _conv_mid_kernel,
        grid=(n1,),
        in_specs=[pl.BlockSpec((2, 1, n2, c), lambda k: (0, k, 0, 0)),
                  pl.BlockSpec((1, 2, n2, n2), lambda k: (k, 0, 0, 0)),
                  pl.BlockSpec((1, 2, n2, n2), lambda k: (k, 0, 0, 0)),
                  pl.BlockSpec((2, 1, n2, c), lambda k: (0, k, 0, order))],
        out_specs=pl.BlockSpec((2, 1, n2, c), lambda k: (0, k, 0, 0)),
        out_shape=jax.ShapeDtypeStruct((2, n1, n2, c), F32),
        compiler_params=_cparams(("parallel",)),
        name="hy_conv_mid",
    )(a, g, gh, kspec)


def _idft_cols_kernel(c_ref, b_ref, g0_ref, g1_ref, x0_ref, x1_ref, bias_ref, o_ref, *, nh):
    cs = c_ref[...]
    pr = _mm(cs, b_ref[0])
    pi = _mm(cs, b_ref[1])
    yr = pr[:nh] - pi[nh:]
    yi = pi[:nh] + pr[nh:]
    bias = bias_ref[...]
    o_ref[0] = g0_ref[0] * (yr + x0_ref[0] * bias)
    o_ref[1] = g1_ref[0] * (yi + x1_ref[0] * bias)


def _idft_cols(cstack, bv, gate, xin, bias_cols):
    nh2, n1 = cstack.shape
    nh = nh2 // 2
    cols = bv.shape[-1]
    tc = _pick_tile(cols, (4096, 2048, 1024, 512, 256, 128))
    row = lambda bi: pl.BlockSpec((1, nh, tc), lambda j: (bi, 0, j))
    return pl.pallas_call(
        functools.partial(_idft_cols_kernel, nh=nh),
        grid=(cols // tc,),
        in_specs=[pl.BlockSpec((nh2, n1), lambda j: (0, 0)),
                  pl.BlockSpec((2, n1, tc), lambda j: (0, 0, j)),
                  row(0), row(1), row(0), row(1),
                  pl.BlockSpec((1, tc), lambda j: (0, j))],
        out_specs=pl.BlockSpec((2, nh, tc), lambda j: (0, 0, j)),
        out_shape=jax.ShapeDtypeStruct((2, nh, cols), F32),
        compiler_params=_cparams(("parallel",)),
        name="idft_cols",
    )(cstack, bv, gate, gate, xin, xin, bias_cols)


def _hy_ctx_kernel(x1_ref, x2_ref, v_ref, fw_ref, bw_ref, nrm_ref, bias_ref, ff_ref, ci_ref, o_ref, *, lc):
    n = 2 * lc
    c = HY_WIDTH
    ff = ff_ref[...]
    ci = ci_ref[...]
    pf = _mm(ff, fw_ref[...])
    pb = _mm(ff, bw_ref[...])
    s = 1.0 / (nrm_ref[...] * float(n))
    kr = (pf[:n] + pb[:n]) * s
    ki = (pf[n:] - pb[n:]) * s

    def conv(z0, z1, o):
        xr, xi = _cplx_left(ff, z0, z1, n)
        krr = kr[:, o * c:(o + 1) * c]
        kii = ki[:, o * c:(o + 1) * c]
        return _cplx_left(ci, xr * krr - xi * kii, xr * kii + xi * krr, lc)

    v0 = v_ref[0]
    v1 = v_ref[1]
    y0, y1 = conv(v0, v1, 0)
    z0 = x1_ref[0] * (y0 + v0 * bias_ref[0:1])
    z1 = x1_ref[1] * (y1 + v1 * bias_ref[0:1])
    y0, y1 = conv(z0, z1, 1)
    o_ref[0] = x2_ref[0] * (y0 + z0 * bias_ref[1:2])
    o_ref[1] = x2_ref[1] * (y1 + z1 * bias_ref[1:2])


def _hy_ctx(x1, x2, v, fw, bw, nrm, bias, ff, ci):
    b, lc, c = v.shape
    vm = pl.BlockSpec(memory_space=pltpu.VMEM)
    return pl.pallas_call(
        functools.partial(_hy_ctx_kernel, lc=lc),
        in_specs=[vm] * 9,
        out_specs=vm,
        out_shape=jax.ShapeDtypeStruct((b, lc, c), F32),
        compiler_params=pltpu.CompilerParams(vmem_limit_bytes=VMEM_LIMIT),
        name="hy_ctx",
    )(x1, x2, v, fw, bw, nrm, bias, ff, ci)


def _outproj_kernel(att_ref, rw_ref, hy_ref, x_ref, mod_ref, w_ref, lg_ref, lb_ref, o_ref, *, lc, tm):
    row0 = pl.program_id(1) * tm
    a0 = ATT_WIDTH
    a1 = ATT_WIDTH + RWKV_WIDTH
    o = jnp.dot(att_ref[0], w_ref[:a0], preferred_element_type=F32)
    o += jnp.dot(rw_ref[0], w_ref[a0:a1], preferred_element_type=F32)
    o += jnp.dot(hy_ref[0].astype(BF16), w_ref[a1:], preferred_element_type=F32)
    g = _sel_mod(mod_ref, 2, row0, tm, lc)
    y = ALPHA * x_ref[0] + g * o
    o_ref[0] = _layer_norm(y) * lg_ref[...] + lb_ref[...]


def _outproj(att, rw, hy, xx, mod, w, lg, lb, lc):
    b, lt, d = xx.shape
    tm = TOK_TILE
    tok = lambda c: pl.BlockSpec((1, tm, c), lambda bi, i: (bi, i, 0))
    full = lambda shape: pl.BlockSpec(shape, lambda bi, i: (0,) * len(shape))
    return pl.pallas_call(
        functools.partial(_outproj_kernel, lc=lc, tm=tm),
        grid=(b, lt // tm),
        in_specs=[tok(ATT_WIDTH), tok(RWKV_WIDTH), tok(HY_WIDTH), tok(d),
                  pl.BlockSpec((1, 2, 6, d), lambda bi, i: (bi, 0, 0, 0)),
                  full(w.shape), full((1, d)), full((1, d))],
        out_specs=tok(d),
        out_shape=jax.ShapeDtypeStruct((b, lt, d), F32),
        compiler_params=_cparams(("parallel", "parallel")),
        name="outproj",
    )(att, rw, hy, xx, mod, w, lg, lb)


def _ffn_kernel(x_ref, mod_ref, w1_ref, w3_ref, w2_ref, lg_ref, lb_ref, o_ref, h_scr, acc_scr, *, lc, tm, nf):
    row0 = pl.program_id(1) * tm
    f = pl.program_id(2)

    @pl.when(f == 0)
    def _():
        sh = _sel_mod(mod_ref, 3, row0, tm, lc)
        sc = _sel_mod(mod_ref, 4, row0, tm, lc)
        h_scr[...] = (_layer_norm(x_ref[0]) * (1.0 + sc) + sh).astype(BF16)
        acc_scr[...] = jnp.zeros(acc_scr.shape, F32)

    h = h_scr[...]
    a = jnp.dot(h, w1_ref[...], preferred_element_type=F32)
    g = jnp.dot(h, w3_ref[...], preferred_element_type=F32)
    acc_scr[...] += jnp.dot((_silu(a) * g).astype(BF16), w2_ref[...], preferred_element_type=F32)

    @pl.when(f == nf - 1)
    def _():
        gate = _sel_mod(mod_ref, 5, row0, tm, lc)
        y = ALPHA * x_ref[0] + gate * acc_scr[...]
        o_ref[0] = _layer_norm(y) * lg_ref[...] + lb_ref[...]


def _ffn(xx, mod, w1, w3, w2, lg, lb, lc):
    b, lt, d = xx.shape
    ff = w1.shape[1]
    tm = _pick_tile(lt, (1280, 768, 512, 256))
    tf = _pick_tile(ff, (256, 128))
    nf = ff // tf
    return pl.pallas_call(
        functools.partial(_ffn_kernel, lc=lc, tm=tm, nf=nf),
        grid=(b, lt // tm, nf),
        in_specs=[pl.BlockSpec((1, tm, d), lambda bi, i, f: (bi, i, 0)),
                  pl.BlockSpec((1, 2, 6, d), lambda bi, i, f: (bi, 0, 0, 0)),
                  pl.BlockSpec((d, tf), lambda bi, i, f: (0, f)),
                  pl.BlockSpec((d, tf), lambda bi, i, f: (0, f)),
                  pl.BlockSpec((tf, d), lambda bi, i, f: (f, 0)),
                  pl.BlockSpec((1, d), lambda bi, i, f: (0, 0)),
                  pl.BlockSpec((1, d), lambda bi, i, f: (0, 0))],
        out_specs=pl.BlockSpec((1, tm, d), lambda bi, i, f: (bi, i, 0)),
        out_shape=jax.ShapeDtypeStruct((b, lt, d), F32),
        scratch_shapes=[pltpu.VMEM((tm, d), BF16), pltpu.VMEM((tm, d), F32)],
        compiler_params=_cparams(("parallel", "parallel", "arbitrary")),
        name="ffn",
    )(xx, mod, w1, w3, w2, lg, lb)


def _moe_kernel(x_ref, mod_ref, wr_ref, w1_ref, w3_ref, w2_ref, lg_ref, lb_ref, o_ref,
                h_scr, acc_scr, gate_scr, *, lc, tm, nf, ns):
    row0 = pl.program_id(1) * tm
    s = pl.program_id(2)

    @pl.when(s == 0)
    def _():
        sh = _sel_mod(mod_ref, 3, row0, tm, lc)
        sc = _sel_mod(mod_ref, 4, row0, tm, lc)
        h = _layer_norm(x_ref[0]) * (1.0 + sc) + sh
        h_scr[...] = h.astype(BF16)
        acc_scr[...] = jnp.zeros(acc_scr.shape, F32)
        logits = _mm(h, wr_ref[...])
        lane = lax.broadcasted_iota(jnp.int32, logits.shape, 1)
        neg = jnp.float32(-jnp.inf)
        lg = jnp.where(lane < N_EXPERTS, logits, neg)
        m1 = jnp.max(lg, -1, keepdims=True)
        i1 = jnp.min(jnp.where(lg == m1, lane, LANES), -1, keepdims=True)
        lg2 = jnp.where(lane == i1, neg, lg)
        m2 = jnp.max(lg2, -1, keepdims=True)
        i2 = jnp.min(jnp.where(lg2 == m2, lane, LANES), -1, keepdims=True)
        e2 = jnp.exp(m2 - m1)
        g1 = 1.0 / (1.0 + e2)
        g2 = e2 / (1.0 + e2)
        for e in range(N_EXPERTS):
            gate_scr[e] = jnp.where(i1 == e, g1, 0.0) + jnp.where(i2 == e, g2, 0.0)

    h = h_scr[...]
    a = jnp.dot(h, w1_ref[0], preferred_element_type=F32)
    g = jnp.dot(h, w3_ref[0], preferred_element_type=F32)
    act = _silu(a) * g * gate_scr[s // nf]
    acc_scr[...] += jnp.dot(act.astype(BF16), w2_ref[0], preferred_element_type=F32)

    @pl.when(s == ns - 1)
    def _():
        gate = _sel_mod(mod_ref, 5, row0, tm, lc)
        y = ALPHA * x_ref[0] + gate * acc_scr[...]
        o_ref[0] = _layer_norm(y) * lg_ref[...] + lb_ref[...]


def _moe(xx, mod, wr, w1, w3, w2, lg, lb, lc):
    b, lt, d = xx.shape
    ne, _, ff = w1.shape
    tm = _pick_tile(lt, (1280, 768, 512, 256))
    tf = _pick_tile(ff, (256, 128))
    nf = ff // tf
    ns = ne * nf
    return pl.pallas_call(
        functools.partial(_moe_kernel, lc=lc, tm=tm, nf=nf, ns=ns),
        grid=(b, lt // tm, ns),
        in_specs=[pl.BlockSpec((1, tm, d), lambda bi, i, s: (bi, i, 0)),
                  pl.BlockSpec((1, 2, 6, d), lambda bi, i, s: (bi, 0, 0, 0)),
                  pl.BlockSpec((d, LANES), lambda bi, i, s: (0, 0)),
                  pl.BlockSpec((1, d, tf), lambda bi, i, s: (s // nf, 0, s % nf)),
                  pl.BlockSpec((1, d, tf), lambda bi, i, s: (s // nf, 0, s % nf)),
                  pl.BlockSpec((1, tf, d), lambda bi, i, s: (s // nf, s % nf, 0)),
                  pl.BlockSpec((1, d), lambda bi, i, s: (0, 0)),
                  pl.BlockSpec((1, d), lambda bi, i, s: (0, 0))],
        out_specs=pl.BlockSpec((1, tm, d), lambda bi, i, s: (bi, i, 0)),
        out_shape=jax.ShapeDtypeStruct((b, lt, d), F32),
        scratch_shapes=[pltpu.VMEM((tm, d), BF16), pltpu.VMEM((tm, d), F32),
                        pltpu.VMEM((ne, tm, 1), F32)],
        compiler_params=_cparams(("parallel", "parallel", "arbitrary")),
        name="moe",
    )(xx, mod, wr, w1, w3, w2, lg, lb)


def _rope_tables(l, lc):
    rows = l // GRID_W
    row = jnp.repeat(jnp.arange(rows, dtype=F32), GRID_W)
    col = jnp.tile(jnp.arange(GRID_W, dtype=F32), rows)
    n_freq = HEAD_DIM // 4
    inv_freq = ROPE_THETA ** (-jnp.arange(n_freq, dtype=F32) / n_freq)
    ang = jnp.concatenate([row[:, None] * inv_freq, col[:, None] * inv_freq], -1)
    cos, sin = jnp.cos(ang), jnp.sin(ang)
    cos64 = jnp.concatenate([jnp.ones((lc, HEAD_DIM), F32), jnp.concatenate([cos, cos], -1)], 0)
    sin64 = jnp.concatenate([jnp.zeros((lc, HEAD_DIM), F32), jnp.concatenate([-sin, sin], -1)], 0)
    return cos64, sin64


def _hy_features(l):
    bands = (HY_EMB - 1) // 2
    t = jnp.linspace(0.0, 1.0, l, dtype=F32)[:, None]
    f = jnp.linspace(1e-4, bands - 1, bands, dtype=F32)[None, :]
    wt = 2.0 * math.pi * jnp.arange(l, dtype=F32)[:, None] / l
    z = jnp.concatenate([t, jnp.cos(f * wt), -jnp.sin(f * wt)], -1)
    return jnp.pad(z, ((0, 0), (0, LANES - HY_EMB)))


def _angle(idx, n):
    return (2.0 * math.pi / n) * (idx % n).astype(F32)


def _dft_tables(n1):
    nh = n1 // 2
    n2 = DFT_N2
    n = n1 * n2
    k1 = jnp.arange(n1, dtype=jnp.int32)
    a1 = _angle(k1[:, None] * jnp.arange(nh, dtype=jnp.int32)[None, :], n1)
    fstack = jnp.concatenate([jnp.cos(a1), -jnp.sin(a1)], 0)
    cstack = jnp.concatenate([jnp.cos(a1.T), jnp.sin(a1.T)], 0)
    k2 = jnp.arange(n2, dtype=jnp.int32)
    ag = _angle(k2[None, None, :] * (k1[:, None, None] + n1 * k2[None, :, None]), n)
    g = jnp.stack([jnp.cos(ag), -jnp.sin(ag)], 1)
    agt = jnp.swapaxes(ag, 1, 2)
    gh = jnp.stack([jnp.cos(agt), jnp.sin(agt)], 1)
    return fstack, cstack, g, gh


def _dense_dft_tables(lc):
    n = 2 * lc
    a = _angle(jnp.arange(n, dtype=jnp.int32)[:, None] * jnp.arange(lc, dtype=jnp.int32)[None, :], n)
    ff = jnp.concatenate([jnp.cos(a), -jnp.sin(a)], 0)
    ci = jnp.concatenate([jnp.cos(a.T), jnp.sin(a.T)], 0)
    return ff, ci


def kernel(x, c, ctx, c_ctx, ada_w, ada_b, w_in, w_out, q_gain, k_gain, rwkv_mu, rwkv_w0, rwkv_wB, rwkv_a0, rwkv_aB, rwkv_gB, rwkv_kk, rwkv_ka, rwkv_rk, rwkv_gn_g, rwkv_gn_b, hy_short_w, hy_short_b, hy_w1, hy_b1, hy_freq1, hy_w2, hy_b2, hy_freq2, hy_w3, hy_decay, hy_bias, ln1_g, ln1_b, ln2_g, ln2_b, ffn_w1, ffn_w3, ffn_w2, moe_router, moe_w1, moe_w3, moe_w2):
    b, l, d = x.shape
    lc = ctx.shape[1]
    lt = lc + l
    depth = ada_w.shape[0]
    assert b == 2, "the long convolution packs the two batch rows as one complex signal"
    assert d == D_MODEL and lc % TOK_TILE == 0 and l % TOK_TILE == 0 and (2 * l) % (2 * DFT_N2) == 0
    cw = RWKV_WIDTH

    xx = jnp.concatenate([ctx, x], 1)
    cond8 = jnp.zeros((SUBLANES, d), F32).at[:b].set(c).at[b].set(c_ctx)
    mod_all = _ada_mod(cond8, ada_w, ada_b)

    cos64, sin64 = _rope_tables(l, lc)
    n1 = 2 * l // DFT_N2
    nh = n1 // 2
    cols = DFT_N2 * HY_WIDTH
    fstack, cstack, g_tab, gh_tab = _dft_tables(n1)
    ff_c, ci_c = _dense_dft_tables(lc)
    feat_l = _hy_features(l)
    feat_c = _hy_features(lc)
    blk = jnp.arange(cw) // HEAD_DIM
    bd = (blk[:, None] == blk[None, :]).astype(F32)
    perm64 = jnp.concatenate([jnp.arange(0, HEAD_DIM, 2), jnp.arange(1, HEAD_DIM, 2)])
    perm_att = jnp.concatenate([h * HEAD_DIM + perm64 for h in range(ATT_HEADS + ATT_KV_HEADS)]
                               + [jnp.arange(ATT_WIDTH + ATT_KV_WIDTH, IN_ATT)])
    tq = TOK_TILE
    tk = _pick_tile(lt, (1280, 1024, 768, 512, 256))

    for li in range(depth):
        ml = mod_all[li]
        mod = jnp.stack([jnp.broadcast_to(ml[b].reshape(1, 6, d), (b, 6, d)), ml[:b].reshape(b, 6, d)], 1)
        wi = w_in[li]
        w_pad = jnp.concatenate([wi[:, :IN_ATT][:, perm_att], wi[:, IN_ATT:IN_ATT + IN_RWKV],
                                 jnp.zeros((d, IN_RWKV_PAD - IN_RWKV), F32), wi[:, IN_ATT + IN_RWKV:]],
                                1).astype(BF16)
        ua, ur, uh = _inproj(xx, mod, w_pad, lc)

        qt, kx, vt = _attn_prep(ua, cos64, sin64, q_gain[li][perm64][None], k_gain[li][perm64][None])
        att_c = _flash(qt, kx, vt, tq, TOK_TILE, 0, lc // tq, lc // TOK_TILE)
        att_l = _flash(qt, kx, vt, tq, tk, lc // tq, l // tq, lt // tk)
        att = jnp.concatenate([att_c, att_l], 1)

        wl = jnp.zeros((LANES, 5 * cw), F32)
        wl = wl.at[0:W_LORA, 0:cw].set(rwkv_wB[li, 0]).at[W_LORA:2 * W_LORA, cw:2 * cw].set(rwkv_wB[li, 1])
        o_a = 2 * W_LORA
        wl = wl.at[o_a:o_a + A_LORA, 2 * cw:3 * cw].set(rwkv_aB[li, 0])
        wl = wl.at[o_a + A_LORA:o_a + 2 * A_LORA, 3 * cw:4 * cw].set(rwkv_aB[li, 1])
        o_g = o_a + 2 * A_LORA
        wl = wl.at[o_g:o_g + G_LORA, 4 * cw:5 * cw].set(rwkv_gB[li])
        mu = jnp.pad(rwkv_mu[li], ((0, 0), (0, IN_RWKV_PAD - IN_RWKV)))
        r_, v_, kk_, g_, bon_, lw_, be_, kd_ = _rwkv_prep(
            ur, mu, wl, bd, rwkv_w0[li], rwkv_a0[li], rwkv_kk[li][None], rwkv_ka[li][None],
            rwkv_rk[li].reshape(1, cw), lc)
        yf = _wkv_scan(r_, v_, kk_, lw_, be_, kd_, lc, True)
        yb = _wkv_scan(r_, v_, kk_, lw_, be_, kd_, lc, False)
        rw = _rwkv_out(yf, yb, bon_, g_, bd, rwkv_gn_g[li][None], rwkv_gn_b[li][None])

        x1, x2, vv = _hy_prep(uh, hy_short_w[li], hy_short_b[li][None], lc)
        w1p = jnp.pad(hy_w1[li], ((0, LANES - HY_EMB), (0, 0)))
        fargs = (w1p, hy_b1[li][None], hy_freq1[li][None], hy_w2[li], hy_b2[li][None], hy_freq2[li][None],
                 hy_w3[li], hy_decay[li][None])
        fw, bw, nrm = _hy_filter(feat_l, *fargs)
        c2 = HY_ORDER * HY_WIDTH
        a4 = _dft_cols(fstack, fw.reshape(1, nh, DFT_N2 * c2), 0, bw.reshape(1, nh, DFT_N2 * c2), 0, True)
        kspec = _spec(a4.reshape(4, n1, DFT_N2, c2), g_tab, nrm, n1 * DFT_N2)
        lat = lambda t: t[:, lc:].reshape(b, nh, cols)
        x1l, x2l, zin = lat(x1), lat(x2), lat(vv)
        for o, gate in enumerate((x1l, x2l)):
            a = _dft_cols(fstack, zin, 0, zin, 1, False)
            bv = _conv_mid(a.reshape(2, n1, DFT_N2, HY_WIDTH), g_tab, gh_tab, kspec, o)
            bias_cols = jnp.tile(hy_bias[li, o], DFT_N2)[None]
            zin = _idft_cols(cstack, bv.reshape(2, n1, cols), gate, zin, bias_cols)
        hy_l = zin.reshape(b, l, HY_WIDTH)
        fw_c, bw_c, nrm_c = _hy_filter(feat_c, *fargs)
        hy_c = _hy_ctx(x1[:, :lc], x2[:, :lc], vv[:, :lc], fw_c, bw_c, nrm_c, hy_bias[li], ff_c, ci_c)
        hy = jnp.concatenate([hy_c, hy_l], 1)

        xx = _outproj(att, rw, hy, xx, mod, w_out[li].astype(BF16), ln1_g[li][None], ln1_b[li][None], lc)

        j = li // 2
        if li % 2 == 0:
            xx = _ffn(xx, mod, ffn_w1[j].astype(BF16), ffn_w3[j].astype(BF16), ffn_w2[j].astype(BF16),
                      ln2_g[li][None], ln2_b[li][None], lc)
        else:
            wr = jnp.pad(moe_router[j], ((0, 0), (0, LANES - N_EXPERTS)))
            xx = _moe(xx, mod, wr, moe_w1[j].astype(BF16), moe_w3[j].astype(BF16), moe_w2[j].astype(BF16),
                      ln2_g[li][None], ln2_b[li][None], lc)
    return xx[:, lc:]
```

```python
import functools
import math

import jax
import jax.numpy as jnp
from jax import lax
from jax.experimental import pallas as pl
from jax.experimental.pallas import tpu as pltpu

F32 = jnp.float32
BF16 = jnp.bfloat16
HI = lax.Precision.HIGHEST

D_MODEL = 1024
DEPTH = 2
GRID_W = 64
HEAD_DIM = 64
HALF_HD = HEAD_DIM // 2
ATT_WIDTH = 512
RWKV_WIDTH = 256
HY_WIDTH = 256
ATT_HEADS = 8
ATT_KV_HEADS = 2
ATT_REP = 4
ATT_KV_WIDTH = 128
ROPE_THETA = 10000.0
QK_EPS = 1e-6
RWKV_HEADS = 4
W_LORA = 16
A_LORA = 16
G_LORA = 32
RWKV_GN_EPS = 64e-5
HY_ORDER = 2
HY_EMB = 33
HY_FFN = 64
N_EXPERTS = 8
LN_EPS = 1e-6
IN_ATT = ATT_WIDTH + 2 * ATT_KV_WIDTH
IN_RWKV = 3 * RWKV_WIDTH + 2 * W_LORA + 2 * A_LORA + G_LORA
IN_RWKV_PAD = 896
IN_HY = 3 * HY_WIDTH
ALPHA = float((2 * DEPTH) ** 0.25)
LOG2E = 1.4426950408889634

LANES = 128
SUBLANES = 8
TOK_TILE = 256
WKV_CHUNK = 64
WKV_TILE = 256
DFT_N2 = 256
VMEM_LIMIT = 48 * 1024 * 1024


def _cparams(sem):
    return pltpu.CompilerParams(dimension_semantics=sem, vmem_limit_bytes=VMEM_LIMIT)


def _pick_tile(n, cands):
    for c in cands:
        if n % c == 0:
            return c
    raise ValueError(f"no tile for {n} in {cands}")


def _layer_norm(x):
    mu = jnp.mean(x, -1, keepdims=True)
    xc = x - mu
    var = jnp.mean(xc * xc, -1, keepdims=True)
    return xc * lax.rsqrt(var + LN_EPS)


def _sel_mod(mod_ref, j, row0, tm, lc):
    rid = row0 + lax.broadcasted_iota(jnp.int32, (tm, 1), 0)
    return jnp.where(rid < lc, mod_ref[0, 0, j:j + 1, :], mod_ref[0, 1, j:j + 1, :])


def _silu(x):
    return x * jax.nn.sigmoid(x)


def _ada_kernel(c_ref, w_ref, b_ref, o_ref):
    s = _silu(c_ref[...])
    o_ref[0] = jnp.dot(s, w_ref[0], precision=HI, preferred_element_type=F32) + b_ref[0]


def _ada_mod(cond8, ada_w, ada_b):
    depth, d, n = ada_w.shape
    tn = _pick_tile(n, (1536, 1024, 512, 256, 128))
    return pl.pallas_call(
        _ada_kernel,
        grid=(depth, n // tn),
        in_specs=[pl.BlockSpec((SUBLANES, d), lambda l, j: (0, 0)),
                  pl.BlockSpec((1, d, tn), lambda l, j: (l, 0, j)),
                  pl.BlockSpec((1, 1, tn), lambda l, j: (l, 0, j))],
        out_specs=pl.BlockSpec((1, SUBLANES, tn), lambda l, j: (l, 0, j)),
        out_shape=jax.ShapeDtypeStruct((depth, SUBLANES, n), F32),
        compiler_params=_cparams(("parallel", "parallel")),
        name="ada_mod",
    )(cond8, ada_w, ada_b.reshape(depth, 1, n))


def _inproj_kernel(x_ref, mod_ref, w_ref, oa_ref, or_ref, oh_ref, *, lc, tm):
    row0 = pl.program_id(1) * tm
    sh = _sel_mod(mod_ref, 0, row0, tm, lc)
    sc = _sel_mod(mod_ref, 1, row0, tm, lc)
    h = (_layer_norm(x_ref[0]) * (1.0 + sc) + sh).astype(BF16)
    u = jnp.dot(h, w_ref[...], preferred_element_type=F32)
    oa_ref[0] = u[:, :IN_ATT]
    or_ref[0] = u[:, IN_ATT:IN_ATT + IN_RWKV_PAD]
    oh_ref[0] = u[:, IN_ATT + IN_RWKV_PAD:]


def _inproj(xx, mod, w_pad, lc):
    b, lt, d = xx.shape
    tm = TOK_TILE
    n = w_pad.shape[1]
    return pl.pallas_call(
        functools.partial(_inproj_kernel, lc=lc, tm=tm),
        grid=(b, lt // tm),
        in_specs=[pl.BlockSpec((1, tm, d), lambda bi, i: (bi, i, 0)),
                  pl.BlockSpec((1, 2, 6, d), lambda bi, i: (bi, 0, 0, 0)),
                  pl.BlockSpec((d, n), lambda bi, i: (0, 0))],
        out_specs=[pl.BlockSpec((1, tm, IN_ATT), lambda bi, i: (bi, i, 0)),
                   pl.BlockSpec((1, tm, IN_RWKV_PAD), lambda bi, i: (bi, i, 0)),
                   pl.BlockSpec((1, tm, IN_HY), lambda bi, i: (bi, i, 0))],
        out_shape=[jax.ShapeDtypeStruct((b, lt, IN_ATT), F32),
                   jax.ShapeDtypeStruct((b, lt, IN_RWKV_PAD), F32),
                   jax.ShapeDtypeStruct((b, lt, IN_HY), F32)],
        compiler_params=_cparams(("parallel", "parallel")),
        name="inproj",
    )(xx, mod, w_pad)


def _attn_prep_kernel(u_ref, cos_ref, sin_ref, qg_ref, kg_ref, qt_ref, k_ref, vt_ref):
    u = u_ref[0]
    cos = cos_ref[...]
    sin = sin_ref[...]

    def norm_rope(xh, g):
        ms = jnp.mean(xh * xh, -1, keepdims=True)
        xn = xh * lax.rsqrt(ms + QK_EPS) * g
        sw = jnp.concatenate([xn[:, HALF_HD:], xn[:, :HALF_HD]], -1)
        return xn * cos + sw * sin

    qg = qg_ref[...]
    kg = kg_ref[...]
    tm = u.shape[0]
    qs = [norm_rope(u[:, h * HEAD_DIM:(h + 1) * HEAD_DIM], qg) * (LOG2E * HEAD_DIM ** -0.5)
          for h in range(ATT_HEADS)]
    qt = jnp.concatenate(qs, -1).T
    for g in range(ATT_KV_HEADS):
        base = g * ATT_REP * HEAD_DIM
        qt_ref[0, g, 0] = jnp.concatenate(
            [qt[base + r * HEAD_DIM:base + (r + 1) * HEAD_DIM] for r in range(ATT_REP)], -1).astype(BF16)
        k0 = ATT_WIDTH + g * HEAD_DIM
        k_ref[0, g] = norm_rope(u[:, k0:k0 + HEAD_DIM], kg).astype(BF16)
    v0 = ATT_WIDTH + ATT_KV_WIDTH
    vt = u[:, v0:v0 + ATT_KV_WIDTH].T
    for g in range(ATT_KV_HEADS):
        vt_ref[0, g] = vt[g * HEAD_DIM:(g + 1) * HEAD_DIM].astype(BF16)


def _attn_prep(ua, cos, sin, qg, kg, lc):
    b, lt, _ = ua.shape
    tm = TOK_TILE
    nct = lc // tm
    nl = lt // tm - nct
    q_pos = lambda i: jnp.where(i < nct, nl + i, i - nct)
    return pl.pallas_call(
        _attn_prep_kernel,
        grid=(b, lt // tm),
        in_specs=[pl.BlockSpec((1, tm, IN_ATT), lambda bi, i: (bi, i, 0)),
                  pl.BlockSpec((tm, HEAD_DIM), lambda bi, i: (i, 0)),
                  pl.BlockSpec((tm, HEAD_DIM), lambda bi, i: (i, 0)),
                  pl.BlockSpec((1, HEAD_DIM), lambda bi, i: (0, 0)),
                  pl.BlockSpec((1, HEAD_DIM), lambda bi, i: (0, 0))],
        out_specs=[pl.BlockSpec((1, ATT_KV_HEADS, 1, HEAD_DIM, ATT_REP * tm),
                                lambda bi, i: (bi, 0, q_pos(i), 0, 0)),
                   pl.BlockSpec((1, ATT_KV_HEADS, tm, HEAD_DIM), lambda bi, i: (bi, 0, i, 0)),
                   pl.BlockSpec((1, ATT_KV_HEADS, HEAD_DIM, tm), lambda bi, i: (bi, 0, 0, i))],
        out_shape=[jax.ShapeDtypeStruct((b, ATT_KV_HEADS, lt // tm, HEAD_DIM, ATT_REP * tm), BF16),
                   jax.ShapeDtypeStruct((b, ATT_KV_HEADS, lt, HEAD_DIM), BF16),
                   jax.ShapeDtypeStruct((b, ATT_KV_HEADS, HEAD_DIM, lt), BF16)],
        compiler_params=_cparams(("parallel", "parallel")),
        name="attn_prep",
    )(ua, cos, sin, qg, kg)


def _flash_kernel(qt_ref, k_ref, vt_ref, o_ref, m_scr, l_scr, acc_scr, *, nk, tq, sub, qn):
    j = pl.program_id(3)

    @pl.when(j == 0)
    def _():
        m_scr[...] = jnp.full(m_scr.shape, -jnp.inf, F32)
        l_scr[...] = jnp.zeros(l_scr.shape, F32)
        acc_scr[...] = jnp.zeros(acc_scr.shape, F32)

    qt = jnp.concatenate([qt_ref[0, 0, t] for t in range(qn)], -1)
    nsub = k_ref.shape[2] // sub
    m = m_scr[...]
    l = l_scr[...]
    acc = acc_scr[...]
    scores = lambda c: jnp.dot(k_ref[0, 0, c * sub:(c + 1) * sub, :], qt, preferred_element_type=F32)
    s_next = scores(0)
    for c in range(nsub):
        s = s_next
        if c + 1 < nsub:
            s_next = scores(c + 1)
        m_new = jnp.maximum(m, jnp.max(s, 0, keepdims=True))
        a = jnp.exp2(m - m_new)
        p = jnp.exp2(s - m_new)
        l = a * l + jnp.sum(p, 0, keepdims=True)
        acc = a * acc + jnp.dot(vt_ref[0, 0, :, c * sub:(c + 1) * sub], p.astype(BF16),
                                preferred_element_type=F32)
        m = m_new
    m_scr[...] = m
    l_scr[...] = l
    acc_scr[...] = acc

    @pl.when(j == nk - 1)
    def _():
        o = (acc / l).T
        for t in range(qn):
            o_ref[0, t * tq:(t + 1) * tq, :] = jnp.concatenate(
                [o[(t * ATT_REP + r) * tq:(t * ATT_REP + r + 1) * tq] for r in range(ATT_REP)], -1
            ).astype(o_ref.dtype)


def _flash(qt, k, vt, tq, tk, qn, q_blk0, nq, nk):
    b = qt.shape[0]
    lq = nq * qn * tq
    sub = _pick_tile(tk, (256, 128))
    lanes = qn * ATT_REP * tq
    return pl.pallas_call(
        functools.partial(_flash_kernel, nk=nk, tq=tq, sub=sub, qn=qn),
        grid=(b, ATT_KV_HEADS, nq, nk),
        in_specs=[pl.BlockSpec((1, 1, qn, HEAD_DIM, ATT_REP * tq), lambda bi, g, i, j: (bi, g, i + q_blk0, 0, 0)),
                  pl.BlockSpec((1, 1, tk, HEAD_DIM), lambda bi, g, i, j: (bi, g, j, 0)),
                  pl.BlockSpec((1, 1, HEAD_DIM, tk), lambda bi, g, i, j: (bi, g, 0, j))],
        out_specs=pl.BlockSpec((1, qn * tq, ATT_REP * HEAD_DIM), lambda bi, g, i, j: (bi, i, g)),
        out_shape=jax.ShapeDtypeStruct((b, lq, ATT_WIDTH), BF16),
        scratch_shapes=[pltpu.VMEM((1, lanes), F32),
                        pltpu.VMEM((1, lanes), F32),
                        pltpu.VMEM((HEAD_DIM, lanes), F32)],
        compiler_params=_cparams(("parallel", "parallel", "parallel", "arbitrary")),
        name="flash",
    )(qt, k, vt)


def _prev_next(u, up8, un8, i, tm, lc, lt):
    start = i * tm
    p_ok = jnp.logical_and(start != 0, start != lc)
    n_ok = jnp.logical_and(start + tm != lc, start + tm != lt)
    prow = jnp.where(p_ok, up8[SUBLANES - 1:SUBLANES], 0.0)
    nrow = jnp.where(n_ok, un8[0:1], 0.0)
    rid = lax.broadcasted_iota(jnp.int32, u.shape, 0)
    prev = jnp.where(rid == 0, prow, pltpu.roll(u, 1, 0))
    nxt = jnp.where(rid == tm - 1, nrow, pltpu.roll(u, tm - 1, 0))
    return prev, nxt


def _halo_specs(tm, c, lt):
    r = tm // SUBLANES
    last = lt // SUBLANES - 1
    return [pl.BlockSpec((1, tm, c), lambda bi, i: (bi, i, 0)),
            pl.BlockSpec((1, SUBLANES, c), lambda bi, i: (bi, jnp.maximum(i * r - 1, 0), 0)),
            pl.BlockSpec((1, SUBLANES, c), lambda bi, i: (bi, jnp.minimum((i + 1) * r, last), 0))]


def _softplus(z):
    return jnp.maximum(z, 0.0) + jnp.log1p(jnp.exp(-jnp.abs(z)))


def _rwkv_prep_kernel(u_ref, up_ref, un_ref, mu_ref, wl_ref, bd_ref, w0_ref, a0_ref, kkw_ref, ka_ref, rk_ref,
                      r_o, v_o, kk_o, g_o, bon_o, lw_o, be_o, kd_o, *, lc, lt, tm):
    i = pl.program_id(1)
    u = u_ref[0]
    prev, nxt = _prev_next(u, up_ref[0], un_ref[0], i, tm, lc, lt)
    us = u + mu_ref[0:1] * (prev - u) + mu_ref[1:2] * (nxt - u)
    c = RWKV_WIDTH
    r = us[:, 0:c]
    k = us[:, c:2 * c]
    v = us[:, 2 * c:3 * c]
    slab = us[:, 3 * c:3 * c + LANES]
    lane = lax.broadcasted_iota(jnp.int32, slab.shape, 1)
    o_a = 2 * W_LORA
    o_g = o_a + 2 * A_LORA
    act = jnp.where(lane < o_a, jnp.tanh(slab),
                    jnp.where(lane < o_g, slab,
                              jnp.where(lane < o_g + G_LORA, jax.nn.sigmoid(slab), 0.0)))
    lo = jnp.dot(act, wl_ref[...], precision=HI, preferred_element_type=F32)
    bd = bd_ref[...]
    kk0 = k * kkw_ref[...]
    kk = kk0 * lax.rsqrt(jnp.dot(kk0 * kk0, bd, precision=HI, preferred_element_type=F32) + 1e-12)
    r_o[0] = r
    v_o[0] = v
    kk_o[0] = kk
    g_o[0] = lo[:, 4 * c:5 * c]
    bon = jnp.zeros_like(r)
    for d in range(2):
        w_raw = w0_ref[d:d + 1] + lo[:, d * c:(d + 1) * c]
        lw = -jnp.exp(-_softplus(-w_raw) - 0.5)
        a = jax.nn.sigmoid(a0_ref[d:d + 1] + lo[:, (2 + d) * c:(3 + d) * c])
        kd = k * (1.0 + (a - 1.0) * ka_ref[...])
        lw_o[0, d] = lw
        be_o[0, d] = a * kk
        kd_o[0, d] = kd
        bon = bon + r * kd * rk_ref[...]
    bon_o[0] = jnp.dot(bon, bd, precision=HI, preferred_element_type=F32) * v


def _rwkv_prep(ur, mu, wl, bd, w0, a0, kkw, ka, rk, lc):
    b, lt, cp = ur.shape
    tm = TOK_TILE
    c = RWKV_WIDTH
    full = lambda shape: pl.BlockSpec(shape, lambda bi, i: (0,) * len(shape))
    tok = pl.BlockSpec((1, tm, c), lambda bi, i: (bi, i, 0))
    tok2 = pl.BlockSpec((1, 2, tm, c), lambda bi, i: (bi, 0, i, 0))
    s1 = jax.ShapeDtypeStruct((b, lt, c), F32)
    s2 = jax.ShapeDtypeStruct((b, 2, lt, c), F32)
    return pl.pallas_call(
        functools.partial(_rwkv_prep_kernel, lc=lc, lt=lt, tm=tm),
        grid=(b, lt // tm),
        in_specs=_halo_specs(tm, cp, lt) + [full((2, cp)), full((LANES, 5 * c)), full((c, c)), full((2, c)),
                                            full((2, c)), full((1, c)), full((1, c)), full((1, c))],
        out_specs=[tok, tok, tok, tok, tok, tok2, tok2, tok2],
        out_shape=[s1, s1, s1, s1, s1, s2, s2, s2],
        compiler_params=_cparams(("parallel", "parallel")),
        name="rwkv_prep",
    )(ur, ur, ur, mu, wl, bd, w0, a0, kkw, ka, rk)


def _mm(a, b):
    return jnp.dot(a, b, precision=HI, preferred_element_type=F32)


_NN = ((1,), (0,))
_NT = ((1,), (1,))
_TN = ((0,), (0,))


def _split2(a):
    hi = a.astype(BF16)
    return hi, (a - hi.astype(F32)).astype(BF16)


def _dotp(a, b, passes, dims=_NN):
    if a.ndim == 3:
        dn = (((dims[0][0] + 1,), (dims[1][0] + 1,)), ((0,), (0,)))
    else:
        dn = (dims, ((), ()))
    dg = lambda p, q: lax.dot_general(p, q, dn, preferred_element_type=F32)
    if passes == 1:
        return dg(a.astype(BF16), b.astype(BF16))
    ah, al = _split2(a)
    bh, bl = _split2(b)
    return dg(ah, bh) + dg(ah, bl) + dg(al, bh)


P_M = 1
P_INV = 1
P_W = 1
P_Z = 1
P_STATE = 3
P_DFT = 1


def _unit_tri_inv(a_mat, row, col, eye, passes):
    t = a_mat.shape[-1]
    eye_f = eye.astype(F32)
    base = SUBLANES
    same = (row // base) == (col // base)
    n1 = -jnp.where(same, a_mat, 0.0)
    n2 = _dotp(n1, n1, passes)
    n4 = _dotp(n2, n2, passes)
    x = _dotp(_dotp(eye_f + n1, eye_f + n2, passes), eye_f + n4, passes)
    m = base
    while m < t:
        off = jnp.logical_and((row // (2 * m)) == (col // (2 * m)), (row // m) != (col // m))
        x = x - _dotp(x, _dotp(jnp.where(off, a_mat, 0.0), x, passes), passes)
        m *= 2
    return x


def _wkv_kernel(r_ref, v_ref, kk_ref, lw_ref, be_ref, kd_ref, y_ref, h_scr, pq_scr, ry_scr, *, fwd):
    t = WKV_CHUNK
    n = HEAD_DIM
    g = WKV_TILE // WKV_CHUNK
    tt = WKV_TILE
    order = list(range(g)) if fwd else list(range(g - 1, -1, -1))

    @pl.when(pl.program_id(1) == 0)
    def _():
        h_scr[...] = jnp.zeros(h_scr.shape, F32)
        pq_scr[...] = jnp.zeros(pq_scr.shape, F32)
        ry_scr[...] = jnp.zeros(ry_scr.shape, F32)

    nh = RWKV_HEADS
    hm = h_scr[...]
    for p in range(g):
        ry = ry_scr[p * nh:(p + 1) * nh]
        pq = pq_scr[p * nh:(p + 1) * nh]
        y = _dotp(ry[:, :, :n], hm, P_STATE) + ry[:, :, n:]
        hm = _dotp(pq[:, :, :n], hm, P_STATE) + pq[:, :, n:]
        c = order[p]
        y_ref[0, c * t:(c + 1) * t, :] = jnp.concatenate([y[h] for h in range(nh)], -1)
    h_scr[...] = hm

    row = lax.broadcasted_iota(jnp.int32, (tt, tt), 0)
    col = lax.broadcasted_iota(jnp.int32, (tt, tt), 1)
    same = (row // t) == (col // t)
    tri = jnp.logical_and(same, (row >= col) if fwd else (row <= col))
    sums = jnp.concatenate([jnp.where(tri, 1.0, 0.0), jnp.where(same, 1.0, 0.0)], 0).astype(BF16)
    lw = lw_ref[0, 0]
    l1 = lw.astype(BF16)
    rem = lw - l1.astype(F32)
    l2 = rem.astype(BF16)
    l3 = (rem - l2.astype(F32)).astype(BF16)
    cc = (jnp.dot(sums, l1, preferred_element_type=F32) + jnp.dot(sums, l2, preferred_element_type=F32)
          + jnp.dot(sums, l3, preferred_element_type=F32))
    cum = cc[:tt]
    ctot = cc[tt:]
    e_neg = jnp.exp(-cum)
    e_end = jnp.exp(ctot - cum)
    g_end = jnp.exp(ctot)
    v = v_ref[0]
    be = be_ref[0, 0]
    kd = kd_ref[0, 0]
    kap_t = kk_ref[0] * jnp.exp(cum - lw)
    r_t = r_ref[0] * jnp.exp(cum)
    be_t = be * e_neg
    kd_t = kd * e_neg
    be_h = be * e_end
    kd_h = kd * e_end
    r64 = lax.broadcasted_iota(jnp.int32, (t, t), 0)
    c64 = lax.broadcasted_iota(jnp.int32, (t, t), 1)
    eye = r64 == c64
    strict = (r64 > c64) if fwd else (r64 < c64)
    incl = jnp.logical_or(strict, eye)

    def units(x, rows=t):
        return jnp.stack([x[order[p] * t:order[p] * t + rows, h * n:(h + 1) * n]
                          for p in range(g) for h in range(nh)], 0)

    kap = units(kap_t)
    rt = units(r_t)
    vh = units(v)
    nu = g * nh
    m = _dotp(jnp.concatenate([kap, rt], 1), jnp.concatenate([units(be_t), units(kd_t)], 1), P_M, _NT)
    a_mat = jnp.where(strict, m[:, :t, :t], 0.0)
    b_mat = jnp.where(strict, m[:, :t, t:], 0.0)
    ab_r = jnp.concatenate([jnp.where(incl, m[:, t:, :t], 0.0), jnp.where(incl, m[:, t:, t:], 0.0)], 2)
    tinv = _unit_tri_inv(a_mat, r64, c64, eye, P_INV)
    w = _dotp(tinv, jnp.concatenate([kap, _dotp(b_mat, vh, P_W)], 2), P_W)
    z = jnp.concatenate([-w, jnp.concatenate([jnp.zeros((nu, t, n), F32), vh], 2)], 1)
    ry = _dotp(ab_r, z, P_Z)
    ry_scr[:, :, :n] = ry[:, :, :n] + rt
    ry_scr[:, :, n:] = ry[:, :, n:]
    pq = _dotp(jnp.concatenate([units(be_h), units(kd_h)], 1), z, P_Z, _TN)
    gd = jnp.where(eye, jnp.broadcast_to(units(g_end, 1), (nu, n, n)), 0.0)
    pq_scr[:, :, :n] = pq[:, :, :n] + gd
    pq_scr[:, :, n:] = pq[:, :, n:]


def _wkv_scan(r, v, kk, lw, be, kd, lc, fwd):
    b, lt, c = r.shape
    tt = WKV_TILE
    nt = lt // tt
    ntc = lc // tt
    d = 0 if fwd else 1

    def tile(i):
        if fwd:
            return i
        return jnp.where(i < ntc, ntc - 1 - i, nt - 1 - (i - ntc))

    t_in = lambda i: tile(jnp.minimum(i, nt - 1))
    t_out = lambda i: tile(jnp.maximum(i - 1, 0))
    one = pl.BlockSpec((1, tt, c), lambda bi, i: (bi, t_in(i), 0))
    two = pl.BlockSpec((1, 1, tt, c), lambda bi, i: (bi, d, t_in(i), 0))
    g = tt // WKV_CHUNK
    return pl.pallas_call(
        functools.partial(_wkv_kernel, fwd=fwd),
        grid=(b, nt + 1),
        in_specs=[one, one, one, two, two, two],
        out_specs=pl.BlockSpec((1, tt, c), lambda bi, i: (bi, t_out(i), 0)),
        out_shape=jax.ShapeDtypeStruct((b, lt, c), F32),
        scratch_shapes=[pltpu.VMEM((RWKV_HEADS, HEAD_DIM, HEAD_DIM), F32),
                        pltpu.VMEM((g * RWKV_HEADS, HEAD_DIM, 2 * HEAD_DIM), F32),
                        pltpu.VMEM((g * RWKV_HEADS, WKV_CHUNK, 2 * HEAD_DIM), F32)],
        compiler_params=_cparams(("parallel", "arbitrary")),
        name="wkv_fwd" if fwd else "wkv_bwd",
    )(r, v, kk, lw, be, kd)


def _rwkv_out_kernel(yf_ref, yb_ref, bon_ref, g_ref, bd_ref, gg_ref, gb_ref, o_ref):
    y = yf_ref[0] + yb_ref[0] + bon_ref[0]
    bd = bd_ref[...]
    mu = _mm(y, bd) * (1.0 / HEAD_DIM)
    yc = y - mu
    var = _mm(yc * yc, bd) * (1.0 / HEAD_DIM)
    yn = yc * lax.rsqrt(var + RWKV_GN_EPS) * gg_ref[...] + gb_ref[...]
    o_ref[0] = (yn * g_ref[0]).astype(o_ref.dtype)


def _rwkv_out(yf, yb, bon, g, bd, gg, gb):
    b, lt, c = yf.shape
    tm = TOK_TILE
    tok = pl.BlockSpec((1, tm, c), lambda bi, i: (bi, i, 0))
    full = lambda shape: pl.BlockSpec(shape, lambda bi, i: (0,) * len(shape))
    return pl.pallas_call(
        _rwkv_out_kernel,
        grid=(b, lt // tm),
        in_specs=[tok, tok, tok, tok, full((c, c)), full((1, c)), full((1, c))],
        out_specs=tok,
        out_shape=jax.ShapeDtypeStruct((b, lt, c), BF16),
        compiler_params=_cparams(("parallel", "parallel")),
        name="rwkv_out",
    )(yf, yb, bon, g, bd, gg, gb)


def _hy_prep_kernel(u_ref, up_ref, un_ref, w_ref, b_ref, x1_o, x2_o, v_o, *, lc, lt, tm):
    i = pl.program_id(1)
    u = u_ref[0]
    prev, nxt = _prev_next(u, up_ref[0], un_ref[0], i, tm, lc, lt)
    y = prev * w_ref[0:1] + u * w_ref[1:2] + nxt * w_ref[2:3] + b_ref[...]
    c = HY_WIDTH
    x1_o[0] = y[:, :c]
    x2_o[0] = y[:, c:2 * c]
    v_o[0] = y[:, 2 * c:]


def _hy_prep(uh, w, bias, lc):
    b, lt, cin = uh.shape
    tm = TOK_TILE
    c = HY_WIDTH
    full = lambda shape: pl.BlockSpec(shape, lambda bi, i: (0,) * len(shape))
    tok = pl.BlockSpec((1, tm, c), lambda bi, i: (bi, i, 0))
    s1 = jax.ShapeDtypeStruct((b, lt, c), F32)
    return pl.pallas_call(
        functools.partial(_hy_prep_kernel, lc=lc, lt=lt, tm=tm),
        grid=(b, lt // tm),
        in_specs=_halo_specs(tm, cin, lt) + [full((3, cin)), full((1, cin))],
        out_specs=[tok, tok, tok],
        out_shape=[s1, s1, s1],
        compiler_params=_cparams(("parallel", "parallel")),
        name="hy_prep",
    )(uh, uh, uh, w, bias)


def _hy_filter_kernel(z_ref, w1_ref, b1_ref, f1_ref, w2_ref, b2_ref, f2_ref, w3_ref, dec_ref,
                      fw_o, bw_o, nrm_o, *, tl):
    i = pl.program_id(0)
    z = z_ref[...]
    h = jnp.sin(f1_ref[...] * (_mm(z, w1_ref[...]) + b1_ref[...]))
    h = jnp.sin(f2_ref[...] * (_mm(h, w2_ref[...]) + b2_ref[...]))
    h = _mm(h, w3_ref[...]) * jnp.exp(-z[:, 0:1] * dec_ref[...])
    c = HY_WIDTH
    fw = jnp.concatenate([h[:, 0:c], h[:, 2 * c:3 * c]], 1)
    bw = jnp.concatenate([h[:, c:2 * c], h[:, 3 * c:4 * c]], 1)
    rid = i * tl + lax.broadcasted_iota(jnp.int32, (tl, 1), 0)
    bw = jnp.where(rid == 0, 0.0, bw)
    fw_o[...] = fw
    bw_o[...] = bw

    @pl.when(i == 0)
    def _():
        nrm_o[...] = jnp.zeros(nrm_o.shape, F32)

    nrm_o[...] += jnp.sum(jnp.abs(fw) + jnp.abs(bw), 0, keepdims=True)


def _hy_filter(feat, w1p, b1, f1, w2, b2, f2, w3, dec):
    l, fe = feat.shape
    tl = _pick_tile(l, (512, 256))
    c2 = HY_ORDER * HY_WIDTH
    full = lambda shape: pl.BlockSpec(shape, lambda i: (0,) * len(shape))
    return pl.pallas_call(
        functools.partial(_hy_filter_kernel, tl=tl),
        grid=(l // tl,),
        in_specs=[pl.BlockSpec((tl, fe), lambda i: (i, 0)), full(w1p.shape), full(b1.shape), full(f1.shape),
                  full(w2.shape), full(b2.shape), full(f2.shape), full(w3.shape), full(dec.shape)],
        out_specs=[pl.BlockSpec((tl, c2), lambda i: (i, 0)), pl.BlockSpec((tl, c2), lambda i: (i, 0)),
                   pl.BlockSpec((1, c2), lambda i: (0, 0))],
        out_shape=[jax.ShapeDtypeStruct((l, c2), F32), jax.ShapeDtypeStruct((l, c2), F32),
                   jax.ShapeDtypeStruct((1, c2), F32)],
        compiler_params=_cparams(("arbitrary",)),
        name="hy_filter",
    )(feat, w1p, b1, f1, w2, b2, f2, w3, dec)


def _dft_cols_kernel(f_ref, xa_ref, xb_ref, o_ref, *, n1, pair):
    f = f_ref[...]
    pa = _dotp(f, xa_ref[0], P_DFT)
    pb = _dotp(f, xb_ref[0], P_DFT)
    if pair:
        o_ref[0] = pa[:n1].astype(o_ref.dtype)
        o_ref[1] = pa[n1:].astype(o_ref.dtype)
        o_ref[2] = pb[:n1].astype(o_ref.dtype)
        o_ref[3] = pb[n1:].astype(o_ref.dtype)
    else:
        o_ref[0] = (pa[:n1] - pb[n1:]).astype(o_ref.dtype)
        o_ref[1] = (pb[:n1] + pa[n1:]).astype(o_ref.dtype)


def _dft_cols(fstack, xa, ia, xb, ib, pair):
    n1 = fstack.shape[0] // 2
    _, nh, cols = xa.shape
    tc = _pick_tile(cols, (4096, 2048, 1024, 512, 256, 128))
    no = 4 if pair else 2
    return pl.pallas_call(
        functools.partial(_dft_cols_kernel, n1=n1, pair=pair),
        grid=(cols // tc,),
        in_specs=[pl.BlockSpec(fstack.shape, lambda j: (0, 0)),
                  pl.BlockSpec((1, nh, tc), lambda j: (ia, 0, j)),
                  pl.BlockSpec((1, nh, tc), lambda j: (ib, 0, j))],
        out_specs=pl.BlockSpec((no, n1, tc), lambda j: (0, 0, j)),
        out_shape=jax.ShapeDtypeStruct((no, n1, cols), BF16),
        compiler_params=_cparams(("parallel",)),
        name="dft_cols",
    )(fstack, xa, xb)


def _cplx_left(gs, zr, zi, n):
    c = zr.shape[1]
    p = _dotp(gs, jnp.concatenate([zr, zi], 1), P_DFT)
    return p[:n, :c] - p[n:, c:], p[:n, c:] + p[n:, :c]


def _spec_kernel(a_ref, g_ref, nrm_ref, o_ref, *, n_total):
    n2 = DFT_N2
    gs = jnp.concatenate([g_ref[0, 0], g_ref[0, 1]], 0)
    fr, fi = _cplx_left(gs, a_ref[0, 0], a_ref[1, 0], n2)
    br, bi = _cplx_left(gs, a_ref[2, 0], a_ref[3, 0], n2)
    s = 1.0 / (nrm_ref[...] * n_total)
    o_ref[0, 0] = (fr + br) * s
    o_ref[1, 0] = (fi - bi) * s


def _spec(a4, g, nrm, n_total):
    _, n1, n2, c2 = a4.shape
    return pl.pallas_call(
        functools.partial(_spec_kernel, n_total=float(n_total)),
        grid=(n1,),
        in_specs=[pl.BlockSpec((4, 1, n2, c2), lambda k: (0, k, 0, 0)),
                  pl.BlockSpec((1, 2, n2, n2), lambda k: (k, 0, 0, 0)),
                  pl.BlockSpec((1, c2), lambda k: (0, 0))],
        out_specs=pl.BlockSpec((2, 1, n2, c2), lambda k: (0, k, 0, 0)),
        out_shape=jax.ShapeDtypeStruct((2, n1, n2, c2), F32),
        compiler_params=_cparams(("parallel",)),
        name="hy_spec",
    )(a4, g, nrm)


def _conv_mid_kernel(a_ref, g_ref, gh_ref, k_ref, o_ref):
    n2 = DFT_N2
    gs = jnp.concatenate([g_ref[0, 0], g_ref[0, 1]], 0)
    xr, xi = _cplx_left(gs, a_ref[0, 0], a_ref[1, 0], n2)
    kr = k_ref[0, 0]
    ki = k_ref[1, 0]
    zr = xr * kr - xi * ki
    zi = xr * ki + xi * kr
    ghs = jnp.concatenate([gh_ref[0, 0], gh_ref[0, 1]], 0)
    yr, yi = _cplx_left(ghs, zr, zi, n2)
    o_ref[0, 0] = yr.astype(o_ref.dtype)
    o_ref[1, 0] = yi.astype(o_ref.dtype)


def _conv_mid(a, g, gh, kspec, order):
    _, n1, n2, c = a.shape
    return pl.pallas_call(
        _conv_mid_kernel,
        grid=(n1,),
        in_specs=[pl.BlockSpec((2, 1, n2, c), lambda k: (0, k, 0, 0)),
                  pl.BlockSpec((1, 2, n2, n2), lambda k: (k, 0, 0, 0)),
                  pl.BlockSpec((1, 2, n2, n2), lambda k: (k, 0, 0, 0)),
                  pl.BlockSpec((2, 1, n2, c), lambda k: (0, k, 0, order))],
        out_specs=pl.BlockSpec((2, 1, n2, c), lambda k: (0, k, 0, 0)),
        out_shape=jax.ShapeDtypeStruct((2, n1, n2, c), BF16),
        compiler_params=_cparams(("parallel",)),
        name="hy_conv_mid",
    )(a, g, gh, kspec)


def _idft_cols_kernel(c_ref, b_ref, g0_ref, g1_ref, x0_ref, x1_ref, bias_ref, o_ref, *, nh):
    cs = c_ref[...]
    pr = _dotp(cs, b_ref[0], P_DFT)
    pi = _dotp(cs, b_ref[1], P_DFT)
    yr = pr[:nh] - pi[nh:]
    yi = pi[:nh] + pr[nh:]
    bias = bias_ref[...]
    o_ref[0] = g0_ref[0] * (yr + x0_ref[0] * bias)
    o_ref[1] = g1_ref[0] * (yi + x1_ref[0] * bias)


def _idft_cols(cstack, bv, gate, xin, bias_cols):
    nh2, n1 = cstack.shape
    nh = nh2 // 2
    cols = bv.shape[-1]
    tc = _pick_tile(cols, (4096, 2048, 1024, 512, 256, 128))
    row = lambda bi: pl.BlockSpec((1, nh, tc), lambda j: (bi, 0, j))
    return pl.pallas_call(
        functools.partial(_idft_cols_kernel, nh=nh),
        grid=(cols // tc,),
        in_specs=[pl.BlockSpec((nh2, n1), lambda j: (0, 0)),
                  pl.BlockSpec((2, n1, tc), lambda j: (0, 0, j)),
                  row(0), row(1), row(0), row(1),
                  pl.BlockSpec((1, tc), lambda j: (0, j))],
        out_specs=pl.BlockSpec((2, nh, tc), lambda j: (0, 0, j)),
        out_shape=jax.ShapeDtypeStruct((2, nh, cols), F32),
        compiler_params=_cparams(("parallel",)),
        name="idft_cols",
    )(cstack, bv, gate, gate, xin, xin, bias_cols)


def _hy_ctx_kernel(x1_ref, x2_ref, v_ref, fw_ref, bw_ref, nrm_ref, bias_ref, ff_ref, ci_ref, o_ref, *, lc):
    n = 2 * lc
    c = HY_WIDTH
    ff = ff_ref[...]
    ci = ci_ref[...]
    pf = _mm(ff, fw_ref[...])
    pb = _mm(ff, bw_ref[...])
    s = 1.0 / (nrm_ref[...] * float(n))
    kr = (pf[:n] + pb[:n]) * s
    ki = (pf[n:] - pb[n:]) * s

    def conv(z0, z1, o):
        xr, xi = _cplx_left(ff, z0, z1, n)
        krr = kr[:, o * c:(o + 1) * c]
        kii = ki[:, o * c:(o + 1) * c]
        return _cplx_left(ci, xr * krr - xi * kii, xr * kii + xi * krr, lc)

    v0 = v_ref[0]
    v1 = v_ref[1]
    y0, y1 = conv(v0, v1, 0)
    z0 = x1_ref[0] * (y0 + v0 * bias_ref[0:1])
    z1 = x1_ref[1] * (y1 + v1 * bias_ref[0:1])
    y0, y1 = conv(z0, z1, 1)
    o_ref[0] = x2_ref[0] * (y0 + z0 * bias_ref[1:2])
    o_ref[1] = x2_ref[1] * (y1 + z1 * bias_ref[1:2])


def _hy_ctx(x1, x2, v, fw, bw, nrm, bias, ff, ci):
    b, lc, c = v.shape
    vm = pl.BlockSpec(memory_space=pltpu.VMEM)
    return pl.pallas_call(
        functools.partial(_hy_ctx_kernel, lc=lc),
        in_specs=[vm] * 9,
        out_specs=vm,
        out_shape=jax.ShapeDtypeStruct((b, lc, c), F32),
        compiler_params=pltpu.CompilerParams(vmem_limit_bytes=VMEM_LIMIT),
        name="hy_ctx",
    )(x1, x2, v, fw, bw, nrm, bias, ff, ci)


def _outproj_kernel(attc_ref, attl_ref, rw_ref, hyc_ref, hyl_ref, x_ref, mod_ref, w_ref, lg_ref, lb_ref, o_ref,
                    *, lc, tm):
    row0 = pl.program_id(1) * tm
    a0 = ATT_WIDTH
    a1 = ATT_WIDTH + RWKV_WIDTH
    is_ctx = row0 < lc
    att = jnp.where(is_ctx, attc_ref[0], attl_ref[0])
    hy = jnp.where(is_ctx, hyc_ref[0], hyl_ref[0])
    o = jnp.dot(att, w_ref[:a0], preferred_element_type=F32)
    o += jnp.dot(rw_ref[0], w_ref[a0:a1], preferred_element_type=F32)
    o += jnp.dot(hy.astype(BF16), w_ref[a1:], preferred_element_type=F32)
    g = _sel_mod(mod_ref, 2, row0, tm, lc)
    y = ALPHA * x_ref[0] + g * o
    o_ref[0] = _layer_norm(y) * lg_ref[...] + lb_ref[...]


def _outproj(att_c, att_l, rw, hy_c, hy_l, xx, mod, w, lg, lb, lc):
    b, lt, d = xx.shape
    tm = TOK_TILE
    nct = lc // tm
    tok = lambda c: pl.BlockSpec((1, tm, c), lambda bi, i: (bi, i, 0))
    ctx = lambda c: pl.BlockSpec((1, tm, c), lambda bi, i: (bi, jnp.minimum(i, nct - 1), 0))
    lat = lambda c: pl.BlockSpec((1, tm, c), lambda bi, i: (bi, jnp.maximum(i - nct, 0), 0))
    full = lambda shape: pl.BlockSpec(shape, lambda bi, i: (0,) * len(shape))
    return pl.pallas_call(
        functools.partial(_outproj_kernel, lc=lc, tm=tm),
        grid=(b, lt // tm),
        in_specs=[ctx(ATT_WIDTH), lat(ATT_WIDTH), tok(RWKV_WIDTH), ctx(HY_WIDTH), lat(HY_WIDTH), tok(d),
                  pl.BlockSpec((1, 2, 6, d), lambda bi, i: (bi, 0, 0, 0)),
                  full(w.shape), full((1, d)), full((1, d))],
        out_specs=tok(d),
        out_shape=jax.ShapeDtypeStruct((b, lt, d), F32),
        compiler_params=_cparams(("parallel", "parallel")),
        name="outproj",
    )(att_c, att_l, rw, hy_c, hy_l, xx, mod, w, lg, lb)


def _ffn_kernel(x_ref, mod_ref, w1_ref, w3_ref, w2_ref, lg_ref, lb_ref, o_ref, h_scr, acc_scr, *, lc, tm, nf):
    row0 = pl.program_id(1) * tm
    f = pl.program_id(2)

    @pl.when(f == 0)
    def _():
        sh = _sel_mod(mod_ref, 3, row0, tm, lc)
        sc = _sel_mod(mod_ref, 4, row0, tm, lc)
        h_scr[...] = (_layer_norm(x_ref[0]) * (1.0 + sc) + sh).astype(BF16)
        acc_scr[...] = jnp.zeros(acc_scr.shape, F32)

    h = h_scr[...]
    a = jnp.dot(h, w1_ref[...], preferred_element_type=F32)
    g = jnp.dot(h, w3_ref[...], preferred_element_type=F32)
    acc_scr[...] += jnp.dot((_silu(a) * g).astype(BF16), w2_ref[...], preferred_element_type=F32)

    @pl.when(f == nf - 1)
    def _():
        gate = _sel_mod(mod_ref, 5, row0, tm, lc)
        y = ALPHA * x_ref[0] + gate * acc_scr[...]
        o_ref[0] = _layer_norm(y) * lg_ref[...] + lb_ref[...]


def _ffn(xx, mod, w1, w3, w2, lg, lb, lc):
    b, lt, d = xx.shape
    ff = w1.shape[1]
    tm = _pick_tile(lt, (1280, 768, 512, 256))
    tf = _pick_tile(ff, (256, 128))
    nf = ff // tf
    return pl.pallas_call(
        functools.partial(_ffn_kernel, lc=lc, tm=tm, nf=nf),
        grid=(b, lt // tm, nf),
        in_specs=[pl.BlockSpec((1, tm, d), lambda bi, i, f: (bi, i, 0)),
                  pl.BlockSpec((1, 2, 6, d), lambda bi, i, f: (bi, 0, 0, 0)),
                  pl.BlockSpec((d, tf), lambda bi, i, f: (0, f)),
                  pl.BlockSpec((d, tf), lambda bi, i, f: (0, f)),
                  pl.BlockSpec((tf, d), lambda bi, i, f: (f, 0)),
                  pl.BlockSpec((1, d), lambda bi, i, f: (0, 0)),
                  pl.BlockSpec((1, d), lambda bi, i, f: (0, 0))],
        out_specs=pl.BlockSpec((1, tm, d), lambda bi, i, f: (bi, i, 0)),
        out_shape=jax.ShapeDtypeStruct((b, lt, d), F32),
        scratch_shapes=[pltpu.VMEM((tm, d), BF16), pltpu.VMEM((tm, d), F32)],
        compiler_params=_cparams(("parallel", "parallel", "arbitrary")),
        name="ffn",
    )(xx, mod, w1, w3, w2, lg, lb)


def _moe_kernel(x_ref, mod_ref, wr_ref, w1_ref, w3_ref, w2_ref, lg_ref, lb_ref, o_ref,
                h_scr, acc_scr, gate_scr, *, lc, tm, nf, ns):
    row0 = pl.program_id(1) * tm
    s = pl.program_id(2)

    @pl.when(s == 0)
    def _():
        sh = _sel_mod(mod_ref, 3, row0, tm, lc)
        sc = _sel_mod(mod_ref, 4, row0, tm, lc)
        h = _layer_norm(x_ref[0]) * (1.0 + sc) + sh
        h_scr[...] = h.astype(BF16)
        acc_scr[...] = jnp.zeros(acc_scr.shape, F32)
        logits = _mm(h, wr_ref[...])
        lane = lax.broadcasted_iota(jnp.int32, logits.shape, 1)
        neg = jnp.float32(-jnp.inf)
        lg = jnp.where(lane < N_EXPERTS, logits, neg)
        m1 = jnp.max(lg, -1, keepdims=True)
        i1 = jnp.min(jnp.where(lg == m1, lane, LANES), -1, keepdims=True)
        lg2 = jnp.where(lane == i1, neg, lg)
        m2 = jnp.max(lg2, -1, keepdims=True)
        i2 = jnp.min(jnp.where(lg2 == m2, lane, LANES), -1, keepdims=True)
        e2 = jnp.exp(m2 - m1)
        g1 = 1.0 / (1.0 + e2)
        g2 = e2 / (1.0 + e2)
        for e in range(N_EXPERTS):
            gate_scr[e] = jnp.where(i1 == e, g1, 0.0) + jnp.where(i2 == e, g2, 0.0)

    h = h_scr[...]
    a = jnp.dot(h, w1_ref[0], preferred_element_type=F32)
    g = jnp.dot(h, w3_ref[0], preferred_element_type=F32)
    act = _silu(a) * g * gate_scr[s // nf]
    acc_scr[...] += jnp.dot(act.astype(BF16), w2_ref[0], preferred_element_type=F32)

    @pl.when(s == ns - 1)
    def _():
        gate = _sel_mod(mod_ref, 5, row0, tm, lc)
        y = ALPHA * x_ref[0] + gate * acc_scr[...]
        o_ref[0] = _layer_norm(y) * lg_ref[...] + lb_ref[...]


def _moe(xx, mod, wr, w1, w3, w2, lg, lb, lc):
    b, lt, d = xx.shape
    ne, _, ff = w1.shape
    tm = _pick_tile(lt, (1280, 768, 512, 256))
    tf = _pick_tile(ff, (256, 128))
    nf = ff // tf
    ns = ne * nf
    return pl.pallas_call(
        functools.partial(_moe_kernel, lc=lc, tm=tm, nf=nf, ns=ns),
        grid=(b, lt // tm, ns),
        in_specs=[pl.BlockSpec((1, tm, d), lambda bi, i, s: (bi, i, 0)),
                  pl.BlockSpec((1, 2, 6, d), lambda bi, i, s: (bi, 0, 0, 0)),
                  pl.BlockSpec((d, LANES), lambda bi, i, s: (0, 0)),
                  pl.BlockSpec((1, d, tf), lambda bi, i, s: (s // nf, 0, s % nf)),
                  pl.BlockSpec((1, d, tf), lambda bi, i, s: (s // nf, 0, s % nf)),
                  pl.BlockSpec((1, tf, d), lambda bi, i, s: (s // nf, s % nf, 0)),
                  pl.BlockSpec((1, d), lambda bi, i, s: (0, 0)),
                  pl.BlockSpec((1, d), lambda bi, i, s: (0, 0))],
        out_specs=pl.BlockSpec((1, tm, d), lambda bi, i, s: (bi, i, 0)),
        out_shape=jax.ShapeDtypeStruct((b, lt, d), F32),
        scratch_shapes=[pltpu.VMEM((tm, d), BF16), pltpu.VMEM((tm, d), F32),
                        pltpu.VMEM((ne, tm, 1), F32)],
        compiler_params=_cparams(("parallel", "parallel", "arbitrary")),
        name="moe",
    )(xx, mod, wr, w1, w3, w2, lg, lb)


def _rope_tables(l, lc):
    rows = l // GRID_W
    row = jnp.repeat(jnp.arange(rows, dtype=F32), GRID_W)
    col = jnp.tile(jnp.arange(GRID_W, dtype=F32), rows)
    n_freq = HEAD_DIM // 4
    inv_freq = ROPE_THETA ** (-jnp.arange(n_freq, dtype=F32) / n_freq)
    ang = jnp.concatenate([row[:, None] * inv_freq, col[:, None] * inv_freq], -1)
    cos, sin = jnp.cos(ang), jnp.sin(ang)
    cos64 = jnp.concatenate([jnp.ones((lc, HEAD_DIM), F32), jnp.concatenate([cos, cos], -1)], 0)
    sin64 = jnp.concatenate([jnp.zeros((lc, HEAD_DIM), F32), jnp.concatenate([-sin, sin], -1)], 0)
    return cos64, sin64


def _hy_features(l):
    bands = (HY_EMB - 1) // 2
    t = jnp.linspace(0.0, 1.0, l, dtype=F32)[:, None]
    f = jnp.linspace(1e-4, bands - 1, bands, dtype=F32)[None, :]
    wt = 2.0 * math.pi * jnp.arange(l, dtype=F32)[:, None] / l
    z = jnp.concatenate([t, jnp.cos(f * wt), -jnp.sin(f * wt)], -1)
    return jnp.pad(z, ((0, 0), (0, LANES - HY_EMB)))


def _angle(idx, n):
    return (2.0 * math.pi / n) * (idx % n).astype(F32)


def _dft_tables(n1):
    nh = n1 // 2
    n2 = DFT_N2
    n = n1 * n2
    k1 = jnp.arange(n1, dtype=jnp.int32)
    a1 = _angle(k1[:, None] * jnp.arange(nh, dtype=jnp.int32)[None, :], n1)
    fstack = jnp.concatenate([jnp.cos(a1), -jnp.sin(a1)], 0)
    cstack = jnp.concatenate([jnp.cos(a1.T), jnp.sin(a1.T)], 0)
    k2 = jnp.arange(n2, dtype=jnp.int32)
    ag = _angle(k2[None, None, :] * (k1[:, None, None] + n1 * k2[None, :, None]), n)
    g = jnp.stack([jnp.cos(ag), -jnp.sin(ag)], 1)
    agt = jnp.swapaxes(ag, 1, 2)
    gh = jnp.stack([jnp.cos(agt), jnp.sin(agt)], 1)
    return fstack, cstack, g, gh


def _dense_dft_tables(lc):
    n = 2 * lc
    a = _angle(jnp.arange(n, dtype=jnp.int32)[:, None] * jnp.arange(lc, dtype=jnp.int32)[None, :], n)
    ff = jnp.concatenate([jnp.cos(a), -jnp.sin(a)], 0)
    ci = jnp.concatenate([jnp.cos(a.T), jnp.sin(a.T)], 0)
    return ff, ci


def kernel(x, c, ctx, c_ctx, ada_w, ada_b, w_in, w_out, q_gain, k_gain, rwkv_mu, rwkv_w0, rwkv_wB, rwkv_a0, rwkv_aB, rwkv_gB, rwkv_kk, rwkv_ka, rwkv_rk, rwkv_gn_g, rwkv_gn_b, hy_short_w, hy_short_b, hy_w1, hy_b1, hy_freq1, hy_w2, hy_b2, hy_freq2, hy_w3, hy_decay, hy_bias, ln1_g, ln1_b, ln2_g, ln2_b, ffn_w1, ffn_w3, ffn_w2, moe_router, moe_w1, moe_w3, moe_w2):
    b, l, d = x.shape
    lc = ctx.shape[1]
    lt = lc + l
    depth = ada_w.shape[0]
    assert b == 2, "the long convolution packs the two batch rows as one complex signal"
    assert d == D_MODEL and lc % TOK_TILE == 0 and l % TOK_TILE == 0 and (2 * l) % (2 * DFT_N2) == 0
    cw = RWKV_WIDTH

    xx = jnp.concatenate([ctx, x], 1)
    cond8 = jnp.zeros((SUBLANES, d), F32).at[:b].set(c).at[b].set(c_ctx)
    mod_all = _ada_mod(cond8, ada_w, ada_b)

    cos64, sin64 = _rope_tables(l, lc)
    n1 = 2 * l // DFT_N2
    nh = n1 // 2
    cols = DFT_N2 * HY_WIDTH
    fstack, cstack, g_tab, gh_tab = (t.astype(BF16) for t in _dft_tables(n1))
    ff_c, ci_c = _dense_dft_tables(lc)
    feat_l = _hy_features(l)
    feat_c = _hy_features(lc)
    blk = jnp.arange(cw) // HEAD_DIM
    bd = (blk[:, None] == blk[None, :]).astype(F32)
    perm64 = jnp.concatenate([jnp.arange(0, HEAD_DIM, 2), jnp.arange(1, HEAD_DIM, 2)])
    perm_att = jnp.concatenate([h * HEAD_DIM + perm64 for h in range(ATT_HEADS + ATT_KV_HEADS)]
                               + [jnp.arange(ATT_WIDTH + ATT_KV_WIDTH, IN_ATT)])
    tq = TOK_TILE
    qn = 2 if (l // tq) % 2 == 0 else 1
    tk = _pick_tile(lt, (1280, 1024, 768, 512, 256))

    for li in range(depth):
        ml = mod_all[li]
        mod = jnp.stack([jnp.broadcast_to(ml[b].reshape(1, 6, d), (b, 6, d)), ml[:b].reshape(b, 6, d)], 1)
        wi = w_in[li]
        w_pad = jnp.concatenate([wi[:, :IN_ATT][:, perm_att], wi[:, IN_ATT:IN_ATT + IN_RWKV],
                                 jnp.zeros((d, IN_RWKV_PAD - IN_RWKV), F32), wi[:, IN_ATT + IN_RWKV:]],
                                1).astype(BF16)
        ua, ur, uh = _inproj(xx, mod, w_pad, lc)

        qt, kx, vt = _attn_prep(ua, cos64, sin64, q_gain[li][perm64][None], k_gain[li][perm64][None], lc)
        att_c = _flash(qt, kx, vt, tq, TOK_TILE, 1, l // tq, lc // tq, lc // TOK_TILE)
        att_l = _flash(qt, kx, vt, tq, tk, qn, 0, l // (qn * tq), lt // tk)

        wl = jnp.zeros((LANES, 5 * cw), F32)
        wl = wl.at[0:W_LORA, 0:cw].set(rwkv_wB[li, 0]).at[W_LORA:2 * W_LORA, cw:2 * cw].set(rwkv_wB[li, 1])
        o_a = 2 * W_LORA
        wl = wl.at[o_a:o_a + A_LORA, 2 * cw:3 * cw].set(rwkv_aB[li, 0])
        wl = wl.at[o_a + A_LORA:o_a + 2 * A_LORA, 3 * cw:4 * cw].set(rwkv_aB[li, 1])
        o_g = o_a + 2 * A_LORA
        wl = wl.at[o_g:o_g + G_LORA, 4 * cw:5 * cw].set(rwkv_gB[li])
        mu = jnp.pad(rwkv_mu[li], ((0, 0), (0, IN_RWKV_PAD - IN_RWKV)))
        r_, v_, kk_, g_, bon_, lw_, be_, kd_ = _rwkv_prep(
            ur, mu, wl, bd, rwkv_w0[li], rwkv_a0[li], rwkv_kk[li][None], rwkv_ka[li][None],
            rwkv_rk[li].reshape(1, cw), lc)
        yf = _wkv_scan(r_, v_, kk_, lw_, be_, kd_, lc, True)
        yb = _wkv_scan(r_, v_, kk_, lw_, be_, kd_, lc, False)
        rw = _rwkv_out(yf, yb, bon_, g_, bd, rwkv_gn_g[li][None], rwkv_gn_b[li][None])

        x1, x2, vv = _hy_prep(uh, hy_short_w[li], hy_short_b[li][None], lc)
        w1p = jnp.pad(hy_w1[li], ((0, LANES - HY_EMB), (0, 0)))
        fargs = (w1p, hy_b1[li][None], hy_freq1[li][None], hy_w2[li], hy_b2[li][None], hy_freq2[li][None],
                 hy_w3[li], hy_decay[li][None])
        fw, bw, nrm = _hy_filter(feat_l, *fargs)
        c2 = HY_ORDER * HY_WIDTH
        a4 = _dft_cols(fstack, fw.reshape(1, nh, DFT_N2 * c2), 0, bw.reshape(1, nh, DFT_N2 * c2), 0, True)
        kspec = _spec(a4.reshape(4, n1, DFT_N2, c2), g_tab, nrm, n1 * DFT_N2)
        lat = lambda t: t[:, lc:].reshape(b, nh, cols)
        x1l, x2l, zin = lat(x1), lat(x2), lat(vv)
        for o, gate in enumerate((x1l, x2l)):
            a = _dft_cols(fstack, zin, 0, zin, 1, False)
            bv = _conv_mid(a.reshape(2, n1, DFT_N2, HY_WIDTH), g_tab, gh_tab, kspec, o)
            bias_cols = jnp.tile(hy_bias[li, o], DFT_N2)[None]
            zin = _idft_cols(cstack, bv.reshape(2, n1, cols), gate, zin, bias_cols)
        hy_l = zin.reshape(b, l, HY_WIDTH)
        fw_c, bw_c, nrm_c = _hy_filter(feat_c, *fargs)
        hy_c = _hy_ctx(x1[:, :lc], x2[:, :lc], vv[:, :lc], fw_c, bw_c, nrm_c, hy_bias[li], ff_c, ci_c)

        xx = _outproj(att_c, att_l, rw, hy_c, hy_l, xx, mod, w_out[li].astype(BF16),
                      ln1_g[li][None], ln1_b[li][None], lc)

        j = li // 2
        if li % 2 == 0:
            xx = _ffn(xx, mod, ffn_w1[j].astype(BF16), ffn_w3[j].astype(BF16), ffn_w2[j].astype(BF16),
                      ln2_g[li][None], ln2_b[li][None], lc)
        else:
            wr = jnp.pad(moe_router[j], ((0, 0), (0, LANES - N_EXPERTS)))
            xx = _moe(xx, mod, wr, moe_w1[j].astype(BF16), moe_w3[j].astype(BF16), moe_w2[j].astype(BF16),
                      ln2_g[li][None], ln2_b[li][None], lc)
    return xx[:, lc:]
```

```python
import functools
import math

import jax
import jax.numpy as jnp
from jax import lax
from jax.experimental import pallas as pl
from jax.experimental.pallas import tpu as pltpu

F32 = jnp.float32
BF16 = jnp.bfloat16
HI = lax.Precision.HIGHEST

D_MODEL = 1024
DEPTH = 2
GRID_W = 64
HEAD_DIM = 64
HALF_HD = HEAD_DIM // 2
ATT_WIDTH = 512
RWKV_WIDTH = 256
HY_WIDTH = 256
ATT_HEADS = 8
ATT_KV_HEADS = 2
ATT_REP = 4
ATT_KV_WIDTH = 128
ROPE_THETA = 10000.0
QK_EPS = 1e-6
RWKV_HEADS = 4
W_LORA = 16
A_LORA = 16
G_LORA = 32
RWKV_GN_EPS = 64e-5
HY_ORDER = 2
HY_EMB = 33
HY_FFN = 64
N_EXPERTS = 8
LN_EPS = 1e-6
IN_ATT = ATT_WIDTH + 2 * ATT_KV_WIDTH
IN_RWKV = 3 * RWKV_WIDTH + 2 * W_LORA + 2 * A_LORA + G_LORA
IN_RWKV_PAD = 896
IN_HY = 3 * HY_WIDTH
ALPHA = float((2 * DEPTH) ** 0.25)
LOG2E = 1.4426950408889634
MAX_UNSHIFTED_SCORE = 40.0

LANES = 128
SUBLANES = 8
TOK_TILE = 256
WKV_CHUNK = 64
WKV_TILE = 256
DFT_N2 = 256
VMEM_LIMIT = 48 * 1024 * 1024


def _cparams(sem):
    return pltpu.CompilerParams(dimension_semantics=sem, vmem_limit_bytes=VMEM_LIMIT)


def _pick_tile(n, cands):
    for c in cands:
        if n % c == 0:
            return c
    raise ValueError(f"no tile for {n} in {cands}")


def _layer_norm(x):
    mu = jnp.mean(x, -1, keepdims=True)
    xc = x - mu
    var = jnp.mean(xc * xc, -1, keepdims=True)
    return xc * lax.rsqrt(var + LN_EPS)


def _sel_mod(mod_ref, j, row0, tm, lc):
    rid = row0 + lax.broadcasted_iota(jnp.int32, (tm, 1), 0)
    return jnp.where(rid < lc, mod_ref[0, 0, j:j + 1, :], mod_ref[0, 1, j:j + 1, :])


def _silu(x):
    return x * jax.nn.sigmoid(x)


def _ada_kernel(c_ref, w_ref, b_ref, o_ref):
    s = _silu(c_ref[...])
    o_ref[0] = jnp.dot(s, w_ref[0], precision=HI, preferred_element_type=F32) + b_ref[0]


def _ada_mod(cond8, ada_w, ada_b):
    depth, d, n = ada_w.shape
    tn = _pick_tile(n, (1536, 1024, 512, 256, 128))
    return pl.pallas_call(
        _ada_kernel,
        grid=(depth, n // tn),
        in_specs=[pl.BlockSpec((SUBLANES, d), lambda l, j: (0, 0)),
                  pl.BlockSpec((1, d, tn), lambda l, j: (l, 0, j)),
                  pl.BlockSpec((1, 1, tn), lambda l, j: (l, 0, j))],
        out_specs=pl.BlockSpec((1, SUBLANES, tn), lambda l, j: (l, 0, j)),
        out_shape=jax.ShapeDtypeStruct((depth, SUBLANES, n), F32),
        compiler_params=_cparams(("parallel", "parallel")),
        name="ada_mod",
    )(cond8, ada_w, ada_b.reshape(depth, 1, n))


def _inproj_kernel(x_ref, mod_ref, w_ref, oa_ref, or_ref, oh_ref, *, lc, tm):
    row0 = pl.program_id(1) * tm
    sh = _sel_mod(mod_ref, 0, row0, tm, lc)
    sc = _sel_mod(mod_ref, 1, row0, tm, lc)
    h = (_layer_norm(x_ref[0]) * (1.0 + sc) + sh).astype(BF16)
    u = jnp.dot(h, w_ref[...], preferred_element_type=F32)
    oa_ref[0] = u[:, :IN_ATT]
    or_ref[0] = u[:, IN_ATT:IN_ATT + IN_RWKV_PAD]
    oh_ref[0] = u[:, IN_ATT + IN_RWKV_PAD:]


def _inproj(xx, mod, w_pad, lc):
    b, lt, d = xx.shape
    tm = TOK_TILE
    n = w_pad.shape[1]
    return pl.pallas_call(
        functools.partial(_inproj_kernel, lc=lc, tm=tm),
        grid=(b, lt // tm),
        in_specs=[pl.BlockSpec((1, tm, d), lambda bi, i: (bi, i, 0)),
                  pl.BlockSpec((1, 2, 6, d), lambda bi, i: (bi, 0, 0, 0)),
                  pl.BlockSpec((d, n), lambda bi, i: (0, 0))],
        out_specs=[pl.BlockSpec((1, tm, IN_ATT), lambda bi, i: (bi, i, 0)),
                   pl.BlockSpec((1, tm, IN_RWKV_PAD), lambda bi, i: (bi, i, 0)),
                   pl.BlockSpec((1, tm, IN_HY), lambda bi, i: (bi, i, 0))],
        out_shape=[jax.ShapeDtypeStruct((b, lt, IN_ATT), F32),
                   jax.ShapeDtypeStruct((b, lt, IN_RWKV_PAD), F32),
                   jax.ShapeDtypeStruct((b, lt, IN_HY), F32)],
        compiler_params=_cparams(("parallel", "parallel")),
        name="inproj",
    )(xx, mod, w_pad)


def _attn_prep_kernel(u_ref, cos_ref, sin_ref, qg_ref, kg_ref, qt_ref, k_ref, vt_ref):
    u = u_ref[0]
    cos = cos_ref[...]
    sin = sin_ref[...]

    def norm_rope(xh, g):
        ms = jnp.mean(xh * xh, -1, keepdims=True)
        xn = xh * lax.rsqrt(ms + QK_EPS) * g
        sw = jnp.concatenate([xn[:, HALF_HD:], xn[:, :HALF_HD]], -1)
        return xn * cos + sw * sin

    qg = qg_ref[...]
    kg = kg_ref[...]
    tm = u.shape[0]
    qs = [norm_rope(u[:, h * HEAD_DIM:(h + 1) * HEAD_DIM], qg) * (LOG2E * HEAD_DIM ** -0.5)
          for h in range(ATT_HEADS)]
    qt = jnp.concatenate(qs, -1).T
    for g in range(ATT_KV_HEADS):
        base = g * ATT_REP * HEAD_DIM
        qt_ref[0, g, 0] = jnp.concatenate(
            [qt[base + r * HEAD_DIM:base + (r + 1) * HEAD_DIM] for r in range(ATT_REP)], -1).astype(BF16)
        k0 = ATT_WIDTH + g * HEAD_DIM
        k_ref[0, g] = norm_rope(u[:, k0:k0 + HEAD_DIM], kg).astype(BF16)
    v0 = ATT_WIDTH + ATT_KV_WIDTH
    vt = u[:, v0:v0 + ATT_KV_WIDTH].T
    for g in range(ATT_KV_HEADS):
        vt_ref[0, g] = vt[g * HEAD_DIM:(g + 1) * HEAD_DIM].astype(BF16)


def _attn_prep(ua, cos, sin, qg, kg, lc):
    b, lt, _ = ua.shape
    tm = TOK_TILE
    nct = lc // tm
    nl = lt // tm - nct
    q_pos = lambda i: jnp.where(i < nct, nl + i, i - nct)
    return pl.pallas_call(
        _attn_prep_kernel,
        grid=(b, lt // tm),
        in_specs=[pl.BlockSpec((1, tm, IN_ATT), lambda bi, i: (bi, i, 0)),
                  pl.BlockSpec((tm, HEAD_DIM), lambda bi, i: (i, 0)),
                  pl.BlockSpec((tm, HEAD_DIM), lambda bi, i: (i, 0)),
                  pl.BlockSpec((1, HEAD_DIM), lambda bi, i: (0, 0)),
                  pl.BlockSpec((1, HEAD_DIM), lambda bi, i: (0, 0))],
        out_specs=[pl.BlockSpec((1, ATT_KV_HEADS, 1, HEAD_DIM, ATT_REP * tm),
                                lambda bi, i: (bi, 0, q_pos(i), 0, 0)),
                   pl.BlockSpec((1, ATT_KV_HEADS, tm, HEAD_DIM), lambda bi, i: (bi, 0, i, 0)),
                   pl.BlockSpec((1, ATT_KV_HEADS, HEAD_DIM, tm), lambda bi, i: (bi, 0, 0, i))],
        out_shape=[jax.ShapeDtypeStruct((b, ATT_KV_HEADS, lt // tm, HEAD_DIM, ATT_REP * tm), BF16),
                   jax.ShapeDtypeStruct((b, ATT_KV_HEADS, lt, HEAD_DIM), BF16),
                   jax.ShapeDtypeStruct((b, ATT_KV_HEADS, HEAD_DIM, lt), BF16)],
        compiler_params=_cparams(("parallel", "parallel")),
        name="attn_prep",
    )(ua, cos, sin, qg, kg)


def _flash_kernel(qt_ref, k_ref, vt_ref, o_ref, m_scr, l_scr, acc_scr, *, nk, tq, sub, qn):
    j = pl.program_id(3)

    @pl.when(j == 0)
    def _():
        m_scr[...] = jnp.full(m_scr.shape, -jnp.inf, F32)
        l_scr[...] = jnp.zeros(l_scr.shape, F32)
        acc_scr[...] = jnp.zeros(acc_scr.shape, F32)

    qt = jnp.concatenate([qt_ref[0, 0, t] for t in range(qn)], -1)
    nsub = k_ref.shape[2] // sub
    m = m_scr[...]
    l = l_scr[...]
    acc = acc_scr[...]
    scores = lambda c: jnp.dot(k_ref[0, 0, c * sub:(c + 1) * sub, :], qt, preferred_element_type=F32)
    pv = lambda c, p: jnp.dot(vt_ref[0, 0, :, c * sub:(c + 1) * sub], p, preferred_element_type=F32)
    s_next = scores(0)
    pend = None
    for c in range(nsub):
        s = s_next
        if c + 1 < nsub:
            s_next = scores(c + 1)
        if pend is not None:
            acc = pend[0] * acc + pv(c - 1, pend[1])
        m_new = jnp.maximum(m, jnp.max(s, 0, keepdims=True))
        a = jnp.exp2(m - m_new)
        p = jnp.exp2(s - m_new)
        l = a * l + jnp.sum(p, 0, keepdims=True)
        pend = (a, p.astype(BF16))
        m = m_new
    acc = pend[0] * acc + pv(nsub - 1, pend[1])
    m_scr[...] = m
    l_scr[...] = l
    acc_scr[...] = acc

    @pl.when(j == nk - 1)
    def _():
        o = (acc / l).T
        for t in range(qn):
            o_ref[0, t * tq:(t + 1) * tq, :] = jnp.concatenate(
                [o[(t * ATT_REP + r) * tq:(t * ATT_REP + r + 1) * tq] for r in range(ATT_REP)], -1
            ).astype(o_ref.dtype)


def _flash_bounded_kernel(qt_ref, k_ref, vt_ref, o_ref, l_scr, acc_scr, *, nk, tq, sub, qn):
    j = pl.program_id(3)

    @pl.when(j == 0)
    def _():
        l_scr[...] = jnp.zeros(l_scr.shape, F32)
        acc_scr[...] = jnp.zeros(acc_scr.shape, F32)

    qt = jnp.concatenate([qt_ref[0, 0, t] for t in range(qn)], -1)
    nsub = k_ref.shape[2] // sub
    l = l_scr[...]
    acc = acc_scr[...]
    scores = lambda c: jnp.dot(k_ref[0, 0, c * sub:(c + 1) * sub, :], qt, preferred_element_type=F32)
    s_next = scores(0)
    for c in range(nsub):
        s = s_next
        if c + 1 < nsub:
            s_next = scores(c + 1)
        p = jnp.exp2(s)
        l = l + jnp.sum(p.reshape(sub // SUBLANES, SUBLANES, p.shape[1]), 0)
        acc = acc + jnp.dot(vt_ref[0, 0, :, c * sub:(c + 1) * sub], p.astype(BF16), preferred_element_type=F32)
    l_scr[...] = l
    acc_scr[...] = acc

    @pl.when(j == nk - 1)
    def _():
        o = (acc / jnp.sum(l, 0, keepdims=True)).T
        for t in range(qn):
            o_ref[0, t * tq:(t + 1) * tq, :] = jnp.concatenate(
                [o[(t * ATT_REP + r) * tq:(t * ATT_REP + r + 1) * tq] for r in range(ATT_REP)], -1
            ).astype(o_ref.dtype)


def _flash(qt, k, vt, tq, tk, qn, q_blk0, nq, nk, bounded=False):
    b = qt.shape[0]
    lq = nq * qn * tq
    sub = _pick_tile(tk, (256, 128))
    lanes = qn * ATT_REP * tq
    if bounded:
        body = functools.partial(_flash_bounded_kernel, nk=nk, tq=tq, sub=sub, qn=qn)
        scratch = [pltpu.VMEM((SUBLANES, lanes), F32), pltpu.VMEM((HEAD_DIM, lanes), F32)]
    else:
        body = functools.partial(_flash_kernel, nk=nk, tq=tq, sub=sub, qn=qn)
        scratch = [pltpu.VMEM((1, lanes), F32), pltpu.VMEM((1, lanes), F32), pltpu.VMEM((HEAD_DIM, lanes), F32)]
    return pl.pallas_call(
        body,
        grid=(b, ATT_KV_HEADS, nq, nk),
        in_specs=[pl.BlockSpec((1, 1, qn, HEAD_DIM, ATT_REP * tq), lambda bi, g, i, j: (bi, g, i + q_blk0, 0, 0)),
                  pl.BlockSpec((1, 1, tk, HEAD_DIM), lambda bi, g, i, j: (bi, g, j, 0)),
                  pl.BlockSpec((1, 1, HEAD_DIM, tk), lambda bi, g, i, j: (bi, g, 0, j))],
        out_specs=pl.BlockSpec((1, qn * tq, ATT_REP * HEAD_DIM), lambda bi, g, i, j: (bi, i, g)),
        out_shape=jax.ShapeDtypeStruct((b, lq, ATT_WIDTH), BF16),
        scratch_shapes=scratch,
        compiler_params=_cparams(("parallel", "parallel", "parallel", "arbitrary")),
        name="flash_bounded" if bounded else "flash",
    )(qt, k, vt)


def _prev_next(u, up8, un8, i, tm, lc, lt):
    start = i * tm
    p_ok = jnp.logical_and(start != 0, start != lc)
    n_ok = jnp.logical_and(start + tm != lc, start + tm != lt)
    prow = jnp.where(p_ok, up8[SUBLANES - 1:SUBLANES], 0.0)
    nrow = jnp.where(n_ok, un8[0:1], 0.0)
    rid = lax.broadcasted_iota(jnp.int32, u.shape, 0)
    prev = jnp.where(rid == 0, prow, pltpu.roll(u, 1, 0))
    nxt = jnp.where(rid == tm - 1, nrow, pltpu.roll(u, tm - 1, 0))
    return prev, nxt


def _halo_specs(tm, c, lt):
    r = tm // SUBLANES
    last = lt // SUBLANES - 1
    return [pl.BlockSpec((1, tm, c), lambda bi, i: (bi, i, 0)),
            pl.BlockSpec((1, SUBLANES, c), lambda bi, i: (bi, jnp.maximum(i * r - 1, 0), 0)),
            pl.BlockSpec((1, SUBLANES, c), lambda bi, i: (bi, jnp.minimum((i + 1) * r, last), 0))]


def _softplus(z):
    return jnp.maximum(z, 0.0) + jnp.log1p(jnp.exp(-jnp.abs(z)))


def _rwkv_prep_kernel(u_ref, up_ref, un_ref, mu_ref, wl_ref, bd_ref, w0_ref, a0_ref, kkw_ref, ka_ref, rk_ref,
                      r_o, v_o, kk_o, g_o, bon_o, lw_o, be_o, kd_o, *, lc, lt, tm):
    i = pl.program_id(1)
    u = u_ref[0]
    prev, nxt = _prev_next(u, up_ref[0], un_ref[0], i, tm, lc, lt)
    us = u + mu_ref[0:1] * (prev - u) + mu_ref[1:2] * (nxt - u)
    c = RWKV_WIDTH
    r = us[:, 0:c]
    k = us[:, c:2 * c]
    v = us[:, 2 * c:3 * c]
    slab = us[:, 3 * c:3 * c + LANES]
    lane = lax.broadcasted_iota(jnp.int32, slab.shape, 1)
    o_a = 2 * W_LORA
    o_g = o_a + 2 * A_LORA
    act = jnp.where(lane < o_a, jnp.tanh(slab),
                    jnp.where(lane < o_g, slab,
                              jnp.where(lane < o_g + G_LORA, jax.nn.sigmoid(slab), 0.0)))
    lo = jnp.dot(act, wl_ref[...], precision=HI, preferred_element_type=F32)
    bd = bd_ref[...]
    kk0 = k * kkw_ref[...]
    kk = kk0 * lax.rsqrt(jnp.dot(kk0 * kk0, bd, precision=HI, preferred_element_type=F32) + 1e-12)
    r_o[0] = r
    v_o[0] = v
    kk_o[0] = kk
    g_o[0] = lo[:, 4 * c:5 * c]
    bon = jnp.zeros_like(r)
    for d in range(2):
        w_raw = w0_ref[d:d + 1] + lo[:, d * c:(d + 1) * c]
        lw = -jnp.exp(-_softplus(-w_raw) - 0.5)
        a = jax.nn.sigmoid(a0_ref[d:d + 1] + lo[:, (2 + d) * c:(3 + d) * c])
        kd = k * (1.0 + (a - 1.0) * ka_ref[...])
        lw_o[0, d] = lw
        be_o[0, d] = a * kk
        kd_o[0, d] = kd
        bon = bon + r * kd * rk_ref[...]
    bon_o[0] = jnp.dot(bon, bd, precision=HI, preferred_element_type=F32) * v


def _rwkv_prep(ur, mu, wl, bd, w0, a0, kkw, ka, rk, lc):
    b, lt, cp = ur.shape
    tm = TOK_TILE
    c = RWKV_WIDTH
    full = lambda shape: pl.BlockSpec(shape, lambda bi, i: (0,) * len(shape))
    tok = pl.BlockSpec((1, tm, c), lambda bi, i: (bi, i, 0))
    tok2 = pl.BlockSpec((1, 2, tm, c), lambda bi, i: (bi, 0, i, 0))
    s1 = jax.ShapeDtypeStruct((b, lt, c), F32)
    s2 = jax.ShapeDtypeStruct((b, 2, lt, c), F32)
    return pl.pallas_call(
        functools.partial(_rwkv_prep_kernel, lc=lc, lt=lt, tm=tm),
        grid=(b, lt // tm),
        in_specs=_halo_specs(tm, cp, lt) + [full((2, cp)), full((LANES, 5 * c)), full((c, c)), full((2, c)),
                                            full((2, c)), full((1, c)), full((1, c)), full((1, c))],
        out_specs=[tok, tok, tok, tok, tok, tok2, tok2, tok2],
        out_shape=[s1, s1, s1, s1, s1, s2, s2, s2],
        compiler_params=_cparams(("parallel", "parallel")),
        name="rwkv_prep",
    )(ur, ur, ur, mu, wl, bd, w0, a0, kkw, ka, rk)


def _mm(a, b):
    return jnp.dot(a, b, precision=HI, preferred_element_type=F32)


_NN = ((1,), (0,))
_NT = ((1,), (1,))
_TN = ((0,), (0,))


def _split2(a):
    hi = a.astype(BF16)
    return hi, (a - hi.astype(F32)).astype(BF16)


def _dotp(a, b, passes, dims=_NN):
    if a.ndim == 3:
        dn = (((dims[0][0] + 1,), (dims[1][0] + 1,)), ((0,), (0,)))
    else:
        dn = (dims, ((), ()))
    dg = lambda p, q: lax.dot_general(p, q, dn, preferred_element_type=F32)
    if passes == 1:
        return dg(a.astype(BF16), b.astype(BF16))
    ah, al = _split2(a)
    bh, bl = _split2(b)
    return dg(ah, bh) + dg(ah, bl) + dg(al, bh)


P_M = 1
P_INV = 1
P_W = 1
P_Z = 1
P_STATE = 3
P_DFT = 1


def _unit_tri_inv(a_mat, row, col, eye, passes):
    t = a_mat.shape[-1]
    eye_f = eye.astype(F32)
    base = SUBLANES
    same = (row // base) == (col // base)
    n1 = -jnp.where(same, a_mat, 0.0)
    n2 = _dotp(n1, n1, passes)
    n4 = _dotp(n2, n2, passes)
    x = _dotp(_dotp(eye_f + n1, eye_f + n2, passes), eye_f + n4, passes)
    m = base
    while m < t:
        off = jnp.logical_and((row // (2 * m)) == (col // (2 * m)), (row // m) != (col // m))
        x = x - _dotp(x, _dotp(jnp.where(off, a_mat, 0.0), x, passes), passes)
        m *= 2
    return x


def _wkv_kernel(r_ref, v_ref, kk_ref, lw_ref, be_ref, kd_ref, y_ref, h_scr, pq_scr, ry_scr, *, fwd):
    t = WKV_CHUNK
    n = HEAD_DIM
    g = WKV_TILE // WKV_CHUNK
    tt = WKV_TILE
    order = list(range(g)) if fwd else list(range(g - 1, -1, -1))

    @pl.when(pl.program_id(1) == 0)
    def _():
        h_scr[...] = jnp.zeros(h_scr.shape, F32)
        pq_scr[...] = jnp.zeros(pq_scr.shape, F32)
        ry_scr[...] = jnp.zeros(ry_scr.shape, F32)

    nh = RWKV_HEADS
    hm = h_scr[...]
    for p in range(g):
        ry = ry_scr[p * nh:(p + 1) * nh]
        pq = pq_scr[p * nh:(p + 1) * nh]
        y = _dotp(ry[:, :, :n], hm, P_STATE) + ry[:, :, n:]
        hm = _dotp(pq[:, :, :n], hm, P_STATE) + pq[:, :, n:]
        c = order[p]
        y_ref[0, c * t:(c + 1) * t, :] = jnp.concatenate([y[h] for h in range(nh)], -1)
    h_scr[...] = hm

    row = lax.broadcasted_iota(jnp.int32, (tt, tt), 0)
    col = lax.broadcasted_iota(jnp.int32, (tt, tt), 1)
    same = (row // t) == (col // t)
    tri = jnp.logical_and(same, (row >= col) if fwd else (row <= col))
    sums = jnp.concatenate([jnp.where(tri, 1.0, 0.0), jnp.where(same, 1.0, 0.0)], 0).astype(BF16)
    lw = lw_ref[0, 0]
    l1 = lw.astype(BF16)
    rem = lw - l1.astype(F32)
    l2 = rem.astype(BF16)
    l3 = (rem - l2.astype(F32)).astype(BF16)
    cc = (jnp.dot(sums, l1, preferred_element_type=F32) + jnp.dot(sums, l2, preferred_element_type=F32)
          + jnp.dot(sums, l3, preferred_element_type=F32))
    cum = cc[:tt]
    ctot = cc[tt:]
    e_neg = jnp.exp(-cum)
    e_end = jnp.exp(ctot - cum)
    g_end = jnp.exp(ctot)
    v = v_ref[0]
    be = be_ref[0, 0]
    kd = kd_ref[0, 0]
    kap_t = kk_ref[0] * jnp.exp(cum - lw)
    r_t = r_ref[0] * jnp.exp(cum)
    be_t = be * e_neg
    kd_t = kd * e_neg
    be_h = be * e_end
    kd_h = kd * e_end
    r64 = lax.broadcasted_iota(jnp.int32, (t, t), 0)
    c64 = lax.broadcasted_iota(jnp.int32, (t, t), 1)
    eye = r64 == c64
    strict = (r64 > c64) if fwd else (r64 < c64)
    incl = jnp.logical_or(strict, eye)

    def units(x, rows=t):
        return jnp.stack([x[order[p] * t:order[p] * t + rows, h * n:(h + 1) * n]
                          for p in range(g) for h in range(nh)], 0)

    kap = units(kap_t)
    rt = units(r_t)
    vh = units(v)
    nu = g * nh
    m = _dotp(jnp.concatenate([kap, rt], 1), jnp.concatenate([units(be_t), units(kd_t)], 1), P_M, _NT)
    a_mat = jnp.where(strict, m[:, :t, :t], 0.0)
    b_mat = jnp.where(strict, m[:, :t, t:], 0.0)
    ab_r = jnp.concatenate([jnp.where(incl, m[:, t:, :t], 0.0), jnp.where(incl, m[:, t:, t:], 0.0)], 2)
    tinv = _unit_tri_inv(a_mat, r64, c64, eye, P_INV)
    w = _dotp(tinv, jnp.concatenate([kap, _dotp(b_mat, vh, P_W)], 2), P_W)
    z = jnp.concatenate([-w, jnp.concatenate([jnp.zeros((nu, t, n), F32), vh], 2)], 1)
    ry = _dotp(ab_r, z, P_Z)
    ry_scr[:, :, :n] = ry[:, :, :n] + rt
    ry_scr[:, :, n:] = ry[:, :, n:]
    pq = _dotp(jnp.concatenate([units(be_h), units(kd_h)], 1), z, P_Z, _TN)
    gd = jnp.where(eye, jnp.broadcast_to(units(g_end, 1), (nu, n, n)), 0.0)
    pq_scr[:, :, :n] = pq[:, :, :n] + gd
    pq_scr[:, :, n:] = pq[:, :, n:]


def _wkv_scan(r, v, kk, lw, be, kd, lc, fwd):
    b, lt, c = r.shape
    tt = WKV_TILE
    nt = lt // tt
    ntc = lc // tt
    d = 0 if fwd else 1

    def tile(i):
        if fwd:
            return i
        return jnp.where(i < ntc, ntc - 1 - i, nt - 1 - (i - ntc))

    t_in = lambda i: tile(jnp.minimum(i, nt - 1))
    t_out = lambda i: tile(jnp.maximum(i - 1, 0))
    one = pl.BlockSpec((1, tt, c), lambda bi, i: (bi, t_in(i), 0))
    two = pl.BlockSpec((1, 1, tt, c), lambda bi, i: (bi, d, t_in(i), 0))
    g = tt // WKV_CHUNK
    return pl.pallas_call(
        functools.partial(_wkv_kernel, fwd=fwd),
        grid=(b, nt + 1),
        in_specs=[one, one, one, two, two, two],
        out_specs=pl.BlockSpec((1, tt, c), lambda bi, i: (bi, t_out(i), 0)),
        out_shape=jax.ShapeDtypeStruct((b, lt, c), F32),
        scratch_shapes=[pltpu.VMEM((RWKV_HEADS, HEAD_DIM, HEAD_DIM), F32),
                        pltpu.VMEM((g * RWKV_HEADS, HEAD_DIM, 2 * HEAD_DIM), F32),
                        pltpu.VMEM((g * RWKV_HEADS, WKV_CHUNK, 2 * HEAD_DIM), F32)],
        compiler_params=_cparams(("parallel", "arbitrary")),
        name="wkv_fwd" if fwd else "wkv_bwd",
    )(r, v, kk, lw, be, kd)


def _rwkv_out_kernel(yf_ref, yb_ref, bon_ref, g_ref, bd_ref, gg_ref, gb_ref, o_ref):
    y = yf_ref[0] + yb_ref[0] + bon_ref[0]
    bd = bd_ref[...]
    mu = _mm(y, bd) * (1.0 / HEAD_DIM)
    yc = y - mu
    var = _mm(yc * yc, bd) * (1.0 / HEAD_DIM)
    yn = yc * lax.rsqrt(var + RWKV_GN_EPS) * gg_ref[...] + gb_ref[...]
    o_ref[0] = (yn * g_ref[0]).astype(o_ref.dtype)


def _rwkv_out(yf, yb, bon, g, bd, gg, gb):
    b, lt, c = yf.shape
    tm = TOK_TILE
    tok = pl.BlockSpec((1, tm, c), lambda bi, i: (bi, i, 0))
    full = lambda shape: pl.BlockSpec(shape, lambda bi, i: (0,) * len(shape))
    return pl.pallas_call(
        _rwkv_out_kernel,
        grid=(b, lt // tm),
        in_specs=[tok, tok, tok, tok, full((c, c)), full((1, c)), full((1, c))],
        out_specs=tok,
        out_shape=jax.ShapeDtypeStruct((b, lt, c), BF16),
        compiler_params=_cparams(("parallel", "parallel")),
        name="rwkv_out",
    )(yf, yb, bon, g, bd, gg, gb)


def _hy_prep_kernel(u_ref, up_ref, un_ref, w_ref, b_ref, x1_o, x2_o, v_o, *, lc, lt, tm):
    i = pl.program_id(1)
    u = u_ref[0]
    prev, nxt = _prev_next(u, up_ref[0], un_ref[0], i, tm, lc, lt)
    y = prev * w_ref[0:1] + u * w_ref[1:2] + nxt * w_ref[2:3] + b_ref[...]
    c = HY_WIDTH
    x1_o[0] = y[:, :c]
    x2_o[0] = y[:, c:2 * c]
    v_o[0] = y[:, 2 * c:]


def _hy_prep(uh, w, bias, lc):
    b, lt, cin = uh.shape
    tm = TOK_TILE
    c = HY_WIDTH
    full = lambda shape: pl.BlockSpec(shape, lambda bi, i: (0,) * len(shape))
    tok = pl.BlockSpec((1, tm, c), lambda bi, i: (bi, i, 0))
    s1 = jax.ShapeDtypeStruct((b, lt, c), F32)
    return pl.pallas_call(
        functools.partial(_hy_prep_kernel, lc=lc, lt=lt, tm=tm),
        grid=(b, lt // tm),
        in_specs=_halo_specs(tm, cin, lt) + [full((3, cin)), full((1, cin))],
        out_specs=[tok, tok, tok],
        out_shape=[s1, s1, s1],
        compiler_params=_cparams(("parallel", "parallel")),
        name="hy_prep",
    )(uh, uh, uh, w, bias)


def _hy_filter_kernel(z_ref, w1_ref, b1_ref, f1_ref, w2_ref, b2_ref, f2_ref, w3_ref, dec_ref,
                      fw_o, bw_o, nrm_o, *, tl):
    i = pl.program_id(0)
    z = z_ref[...]
    h = jnp.sin(f1_ref[...] * (_mm(z, w1_ref[...]) + b1_ref[...]))
    h = jnp.sin(f2_ref[...] * (_mm(h, w2_ref[...]) + b2_ref[...]))
    h = _mm(h, w3_ref[...]) * jnp.exp(-z[:, 0:1] * dec_ref[...])
    c = HY_WIDTH
    fw = jnp.concatenate([h[:, 0:c], h[:, 2 * c:3 * c]], 1)
    bw = jnp.concatenate([h[:, c:2 * c], h[:, 3 * c:4 * c]], 1)
    rid = i * tl + lax.broadcasted_iota(jnp.int32, (tl, 1), 0)
    bw = jnp.where(rid == 0, 0.0, bw)
    fw_o[...] = fw
    bw_o[...] = bw

    @pl.when(i == 0)
    def _():
        nrm_o[...] = jnp.zeros(nrm_o.shape, F32)

    nrm_o[...] += jnp.sum(jnp.abs(fw) + jnp.abs(bw), 0, keepdims=True)


def _hy_filter(feat, w1p, b1, f1, w2, b2, f2, w3, dec):
    l, fe = feat.shape
    tl = _pick_tile(l, (512, 256))
    c2 = HY_ORDER * HY_WIDTH
    full = lambda shape: pl.BlockSpec(shape, lambda i: (0,) * len(shape))
    return pl.pallas_call(
        functools.partial(_hy_filter_kernel, tl=tl),
        grid=(l // tl,),
        in_specs=[pl.BlockSpec((tl, fe), lambda i: (i, 0)), full(w1p.shape), full(b1.shape), full(f1.shape),
                  full(w2.shape), full(b2.shape), full(f2.shape), full(w3.shape), full(dec.shape)],
        out_specs=[pl.BlockSpec((tl, c2), lambda i: (i, 0)), pl.BlockSpec((tl, c2), lambda i: (i, 0)),
                   pl.BlockSpec((1, c2), lambda i: (0, 0))],
        out_shape=[jax.ShapeDtypeStruct((l, c2), F32), jax.ShapeDtypeStruct((l, c2), F32),
                   jax.ShapeDtypeStruct((1, c2), F32)],
        compiler_params=_cparams(("arbitrary",)),
        name="hy_filter",
    )(feat, w1p, b1, f1, w2, b2, f2, w3, dec)


def _dft_cols_kernel(f_ref, xa_ref, xb_ref, o_ref, *, n1, pair):
    f = f_ref[...]
    pa = _dotp(f, xa_ref[0], P_DFT)
    pb = _dotp(f, xb_ref[0], P_DFT)
    if pair:
        o_ref[0] = pa[:n1].astype(o_ref.dtype)
        o_ref[1] = pa[n1:].astype(o_ref.dtype)
        o_ref[2] = pb[:n1].astype(o_ref.dtype)
        o_ref[3] = pb[n1:].astype(o_ref.dtype)
    else:
        o_ref[0] = (pa[:n1] - pb[n1:]).astype(o_ref.dtype)
        o_ref[1] = (pb[:n1] + pa[n1:]).astype(o_ref.dtype)


def _dft_cols(fstack, xa, ia, xb, ib, pair):
    n1 = fstack.shape[0] // 2
    _, nh, cols = xa.shape
    tc = _pick_tile(cols, (4096, 2048, 1024, 512, 256, 128))
    no = 4 if pair else 2
    return pl.pallas_call(
        functools.partial(_dft_cols_kernel, n1=n1, pair=pair),
        grid=(cols // tc,),
        in_specs=[pl.BlockSpec(fstack.shape, lambda j: (0, 0)),
                  pl.BlockSpec((1, nh, tc), lambda j: (ia, 0, j)),
                  pl.BlockSpec((1, nh, tc), lambda j: (ib, 0, j))],
        out_specs=pl.BlockSpec((no, n1, tc), lambda j: (0, 0, j)),
        out_shape=jax.ShapeDtypeStruct((no, n1, cols), BF16),
        compiler_params=_cparams(("parallel",)),
        name="dft_cols",
    )(fstack, xa, xb)


def _cplx_left(gs, zr, zi, n):
    c = zr.shape[1]
    p = _dotp(gs, jnp.concatenate([zr, zi], 1), P_DFT)
    return p[:n, :c] - p[n:, c:], p[:n, c:] + p[n:, :c]


def _spec_kernel(a_ref, g_ref, nrm_ref, o_ref, *, n_total):
    n2 = DFT_N2
    gs = jnp.concatenate([g_ref[0, 0], g_ref[0, 1]], 0)
    fr, fi = _cplx_left(gs, a_ref[0, 0], a_ref[1, 0], n2)
    br, bi = _cplx_left(gs, a_ref[2, 0], a_ref[3, 0], n2)
    s = 1.0 / (nrm_ref[...] * n_total)
    o_ref[0, 0] = (fr + br) * s
    o_ref[1, 0] = (fi - bi) * s


def _spec(a4, g, nrm, n_total):
    _, n1, n2, c2 = a4.shape
    return pl.pallas_call(
        functools.partial(_spec_kernel, n_total=float(n_total)),
        grid=(n1,),
        in_specs=[pl.BlockSpec((4, 1, n2, c2), lambda k: (0, k, 0, 0)),
                  pl.BlockSpec((1, 2, n2, n2), lambda k: (k, 0, 0, 0)),
                  pl.BlockSpec((1, c2), lambda k: (0, 0))],
        out_specs=pl.BlockSpec((2, 1, n2, c2), lambda k: (0, k, 0, 0)),
        out_shape=jax.ShapeDtypeStruct((2, n1, n2, c2), F32),
        compiler_params=_cparams(("parallel",)),
        name="hy_spec",
    )(a4, g, nrm)


def _conv_mid_kernel(a_ref, g_ref, gh_ref, k_ref, o_ref):
    n2 = DFT_N2
    gs = jnp.concatenate([g_ref[0, 0], g_ref[0, 1]], 0)
    xr, xi = _cplx_left(gs, a_ref[0, 0], a_ref[1, 0], n2)
    kr = k_ref[0, 0]
    ki = k_ref[1, 0]
    zr = xr * kr - xi * ki
    zi = xr * ki + xi * kr
    ghs = jnp.concatenate([gh_ref[0, 0], gh_ref[0, 1]], 0)
    yr, yi = _cplx_left(ghs, zr, zi, n2)
    o_ref[0, 0] = yr.astype(o_ref.dtype)
    o_ref[1, 0] = yi.astype(o_ref.dtype)


def _conv_mid(a, g, gh, kspec, order):
    _, n1, n2, c = a.shape
    return pl.pallas_call(
        _conv_mid_kernel,
        grid=(n1,),
        in_specs=[pl.BlockSpec((2, 1, n2, c), lambda k: (0, k, 0, 0)),
                  pl.BlockSpec((1, 2, n2, n2), lambda k: (k, 0, 0, 0)),
                  pl.BlockSpec((1, 2, n2, n2), lambda k: (k, 0, 0, 0)),
                  pl.BlockSpec((2, 1, n2, c), lambda k: (0, k, 0, order))],
        out_specs=pl.BlockSpec((2, 1, n2, c), lambda k: (0, k, 0, 0)),
        out_shape=jax.ShapeDtypeStruct((2, n1, n2, c), BF16),
        compiler_params=_cparams(("parallel",)),
        name="hy_conv_mid",
    )(a, g, gh, kspec)


def _idft_cols_kernel(c_ref, b_ref, g0_ref, g1_ref, x0_ref, x1_ref, bias_ref, o_ref, *, nh):
    cs = c_ref[...]
    pr = _dotp(cs, b_ref[0], P_DFT)
    pi = _dotp(cs, b_ref[1], P_DFT)
    yr = pr[:nh] - pi[nh:]
    yi = pi[:nh] + pr[nh:]
    bias = bias_ref[...]
    o_ref[0] = g0_ref[0] * (yr + x0_ref[0] * bias)
    o_ref[1] = g1_ref[0] * (yi + x1_ref[0] * bias)


def _idft_cols(cstack, bv, gate, xin, bias_cols):
    nh2, n1 = cstack.shape
    nh = nh2 // 2
    cols = bv.shape[-1]
    tc = _pick_tile(cols, (4096, 2048, 1024, 512, 256, 128))
    row = lambda bi: pl.BlockSpec((1, nh, tc), lambda j: (bi, 0, j))
    return pl.pallas_call(
        functools.partial(_idft_cols_kernel, nh=nh),
        grid=(cols // tc,),
        in_specs=[pl.BlockSpec((nh2, n1), lambda j: (0, 0)),
                  pl.BlockSpec((2, n1, tc), lambda j: (0, 0, j)),
                  row(0), row(1), row(0), row(1),
                  pl.BlockSpec((1, tc), lambda j: (0, j))],
        out_specs=pl.BlockSpec((2, nh, tc), lambda j: (0, 0, j)),
        out_shape=jax.ShapeDtypeStruct((2, nh, cols), F32),
        compiler_params=_cparams(("parallel",)),
        name="idft_cols",
    )(cstack, bv, gate, gate, xin, xin, bias_cols)


def _hy_ctx_kernel(x1_ref, x2_ref, v_ref, fw_ref, bw_ref, nrm_ref, bias_ref, ff_ref, ci_ref, o_ref, *, lc):
    n = 2 * lc
    c = HY_WIDTH
    ff = ff_ref[...]
    ci = ci_ref[...]
    pf = _mm(ff, fw_ref[...])
    pb = _mm(ff, bw_ref[...])
    s = 1.0 / (nrm_ref[...] * float(n))
    kr = (pf[:n] + pb[:n]) * s
    ki = (pf[n:] - pb[n:]) * s

    def conv(z0, z1, o):
        xr, xi = _cplx_left(ff, z0, z1, n)
        krr = kr[:, o * c:(o + 1) * c]
        kii = ki[:, o * c:(o + 1) * c]
        return _cplx_left(ci, xr * krr - xi * kii, xr * kii + xi * krr, lc)

    v0 = v_ref[0]
    v1 = v_ref[1]
    y0, y1 = conv(v0, v1, 0)
    z0 = x1_ref[0] * (y0 + v0 * bias_ref[0:1])
    z1 = x1_ref[1] * (y1 + v1 * bias_ref[0:1])
    y0, y1 = conv(z0, z1, 1)
    o_ref[0] = x2_ref[0] * (y0 + z0 * bias_ref[1:2])
    o_ref[1] = x2_ref[1] * (y1 + z1 * bias_ref[1:2])


def _hy_ctx(x1, x2, v, fw, bw, nrm, bias, ff, ci):
    b, lc, c = v.shape
    vm = pl.BlockSpec(memory_space=pltpu.VMEM)
    return pl.pallas_call(
        functools.partial(_hy_ctx_kernel, lc=lc),
        in_specs=[vm] * 9,
        out_specs=vm,
        out_shape=jax.ShapeDtypeStruct((b, lc, c), F32),
        compiler_params=pltpu.CompilerParams(vmem_limit_bytes=VMEM_LIMIT),
        name="hy_ctx",
    )(x1, x2, v, fw, bw, nrm, bias, ff, ci)


def _outproj_kernel(attc_ref, attl_ref, rw_ref, hyc_ref, hyl_ref, x_ref, mod_ref, w_ref, lg_ref, lb_ref, o_ref,
                    *, lc, tm):
    row0 = pl.program_id(1) * tm
    a0 = ATT_WIDTH
    a1 = ATT_WIDTH + RWKV_WIDTH
    is_ctx = row0 < lc
    att = jnp.where(is_ctx, attc_ref[0], attl_ref[0])
    hy = jnp.where(is_ctx, hyc_ref[0], hyl_ref[0])
    o = jnp.dot(att, w_ref[:a0], preferred_element_type=F32)
    o += jnp.dot(rw_ref[0], w_ref[a0:a1], preferred_element_type=F32)
    o += jnp.dot(hy.astype(BF16), w_ref[a1:], preferred_element_type=F32)
    g = _sel_mod(mod_ref, 2, row0, tm, lc)
    y = ALPHA * x_ref[0] + g * o
    o_ref[0] = _layer_norm(y) * lg_ref[...] + lb_ref[...]


def _outproj(att_c, att_l, rw, hy_c, hy_l, xx, mod, w, lg, lb, lc):
    b, lt, d = xx.shape
    tm = TOK_TILE
    nct = lc // tm
    tok = lambda c: pl.BlockSpec((1, tm, c), lambda bi, i: (bi, i, 0))
    ctx = lambda c: pl.BlockSpec((1, tm, c), lambda bi, i: (bi, jnp.minimum(i, nct - 1), 0))
    lat = lambda c: pl.BlockSpec((1, tm, c), lambda bi, i: (bi, jnp.maximum(i - nct, 0), 0))
    full = lambda shape: pl.BlockSpec(shape, lambda bi, i: (0,) * len(shape))
    return pl.pallas_call(
        functools.partial(_outproj_kernel, lc=lc, tm=tm),
        grid=(b, lt // tm),
        in_specs=[ctx(ATT_WIDTH), lat(ATT_WIDTH), tok(RWKV_WIDTH), ctx(HY_WIDTH), lat(HY_WIDTH), tok(d),
                  pl.BlockSpec((1, 2, 6, d), lambda bi, i: (bi, 0, 0, 0)),
                  full(w.shape), full((1, d)), full((1, d))],
        out_specs=tok(d),
        out_shape=jax.ShapeDtypeStruct((b, lt, d), F32),
        compiler_params=_cparams(("parallel", "parallel")),
        name="outproj",
    )(att_c, att_l, rw, hy_c, hy_l, xx, mod, w, lg, lb)


def _ffn_kernel(x_ref, mod_ref, w1_ref, w3_ref, w2_ref, lg_ref, lb_ref, o_ref, h_scr, acc_scr, *, lc, tm, nf):
    row0 = pl.program_id(1) * tm
    f = pl.program_id(2)

    @pl.when(f == 0)
    def _():
        sh = _sel_mod(mod_ref, 3, row0, tm, lc)
        sc = _sel_mod(mod_ref, 4, row0, tm, lc)
        h_scr[...] = (_layer_norm(x_ref[0]) * (1.0 + sc) + sh).astype(BF16)
        acc_scr[...] = jnp.zeros(acc_scr.shape, F32)

    h = h_scr[...]
    a = jnp.dot(h, w1_ref[...], preferred_element_type=F32)
    g = jnp.dot(h, w3_ref[...], preferred_element_type=F32)
    acc_scr[...] += jnp.dot((_silu(a) * g).astype(BF16), w2_ref[...], preferred_element_type=F32)

    @pl.when(f == nf - 1)
    def _():
        gate = _sel_mod(mod_ref, 5, row0, tm, lc)
        y = ALPHA * x_ref[0] + gate * acc_scr[...]
        o_ref[0] = _layer_norm(y) * lg_ref[...] + lb_ref[...]


def _ffn(xx, mod, w1, w3, w2, lg, lb, lc):
    b, lt, d = xx.shape
    ff = w1.shape[1]
    tm = _pick_tile(lt, (1280, 768, 512, 256))
    tf = _pick_tile(ff, (256, 128))
    nf = ff // tf
    return pl.pallas_call(
        functools.partial(_ffn_kernel, lc=lc, tm=tm, nf=nf),
        grid=(b, lt // tm, nf),
        in_specs=[pl.BlockSpec((1, tm, d), lambda bi, i, f: (bi, i, 0)),
                  pl.BlockSpec((1, 2, 6, d), lambda bi, i, f: (bi, 0, 0, 0)),
                  pl.BlockSpec((d, tf), lambda bi, i, f: (0, f)),
                  pl.BlockSpec((d, tf), lambda bi, i, f: (0, f)),
                  pl.BlockSpec((tf, d), lambda bi, i, f: (f, 0)),
                  pl.BlockSpec((1, d), lambda bi, i, f: (0, 0)),
                  pl.BlockSpec((1, d), lambda bi, i, f: (0, 0))],
        out_specs=pl.BlockSpec((1, tm, d), lambda bi, i, f: (bi, i, 0)),
        out_shape=jax.ShapeDtypeStruct((b, lt, d), F32),
        scratch_shapes=[pltpu.VMEM((tm, d), BF16), pltpu.VMEM((tm, d), F32)],
        compiler_params=_cparams(("parallel", "parallel", "arbitrary")),
        name="ffn",
    )(xx, mod, w1, w3, w2, lg, lb)


def _moe_kernel(x_ref, mod_ref, wr_ref, w1_ref, w3_ref, w2_ref, lg_ref, lb_ref, o_ref,
                h_scr, acc_scr, gate_scr, *, lc, tm, nf, ns):
    row0 = pl.program_id(1) * tm
    s = pl.program_id(2)

    @pl.when(s == 0)
    def _():
        sh = _sel_mod(mod_ref, 3, row0, tm, lc)
        sc = _sel_mod(mod_ref, 4, row0, tm, lc)
        h = _layer_norm(x_ref[0]) * (1.0 + sc) + sh
        h_scr[...] = h.astype(BF16)
        acc_scr[...] = jnp.zeros(acc_scr.shape, F32)
        logits = _mm(h, wr_ref[...])
        lane = lax.broadcasted_iota(jnp.int32, logits.shape, 1)
        neg = jnp.float32(-jnp.inf)
        lg = jnp.where(lane < N_EXPERTS, logits, neg)
        m1 = jnp.max(lg, -1, keepdims=True)
        i1 = jnp.min(jnp.where(lg == m1, lane, LANES), -1, keepdims=True)
        lg2 = jnp.where(lane == i1, neg, lg)
        m2 = jnp.max(lg2, -1, keepdims=True)
        i2 = jnp.min(jnp.where(lg2 == m2, lane, LANES), -1, keepdims=True)
        e2 = jnp.exp(m2 - m1)
        g1 = 1.0 / (1.0 + e2)
        g2 = e2 / (1.0 + e2)
        for e in range(N_EXPERTS):
            gate_scr[e] = jnp.where(i1 == e, g1, 0.0) + jnp.where(i2 == e, g2, 0.0)

    h = h_scr[...]
    a = jnp.dot(h, w1_ref[0], preferred_element_type=F32)
    g = jnp.dot(h, w3_ref[0], preferred_element_type=F32)
    act = _silu(a) * g * gate_scr[s // nf]
    acc_scr[...] += jnp.dot(act.astype(BF16), w2_ref[0], preferred_element_type=F32)

    @pl.when(s == ns - 1)
    def _():
        gate = _sel_mod(mod_ref, 5, row0, tm, lc)
        y = ALPHA * x_ref[0] + gate * acc_scr[...]
        o_ref[0] = _layer_norm(y) * lg_ref[...] + lb_ref[...]


def _moe(xx, mod, wr, w1, w3, w2, lg, lb, lc):
    b, lt, d = xx.shape
    ne, _, ff = w1.shape
    tm = _pick_tile(lt, (1280, 768, 512, 256))
    tf = _pick_tile(ff, (256, 128))
    nf = ff // tf
    ns = ne * nf
    return pl.pallas_call(
        functools.partial(_moe_kernel, lc=lc, tm=tm, nf=nf, ns=ns),
        grid=(b, lt // tm, ns),
        in_specs=[pl.BlockSpec((1, tm, d), lambda bi, i, s: (bi, i, 0)),
                  pl.BlockSpec((1, 2, 6, d), lambda bi, i, s: (bi, 0, 0, 0)),
                  pl.BlockSpec((d, LANES), lambda bi, i, s: (0, 0)),
                  pl.BlockSpec((1, d, tf), lambda bi, i, s: (s // nf, 0, s % nf)),
                  pl.BlockSpec((1, d, tf), lambda bi, i, s: (s // nf, 0, s % nf)),
                  pl.BlockSpec((1, tf, d), lambda bi, i, s: (s // nf, s % nf, 0)),
                  pl.BlockSpec((1, d), lambda bi, i, s: (0, 0)),
                  pl.BlockSpec((1, d), lambda bi, i, s: (0, 0))],
        out_specs=pl.BlockSpec((1, tm, d), lambda bi, i, s: (bi, i, 0)),
        out_shape=jax.ShapeDtypeStruct((b, lt, d), F32),
        scratch_shapes=[pltpu.VMEM((tm, d), BF16), pltpu.VMEM((tm, d), F32),
                        pltpu.VMEM((ne, tm, 1), F32)],
        compiler_params=_cparams(("parallel", "parallel", "arbitrary")),
        name="moe",
    )(xx, mod, wr, w1, w3, w2, lg, lb)


def _rope_tables(l, lc):
    rows = l // GRID_W
    row = jnp.repeat(jnp.arange(rows, dtype=F32), GRID_W)
    col = jnp.tile(jnp.arange(GRID_W, dtype=F32), rows)
    n_freq = HEAD_DIM // 4
    inv_freq = ROPE_THETA ** (-jnp.arange(n_freq, dtype=F32) / n_freq)
    ang = jnp.concatenate([row[:, None] * inv_freq, col[:, None] * inv_freq], -1)
    cos, sin = jnp.cos(ang), jnp.sin(ang)
    cos64 = jnp.concatenate([jnp.ones((lc, HEAD_DIM), F32), jnp.concatenate([cos, cos], -1)], 0)
    sin64 = jnp.concatenate([jnp.zeros((lc, HEAD_DIM), F32), jnp.concatenate([-sin, sin], -1)], 0)
    return cos64, sin64


def _hy_features(l):
    bands = (HY_EMB - 1) // 2
    t = jnp.linspace(0.0, 1.0, l, dtype=F32)[:, None]
    f = jnp.linspace(1e-4, bands - 1, bands, dtype=F32)[None, :]
    wt = 2.0 * math.pi * jnp.arange(l, dtype=F32)[:, None] / l
    z = jnp.concatenate([t, jnp.cos(f * wt), -jnp.sin(f * wt)], -1)
    return jnp.pad(z, ((0, 0), (0, LANES - HY_EMB)))


def _angle(idx, n):
    return (2.0 * math.pi / n) * (idx % n).astype(F32)


def _dft_tables(n1):
    nh = n1 // 2
    n2 = DFT_N2
    n = n1 * n2
    k1 = jnp.arange(n1, dtype=jnp.int32)
    a1 = _angle(k1[:, None] * jnp.arange(nh, dtype=jnp.int32)[None, :], n1)
    fstack = jnp.concatenate([jnp.cos(a1), -jnp.sin(a1)], 0)
    cstack = jnp.concatenate([jnp.cos(a1.T), jnp.sin(a1.T)], 0)
    k2 = jnp.arange(n2, dtype=jnp.int32)
    at = _angle(k1[:, None] * k2[None, :], n)
    tr, ti = jnp.cos(at), -jnp.sin(at)
    a2 = _angle(k2[:, None] * k2[None, :], n2)
    fr, fi = jnp.cos(a2), -jnp.sin(a2)
    g = jnp.stack([tr[:, None, :] * fr[None] - ti[:, None, :] * fi[None],
                   tr[:, None, :] * fi[None] + ti[:, None, :] * fr[None]], 1)
    gh = jnp.stack([tr[:, :, None] * fr[None] - ti[:, :, None] * fi[None],
                    -(tr[:, :, None] * fi[None] + ti[:, :, None] * fr[None])], 1)
    return fstack, cstack, g, gh


def _dense_dft_tables(lc):
    n = 2 * lc
    a = _angle(jnp.arange(n, dtype=jnp.int32)[:, None] * jnp.arange(lc, dtype=jnp.int32)[None, :], n)
    ff = jnp.concatenate([jnp.cos(a), -jnp.sin(a)], 0)
    ci = jnp.concatenate([jnp.cos(a.T), jnp.sin(a.T)], 0)
    return ff, ci


def kernel(x, c, ctx, c_ctx, ada_w, ada_b, w_in, w_out, q_gain, k_gain, rwkv_mu, rwkv_w0, rwkv_wB, rwkv_a0, rwkv_aB, rwkv_gB, rwkv_kk, rwkv_ka, rwkv_rk, rwkv_gn_g, rwkv_gn_b, hy_short_w, hy_short_b, hy_w1, hy_b1, hy_freq1, hy_w2, hy_b2, hy_freq2, hy_w3, hy_decay, hy_bias, ln1_g, ln1_b, ln2_g, ln2_b, ffn_w1, ffn_w3, ffn_w2, moe_router, moe_w1, moe_w3, moe_w2):
    b, l, d = x.shape
    lc = ctx.shape[1]
    lt = lc + l
    depth = ada_w.shape[0]
    assert b == 2, "the long convolution packs the two batch rows as one complex signal"
    assert d == D_MODEL and lc % TOK_TILE == 0 and l % TOK_TILE == 0 and (2 * l) % (2 * DFT_N2) == 0
    cw = RWKV_WIDTH

    xx = jnp.concatenate([ctx, x], 1)
    cond8 = jnp.zeros((SUBLANES, d), F32).at[:b].set(c).at[b].set(c_ctx)
    mod_all = _ada_mod(cond8, ada_w, ada_b)

    cos64, sin64 = _rope_tables(l, lc)
    n1 = 2 * l // DFT_N2
    nh = n1 // 2
    cols = DFT_N2 * HY_WIDTH
    fstack, cstack, g_tab, gh_tab = (t.astype(BF16) for t in _dft_tables(n1))
    ff_c, ci_c = _dense_dft_tables(lc)
    feat_l = _hy_features(l)
    feat_c = _hy_features(lc)
    blk = jnp.arange(cw) // HEAD_DIM
    bd = (blk[:, None] == blk[None, :]).astype(F32)
    perm64 = jnp.concatenate([jnp.arange(0, HEAD_DIM, 2), jnp.arange(1, HEAD_DIM, 2)])
    perm_att = jnp.concatenate([h * HEAD_DIM + perm64 for h in range(ATT_HEADS + ATT_KV_HEADS)]
                               + [jnp.arange(ATT_WIDTH + ATT_KV_WIDTH, IN_ATT)])
    tq = TOK_TILE
    qn = 2 if (l // tq) % 2 == 0 else 1
    tk = _pick_tile(lt, (1280, 1024, 768, 512, 256))

    for li in range(depth):
        ml = mod_all[li]
        mod = jnp.stack([jnp.broadcast_to(ml[b].reshape(1, 6, d), (b, 6, d)), ml[:b].reshape(b, 6, d)], 1)
        wi = w_in[li]
        w_pad = jnp.concatenate([wi[:, :IN_ATT][:, perm_att], wi[:, IN_ATT:IN_ATT + IN_RWKV],
                                 jnp.zeros((d, IN_RWKV_PAD - IN_RWKV), F32), wi[:, IN_ATT + IN_RWKV:]],
                                1).astype(BF16)
        ua, ur, uh = _inproj(xx, mod, w_pad, lc)

        qt, kx, vt = _attn_prep(ua, cos64, sin64, q_gain[li][perm64][None], k_gain[li][perm64][None], lc)
        att_c = _flash(qt, kx, vt, tq, TOK_TILE, 1, l // tq, lc // tq, lc // TOK_TILE)
        s_bound = (HEAD_DIM ** 0.5) * LOG2E * jnp.max(jnp.abs(q_gain[li])) * jnp.max(jnp.abs(k_gain[li]))
        lat_args = (qt, kx, vt, tq, tk, qn, 0, l // (qn * tq), lt // tk)
        att_l = lax.cond(s_bound <= MAX_UNSHIFTED_SCORE,
                         lambda: _flash(*lat_args, bounded=True), lambda: _flash(*lat_args, bounded=False))

        wl = jnp.zeros((LANES, 5 * cw), F32)
        wl = wl.at[0:W_LORA, 0:cw].set(rwkv_wB[li, 0]).at[W_LORA:2 * W_LORA, cw:2 * cw].set(rwkv_wB[li, 1])
        o_a = 2 * W_LORA
        wl = wl.at[o_a:o_a + A_LORA, 2 * cw:3 * cw].set(rwkv_aB[li, 0])
        wl = wl.at[o_a + A_LORA:o_a + 2 * A_LORA, 3 * cw:4 * cw].set(rwkv_aB[li, 1])
        o_g = o_a + 2 * A_LORA
        wl = wl.at[o_g:o_g + G_LORA, 4 * cw:5 * cw].set(rwkv_gB[li])
        mu = jnp.pad(rwkv_mu[li], ((0, 0), (0, IN_RWKV_PAD - IN_RWKV)))
        r_, v_, kk_, g_, bon_, lw_, be_, kd_ = _rwkv_prep(
            ur, mu, wl, bd, rwkv_w0[li], rwkv_a0[li], rwkv_kk[li][None], rwkv_ka[li][None],
            rwkv_rk[li].reshape(1, cw), lc)
        yf = _wkv_scan(r_, v_, kk_, lw_, be_, kd_, lc, True)
        yb = _wkv_scan(r_, v_, kk_, lw_, be_, kd_, lc, False)
        rw = _rwkv_out(yf, yb, bon_, g_, bd, rwkv_gn_g[li][None], rwkv_gn_b[li][None])

        x1, x2, vv = _hy_prep(uh, hy_short_w[li], hy_short_b[li][None], lc)
        w1p = jnp.pad(hy_w1[li], ((0, LANES - HY_EMB), (0, 0)))
        fargs = (w1p, hy_b1[li][None], hy_freq1[li][None], hy_w2[li], hy_b2[li][None], hy_freq2[li][None],
                 hy_w3[li], hy_decay[li][None])
        fw, bw, nrm = _hy_filter(feat_l, *fargs)
        c2 = HY_ORDER * HY_WIDTH
        a4 = _dft_cols(fstack, fw.reshape(1, nh, DFT_N2 * c2), 0, bw.reshape(1, nh, DFT_N2 * c2), 0, True)
        kspec = _spec(a4.reshape(4, n1, DFT_N2, c2), g_tab, nrm, n1 * DFT_N2)
        lat = lambda t: t[:, lc:].reshape(b, nh, cols)
        x1l, x2l, zin = lat(x1), lat(x2), lat(vv)
        for o, gate in enumerate((x1l, x2l)):
            a = _dft_cols(fstack, zin, 0, zin, 1, False)
            bv = _conv_mid(a.reshape(2, n1, DFT_N2, HY_WIDTH), g_tab, gh_tab, kspec, o)
            bias_cols = jnp.tile(hy_bias[li, o], DFT_N2)[None]
            zin = _idft_cols(cstack, bv.reshape(2, n1, cols), gate, zin, bias_cols)
        hy_l = zin.reshape(b, l, HY_WIDTH)
        fw_c, bw_c, nrm_c = _hy_filter(feat_c, *fargs)
        hy_c = _hy_ctx(x1[:, :lc], x2[:, :lc], vv[:, :lc], fw_c, bw_c, nrm_c, hy_bias[li], ff_c, ci_c)

        xx = _outproj(att_c, att_l, rw, hy_c, hy_l, xx, mod, w_out[li].astype(BF16),
                      ln1_g[li][None], ln1_b[li][None], lc)

        j = li // 2
        if li % 2 == 0:
            xx = _ffn(xx, mod, ffn_w1[j].astype(BF16), ffn_w3[j].astype(BF16), ffn_w2[j].astype(BF16),
                      ln2_g[li][None], ln2_b[li][None], lc)
        else:
            wr = jnp.pad(moe_router[j], ((0, 0), (0, LANES - N_EXPERTS)))
            xx = _moe(xx, mod, wr, moe_w1[j].astype(BF16), moe_w3[j].astype(BF16), moe_w2[j].astype(BF16),
                      ln2_g[li][None], ln2_b[li][None], lc)
    return xx[:, lc:]
```

```python
import functools
import math

import jax
import jax.numpy as jnp
from jax import lax
from jax.experimental import pallas as pl
from jax.experimental.pallas import tpu as pltpu

F32 = jnp.float32
BF16 = jnp.bfloat16
HI = lax.Precision.HIGHEST

D_MODEL = 1024
DEPTH = 2
GRID_W = 64
HEAD_DIM = 64
HALF_HD = HEAD_DIM // 2
ATT_WIDTH = 512
RWKV_WIDTH = 256
HY_WIDTH = 256
ATT_HEADS = 8
ATT_KV_HEADS = 2
ATT_REP = 4
ATT_KV_WIDTH = 128
ROPE_THETA = 10000.0
QK_EPS = 1e-6
RWKV_HEADS = 4
W_LORA = 16
A_LORA = 16
G_LORA = 32
RWKV_GN_EPS = 64e-5
HY_ORDER = 2
HY_EMB = 33
HY_FFN = 64
N_EXPERTS = 8
LN_EPS = 1e-6
IN_ATT = ATT_WIDTH + 2 * ATT_KV_WIDTH
IN_RWKV = 3 * RWKV_WIDTH + 2 * W_LORA + 2 * A_LORA + G_LORA
IN_RWKV_PAD = 896
IN_HY = 3 * HY_WIDTH
ALPHA = float((2 * DEPTH) ** 0.25)
LOG2E = 1.4426950408889634
MAX_UNSHIFTED_SCORE = 40.0

LANES = 128
SUBLANES = 8
TOK_TILE = 256
WKV_CHUNK = 64
WKV_TILE = 256
DFT_N2 = 256
VMEM_LIMIT = 48 * 1024 * 1024


def _cparams(sem):
    return pltpu.CompilerParams(dimension_semantics=sem, vmem_limit_bytes=VMEM_LIMIT)


def _pick_tile(n, cands):
    for c in cands:
        if n % c == 0:
            return c
    raise ValueError(f"no tile for {n} in {cands}")


def _layer_norm(x):
    mu = jnp.mean(x, -1, keepdims=True)
    xc = x - mu
    var = jnp.mean(xc * xc, -1, keepdims=True)
    return xc * lax.rsqrt(var + LN_EPS)


def _sel_mod(mod_ref, j, row0, tm, lc):
    rid = row0 + lax.broadcasted_iota(jnp.int32, (tm, 1), 0)
    return jnp.where(rid < lc, mod_ref[0, 0, j:j + 1, :], mod_ref[0, 1, j:j + 1, :])


def _silu(x):
    return x * jax.nn.sigmoid(x)


def _ada_kernel(c_ref, w_ref, b_ref, o_ref):
    s = _silu(c_ref[...])
    o_ref[0] = jnp.dot(s, w_ref[0], precision=HI, preferred_element_type=F32) + b_ref[0]


def _ada_mod(cond8, ada_w, ada_b):
    depth, d, n = ada_w.shape
    tn = _pick_tile(n, (1536, 1024, 512, 256, 128))
    return pl.pallas_call(
        _ada_kernel,
        grid=(depth, n // tn),
        in_specs=[pl.BlockSpec((SUBLANES, d), lambda l, j: (0, 0)),
                  pl.BlockSpec((1, d, tn), lambda l, j: (l, 0, j)),
                  pl.BlockSpec((1, 1, tn), lambda l, j: (l, 0, j))],
        out_specs=pl.BlockSpec((1, SUBLANES, tn), lambda l, j: (l, 0, j)),
        out_shape=jax.ShapeDtypeStruct((depth, SUBLANES, n), F32),
        compiler_params=_cparams(("parallel", "parallel")),
        name="ada_mod",
    )(cond8, ada_w, ada_b.reshape(depth, 1, n))


def _inproj_kernel(x_ref, mod_ref, w_ref, oa_ref, or_ref, oh_ref, *, lc, tm):
    row0 = pl.program_id(1) * tm
    sh = _sel_mod(mod_ref, 0, row0, tm, lc)
    sc = _sel_mod(mod_ref, 1, row0, tm, lc)
    h = (_layer_norm(x_ref[0]) * (1.0 + sc) + sh).astype(BF16)
    u = jnp.dot(h, w_ref[...], preferred_element_type=F32)
    oa_ref[0] = u[:, :IN_ATT]
    or_ref[0] = u[:, IN_ATT:IN_ATT + IN_RWKV_PAD]
    oh_ref[0] = u[:, IN_ATT + IN_RWKV_PAD:]


def _inproj(xx, mod, w_pad, lc):
    b, lt, d = xx.shape
    tm = TOK_TILE
    n = w_pad.shape[1]
    return pl.pallas_call(
        functools.partial(_inproj_kernel, lc=lc, tm=tm),
        grid=(b, lt // tm),
        in_specs=[pl.BlockSpec((1, tm, d), lambda bi, i: (bi, i, 0)),
                  pl.BlockSpec((1, 2, 6, d), lambda bi, i: (bi, 0, 0, 0)),
                  pl.BlockSpec((d, n), lambda bi, i: (0, 0))],
        out_specs=[pl.BlockSpec((1, tm, IN_ATT), lambda bi, i: (bi, i, 0)),
                   pl.BlockSpec((1, tm, IN_RWKV_PAD), lambda bi, i: (bi, i, 0)),
                   pl.BlockSpec((1, tm, IN_HY), lambda bi, i: (bi, i, 0))],
        out_shape=[jax.ShapeDtypeStruct((b, lt, IN_ATT), F32),
                   jax.ShapeDtypeStruct((b, lt, IN_RWKV_PAD), F32),
                   jax.ShapeDtypeStruct((b, lt, IN_HY), F32)],
        compiler_params=_cparams(("parallel", "parallel")),
        name="inproj",
    )(xx, mod, w_pad)


def _attn_prep_kernel(u_ref, cos_ref, sin_ref, qg_ref, kg_ref, bd_ref, sw_ref, qt_ref, k_ref, vt_ref):
    u = u_ref[0]

    def norm_rope(x, g):
        w = x.shape[1]
        tile = lambda t: jnp.concatenate([t] * (w // LANES), -1)
        ms = _dot01(x * x, bd_ref[:w, :w]) * (1.0 / HEAD_DIM)
        xn = x * lax.rsqrt(ms + QK_EPS) * tile(g)
        sw = _dot01(xn, sw_ref[:w, :w])
        return xn * tile(cos_ref[...]) + sw * tile(sin_ref[...])

    q = norm_rope(u[:, :ATT_WIDTH], qg_ref[...]) * (LOG2E * HEAD_DIM ** -0.5)
    qt = q.T
    kx = norm_rope(u[:, ATT_WIDTH:ATT_WIDTH + ATT_KV_WIDTH], kg_ref[...])
    for g in range(ATT_KV_HEADS):
        base = g * ATT_REP * HEAD_DIM
        qt_ref[0, g, 0] = jnp.concatenate(
            [qt[base + r * HEAD_DIM:base + (r + 1) * HEAD_DIM] for r in range(ATT_REP)], -1).astype(BF16)
        k_ref[0, g] = kx[:, g * HEAD_DIM:(g + 1) * HEAD_DIM].astype(BF16)
    v0 = ATT_WIDTH + ATT_KV_WIDTH
    vt = u[:, v0:v0 + ATT_KV_WIDTH].T
    for g in range(ATT_KV_HEADS):
        vt_ref[0, g] = vt[g * HEAD_DIM:(g + 1) * HEAD_DIM].astype(BF16)


def _attn_prep(ua, cos, sin, qg, kg, bd_att, swap_att, lc):
    b, lt, _ = ua.shape
    tm = TOK_TILE
    nct = lc // tm
    nl = lt // tm - nct
    q_pos = lambda i: jnp.where(i < nct, nl + i, i - nct)
    return pl.pallas_call(
        _attn_prep_kernel,
        grid=(b, lt // tm),
        in_specs=[pl.BlockSpec((1, tm, IN_ATT), lambda bi, i: (bi, i, 0)),
                  pl.BlockSpec((tm, LANES), lambda bi, i: (i, 0)),
                  pl.BlockSpec((tm, LANES), lambda bi, i: (i, 0)),
                  pl.BlockSpec((1, LANES), lambda bi, i: (0, 0)),
                  pl.BlockSpec((1, LANES), lambda bi, i: (0, 0)),
                  pl.BlockSpec((ATT_WIDTH, ATT_WIDTH), lambda bi, i: (0, 0)),
                  pl.BlockSpec((ATT_WIDTH, ATT_WIDTH), lambda bi, i: (0, 0))],
        out_specs=[pl.BlockSpec((1, ATT_KV_HEADS, 1, HEAD_DIM, ATT_REP * tm),
                                lambda bi, i: (bi, 0, q_pos(i), 0, 0)),
                   pl.BlockSpec((1, ATT_KV_HEADS, tm, HEAD_DIM), lambda bi, i: (bi, 0, i, 0)),
                   pl.BlockSpec((1, ATT_KV_HEADS, HEAD_DIM, tm), lambda bi, i: (bi, 0, 0, i))],
        out_shape=[jax.ShapeDtypeStruct((b, ATT_KV_HEADS, lt // tm, HEAD_DIM, ATT_REP * tm), BF16),
                   jax.ShapeDtypeStruct((b, ATT_KV_HEADS, lt, HEAD_DIM), BF16),
                   jax.ShapeDtypeStruct((b, ATT_KV_HEADS, HEAD_DIM, lt), BF16)],
        compiler_params=_cparams(("parallel", "parallel")),
        name="attn_prep",
    )(ua, cos, sin, qg, kg, bd_att, swap_att)


def _flash_kernel(qt_ref, k_ref, vt_ref, o_ref, m_scr, l_scr, acc_scr, *, nk, tq, sub, qn):
    j = pl.program_id(3)

    @pl.when(j == 0)
    def _():
        m_scr[...] = jnp.full(m_scr.shape, -jnp.inf, F32)
        l_scr[...] = jnp.zeros(l_scr.shape, F32)
        acc_scr[...] = jnp.zeros(acc_scr.shape, F32)

    qt = jnp.concatenate([qt_ref[0, 0, t] for t in range(qn)], -1)
    nsub = k_ref.shape[2] // sub
    m = m_scr[...]
    l = l_scr[...]
    acc = acc_scr[...]
    scores = lambda c: jnp.dot(k_ref[0, 0, c * sub:(c + 1) * sub, :], qt, preferred_element_type=F32)
    pv = lambda c, p: jnp.dot(vt_ref[0, 0, :, c * sub:(c + 1) * sub], p, preferred_element_type=F32)
    s_next = scores(0)
    pend = None
    for c in range(nsub):
        s = s_next
        if c + 1 < nsub:
            s_next = scores(c + 1)
        if pend is not None:
            acc = pend[0] * acc + pv(c - 1, pend[1])
        m_new = jnp.maximum(m, jnp.max(s, 0, keepdims=True))
        a = jnp.exp2(m - m_new)
        p = jnp.exp2(s - m_new)
        l = a * l + jnp.sum(p, 0, keepdims=True)
        pend = (a, p.astype(BF16))
        m = m_new
    acc = pend[0] * acc + pv(nsub - 1, pend[1])
    m_scr[...] = m
    l_scr[...] = l
    acc_scr[...] = acc

    @pl.when(j == nk - 1)
    def _():
        o = (acc / l).T
        for t in range(qn):
            o_ref[0, t * tq:(t + 1) * tq, :] = jnp.concatenate(
                [o[(t * ATT_REP + r) * tq:(t * ATT_REP + r + 1) * tq] for r in range(ATT_REP)], -1
            ).astype(o_ref.dtype)


def _flash_bounded_kernel(qt_ref, k_ref, vt_ref, o_ref, l_scr, acc_scr, *, nk, tq, sub, qn):
    j = pl.program_id(3)

    @pl.when(j == 0)
    def _():
        l_scr[...] = jnp.zeros(l_scr.shape, F32)
        acc_scr[...] = jnp.zeros(acc_scr.shape, F32)

    qt = jnp.concatenate([qt_ref[0, 0, t] for t in range(qn)], -1)
    nsub = k_ref.shape[2] // sub
    l = l_scr[...]
    acc = acc_scr[...]
    scores = lambda c: jnp.dot(k_ref[0, 0, c * sub:(c + 1) * sub, :], qt, preferred_element_type=F32)
    s_next = scores(0)
    for c in range(nsub):
        s = s_next
        if c + 1 < nsub:
            s_next = scores(c + 1)
        p = jnp.exp2(s)
        l = l + jnp.sum(p.reshape(sub // SUBLANES, SUBLANES, p.shape[1]), 0)
        acc = acc + jnp.dot(vt_ref[0, 0, :, c * sub:(c + 1) * sub], p.astype(BF16), preferred_element_type=F32)
    l_scr[...] = l
    acc_scr[...] = acc

    @pl.when(j == nk - 1)
    def _():
        o = (acc / jnp.sum(l, 0, keepdims=True)).T
        for t in range(qn):
            o_ref[0, t * tq:(t + 1) * tq, :] = jnp.concatenate(
                [o[(t * ATT_REP + r) * tq:(t * ATT_REP + r + 1) * tq] for r in range(ATT_REP)], -1
            ).astype(o_ref.dtype)


def _flash(qt, k, vt, tq, tk, qn, q_blk0, nq, nk, bounded=False):
    b = qt.shape[0]
    lq = nq * qn * tq
    sub = _pick_tile(tk, (256, 128))
    lanes = qn * ATT_REP * tq
    if bounded:
        body = functools.partial(_flash_bounded_kernel, nk=nk, tq=tq, sub=sub, qn=qn)
        scratch = [pltpu.VMEM((SUBLANES, lanes), F32), pltpu.VMEM((HEAD_DIM, lanes), F32)]
    else:
        body = functools.partial(_flash_kernel, nk=nk, tq=tq, sub=sub, qn=qn)
        scratch = [pltpu.VMEM((1, lanes), F32), pltpu.VMEM((1, lanes), F32), pltpu.VMEM((HEAD_DIM, lanes), F32)]
    return pl.pallas_call(
        body,
        grid=(b, ATT_KV_HEADS, nq, nk),
        in_specs=[pl.BlockSpec((1, 1, qn, HEAD_DIM, ATT_REP * tq), lambda bi, g, i, j: (bi, g, i + q_blk0, 0, 0)),
                  pl.BlockSpec((1, 1, tk, HEAD_DIM), lambda bi, g, i, j: (bi, g, j, 0)),
                  pl.BlockSpec((1, 1, HEAD_DIM, tk), lambda bi, g, i, j: (bi, g, 0, j))],
        out_specs=pl.BlockSpec((1, qn * tq, ATT_REP * HEAD_DIM), lambda bi, g, i, j: (bi, i, g)),
        out_shape=jax.ShapeDtypeStruct((b, lq, ATT_WIDTH), BF16),
        scratch_shapes=scratch,
        compiler_params=_cparams(("parallel", "parallel", "parallel", "arbitrary")),
        name="flash_bounded" if bounded else "flash",
    )(qt, k, vt)


def _prev_next(u, up8, un8, i, tm, lc, lt):
    start = i * tm
    p_ok = jnp.logical_and(start != 0, start != lc)
    n_ok = jnp.logical_and(start + tm != lc, start + tm != lt)
    prow = jnp.where(p_ok, up8[SUBLANES - 1:SUBLANES], 0.0)
    nrow = jnp.where(n_ok, un8[0:1], 0.0)
    rid = lax.broadcasted_iota(jnp.int32, u.shape, 0)
    prev = jnp.where(rid == 0, prow, pltpu.roll(u, 1, 0))
    nxt = jnp.where(rid == tm - 1, nrow, pltpu.roll(u, tm - 1, 0))
    return prev, nxt


def _halo_specs(tm, c, lt):
    r = tm // SUBLANES
    last = lt // SUBLANES - 1
    return [pl.BlockSpec((1, tm, c), lambda bi, i: (bi, i, 0)),
            pl.BlockSpec((1, SUBLANES, c), lambda bi, i: (bi, jnp.maximum(i * r - 1, 0), 0)),
            pl.BlockSpec((1, SUBLANES, c), lambda bi, i: (bi, jnp.minimum((i + 1) * r, last), 0))]


def _softplus(z):
    return jnp.maximum(z, 0.0) + jnp.log1p(jnp.exp(-jnp.abs(z)))


def _rwkv_prep_kernel(u_ref, up_ref, un_ref, mu_ref, wl_ref, bd_ref, w0_ref, a0_ref, kkw_ref, ka_ref, rk_ref,
                      r_o, v_o, kk_o, g_o, bon_o, lw_o, be_o, kd_o, *, lc, lt, tm):
    i = pl.program_id(1)
    u = u_ref[0]
    prev, nxt = _prev_next(u, up_ref[0], un_ref[0], i, tm, lc, lt)
    us = u + mu_ref[0:1] * (prev - u) + mu_ref[1:2] * (nxt - u)
    c = RWKV_WIDTH
    r = us[:, 0:c]
    k = us[:, c:2 * c]
    v = us[:, 2 * c:3 * c]
    slab = us[:, 3 * c:3 * c + LANES]
    lane = lax.broadcasted_iota(jnp.int32, slab.shape, 1)
    o_a = 2 * W_LORA
    o_g = o_a + 2 * A_LORA
    act = jnp.where(lane < o_a, jnp.tanh(slab),
                    jnp.where(lane < o_g, slab,
                              jnp.where(lane < o_g + G_LORA, jax.nn.sigmoid(slab), 0.0)))
    lo = _dotp(act, wl_ref[...], 3)
    bd = bd_ref[...]
    kk0 = k * kkw_ref[...]
    kk = kk0 * lax.rsqrt(_dot01(kk0 * kk0, bd) + 1e-12)
    r_o[0] = r
    v_o[0] = v
    kk_o[0] = kk
    g_o[0] = lo[:, 4 * c:5 * c]
    bon = jnp.zeros_like(r)
    for d in range(2):
        w_raw = w0_ref[d:d + 1] + lo[:, d * c:(d + 1) * c]
        lw = -jnp.exp(-_softplus(-w_raw) - 0.5)
        a = jax.nn.sigmoid(a0_ref[d:d + 1] + lo[:, (2 + d) * c:(3 + d) * c])
        kd = k * (1.0 + (a - 1.0) * ka_ref[...])
        lw_o[0, d] = lw
        be_o[0, d] = a * kk
        kd_o[0, d] = kd
        bon = bon + r * kd * rk_ref[...]
    bon_o[0] = _dot01(bon, bd) * v


def _rwkv_prep(ur, mu, wl, bd, w0, a0, kkw, ka, rk, lc):
    b, lt, cp = ur.shape
    tm = TOK_TILE
    c = RWKV_WIDTH
    full = lambda shape: pl.BlockSpec(shape, lambda bi, i: (0,) * len(shape))
    tok = pl.BlockSpec((1, tm, c), lambda bi, i: (bi, i, 0))
    tok2 = pl.BlockSpec((1, 2, tm, c), lambda bi, i: (bi, 0, i, 0))
    s1 = jax.ShapeDtypeStruct((b, lt, c), F32)
    s2 = jax.ShapeDtypeStruct((b, 2, lt, c), F32)
    return pl.pallas_call(
        functools.partial(_rwkv_prep_kernel, lc=lc, lt=lt, tm=tm),
        grid=(b, lt // tm),
        in_specs=_halo_specs(tm, cp, lt) + [full((2, cp)), full((LANES, 5 * c)), full((c, c)), full((2, c)),
                                            full((2, c)), full((1, c)), full((1, c)), full((1, c))],
        out_specs=[tok, tok, tok, tok, tok, tok2, tok2, tok2],
        out_shape=[s1, s1, s1, s1, s1, s2, s2, s2],
        compiler_params=_cparams(("parallel", "parallel")),
        name="rwkv_prep",
    )(ur, ur, ur, mu, wl, bd, w0, a0, kkw, ka, rk)


def _mm(a, b):
    return jnp.dot(a, b, precision=HI, preferred_element_type=F32)


_NN = ((1,), (0,))
_NT = ((1,), (1,))
_TN = ((0,), (0,))


def _split2(a):
    hi = a.astype(BF16)
    return hi, (a - hi.astype(F32)).astype(BF16)


def _dotp(a, b, passes, dims=_NN):
    if a.ndim == 3:
        dn = (((dims[0][0] + 1,), (dims[1][0] + 1,)), ((0,), (0,)))
    else:
        dn = (dims, ((), ()))
    dg = lambda p, q: lax.dot_general(p, q, dn, preferred_element_type=F32)
    if passes == 1:
        return dg(a.astype(BF16), b.astype(BF16))
    ah, al = _split2(a)
    bh, bl = _split2(b)
    return dg(ah, bh) + dg(ah, bl) + dg(al, bh)


def _dot01(a, ones):
    ah, al = _split2(a)
    o = ones.astype(BF16)
    return jnp.dot(ah, o, preferred_element_type=F32) + jnp.dot(al, o, preferred_element_type=F32)


P_M = 1
P_INV = 1
P_W = 1
P_Z = 1
P_STATE = 3
P_DFT = 1


def _unit_tri_inv(a_mat, row, col, eye, passes):
    t = a_mat.shape[-1]
    eye_f = eye.astype(F32)
    base = SUBLANES
    same = (row // base) == (col // base)
    n1 = -jnp.where(same, a_mat, 0.0)
    n2 = _dotp(n1, n1, passes)
    n4 = _dotp(n2, n2, passes)
    x = _dotp(_dotp(eye_f + n1, eye_f + n2, passes), eye_f + n4, passes)
    m = base
    while m < t:
        off = jnp.logical_and((row // (2 * m)) == (col // (2 * m)), (row // m) != (col // m))
        x = x - _dotp(x, _dotp(jnp.where(off, a_mat, 0.0), x, passes), passes)
        m *= 2
    return x


def _wkv_kernel(r_ref, v_ref, kk_ref, lw_ref, be_ref, kd_ref, y_ref, h_scr, pq_scr, ry_scr, *, fwd):
    t = WKV_CHUNK
    n = HEAD_DIM
    g = WKV_TILE // WKV_CHUNK
    tt = WKV_TILE
    order = list(range(g)) if fwd else list(range(g - 1, -1, -1))

    @pl.when(pl.program_id(1) == 0)
    def _():
        h_scr[...] = jnp.zeros(h_scr.shape, F32)
        pq_scr[...] = jnp.zeros(pq_scr.shape, F32)
        ry_scr[...] = jnp.zeros(ry_scr.shape, F32)

    nh = RWKV_HEADS
    hm = h_scr[...]
    for p in range(g):
        ry = ry_scr[p * nh:(p + 1) * nh]
        pq = pq_scr[p * nh:(p + 1) * nh]
        y = _dotp(ry[:, :, :n], hm, P_STATE) + ry[:, :, n:]
        hm = _dotp(pq[:, :, :n], hm, P_STATE) + pq[:, :, n:]
        c = order[p]
        y_ref[0, c * t:(c + 1) * t, :] = jnp.concatenate([y[h] for h in range(nh)], -1)
    h_scr[...] = hm

    row = lax.broadcasted_iota(jnp.int32, (tt, tt), 0)
    col = lax.broadcasted_iota(jnp.int32, (tt, tt), 1)
    same = (row // t) == (col // t)
    tri = jnp.logical_and(same, (row >= col) if fwd else (row <= col))
    sums = jnp.concatenate([jnp.where(tri, 1.0, 0.0), jnp.where(same, 1.0, 0.0)], 0).astype(BF16)
    lw = lw_ref[0, 0]
    l1 = lw.astype(BF16)
    rem = lw - l1.astype(F32)
    l2 = rem.astype(BF16)
    l3 = (rem - l2.astype(F32)).astype(BF16)
    cc = (jnp.dot(sums, l1, preferred_element_type=F32) + jnp.dot(sums, l2, preferred_element_type=F32)
          + jnp.dot(sums, l3, preferred_element_type=F32))
    cum = cc[:tt]
    ctot = cc[tt:]
    e_neg = jnp.exp(-cum)
    e_end = jnp.exp(ctot - cum)
    g_end = jnp.exp(ctot)
    v = v_ref[0]
    be = be_ref[0, 0]
    kd = kd_ref[0, 0]
    kap_t = kk_ref[0] * jnp.exp(cum - lw)
    r_t = r_ref[0] * jnp.exp(cum)
    be_t = be * e_neg
    kd_t = kd * e_neg
    be_h = be * e_end
    kd_h = kd * e_end
    r64 = lax.broadcasted_iota(jnp.int32, (t, t), 0)
    c64 = lax.broadcasted_iota(jnp.int32, (t, t), 1)
    eye = r64 == c64
    strict = (r64 > c64) if fwd else (r64 < c64)
    incl = jnp.logical_or(strict, eye)

    def units(x, rows=t):
        return jnp.stack([x[order[p] * t:order[p] * t + rows, h * n:(h + 1) * n]
                          for p in range(g) for h in range(nh)], 0)

    kap = units(kap_t)
    rt = units(r_t)
    vh = units(v)
    nu = g * nh
    m = _dotp(jnp.concatenate([kap, rt], 1), jnp.concatenate([units(be_t), units(kd_t)], 1), P_M, _NT)
    a_mat = jnp.where(strict, m[:, :t, :t], 0.0)
    b_mat = jnp.where(strict, m[:, :t, t:], 0.0)
    ab_r = jnp.concatenate([jnp.where(incl, m[:, t:, :t], 0.0), jnp.where(incl, m[:, t:, t:], 0.0)], 2)
    tinv = _unit_tri_inv(a_mat, r64, c64, eye, P_INV)
    w = _dotp(tinv, jnp.concatenate([kap, _dotp(b_mat, vh, P_W)], 2), P_W)
    z = jnp.concatenate([-w, jnp.concatenate([jnp.zeros((nu, t, n), F32), vh], 2)], 1)
    ry = _dotp(ab_r, z, P_Z)
    ry_scr[:, :, :n] = ry[:, :, :n] + rt
    ry_scr[:, :, n:] = ry[:, :, n:]
    pq = _dotp(jnp.concatenate([units(be_h), units(kd_h)], 1), z, P_Z, _TN)
    gd = jnp.where(eye, jnp.broadcast_to(units(g_end, 1), (nu, n, n)), 0.0)
    pq_scr[:, :, :n] = pq[:, :, :n] + gd
    pq_scr[:, :, n:] = pq[:, :, n:]


def _wkv_scan(r, v, kk, lw, be, kd, lc, fwd):
    b, lt, c = r.shape
    tt = WKV_TILE
    nt = lt // tt
    ntc = lc // tt
    d = 0 if fwd else 1

    def tile(i):
        if fwd:
            return i
        return jnp.where(i < ntc, ntc - 1 - i, nt - 1 - (i - ntc))

    t_in = lambda i: tile(jnp.minimum(i, nt - 1))
    t_out = lambda i: tile(jnp.maximum(i - 1, 0))
    one = pl.BlockSpec((1, tt, c), lambda bi, i: (bi, t_in(i), 0))
    two = pl.BlockSpec((1, 1, tt, c), lambda bi, i: (bi, d, t_in(i), 0))
    g = tt // WKV_CHUNK
    return pl.pallas_call(
        functools.partial(_wkv_kernel, fwd=fwd),
        grid=(b, nt + 1),
        in_specs=[one, one, one, two, two, two],
        out_specs=pl.BlockSpec((1, tt, c), lambda bi, i: (bi, t_out(i), 0)),
        out_shape=jax.ShapeDtypeStruct((b, lt, c), F32),
        scratch_shapes=[pltpu.VMEM((RWKV_HEADS, HEAD_DIM, HEAD_DIM), F32),
                        pltpu.VMEM((g * RWKV_HEADS, HEAD_DIM, 2 * HEAD_DIM), F32),
                        pltpu.VMEM((g * RWKV_HEADS, WKV_CHUNK, 2 * HEAD_DIM), F32)],
        compiler_params=_cparams(("parallel", "arbitrary")),
        name="wkv_fwd" if fwd else "wkv_bwd",
    )(r, v, kk, lw, be, kd)


def _rwkv_out_kernel(yf_ref, yb_ref, bon_ref, g_ref, bd_ref, gg_ref, gb_ref, o_ref):
    y = yf_ref[0] + yb_ref[0] + bon_ref[0]
    bd = bd_ref[...]
    mu = _dot01(y, bd) * (1.0 / HEAD_DIM)
    yc = y - mu
    var = _dot01(yc * yc, bd) * (1.0 / HEAD_DIM)
    yn = yc * lax.rsqrt(var + RWKV_GN_EPS) * gg_ref[...] + gb_ref[...]
    o_ref[0] = (yn * g_ref[0]).astype(o_ref.dtype)


def _rwkv_out(yf, yb, bon, g, bd, gg, gb):
    b, lt, c = yf.shape
    tm = TOK_TILE
    tok = pl.BlockSpec((1, tm, c), lambda bi, i: (bi, i, 0))
    full = lambda shape: pl.BlockSpec(shape, lambda bi, i: (0,) * len(shape))
    return pl.pallas_call(
        _rwkv_out_kernel,
        grid=(b, lt // tm),
        in_specs=[tok, tok, tok, tok, full((c, c)), full((1, c)), full((1, c))],
        out_specs=tok,
        out_shape=jax.ShapeDtypeStruct((b, lt, c), BF16),
        compiler_params=_cparams(("parallel", "parallel")),
        name="rwkv_out",
    )(yf, yb, bon, g, bd, gg, gb)


def _hy_prep_kernel(u_ref, up_ref, un_ref, w_ref, b_ref, x1_o, x2_o, v_o, *, lc, lt, tm):
    i = pl.program_id(1)
    u = u_ref[0]
    prev, nxt = _prev_next(u, up_ref[0], un_ref[0], i, tm, lc, lt)
    y = prev * w_ref[0:1] + u * w_ref[1:2] + nxt * w_ref[2:3] + b_ref[...]
    c = HY_WIDTH
    x1_o[0] = y[:, :c]
    x2_o[0] = y[:, c:2 * c]
    v_o[0] = y[:, 2 * c:]


def _hy_prep(uh, w, bias, lc):
    b, lt, cin = uh.shape
    tm = TOK_TILE
    c = HY_WIDTH
    full = lambda shape: pl.BlockSpec(shape, lambda bi, i: (0,) * len(shape))
    tok = pl.BlockSpec((1, tm, c), lambda bi, i: (bi, i, 0))
    s1 = jax.ShapeDtypeStruct((b, lt, c), F32)
    return pl.pallas_call(
        functools.partial(_hy_prep_kernel, lc=lc, lt=lt, tm=tm),
        grid=(b, lt // tm),
        in_specs=_halo_specs(tm, cin, lt) + [full((3, cin)), full((1, cin))],
        out_specs=[tok, tok, tok],
        out_shape=[s1, s1, s1],
        compiler_params=_cparams(("parallel", "parallel")),
        name="hy_prep",
    )(uh, uh, uh, w, bias)


def _hy_filter_kernel(z_ref, w1_ref, b1_ref, f1_ref, w2_ref, b2_ref, f2_ref, w3_ref, dec_ref,
                      fw_o, bw_o, nrm_o, *, tl):
    i = pl.program_id(0)
    z = z_ref[...]
    h = jnp.sin(f1_ref[...] * (_dotp(z, w1_ref[...], 3) + b1_ref[...]))
    h = jnp.sin(f2_ref[...] * (_dotp(h, w2_ref[...], 3) + b2_ref[...]))
    h = _dotp(h, w3_ref[...], 3) * jnp.exp(-z[:, 0:1] * dec_ref[...])
    c = HY_WIDTH
    fw = jnp.concatenate([h[:, 0:c], h[:, 2 * c:3 * c]], 1)
    bw = jnp.concatenate([h[:, c:2 * c], h[:, 3 * c:4 * c]], 1)
    rid = i * tl + lax.broadcasted_iota(jnp.int32, (tl, 1), 0)
    bw = jnp.where(rid == 0, 0.0, bw)
    fw_o[...] = fw
    bw_o[...] = bw

    @pl.when(i == 0)
    def _():
        nrm_o[...] = jnp.zeros(nrm_o.shape, F32)

    nrm_o[...] += jnp.sum(jnp.abs(fw) + jnp.abs(bw), 0, keepdims=True)


def _hy_filter(feat, w1p, b1, f1, w2, b2, f2, w3, dec):
    l, fe = feat.shape
    tl = _pick_tile(l, (512, 256))
    c2 = HY_ORDER * HY_WIDTH
    full = lambda shape: pl.BlockSpec(shape, lambda i: (0,) * len(shape))
    return pl.pallas_call(
        functools.partial(_hy_filter_kernel, tl=tl),
        grid=(l // tl,),
        in_specs=[pl.BlockSpec((tl, fe), lambda i: (i, 0)), full(w1p.shape), full(b1.shape), full(f1.shape),
                  full(w2.shape), full(b2.shape), full(f2.shape), full(w3.shape), full(dec.shape)],
        out_specs=[pl.BlockSpec((tl, c2), lambda i: (i, 0)), pl.BlockSpec((tl, c2), lambda i: (i, 0)),
                   pl.BlockSpec((1, c2), lambda i: (0, 0))],
        out_shape=[jax.ShapeDtypeStruct((l, c2), F32), jax.ShapeDtypeStruct((l, c2), F32),
                   jax.ShapeDtypeStruct((1, c2), F32)],
        compiler_params=_cparams(("arbitrary",)),
        name="hy_filter",
    )(feat, w1p, b1, f1, w2, b2, f2, w3, dec)


def _dft_cols_kernel(f_ref, xa_ref, xb_ref, o_ref, *, n1, pair):
    f = f_ref[...]
    pa = _dotp(f, xa_ref[0], P_DFT)
    pb = _dotp(f, xb_ref[0], P_DFT)
    if pair:
        o_ref[0] = pa[:n1].astype(o_ref.dtype)
        o_ref[1] = pa[n1:].astype(o_ref.dtype)
        o_ref[2] = pb[:n1].astype(o_ref.dtype)
        o_ref[3] = pb[n1:].astype(o_ref.dtype)
    else:
        o_ref[0] = (pa[:n1] - pb[n1:]).astype(o_ref.dtype)
        o_ref[1] = (pb[:n1] + pa[n1:]).astype(o_ref.dtype)


def _dft_cols(fstack, xa, ia, xb, ib, pair):
    n1 = fstack.shape[0] // 2
    _, nh, cols = xa.shape
    tc = _pick_tile(cols, (4096, 2048, 1024, 512, 256, 128))
    no = 4 if pair else 2
    return pl.pallas_call(
        functools.partial(_dft_cols_kernel, n1=n1, pair=pair),
        grid=(cols // tc,),
        in_specs=[pl.BlockSpec(fstack.shape, lambda j: (0, 0)),
                  pl.BlockSpec((1, nh, tc), lambda j: (ia, 0, j)),
                  pl.BlockSpec((1, nh, tc), lambda j: (ib, 0, j))],
        out_specs=pl.BlockSpec((no, n1, tc), lambda j: (0, 0, j)),
        out_shape=jax.ShapeDtypeStruct((no, n1, cols), BF16),
        compiler_params=_cparams(("parallel",)),
        name="dft_cols",
    )(fstack, xa, xb)


def _cplx_left(gs, zr, zi, n):
    c = zr.shape[1]
    p = _dotp(gs, jnp.concatenate([zr, zi], 1), P_DFT)
    return p[:n, :c] - p[n:, c:], p[:n, c:] + p[n:, :c]


def _spec_kernel(a_ref, g_ref, nrm_ref, o_ref, *, n_total):
    n2 = DFT_N2
    gs = jnp.concatenate([g_ref[0, 0], g_ref[0, 1]], 0)
    fr, fi = _cplx_left(gs, a_ref[0, 0], a_ref[1, 0], n2)
    br, bi = _cplx_left(gs, a_ref[2, 0], a_ref[3, 0], n2)
    s = 1.0 / (nrm_ref[...] * n_total)
    o_ref[0, 0] = (fr + br) * s
    o_ref[1, 0] = (fi - bi) * s


def _spec(a4, g, nrm, n_total):
    _, n1, n2, c2 = a4.shape
    return pl.pallas_call(
        functools.partial(_spec_kernel, n_total=float(n_total)),
        grid=(n1,),
        in_specs=[pl.BlockSpec((4, 1, n2, c2), lambda k: (0, k, 0, 0)),
                  pl.BlockSpec((1, 2, n2, n2), lambda k: (k, 0, 0, 0)),
                  pl.BlockSpec((1, c2), lambda k: (0, 0))],
        out_specs=pl.BlockSpec((2, 1, n2, c2), lambda k: (0, k, 0, 0)),
        out_shape=jax.ShapeDtypeStruct((2, n1, n2, c2), F32),
        compiler_params=_cparams(("parallel",)),
        name="hy_spec",
    )(a4, g, nrm)


def _conv_mid_kernel(a_ref, g_ref, gh_ref, k_ref, o_ref):
    n2 = DFT_N2
    gs = jnp.concatenate([g_ref[0, 0], g_ref[0, 1]], 0)
    xr, xi = _cplx_left(gs, a_ref[0, 0], a_ref[1, 0], n2)
    kr = k_ref[0, 0]
    ki = k_ref[1, 0]
    zr = xr * kr - xi * ki
    zi = xr * ki + xi * kr
    ghs = jnp.concatenate([gh_ref[0, 0], gh_ref[0, 1]], 0)
    yr, yi = _cplx_left(ghs, zr, zi, n2)
    o_ref[0, 0] = yr.astype(o_ref.dtype)
    o_ref[1, 0] = yi.astype(o_ref.dtype)


def _conv_mid(a, g, gh, kspec, order):
    _, n1, n2, c = a.shape
    return pl.pallas_call(
        _conv_mid_kernel,
        grid=(n1,),
        in_specs=[pl.BlockSpec((2, 1, n2, c), lambda k: (0, k, 0, 0)),
                  pl.BlockSpec((1, 2, n2, n2), lambda k: (k, 0, 0, 0)),
                  pl.BlockSpec((1, 2, n2, n2), lambda k: (k, 0, 0, 0)),
                  pl.BlockSpec((2, 1, n2, c), lambda k: (0, k, 0, order))],
        out_specs=pl.BlockSpec((2, 1, n2, c), lambda k: (0, k, 0, 0)),
        out_shape=jax.ShapeDtypeStruct((2, n1, n2, c), BF16),
        compiler_params=_cparams(("parallel",)),
        name="hy_conv_mid",
    )(a, g, gh, kspec)


def _idft_cols_kernel(c_ref, b_ref, g0_ref, g1_ref, x0_ref, x1_ref, bias_ref, o_ref, *, nh):
    cs = c_ref[...]
    pr = _dotp(cs, b_ref[0], P_DFT)
    pi = _dotp(cs, b_ref[1], P_DFT)
    yr = pr[:nh] - pi[nh:]
    yi = pi[:nh] + pr[nh:]
    bias = bias_ref[...]
    o_ref[0] = g0_ref[0] * (yr + x0_ref[0] * bias)
    o_ref[1] = g1_ref[0] * (yi + x1_ref[0] * bias)


def _idft_cols(cstack, bv, gate, xin, bias_cols):
    nh2, n1 = cstack.shape
    nh = nh2 // 2
    cols = bv.shape[-1]
    tc = _pick_tile(cols, (4096, 2048, 1024, 512, 256, 128))
    row = lambda bi: pl.BlockSpec((1, nh, tc), lambda j: (bi, 0, j))
    return pl.pallas_call(
        functools.partial(_idft_cols_kernel, nh=nh),
        grid=(cols // tc,),
        in_specs=[pl.BlockSpec((nh2, n1), lambda j: (0, 0)),
                  pl.BlockSpec((2, n1, tc), lambda j: (0, 0, j)),
                  row(0), row(1), row(0), row(1),
                  pl.BlockSpec((1, tc), lambda j: (0, j))],
        out_specs=pl.BlockSpec((2, nh, tc), lambda j: (0, 0, j)),
        out_shape=jax.ShapeDtypeStruct((2, nh, cols), F32),
        compiler_params=_cparams(("parallel",)),
        name="idft_cols",
    )(cstack, bv, gate, gate, xin, xin, bias_cols)


def _hy_ctx_kernel(x1_ref, x2_ref, v_ref, fw_ref, bw_ref, nrm_ref, bias_ref, ff_ref, ci_ref, o_ref, *, lc):
    n = 2 * lc
    c = HY_WIDTH
    ff = ff_ref[...]
    ci = ci_ref[...]
    pf = _mm(ff, fw_ref[...])
    pb = _mm(ff, bw_ref[...])
    s = 1.0 / (nrm_ref[...] * float(n))
    kr = (pf[:n] + pb[:n]) * s
    ki = (pf[n:] - pb[n:]) * s

    def conv(z0, z1, o):
        xr, xi = _cplx_left(ff, z0, z1, n)
        krr = kr[:, o * c:(o + 1) * c]
        kii = ki[:, o * c:(o + 1) * c]
        return _cplx_left(ci, xr * krr - xi * kii, xr * kii + xi * krr, lc)

    v0 = v_ref[0]
    v1 = v_ref[1]
    y0, y1 = conv(v0, v1, 0)
    z0 = x1_ref[0] * (y0 + v0 * bias_ref[0:1])
    z1 = x1_ref[1] * (y1 + v1 * bias_ref[0:1])
    y0, y1 = conv(z0, z1, 1)
    o_ref[0] = x2_ref[0] * (y0 + z0 * bias_ref[1:2])
    o_ref[1] = x2_ref[1] * (y1 + z1 * bias_ref[1:2])


def _hy_ctx(x1, x2, v, fw, bw, nrm, bias, ff, ci):
    b, lc, c = v.shape
    vm = pl.BlockSpec(memory_space=pltpu.VMEM)
    return pl.pallas_call(
        functools.partial(_hy_ctx_kernel, lc=lc),
        in_specs=[vm] * 9,
        out_specs=vm,
        out_shape=jax.ShapeDtypeStruct((b, lc, c), F32),
        compiler_params=pltpu.CompilerParams(vmem_limit_bytes=VMEM_LIMIT),
        name="hy_ctx",
    )(x1, x2, v, fw, bw, nrm, bias, ff, ci)


def _outproj_kernel(attc_ref, attl_ref, rw_ref, hyc_ref, hyl_ref, x_ref, mod_ref, w_ref, lg_ref, lb_ref, o_ref,
                    *, lc, tm):
    row0 = pl.program_id(1) * tm
    a0 = ATT_WIDTH
    a1 = ATT_WIDTH + RWKV_WIDTH
    is_ctx = row0 < lc
    att = jnp.where(is_ctx, attc_ref[0], attl_ref[0])
    hy = jnp.where(is_ctx, hyc_ref[0], hyl_ref[0])
    o = jnp.dot(att, w_ref[:a0], preferred_element_type=F32)
    o += jnp.dot(rw_ref[0], w_ref[a0:a1], preferred_element_type=F32)
    o += jnp.dot(hy.astype(BF16), w_ref[a1:], preferred_element_type=F32)
    g = _sel_mod(mod_ref, 2, row0, tm, lc)
    y = ALPHA * x_ref[0] + g * o
    o_ref[0] = _layer_norm(y) * lg_ref[...] + lb_ref[...]


def _outproj(att_c, att_l, rw, hy_c, hy_l, xx, mod, w, lg, lb, lc):
    b, lt, d = xx.shape
    tm = TOK_TILE
    nct = lc // tm
    tok = lambda c: pl.BlockSpec((1, tm, c), lambda bi, i: (bi, i, 0))
    ctx = lambda c: pl.BlockSpec((1, tm, c), lambda bi, i: (bi, jnp.minimum(i, nct - 1), 0))
    lat = lambda c: pl.BlockSpec((1, tm, c), lambda bi, i: (bi, jnp.maximum(i - nct, 0), 0))
    full = lambda shape: pl.BlockSpec(shape, lambda bi, i: (0,) * len(shape))
    return pl.pallas_call(
        functools.partial(_outproj_kernel, lc=lc, tm=tm),
        grid=(b, lt // tm),
        in_specs=[ctx(ATT_WIDTH), lat(ATT_WIDTH), tok(RWKV_WIDTH), ctx(HY_WIDTH), lat(HY_WIDTH), tok(d),
                  pl.BlockSpec((1, 2, 6, d), lambda bi, i: (bi, 0, 0, 0)),
                  full(w.shape), full((1, d)), full((1, d))],
        out_specs=tok(d),
        out_shape=jax.ShapeDtypeStruct((b, lt, d), F32),
        compiler_params=_cparams(("parallel", "parallel")),
        name="outproj",
    )(att_c, att_l, rw, hy_c, hy_l, xx, mod, w, lg, lb)


def _ffn_kernel(x_ref, mod_ref, w1_ref, w3_ref, w2_ref, lg_ref, lb_ref, o_ref, h_scr, acc_scr, *, lc, tm, nf):
    row0 = pl.program_id(1) * tm
    f = pl.program_id(2)

    @pl.when(f == 0)
    def _():
        sh = _sel_mod(mod_ref, 3, row0, tm, lc)
        sc = _sel_mod(mod_ref, 4, row0, tm, lc)
        h_scr[...] = (_layer_norm(x_ref[0]) * (1.0 + sc) + sh).astype(BF16)
        acc_scr[...] = jnp.zeros(acc_scr.shape, F32)

    h = h_scr[...]
    a = jnp.dot(h, w1_ref[...], preferred_element_type=F32)
    g = jnp.dot(h, w3_ref[...], preferred_element_type=F32)
    acc_scr[...] += jnp.dot((_silu(a) * g).astype(BF16), w2_ref[...], preferred_element_type=F32)

    @pl.when(f == nf - 1)
    def _():
        gate = _sel_mod(mod_ref, 5, row0, tm, lc)
        y = ALPHA * x_ref[0] + gate * acc_scr[...]
        o_ref[0] = _layer_norm(y) * lg_ref[...] + lb_ref[...]


def _ffn(xx, mod, w1, w3, w2, lg, lb, lc):
    b, lt, d = xx.shape
    ff = w1.shape[1]
    tm = _pick_tile(lt, (1280, 768, 512, 256))
    tf = _pick_tile(ff, (256, 128))
    nf = ff // tf
    return pl.pallas_call(
        functools.partial(_ffn_kernel, lc=lc, tm=tm, nf=nf),
        grid=(b, lt // tm, nf),
        in_specs=[pl.BlockSpec((1, tm, d), lambda bi, i, f: (bi, i, 0)),
                  pl.BlockSpec((1, 2, 6, d), lambda bi, i, f: (bi, 0, 0, 0)),
                  pl.BlockSpec((d, tf), lambda bi, i, f: (0, f)),
                  pl.BlockSpec((d, tf), lambda bi, i, f: (0, f)),
                  pl.BlockSpec((tf, d), lambda bi, i, f: (f, 0)),
                  pl.BlockSpec((1, d), lambda bi, i, f: (0, 0)),
                  pl.BlockSpec((1, d), lambda bi, i, f: (0, 0))],
        out_specs=pl.BlockSpec((1, tm, d), lambda bi, i, f: (bi, i, 0)),
        out_shape=jax.ShapeDtypeStruct((b, lt, d), F32),
        scratch_shapes=[pltpu.VMEM((tm, d), BF16), pltpu.VMEM((tm, d), F32)],
        compiler_params=_cparams(("parallel", "parallel", "arbitrary")),
        name="ffn",
    )(xx, mod, w1, w3, w2, lg, lb)


def _moe_kernel(x_ref, mod_ref, wr_ref, w1_ref, w3_ref, w2_ref, lg_ref, lb_ref, o_ref,
                h_scr, acc_scr, gate_scr, *, lc, tm, nf, ns):
    row0 = pl.program_id(1) * tm
    s = pl.program_id(2)

    @pl.when(s == 0)
    def _():
        sh = _sel_mod(mod_ref, 3, row0, tm, lc)
        sc = _sel_mod(mod_ref, 4, row0, tm, lc)
        h = _layer_norm(x_ref[0]) * (1.0 + sc) + sh
        h_scr[...] = h.astype(BF16)
        acc_scr[...] = jnp.zeros(acc_scr.shape, F32)
        logits = _dotp(h, wr_ref[...], 3)
        lane = lax.broadcasted_iota(jnp.int32, logits.shape, 1)
        neg = jnp.float32(-jnp.inf)
        lg = jnp.where(lane < N_EXPERTS, logits, neg)
        m1 = jnp.max(lg, -1, keepdims=True)
        i1 = jnp.min(jnp.where(lg == m1, lane, LANES), -1, keepdims=True)
        lg2 = jnp.where(lane == i1, neg, lg)
        m2 = jnp.max(lg2, -1, keepdims=True)
        i2 = jnp.min(jnp.where(lg2 == m2, lane, LANES), -1, keepdims=True)
        e2 = jnp.exp(m2 - m1)
        g1 = 1.0 / (1.0 + e2)
        g2 = e2 / (1.0 + e2)
        for e in range(N_EXPERTS):
            gate_scr[e] = jnp.where(i1 == e, g1, 0.0) + jnp.where(i2 == e, g2, 0.0)

    h = h_scr[...]
    a = jnp.dot(h, w1_ref[0], preferred_element_type=F32)
    g = jnp.dot(h, w3_ref[0], preferred_element_type=F32)
    act = _silu(a) * g * gate_scr[s // nf]
    acc_scr[...] += jnp.dot(act.astype(BF16), w2_ref[0], preferred_element_type=F32)

    @pl.when(s == ns - 1)
    def _():
        gate = _sel_mod(mod_ref, 5, row0, tm, lc)
        y = ALPHA * x_ref[0] + gate * acc_scr[...]
        o_ref[0] = _layer_norm(y) * lg_ref[...] + lb_ref[...]


def _moe(xx, mod, wr, w1, w3, w2, lg, lb, lc):
    b, lt, d = xx.shape
    ne, _, ff = w1.shape
    tm = _pick_tile(lt, (1280, 768, 512, 256))
    tf = _pick_tile(ff, (256, 128))
    nf = ff // tf
    ns = ne * nf
    return pl.pallas_call(
        functools.partial(_moe_kernel, lc=lc, tm=tm, nf=nf, ns=ns),
        grid=(b, lt // tm, ns),
        in_specs=[pl.BlockSpec((1, tm, d), lambda bi, i, s: (bi, i, 0)),
                  pl.BlockSpec((1, 2, 6, d), lambda bi, i, s: (bi, 0, 0, 0)),
                  pl.BlockSpec((d, LANES), lambda bi, i, s: (0, 0)),
                  pl.BlockSpec((1, d, tf), lambda bi, i, s: (s // nf, 0, s % nf)),
                  pl.BlockSpec((1, d, tf), lambda bi, i, s: (s // nf, 0, s % nf)),
                  pl.BlockSpec((1, tf, d), lambda bi, i, s: (s // nf, s % nf, 0)),
                  pl.BlockSpec((1, d), lambda bi, i, s: (0, 0)),
                  pl.BlockSpec((1, d), lambda bi, i, s: (0, 0))],
        out_specs=pl.BlockSpec((1, tm, d), lambda bi, i, s: (bi, i, 0)),
        out_shape=jax.ShapeDtypeStruct((b, lt, d), F32),
        scratch_shapes=[pltpu.VMEM((tm, d), BF16), pltpu.VMEM((tm, d), F32),
                        pltpu.VMEM((ne, tm, 1), F32)],
        compiler_params=_cparams(("parallel", "parallel", "arbitrary")),
        name="moe",
    )(xx, mod, wr, w1, w3, w2, lg, lb)


def _rope_tables(l, lc):
    rows = l // GRID_W
    row = jnp.repeat(jnp.arange(rows, dtype=F32), GRID_W)
    col = jnp.tile(jnp.arange(GRID_W, dtype=F32), rows)
    n_freq = HEAD_DIM // 4
    inv_freq = ROPE_THETA ** (-jnp.arange(n_freq, dtype=F32) / n_freq)
    ang = jnp.concatenate([row[:, None] * inv_freq, col[:, None] * inv_freq], -1)
    cos, sin = jnp.cos(ang), jnp.sin(ang)
    cos2 = jnp.concatenate([jnp.ones((lc, LANES), F32), jnp.concatenate([cos, cos, cos, cos], -1)], 0)
    sin2 = jnp.concatenate([jnp.zeros((lc, LANES), F32), jnp.concatenate([-sin, sin, -sin, sin], -1)], 0)
    return cos2, sin2


def _hy_features(l):
    bands = (HY_EMB - 1) // 2
    t = jnp.linspace(0.0, 1.0, l, dtype=F32)[:, None]
    f = jnp.linspace(1e-4, bands - 1, bands, dtype=F32)[None, :]
    wt = 2.0 * math.pi * jnp.arange(l, dtype=F32)[:, None] / l
    z = jnp.concatenate([t, jnp.cos(f * wt), -jnp.sin(f * wt)], -1)
    return jnp.pad(z, ((0, 0), (0, LANES - HY_EMB)))


def _angle(idx, n):
    return (2.0 * math.pi / n) * (idx % n).astype(F32)


def _dft_tables(n1):
    nh = n1 // 2
    n2 = DFT_N2
    n = n1 * n2
    k1 = jnp.arange(n1, dtype=jnp.int32)
    a1 = _angle(k1[:, None] * jnp.arange(nh, dtype=jnp.int32)[None, :], n1)
    fstack = jnp.concatenate([jnp.cos(a1), -jnp.sin(a1)], 0)
    cstack = jnp.concatenate([jnp.cos(a1.T), jnp.sin(a1.T)], 0)
    k2 = jnp.arange(n2, dtype=jnp.int32)
    at = _angle(k1[:, None] * k2[None, :], n)
    tr, ti = jnp.cos(at), -jnp.sin(at)
    a2 = _angle(k2[:, None] * k2[None, :], n2)
    fr, fi = jnp.cos(a2), -jnp.sin(a2)
    g = jnp.stack([tr[:, None, :] * fr[None] - ti[:, None, :] * fi[None],
                   tr[:, None, :] * fi[None] + ti[:, None, :] * fr[None]], 1)
    gh = jnp.stack([tr[:, :, None] * fr[None] - ti[:, :, None] * fi[None],
                    -(tr[:, :, None] * fi[None] + ti[:, :, None] * fr[None])], 1)
    return fstack, cstack, g, gh


def _dense_dft_tables(lc):
    n = 2 * lc
    a = _angle(jnp.arange(n, dtype=jnp.int32)[:, None] * jnp.arange(lc, dtype=jnp.int32)[None, :], n)
    ff = jnp.concatenate([jnp.cos(a), -jnp.sin(a)], 0)
    ci = jnp.concatenate([jnp.cos(a.T), jnp.sin(a.T)], 0)
    return ff, ci


def kernel(x, c, ctx, c_ctx, ada_w, ada_b, w_in, w_out, q_gain, k_gain, rwkv_mu, rwkv_w0, rwkv_wB, rwkv_a0, rwkv_aB, rwkv_gB, rwkv_kk, rwkv_ka, rwkv_rk, rwkv_gn_g, rwkv_gn_b, hy_short_w, hy_short_b, hy_w1, hy_b1, hy_freq1, hy_w2, hy_b2, hy_freq2, hy_w3, hy_decay, hy_bias, ln1_g, ln1_b, ln2_g, ln2_b, ffn_w1, ffn_w3, ffn_w2, moe_router, moe_w1, moe_w3, moe_w2):
    b, l, d = x.shape
    lc = ctx.shape[1]
    lt = lc + l
    depth = ada_w.shape[0]
    assert b == 2, "the long convolution packs the two batch rows as one complex signal"
    assert d == D_MODEL and lc % TOK_TILE == 0 and l % TOK_TILE == 0 and (2 * l) % (2 * DFT_N2) == 0
    cw = RWKV_WIDTH

    xx = jnp.concatenate([ctx, x], 1)
    cond8 = jnp.zeros((SUBLANES, d), F32).at[:b].set(c).at[b].set(c_ctx)
    mod_all = _ada_mod(cond8, ada_w, ada_b)

    cos64, sin64 = _rope_tables(l, lc)
    n1 = 2 * l // DFT_N2
    nh = n1 // 2
    cols = DFT_N2 * HY_WIDTH
    fstack, cstack, g_tab, gh_tab = (t.astype(BF16) for t in _dft_tables(n1))
    ff_c, ci_c = _dense_dft_tables(lc)
    feat_l = _hy_features(l)
    feat_c = _hy_features(lc)
    blk = jnp.arange(cw) // HEAD_DIM
    bd = (blk[:, None] == blk[None, :]).astype(F32)
    ch = jnp.arange(ATT_WIDTH)
    bd_att = (ch[:, None] // HEAD_DIM == ch[None, :] // HEAD_DIM).astype(BF16)
    swap_att = (ch[:, None] == (ch[None, :] + HALF_HD) % HEAD_DIM + (ch[None, :] // HEAD_DIM) * HEAD_DIM
                ).astype(BF16)
    perm64 = jnp.concatenate([jnp.arange(0, HEAD_DIM, 2), jnp.arange(1, HEAD_DIM, 2)])
    perm_att = jnp.concatenate([h * HEAD_DIM + perm64 for h in range(ATT_HEADS + ATT_KV_HEADS)]
                               + [jnp.arange(ATT_WIDTH + ATT_KV_WIDTH, IN_ATT)])
    tq = TOK_TILE
    qn = next(n for n in (4, 2, 1) if (l // tq) % n == 0)
    tk = _pick_tile(lt, (3328, 1280, 1024, 768, 512, 256))

    for li in range(depth):
        ml = mod_all[li]
        mod = jnp.stack([jnp.broadcast_to(ml[b].reshape(1, 6, d), (b, 6, d)), ml[:b].reshape(b, 6, d)], 1)
        wi = w_in[li]
        w_pad = jnp.concatenate([wi[:, :IN_ATT][:, perm_att], wi[:, IN_ATT:IN_ATT + IN_RWKV],
                                 jnp.zeros((d, IN_RWKV_PAD - IN_RWKV), F32), wi[:, IN_ATT + IN_RWKV:]],
                                1).astype(BF16)
        ua, ur, uh = _inproj(xx, mod, w_pad, lc)

        two = lambda gain: jnp.tile(gain[perm64], 2)[None]
        qt, kx, vt = _attn_prep(ua, cos64, sin64, two(q_gain[li]), two(k_gain[li]), bd_att, swap_att, lc)
        att_c = _flash(qt, kx, vt, tq, TOK_TILE, 1, l // tq, lc // tq, lc // TOK_TILE)
        s_bound = (HEAD_DIM ** 0.5) * LOG2E * jnp.max(jnp.abs(q_gain[li])) * jnp.max(jnp.abs(k_gain[li]))
        lat_args = (qt, kx, vt, tq, tk, qn, 0, l // (qn * tq), lt // tk)
        att_l = lax.cond(s_bound <= MAX_UNSHIFTED_SCORE,
                         lambda: _flash(*lat_args, bounded=True), lambda: _flash(*lat_args, bounded=False))

        wl = jnp.zeros((LANES, 5 * cw), F32)
        wl = wl.at[0:W_LORA, 0:cw].set(rwkv_wB[li, 0]).at[W_LORA:2 * W_LORA, cw:2 * cw].set(rwkv_wB[li, 1])
        o_a = 2 * W_LORA
        wl = wl.at[o_a:o_a + A_LORA, 2 * cw:3 * cw].set(rwkv_aB[li, 0])
        wl = wl.at[o_a + A_LORA:o_a + 2 * A_LORA, 3 * cw:4 * cw].set(rwkv_aB[li, 1])
        o_g = o_a + 2 * A_LORA
        wl = wl.at[o_g:o_g + G_LORA, 4 * cw:5 * cw].set(rwkv_gB[li])
        mu = jnp.pad(rwkv_mu[li], ((0, 0), (0, IN_RWKV_PAD - IN_RWKV)))
        r_, v_, kk_, g_, bon_, lw_, be_, kd_ = _rwkv_prep(
            ur, mu, wl, bd, rwkv_w0[li], rwkv_a0[li], rwkv_kk[li][None], rwkv_ka[li][None],
            rwkv_rk[li].reshape(1, cw), lc)
        yf = _wkv_scan(r_, v_, kk_, lw_, be_, kd_, lc, True)
        yb = _wkv_scan(r_, v_, kk_, lw_, be_, kd_, lc, False)
        rw = _rwkv_out(yf, yb, bon_, g_, bd, rwkv_gn_g[li][None], rwkv_gn_b[li][None])

        x1, x2, vv = _hy_prep(uh, hy_short_w[li], hy_short_b[li][None], lc)
        w1p = jnp.pad(hy_w1[li], ((0, LANES - HY_EMB), (0, 0)))
        fargs = (w1p, hy_b1[li][None], hy_freq1[li][None], hy_w2[li], hy_b2[li][None], hy_freq2[li][None],
                 hy_w3[li], hy_decay[li][None])
        fw, bw, nrm = _hy_filter(feat_l, *fargs)
        c2 = HY_ORDER * HY_WIDTH
        a4 = _dft_cols(fstack, fw.reshape(1, nh, DFT_N2 * c2), 0, bw.reshape(1, nh, DFT_N2 * c2), 0, True)
        kspec = _spec(a4.reshape(4, n1, DFT_N2, c2), g_tab, nrm, n1 * DFT_N2)
        lat = lambda t: t[:, lc:].reshape(b, nh, cols)
        x1l, x2l, zin = lat(x1), lat(x2), lat(vv)
        for o, gate in enumerate((x1l, x2l)):
            a = _dft_cols(fstack, zin, 0, zin, 1, False)
            bv = _conv_mid(a.reshape(2, n1, DFT_N2, HY_WIDTH), g_tab, gh_tab, kspec, o)
            bias_cols = jnp.tile(hy_bias[li, o], DFT_N2)[None]
            zin = _idft_cols(cstack, bv.reshape(2, n1, cols), gate, zin, bias_cols)
        hy_l = zin.reshape(b, l, HY_WIDTH)
        fw_c, bw_c, nrm_c = _hy_filter(feat_c, *fargs)
        hy_c = _hy_ctx(x1[:, :lc], x2[:, :lc], vv[:, :lc], fw_c, bw_c, nrm_c, hy_bias[li], ff_c, ci_c)

        xx = _outproj(att_c, att_l, rw, hy_c, hy_l, xx, mod, w_out[li].astype(BF16),
                      ln1_g[li][None], ln1_b[li][None], lc)

        j = li // 2
        if li % 2 == 0:
            xx = _ffn(xx, mod, ffn_w1[j].astype(BF16), ffn_w3[j].astype(BF16), ffn_w2[j].astype(BF16),
                      ln2_g[li][None], ln2_b[li][None], lc)
        else:
            wr = jnp.pad(moe_router[j], ((0, 0), (0, LANES - N_EXPERTS)))
            xx = _moe(xx, mod, wr, moe_w1[j].astype(BF16), moe_w3[j].astype(BF16), moe_w2[j].astype(BF16),
                      ln2_g[li][None], ln2_b[li][None], lc)
    return xx[:, lc:]
```

```python
import functools
import math

import jax
import jax.numpy as jnp
from jax import lax
from jax.experimental import pallas as pl
from jax.experimental.pallas import tpu as pltpu

F32 = jnp.float32
BF16 = jnp.bfloat16
HI = lax.Precision.HIGHEST

D_MODEL = 1024
DEPTH = 2
GRID_W = 64
HEAD_DIM = 64
HALF_HD = HEAD_DIM // 2
ATT_WIDTH = 512
RWKV_WIDTH = 256
HY_WIDTH = 256
ATT_HEADS = 8
ATT_KV_HEADS = 2
ATT_REP = 4
ATT_KV_WIDTH = 128
ROPE_THETA = 10000.0
QK_EPS = 1e-6
RWKV_HEADS = 4
W_LORA = 16
A_LORA = 16
G_LORA = 32
RWKV_GN_EPS = 64e-5
HY_ORDER = 2
HY_EMB = 33
HY_FFN = 64
N_EXPERTS = 8
LN_EPS = 1e-6
IN_ATT = ATT_WIDTH + 2 * ATT_KV_WIDTH
IN_RWKV = 3 * RWKV_WIDTH + 2 * W_LORA + 2 * A_LORA + G_LORA
IN_RWKV_PAD = 896
IN_HY = 3 * HY_WIDTH
ALPHA = float((2 * DEPTH) ** 0.25)
LOG2E = 1.4426950408889634
MAX_UNSHIFTED_SCORE = 40.0

LANES = 128
SUBLANES = 8
TOK_TILE = 256
WKV_CHUNK = 64
WKV_TILE = 256
DFT_N2 = 256
VMEM_LIMIT = 48 * 1024 * 1024


def _cparams(sem):
    return pltpu.CompilerParams(dimension_semantics=sem, vmem_limit_bytes=VMEM_LIMIT)


def _pick_tile(n, cands):
    for c in cands:
        if n % c == 0:
            return c
    raise ValueError(f"no tile for {n} in {cands}")


def _layer_norm(x):
    mu = jnp.mean(x, -1, keepdims=True)
    xc = x - mu
    var = jnp.mean(xc * xc, -1, keepdims=True)
    return xc * lax.rsqrt(var + LN_EPS)


def _sel_mod(mod_ref, j, row0, tm, lc):
    rid = row0 + lax.broadcasted_iota(jnp.int32, (tm, 1), 0)
    return jnp.where(rid < lc, mod_ref[0, 0, j:j + 1, :], mod_ref[0, 1, j:j + 1, :])


def _silu(x):
    return x * jax.nn.sigmoid(x)


def _ada_kernel(c_ref, w_ref, b_ref, o_ref):
    s = _silu(c_ref[...])
    o_ref[0] = jnp.dot(s, w_ref[0], precision=HI, preferred_element_type=F32) + b_ref[0]


def _ada_mod(cond8, ada_w, ada_b):
    depth, d, n = ada_w.shape
    tn = _pick_tile(n, (1536, 1024, 512, 256, 128))
    return pl.pallas_call(
        _ada_kernel,
        grid=(depth, n // tn),
        in_specs=[pl.BlockSpec((SUBLANES, d), lambda l, j: (0, 0)),
                  pl.BlockSpec((1, d, tn), lambda l, j: (l, 0, j)),
                  pl.BlockSpec((1, 1, tn), lambda l, j: (l, 0, j))],
        out_specs=pl.BlockSpec((1, SUBLANES, tn), lambda l, j: (l, 0, j)),
        out_shape=jax.ShapeDtypeStruct((depth, SUBLANES, n), F32),
        compiler_params=_cparams(("parallel", "parallel")),
        name="ada_mod",
    )(cond8, ada_w, ada_b.reshape(depth, 1, n))


def _inproj_kernel(x_ref, mod_ref, w_ref, oa_ref, or_ref, oh_ref, *, lc, tm):
    row0 = pl.program_id(1) * tm
    sh = _sel_mod(mod_ref, 0, row0, tm, lc)
    sc = _sel_mod(mod_ref, 1, row0, tm, lc)
    h = (_layer_norm(x_ref[0]) * (1.0 + sc) + sh).astype(BF16)
    u = jnp.dot(h, w_ref[...], preferred_element_type=F32)
    oa_ref[0] = u[:, :IN_ATT]
    or_ref[0] = u[:, IN_ATT:IN_ATT + IN_RWKV_PAD]
    oh_ref[0] = u[:, IN_ATT + IN_RWKV_PAD:]


def _inproj(xx, mod, w_pad, lc):
    b, lt, d = xx.shape
    tm = TOK_TILE
    n = w_pad.shape[1]
    return pl.pallas_call(
        functools.partial(_inproj_kernel, lc=lc, tm=tm),
        grid=(b, lt // tm),
        in_specs=[pl.BlockSpec((1, tm, d), lambda bi, i: (bi, i, 0)),
                  pl.BlockSpec((1, 2, 6, d), lambda bi, i: (bi, 0, 0, 0)),
                  pl.BlockSpec((d, n), lambda bi, i: (0, 0))],
        out_specs=[pl.BlockSpec((1, tm, IN_ATT), lambda bi, i: (bi, i, 0)),
                   pl.BlockSpec((1, tm, IN_RWKV_PAD), lambda bi, i: (bi, i, 0)),
                   pl.BlockSpec((1, tm, IN_HY), lambda bi, i: (bi, i, 0))],
        out_shape=[jax.ShapeDtypeStruct((b, lt, IN_ATT), F32),
                   jax.ShapeDtypeStruct((b, lt, IN_RWKV_PAD), F32),
                   jax.ShapeDtypeStruct((b, lt, IN_HY), F32)],
        compiler_params=_cparams(("parallel", "parallel")),
        name="inproj",
    )(xx, mod, w_pad)


def _attn_prep_kernel(u_ref, cos_ref, sin_ref, qg_ref, kg_ref, bd_ref, sw_ref, qt_ref, k_ref, vt_ref):
    u = u_ref[0]

    def norm_rope(x, g):
        w = x.shape[1]
        tile = lambda t: jnp.concatenate([t] * (w // LANES), -1)
        ms = _dot01(x * x, bd_ref[:w, :w]) * (1.0 / HEAD_DIM)
        xn = x * lax.rsqrt(ms + QK_EPS) * tile(g)
        sw = _dot01(xn, sw_ref[:w, :w])
        return xn * tile(cos_ref[...]) + sw * tile(sin_ref[...])

    q = norm_rope(u[:, :ATT_WIDTH], qg_ref[...]) * (LOG2E * HEAD_DIM ** -0.5)
    qt = q.T
    kx = norm_rope(u[:, ATT_WIDTH:ATT_WIDTH + ATT_KV_WIDTH], kg_ref[...])
    for g in range(ATT_KV_HEADS):
        base = g * ATT_REP * HEAD_DIM
        qt_ref[0, g, 0] = jnp.concatenate(
            [qt[base + r * HEAD_DIM:base + (r + 1) * HEAD_DIM] for r in range(ATT_REP)], -1).astype(BF16)
        k_ref[0, g] = kx[:, g * HEAD_DIM:(g + 1) * HEAD_DIM].astype(BF16)
    v0 = ATT_WIDTH + ATT_KV_WIDTH
    vt = u[:, v0:v0 + ATT_KV_WIDTH].T
    for g in range(ATT_KV_HEADS):
        vt_ref[0, g] = vt[g * HEAD_DIM:(g + 1) * HEAD_DIM].astype(BF16)


def _attn_prep(ua, cos, sin, qg, kg, bd_att, swap_att, lc):
    b, lt, _ = ua.shape
    tm = TOK_TILE
    nct = lc // tm
    nl = lt // tm - nct
    q_pos = lambda i: jnp.where(i < nct, nl + i, i - nct)
    return pl.pallas_call(
        _attn_prep_kernel,
        grid=(b, lt // tm),
        in_specs=[pl.BlockSpec((1, tm, IN_ATT), lambda bi, i: (bi, i, 0)),
                  pl.BlockSpec((tm, LANES), lambda bi, i: (i, 0)),
                  pl.BlockSpec((tm, LANES), lambda bi, i: (i, 0)),
                  pl.BlockSpec((1, LANES), lambda bi, i: (0, 0)),
                  pl.BlockSpec((1, LANES), lambda bi, i: (0, 0)),
                  pl.BlockSpec((ATT_WIDTH, ATT_WIDTH), lambda bi, i: (0, 0)),
                  pl.BlockSpec((ATT_WIDTH, ATT_WIDTH), lambda bi, i: (0, 0))],
        out_specs=[pl.BlockSpec((1, ATT_KV_HEADS, 1, HEAD_DIM, ATT_REP * tm),
                                lambda bi, i: (bi, 0, q_pos(i), 0, 0)),
                   pl.BlockSpec((1, ATT_KV_HEADS, tm, HEAD_DIM), lambda bi, i: (bi, 0, i, 0)),
                   pl.BlockSpec((1, ATT_KV_HEADS, HEAD_DIM, tm), lambda bi, i: (bi, 0, 0, i))],
        out_shape=[jax.ShapeDtypeStruct((b, ATT_KV_HEADS, lt // tm, HEAD_DIM, ATT_REP * tm), BF16),
                   jax.ShapeDtypeStruct((b, ATT_KV_HEADS, lt, HEAD_DIM), BF16),
                   jax.ShapeDtypeStruct((b, ATT_KV_HEADS, HEAD_DIM, lt), BF16)],
        compiler_params=_cparams(("parallel", "parallel")),
        name="attn_prep",
    )(ua, cos, sin, qg, kg, bd_att, swap_att)


def _flash_kernel(qt_ref, k_ref, vt_ref, o_ref, m_scr, l_scr, acc_scr, *, nk, tq, sub, qn):
    j = pl.program_id(3)

    @pl.when(j == 0)
    def _():
        m_scr[...] = jnp.full(m_scr.shape, -jnp.inf, F32)
        l_scr[...] = jnp.zeros(l_scr.shape, F32)
        acc_scr[...] = jnp.zeros(acc_scr.shape, F32)

    qt = jnp.concatenate([qt_ref[0, 0, t] for t in range(qn)], -1)
    nsub = k_ref.shape[2] // sub
    m = m_scr[...]
    l = l_scr[...]
    acc = acc_scr[...]
    scores = lambda c: jnp.dot(k_ref[0, 0, c * sub:(c + 1) * sub, :], qt, preferred_element_type=F32)
    pv = lambda c, p: jnp.dot(vt_ref[0, 0, :, c * sub:(c + 1) * sub], p, preferred_element_type=F32)
    s_next = scores(0)
    pend = None
    for c in range(nsub):
        s = s_next
        if c + 1 < nsub:
            s_next = scores(c + 1)
        if pend is not None:
            acc = pend[0] * acc + pv(c - 1, pend[1])
        m_new = jnp.maximum(m, jnp.max(s, 0, keepdims=True))
        a = jnp.exp2(m - m_new)
        p = jnp.exp2(s - m_new)
        l = a * l + jnp.sum(p, 0, keepdims=True)
        pend = (a, p.astype(BF16))
        m = m_new
    acc = pend[0] * acc + pv(nsub - 1, pend[1])
    m_scr[...] = m
    l_scr[...] = l
    acc_scr[...] = acc

    @pl.when(j == nk - 1)
    def _():
        o = (acc / l).T
        for t in range(qn):
            o_ref[0, t * tq:(t + 1) * tq, :] = jnp.concatenate(
                [o[(t * ATT_REP + r) * tq:(t * ATT_REP + r + 1) * tq] for r in range(ATT_REP)], -1
            ).astype(o_ref.dtype)


def _flash_bounded_kernel(qt_ref, k_ref, vt_ref, o_ref, l_scr, acc_scr, *, nk, tq, sub, qn):
    j = pl.program_id(3)

    @pl.when(j == 0)
    def _():
        l_scr[...] = jnp.zeros(l_scr.shape, F32)
        acc_scr[...] = jnp.zeros(acc_scr.shape, F32)

    qt = jnp.concatenate([qt_ref[0, 0, t] for t in range(qn)], -1)
    nsub = k_ref.shape[2] // sub
    l = l_scr[...]
    acc = acc_scr[...]
    scores = lambda c: jnp.dot(k_ref[0, 0, c * sub:(c + 1) * sub, :], qt, preferred_element_type=F32)
    s_next = scores(0)
    for c in range(nsub):
        s = s_next
        if c + 1 < nsub:
            s_next = scores(c + 1)
        p = jnp.exp2(s)
        l = l + jnp.sum(p.reshape(sub // SUBLANES, SUBLANES, p.shape[1]), 0)
        acc = acc + jnp.dot(vt_ref[0, 0, :, c * sub:(c + 1) * sub], p.astype(BF16), preferred_element_type=F32)
    l_scr[...] = l
    acc_scr[...] = acc

    @pl.when(j == nk - 1)
    def _():
        o = (acc / jnp.sum(l, 0, keepdims=True)).T
        for t in range(qn):
            o_ref[0, t * tq:(t + 1) * tq, :] = jnp.concatenate(
                [o[(t * ATT_REP + r) * tq:(t * ATT_REP + r + 1) * tq] for r in range(ATT_REP)], -1
            ).astype(o_ref.dtype)


def _flash(qt, k, vt, tq, tk, qn, q_blk0, nq, nk, bounded=False):
    b = qt.shape[0]
    lq = nq * qn * tq
    sub = _pick_tile(tk, (256, 128))
    lanes = qn * ATT_REP * tq
    if bounded:
        body = functools.partial(_flash_bounded_kernel, nk=nk, tq=tq, sub=sub, qn=qn)
        scratch = [pltpu.VMEM((SUBLANES, lanes), F32), pltpu.VMEM((HEAD_DIM, lanes), F32)]
    else:
        body = functools.partial(_flash_kernel, nk=nk, tq=tq, sub=sub, qn=qn)
        scratch = [pltpu.VMEM((1, lanes), F32), pltpu.VMEM((1, lanes), F32), pltpu.VMEM((HEAD_DIM, lanes), F32)]
    return pl.pallas_call(
        body,
        grid=(b, ATT_KV_HEADS, nq, nk),
        in_specs=[pl.BlockSpec((1, 1, qn, HEAD_DIM, ATT_REP * tq), lambda bi, g, i, j: (bi, g, i + q_blk0, 0, 0)),
                  pl.BlockSpec((1, 1, tk, HEAD_DIM), lambda bi, g, i, j: (bi, g, j, 0)),
                  pl.BlockSpec((1, 1, HEAD_DIM, tk), lambda bi, g, i, j: (bi, g, 0, j))],
        out_specs=pl.BlockSpec((1, qn * tq, ATT_REP * HEAD_DIM), lambda bi, g, i, j: (bi, i, g)),
        out_shape=jax.ShapeDtypeStruct((b, lq, ATT_WIDTH), BF16),
        scratch_shapes=scratch,
        compiler_params=_cparams(("parallel", "parallel", "parallel", "arbitrary")),
        name="flash_bounded" if bounded else "flash",
    )(qt, k, vt)


def _prev_next(u, up8, un8, i, tm, lc, lt):
    start = i * tm
    p_ok = jnp.logical_and(start != 0, start != lc)
    n_ok = jnp.logical_and(start + tm != lc, start + tm != lt)
    prow = jnp.where(p_ok, up8[SUBLANES - 1:SUBLANES], 0.0)
    nrow = jnp.where(n_ok, un8[0:1], 0.0)
    rid = lax.broadcasted_iota(jnp.int32, u.shape, 0)
    prev = jnp.where(rid == 0, prow, pltpu.roll(u, 1, 0))
    nxt = jnp.where(rid == tm - 1, nrow, pltpu.roll(u, tm - 1, 0))
    return prev, nxt


def _halo_specs(tm, c, lt):
    r = tm // SUBLANES
    last = lt // SUBLANES - 1
    return [pl.BlockSpec((1, tm, c), lambda bi, i: (bi, i, 0)),
            pl.BlockSpec((1, SUBLANES, c), lambda bi, i: (bi, jnp.maximum(i * r - 1, 0), 0)),
            pl.BlockSpec((1, SUBLANES, c), lambda bi, i: (bi, jnp.minimum((i + 1) * r, last), 0))]


def _softplus(z):
    return jnp.maximum(z, 0.0) + jnp.log1p(jnp.exp(-jnp.abs(z)))


def _rwkv_prep_kernel(u_ref, up_ref, un_ref, mu_ref, wl_ref, bd_ref, w0_ref, a0_ref, kkw_ref, ka_ref, rk_ref,
                      r_o, v_o, kk_o, g_o, bon_o, lw_o, be_o, kd_o, *, lc, lt, tm):
    i = pl.program_id(1)
    u = u_ref[0]
    prev, nxt = _prev_next(u, up_ref[0], un_ref[0], i, tm, lc, lt)
    us = u + mu_ref[0:1] * (prev - u) + mu_ref[1:2] * (nxt - u)
    c = RWKV_WIDTH
    r = us[:, 0:c]
    k = us[:, c:2 * c]
    v = us[:, 2 * c:3 * c]
    slab = us[:, 3 * c:3 * c + LANES]
    lane = lax.broadcasted_iota(jnp.int32, slab.shape, 1)
    o_a = 2 * W_LORA
    o_g = o_a + 2 * A_LORA
    act = jnp.where(lane < o_a, jnp.tanh(slab),
                    jnp.where(lane < o_g, slab,
                              jnp.where(lane < o_g + G_LORA, jax.nn.sigmoid(slab), 0.0)))
    lo = _dotp(act, wl_ref[...], 3)
    bd = bd_ref[...]
    kk0 = k * kkw_ref[...]
    kk = kk0 * lax.rsqrt(_dot01(kk0 * kk0, bd) + 1e-12)
    r_o[0] = r
    v_o[0] = v
    kk_o[0] = kk
    g_o[0] = lo[:, 4 * c:5 * c]
    bon = jnp.zeros_like(r)
    for d in range(2):
        w_raw = w0_ref[d:d + 1] + lo[:, d * c:(d + 1) * c]
        lw = -jnp.exp(-_softplus(-w_raw) - 0.5)
        a = jax.nn.sigmoid(a0_ref[d:d + 1] + lo[:, (2 + d) * c:(3 + d) * c])
        kd = k * (1.0 + (a - 1.0) * ka_ref[...])
        lw_o[0, d] = lw
        be_o[0, d] = a * kk
        kd_o[0, d] = kd
        bon = bon + r * kd * rk_ref[...]
    bon_o[0] = _dot01(bon, bd) * v


def _rwkv_prep(ur, mu, wl, bd, w0, a0, kkw, ka, rk, lc):
    b, lt, cp = ur.shape
    tm = TOK_TILE
    c = RWKV_WIDTH
    full = lambda shape: pl.BlockSpec(shape, lambda bi, i: (0,) * len(shape))
    tok = pl.BlockSpec((1, tm, c), lambda bi, i: (bi, i, 0))
    tok2 = pl.BlockSpec((1, 2, tm, c), lambda bi, i: (bi, 0, i, 0))
    s1 = jax.ShapeDtypeStruct((b, lt, c), F32)
    s2 = jax.ShapeDtypeStruct((b, 2, lt, c), F32)
    return pl.pallas_call(
        functools.partial(_rwkv_prep_kernel, lc=lc, lt=lt, tm=tm),
        grid=(b, lt // tm),
        in_specs=_halo_specs(tm, cp, lt) + [full((2, cp)), full((LANES, 5 * c)), full((c, c)), full((2, c)),
                                            full((2, c)), full((1, c)), full((1, c)), full((1, c))],
        out_specs=[tok, tok, tok, tok, tok, tok2, tok2, tok2],
        out_shape=[s1, s1, s1, s1, s1, s2, s2, s2],
        compiler_params=_cparams(("parallel", "parallel")),
        name="rwkv_prep",
    )(ur, ur, ur, mu, wl, bd, w0, a0, kkw, ka, rk)


def _mm(a, b):
    return jnp.dot(a, b, precision=HI, preferred_element_type=F32)


_NN = ((1,), (0,))
_NT = ((1,), (1,))
_TN = ((0,), (0,))


def _split2(a):
    hi = a.astype(BF16)
    return hi, (a - hi.astype(F32)).astype(BF16)


def _dotp(a, b, passes, dims=_NN):
    if a.ndim == 3:
        dn = (((dims[0][0] + 1,), (dims[1][0] + 1,)), ((0,), (0,)))
    else:
        dn = (dims, ((), ()))
    dg = lambda p, q: lax.dot_general(p, q, dn, preferred_element_type=F32)
    if passes == 1:
        return dg(a.astype(BF16), b.astype(BF16))
    ah, al = _split2(a)
    bh, bl = _split2(b)
    return dg(ah, bh) + dg(ah, bl) + dg(al, bh)


def _dot01(a, ones):
    ah, al = _split2(a)
    o = ones.astype(BF16)
    return jnp.dot(ah, o, preferred_element_type=F32) + jnp.dot(al, o, preferred_element_type=F32)


P_M = 1
P_INV = 1
P_W = 1
P_Z = 1
P_STATE = 3
P_DFT = 1


def _unit_tri_inv(a_mat, row, col, eye, passes):
    t = a_mat.shape[-1]
    eye_f = eye.astype(F32)
    base = SUBLANES
    same = (row // base) == (col // base)
    n1 = -jnp.where(same, a_mat, 0.0)
    n2 = _dotp(n1, n1, passes)
    n4 = _dotp(n2, n2, passes)
    x = _dotp(_dotp(eye_f + n1, eye_f + n2, passes), eye_f + n4, passes)
    m = base
    while m < t:
        off = jnp.logical_and((row // (2 * m)) == (col // (2 * m)), (row // m) != (col // m))
        x = x - _dotp(x, _dotp(jnp.where(off, a_mat, 0.0), x, passes), passes)
        m *= 2
    return x


def _wkv_kernel(r_ref, v_ref, kk_ref, lw_ref, be_ref, kd_ref, y_ref, h_scr, pq_scr, ry_scr, *, fwd):
    t = WKV_CHUNK
    n = HEAD_DIM
    g = WKV_TILE // WKV_CHUNK
    tt = WKV_TILE
    order = list(range(g)) if fwd else list(range(g - 1, -1, -1))

    @pl.when(pl.program_id(1) == 0)
    def _():
        h_scr[...] = jnp.zeros(h_scr.shape, F32)
        pq_scr[...] = jnp.zeros(pq_scr.shape, F32)
        ry_scr[...] = jnp.zeros(ry_scr.shape, F32)

    nh = RWKV_HEADS
    hm = h_scr[...]
    for p in range(g):
        ry = ry_scr[p * nh:(p + 1) * nh]
        pq = pq_scr[p * nh:(p + 1) * nh]
        y = _dotp(ry[:, :, :n], hm, P_STATE) + ry[:, :, n:]
        hm = _dotp(pq[:, :, :n], hm, P_STATE) + pq[:, :, n:]
        c = order[p]
        y_ref[0, c * t:(c + 1) * t, :] = jnp.concatenate([y[h] for h in range(nh)], -1)
    h_scr[...] = hm

    row = lax.broadcasted_iota(jnp.int32, (tt, tt), 0)
    col = lax.broadcasted_iota(jnp.int32, (tt, tt), 1)
    same = (row // t) == (col // t)
    tri = jnp.logical_and(same, (row >= col) if fwd else (row <= col))
    sums = jnp.concatenate([jnp.where(tri, 1.0, 0.0), jnp.where(same, 1.0, 0.0)], 0).astype(BF16)
    lw = lw_ref[0, 0]
    l1 = lw.astype(BF16)
    rem = lw - l1.astype(F32)
    l2 = rem.astype(BF16)
    l3 = (rem - l2.astype(F32)).astype(BF16)
    cc = (jnp.dot(sums, l1, preferred_element_type=F32) + jnp.dot(sums, l2, preferred_element_type=F32)
          + jnp.dot(sums, l3, preferred_element_type=F32))
    cum = cc[:tt]
    ctot = cc[tt:]
    e_neg = jnp.exp(-cum)
    e_end = jnp.exp(ctot - cum)
    g_end = jnp.exp(ctot)
    v = v_ref[0]
    be = be_ref[0, 0]
    kd = kd_ref[0, 0]
    kap_t = kk_ref[0] * jnp.exp(cum - lw)
    r_t = r_ref[0] * jnp.exp(cum)
    be_t = be * e_neg
    kd_t = kd * e_neg
    be_h = be * e_end
    kd_h = kd * e_end
    r64 = lax.broadcasted_iota(jnp.int32, (t, t), 0)
    c64 = lax.broadcasted_iota(jnp.int32, (t, t), 1)
    eye = r64 == c64
    strict = (r64 > c64) if fwd else (r64 < c64)
    incl = jnp.logical_or(strict, eye)

    def units(x, rows=t):
        return jnp.stack([x[order[p] * t:order[p] * t + rows, h * n:(h + 1) * n]
                          for p in range(g) for h in range(nh)], 0)

    kap = units(kap_t)
    rt = units(r_t)
    vh = units(v)
    nu = g * nh
    m = _dotp(jnp.concatenate([kap, rt], 1), jnp.concatenate([units(be_t), units(kd_t)], 1), P_M, _NT)
    a_mat = jnp.where(strict, m[:, :t, :t], 0.0)
    b_mat = jnp.where(strict, m[:, :t, t:], 0.0)
    ab_r = jnp.concatenate([jnp.where(incl, m[:, t:, :t], 0.0), jnp.where(incl, m[:, t:, t:], 0.0)], 2)
    tinv = _unit_tri_inv(a_mat, r64, c64, eye, P_INV)
    w = _dotp(tinv, jnp.concatenate([kap, _dotp(b_mat, vh, P_W)], 2), P_W)
    z = jnp.concatenate([-w, jnp.concatenate([jnp.zeros((nu, t, n), F32), vh], 2)], 1)
    ry = _dotp(ab_r, z, P_Z)
    ry_scr[:, :, :n] = ry[:, :, :n] + rt
    ry_scr[:, :, n:] = ry[:, :, n:]
    pq = _dotp(jnp.concatenate([units(be_h), units(kd_h)], 1), z, P_Z, _TN)
    gd = jnp.where(eye, jnp.broadcast_to(units(g_end, 1), (nu, n, n)), 0.0)
    pq_scr[:, :, :n] = pq[:, :, :n] + gd
    pq_scr[:, :, n:] = pq[:, :, n:]


def _wkv_scan(r, v, kk, lw, be, kd, lc, fwd):
    b, lt, c = r.shape
    tt = WKV_TILE
    nt = lt // tt
    ntc = lc // tt
    d = 0 if fwd else 1

    def tile(i):
        if fwd:
            return i
        return jnp.where(i < ntc, ntc - 1 - i, nt - 1 - (i - ntc))

    t_in = lambda i: tile(jnp.minimum(i, nt - 1))
    t_out = lambda i: tile(jnp.maximum(i - 1, 0))
    one = pl.BlockSpec((1, tt, c), lambda bi, i: (bi, t_in(i), 0))
    two = pl.BlockSpec((1, 1, tt, c), lambda bi, i: (bi, d, t_in(i), 0))
    g = tt // WKV_CHUNK
    return pl.pallas_call(
        functools.partial(_wkv_kernel, fwd=fwd),
        grid=(b, nt + 1),
        in_specs=[one, one, one, two, two, two],
        out_specs=pl.BlockSpec((1, tt, c), lambda bi, i: (bi, t_out(i), 0)),
        out_shape=jax.ShapeDtypeStruct((b, lt, c), F32),
        scratch_shapes=[pltpu.VMEM((RWKV_HEADS, HEAD_DIM, HEAD_DIM), F32),
                        pltpu.VMEM((g * RWKV_HEADS, HEAD_DIM, 2 * HEAD_DIM), F32),
                        pltpu.VMEM((g * RWKV_HEADS, WKV_CHUNK, 2 * HEAD_DIM), F32)],
        compiler_params=_cparams(("parallel", "arbitrary")),
        name="wkv_fwd" if fwd else "wkv_bwd",
    )(r, v, kk, lw, be, kd)


def _rwkv_out_kernel(yf_ref, yb_ref, bon_ref, g_ref, bd_ref, gg_ref, gb_ref, o_ref):
    y = yf_ref[0] + yb_ref[0] + bon_ref[0]
    bd = bd_ref[...]
    mu = _dot01(y, bd) * (1.0 / HEAD_DIM)
    yc = y - mu
    var = _dot01(yc * yc, bd) * (1.0 / HEAD_DIM)
    yn = yc * lax.rsqrt(var + RWKV_GN_EPS) * gg_ref[...] + gb_ref[...]
    o_ref[0] = (yn * g_ref[0]).astype(o_ref.dtype)


def _rwkv_out(yf, yb, bon, g, bd, gg, gb):
    b, lt, c = yf.shape
    tm = TOK_TILE
    tok = pl.BlockSpec((1, tm, c), lambda bi, i: (bi, i, 0))
    full = lambda shape: pl.BlockSpec(shape, lambda bi, i: (0,) * len(shape))
    return pl.pallas_call(
        _rwkv_out_kernel,
        grid=(b, lt // tm),
        in_specs=[tok, tok, tok, tok, full((c, c)), full((1, c)), full((1, c))],
        out_specs=tok,
        out_shape=jax.ShapeDtypeStruct((b, lt, c), BF16),
        compiler_params=_cparams(("parallel", "parallel")),
        name="rwkv_out",
    )(yf, yb, bon, g, bd, gg, gb)


def _hy_prep_kernel(u_ref, up_ref, un_ref, w_ref, b_ref, x1_o, x2_o, v_o, *, lc, lt, tm):
    i = pl.program_id(1)
    u = u_ref[0]
    prev, nxt = _prev_next(u, up_ref[0], un_ref[0], i, tm, lc, lt)
    y = prev * w_ref[0:1] + u * w_ref[1:2] + nxt * w_ref[2:3] + b_ref[...]
    c = HY_WIDTH
    x1_o[0] = y[:, :c]
    x2_o[0] = y[:, c:2 * c]
    v_o[0] = y[:, 2 * c:]


def _hy_prep(uh, w, bias, lc):
    b, lt, cin = uh.shape
    tm = TOK_TILE
    c = HY_WIDTH
    full = lambda shape: pl.BlockSpec(shape, lambda bi, i: (0,) * len(shape))
    tok = pl.BlockSpec((1, tm, c), lambda bi, i: (bi, i, 0))
    s1 = jax.ShapeDtypeStruct((b, lt, c), F32)
    return pl.pallas_call(
        functools.partial(_hy_prep_kernel, lc=lc, lt=lt, tm=tm),
        grid=(b, lt // tm),
        in_specs=_halo_specs(tm, cin, lt) + [full((3, cin)), full((1, cin))],
        out_specs=[tok, tok, tok],
        out_shape=[s1, s1, s1],
        compiler_params=_cparams(("parallel", "parallel")),
        name="hy_prep",
    )(uh, uh, uh, w, bias)


def _hy_filter_kernel(z_ref, w1_ref, b1_ref, f1_ref, w2_ref, b2_ref, f2_ref, w3_ref, dec_ref,
                      fw_o, bw_o, nrm_o, *, tl):
    i = pl.program_id(0)
    z = z_ref[...]
    h = jnp.sin(f1_ref[...] * (_dotp(z, w1_ref[...], 3) + b1_ref[...]))
    h = jnp.sin(f2_ref[...] * (_dotp(h, w2_ref[...], 3) + b2_ref[...]))
    h = _dotp(h, w3_ref[...], 3) * jnp.exp(-z[:, 0:1] * dec_ref[...])
    c = HY_WIDTH
    fw = jnp.concatenate([h[:, 0:c], h[:, 2 * c:3 * c]], 1)
    bw = jnp.concatenate([h[:, c:2 * c], h[:, 3 * c:4 * c]], 1)
    rid = i * tl + lax.broadcasted_iota(jnp.int32, (tl, 1), 0)
    bw = jnp.where(rid == 0, 0.0, bw)
    fw_o[...] = fw
    bw_o[...] = bw

    @pl.when(i == 0)
    def _():
        nrm_o[...] = jnp.zeros(nrm_o.shape, F32)

    nrm_o[...] += jnp.sum(jnp.abs(fw) + jnp.abs(bw), 0, keepdims=True)


def _hy_filter(feat, w1p, b1, f1, w2, b2, f2, w3, dec):
    l, fe = feat.shape
    tl = _pick_tile(l, (512, 256))
    c2 = HY_ORDER * HY_WIDTH
    full = lambda shape: pl.BlockSpec(shape, lambda i: (0,) * len(shape))
    return pl.pallas_call(
        functools.partial(_hy_filter_kernel, tl=tl),
        grid=(l // tl,),
        in_specs=[pl.BlockSpec((tl, fe), lambda i: (i, 0)), full(w1p.shape), full(b1.shape), full(f1.shape),
                  full(w2.shape), full(b2.shape), full(f2.shape), full(w3.shape), full(dec.shape)],
        out_specs=[pl.BlockSpec((tl, c2), lambda i: (i, 0)), pl.BlockSpec((tl, c2), lambda i: (i, 0)),
                   pl.BlockSpec((1, c2), lambda i: (0, 0))],
        out_shape=[jax.ShapeDtypeStruct((l, c2), F32), jax.ShapeDtypeStruct((l, c2), F32),
                   jax.ShapeDtypeStruct((1, c2), F32)],
        compiler_params=_cparams(("arbitrary",)),
        name="hy_filter",
    )(feat, w1p, b1, f1, w2, b2, f2, w3, dec)


def _dft_cols_kernel(f_ref, xa_ref, xb_ref, o_ref, *, n1, pair):
    f = f_ref[...]
    pa = _dotp(f, xa_ref[0], P_DFT)
    pb = _dotp(f, xb_ref[0], P_DFT)
    if pair:
        o_ref[0] = pa[:n1].astype(o_ref.dtype)
        o_ref[1] = pa[n1:].astype(o_ref.dtype)
        o_ref[2] = pb[:n1].astype(o_ref.dtype)
        o_ref[3] = pb[n1:].astype(o_ref.dtype)
    else:
        o_ref[0] = (pa[:n1] - pb[n1:]).astype(o_ref.dtype)
        o_ref[1] = (pb[:n1] + pa[n1:]).astype(o_ref.dtype)


def _dft_cols(fstack, xa, ia, xb, ib, pair):
    n1 = fstack.shape[0] // 2
    _, nh, cols = xa.shape
    tc = _pick_tile(cols, (4096, 2048, 1024, 512, 256, 128))
    no = 4 if pair else 2
    return pl.pallas_call(
        functools.partial(_dft_cols_kernel, n1=n1, pair=pair),
        grid=(cols // tc,),
        in_specs=[pl.BlockSpec(fstack.shape, lambda j: (0, 0)),
                  pl.BlockSpec((1, nh, tc), lambda j: (ia, 0, j)),
                  pl.BlockSpec((1, nh, tc), lambda j: (ib, 0, j))],
        out_specs=pl.BlockSpec((no, n1, tc), lambda j: (0, 0, j)),
        out_shape=jax.ShapeDtypeStruct((no, n1, cols), BF16),
        compiler_params=_cparams(("parallel",)),
        name="dft_cols",
    )(fstack, xa, xb)


def _cplx_left(gs, zr, zi, n):
    c = zr.shape[1]
    p = _dotp(gs, jnp.concatenate([zr, zi], 1), P_DFT)
    return p[:n, :c] - p[n:, c:], p[:n, c:] + p[n:, :c]


def _spec_kernel(a_ref, g_ref, nrm_ref, o_ref, *, n_total):
    n2 = DFT_N2
    gs = jnp.concatenate([g_ref[0, 0], g_ref[0, 1]], 0)
    fr, fi = _cplx_left(gs, a_ref[0, 0], a_ref[1, 0], n2)
    br, bi = _cplx_left(gs, a_ref[2, 0], a_ref[3, 0], n2)
    s = 1.0 / (nrm_ref[...] * n_total)
    o_ref[0, 0] = (fr + br) * s
    o_ref[1, 0] = (fi - bi) * s


def _spec(a4, g, nrm, n_total):
    _, n1, n2, c2 = a4.shape
    return pl.pallas_call(
        functools.partial(_spec_kernel, n_total=float(n_total)),
        grid=(n1,),
        in_specs=[pl.BlockSpec((4, 1, n2, c2), lambda k: (0, k, 0, 0)),
                  pl.BlockSpec((1, 2, n2, n2), lambda k: (k, 0, 0, 0)),
                  pl.BlockSpec((1, c2), lambda k: (0, 0))],
        out_specs=pl.BlockSpec((2, 1, n2, c2), lambda k: (0, k, 0, 0)),
        out_shape=jax.ShapeDtypeStruct((2, n1, n2, c2), F32),
        compiler_params=_cparams(("parallel",)),
        name="hy_spec",
    )(a4, g, nrm)


def _conv_mid_kernel(a_ref, g_ref, gh_ref, k_ref, o_ref):
    n2 = DFT_N2
    gs = jnp.concatenate([g_ref[0, 0], g_ref[0, 1]], 0)
    xr, xi = _cplx_left(gs, a_ref[0, 0], a_ref[1, 0], n2)
    kr = k_ref[0, 0]
    ki = k_ref[1, 0]
    zr = xr * kr - xi * ki
    zi = xr * ki + xi * kr
    ghs = jnp.concatenate([gh_ref[0, 0], gh_ref[0, 1]], 0)
    yr, yi = _cplx_left(ghs, zr, zi, n2)
    o_ref[0, 0] = yr.astype(o_ref.dtype)
    o_ref[1, 0] = yi.astype(o_ref.dtype)


def _conv_mid(a, g, gh, kspec, order):
    _, n1, n2, c = a.shape
    return pl.pallas_call(
        _conv_mid_kernel,
        grid=(n1,),
        in_specs=[pl.BlockSpec((2, 1, n2, c), lambda k: (0, k, 0, 0)),
                  pl.BlockSpec((1, 2, n2, n2), lambda k: (k, 0, 0, 0)),
                  pl.BlockSpec((1, 2, n2, n2), lambda k: (k, 0, 0, 0)),
                  pl.BlockSpec((2, 1, n2, c), lambda k: (0, k, 0, order))],
        out_specs=pl.BlockSpec((2, 1, n2, c), lambda k: (0, k, 0, 0)),
        out_shape=jax.ShapeDtypeStruct((2, n1, n2, c), BF16),
        compiler_params=_cparams(("parallel",)),
        name="hy_conv_mid",
    )(a, g, gh, kspec)


def _idft_cols_kernel(c_ref, b_ref, g0_ref, g1_ref, x0_ref, x1_ref, bias_ref, o_ref, *, nh):
    cs = c_ref[...]
    pr = _dotp(cs, b_ref[0], P_DFT)
    pi = _dotp(cs, b_ref[1], P_DFT)
    yr = pr[:nh] - pi[nh:]
    yi = pi[:nh] + pr[nh:]
    bias = bias_ref[...]
    o_ref[0] = g0_ref[0] * (yr + x0_ref[0] * bias)
    o_ref[1] = g1_ref[0] * (yi + x1_ref[0] * bias)


def _idft_cols(cstack, bv, gate, xin, bias_cols):
    nh2, n1 = cstack.shape
    nh = nh2 // 2
    cols = bv.shape[-1]
    tc = _pick_tile(cols, (4096, 2048, 1024, 512, 256, 128))
    row = lambda bi: pl.BlockSpec((1, nh, tc), lambda j: (bi, 0, j))
    return pl.pallas_call(
        functools.partial(_idft_cols_kernel, nh=nh),
        grid=(cols // tc,),
        in_specs=[pl.BlockSpec((nh2, n1), lambda j: (0, 0)),
                  pl.BlockSpec((2, n1, tc), lambda j: (0, 0, j)),
                  row(0), row(1), row(0), row(1),
                  pl.BlockSpec((1, tc), lambda j: (0, j))],
        out_specs=pl.BlockSpec((2, nh, tc), lambda j: (0, 0, j)),
        out_shape=jax.ShapeDtypeStruct((2, nh, cols), F32),
        compiler_params=_cparams(("parallel",)),
        name="idft_cols",
    )(cstack, bv, gate, gate, xin, xin, bias_cols)


def _hy_ctx_kernel(x1_ref, x2_ref, v_ref, fw_ref, bw_ref, nrm_ref, bias_ref, ff_ref, ci_ref, o_ref, *, lc):
    n = 2 * lc
    c = HY_WIDTH
    ff = ff_ref[...]
    ci = ci_ref[...]
    pf = _mm(ff, fw_ref[...])
    pb = _mm(ff, bw_ref[...])
    s = 1.0 / (nrm_ref[...] * float(n))
    kr = (pf[:n] + pb[:n]) * s
    ki = (pf[n:] - pb[n:]) * s

    def conv(z0, z1, o):
        xr, xi = _cplx_left(ff, z0, z1, n)
        krr = kr[:, o * c:(o + 1) * c]
        kii = ki[:, o * c:(o + 1) * c]
        return _cplx_left(ci, xr * krr - xi * kii, xr * kii + xi * krr, lc)

    v0 = v_ref[0]
    v1 = v_ref[1]
    y0, y1 = conv(v0, v1, 0)
    z0 = x1_ref[0] * (y0 + v0 * bias_ref[0:1])
    z1 = x1_ref[1] * (y1 + v1 * bias_ref[0:1])
    y0, y1 = conv(z0, z1, 1)
    o_ref[0] = x2_ref[0] * (y0 + z0 * bias_ref[1:2])
    o_ref[1] = x2_ref[1] * (y1 + z1 * bias_ref[1:2])


def _hy_ctx(x1, x2, v, fw, bw, nrm, bias, ff, ci):
    b, lc, c = v.shape
    vm = pl.BlockSpec(memory_space=pltpu.VMEM)
    return pl.pallas_call(
        functools.partial(_hy_ctx_kernel, lc=lc),
        in_specs=[vm] * 9,
        out_specs=vm,
        out_shape=jax.ShapeDtypeStruct((b, lc, c), F32),
        compiler_params=pltpu.CompilerParams(vmem_limit_bytes=VMEM_LIMIT),
        name="hy_ctx",
    )(x1, x2, v, fw, bw, nrm, bias, ff, ci)


def _outproj_kernel(attc_ref, attl_ref, rw_ref, hyc_ref, hyl_ref, x_ref, mod_ref, w_ref, lg_ref, lb_ref, o_ref,
                    *, lc, tm):
    row0 = pl.program_id(1) * tm
    a0 = ATT_WIDTH
    a1 = ATT_WIDTH + RWKV_WIDTH
    is_ctx = row0 < lc
    att = jnp.where(is_ctx, attc_ref[0], attl_ref[0])
    hy = jnp.where(is_ctx, hyc_ref[0], hyl_ref[0])
    o = jnp.dot(att, w_ref[:a0], preferred_element_type=F32)
    o += jnp.dot(rw_ref[0], w_ref[a0:a1], preferred_element_type=F32)
    o += jnp.dot(hy.astype(BF16), w_ref[a1:], preferred_element_type=F32)
    g = _sel_mod(mod_ref, 2, row0, tm, lc)
    y = ALPHA * x_ref[0] + g * o
    o_ref[0] = _layer_norm(y) * lg_ref[...] + lb_ref[...]


def _outproj(att_c, att_l, rw, hy_c, hy_l, xx, mod, w, lg, lb, lc):
    b, lt, d = xx.shape
    tm = TOK_TILE
    nct = lc // tm
    tok = lambda c: pl.BlockSpec((1, tm, c), lambda bi, i: (bi, i, 0))
    ctx = lambda c: pl.BlockSpec((1, tm, c), lambda bi, i: (bi, jnp.minimum(i, nct - 1), 0))
    lat = lambda c: pl.BlockSpec((1, tm, c), lambda bi, i: (bi, jnp.maximum(i - nct, 0), 0))
    full = lambda shape: pl.BlockSpec(shape, lambda bi, i: (0,) * len(shape))
    return pl.pallas_call(
        functools.partial(_outproj_kernel, lc=lc, tm=tm),
        grid=(b, lt // tm),
        in_specs=[ctx(ATT_WIDTH), lat(ATT_WIDTH), tok(RWKV_WIDTH), ctx(HY_WIDTH), lat(HY_WIDTH), tok(d),
                  pl.BlockSpec((1, 2, 6, d), lambda bi, i: (bi, 0, 0, 0)),
                  full(w.shape), full((1, d)), full((1, d))],
        out_specs=tok(d),
        out_shape=jax.ShapeDtypeStruct((b, lt, d), F32),
        compiler_params=_cparams(("parallel", "parallel")),
        name="outproj",
    )(att_c, att_l, rw, hy_c, hy_l, xx, mod, w, lg, lb)


def _ffn_kernel(x_ref, mod_ref, w1_ref, w3_ref, w2_ref, lg_ref, lb_ref, o_ref, h_scr, acc_scr, *, lc, tm, nf):
    row0 = pl.program_id(1) * tm
    f = pl.program_id(2)

    @pl.when(f == 0)
    def _():
        sh = _sel_mod(mod_ref, 3, row0, tm, lc)
        sc = _sel_mod(mod_ref, 4, row0, tm, lc)
        h_scr[...] = (_layer_norm(x_ref[0]) * (1.0 + sc) + sh).astype(BF16)
        acc_scr[...] = jnp.zeros(acc_scr.shape, F32)

    h = h_scr[...]
    a = jnp.dot(h, w1_ref[...], preferred_element_type=F32)
    g = jnp.dot(h, w3_ref[...], preferred_element_type=F32)
    acc_scr[...] += jnp.dot((_silu(a) * g).astype(BF16), w2_ref[...], preferred_element_type=F32)

    @pl.when(f == nf - 1)
    def _():
        gate = _sel_mod(mod_ref, 5, row0, tm, lc)
        y = ALPHA * x_ref[0] + gate * acc_scr[...]
        o_ref[0] = _layer_norm(y) * lg_ref[...] + lb_ref[...]


def _ffn(xx, mod, w1, w3, w2, lg, lb, lc):
    b, lt, d = xx.shape
    ff = w1.shape[1]
    tm = _pick_tile(lt, (1280, 768, 512, 256))
    tf = _pick_tile(ff, (256, 128))
    nf = ff // tf
    return pl.pallas_call(
        functools.partial(_ffn_kernel, lc=lc, tm=tm, nf=nf),
        grid=(b, lt // tm, nf),
        in_specs=[pl.BlockSpec((1, tm, d), lambda bi, i, f: (bi, i, 0)),
                  pl.BlockSpec((1, 2, 6, d), lambda bi, i, f: (bi, 0, 0, 0)),
                  pl.BlockSpec((d, tf), lambda bi, i, f: (0, f)),
                  pl.BlockSpec((d, tf), lambda bi, i, f: (0, f)),
                  pl.BlockSpec((tf, d), lambda bi, i, f: (f, 0)),
                  pl.BlockSpec((1, d), lambda bi, i, f: (0, 0)),
                  pl.BlockSpec((1, d), lambda bi, i, f: (0, 0))],
        out_specs=pl.BlockSpec((1, tm, d), lambda bi, i, f: (bi, i, 0)),
        out_shape=jax.ShapeDtypeStruct((b, lt, d), F32),
        scratch_shapes=[pltpu.VMEM((tm, d), BF16), pltpu.VMEM((tm, d), F32)],
        compiler_params=_cparams(("parallel", "parallel", "arbitrary")),
        name="ffn",
    )(xx, mod, w1, w3, w2, lg, lb)


def _route(x_ref, mod_ref, wr_ref, row0, tm, lc):
    sh = _sel_mod(mod_ref, 3, row0, tm, lc)
    sc = _sel_mod(mod_ref, 4, row0, tm, lc)
    h = _layer_norm(x_ref[0]) * (1.0 + sc) + sh
    logits = _dotp(h, wr_ref[...], 3)
    lane = lax.broadcasted_iota(jnp.int32, logits.shape, 1)
    neg = jnp.float32(-jnp.inf)
    lg = jnp.where(lane < N_EXPERTS, logits, neg)
    m1 = jnp.max(lg, -1, keepdims=True)
    i1 = jnp.min(jnp.where(lg == m1, lane, LANES), -1, keepdims=True)
    lg2 = jnp.where(lane == i1, neg, lg)
    m2 = jnp.max(lg2, -1, keepdims=True)
    i2 = jnp.min(jnp.where(lg2 == m2, lane, LANES), -1, keepdims=True)
    e2 = jnp.exp(m2 - m1)
    return h, lane, i1, i2, 1.0 / (1.0 + e2), e2 / (1.0 + e2)


def _moe_kernel(x_ref, mod_ref, wr_ref, w1_ref, w3_ref, w2_ref, lg_ref, lb_ref, o_ref,
                h_scr, acc_scr, gate_scr, *, lc, tm, nf, ns):
    row0 = pl.program_id(1) * tm
    s = pl.program_id(2)

    @pl.when(s == 0)
    def _():
        h, _, i1, i2, g1, g2 = _route(x_ref, mod_ref, wr_ref, row0, tm, lc)
        h_scr[...] = h.astype(BF16)
        acc_scr[...] = jnp.zeros(acc_scr.shape, F32)
        for e in range(N_EXPERTS):
            gate_scr[e] = jnp.where(i1 == e, g1, 0.0) + jnp.where(i2 == e, g2, 0.0)

    h = h_scr[...]
    a = jnp.dot(h, w1_ref[0], preferred_element_type=F32)
    g = jnp.dot(h, w3_ref[0], preferred_element_type=F32)
    act = _silu(a) * g * gate_scr[s // nf]
    acc_scr[...] += jnp.dot(act.astype(BF16), w2_ref[0], preferred_element_type=F32)

    @pl.when(s == ns - 1)
    def _():
        gate = _sel_mod(mod_ref, 5, row0, tm, lc)
        y = ALPHA * x_ref[0] + gate * acc_scr[...]
        o_ref[0] = _layer_norm(y) * lg_ref[...] + lb_ref[...]


def _moe(xx, mod, wr, w1, w3, w2, lg, lb, lc):
    b, lt, d = xx.shape
    ne, _, ff = w1.shape
    tm = _pick_tile(lt, (1280, 768, 512, 256))
    tf = _pick_tile(ff, (256, 128))
    nf = ff // tf
    ns = ne * nf
    return pl.pallas_call(
        functools.partial(_moe_kernel, lc=lc, tm=tm, nf=nf, ns=ns),
        grid=(b, lt // tm, ns),
        in_specs=[pl.BlockSpec((1, tm, d), lambda bi, i, s: (bi, i, 0)),
                  pl.BlockSpec((1, 2, 6, d), lambda bi, i, s: (bi, 0, 0, 0)),
                  pl.BlockSpec((d, LANES), lambda bi, i, s: (0, 0)),
                  pl.BlockSpec((1, d, tf), lambda bi, i, s: (s // nf, 0, s % nf)),
                  pl.BlockSpec((1, d, tf), lambda bi, i, s: (s // nf, 0, s % nf)),
                  pl.BlockSpec((1, tf, d), lambda bi, i, s: (s // nf, s % nf, 0)),
                  pl.BlockSpec((1, d), lambda bi, i, s: (0, 0)),
                  pl.BlockSpec((1, d), lambda bi, i, s: (0, 0))],
        out_specs=pl.BlockSpec((1, tm, d), lambda bi, i, s: (bi, i, 0)),
        out_shape=jax.ShapeDtypeStruct((b, lt, d), F32),
        scratch_shapes=[pltpu.VMEM((tm, d), BF16), pltpu.VMEM((tm, d), F32),
                        pltpu.VMEM((ne, tm, 1), F32)],
        compiler_params=_cparams(("parallel", "parallel", "arbitrary")),
        name="moe",
    )(xx, mod, wr, w1, w3, w2, lg, lb)


def _moe_count_kernel(x_ref, mod_ref, wr_ref, o_ref, *, lc, tm):
    row0 = pl.program_id(1) * tm
    _, lane, i1, i2, _, _ = _route(x_ref, mod_ref, wr_ref, row0, tm, lc)
    routed = jnp.where(jnp.logical_or(lane == i1, lane == i2), 1.0, 0.0)
    o_ref[0, 0] = jnp.sum(routed, 0, keepdims=True)


def _moe_counts(xx, mod, wr, tm, lc):
    b, lt, d = xx.shape
    return pl.pallas_call(
        functools.partial(_moe_count_kernel, lc=lc, tm=tm),
        grid=(b, lt // tm),
        in_specs=[pl.BlockSpec((1, tm, d), lambda bi, i: (bi, i, 0)),
                  pl.BlockSpec((1, 2, 6, d), lambda bi, i: (bi, 0, 0, 0)),
                  pl.BlockSpec((d, LANES), lambda bi, i: (0, 0))],
        out_specs=pl.BlockSpec((1, 1, 1, LANES), lambda bi, i: (bi, i, 0, 0)),
        out_shape=jax.ShapeDtypeStruct((b, lt // tm, 1, LANES), F32),
        compiler_params=_cparams(("parallel", "parallel")),
        name="moe_counts",
    )(xx, mod, wr)


def _moe_sparse_kernel(x_ref, mod_ref, wr_ref, w1_ref, w3_ref, w2_ref, lg_ref, lb_ref, o_ref,
                       h_scr, acc_scr, gate_scr, posc_scr, posr_scr, xs_scr, ye_scr, *, lc, tm, nf, ns, cap):
    row0 = pl.program_id(1) * tm
    s = pl.program_id(2)
    e = s // nf
    f = s % nf

    @pl.when(s == 0)
    def _():
        h, lane, i1, i2, g1, g2 = _route(x_ref, mod_ref, wr_ref, row0, tm, lc)
        h_scr[...] = h.astype(BF16)
        acc_scr[...] = jnp.zeros(acc_scr.shape, F32)
        routed = jnp.where(jnp.logical_or(lane == i1, lane == i2), 1.0, 0.0)
        r = lax.broadcasted_iota(jnp.int32, (tm, tm), 0)
        c = lax.broadcasted_iota(jnp.int32, (tm, tm), 1)
        before = jnp.where(c < r, 1.0, 0.0).astype(BF16)
        rank_c = jnp.dot(before, routed.astype(BF16), preferred_element_type=F32)
        gate_scr[...] = jnp.where(lane == i1, g1, 0.0) + jnp.where(lane == i2, g2, 0.0)
        posc_scr[...] = jnp.where(routed > 0.0, rank_c, -1.0)
        routed_t = routed.T[:2 * SUBLANES]
        after = jnp.where(r < c, 1.0, 0.0).astype(BF16)
        rank_r = jnp.dot(routed_t.astype(BF16), after, preferred_element_type=F32)
        posr_scr[...] = jnp.where(routed_t > 0.0, rank_r, -1.0)

    @pl.when(f == 0)
    def _():
        slot = lax.broadcasted_iota(jnp.int32, (cap, tm), 0).astype(F32)
        take = jnp.where(slot == posr_scr[pl.ds(e, 1), :], 1.0, 0.0).astype(BF16)
        xs_scr[...] = jnp.dot(take, h_scr[...], preferred_element_type=F32).astype(BF16)
        ye_scr[...] = jnp.zeros(ye_scr.shape, F32)

    xs = xs_scr[...]
    a = jnp.dot(xs, w1_ref[0], preferred_element_type=F32)
    g = jnp.dot(xs, w3_ref[0], preferred_element_type=F32)
    ye_scr[...] += jnp.dot((_silu(a) * g).astype(BF16), w2_ref[0], preferred_element_type=F32)

    @pl.when(f == nf - 1)
    def _():
        lane = lax.broadcasted_iota(jnp.int32, (tm, LANES), 1)
        column = lambda ref: jnp.sum(jnp.where(lane == e, ref[...], 0.0), -1, keepdims=True)
        slot = lax.broadcasted_iota(jnp.int32, (tm, cap), 1).astype(F32)
        put = jnp.where(slot == column(posc_scr), 1.0, 0.0).astype(BF16)
        y = jnp.dot(put, ye_scr[...].astype(BF16), preferred_element_type=F32)
        acc_scr[...] += column(gate_scr) * y

    @pl.when(s == ns - 1)
    def _():
        gate = _sel_mod(mod_ref, 5, row0, tm, lc)
        y = ALPHA * x_ref[0] + gate * acc_scr[...]
        o_ref[0] = _layer_norm(y) * lg_ref[...] + lb_ref[...]


def _moe_sparse(xx, mod, wr, w1, w3, w2, lg, lb, lc, tm, cap):
    b, lt, d = xx.shape
    ne, _, ff = w1.shape
    tf = _pick_tile(ff, (256, 128))
    nf = ff // tf
    ns = ne * nf
    return pl.pallas_call(
        functools.partial(_moe_sparse_kernel, lc=lc, tm=tm, nf=nf, ns=ns, cap=cap),
        grid=(b, lt // tm, ns),
        in_specs=[pl.BlockSpec((1, tm, d), lambda bi, i, s: (bi, i, 0)),
                  pl.BlockSpec((1, 2, 6, d), lambda bi, i, s: (bi, 0, 0, 0)),
                  pl.BlockSpec((d, LANES), lambda bi, i, s: (0, 0)),
                  pl.BlockSpec((1, d, tf), lambda bi, i, s: (s // nf, 0, s % nf)),
                  pl.BlockSpec((1, d, tf), lambda bi, i, s: (s // nf, 0, s % nf)),
                  pl.BlockSpec((1, tf, d), lambda bi, i, s: (s // nf, s % nf, 0)),
                  pl.BlockSpec((1, d), lambda bi, i, s: (0, 0)),
                  pl.BlockSpec((1, d), lambda bi, i, s: (0, 0))],
        out_specs=pl.BlockSpec((1, tm, d), lambda bi, i, s: (bi, i, 0)),
        out_shape=jax.ShapeDtypeStruct((b, lt, d), F32),
        scratch_shapes=[pltpu.VMEM((tm, d), BF16), pltpu.VMEM((tm, d), F32),
                        pltpu.VMEM((tm, LANES), F32), pltpu.VMEM((tm, LANES), F32),
                        pltpu.VMEM((2 * SUBLANES, tm), F32),
                        pltpu.VMEM((cap, d), BF16), pltpu.VMEM((cap, d), F32)],
        compiler_params=_cparams(("parallel", "parallel", "arbitrary")),
        name="moe_sparse",
    )(xx, mod, wr, w1, w3, w2, lg, lb)


def _rope_tables(l, lc):
    rows = l // GRID_W
    row = jnp.repeat(jnp.arange(rows, dtype=F32), GRID_W)
    col = jnp.tile(jnp.arange(GRID_W, dtype=F32), rows)
    n_freq = HEAD_DIM // 4
    inv_freq = ROPE_THETA ** (-jnp.arange(n_freq, dtype=F32) / n_freq)
    ang = jnp.concatenate([row[:, None] * inv_freq, col[:, None] * inv_freq], -1)
    cos, sin = jnp.cos(ang), jnp.sin(ang)
    cos2 = jnp.concatenate([jnp.ones((lc, LANES), F32), jnp.concatenate([cos, cos, cos, cos], -1)], 0)
    sin2 = jnp.concatenate([jnp.zeros((lc, LANES), F32), jnp.concatenate([-sin, sin, -sin, sin], -1)], 0)
    return cos2, sin2


def _hy_features(l):
    bands = (HY_EMB - 1) // 2
    t = jnp.linspace(0.0, 1.0, l, dtype=F32)[:, None]
    f = jnp.linspace(1e-4, bands - 1, bands, dtype=F32)[None, :]
    wt = 2.0 * math.pi * jnp.arange(l, dtype=F32)[:, None] / l
    z = jnp.concatenate([t, jnp.cos(f * wt), -jnp.sin(f * wt)], -1)
    return jnp.pad(z, ((0, 0), (0, LANES - HY_EMB)))


def _angle(idx, n):
    return (2.0 * math.pi / n) * (idx % n).astype(F32)


def _dft_tables(n1):
    nh = n1 // 2
    n2 = DFT_N2
    n = n1 * n2
    k1 = jnp.arange(n1, dtype=jnp.int32)
    a1 = _angle(k1[:, None] * jnp.arange(nh, dtype=jnp.int32)[None, :], n1)
    fstack = jnp.concatenate([jnp.cos(a1), -jnp.sin(a1)], 0)
    cstack = jnp.concatenate([jnp.cos(a1.T), jnp.sin(a1.T)], 0)
    k2 = jnp.arange(n2, dtype=jnp.int32)
    at = _angle(k1[:, None] * k2[None, :], n)
    tr, ti = jnp.cos(at), -jnp.sin(at)
    a2 = _angle(k2[:, None] * k2[None, :], n2)
    fr, fi = jnp.cos(a2), -jnp.sin(a2)
    g = jnp.stack([tr[:, None, :] * fr[None] - ti[:, None, :] * fi[None],
                   tr[:, None, :] * fi[None] + ti[:, None, :] * fr[None]], 1)
    gh = jnp.stack([tr[:, :, None] * fr[None] - ti[:, :, None] * fi[None],
                    -(tr[:, :, None] * fi[None] + ti[:, :, None] * fr[None])], 1)
    return fstack, cstack, g, gh


def _dense_dft_tables(lc):
    n = 2 * lc
    a = _angle(jnp.arange(n, dtype=jnp.int32)[:, None] * jnp.arange(lc, dtype=jnp.int32)[None, :], n)
    ff = jnp.concatenate([jnp.cos(a), -jnp.sin(a)], 0)
    ci = jnp.concatenate([jnp.cos(a.T), jnp.sin(a.T)], 0)
    return ff, ci


def kernel(x, c, ctx, c_ctx, ada_w, ada_b, w_in, w_out, q_gain, k_gain, rwkv_mu, rwkv_w0, rwkv_wB, rwkv_a0, rwkv_aB, rwkv_gB, rwkv_kk, rwkv_ka, rwkv_rk, rwkv_gn_g, rwkv_gn_b, hy_short_w, hy_short_b, hy_w1, hy_b1, hy_freq1, hy_w2, hy_b2, hy_freq2, hy_w3, hy_decay, hy_bias, ln1_g, ln1_b, ln2_g, ln2_b, ffn_w1, ffn_w3, ffn_w2, moe_router, moe_w1, moe_w3, moe_w2):
    b, l, d = x.shape
    lc = ctx.shape[1]
    lt = lc + l
    depth = ada_w.shape[0]
    assert b == 2, "the long convolution packs the two batch rows as one complex signal"
    assert d == D_MODEL and lc % TOK_TILE == 0 and l % TOK_TILE == 0 and (2 * l) % (2 * DFT_N2) == 0
    cw = RWKV_WIDTH

    xx = jnp.concatenate([ctx, x], 1)
    cond8 = jnp.zeros((SUBLANES, d), F32).at[:b].set(c).at[b].set(c_ctx)
    mod_all = _ada_mod(cond8, ada_w, ada_b)

    cos64, sin64 = _rope_tables(l, lc)
    n1 = 2 * l // DFT_N2
    nh = n1 // 2
    cols = DFT_N2 * HY_WIDTH
    fstack, cstack, g_tab, gh_tab = (t.astype(BF16) for t in _dft_tables(n1))
    ff_c, ci_c = _dense_dft_tables(lc)
    feat_l = _hy_features(l)
    feat_c = _hy_features(lc)
    blk = jnp.arange(cw) // HEAD_DIM
    bd = (blk[:, None] == blk[None, :]).astype(F32)
    ch = jnp.arange(ATT_WIDTH)
    bd_att = (ch[:, None] // HEAD_DIM == ch[None, :] // HEAD_DIM).astype(BF16)
    swap_att = (ch[:, None] == (ch[None, :] + HALF_HD) % HEAD_DIM + (ch[None, :] // HEAD_DIM) * HEAD_DIM
                ).astype(BF16)
    perm64 = jnp.concatenate([jnp.arange(0, HEAD_DIM, 2), jnp.arange(1, HEAD_DIM, 2)])
    perm_att = jnp.concatenate([h * HEAD_DIM + perm64 for h in range(ATT_HEADS + ATT_KV_HEADS)]
                               + [jnp.arange(ATT_WIDTH + ATT_KV_WIDTH, IN_ATT)])
    tq = TOK_TILE
    qn = next(n for n in (4, 2, 1) if (l // tq) % n == 0)
    tk = _pick_tile(lt, (3328, 1280, 1024, 768, 512, 256))

    for li in range(depth):
        ml = mod_all[li]
        mod = jnp.stack([jnp.broadcast_to(ml[b].reshape(1, 6, d), (b, 6, d)), ml[:b].reshape(b, 6, d)], 1)
        wi = w_in[li]
        w_pad = jnp.concatenate([wi[:, :IN_ATT][:, perm_att], wi[:, IN_ATT:IN_ATT + IN_RWKV],
                                 jnp.zeros((d, IN_RWKV_PAD - IN_RWKV), F32), wi[:, IN_ATT + IN_RWKV:]],
                                1).astype(BF16)
        ua, ur, uh = _inproj(xx, mod, w_pad, lc)

        two = lambda gain: jnp.tile(gain[perm64], 2)[None]
        qt, kx, vt = _attn_prep(ua, cos64, sin64, two(q_gain[li]), two(k_gain[li]), bd_att, swap_att, lc)
        att_c = _flash(qt, kx, vt, tq, TOK_TILE, 1, l // tq, lc // tq, lc // TOK_TILE)
        s_bound = (HEAD_DIM ** 0.5) * LOG2E * jnp.max(jnp.abs(q_gain[li])) * jnp.max(jnp.abs(k_gain[li]))
        lat_args = (qt, kx, vt, tq, tk, qn, 0, l // (qn * tq), lt // tk)
        att_l = lax.cond(s_bound <= MAX_UNSHIFTED_SCORE,
                         lambda: _flash(*lat_args, bounded=True), lambda: _flash(*lat_args, bounded=False))

        wl = jnp.zeros((LANES, 5 * cw), F32)
        wl = wl.at[0:W_LORA, 0:cw].set(rwkv_wB[li, 0]).at[W_LORA:2 * W_LORA, cw:2 * cw].set(rwkv_wB[li, 1])
        o_a = 2 * W_LORA
        wl = wl.at[o_a:o_a + A_LORA, 2 * cw:3 * cw].set(rwkv_aB[li, 0])
        wl = wl.at[o_a + A_LORA:o_a + 2 * A_LORA, 3 * cw:4 * cw].set(rwkv_aB[li, 1])
        o_g = o_a + 2 * A_LORA
        wl = wl.at[o_g:o_g + G_LORA, 4 * cw:5 * cw].set(rwkv_gB[li])
        mu = jnp.pad(rwkv_mu[li], ((0, 0), (0, IN_RWKV_PAD - IN_RWKV)))
        r_, v_, kk_, g_, bon_, lw_, be_, kd_ = _rwkv_prep(
            ur, mu, wl, bd, rwkv_w0[li], rwkv_a0[li], rwkv_kk[li][None], rwkv_ka[li][None],
            rwkv_rk[li].reshape(1, cw), lc)
        yf = _wkv_scan(r_, v_, kk_, lw_, be_, kd_, lc, True)
        yb = _wkv_scan(r_, v_, kk_, lw_, be_, kd_, lc, False)
        rw = _rwkv_out(yf, yb, bon_, g_, bd, rwkv_gn_g[li][None], rwkv_gn_b[li][None])

        x1, x2, vv = _hy_prep(uh, hy_short_w[li], hy_short_b[li][None], lc)
        w1p = jnp.pad(hy_w1[li], ((0, LANES - HY_EMB), (0, 0)))
        fargs = (w1p, hy_b1[li][None], hy_freq1[li][None], hy_w2[li], hy_b2[li][None], hy_freq2[li][None],
                 hy_w3[li], hy_decay[li][None])
        fw, bw, nrm = _hy_filter(feat_l, *fargs)
        c2 = HY_ORDER * HY_WIDTH
        a4 = _dft_cols(fstack, fw.reshape(1, nh, DFT_N2 * c2), 0, bw.reshape(1, nh, DFT_N2 * c2), 0, True)
        kspec = _spec(a4.reshape(4, n1, DFT_N2, c2), g_tab, nrm, n1 * DFT_N2)
        lat = lambda t: t[:, lc:].reshape(b, nh, cols)
        x1l, x2l, zin = lat(x1), lat(x2), lat(vv)
        for o, gate in enumerate((x1l, x2l)):
            a = _dft_cols(fstack, zin, 0, zin, 1, False)
            bv = _conv_mid(a.reshape(2, n1, DFT_N2, HY_WIDTH), g_tab, gh_tab, kspec, o)
            bias_cols = jnp.tile(hy_bias[li, o], DFT_N2)[None]
            zin = _idft_cols(cstack, bv.reshape(2, n1, cols), gate, zin, bias_cols)
        hy_l = zin.reshape(b, l, HY_WIDTH)
        fw_c, bw_c, nrm_c = _hy_filter(feat_c, *fargs)
        hy_c = _hy_ctx(x1[:, :lc], x2[:, :lc], vv[:, :lc], fw_c, bw_c, nrm_c, hy_bias[li], ff_c, ci_c)

        xx = _outproj(att_c, att_l, rw, hy_c, hy_l, xx, mod, w_out[li].astype(BF16),
                      ln1_g[li][None], ln1_b[li][None], lc)

        j = li // 2
        if li % 2 == 0:
            xx = _ffn(xx, mod, ffn_w1[j].astype(BF16), ffn_w3[j].astype(BF16), ffn_w2[j].astype(BF16),
                      ln2_g[li][None], ln2_b[li][None], lc)
        else:
            wr = jnp.pad(moe_router[j], ((0, 0), (0, LANES - N_EXPERTS)))
            margs = (xx, mod, wr, moe_w1[j].astype(BF16), moe_w3[j].astype(BF16), moe_w2[j].astype(BF16),
                     ln2_g[li][None], ln2_b[li][None], lc)
            tm_moe = _pick_tile(lt, (1280, 768, 512, 256))
            cap = -(-(tm_moe * 2 // 5) // LANES) * LANES
            fits = jnp.max(_moe_counts(xx, mod, wr, tm_moe, lc)) <= cap
            xx = lax.cond(fits, lambda: _moe_sparse(*margs, tm_moe, cap), lambda: _moe(*margs))
    return xx[:, lc:]
```

```python
import functools
import math

import jax
import jax.numpy as jnp
from jax import lax
from jax.experimental import pallas as pl
from jax.experimental.pallas import tpu as pltpu

F32 = jnp.float32
BF16 = jnp.bfloat16
HI = lax.Precision.HIGHEST

D_MODEL = 1024
DEPTH = 2
GRID_W = 64
HEAD_DIM = 64
HALF_HD = HEAD_DIM // 2
ATT_WIDTH = 512
RWKV_WIDTH = 256
HY_WIDTH = 256
ATT_HEADS = 8
ATT_KV_HEADS = 2
ATT_REP = 4
ATT_KV_WIDTH = 128
ROPE_THETA = 10000.0
QK_EPS = 1e-6
RWKV_HEADS = 4
W_LORA = 16
A_LORA = 16
G_LORA = 32
RWKV_GN_EPS = 64e-5
HY_ORDER = 2
HY_EMB = 33
HY_FFN = 64
N_EXPERTS = 8
LN_EPS = 1e-6
IN_ATT = ATT_WIDTH + 2 * ATT_KV_WIDTH
IN_RWKV = 3 * RWKV_WIDTH + 2 * W_LORA + 2 * A_LORA + G_LORA
IN_RWKV_PAD = 896
IN_HY = 3 * HY_WIDTH
ALPHA = float((2 * DEPTH) ** 0.25)
LOG2E = 1.4426950408889634
MAX_UNSHIFTED_SCORE = 40.0

LANES = 128
SUBLANES = 8
TOK_TILE = 256
WKV_CHUNK = 64
WKV_TILE = 256
DFT_N2 = 256
MOE_ROW_BLOCK = 256
VMEM_LIMIT = 48 * 1024 * 1024
VMEM_LIMIT_MOE = 56 * 1024 * 1024


def _cparams(sem, vmem=VMEM_LIMIT):
    return pltpu.CompilerParams(dimension_semantics=sem, vmem_limit_bytes=vmem)


def _pick_tile(n, cands):
    for c in cands:
        if n % c == 0:
            return c
    raise ValueError(f"no tile for {n} in {cands}")


def _layer_norm(x):
    mu = jnp.mean(x, -1, keepdims=True)
    xc = x - mu
    var = jnp.mean(xc * xc, -1, keepdims=True)
    return xc * lax.rsqrt(var + LN_EPS)


def _sel_mod(mod_ref, j, row0, tm, lc):
    rid = row0 + lax.broadcasted_iota(jnp.int32, (tm, 1), 0)
    return jnp.where(rid < lc, mod_ref[0, 0, j:j + 1, :], mod_ref[0, 1, j:j + 1, :])


def _silu(x):
    return x * jax.nn.sigmoid(x)


def _ada_kernel(c_ref, w_ref, b_ref, o_ref):
    s = _silu(c_ref[...])
    o_ref[0] = jnp.dot(s, w_ref[0], precision=HI, preferred_element_type=F32) + b_ref[0]


def _ada_mod(cond8, ada_w, ada_b):
    depth, d, n = ada_w.shape
    tn = _pick_tile(n, (1536, 1024, 512, 256, 128))
    return pl.pallas_call(
        _ada_kernel,
        grid=(depth, n // tn),
        in_specs=[pl.BlockSpec((SUBLANES, d), lambda l, j: (0, 0)),
                  pl.BlockSpec((1, d, tn), lambda l, j: (l, 0, j)),
                  pl.BlockSpec((1, 1, tn), lambda l, j: (l, 0, j))],
        out_specs=pl.BlockSpec((1, SUBLANES, tn), lambda l, j: (l, 0, j)),
        out_shape=jax.ShapeDtypeStruct((depth, SUBLANES, n), F32),
        compiler_params=_cparams(("parallel", "parallel")),
        name="ada_mod",
    )(cond8, ada_w, ada_b.reshape(depth, 1, n))


def _inproj_kernel(x_ref, mod_ref, w_ref, oa_ref, or_ref, oh_ref, *, lc, tm):
    row0 = pl.program_id(1) * tm
    sh = _sel_mod(mod_ref, 0, row0, tm, lc)
    sc = _sel_mod(mod_ref, 1, row0, tm, lc)
    h = (_layer_norm(x_ref[0]) * (1.0 + sc) + sh).astype(BF16)
    u = jnp.dot(h, w_ref[...], preferred_element_type=F32)
    oa_ref[0] = u[:, :IN_ATT]
    or_ref[0] = u[:, IN_ATT:IN_ATT + IN_RWKV_PAD]
    oh_ref[0] = u[:, IN_ATT + IN_RWKV_PAD:]


def _inproj(xx, mod, w_pad, lc):
    b, lt, d = xx.shape
    tm = TOK_TILE
    n = w_pad.shape[1]
    return pl.pallas_call(
        functools.partial(_inproj_kernel, lc=lc, tm=tm),
        grid=(b, lt // tm),
        in_specs=[pl.BlockSpec((1, tm, d), lambda bi, i: (bi, i, 0)),
                  pl.BlockSpec((1, 2, 6, d), lambda bi, i: (bi, 0, 0, 0)),
                  pl.BlockSpec((d, n), lambda bi, i: (0, 0))],
        out_specs=[pl.BlockSpec((1, tm, IN_ATT), lambda bi, i: (bi, i, 0)),
                   pl.BlockSpec((1, tm, IN_RWKV_PAD), lambda bi, i: (bi, i, 0)),
                   pl.BlockSpec((1, tm, IN_HY), lambda bi, i: (bi, i, 0))],
        out_shape=[jax.ShapeDtypeStruct((b, lt, IN_ATT), F32),
                   jax.ShapeDtypeStruct((b, lt, IN_RWKV_PAD), F32),
                   jax.ShapeDtypeStruct((b, lt, IN_HY), F32)],
        compiler_params=_cparams(("parallel", "parallel")),
        name="inproj",
    )(xx, mod, w_pad)


def _attn_prep_kernel(u_ref, cos_ref, sin_ref, qg_ref, kg_ref, bd_ref, sw_ref, qt_ref, k_ref, vt_ref):
    u = u_ref[0]

    def norm_rope(x, g):
        w = x.shape[1]
        tile = lambda t: jnp.concatenate([t] * (w // LANES), -1)
        ms = _dot01(x * x, bd_ref[:w, :w]) * (1.0 / HEAD_DIM)
        xn = x * lax.rsqrt(ms + QK_EPS) * tile(g)
        sw = _dot01(xn, sw_ref[:w, :w])
        return xn * tile(cos_ref[...]) + sw * tile(sin_ref[...])

    q = norm_rope(u[:, :ATT_WIDTH], qg_ref[...]) * (LOG2E * HEAD_DIM ** -0.5)
    qt = q.T
    kx = norm_rope(u[:, ATT_WIDTH:ATT_WIDTH + ATT_KV_WIDTH], kg_ref[...])
    for g in range(ATT_KV_HEADS):
        base = g * ATT_REP * HEAD_DIM
        qt_ref[0, g, 0] = jnp.concatenate(
            [qt[base + r * HEAD_DIM:base + (r + 1) * HEAD_DIM] for r in range(ATT_REP)], -1).astype(BF16)
        k_ref[0, g] = kx[:, g * HEAD_DIM:(g + 1) * HEAD_DIM].astype(BF16)
    v0 = ATT_WIDTH + ATT_KV_WIDTH
    vt = u[:, v0:v0 + ATT_KV_WIDTH].T
    for g in range(ATT_KV_HEADS):
        vt_ref[0, g] = vt[g * HEAD_DIM:(g + 1) * HEAD_DIM].astype(BF16)


def _attn_prep(ua, cos, sin, qg, kg, bd_att, swap_att, lc):
    b, lt, _ = ua.shape
    tm = TOK_TILE
    nct = lc // tm
    nl = lt // tm - nct
    q_pos = lambda i: jnp.where(i < nct, nl + i, i - nct)
    return pl.pallas_call(
        _attn_prep_kernel,
        grid=(b, lt // tm),
        in_specs=[pl.BlockSpec((1, tm, IN_ATT), lambda bi, i: (bi, i, 0)),
                  pl.BlockSpec((tm, LANES), lambda bi, i: (i, 0)),
                  pl.BlockSpec((tm, LANES), lambda bi, i: (i, 0)),
                  pl.BlockSpec((1, LANES), lambda bi, i: (0, 0)),
                  pl.BlockSpec((1, LANES), lambda bi, i: (0, 0)),
                  pl.BlockSpec((ATT_WIDTH, ATT_WIDTH), lambda bi, i: (0, 0)),
                  pl.BlockSpec((ATT_WIDTH, ATT_WIDTH), lambda bi, i: (0, 0))],
        out_specs=[pl.BlockSpec((1, ATT_KV_HEADS, 1, HEAD_DIM, ATT_REP * tm),
                                lambda bi, i: (bi, 0, q_pos(i), 0, 0)),
                   pl.BlockSpec((1, ATT_KV_HEADS, tm, HEAD_DIM), lambda bi, i: (bi, 0, i, 0)),
                   pl.BlockSpec((1, ATT_KV_HEADS, HEAD_DIM, tm), lambda bi, i: (bi, 0, 0, i))],
        out_shape=[jax.ShapeDtypeStruct((b, ATT_KV_HEADS, lt // tm, HEAD_DIM, ATT_REP * tm), BF16),
                   jax.ShapeDtypeStruct((b, ATT_KV_HEADS, lt, HEAD_DIM), BF16),
                   jax.ShapeDtypeStruct((b, ATT_KV_HEADS, HEAD_DIM, lt), BF16)],
        compiler_params=_cparams(("parallel", "parallel")),
        name="attn_prep",
    )(ua, cos, sin, qg, kg, bd_att, swap_att)


def _flash_kernel(qt_ref, k_ref, vt_ref, o_ref, m_scr, l_scr, acc_scr, *, nk, tq, sub, qn):
    j = pl.program_id(3)

    @pl.when(j == 0)
    def _():
        m_scr[...] = jnp.full(m_scr.shape, -jnp.inf, F32)
        l_scr[...] = jnp.zeros(l_scr.shape, F32)
        acc_scr[...] = jnp.zeros(acc_scr.shape, F32)

    qt = jnp.concatenate([qt_ref[0, 0, t] for t in range(qn)], -1)
    nsub = k_ref.shape[2] // sub
    m = m_scr[...]
    l = l_scr[...]
    acc = acc_scr[...]
    scores = lambda c: jnp.dot(k_ref[0, 0, c * sub:(c + 1) * sub, :], qt, preferred_element_type=F32)
    pv = lambda c, p: jnp.dot(vt_ref[0, 0, :, c * sub:(c + 1) * sub], p, preferred_element_type=F32)
    s_next = scores(0)
    pend = None
    for c in range(nsub):
        s = s_next
        if c + 1 < nsub:
            s_next = scores(c + 1)
        if pend is not None:
            acc = pend[0] * acc + pv(c - 1, pend[1])
        m_new = jnp.maximum(m, jnp.max(s, 0, keepdims=True))
        a = jnp.exp2(m - m_new)
        p = jnp.exp2(s - m_new)
        l = a * l + jnp.sum(p, 0, keepdims=True)
        pend = (a, p.astype(BF16))
        m = m_new
    acc = pend[0] * acc + pv(nsub - 1, pend[1])
    m_scr[...] = m
    l_scr[...] = l
    acc_scr[...] = acc

    @pl.when(j == nk - 1)
    def _():
        o = (acc / l).T
        for t in range(qn):
            o_ref[0, t * tq:(t + 1) * tq, :] = jnp.concatenate(
                [o[(t * ATT_REP + r) * tq:(t * ATT_REP + r + 1) * tq] for r in range(ATT_REP)], -1
            ).astype(o_ref.dtype)


def _flash_bounded_kernel(qt_ref, k_ref, vt_ref, o_ref, l_scr, acc_scr, *, nk, tq, sub, qn):
    j = pl.program_id(3)

    @pl.when(j == 0)
    def _():
        l_scr[...] = jnp.zeros(l_scr.shape, F32)
        acc_scr[...] = jnp.zeros(acc_scr.shape, F32)

    qt = jnp.concatenate([qt_ref[0, 0, t] for t in range(qn)], -1)
    nsub = k_ref.shape[2] // sub
    l = l_scr[...]
    acc = acc_scr[...]
    scores = lambda c: jnp.dot(k_ref[0, 0, c * sub:(c + 1) * sub, :], qt, preferred_element_type=F32)
    s_next = scores(0)
    for c in range(nsub):
        s = s_next
        if c + 1 < nsub:
            s_next = scores(c + 1)
        p = jnp.exp2(s)
        l = l + jnp.sum(p.reshape(sub // SUBLANES, SUBLANES, p.shape[1]), 0)
        acc = acc + jnp.dot(vt_ref[0, 0, :, c * sub:(c + 1) * sub], p.astype(BF16), preferred_element_type=F32)
    l_scr[...] = l
    acc_scr[...] = acc

    @pl.when(j == nk - 1)
    def _():
        o = (acc / jnp.sum(l, 0, keepdims=True)).T
        for t in range(qn):
            o_ref[0, t * tq:(t + 1) * tq, :] = jnp.concatenate(
                [o[(t * ATT_REP + r) * tq:(t * ATT_REP + r + 1) * tq] for r in range(ATT_REP)], -1
            ).astype(o_ref.dtype)


def _flash(qt, k, vt, tq, tk, qn, q_blk0, nq, nk, bounded=False):
    b = qt.shape[0]
    lq = nq * qn * tq
    sub = _pick_tile(tk, (256, 128))
    lanes = qn * ATT_REP * tq
    if bounded:
        body = functools.partial(_flash_bounded_kernel, nk=nk, tq=tq, sub=sub, qn=qn)
        scratch = [pltpu.VMEM((SUBLANES, lanes), F32), pltpu.VMEM((HEAD_DIM, lanes), F32)]
    else:
        body = functools.partial(_flash_kernel, nk=nk, tq=tq, sub=sub, qn=qn)
        scratch = [pltpu.VMEM((1, lanes), F32), pltpu.VMEM((1, lanes), F32), pltpu.VMEM((HEAD_DIM, lanes), F32)]
    return pl.pallas_call(
        body,
        grid=(b, ATT_KV_HEADS, nq, nk),
        in_specs=[pl.BlockSpec((1, 1, qn, HEAD_DIM, ATT_REP * tq), lambda bi, g, i, j: (bi, g, i + q_blk0, 0, 0)),
                  pl.BlockSpec((1, 1, tk, HEAD_DIM), lambda bi, g, i, j: (bi, g, j, 0)),
                  pl.BlockSpec((1, 1, HEAD_DIM, tk), lambda bi, g, i, j: (bi, g, 0, j))],
        out_specs=pl.BlockSpec((1, qn * tq, ATT_REP * HEAD_DIM), lambda bi, g, i, j: (bi, i, g)),
        out_shape=jax.ShapeDtypeStruct((b, lq, ATT_WIDTH), BF16),
        scratch_shapes=scratch,
        compiler_params=_cparams(("parallel", "parallel", "parallel", "arbitrary")),
        name="flash_bounded" if bounded else "flash",
    )(qt, k, vt)


def _prev_next(u, up8, un8, i, tm, lc, lt):
    start = i * tm
    p_ok = jnp.logical_and(start != 0, start != lc)
    n_ok = jnp.logical_and(start + tm != lc, start + tm != lt)
    prow = jnp.where(p_ok, up8[SUBLANES - 1:SUBLANES], 0.0)
    nrow = jnp.where(n_ok, un8[0:1], 0.0)
    rid = lax.broadcasted_iota(jnp.int32, u.shape, 0)
    prev = jnp.where(rid == 0, prow, pltpu.roll(u, 1, 0))
    nxt = jnp.where(rid == tm - 1, nrow, pltpu.roll(u, tm - 1, 0))
    return prev, nxt


def _halo_specs(tm, c, lt):
    r = tm // SUBLANES
    last = lt // SUBLANES - 1
    return [pl.BlockSpec((1, tm, c), lambda bi, i: (bi, i, 0)),
            pl.BlockSpec((1, SUBLANES, c), lambda bi, i: (bi, jnp.maximum(i * r - 1, 0), 0)),
            pl.BlockSpec((1, SUBLANES, c), lambda bi, i: (bi, jnp.minimum((i + 1) * r, last), 0))]


def _softplus(z):
    return jnp.maximum(z, 0.0) + jnp.log1p(jnp.exp(-jnp.abs(z)))


def _rwkv_prep_kernel(u_ref, up_ref, un_ref, mu_ref, wl_ref, bd_ref, w0_ref, a0_ref, kkw_ref, ka_ref, rk_ref,
                      r_o, v_o, kk_o, g_o, bon_o, lw_o, be_o, kd_o, *, lc, lt, tm):
    i = pl.program_id(1)
    u = u_ref[0]
    prev, nxt = _prev_next(u, up_ref[0], un_ref[0], i, tm, lc, lt)
    us = u + mu_ref[0:1] * (prev - u) + mu_ref[1:2] * (nxt - u)
    c = RWKV_WIDTH
    r = us[:, 0:c]
    k = us[:, c:2 * c]
    v = us[:, 2 * c:3 * c]
    slab = us[:, 3 * c:3 * c + LANES]
    lane = lax.broadcasted_iota(jnp.int32, slab.shape, 1)
    o_a = 2 * W_LORA
    o_g = o_a + 2 * A_LORA
    act = jnp.where(lane < o_a, jnp.tanh(slab),
                    jnp.where(lane < o_g, slab,
                              jnp.where(lane < o_g + G_LORA, jax.nn.sigmoid(slab), 0.0)))
    lo = _dotp(act, wl_ref[...], 3)
    bd = bd_ref[...]
    kk0 = k * kkw_ref[...]
    kk = kk0 * lax.rsqrt(_dot01(kk0 * kk0, bd) + 1e-12)
    r_o[0] = r
    v_o[0] = v
    kk_o[0] = kk
    g_o[0] = lo[:, 4 * c:5 * c]
    bon = jnp.zeros_like(r)
    for d in range(2):
        w_raw = w0_ref[d:d + 1] + lo[:, d * c:(d + 1) * c]
        lw = -jnp.exp(-_softplus(-w_raw) - 0.5)
        a = jax.nn.sigmoid(a0_ref[d:d + 1] + lo[:, (2 + d) * c:(3 + d) * c])
        kd = k * (1.0 + (a - 1.0) * ka_ref[...])
        lw_o[0, d] = lw
        be_o[0, d] = a * kk
        kd_o[0, d] = kd
        bon = bon + r * kd * rk_ref[...]
    bon_o[0] = _dot01(bon, bd) * v


def _rwkv_prep(ur, mu, wl, bd, w0, a0, kkw, ka, rk, lc):
    b, lt, cp = ur.shape
    tm = TOK_TILE
    c = RWKV_WIDTH
    full = lambda shape: pl.BlockSpec(shape, lambda bi, i: (0,) * len(shape))
    tok = pl.BlockSpec((1, tm, c), lambda bi, i: (bi, i, 0))
    tok2 = pl.BlockSpec((1, 2, tm, c), lambda bi, i: (bi, 0, i, 0))
    s1 = jax.ShapeDtypeStruct((b, lt, c), F32)
    s2 = jax.ShapeDtypeStruct((b, 2, lt, c), F32)
    return pl.pallas_call(
        functools.partial(_rwkv_prep_kernel, lc=lc, lt=lt, tm=tm),
        grid=(b, lt // tm),
        in_specs=_halo_specs(tm, cp, lt) + [full((2, cp)), full((LANES, 5 * c)), full((c, c)), full((2, c)),
                                            full((2, c)), full((1, c)), full((1, c)), full((1, c))],
        out_specs=[tok, tok, tok, tok, tok, tok2, tok2, tok2],
        out_shape=[s1, s1, s1, s1, s1, s2, s2, s2],
        compiler_params=_cparams(("parallel", "parallel")),
        name="rwkv_prep",
    )(ur, ur, ur, mu, wl, bd, w0, a0, kkw, ka, rk)


def _mm(a, b):
    return jnp.dot(a, b, precision=HI, preferred_element_type=F32)


_NN = ((1,), (0,))
_NT = ((1,), (1,))
_TN = ((0,), (0,))


def _split2(a):
    hi = a.astype(BF16)
    return hi, (a - hi.astype(F32)).astype(BF16)


def _dotp(a, b, passes, dims=_NN):
    if a.ndim == 3:
        dn = (((dims[0][0] + 1,), (dims[1][0] + 1,)), ((0,), (0,)))
    else:
        dn = (dims, ((), ()))
    dg = lambda p, q: lax.dot_general(p, q, dn, preferred_element_type=F32)
    if passes == 1:
        return dg(a.astype(BF16), b.astype(BF16))
    ah, al = _split2(a)
    bh, bl = _split2(b)
    return dg(ah, bh) + dg(ah, bl) + dg(al, bh)


def _dot01(a, ones):
    ah, al = _split2(a)
    o = ones.astype(BF16)
    return jnp.dot(ah, o, preferred_element_type=F32) + jnp.dot(al, o, preferred_element_type=F32)


P_M = 1
P_INV = 1
P_W = 1
P_Z = 1
P_STATE = 3
P_DFT = 1


def _unit_tri_inv(a_mat, row, col, eye, passes):
    t = a_mat.shape[-1]
    eye_f = eye.astype(F32)
    base = SUBLANES
    same = (row // base) == (col // base)
    n1 = -jnp.where(same, a_mat, 0.0)
    n2 = _dotp(n1, n1, passes)
    n4 = _dotp(n2, n2, passes)
    x = _dotp(_dotp(eye_f + n1, eye_f + n2, passes), eye_f + n4, passes)
    m = base
    while m < t:
        off = jnp.logical_and((row // (2 * m)) == (col // (2 * m)), (row // m) != (col // m))
        x = x - _dotp(x, _dotp(jnp.where(off, a_mat, 0.0), x, passes), passes)
        m *= 2
    return x


def _wkv_kernel(r_ref, v_ref, kk_ref, lw_ref, be_ref, kd_ref, y_ref, h_scr, pq_scr, ry_scr, *, fwd):
    t = WKV_CHUNK
    n = HEAD_DIM
    g = WKV_TILE // WKV_CHUNK
    tt = WKV_TILE
    order = list(range(g)) if fwd else list(range(g - 1, -1, -1))

    @pl.when(pl.program_id(1) == 0)
    def _():
        h_scr[...] = jnp.zeros(h_scr.shape, F32)
        pq_scr[...] = jnp.zeros(pq_scr.shape, F32)
        ry_scr[...] = jnp.zeros(ry_scr.shape, F32)

    nh = RWKV_HEADS
    hm = h_scr[...]
    for p in range(g):
        ry = ry_scr[p * nh:(p + 1) * nh]
        pq = pq_scr[p * nh:(p + 1) * nh]
        y = _dotp(ry[:, :, :n], hm, P_STATE) + ry[:, :, n:]
        hm = _dotp(pq[:, :, :n], hm, P_STATE) + pq[:, :, n:]
        c = order[p]
        y_ref[0, c * t:(c + 1) * t, :] = jnp.concatenate([y[h] for h in range(nh)], -1)
    h_scr[...] = hm

    row = lax.broadcasted_iota(jnp.int32, (tt, tt), 0)
    col = lax.broadcasted_iota(jnp.int32, (tt, tt), 1)
    same = (row // t) == (col // t)
    tri = jnp.logical_and(same, (row >= col) if fwd else (row <= col))
    sums = jnp.concatenate([jnp.where(tri, 1.0, 0.0), jnp.where(same, 1.0, 0.0)], 0).astype(BF16)
    lw = lw_ref[0, 0]
    l1 = lw.astype(BF16)
    rem = lw - l1.astype(F32)
    l2 = rem.astype(BF16)
    l3 = (rem - l2.astype(F32)).astype(BF16)
    cc = (jnp.dot(sums, l1, preferred_element_type=F32) + jnp.dot(sums, l2, preferred_element_type=F32)
          + jnp.dot(sums, l3, preferred_element_type=F32))
    cum = cc[:tt]
    ctot = cc[tt:]
    e_neg = jnp.exp(-cum)
    e_end = jnp.exp(ctot - cum)
    g_end = jnp.exp(ctot)
    v = v_ref[0]
    be = be_ref[0, 0]
    kd = kd_ref[0, 0]
    kap_t = kk_ref[0] * jnp.exp(cum - lw)
    r_t = r_ref[0] * jnp.exp(cum)
    be_t = be * e_neg
    kd_t = kd * e_neg
    be_h = be * e_end
    kd_h = kd * e_end
    r64 = lax.broadcasted_iota(jnp.int32, (t, t), 0)
    c64 = lax.broadcasted_iota(jnp.int32, (t, t), 1)
    eye = r64 == c64
    strict = (r64 > c64) if fwd else (r64 < c64)
    incl = jnp.logical_or(strict, eye)

    def units(x, rows=t):
        return jnp.stack([x[order[p] * t:order[p] * t + rows, h * n:(h + 1) * n]
                          for p in range(g) for h in range(nh)], 0)

    kap = units(kap_t)
    rt = units(r_t)
    vh = units(v)
    nu = g * nh
    m = _dotp(jnp.concatenate([kap, rt], 1), jnp.concatenate([units(be_t), units(kd_t)], 1), P_M, _NT)
    a_mat = jnp.where(strict, m[:, :t, :t], 0.0)
    b_mat = jnp.where(strict, m[:, :t, t:], 0.0)
    ab_r = jnp.concatenate([jnp.where(incl, m[:, t:, :t], 0.0), jnp.where(incl, m[:, t:, t:], 0.0)], 2)
    tinv = _unit_tri_inv(a_mat, r64, c64, eye, P_INV)
    w = _dotp(tinv, jnp.concatenate([kap, _dotp(b_mat, vh, P_W)], 2), P_W)
    z = jnp.concatenate([-w, jnp.concatenate([jnp.zeros((nu, t, n), F32), vh], 2)], 1)
    ry = _dotp(ab_r, z, P_Z)
    ry_scr[:, :, :n] = ry[:, :, :n] + rt
    ry_scr[:, :, n:] = ry[:, :, n:]
    pq = _dotp(jnp.concatenate([units(be_h), units(kd_h)], 1), z, P_Z, _TN)
    gd = jnp.where(eye, jnp.broadcast_to(units(g_end, 1), (nu, n, n)), 0.0)
    pq_scr[:, :, :n] = pq[:, :, :n] + gd
    pq_scr[:, :, n:] = pq[:, :, n:]


def _wkv_scan(r, v, kk, lw, be, kd, lc, fwd):
    b, lt, c = r.shape
    tt = WKV_TILE
    nt = lt // tt
    ntc = lc // tt
    d = 0 if fwd else 1

    def tile(i):
        if fwd:
            return i
        return jnp.where(i < ntc, ntc - 1 - i, nt - 1 - (i - ntc))

    t_in = lambda i: tile(jnp.minimum(i, nt - 1))
    t_out = lambda i: tile(jnp.maximum(i - 1, 0))
    one = pl.BlockSpec((1, tt, c), lambda bi, i: (bi, t_in(i), 0))
    two = pl.BlockSpec((1, 1, tt, c), lambda bi, i: (bi, d, t_in(i), 0))
    g = tt // WKV_CHUNK
    return pl.pallas_call(
        functools.partial(_wkv_kernel, fwd=fwd),
        grid=(b, nt + 1),
        in_specs=[one, one, one, two, two, two],
        out_specs=pl.BlockSpec((1, tt, c), lambda bi, i: (bi, t_out(i), 0)),
        out_shape=jax.ShapeDtypeStruct((b, lt, c), F32),
        scratch_shapes=[pltpu.VMEM((RWKV_HEADS, HEAD_DIM, HEAD_DIM), F32),
                        pltpu.VMEM((g * RWKV_HEADS, HEAD_DIM, 2 * HEAD_DIM), F32),
                        pltpu.VMEM((g * RWKV_HEADS, WKV_CHUNK, 2 * HEAD_DIM), F32)],
        compiler_params=_cparams(("parallel", "arbitrary")),
        name="wkv_fwd" if fwd else "wkv_bwd",
    )(r, v, kk, lw, be, kd)


def _rwkv_out_kernel(yf_ref, yb_ref, bon_ref, g_ref, bd_ref, gg_ref, gb_ref, o_ref):
    y = yf_ref[0] + yb_ref[0] + bon_ref[0]
    bd = bd_ref[...]
    mu = _dot01(y, bd) * (1.0 / HEAD_DIM)
    yc = y - mu
    var = _dot01(yc * yc, bd) * (1.0 / HEAD_DIM)
    yn = yc * lax.rsqrt(var + RWKV_GN_EPS) * gg_ref[...] + gb_ref[...]
    o_ref[0] = (yn * g_ref[0]).astype(o_ref.dtype)


def _rwkv_out(yf, yb, bon, g, bd, gg, gb):
    b, lt, c = yf.shape
    tm = TOK_TILE
    tok = pl.BlockSpec((1, tm, c), lambda bi, i: (bi, i, 0))
    full = lambda shape: pl.BlockSpec(shape, lambda bi, i: (0,) * len(shape))
    return pl.pallas_call(
        _rwkv_out_kernel,
        grid=(b, lt // tm),
        in_specs=[tok, tok, tok, tok, full((c, c)), full((1, c)), full((1, c))],
        out_specs=tok,
        out_shape=jax.ShapeDtypeStruct((b, lt, c), BF16),
        compiler_params=_cparams(("parallel", "parallel")),
        name="rwkv_out",
    )(yf, yb, bon, g, bd, gg, gb)


def _hy_prep_kernel(u_ref, up_ref, un_ref, w_ref, b_ref, x1_o, x2_o, v_o, *, lc, lt, tm):
    i = pl.program_id(1)
    u = u_ref[0]
    prev, nxt = _prev_next(u, up_ref[0], un_ref[0], i, tm, lc, lt)
    y = prev * w_ref[0:1] + u * w_ref[1:2] + nxt * w_ref[2:3] + b_ref[...]
    c = HY_WIDTH
    x1_o[0] = y[:, :c]
    x2_o[0] = y[:, c:2 * c]
    v_o[0] = y[:, 2 * c:]


def _hy_prep(uh, w, bias, lc):
    b, lt, cin = uh.shape
    tm = TOK_TILE
    c = HY_WIDTH
    full = lambda shape: pl.BlockSpec(shape, lambda bi, i: (0,) * len(shape))
    tok = pl.BlockSpec((1, tm, c), lambda bi, i: (bi, i, 0))
    s1 = jax.ShapeDtypeStruct((b, lt, c), F32)
    return pl.pallas_call(
        functools.partial(_hy_prep_kernel, lc=lc, lt=lt, tm=tm),
        grid=(b, lt // tm),
        in_specs=_halo_specs(tm, cin, lt) + [full((3, cin)), full((1, cin))],
        out_specs=[tok, tok, tok],
        out_shape=[s1, s1, s1],
        compiler_params=_cparams(("parallel", "parallel")),
        name="hy_prep",
    )(uh, uh, uh, w, bias)


def _hy_filter_kernel(z_ref, w1_ref, b1_ref, f1_ref, w2_ref, b2_ref, f2_ref, w3_ref, dec_ref,
                      fw_o, bw_o, nrm_o, *, tl):
    i = pl.program_id(0)
    z = z_ref[...]
    h = jnp.sin(f1_ref[...] * (_dotp(z, w1_ref[...], 3) + b1_ref[...]))
    h = jnp.sin(f2_ref[...] * (_dotp(h, w2_ref[...], 3) + b2_ref[...]))
    h = _dotp(h, w3_ref[...], 3) * jnp.exp(-z[:, 0:1] * dec_ref[...])
    c = HY_WIDTH
    fw = jnp.concatenate([h[:, 0:c], h[:, 2 * c:3 * c]], 1)
    bw = jnp.concatenate([h[:, c:2 * c], h[:, 3 * c:4 * c]], 1)
    rid = i * tl + lax.broadcasted_iota(jnp.int32, (tl, 1), 0)
    bw = jnp.where(rid == 0, 0.0, bw)
    fw_o[...] = fw
    bw_o[...] = bw

    @pl.when(i == 0)
    def _():
        nrm_o[...] = jnp.zeros(nrm_o.shape, F32)

    nrm_o[...] += jnp.sum(jnp.abs(fw) + jnp.abs(bw), 0, keepdims=True)


def _hy_filter(feat, w1p, b1, f1, w2, b2, f2, w3, dec):
    l, fe = feat.shape
    tl = _pick_tile(l, (512, 256))
    c2 = HY_ORDER * HY_WIDTH
    full = lambda shape: pl.BlockSpec(shape, lambda i: (0,) * len(shape))
    return pl.pallas_call(
        functools.partial(_hy_filter_kernel, tl=tl),
        grid=(l // tl,),
        in_specs=[pl.BlockSpec((tl, fe), lambda i: (i, 0)), full(w1p.shape), full(b1.shape), full(f1.shape),
                  full(w2.shape), full(b2.shape), full(f2.shape), full(w3.shape), full(dec.shape)],
        out_specs=[pl.BlockSpec((tl, c2), lambda i: (i, 0)), pl.BlockSpec((tl, c2), lambda i: (i, 0)),
                   pl.BlockSpec((1, c2), lambda i: (0, 0))],
        out_shape=[jax.ShapeDtypeStruct((l, c2), F32), jax.ShapeDtypeStruct((l, c2), F32),
                   jax.ShapeDtypeStruct((1, c2), F32)],
        compiler_params=_cparams(("arbitrary",)),
        name="hy_filter",
    )(feat, w1p, b1, f1, w2, b2, f2, w3, dec)


def _dft_cols_kernel(f_ref, xa_ref, xb_ref, o_ref, *, n1, pair):
    f = f_ref[...]
    pa = _dotp(f, xa_ref[0], P_DFT)
    pb = _dotp(f, xb_ref[0], P_DFT)
    if pair:
        o_ref[0] = pa[:n1].astype(o_ref.dtype)
        o_ref[1] = pa[n1:].astype(o_ref.dtype)
        o_ref[2] = pb[:n1].astype(o_ref.dtype)
        o_ref[3] = pb[n1:].astype(o_ref.dtype)
    else:
        o_ref[0] = (pa[:n1] - pb[n1:]).astype(o_ref.dtype)
        o_ref[1] = (pb[:n1] + pa[n1:]).astype(o_ref.dtype)


def _dft_cols(fstack, xa, ia, xb, ib, pair):
    n1 = fstack.shape[0] // 2
    _, nh, cols = xa.shape
    tc = _pick_tile(cols, (4096, 2048, 1024, 512, 256, 128))
    no = 4 if pair else 2
    return pl.pallas_call(
        functools.partial(_dft_cols_kernel, n1=n1, pair=pair),
        grid=(cols // tc,),
        in_specs=[pl.BlockSpec(fstack.shape, lambda j: (0, 0)),
                  pl.BlockSpec((1, nh, tc), lambda j: (ia, 0, j)),
                  pl.BlockSpec((1, nh, tc), lambda j: (ib, 0, j))],
        out_specs=pl.BlockSpec((no, n1, tc), lambda j: (0, 0, j)),
        out_shape=jax.ShapeDtypeStruct((no, n1, cols), BF16),
        compiler_params=_cparams(("parallel",)),
        name="dft_cols",
    )(fstack, xa, xb)


def _cplx_left(gs, zr, zi, n):
    c = zr.shape[1]
    p = _dotp(gs, jnp.concatenate([zr, zi], 1), P_DFT)
    return p[:n, :c] - p[n:, c:], p[:n, c:] + p[n:, :c]


def _spec_kernel(a_ref, g_ref, nrm_ref, o_ref, *, n_total):
    n2 = DFT_N2
    gs = jnp.concatenate([g_ref[0, 0], g_ref[0, 1]], 0)
    fr, fi = _cplx_left(gs, a_ref[0, 0], a_ref[1, 0], n2)
    br, bi = _cplx_left(gs, a_ref[2, 0], a_ref[3, 0], n2)
    s = 1.0 / (nrm_ref[...] * n_total)
    o_ref[0, 0] = (fr + br) * s
    o_ref[1, 0] = (fi - bi) * s


def _spec(a4, g, nrm, n_total):
    _, n1, n2, c2 = a4.shape
    return pl.pallas_call(
        functools.partial(_spec_kernel, n_total=float(n_total)),
        grid=(n1,),
        in_specs=[pl.BlockSpec((4, 1, n2, c2), lambda k: (0, k, 0, 0)),
                  pl.BlockSpec((1, 2, n2, n2), lambda k: (k, 0, 0, 0)),
                  pl.BlockSpec((1, c2), lambda k: (0, 0))],
        out_specs=pl.BlockSpec((2, 1, n2, c2), lambda k: (0, k, 0, 0)),
        out_shape=jax.ShapeDtypeStruct((2, n1, n2, c2), F32),
        compiler_params=_cparams(("parallel",)),
        name="hy_spec",
    )(a4, g, nrm)


def _conv_mid_kernel(a_ref, g_ref, gh_ref, k_ref, o_ref):
    n2 = DFT_N2
    gs = jnp.concatenate([g_ref[0, 0], g_ref[0, 1]], 0)
    xr, xi = _cplx_left(gs, a_ref[0, 0], a_ref[1, 0], n2)
    kr = k_ref[0, 0]
    ki = k_ref[1, 0]
    zr = xr * kr - xi * ki
    zi = xr * ki + xi * kr
    ghs = jnp.concatenate([gh_ref[0, 0], gh_ref[0, 1]], 0)
    yr, yi = _cplx_left(ghs, zr, zi, n2)
    o_ref[0, 0] = yr.astype(o_ref.dtype)
    o_ref[1, 0] = yi.astype(o_ref.dtype)


def _conv_mid(a, g, gh, kspec, order):
    _, n1, n2, c = a.shape
    return pl.pallas_call(
        _conv_mid_kernel,
        grid=(n1,),
        in_specs=[pl.BlockSpec((2, 1, n2, c), lambda k: (0, k, 0, 0)),
                  pl.BlockSpec((1, 2, n2, n2), lambda k: (k, 0, 0, 0)),
                  pl.BlockSpec((1, 2, n2, n2), lambda k: (k, 0, 0, 0)),
                  pl.BlockSpec((2, 1, n2, c), lambda k: (0, k, 0, order))],
        out_specs=pl.BlockSpec((2, 1, n2, c), lambda k: (0, k, 0, 0)),
        out_shape=jax.ShapeDtypeStruct((2, n1, n2, c), BF16),
        compiler_params=_cparams(("parallel",)),
        name="hy_conv_mid",
    )(a, g, gh, kspec)


def _idft_cols_kernel(c_ref, b_ref, g0_ref, g1_ref, x0_ref, x1_ref, bias_ref, o_ref, *, nh):
    cs = c_ref[...]
    pr = _dotp(cs, b_ref[0], P_DFT)
    pi = _dotp(cs, b_ref[1], P_DFT)
    yr = pr[:nh] - pi[nh:]
    yi = pi[:nh] + pr[nh:]
    bias = bias_ref[...]
    o_ref[0] = g0_ref[0] * (yr + x0_ref[0] * bias)
    o_ref[1] = g1_ref[0] * (yi + x1_ref[0] * bias)


def _idft_cols(cstack, bv, gate, xin, bias_cols):
    nh2, n1 = cstack.shape
    nh = nh2 // 2
    cols = bv.shape[-1]
    tc = _pick_tile(cols, (4096, 2048, 1024, 512, 256, 128))
    row = lambda bi: pl.BlockSpec((1, nh, tc), lambda j: (bi, 0, j))
    return pl.pallas_call(
        functools.partial(_idft_cols_kernel, nh=nh),
        grid=(cols // tc,),
        in_specs=[pl.BlockSpec((nh2, n1), lambda j: (0, 0)),
                  pl.BlockSpec((2, n1, tc), lambda j: (0, 0, j)),
                  row(0), row(1), row(0), row(1),
                  pl.BlockSpec((1, tc), lambda j: (0, j))],
        out_specs=pl.BlockSpec((2, nh, tc), lambda j: (0, 0, j)),
        out_shape=jax.ShapeDtypeStruct((2, nh, cols), F32),
        compiler_params=_cparams(("parallel",)),
        name="idft_cols",
    )(cstack, bv, gate, gate, xin, xin, bias_cols)


def _hy_ctx_kernel(x1_ref, x2_ref, v_ref, fw_ref, bw_ref, nrm_ref, bias_ref, ff_ref, ci_ref, o_ref, *, lc):
    n = 2 * lc
    c = HY_WIDTH
    ff = ff_ref[...]
    ci = ci_ref[...]
    pf = _mm(ff, fw_ref[...])
    pb = _mm(ff, bw_ref[...])
    s = 1.0 / (nrm_ref[...] * float(n))
    kr = (pf[:n] + pb[:n]) * s
    ki = (pf[n:] - pb[n:]) * s

    def conv(z0, z1, o):
        xr, xi = _cplx_left(ff, z0, z1, n)
        krr = kr[:, o * c:(o + 1) * c]
        kii = ki[:, o * c:(o + 1) * c]
        return _cplx_left(ci, xr * krr - xi * kii, xr * kii + xi * krr, lc)

    v0 = v_ref[0]
    v1 = v_ref[1]
    y0, y1 = conv(v0, v1, 0)
    z0 = x1_ref[0] * (y0 + v0 * bias_ref[0:1])
    z1 = x1_ref[1] * (y1 + v1 * bias_ref[0:1])
    y0, y1 = conv(z0, z1, 1)
    o_ref[0] = x2_ref[0] * (y0 + z0 * bias_ref[1:2])
    o_ref[1] = x2_ref[1] * (y1 + z1 * bias_ref[1:2])


def _hy_ctx(x1, x2, v, fw, bw, nrm, bias, ff, ci):
    b, lc, c = v.shape
    vm = pl.BlockSpec(memory_space=pltpu.VMEM)
    return pl.pallas_call(
        functools.partial(_hy_ctx_kernel, lc=lc),
        in_specs=[vm] * 9,
        out_specs=vm,
        out_shape=jax.ShapeDtypeStruct((b, lc, c), F32),
        compiler_params=pltpu.CompilerParams(vmem_limit_bytes=VMEM_LIMIT),
        name="hy_ctx",
    )(x1, x2, v, fw, bw, nrm, bias, ff, ci)


def _outproj_kernel(attc_ref, attl_ref, rw_ref, hyc_ref, hyl_ref, x_ref, mod_ref, w_ref, lg_ref, lb_ref, o_ref,
                    *, lc, tm):
    row0 = pl.program_id(1) * tm
    a0 = ATT_WIDTH
    a1 = ATT_WIDTH + RWKV_WIDTH
    is_ctx = row0 < lc
    att = jnp.where(is_ctx, attc_ref[0], attl_ref[0])
    hy = jnp.where(is_ctx, hyc_ref[0], hyl_ref[0])
    o = jnp.dot(att, w_ref[:a0], preferred_element_type=F32)
    o += jnp.dot(rw_ref[0], w_ref[a0:a1], preferred_element_type=F32)
    o += jnp.dot(hy.astype(BF16), w_ref[a1:], preferred_element_type=F32)
    g = _sel_mod(mod_ref, 2, row0, tm, lc)
    y = ALPHA * x_ref[0] + g * o
    o_ref[0] = _layer_norm(y) * lg_ref[...] + lb_ref[...]


def _outproj(att_c, att_l, rw, hy_c, hy_l, xx, mod, w, lg, lb, lc):
    b, lt, d = xx.shape
    tm = TOK_TILE
    nct = lc // tm
    tok = lambda c: pl.BlockSpec((1, tm, c), lambda bi, i: (bi, i, 0))
    ctx = lambda c: pl.BlockSpec((1, tm, c), lambda bi, i: (bi, jnp.minimum(i, nct - 1), 0))
    lat = lambda c: pl.BlockSpec((1, tm, c), lambda bi, i: (bi, jnp.maximum(i - nct, 0), 0))
    full = lambda shape: pl.BlockSpec(shape, lambda bi, i: (0,) * len(shape))
    return pl.pallas_call(
        functools.partial(_outproj_kernel, lc=lc, tm=tm),
        grid=(b, lt // tm),
        in_specs=[ctx(ATT_WIDTH), lat(ATT_WIDTH), tok(RWKV_WIDTH), ctx(HY_WIDTH), lat(HY_WIDTH), tok(d),
                  pl.BlockSpec((1, 2, 6, d), lambda bi, i: (bi, 0, 0, 0)),
                  full(w.shape), full((1, d)), full((1, d))],
        out_specs=tok(d),
        out_shape=jax.ShapeDtypeStruct((b, lt, d), F32),
        compiler_params=_cparams(("parallel", "parallel")),
        name="outproj",
    )(att_c, att_l, rw, hy_c, hy_l, xx, mod, w, lg, lb)


def _ffn_kernel(x_ref, mod_ref, w1_ref, w3_ref, w2_ref, lg_ref, lb_ref, o_ref, h_scr, acc_scr, *, lc, tm, nf):
    row0 = pl.program_id(1) * tm
    f = pl.program_id(2)

    @pl.when(f == 0)
    def _():
        sh = _sel_mod(mod_ref, 3, row0, tm, lc)
        sc = _sel_mod(mod_ref, 4, row0, tm, lc)
        h_scr[...] = (_layer_norm(x_ref[0]) * (1.0 + sc) + sh).astype(BF16)
        acc_scr[...] = jnp.zeros(acc_scr.shape, F32)

    h = h_scr[...]
    a = jnp.dot(h, w1_ref[...], preferred_element_type=F32)
    g = jnp.dot(h, w3_ref[...], preferred_element_type=F32)
    acc_scr[...] += jnp.dot((_silu(a) * g).astype(BF16), w2_ref[...], preferred_element_type=F32)

    @pl.when(f == nf - 1)
    def _():
        gate = _sel_mod(mod_ref, 5, row0, tm, lc)
        y = ALPHA * x_ref[0] + gate * acc_scr[...]
        o_ref[0] = _layer_norm(y) * lg_ref[...] + lb_ref[...]


def _ffn(xx, mod, w1, w3, w2, lg, lb, lc):
    b, lt, d = xx.shape
    ff = w1.shape[1]
    tm = _pick_tile(lt, (1280, 768, 512, 256))
    tf = _pick_tile(ff, (256, 128))
    nf = ff // tf
    return pl.pallas_call(
        functools.partial(_ffn_kernel, lc=lc, tm=tm, nf=nf),
        grid=(b, lt // tm, nf),
        in_specs=[pl.BlockSpec((1, tm, d), lambda bi, i, f: (bi, i, 0)),
                  pl.BlockSpec((1, 2, 6, d), lambda bi, i, f: (bi, 0, 0, 0)),
                  pl.BlockSpec((d, tf), lambda bi, i, f: (0, f)),
                  pl.BlockSpec((d, tf), lambda bi, i, f: (0, f)),
                  pl.BlockSpec((tf, d), lambda bi, i, f: (f, 0)),
                  pl.BlockSpec((1, d), lambda bi, i, f: (0, 0)),
                  pl.BlockSpec((1, d), lambda bi, i, f: (0, 0))],
        out_specs=pl.BlockSpec((1, tm, d), lambda bi, i, f: (bi, i, 0)),
        out_shape=jax.ShapeDtypeStruct((b, lt, d), F32),
        scratch_shapes=[pltpu.VMEM((tm, d), BF16), pltpu.VMEM((tm, d), F32)],
        compiler_params=_cparams(("parallel", "parallel", "arbitrary")),
        name="ffn",
    )(xx, mod, w1, w3, w2, lg, lb)


def _route(x_ref, mod_ref, wr_ref, row0, tm, lc):
    sh = _sel_mod(mod_ref, 3, row0, tm, lc)
    sc = _sel_mod(mod_ref, 4, row0, tm, lc)
    h = _layer_norm(x_ref[0]) * (1.0 + sc) + sh
    logits = _dotp(h, wr_ref[...], 3)
    lane = lax.broadcasted_iota(jnp.int32, logits.shape, 1)
    neg = jnp.float32(-jnp.inf)
    lg = jnp.where(lane < N_EXPERTS, logits, neg)
    m1 = jnp.max(lg, -1, keepdims=True)
    i1 = jnp.min(jnp.where(lg == m1, lane, LANES), -1, keepdims=True)
    lg2 = jnp.where(lane == i1, neg, lg)
    m2 = jnp.max(lg2, -1, keepdims=True)
    i2 = jnp.min(jnp.where(lg2 == m2, lane, LANES), -1, keepdims=True)
    e2 = jnp.exp(m2 - m1)
    return h, lane, i1, i2, 1.0 / (1.0 + e2), e2 / (1.0 + e2)


def _moe_sparse_kernel(x_ref, mod_ref, wr_ref, w1_ref, w3_ref, w2_ref, lg_ref, lb_ref, o_ref,
                       h_scr, acc_scr, gate_scr, posc_scr, posr_scr, xs_scr, ye_scr, nblk_scr, *, lc, tm, nf, ns, blk):
    row0 = pl.program_id(1) * tm
    s = pl.program_id(2)
    e = s // nf
    f = s % nf

    @pl.when(s == 0)
    def _():
        h, lane, i1, i2, g1, g2 = _route(x_ref, mod_ref, wr_ref, row0, tm, lc)
        h_scr[...] = h.astype(BF16)
        acc_scr[...] = jnp.zeros(acc_scr.shape, F32)
        routed = jnp.where(jnp.logical_or(lane == i1, lane == i2), 1.0, 0.0)
        r = lax.broadcasted_iota(jnp.int32, (tm, tm), 0)
        c = lax.broadcasted_iota(jnp.int32, (tm, tm), 1)
        before = jnp.where(c < r, 1.0, 0.0).astype(BF16)
        rank_c = jnp.dot(before, routed.astype(BF16), preferred_element_type=F32)
        gate_scr[...] = jnp.where(lane == i1, g1, 0.0) + jnp.where(lane == i2, g2, 0.0)
        posc_scr[...] = jnp.where(routed > 0.0, rank_c, -1.0)
        routed_t = routed.T[:2 * SUBLANES]
        after = jnp.where(r < c, 1.0, 0.0).astype(BF16)
        rank_r = jnp.dot(routed_t.astype(BF16), after, preferred_element_type=F32)
        posr_scr[...] = jnp.where(routed_t > 0.0, rank_r, -1.0)
        counts = jnp.sum(routed, 0, keepdims=True)
        lane1 = lax.broadcasted_iota(jnp.int32, counts.shape, 1)
        for ex in range(N_EXPERTS):
            n_rows = jnp.sum(jnp.where(lane1 == ex, counts, 0.0)).astype(jnp.int32)
            nblk_scr[ex] = (n_rows + (blk - 1)) // blk

    nblk = nblk_scr[e]
    sizes = [(k + 1) * blk for k in range(tm // blk)]

    def per_size(cond, fn):
        for k, m in enumerate(sizes):
            pl.when(jnp.logical_and(cond, nblk == k + 1))(functools.partial(fn, m))

    def gather(m):
        slot = lax.broadcasted_iota(jnp.int32, (m, tm), 0).astype(F32)
        take = jnp.where(slot == posr_scr[pl.ds(e, 1), :], 1.0, 0.0).astype(BF16)
        xs_scr[:m] = jnp.dot(take, h_scr[...], preferred_element_type=F32).astype(BF16)
        ye_scr[:m] = jnp.zeros((m, ye_scr.shape[1]), F32)

    def expert(m):
        xs = xs_scr[:m]
        a = jnp.dot(xs, w1_ref[0], preferred_element_type=F32)
        g = jnp.dot(xs, w3_ref[0], preferred_element_type=F32)
        ye_scr[:m] += jnp.dot((_silu(a) * g).astype(BF16), w2_ref[0], preferred_element_type=F32)

    def scatter(m):
        lane = lax.broadcasted_iota(jnp.int32, (tm, LANES), 1)
        column = lambda ref: jnp.sum(jnp.where(lane == e, ref[...], 0.0), -1, keepdims=True)
        slot = lax.broadcasted_iota(jnp.int32, (tm, m), 1).astype(F32)
        put = jnp.where(slot == column(posc_scr), 1.0, 0.0).astype(BF16)
        y = jnp.dot(put, ye_scr[:m].astype(BF16), preferred_element_type=F32)
        acc_scr[...] += column(gate_scr) * y

    per_size(f == 0, gather)
    per_size(True, expert)
    per_size(f == nf - 1, scatter)

    @pl.when(s == ns - 1)
    def _():
        gate = _sel_mod(mod_ref, 5, row0, tm, lc)
        y = ALPHA * x_ref[0] + gate * acc_scr[...]
        o_ref[0] = _layer_norm(y) * lg_ref[...] + lb_ref[...]


def _moe_sparse(xx, mod, wr, w1, w3, w2, lg, lb, lc):
    b, lt, d = xx.shape
    ne, _, ff = w1.shape
    tm = _pick_tile(lt, (1280, 768, 512, 256))
    tf = _pick_tile(ff, (256, 128))
    nf = ff // tf
    ns = ne * nf
    return pl.pallas_call(
        functools.partial(_moe_sparse_kernel, lc=lc, tm=tm, nf=nf, ns=ns, blk=MOE_ROW_BLOCK),
        grid=(b, lt // tm, ns),
        in_specs=[pl.BlockSpec((1, tm, d), lambda bi, i, s: (bi, i, 0)),
                  pl.BlockSpec((1, 2, 6, d), lambda bi, i, s: (bi, 0, 0, 0)),
                  pl.BlockSpec((d, LANES), lambda bi, i, s: (0, 0)),
                  pl.BlockSpec((1, d, tf), lambda bi, i, s: (s // nf, 0, s % nf)),
                  pl.BlockSpec((1, d, tf), lambda bi, i, s: (s // nf, 0, s % nf)),
                  pl.BlockSpec((1, tf, d), lambda bi, i, s: (s // nf, s % nf, 0)),
                  pl.BlockSpec((1, d), lambda bi, i, s: (0, 0)),
                  pl.BlockSpec((1, d), lambda bi, i, s: (0, 0))],
        out_specs=pl.BlockSpec((1, tm, d), lambda bi, i, s: (bi, i, 0)),
        out_shape=jax.ShapeDtypeStruct((b, lt, d), F32),
        scratch_shapes=[pltpu.VMEM((tm, d), BF16), pltpu.VMEM((tm, d), F32),
                        pltpu.VMEM((tm, LANES), F32), pltpu.VMEM((tm, LANES), F32),
                        pltpu.VMEM((2 * SUBLANES, tm), F32),
                        pltpu.VMEM((tm, d), BF16), pltpu.VMEM((tm, d), F32),
                        pltpu.SMEM((ne,), jnp.int32)],
        compiler_params=_cparams(("parallel", "parallel", "arbitrary"), VMEM_LIMIT_MOE),
        name="moe_sparse",
    )(xx, mod, wr, w1, w3, w2, lg, lb)


def _rope_tables(l, lc):
    rows = l // GRID_W
    row = jnp.repeat(jnp.arange(rows, dtype=F32), GRID_W)
    col = jnp.tile(jnp.arange(GRID_W, dtype=F32), rows)
    n_freq = HEAD_DIM // 4
    inv_freq = ROPE_THETA ** (-jnp.arange(n_freq, dtype=F32) / n_freq)
    ang = jnp.concatenate([row[:, None] * inv_freq, col[:, None] * inv_freq], -1)
    cos, sin = jnp.cos(ang), jnp.sin(ang)
    cos2 = jnp.concatenate([jnp.ones((lc, LANES), F32), jnp.concatenate([cos, cos, cos, cos], -1)], 0)
    sin2 = jnp.concatenate([jnp.zeros((lc, LANES), F32), jnp.concatenate([-sin, sin, -sin, sin], -1)], 0)
    return cos2, sin2


def _hy_features(l):
    bands = (HY_EMB - 1) // 2
    t = jnp.linspace(0.0, 1.0, l, dtype=F32)[:, None]
    f = jnp.linspace(1e-4, bands - 1, bands, dtype=F32)[None, :]
    wt = 2.0 * math.pi * jnp.arange(l, dtype=F32)[:, None] / l
    z = jnp.concatenate([t, jnp.cos(f * wt), -jnp.sin(f * wt)], -1)
    return jnp.pad(z, ((0, 0), (0, LANES - HY_EMB)))


def _angle(idx, n):
    return (2.0 * math.pi / n) * (idx % n).astype(F32)


def _dft_tables(n1):
    nh = n1 // 2
    n2 = DFT_N2
    n = n1 * n2
    k1 = jnp.arange(n1, dtype=jnp.int32)
    a1 = _angle(k1[:, None] * jnp.arange(nh, dtype=jnp.int32)[None, :], n1)
    fstack = jnp.concatenate([jnp.cos(a1), -jnp.sin(a1)], 0)
    cstack = jnp.concatenate([jnp.cos(a1.T), jnp.sin(a1.T)], 0)
    k2 = jnp.arange(n2, dtype=jnp.int32)
    at = _angle(k1[:, None] * k2[None, :], n)
    tr, ti = jnp.cos(at), -jnp.sin(at)
    a2 = _angle(k2[:, None] * k2[None, :], n2)
    fr, fi = jnp.cos(a2), -jnp.sin(a2)
    g = jnp.stack([tr[:, None, :] * fr[None] - ti[:, None, :] * fi[None],
                   tr[:, None, :] * fi[None] + ti[:, None, :] * fr[None]], 1)
    gh = jnp.stack([tr[:, :, None] * fr[None] - ti[:, :, None] * fi[None],
                    -(tr[:, :, None] * fi[None] + ti[:, :, None] * fr[None])], 1)
    return fstack, cstack, g, gh


def _dense_dft_tables(lc):
    n = 2 * lc
    a = _angle(jnp.arange(n, dtype=jnp.int32)[:, None] * jnp.arange(lc, dtype=jnp.int32)[None, :], n)
    ff = jnp.concatenate([jnp.cos(a), -jnp.sin(a)], 0)
    ci = jnp.concatenate([jnp.cos(a.T), jnp.sin(a.T)], 0)
    return ff, ci


def kernel(x, c, ctx, c_ctx, ada_w, ada_b, w_in, w_out, q_gain, k_gain, rwkv_mu, rwkv_w0, rwkv_wB, rwkv_a0, rwkv_aB, rwkv_gB, rwkv_kk, rwkv_ka, rwkv_rk, rwkv_gn_g, rwkv_gn_b, hy_short_w, hy_short_b, hy_w1, hy_b1, hy_freq1, hy_w2, hy_b2, hy_freq2, hy_w3, hy_decay, hy_bias, ln1_g, ln1_b, ln2_g, ln2_b, ffn_w1, ffn_w3, ffn_w2, moe_router, moe_w1, moe_w3, moe_w2):
    b, l, d = x.shape
    lc = ctx.shape[1]
    lt = lc + l
    depth = ada_w.shape[0]
    assert b == 2, "the long convolution packs the two batch rows as one complex signal"
    assert d == D_MODEL and lc % TOK_TILE == 0 and l % TOK_TILE == 0 and (2 * l) % (2 * DFT_N2) == 0
    cw = RWKV_WIDTH

    xx = jnp.concatenate([ctx, x], 1)
    cond8 = jnp.zeros((SUBLANES, d), F32).at[:b].set(c).at[b].set(c_ctx)
    mod_all = _ada_mod(cond8, ada_w, ada_b)

    cos64, sin64 = _rope_tables(l, lc)
    n1 = 2 * l // DFT_N2
    nh = n1 // 2
    cols = DFT_N2 * HY_WIDTH
    fstack, cstack, g_tab, gh_tab = (t.astype(BF16) for t in _dft_tables(n1))
    ff_c, ci_c = _dense_dft_tables(lc)
    feat_l = _hy_features(l)
    feat_c = _hy_features(lc)
    blk = jnp.arange(cw) // HEAD_DIM
    bd = (blk[:, None] == blk[None, :]).astype(F32)
    ch = jnp.arange(ATT_WIDTH)
    bd_att = (ch[:, None] // HEAD_DIM == ch[None, :] // HEAD_DIM).astype(BF16)
    swap_att = (ch[:, None] == (ch[None, :] + HALF_HD) % HEAD_DIM + (ch[None, :] // HEAD_DIM) * HEAD_DIM
                ).astype(BF16)
    perm64 = jnp.concatenate([jnp.arange(0, HEAD_DIM, 2), jnp.arange(1, HEAD_DIM, 2)])
    perm_att = jnp.concatenate([h * HEAD_DIM + perm64 for h in range(ATT_HEADS + ATT_KV_HEADS)]
                               + [jnp.arange(ATT_WIDTH + ATT_KV_WIDTH, IN_ATT)])
    tq = TOK_TILE
    qn = next(n for n in (4, 2, 1) if (l // tq) % n == 0)
    tk = _pick_tile(lt, (3328, 1280, 1024, 768, 512, 256))

    for li in range(depth):
        ml = mod_all[li]
        mod = jnp.stack([jnp.broadcast_to(ml[b].reshape(1, 6, d), (b, 6, d)), ml[:b].reshape(b, 6, d)], 1)
        wi = w_in[li]
        w_pad = jnp.concatenate([wi[:, :IN_ATT][:, perm_att], wi[:, IN_ATT:IN_ATT + IN_RWKV],
                                 jnp.zeros((d, IN_RWKV_PAD - IN_RWKV), F32), wi[:, IN_ATT + IN_RWKV:]],
                                1).astype(BF16)
        ua, ur, uh = _inproj(xx, mod, w_pad, lc)

        two = lambda gain: jnp.tile(gain[perm64], 2)[None]
        qt, kx, vt = _attn_prep(ua, cos64, sin64, two(q_gain[li]), two(k_gain[li]), bd_att, swap_att, lc)
        att_c = _flash(qt, kx, vt, tq, TOK_TILE, 1, l // tq, lc // tq, lc // TOK_TILE)
        s_bound = (HEAD_DIM ** 0.5) * LOG2E * jnp.max(jnp.abs(q_gain[li])) * jnp.max(jnp.abs(k_gain[li]))
        lat_args = (qt, kx, vt, tq, tk, qn, 0, l // (qn * tq), lt // tk)
        att_l = lax.cond(s_bound <= MAX_UNSHIFTED_SCORE,
                         lambda: _flash(*lat_args, bounded=True), lambda: _flash(*lat_args, bounded=False))

        wl = jnp.zeros((LANES, 5 * cw), F32)
        wl = wl.at[0:W_LORA, 0:cw].set(rwkv_wB[li, 0]).at[W_LORA:2 * W_LORA, cw:2 * cw].set(rwkv_wB[li, 1])
        o_a = 2 * W_LORA
        wl = wl.at[o_a:o_a + A_LORA, 2 * cw:3 * cw].set(rwkv_aB[li, 0])
        wl = wl.at[o_a + A_LORA:o_a + 2 * A_LORA, 3 * cw:4 * cw].set(rwkv_aB[li, 1])
        o_g = o_a + 2 * A_LORA
        wl = wl.at[o_g:o_g + G_LORA, 4 * cw:5 * cw].set(rwkv_gB[li])
        mu = jnp.pad(rwkv_mu[li], ((0, 0), (0, IN_RWKV_PAD - IN_RWKV)))
        r_, v_, kk_, g_, bon_, lw_, be_, kd_ = _rwkv_prep(
            ur, mu, wl, bd, rwkv_w0[li], rwkv_a0[li], rwkv_kk[li][None], rwkv_ka[li][None],
            rwkv_rk[li].reshape(1, cw), lc)
        yf = _wkv_scan(r_, v_, kk_, lw_, be_, kd_, lc, True)
        yb = _wkv_scan(r_, v_, kk_, lw_, be_, kd_, lc, False)
        rw = _rwkv_out(yf, yb, bon_, g_, bd, rwkv_gn_g[li][None], rwkv_gn_b[li][None])

        x1, x2, vv = _hy_prep(uh, hy_short_w[li], hy_short_b[li][None], lc)
        w1p = jnp.pad(hy_w1[li], ((0, LANES - HY_EMB), (0, 0)))
        fargs = (w1p, hy_b1[li][None], hy_freq1[li][None], hy_w2[li], hy_b2[li][None], hy_freq2[li][None],
                 hy_w3[li], hy_decay[li][None])
        fw, bw, nrm = _hy_filter(feat_l, *fargs)
        c2 = HY_ORDER * HY_WIDTH
        a4 = _dft_cols(fstack, fw.reshape(1, nh, DFT_N2 * c2), 0, bw.reshape(1, nh, DFT_N2 * c2), 0, True)
        kspec = _spec(a4.reshape(4, n1, DFT_N2, c2), g_tab, nrm, n1 * DFT_N2)
        lat = lambda t: t[:, lc:].reshape(b, nh, cols)
        x1l, x2l, zin = lat(x1), lat(x2), lat(vv)
        for o, gate in enumerate((x1l, x2l)):
            a = _dft_cols(fstack, zin, 0, zin, 1, False)
            bv = _conv_mid(a.reshape(2, n1, DFT_N2, HY_WIDTH), g_tab, gh_tab, kspec, o)
            bias_cols = jnp.tile(hy_bias[li, o], DFT_N2)[None]
            zin = _idft_cols(cstack, bv.reshape(2, n1, cols), gate, zin, bias_cols)
        hy_l = zin.reshape(b, l, HY_WIDTH)
        fw_c, bw_c, nrm_c = _hy_filter(feat_c, *fargs)
        hy_c = _hy_ctx(x1[:, :lc], x2[:, :lc], vv[:, :lc], fw_c, bw_c, nrm_c, hy_bias[li], ff_c, ci_c)

        xx = _outproj(att_c, att_l, rw, hy_c, hy_l, xx, mod, w_out[li].astype(BF16),
                      ln1_g[li][None], ln1_b[li][None], lc)

        j = li // 2
        if li % 2 == 0:
            xx = _ffn(xx, mod, ffn_w1[j].astype(BF16), ffn_w3[j].astype(BF16), ffn_w2[j].astype(BF16),
                      ln2_g[li][None], ln2_b[li][None], lc)
        else:
            wr = jnp.pad(moe_router[j], ((0, 0), (0, LANES - N_EXPERTS)))
            xx = _moe_sparse(xx, mod, wr, moe_w1[j].astype(BF16), moe_w3[j].astype(BF16), moe_w2[j].astype(BF16),
                             ln2_g[li][None], ln2_b[li][None], lc)
    return xx[:, lc:]
```

```python
import functools
import math

import jax
import jax.numpy as jnp
from jax import lax
from jax.experimental import pallas as pl
from jax.experimental.pallas import tpu as pltpu

F32 = jnp.float32
BF16 = jnp.bfloat16
HI = lax.Precision.HIGHEST

D_MODEL = 1024
DEPTH = 2
GRID_W = 64
HEAD_DIM = 64
HALF_HD = HEAD_DIM // 2
ATT_WIDTH = 512
RWKV_WIDTH = 256
HY_WIDTH = 256
ATT_HEADS = 8
ATT_KV_HEADS = 2
ATT_REP = 4
ATT_KV_WIDTH = 128
ROPE_THETA = 10000.0
QK_EPS = 1e-6
RWKV_HEADS = 4
W_LORA = 16
A_LORA = 16
G_LORA = 32
RWKV_GN_EPS = 64e-5
HY_ORDER = 2
HY_EMB = 33
HY_FFN = 64
N_EXPERTS = 8
LN_EPS = 1e-6
IN_ATT = ATT_WIDTH + 2 * ATT_KV_WIDTH
IN_RWKV = 3 * RWKV_WIDTH + 2 * W_LORA + 2 * A_LORA + G_LORA
IN_RWKV_PAD = 896
IN_HY = 3 * HY_WIDTH
ALPHA = float((2 * DEPTH) ** 0.25)
LOG2E = 1.4426950408889634
MAX_UNSHIFTED_SCORE = 40.0

LANES = 128
SUBLANES = 8
TOK_TILE = 256
WKV_CHUNK = 64
WKV_TILE = 256
DFT_N2 = 256
MOE_ROW_BLOCK = 256
FLASH_BLOCKS_PER_TRIP = 3
VMEM_LIMIT = 48 * 1024 * 1024
VMEM_LIMIT_MOE = 56 * 1024 * 1024


def _cparams(sem, vmem=VMEM_LIMIT):
    return pltpu.CompilerParams(dimension_semantics=sem, vmem_limit_bytes=vmem)


def _pick_tile(n, cands):
    for c in cands:
        if n % c == 0:
            return c
    raise ValueError(f"no tile for {n} in {cands}")


def _layer_norm(x):
    mu = jnp.mean(x, -1, keepdims=True)
    xc = x - mu
    var = jnp.mean(xc * xc, -1, keepdims=True)
    return xc * lax.rsqrt(var + LN_EPS)


def _sel_mod(mod_ref, j, row0, tm, lc):
    rid = row0 + lax.broadcasted_iota(jnp.int32, (tm, 1), 0)
    return jnp.where(rid < lc, mod_ref[0, 0, j:j + 1, :], mod_ref[0, 1, j:j + 1, :])


def _silu(x):
    return x * jax.nn.sigmoid(x)


def _ada_kernel(c_ref, w_ref, b_ref, o_ref):
    s = _silu(c_ref[...])
    o_ref[0] = jnp.dot(s, w_ref[0], precision=HI, preferred_element_type=F32) + b_ref[0]


def _ada_mod(cond8, ada_w, ada_b):
    depth, d, n = ada_w.shape
    tn = _pick_tile(n, (1536, 1024, 512, 256, 128))
    return pl.pallas_call(
        _ada_kernel,
        grid=(depth, n // tn),
        in_specs=[pl.BlockSpec((SUBLANES, d), lambda l, j: (0, 0)),
                  pl.BlockSpec((1, d, tn), lambda l, j: (l, 0, j)),
                  pl.BlockSpec((1, 1, tn), lambda l, j: (l, 0, j))],
        out_specs=pl.BlockSpec((1, SUBLANES, tn), lambda l, j: (l, 0, j)),
        out_shape=jax.ShapeDtypeStruct((depth, SUBLANES, n), F32),
        compiler_params=_cparams(("parallel", "parallel")),
        name="ada_mod",
    )(cond8, ada_w, ada_b.reshape(depth, 1, n))


def _inproj_kernel(x_ref, mod_ref, w_ref, oa_ref, or_ref, oh_ref, *, lc, tm):
    row0 = pl.program_id(1) * tm
    sh = _sel_mod(mod_ref, 0, row0, tm, lc)
    sc = _sel_mod(mod_ref, 1, row0, tm, lc)
    h = (_layer_norm(x_ref[0]) * (1.0 + sc) + sh).astype(BF16)
    u = jnp.dot(h, w_ref[...], preferred_element_type=F32)
    oa_ref[0] = u[:, :IN_ATT]
    or_ref[0] = u[:, IN_ATT:IN_ATT + IN_RWKV_PAD]
    oh_ref[0] = u[:, IN_ATT + IN_RWKV_PAD:]


def _inproj(xx, mod, w_pad, lc):
    b, lt, d = xx.shape
    tm = TOK_TILE
    n = w_pad.shape[1]
    return pl.pallas_call(
        functools.partial(_inproj_kernel, lc=lc, tm=tm),
        grid=(b, lt // tm),
        in_specs=[pl.BlockSpec((1, tm, d), lambda bi, i: (bi, i, 0)),
                  pl.BlockSpec((1, 2, 6, d), lambda bi, i: (bi, 0, 0, 0)),
                  pl.BlockSpec((d, n), lambda bi, i: (0, 0))],
        out_specs=[pl.BlockSpec((1, tm, IN_ATT), lambda bi, i: (bi, i, 0)),
                   pl.BlockSpec((1, tm, IN_RWKV_PAD), lambda bi, i: (bi, i, 0)),
                   pl.BlockSpec((1, tm, IN_HY), lambda bi, i: (bi, i, 0))],
        out_shape=[jax.ShapeDtypeStruct((b, lt, IN_ATT), F32),
                   jax.ShapeDtypeStruct((b, lt, IN_RWKV_PAD), F32),
                   jax.ShapeDtypeStruct((b, lt, IN_HY), F32)],
        compiler_params=_cparams(("parallel", "parallel")),
        name="inproj",
    )(xx, mod, w_pad)


def _attn_prep_kernel(u_ref, cos_ref, sin_ref, qg_ref, kg_ref, bd_ref, sw_ref, qt_ref, k_ref, vt_ref):
    u = u_ref[0]

    def norm_rope(x, g):
        w = x.shape[1]
        tile = lambda t: jnp.concatenate([t] * (w // LANES), -1)
        ms = _dot01(x * x, bd_ref[:w, :w]) * (1.0 / HEAD_DIM)
        xn = x * lax.rsqrt(ms + QK_EPS) * tile(g)
        sw = _dot01(xn, sw_ref[:w, :w])
        return xn * tile(cos_ref[...]) + sw * tile(sin_ref[...])

    q = norm_rope(u[:, :ATT_WIDTH], qg_ref[...]) * (LOG2E * HEAD_DIM ** -0.5)
    qt = q.T
    kx = norm_rope(u[:, ATT_WIDTH:ATT_WIDTH + ATT_KV_WIDTH], kg_ref[...])
    for g in range(ATT_KV_HEADS):
        base = g * ATT_REP * HEAD_DIM
        qt_ref[0, g, 0] = jnp.concatenate(
            [qt[base + r * HEAD_DIM:base + (r + 1) * HEAD_DIM] for r in range(ATT_REP)], -1).astype(BF16)
        k_ref[0, g] = kx[:, g * HEAD_DIM:(g + 1) * HEAD_DIM].astype(BF16)
    v0 = ATT_WIDTH + ATT_KV_WIDTH
    vt = u[:, v0:v0 + ATT_KV_WIDTH].T
    for g in range(ATT_KV_HEADS):
        vt_ref[0, g] = vt[g * HEAD_DIM:(g + 1) * HEAD_DIM].astype(BF16)


def _attn_prep(ua, cos, sin, qg, kg, bd_att, swap_att, lc):
    b, lt, _ = ua.shape
    tm = TOK_TILE
    nct = lc // tm
    nl = lt // tm - nct
    q_pos = lambda i: jnp.where(i < nct, nl + i, i - nct)
    return pl.pallas_call(
        _attn_prep_kernel,
        grid=(b, lt // tm),
        in_specs=[pl.BlockSpec((1, tm, IN_ATT), lambda bi, i: (bi, i, 0)),
                  pl.BlockSpec((tm, LANES), lambda bi, i: (i, 0)),
                  pl.BlockSpec((tm, LANES), lambda bi, i: (i, 0)),
                  pl.BlockSpec((1, LANES), lambda bi, i: (0, 0)),
                  pl.BlockSpec((1, LANES), lambda bi, i: (0, 0)),
                  pl.BlockSpec((ATT_WIDTH, ATT_WIDTH), lambda bi, i: (0, 0)),
                  pl.BlockSpec((ATT_WIDTH, ATT_WIDTH), lambda bi, i: (0, 0))],
        out_specs=[pl.BlockSpec((1, ATT_KV_HEADS, 1, HEAD_DIM, ATT_REP * tm),
                                lambda bi, i: (bi, 0, q_pos(i), 0, 0)),
                   pl.BlockSpec((1, ATT_KV_HEADS, tm, HEAD_DIM), lambda bi, i: (bi, 0, i, 0)),
                   pl.BlockSpec((1, ATT_KV_HEADS, HEAD_DIM, tm), lambda bi, i: (bi, 0, 0, i))],
        out_shape=[jax.ShapeDtypeStruct((b, ATT_KV_HEADS, lt // tm, HEAD_DIM, ATT_REP * tm), BF16),
                   jax.ShapeDtypeStruct((b, ATT_KV_HEADS, lt, HEAD_DIM), BF16),
                   jax.ShapeDtypeStruct((b, ATT_KV_HEADS, HEAD_DIM, lt), BF16)],
        compiler_params=_cparams(("parallel", "parallel")),
        name="attn_prep",
    )(ua, cos, sin, qg, kg, bd_att, swap_att)


def _flash_kernel(qt_ref, k_ref, vt_ref, o_ref, m_scr, l_scr, acc_scr, *, nk, tq, sub, qn):
    j = pl.program_id(3)

    @pl.when(j == 0)
    def _():
        m_scr[...] = jnp.full(m_scr.shape, -jnp.inf, F32)
        l_scr[...] = jnp.zeros(l_scr.shape, F32)
        acc_scr[...] = jnp.zeros(acc_scr.shape, F32)

    qt = jnp.concatenate([qt_ref[0, 0, t] for t in range(qn)], -1)
    nsub = k_ref.shape[2] // sub
    m = m_scr[...]
    l = l_scr[...]
    acc = acc_scr[...]
    scores = lambda c: jnp.dot(k_ref[0, 0, c * sub:(c + 1) * sub, :], qt, preferred_element_type=F32)
    pv = lambda c, p: jnp.dot(vt_ref[0, 0, :, c * sub:(c + 1) * sub], p, preferred_element_type=F32)
    s_next = scores(0)
    pend = None
    for c in range(nsub):
        s = s_next
        if c + 1 < nsub:
            s_next = scores(c + 1)
        if pend is not None:
            acc = pend[0] * acc + pv(c - 1, pend[1])
        m_new = jnp.maximum(m, jnp.max(s, 0, keepdims=True))
        a = jnp.exp2(m - m_new)
        p = jnp.exp2(s - m_new)
        l = a * l + jnp.sum(p, 0, keepdims=True)
        pend = (a, p.astype(BF16))
        m = m_new
    acc = pend[0] * acc + pv(nsub - 1, pend[1])
    m_scr[...] = m
    l_scr[...] = l
    acc_scr[...] = acc

    @pl.when(j == nk - 1)
    def _():
        o = (acc / l).T
        for t in range(qn):
            o_ref[0, t * tq:(t + 1) * tq, :] = jnp.concatenate(
                [o[(t * ATT_REP + r) * tq:(t * ATT_REP + r + 1) * tq] for r in range(ATT_REP)], -1
            ).astype(o_ref.dtype)


def _flash_bounded_kernel(qt_ref, k_ref, vt_ref, o_ref, l_scr, acc_scr, *, nk, tq, sub, qn):
    j = pl.program_id(3)

    @pl.when(j == 0)
    def _():
        l_scr[...] = jnp.zeros(l_scr.shape, F32)
        acc_scr[...] = jnp.zeros(acc_scr.shape, F32)

    qt = jnp.concatenate([qt_ref[0, 0, t] for t in range(qn)], -1)
    nsub = k_ref.shape[2] // sub

    def blocks(c0, n):
        start = lambda c: pl.multiple_of((c0 + c) * sub, sub)
        scores = lambda c: jnp.dot(k_ref[0, 0, pl.ds(start(c), sub), :], qt, preferred_element_type=F32)
        l = l_scr[...]
        acc = acc_scr[...]
        s_next = scores(0)
        for c in range(n):
            s = s_next
            if c + 1 < n:
                s_next = scores(c + 1)
            p = jnp.exp2(s)
            l = l + jnp.sum(p.reshape(sub // SUBLANES, SUBLANES, p.shape[1]), 0)
            acc = acc + jnp.dot(vt_ref[0, 0, :, pl.ds(start(c), sub)], p.astype(BF16),
                                preferred_element_type=F32)
        l_scr[...] = l
        acc_scr[...] = acc

    per_trip = min(FLASH_BLOCKS_PER_TRIP, nsub)
    trips = nsub // per_trip
    if trips > 1:
        lax.fori_loop(0, trips, lambda i, carry: (blocks(i * per_trip, per_trip), carry)[1], 0)
    else:
        blocks(0, per_trip)
    if nsub % per_trip:
        blocks(trips * per_trip, nsub % per_trip)

    @pl.when(j == nk - 1)
    def _():
        acc = acc_scr[...]
        l = l_scr[...]
        o = (acc / jnp.sum(l, 0, keepdims=True)).T
        for t in range(qn):
            o_ref[0, t * tq:(t + 1) * tq, :] = jnp.concatenate(
                [o[(t * ATT_REP + r) * tq:(t * ATT_REP + r + 1) * tq] for r in range(ATT_REP)], -1
            ).astype(o_ref.dtype)


def _flash(qt, k, vt, tq, tk, qn, q_blk0, nq, nk, bounded=False):
    b = qt.shape[0]
    lq = nq * qn * tq
    sub = _pick_tile(tk, (256, 128))
    lanes = qn * ATT_REP * tq
    if bounded:
        body = functools.partial(_flash_bounded_kernel, nk=nk, tq=tq, sub=sub, qn=qn)
        scratch = [pltpu.VMEM((SUBLANES, lanes), F32), pltpu.VMEM((HEAD_DIM, lanes), F32)]
    else:
        body = functools.partial(_flash_kernel, nk=nk, tq=tq, sub=sub, qn=qn)
        scratch = [pltpu.VMEM((1, lanes), F32), pltpu.VMEM((1, lanes), F32), pltpu.VMEM((HEAD_DIM, lanes), F32)]
    return pl.pallas_call(
        body,
        grid=(b, ATT_KV_HEADS, nq, nk),
        in_specs=[pl.BlockSpec((1, 1, qn, HEAD_DIM, ATT_REP * tq), lambda bi, g, i, j: (bi, g, i + q_blk0, 0, 0)),
                  pl.BlockSpec((1, 1, tk, HEAD_DIM), lambda bi, g, i, j: (bi, g, j, 0)),
                  pl.BlockSpec((1, 1, HEAD_DIM, tk), lambda bi, g, i, j: (bi, g, 0, j))],
        out_specs=pl.BlockSpec((1, qn * tq, ATT_REP * HEAD_DIM), lambda bi, g, i, j: (bi, i, g)),
        out_shape=jax.ShapeDtypeStruct((b, lq, ATT_WIDTH), BF16),
        scratch_shapes=scratch,
        compiler_params=_cparams(("parallel", "parallel", "parallel", "arbitrary")),
        name="flash_bounded" if bounded else "flash",
    )(qt, k, vt)


def _prev_next(u, up8, un8, i, tm, lc, lt):
    start = i * tm
    p_ok = jnp.logical_and(start != 0, start != lc)
    n_ok = jnp.logical_and(start + tm != lc, start + tm != lt)
    prow = jnp.where(p_ok, up8[SUBLANES - 1:SUBLANES], 0.0)
    nrow = jnp.where(n_ok, un8[0:1], 0.0)
    rid = lax.broadcasted_iota(jnp.int32, u.shape, 0)
    prev = jnp.where(rid == 0, prow, pltpu.roll(u, 1, 0))
    nxt = jnp.where(rid == tm - 1, nrow, pltpu.roll(u, tm - 1, 0))
    return prev, nxt


def _halo_specs(tm, c, lt):
    r = tm // SUBLANES
    last = lt // SUBLANES - 1
    return [pl.BlockSpec((1, tm, c), lambda bi, i: (bi, i, 0)),
            pl.BlockSpec((1, SUBLANES, c), lambda bi, i: (bi, jnp.maximum(i * r - 1, 0), 0)),
            pl.BlockSpec((1, SUBLANES, c), lambda bi, i: (bi, jnp.minimum((i + 1) * r, last), 0))]


def _softplus(z):
    return jnp.maximum(z, 0.0) + jnp.log1p(jnp.exp(-jnp.abs(z)))


def _rwkv_prep_kernel(u_ref, up_ref, un_ref, mu_ref, wl_ref, bd_ref, w0_ref, a0_ref, kkw_ref, ka_ref, rk_ref,
                      r_o, v_o, kk_o, g_o, bon_o, lw_o, be_o, kd_o, *, lc, lt, tm):
    i = pl.program_id(1)
    u = u_ref[0]
    prev, nxt = _prev_next(u, up_ref[0], un_ref[0], i, tm, lc, lt)
    us = u + mu_ref[0:1] * (prev - u) + mu_ref[1:2] * (nxt - u)
    c = RWKV_WIDTH
    r = us[:, 0:c]
    k = us[:, c:2 * c]
    v = us[:, 2 * c:3 * c]
    slab = us[:, 3 * c:3 * c + LANES]
    lane = lax.broadcasted_iota(jnp.int32, slab.shape, 1)
    o_a = 2 * W_LORA
    o_g = o_a + 2 * A_LORA
    act = jnp.where(lane < o_a, jnp.tanh(slab),
                    jnp.where(lane < o_g, slab,
                              jnp.where(lane < o_g + G_LORA, jax.nn.sigmoid(slab), 0.0)))
    lo = _dotp(act, wl_ref[...], 3)
    bd = bd_ref[...]
    kk0 = k * kkw_ref[...]
    kk = kk0 * lax.rsqrt(_dot01(kk0 * kk0, bd) + 1e-12)
    r_o[0] = r
    v_o[0] = v
    kk_o[0] = kk
    g_o[0] = lo[:, 4 * c:5 * c]
    bon = jnp.zeros_like(r)
    for d in range(2):
        w_raw = w0_ref[d:d + 1] + lo[:, d * c:(d + 1) * c]
        lw = -jnp.exp(-_softplus(-w_raw) - 0.5)
        a = jax.nn.sigmoid(a0_ref[d:d + 1] + lo[:, (2 + d) * c:(3 + d) * c])
        kd = k * (1.0 + (a - 1.0) * ka_ref[...])
        lw_o[0, d] = lw
        be_o[0, d] = a * kk
        kd_o[0, d] = kd
        bon = bon + r * kd * rk_ref[...]
    bon_o[0] = _dot01(bon, bd) * v


def _rwkv_prep(ur, mu, wl, bd, w0, a0, kkw, ka, rk, lc):
    b, lt, cp = ur.shape
    tm = TOK_TILE
    c = RWKV_WIDTH
    full = lambda shape: pl.BlockSpec(shape, lambda bi, i: (0,) * len(shape))
    tok = pl.BlockSpec((1, tm, c), lambda bi, i: (bi, i, 0))
    tok2 = pl.BlockSpec((1, 2, tm, c), lambda bi, i: (bi, 0, i, 0))
    s1 = jax.ShapeDtypeStruct((b, lt, c), F32)
    s2 = jax.ShapeDtypeStruct((b, 2, lt, c), F32)
    return pl.pallas_call(
        functools.partial(_rwkv_prep_kernel, lc=lc, lt=lt, tm=tm),
        grid=(b, lt // tm),
        in_specs=_halo_specs(tm, cp, lt) + [full((2, cp)), full((LANES, 5 * c)), full((c, c)), full((2, c)),
                                            full((2, c)), full((1, c)), full((1, c)), full((1, c))],
        out_specs=[tok, tok, tok, tok, tok, tok2, tok2, tok2],
        out_shape=[s1, s1, s1, s1, s1, s2, s2, s2],
        compiler_params=_cparams(("parallel", "parallel")),
        name="rwkv_prep",
    )(ur, ur, ur, mu, wl, bd, w0, a0, kkw, ka, rk)


def _mm(a, b):
    return jnp.dot(a, b, precision=HI, preferred_element_type=F32)


_NN = ((1,), (0,))
_NT = ((1,), (1,))
_TN = ((0,), (0,))


def _split2(a):
    hi = a.astype(BF16)
    return hi, (a - hi.astype(F32)).astype(BF16)


def _dotp(a, b, passes, dims=_NN):
    if a.ndim == 3:
        dn = (((dims[0][0] + 1,), (dims[1][0] + 1,)), ((0,), (0,)))
    else:
        dn = (dims, ((), ()))
    dg = lambda p, q: lax.dot_general(p, q, dn, preferred_element_type=F32)
    if passes == 1:
        return dg(a.astype(BF16), b.astype(BF16))
    ah, al = _split2(a)
    bh, bl = _split2(b)
    return dg(ah, bh) + dg(ah, bl) + dg(al, bh)


def _dot01(a, ones):
    ah, al = _split2(a)
    o = ones.astype(BF16)
    return jnp.dot(ah, o, preferred_element_type=F32) + jnp.dot(al, o, preferred_element_type=F32)


P_M = 1
P_INV = 1
P_W = 1
P_Z = 1
P_STATE = 3
P_DFT = 1


def _unit_tri_inv(a_mat, row, col, eye, passes):
    t = a_mat.shape[-1]
    eye_f = eye.astype(F32)
    base = SUBLANES
    same = (row // base) == (col // base)
    n1 = -jnp.where(same, a_mat, 0.0)
    n2 = _dotp(n1, n1, passes)
    n4 = _dotp(n2, n2, passes)
    x = _dotp(_dotp(eye_f + n1, eye_f + n2, passes), eye_f + n4, passes)
    m = base
    while m < t:
        off = jnp.logical_and((row // (2 * m)) == (col // (2 * m)), (row // m) != (col // m))
        x = x - _dotp(x, _dotp(jnp.where(off, a_mat, 0.0), x, passes), passes)
        m *= 2
    return x


def _wkv_kernel(r_ref, v_ref, kk_ref, lw_ref, be_ref, kd_ref, y_ref, h_scr, pq_scr, ry_scr, *, fwd):
    t = WKV_CHUNK
    n = HEAD_DIM
    g = WKV_TILE // WKV_CHUNK
    tt = WKV_TILE
    order = list(range(g)) if fwd else list(range(g - 1, -1, -1))

    @pl.when(pl.program_id(1) == 0)
    def _():
        h_scr[...] = jnp.zeros(h_scr.shape, F32)
        pq_scr[...] = jnp.zeros(pq_scr.shape, F32)
        ry_scr[...] = jnp.zeros(ry_scr.shape, F32)

    nh = RWKV_HEADS
    hm = h_scr[...]
    for p in range(g):
        ry = ry_scr[p * nh:(p + 1) * nh]
        pq = pq_scr[p * nh:(p + 1) * nh]
        y = _dotp(ry[:, :, :n], hm, P_STATE) + ry[:, :, n:]
        hm = _dotp(pq[:, :, :n], hm, P_STATE) + pq[:, :, n:]
        c = order[p]
        y_ref[0, c * t:(c + 1) * t, :] = jnp.concatenate([y[h] for h in range(nh)], -1)
    h_scr[...] = hm

    row = lax.broadcasted_iota(jnp.int32, (tt, tt), 0)
    col = lax.broadcasted_iota(jnp.int32, (tt, tt), 1)
    same = (row // t) == (col // t)
    tri = jnp.logical_and(same, (row >= col) if fwd else (row <= col))
    sums = jnp.concatenate([jnp.where(tri, 1.0, 0.0), jnp.where(same, 1.0, 0.0)], 0).astype(BF16)
    lw = lw_ref[0, 0]
    l1 = lw.astype(BF16)
    rem = lw - l1.astype(F32)
    l2 = rem.astype(BF16)
    l3 = (rem - l2.astype(F32)).astype(BF16)
    cc = (jnp.dot(sums, l1, preferred_element_type=F32) + jnp.dot(sums, l2, preferred_element_type=F32)
          + jnp.dot(sums, l3, preferred_element_type=F32))
    cum = cc[:tt]
    ctot = cc[tt:]
    e_neg = jnp.exp(-cum)
    e_end = jnp.exp(ctot - cum)
    g_end = jnp.exp(ctot)
    v = v_ref[0]
    be = be_ref[0, 0]
    kd = kd_ref[0, 0]
    kap_t = kk_ref[0] * jnp.exp(cum - lw)
    r_t = r_ref[0] * jnp.exp(cum)
    be_t = be * e_neg
    kd_t = kd * e_neg
    be_h = be * e_end
    kd_h = kd * e_end
    r64 = lax.broadcasted_iota(jnp.int32, (t, t), 0)
    c64 = lax.broadcasted_iota(jnp.int32, (t, t), 1)
    eye = r64 == c64
    strict = (r64 > c64) if fwd else (r64 < c64)
    incl = jnp.logical_or(strict, eye)

    def units(x, rows=t):
        return jnp.stack([x[order[p] * t:order[p] * t + rows, h * n:(h + 1) * n]
                          for p in range(g) for h in range(nh)], 0)

    kap = units(kap_t)
    rt = units(r_t)
    vh = units(v)
    nu = g * nh
    m = _dotp(jnp.concatenate([kap, rt], 1), jnp.concatenate([units(be_t), units(kd_t)], 1), P_M, _NT)
    a_mat = jnp.where(strict, m[:, :t, :t], 0.0)
    b_mat = jnp.where(strict, m[:, :t, t:], 0.0)
    ab_r = jnp.concatenate([jnp.where(incl, m[:, t:, :t], 0.0), jnp.where(incl, m[:, t:, t:], 0.0)], 2)
    tinv = _unit_tri_inv(a_mat, r64, c64, eye, P_INV)
    w = _dotp(tinv, jnp.concatenate([kap, _dotp(b_mat, vh, P_W)], 2), P_W)
    z = jnp.concatenate([-w, jnp.concatenate([jnp.zeros((nu, t, n), F32), vh], 2)], 1)
    ry = _dotp(ab_r, z, P_Z)
    ry_scr[:, :, :n] = ry[:, :, :n] + rt
    ry_scr[:, :, n:] = ry[:, :, n:]
    pq = _dotp(jnp.concatenate([units(be_h), units(kd_h)], 1), z, P_Z, _TN)
    gd = jnp.where(eye, jnp.broadcast_to(units(g_end, 1), (nu, n, n)), 0.0)
    pq_scr[:, :, :n] = pq[:, :, :n] + gd
    pq_scr[:, :, n:] = pq[:, :, n:]


def _wkv_scan(r, v, kk, lw, be, kd, lc, fwd):
    b, lt, c = r.shape
    tt = WKV_TILE
    nt = lt // tt
    ntc = lc // tt
    d = 0 if fwd else 1

    def tile(i):
        if fwd:
            return i
        return jnp.where(i < ntc, ntc - 1 - i, nt - 1 - (i - ntc))

    t_in = lambda i: tile(jnp.minimum(i, nt - 1))
    t_out = lambda i: tile(jnp.maximum(i - 1, 0))
    one = pl.BlockSpec((1, tt, c), lambda bi, i: (bi, t_in(i), 0))
    two = pl.BlockSpec((1, 1, tt, c), lambda bi, i: (bi, d, t_in(i), 0))
    g = tt // WKV_CHUNK
    return pl.pallas_call(
        functools.partial(_wkv_kernel, fwd=fwd),
        grid=(b, nt + 1),
        in_specs=[one, one, one, two, two, two],
        out_specs=pl.BlockSpec((1, tt, c), lambda bi, i: (bi, t_out(i), 0)),
        out_shape=jax.ShapeDtypeStruct((b, lt, c), F32),
        scratch_shapes=[pltpu.VMEM((RWKV_HEADS, HEAD_DIM, HEAD_DIM), F32),
                        pltpu.VMEM((g * RWKV_HEADS, HEAD_DIM, 2 * HEAD_DIM), F32),
                        pltpu.VMEM((g * RWKV_HEADS, WKV_CHUNK, 2 * HEAD_DIM), F32)],
        compiler_params=_cparams(("parallel", "arbitrary")),
        name="wkv_fwd" if fwd else "wkv_bwd",
    )(r, v, kk, lw, be, kd)


def _rwkv_out_kernel(yf_ref, yb_ref, bon_ref, g_ref, bd_ref, gg_ref, gb_ref, o_ref):
    y = yf_ref[0] + yb_ref[0] + bon_ref[0]
    bd = bd_ref[...]
    mu = _dot01(y, bd) * (1.0 / HEAD_DIM)
    yc = y - mu
    var = _dot01(yc * yc, bd) * (1.0 / HEAD_DIM)
    yn = yc * lax.rsqrt(var + RWKV_GN_EPS) * gg_ref[...] + gb_ref[...]
    o_ref[0] = (yn * g_ref[0]).astype(o_ref.dtype)


def _rwkv_out(yf, yb, bon, g, bd, gg, gb):
    b, lt, c = yf.shape
    tm = TOK_TILE
    tok = pl.BlockSpec((1, tm, c), lambda bi, i: (bi, i, 0))
    full = lambda shape: pl.BlockSpec(shape, lambda bi, i: (0,) * len(shape))
    return pl.pallas_call(
        _rwkv_out_kernel,
        grid=(b, lt // tm),
        in_specs=[tok, tok, tok, tok, full((c, c)), full((1, c)), full((1, c))],
        out_specs=tok,
        out_shape=jax.ShapeDtypeStruct((b, lt, c), BF16),
        compiler_params=_cparams(("parallel", "parallel")),
        name="rwkv_out",
    )(yf, yb, bon, g, bd, gg, gb)


def _hy_prep_kernel(u_ref, up_ref, un_ref, w_ref, b_ref, x1_o, x2_o, v_o, *, lc, lt, tm):
    i = pl.program_id(1)
    u = u_ref[0]
    prev, nxt = _prev_next(u, up_ref[0], un_ref[0], i, tm, lc, lt)
    y = prev * w_ref[0:1] + u * w_ref[1:2] + nxt * w_ref[2:3] + b_ref[...]
    c = HY_WIDTH
    x1_o[0] = y[:, :c]
    x2_o[0] = y[:, c:2 * c]
    v_o[0] = y[:, 2 * c:]


def _hy_prep(uh, w, bias, lc):
    b, lt, cin = uh.shape
    tm = TOK_TILE
    c = HY_WIDTH
    full = lambda shape: pl.BlockSpec(shape, lambda bi, i: (0,) * len(shape))
    tok = pl.BlockSpec((1, tm, c), lambda bi, i: (bi, i, 0))
    s1 = jax.ShapeDtypeStruct((b, lt, c), F32)
    return pl.pallas_call(
        functools.partial(_hy_prep_kernel, lc=lc, lt=lt, tm=tm),
        grid=(b, lt // tm),
        in_specs=_halo_specs(tm, cin, lt) + [full((3, cin)), full((1, cin))],
        out_specs=[tok, tok, tok],
        out_shape=[s1, s1, s1],
        compiler_params=_cparams(("parallel", "parallel")),
        name="hy_prep",
    )(uh, uh, uh, w, bias)


def _hy_filter_kernel(z_ref, w1_ref, b1_ref, f1_ref, w2_ref, b2_ref, f2_ref, w3_ref, dec_ref,
                      fw_o, bw_o, nrm_o, *, tl):
    i = pl.program_id(0)
    z = z_ref[...]
    h = jnp.sin(f1_ref[...] * (_dotp(z, w1_ref[...], 3) + b1_ref[...]))
    h = jnp.sin(f2_ref[...] * (_dotp(h, w2_ref[...], 3) + b2_ref[...]))
    h = _dotp(h, w3_ref[...], 3) * jnp.exp(-z[:, 0:1] * dec_ref[...])
    c = HY_WIDTH
    fw = jnp.concatenate([h[:, 0:c], h[:, 2 * c:3 * c]], 1)
    bw = jnp.concatenate([h[:, c:2 * c], h[:, 3 * c:4 * c]], 1)
    rid = i * tl + lax.broadcasted_iota(jnp.int32, (tl, 1), 0)
    bw = jnp.where(rid == 0, 0.0, bw)
    fw_o[...] = fw
    bw_o[...] = bw

    @pl.when(i == 0)
    def _():
        nrm_o[...] = jnp.zeros(nrm_o.shape, F32)

    nrm_o[...] += jnp.sum(jnp.abs(fw) + jnp.abs(bw), 0, keepdims=True)


def _hy_filter(feat, w1p, b1, f1, w2, b2, f2, w3, dec):
    l, fe = feat.shape
    tl = _pick_tile(l, (512, 256))
    c2 = HY_ORDER * HY_WIDTH
    full = lambda shape: pl.BlockSpec(shape, lambda i: (0,) * len(shape))
    return pl.pallas_call(
        functools.partial(_hy_filter_kernel, tl=tl),
        grid=(l // tl,),
        in_specs=[pl.BlockSpec((tl, fe), lambda i: (i, 0)), full(w1p.shape), full(b1.shape), full(f1.shape),
                  full(w2.shape), full(b2.shape), full(f2.shape), full(w3.shape), full(dec.shape)],
        out_specs=[pl.BlockSpec((tl, c2), lambda i: (i, 0)), pl.BlockSpec((tl, c2), lambda i: (i, 0)),
                   pl.BlockSpec((1, c2), lambda i: (0, 0))],
        out_shape=[jax.ShapeDtypeStruct((l, c2), F32), jax.ShapeDtypeStruct((l, c2), F32),
                   jax.ShapeDtypeStruct((1, c2), F32)],
        compiler_params=_cparams(("arbitrary",)),
        name="hy_filter",
    )(feat, w1p, b1, f1, w2, b2, f2, w3, dec)


def _dft_cols_kernel(f_ref, xa_ref, xb_ref, o_ref, *, n1, pair):
    f = f_ref[...]
    pa = _dotp(f, xa_ref[0], P_DFT)
    pb = _dotp(f, xb_ref[0], P_DFT)
    if pair:
        o_ref[0] = pa[:n1].astype(o_ref.dtype)
        o_ref[1] = pa[n1:].astype(o_ref.dtype)
        o_ref[2] = pb[:n1].astype(o_ref.dtype)
        o_ref[3] = pb[n1:].astype(o_ref.dtype)
    else:
        o_ref[0] = (pa[:n1] - pb[n1:]).astype(o_ref.dtype)
        o_ref[1] = (pb[:n1] + pa[n1:]).astype(o_ref.dtype)


def _dft_cols(fstack, xa, ia, xb, ib, pair):
    n1 = fstack.shape[0] // 2
    _, nh, cols = xa.shape
    tc = _pick_tile(cols, (4096, 2048, 1024, 512, 256, 128))
    no = 4 if pair else 2
    return pl.pallas_call(
        functools.partial(_dft_cols_kernel, n1=n1, pair=pair),
        grid=(cols // tc,),
        in_specs=[pl.BlockSpec(fstack.shape, lambda j: (0, 0)),
                  pl.BlockSpec((1, nh, tc), lambda j: (ia, 0, j)),
                  pl.BlockSpec((1, nh, tc), lambda j: (ib, 0, j))],
        out_specs=pl.BlockSpec((no, n1, tc), lambda j: (0, 0, j)),
        out_shape=jax.ShapeDtypeStruct((no, n1, cols), BF16),
        compiler_params=_cparams(("parallel",)),
        name="dft_cols",
    )(fstack, xa, xb)


def _cplx_left(gs, zr, zi, n):
    c = zr.shape[1]
    p = _dotp(gs, jnp.concatenate([zr, zi], 1), P_DFT)
    return p[:n, :c] - p[n:, c:], p[:n, c:] + p[n:, :c]


def _spec_kernel(a_ref, g_ref, nrm_ref, o_ref, *, n_total):
    n2 = DFT_N2
    gs = jnp.concatenate([g_ref[0, 0], g_ref[0, 1]], 0)
    fr, fi = _cplx_left(gs, a_ref[0, 0], a_ref[1, 0], n2)
    br, bi = _cplx_left(gs, a_ref[2, 0], a_ref[3, 0], n2)
    s = 1.0 / (nrm_ref[...] * n_total)
    o_ref[0, 0] = (fr + br) * s
    o_ref[1, 0] = (fi - bi) * s


def _spec(a4, g, nrm, n_total):
    _, n1, n2, c2 = a4.shape
    return pl.pallas_call(
        functools.partial(_spec_kernel, n_total=float(n_total)),
        grid=(n1,),
        in_specs=[pl.BlockSpec((4, 1, n2, c2), lambda k: (0, k, 0, 0)),
                  pl.BlockSpec((1, 2, n2, n2), lambda k: (k, 0, 0, 0)),
                  pl.BlockSpec((1, c2), lambda k: (0, 0))],
        out_specs=pl.BlockSpec((2, 1, n2, c2), lambda k: (0, k, 0, 0)),
        out_shape=jax.ShapeDtypeStruct((2, n1, n2, c2), F32),
        compiler_params=_cparams(("parallel",)),
        name="hy_spec",
    )(a4, g, nrm)


def _conv_mid_kernel(a_ref, g_ref, gh_ref, k_ref, o_ref):
    n2 = DFT_N2
    gs = jnp.concatenate([g_ref[0, 0], g_ref[0, 1]], 0)
    xr, xi = _cplx_left(gs, a_ref[0, 0], a_ref[1, 0], n2)
    kr = k_ref[0, 0]
    ki = k_ref[1, 0]
    zr = xr * kr - xi * ki
    zi = xr * ki + xi * kr
    ghs = jnp.concatenate([gh_ref[0, 0], gh_ref[0, 1]], 0)
    yr, yi = _cplx_left(ghs, zr, zi, n2)
    o_ref[0, 0] = yr.astype(o_ref.dtype)
    o_ref[1, 0] = yi.astype(o_ref.dtype)


def _conv_mid(a, g, gh, kspec, order):
    _, n1, n2, c = a.shape
    return pl.pallas_call(
        _conv_mid_kernel,
        grid=(n1,),
        in_specs=[pl.BlockSpec((2, 1, n2, c), lambda k: (0, k, 0, 0)),
                  pl.BlockSpec((1, 2, n2, n2), lambda k: (k, 0, 0, 0)),
                  pl.BlockSpec((1, 2, n2, n2), lambda k: (k, 0, 0, 0)),
                  pl.BlockSpec((2, 1, n2, c), lambda k: (0, k, 0, order))],
        out_specs=pl.BlockSpec((2, 1, n2, c), lambda k: (0, k, 0, 0)),
        out_shape=jax.ShapeDtypeStruct((2, n1, n2, c), BF16),
        compiler_params=_cparams(("parallel",)),
        name="hy_conv_mid",
    )(a, g, gh, kspec)


def _idft_cols_kernel(c_ref, b_ref, g0_ref, g1_ref, x0_ref, x1_ref, bias_ref, o_ref, *, nh):
    cs = c_ref[...]
    pr = _dotp(cs, b_ref[0], P_DFT)
    pi = _dotp(cs, b_ref[1], P_DFT)
    yr = pr[:nh] - pi[nh:]
    yi = pi[:nh] + pr[nh:]
    bias = bias_ref[...]
    o_ref[0] = g0_ref[0] * (yr + x0_ref[0] * bias)
    o_ref[1] = g1_ref[0] * (yi + x1_ref[0] * bias)


def _idft_cols(cstack, bv, gate, xin, bias_cols):
    nh2, n1 = cstack.shape
    nh = nh2 // 2
    cols = bv.shape[-1]
    tc = _pick_tile(cols, (4096, 2048, 1024, 512, 256, 128))
    row = lambda bi: pl.BlockSpec((1, nh, tc), lambda j: (bi, 0, j))
    return pl.pallas_call(
        functools.partial(_idft_cols_kernel, nh=nh),
        grid=(cols // tc,),
        in_specs=[pl.BlockSpec((nh2, n1), lambda j: (0, 0)),
                  pl.BlockSpec((2, n1, tc), lambda j: (0, 0, j)),
                  row(0), row(1), row(0), row(1),
                  pl.BlockSpec((1, tc), lambda j: (0, j))],
        out_specs=pl.BlockSpec((2, nh, tc), lambda j: (0, 0, j)),
        out_shape=jax.ShapeDtypeStruct((2, nh, cols), F32),
        compiler_params=_cparams(("parallel",)),
        name="idft_cols",
    )(cstack, bv, gate, gate, xin, xin, bias_cols)


def _hy_ctx_kernel(x1_ref, x2_ref, v_ref, fw_ref, bw_ref, nrm_ref, bias_ref, ff_ref, ci_ref, o_ref, *, lc):
    n = 2 * lc
    c = HY_WIDTH
    ff = ff_ref[...]
    ci = ci_ref[...]
    pf = _mm(ff, fw_ref[...])
    pb = _mm(ff, bw_ref[...])
    s = 1.0 / (nrm_ref[...] * float(n))
    kr = (pf[:n] + pb[:n]) * s
    ki = (pf[n:] - pb[n:]) * s

    def conv(z0, z1, o):
        xr, xi = _cplx_left(ff, z0, z1, n)
        krr = kr[:, o * c:(o + 1) * c]
        kii = ki[:, o * c:(o + 1) * c]
        return _cplx_left(ci, xr * krr - xi * kii, xr * kii + xi * krr, lc)

    v0 = v_ref[0]
    v1 = v_ref[1]
    y0, y1 = conv(v0, v1, 0)
    z0 = x1_ref[0] * (y0 + v0 * bias_ref[0:1])
    z1 = x1_ref[1] * (y1 + v1 * bias_ref[0:1])
    y0, y1 = conv(z0, z1, 1)
    o_ref[0] = x2_ref[0] * (y0 + z0 * bias_ref[1:2])
    o_ref[1] = x2_ref[1] * (y1 + z1 * bias_ref[1:2])


def _hy_ctx(x1, x2, v, fw, bw, nrm, bias, ff, ci):
    b, lc, c = v.shape
    vm = pl.BlockSpec(memory_space=pltpu.VMEM)
    return pl.pallas_call(
        functools.partial(_hy_ctx_kernel, lc=lc),
        in_specs=[vm] * 9,
        out_specs=vm,
        out_shape=jax.ShapeDtypeStruct((b, lc, c), F32),
        compiler_params=pltpu.CompilerParams(vmem_limit_bytes=VMEM_LIMIT),
        name="hy_ctx",
    )(x1, x2, v, fw, bw, nrm, bias, ff, ci)


def _outproj_kernel(attc_ref, attl_ref, rw_ref, hyc_ref, hyl_ref, x_ref, mod_ref, w_ref, lg_ref, lb_ref, o_ref,
                    *, lc, tm):
    row0 = pl.program_id(1) * tm
    a0 = ATT_WIDTH
    a1 = ATT_WIDTH + RWKV_WIDTH
    is_ctx = row0 < lc
    att = jnp.where(is_ctx, attc_ref[0], attl_ref[0])
    hy = jnp.where(is_ctx, hyc_ref[0], hyl_ref[0])
    o = jnp.dot(att, w_ref[:a0], preferred_element_type=F32)
    o += jnp.dot(rw_ref[0], w_ref[a0:a1], preferred_element_type=F32)
    o += jnp.dot(hy.astype(BF16), w_ref[a1:], preferred_element_type=F32)
    g = _sel_mod(mod_ref, 2, row0, tm, lc)
    y = ALPHA * x_ref[0] + g * o
    o_ref[0] = _layer_norm(y) * lg_ref[...] + lb_ref[...]


def _outproj(att_c, att_l, rw, hy_c, hy_l, xx, mod, w, lg, lb, lc):
    b, lt, d = xx.shape
    tm = TOK_TILE
    nct = lc // tm
    tok = lambda c: pl.BlockSpec((1, tm, c), lambda bi, i: (bi, i, 0))
    ctx = lambda c: pl.BlockSpec((1, tm, c), lambda bi, i: (bi, jnp.minimum(i, nct - 1), 0))
    lat = lambda c: pl.BlockSpec((1, tm, c), lambda bi, i: (bi, jnp.maximum(i - nct, 0), 0))
    full = lambda shape: pl.BlockSpec(shape, lambda bi, i: (0,) * len(shape))
    return pl.pallas_call(
        functools.partial(_outproj_kernel, lc=lc, tm=tm),
        grid=(b, lt // tm),
        in_specs=[ctx(ATT_WIDTH), lat(ATT_WIDTH), tok(RWKV_WIDTH), ctx(HY_WIDTH), lat(HY_WIDTH), tok(d),
                  pl.BlockSpec((1, 2, 6, d), lambda bi, i: (bi, 0, 0, 0)),
                  full(w.shape), full((1, d)), full((1, d))],
        out_specs=tok(d),
        out_shape=jax.ShapeDtypeStruct((b, lt, d), F32),
        compiler_params=_cparams(("parallel", "parallel")),
        name="outproj",
    )(att_c, att_l, rw, hy_c, hy_l, xx, mod, w, lg, lb)


def _ffn_kernel(x_ref, mod_ref, w1_ref, w3_ref, w2_ref, lg_ref, lb_ref, o_ref, h_scr, acc_scr, *, lc, tm, nf):
    row0 = pl.program_id(1) * tm
    f = pl.program_id(2)

    @pl.when(f == 0)
    def _():
        sh = _sel_mod(mod_ref, 3, row0, tm, lc)
        sc = _sel_mod(mod_ref, 4, row0, tm, lc)
        h_scr[...] = (_layer_norm(x_ref[0]) * (1.0 + sc) + sh).astype(BF16)
        acc_scr[...] = jnp.zeros(acc_scr.shape, F32)

    h = h_scr[...]
    a = jnp.dot(h, w1_ref[...], preferred_element_type=F32)
    g = jnp.dot(h, w3_ref[...], preferred_element_type=F32)
    acc_scr[...] += jnp.dot((_silu(a) * g).astype(BF16), w2_ref[...], preferred_element_type=F32)

    @pl.when(f == nf - 1)
    def _():
        gate = _sel_mod(mod_ref, 5, row0, tm, lc)
        y = ALPHA * x_ref[0] + gate * acc_scr[...]
        o_ref[0] = _layer_norm(y) * lg_ref[...] + lb_ref[...]


def _ffn(xx, mod, w1, w3, w2, lg, lb, lc):
    b, lt, d = xx.shape
    ff = w1.shape[1]
    tm = _pick_tile(lt, (1280, 768, 512, 256))
    tf = _pick_tile(ff, (256, 128))
    nf = ff // tf
    return pl.pallas_call(
        functools.partial(_ffn_kernel, lc=lc, tm=tm, nf=nf),
        grid=(b, lt // tm, nf),
        in_specs=[pl.BlockSpec((1, tm, d), lambda bi, i, f: (bi, i, 0)),
                  pl.BlockSpec((1, 2, 6, d), lambda bi, i, f: (bi, 0, 0, 0)),
                  pl.BlockSpec((d, tf), lambda bi, i, f: (0, f)),
                  pl.BlockSpec((d, tf), lambda bi, i, f: (0, f)),
                  pl.BlockSpec((tf, d), lambda bi, i, f: (f, 0)),
                  pl.BlockSpec((1, d), lambda bi, i, f: (0, 0)),
                  pl.BlockSpec((1, d), lambda bi, i, f: (0, 0))],
        out_specs=pl.BlockSpec((1, tm, d), lambda bi, i, f: (bi, i, 0)),
        out_shape=jax.ShapeDtypeStruct((b, lt, d), F32),
        scratch_shapes=[pltpu.VMEM((tm, d), BF16), pltpu.VMEM((tm, d), F32)],
        compiler_params=_cparams(("parallel", "parallel", "arbitrary")),
        name="ffn",
    )(xx, mod, w1, w3, w2, lg, lb)


def _route(x_ref, mod_ref, wr_ref, row0, tm, lc):
    sh = _sel_mod(mod_ref, 3, row0, tm, lc)
    sc = _sel_mod(mod_ref, 4, row0, tm, lc)
    h = _layer_norm(x_ref[0]) * (1.0 + sc) + sh
    logits = _dotp(h, wr_ref[...], 3)
    lane = lax.broadcasted_iota(jnp.int32, logits.shape, 1)
    neg = jnp.float32(-jnp.inf)
    lg = jnp.where(lane < N_EXPERTS, logits, neg)
    m1 = jnp.max(lg, -1, keepdims=True)
    i1 = jnp.min(jnp.where(lg == m1, lane, LANES), -1, keepdims=True)
    lg2 = jnp.where(lane == i1, neg, lg)
    m2 = jnp.max(lg2, -1, keepdims=True)
    i2 = jnp.min(jnp.where(lg2 == m2, lane, LANES), -1, keepdims=True)
    e2 = jnp.exp(m2 - m1)
    return h, lane, i1, i2, 1.0 / (1.0 + e2), e2 / (1.0 + e2)


def _moe_sparse_kernel(x_ref, mod_ref, wr_ref, w1_ref, w3_ref, w2_ref, lg_ref, lb_ref, o_ref,
                       h_scr, acc_scr, gate_scr, posc_scr, posr_scr, xs_scr, ye_scr, nblk_scr, *, lc, tm, nf, ns, blk):
    row0 = pl.program_id(1) * tm
    s = pl.program_id(2)
    e = s // nf
    f = s % nf

    @pl.when(s == 0)
    def _():
        h, lane, i1, i2, g1, g2 = _route(x_ref, mod_ref, wr_ref, row0, tm, lc)
        h_scr[...] = h.astype(BF16)
        acc_scr[...] = jnp.zeros(acc_scr.shape, F32)
        routed = jnp.where(jnp.logical_or(lane == i1, lane == i2), 1.0, 0.0)
        r = lax.broadcasted_iota(jnp.int32, (tm, tm), 0)
        c = lax.broadcasted_iota(jnp.int32, (tm, tm), 1)
        before = jnp.where(c < r, 1.0, 0.0).astype(BF16)
        rank_c = jnp.dot(before, routed.astype(BF16), preferred_element_type=F32)
        gate_scr[...] = jnp.where(lane == i1, g1, 0.0) + jnp.where(lane == i2, g2, 0.0)
        posc_scr[...] = jnp.where(routed > 0.0, rank_c, -1.0)
        routed_t = routed.T[:2 * SUBLANES]
        after = jnp.where(r < c, 1.0, 0.0).astype(BF16)
        rank_r = jnp.dot(routed_t.astype(BF16), after, preferred_element_type=F32)
        posr_scr[...] = jnp.where(routed_t > 0.0, rank_r, -1.0)
        counts = jnp.sum(routed, 0, keepdims=True)
        lane1 = lax.broadcasted_iota(jnp.int32, counts.shape, 1)
        for ex in range(N_EXPERTS):
            n_rows = jnp.sum(jnp.where(lane1 == ex, counts, 0.0)).astype(jnp.int32)
            nblk_scr[ex] = (n_rows + (blk - 1)) // blk

    nblk = nblk_scr[e]
    sizes = [(k + 1) * blk for k in range(tm // blk)]

    def per_size(cond, fn):
        for k, m in enumerate(sizes):
            pl.when(jnp.logical_and(cond, nblk == k + 1))(functools.partial(fn, m))

    def gather(m):
        slot = lax.broadcasted_iota(jnp.int32, (m, tm), 0).astype(F32)
        take = jnp.where(slot == posr_scr[pl.ds(e, 1), :], 1.0, 0.0).astype(BF16)
        xs_scr[:m] = jnp.dot(take, h_scr[...], preferred_element_type=F32).astype(BF16)
        ye_scr[:m] = jnp.zeros((m, ye_scr.shape[1]), F32)

    def expert(m):
        xs = xs_scr[:m]
        a = jnp.dot(xs, w1_ref[0], preferred_element_type=F32)
        g = jnp.dot(xs, w3_ref[0], preferred_element_type=F32)
        ye_scr[:m] += jnp.dot((_silu(a) * g).astype(BF16), w2_ref[0], preferred_element_type=F32)

    def scatter(m):
        lane = lax.broadcasted_iota(jnp.int32, (tm, LANES), 1)
        column = lambda ref: jnp.sum(jnp.where(lane == e, ref[...], 0.0), -1, keepdims=True)
        slot = lax.broadcasted_iota(jnp.int32, (tm, m), 1).astype(F32)
        put = jnp.where(slot == column(posc_scr), 1.0, 0.0).astype(BF16)
        y = jnp.dot(put, ye_scr[:m].astype(BF16), preferred_element_type=F32)
        acc_scr[...] += column(gate_scr) * y

    per_size(f == 0, gather)
    per_size(True, expert)
    per_size(f == nf - 1, scatter)

    @pl.when(s == ns - 1)
    def _():
        gate = _sel_mod(mod_ref, 5, row0, tm, lc)
        y = ALPHA * x_ref[0] + gate * acc_scr[...]
        o_ref[0] = _layer_norm(y) * lg_ref[...] + lb_ref[...]


def _moe_sparse(xx, mod, wr, w1, w3, w2, lg, lb, lc):
    b, lt, d = xx.shape
    ne, _, ff = w1.shape
    tm = _pick_tile(lt, (1280, 768, 512, 256))
    tf = _pick_tile(ff, (256, 128))
    nf = ff // tf
    ns = ne * nf
    return pl.pallas_call(
        functools.partial(_moe_sparse_kernel, lc=lc, tm=tm, nf=nf, ns=ns, blk=MOE_ROW_BLOCK),
        grid=(b, lt // tm, ns),
        in_specs=[pl.BlockSpec((1, tm, d), lambda bi, i, s: (bi, i, 0)),
                  pl.BlockSpec((1, 2, 6, d), lambda bi, i, s: (bi, 0, 0, 0)),
                  pl.BlockSpec((d, LANES), lambda bi, i, s: (0, 0)),
                  pl.BlockSpec((1, d, tf), lambda bi, i, s: (s // nf, 0, s % nf)),
                  pl.BlockSpec((1, d, tf), lambda bi, i, s: (s // nf, 0, s % nf)),
                  pl.BlockSpec((1, tf, d), lambda bi, i, s: (s // nf, s % nf, 0)),
                  pl.BlockSpec((1, d), lambda bi, i, s: (0, 0)),
                  pl.BlockSpec((1, d), lambda bi, i, s: (0, 0))],
        out_specs=pl.BlockSpec((1, tm, d), lambda bi, i, s: (bi, i, 0)),
        out_shape=jax.ShapeDtypeStruct((b, lt, d), F32),
        scratch_shapes=[pltpu.VMEM((tm, d), BF16), pltpu.VMEM((tm, d), F32),
                        pltpu.VMEM((tm, LANES), F32), pltpu.VMEM((tm, LANES), F32),
                        pltpu.VMEM((2 * SUBLANES, tm), F32),
                        pltpu.VMEM((tm, d), BF16), pltpu.VMEM((tm, d), F32),
                        pltpu.SMEM((ne,), jnp.int32)],
        compiler_params=_cparams(("parallel", "parallel", "arbitrary"), VMEM_LIMIT_MOE),
        name="moe_sparse",
    )(xx, mod, wr, w1, w3, w2, lg, lb)


def _rope_tables(l, lc):
    rows = l // GRID_W
    row = jnp.repeat(jnp.arange(rows, dtype=F32), GRID_W)
    col = jnp.tile(jnp.arange(GRID_W, dtype=F32), rows)
    n_freq = HEAD_DIM // 4
    inv_freq = ROPE_THETA ** (-jnp.arange(n_freq, dtype=F32) / n_freq)
    ang = jnp.concatenate([row[:, None] * inv_freq, col[:, None] * inv_freq], -1)
    cos, sin = jnp.cos(ang), jnp.sin(ang)
    cos2 = jnp.concatenate([jnp.ones((lc, LANES), F32), jnp.concatenate([cos, cos, cos, cos], -1)], 0)
    sin2 = jnp.concatenate([jnp.zeros((lc, LANES), F32), jnp.concatenate([-sin, sin, -sin, sin], -1)], 0)
    return cos2, sin2


def _hy_features(l):
    bands = (HY_EMB - 1) // 2
    t = jnp.linspace(0.0, 1.0, l, dtype=F32)[:, None]
    f = jnp.linspace(1e-4, bands - 1, bands, dtype=F32)[None, :]
    wt = 2.0 * math.pi * jnp.arange(l, dtype=F32)[:, None] / l
    z = jnp.concatenate([t, jnp.cos(f * wt), -jnp.sin(f * wt)], -1)
    return jnp.pad(z, ((0, 0), (0, LANES - HY_EMB)))


def _angle(idx, n):
    return (2.0 * math.pi / n) * (idx % n).astype(F32)


def _dft_tables(n1):
    nh = n1 // 2
    n2 = DFT_N2
    n = n1 * n2
    k1 = jnp.arange(n1, dtype=jnp.int32)
    a1 = _angle(k1[:, None] * jnp.arange(nh, dtype=jnp.int32)[None, :], n1)
    fstack = jnp.concatenate([jnp.cos(a1), -jnp.sin(a1)], 0)
    cstack = jnp.concatenate([jnp.cos(a1.T), jnp.sin(a1.T)], 0)
    k2 = jnp.arange(n2, dtype=jnp.int32)
    at = _angle(k1[:, None] * k2[None, :], n)
    tr, ti = jnp.cos(at), -jnp.sin(at)
    a2 = _angle(k2[:, None] * k2[None, :], n2)
    fr, fi = jnp.cos(a2), -jnp.sin(a2)
    g = jnp.stack([tr[:, None, :] * fr[None] - ti[:, None, :] * fi[None],
                   tr[:, None, :] * fi[None] + ti[:, None, :] * fr[None]], 1)
    gh = jnp.stack([tr[:, :, None] * fr[None] - ti[:, :, None] * fi[None],
                    -(tr[:, :, None] * fi[None] + ti[:, :, None] * fr[None])], 1)
    return fstack, cstack, g, gh


def _dense_dft_tables(lc):
    n = 2 * lc
    a = _angle(jnp.arange(n, dtype=jnp.int32)[:, None] * jnp.arange(lc, dtype=jnp.int32)[None, :], n)
    ff = jnp.concatenate([jnp.cos(a), -jnp.sin(a)], 0)
    ci = jnp.concatenate([jnp.cos(a.T), jnp.sin(a.T)], 0)
    return ff, ci


def kernel(x, c, ctx, c_ctx, ada_w, ada_b, w_in, w_out, q_gain, k_gain, rwkv_mu, rwkv_w0, rwkv_wB, rwkv_a0, rwkv_aB, rwkv_gB, rwkv_kk, rwkv_ka, rwkv_rk, rwkv_gn_g, rwkv_gn_b, hy_short_w, hy_short_b, hy_w1, hy_b1, hy_freq1, hy_w2, hy_b2, hy_freq2, hy_w3, hy_decay, hy_bias, ln1_g, ln1_b, ln2_g, ln2_b, ffn_w1, ffn_w3, ffn_w2, moe_router, moe_w1, moe_w3, moe_w2):
    b, l, d = x.shape
    lc = ctx.shape[1]
    lt = lc + l
    depth = ada_w.shape[0]
    assert b == 2, "the long convolution packs the two batch rows as one complex signal"
    assert d == D_MODEL and lc % TOK_TILE == 0 and l % TOK_TILE == 0 and (2 * l) % (2 * DFT_N2) == 0
    cw = RWKV_WIDTH

    xx = jnp.concatenate([ctx, x], 1)
    cond8 = jnp.zeros((SUBLANES, d), F32).at[:b].set(c).at[b].set(c_ctx)
    mod_all = _ada_mod(cond8, ada_w, ada_b)

    cos64, sin64 = _rope_tables(l, lc)
    n1 = 2 * l // DFT_N2
    nh = n1 // 2
    cols = DFT_N2 * HY_WIDTH
    fstack, cstack, g_tab, gh_tab = (t.astype(BF16) for t in _dft_tables(n1))
    ff_c, ci_c = _dense_dft_tables(lc)
    feat_l = _hy_features(l)
    feat_c = _hy_features(lc)
    blk = jnp.arange(cw) // HEAD_DIM
    bd = (blk[:, None] == blk[None, :]).astype(F32)
    ch = jnp.arange(ATT_WIDTH)
    bd_att = (ch[:, None] // HEAD_DIM == ch[None, :] // HEAD_DIM).astype(BF16)
    swap_att = (ch[:, None] == (ch[None, :] + HALF_HD) % HEAD_DIM + (ch[None, :] // HEAD_DIM) * HEAD_DIM
                ).astype(BF16)
    perm64 = jnp.concatenate([jnp.arange(0, HEAD_DIM, 2), jnp.arange(1, HEAD_DIM, 2)])
    perm_att = jnp.concatenate([h * HEAD_DIM + perm64 for h in range(ATT_HEADS + ATT_KV_HEADS)]
                               + [jnp.arange(ATT_WIDTH + ATT_KV_WIDTH, IN_ATT)])
    tq = TOK_TILE
    qn = next(n for n in (4, 2, 1) if (l // tq) % n == 0)
    tk = _pick_tile(lt, (3328, 1280, 1024, 768, 512, 256))

    for li in range(depth):
        ml = mod_all[li]
        mod = jnp.stack([jnp.broadcast_to(ml[b].reshape(1, 6, d), (b, 6, d)), ml[:b].reshape(b, 6, d)], 1)
        wi = w_in[li]
        w_pad = jnp.concatenate([wi[:, :IN_ATT][:, perm_att], wi[:, IN_ATT:IN_ATT + IN_RWKV],
                                 jnp.zeros((d, IN_RWKV_PAD - IN_RWKV), F32), wi[:, IN_ATT + IN_RWKV:]],
                                1).astype(BF16)
        ua, ur, uh = _inproj(xx, mod, w_pad, lc)

        two = lambda gain: jnp.tile(gain[perm64], 2)[None]
        qt, kx, vt = _attn_prep(ua, cos64, sin64, two(q_gain[li]), two(k_gain[li]), bd_att, swap_att, lc)
        att_c = _flash(qt, kx, vt, tq, TOK_TILE, 1, l // tq, lc // tq, lc // TOK_TILE)
        s_bound = (HEAD_DIM ** 0.5) * LOG2E * jnp.max(jnp.abs(q_gain[li])) * jnp.max(jnp.abs(k_gain[li]))
        lat_args = (qt, kx, vt, tq, tk, qn, 0, l // (qn * tq), lt // tk)
        att_l = lax.cond(s_bound <= MAX_UNSHIFTED_SCORE,
                         lambda: _flash(*lat_args, bounded=True), lambda: _flash(*lat_args, bounded=False))

        wl = jnp.zeros((LANES, 5 * cw), F32)
        wl = wl.at[0:W_LORA, 0:cw].set(rwkv_wB[li, 0]).at[W_LORA:2 * W_LORA, cw:2 * cw].set(rwkv_wB[li, 1])
        o_a = 2 * W_LORA
        wl = wl.at[o_a:o_a + A_LORA, 2 * cw:3 * cw].set(rwkv_aB[li, 0])
        wl = wl.at[o_a + A_LORA:o_a + 2 * A_LORA, 3 * cw:4 * cw].set(rwkv_aB[li, 1])
        o_g = o_a + 2 * A_LORA
        wl = wl.at[o_g:o_g + G_LORA, 4 * cw:5 * cw].set(rwkv_gB[li])
        mu = jnp.pad(rwkv_mu[li], ((0, 0), (0, IN_RWKV_PAD - IN_RWKV)))
        r_, v_, kk_, g_, bon_, lw_, be_, kd_ = _rwkv_prep(
            ur, mu, wl, bd, rwkv_w0[li], rwkv_a0[li], rwkv_kk[li][None], rwkv_ka[li][None],
            rwkv_rk[li].reshape(1, cw), lc)
        yf = _wkv_scan(r_, v_, kk_, lw_, be_, kd_, lc, True)
        yb = _wkv_scan(r_, v_, kk_, lw_, be_, kd_, lc, False)
        rw = _rwkv_out(yf, yb, bon_, g_, bd, rwkv_gn_g[li][None], rwkv_gn_b[li][None])

        x1, x2, vv = _hy_prep(uh, hy_short_w[li], hy_short_b[li][None], lc)
        w1p = jnp.pad(hy_w1[li], ((0, LANES - HY_EMB), (0, 0)))
        fargs = (w1p, hy_b1[li][None], hy_freq1[li][None], hy_w2[li], hy_b2[li][None], hy_freq2[li][None],
                 hy_w3[li], hy_decay[li][None])
        fw, bw, nrm = _hy_filter(feat_l, *fargs)
        c2 = HY_ORDER * HY_WIDTH
        a4 = _dft_cols(fstack, fw.reshape(1, nh, DFT_N2 * c2), 0, bw.reshape(1, nh, DFT_N2 * c2), 0, True)
        kspec = _spec(a4.reshape(4, n1, DFT_N2, c2), g_tab, nrm, n1 * DFT_N2)
        lat = lambda t: t[:, lc:].reshape(b, nh, cols)
        x1l, x2l, zin = lat(x1), lat(x2), lat(vv)
        for o, gate in enumerate((x1l, x2l)):
            a = _dft_cols(fstack, zin, 0, zin, 1, False)
            bv = _conv_mid(a.reshape(2, n1, DFT_N2, HY_WIDTH), g_tab, gh_tab, kspec, o)
            bias_cols = jnp.tile(hy_bias[li, o], DFT_N2)[None]
            zin = _idft_cols(cstack, bv.reshape(2, n1, cols), gate, zin, bias_cols)
        hy_l = zin.reshape(b, l, HY_WIDTH)
        fw_c, bw_c, nrm_c = _hy_filter(feat_c, *fargs)
        hy_c = _hy_ctx(x1[:, :lc], x2[:, :lc], vv[:, :lc], fw_c, bw_c, nrm_c, hy_bias[li], ff_c, ci_c)

        xx = _outproj(att_c, att_l, rw, hy_c, hy_l, xx, mod, w_out[li].astype(BF16),
                      ln1_g[li][None], ln1_b[li][None], lc)

        j = li // 2
        if li % 2 == 0:
            xx = _ffn(xx, mod, ffn_w1[j].astype(BF16), ffn_w3[j].astype(BF16), ffn_w2[j].astype(BF16),
                      ln2_g[li][None], ln2_b[li][None], lc)
        else:
            wr = jnp.pad(moe_router[j], ((0, 0), (0, LANES - N_EXPERTS)))
            xx = _moe_sparse(xx, mod, wr, moe_w1[j].astype(BF16), moe_w3[j].astype(BF16), moe_w2[j].astype(BF16),
                             ln2_g[li][None], ln2_b[li][None], lc)
    return xx[:, lc:]
```

```python
import functools
import math

import jax
import jax.numpy as jnp
from jax import lax
from jax.experimental import pallas as pl
from jax.experimental.pallas import tpu as pltpu

F32 = jnp.float32
BF16 = jnp.bfloat16
HI = lax.Precision.HIGHEST

D_MODEL = 1024
DEPTH = 2
GRID_W = 64
HEAD_DIM = 64
HALF_HD = HEAD_DIM // 2
ATT_WIDTH = 512
RWKV_WIDTH = 256
HY_WIDTH = 256
ATT_HEADS = 8
ATT_KV_HEADS = 2
ATT_REP = 4
ATT_KV_WIDTH = 128
ROPE_THETA = 10000.0
QK_EPS = 1e-6
RWKV_HEADS = 4
W_LORA = 16
A_LORA = 16
G_LORA = 32
RWKV_GN_EPS = 64e-5
HY_ORDER = 2
HY_EMB = 33
HY_FFN = 64
N_EXPERTS = 8
LN_EPS = 1e-6
IN_ATT = ATT_WIDTH + 2 * ATT_KV_WIDTH
IN_RWKV = 3 * RWKV_WIDTH + 2 * W_LORA + 2 * A_LORA + G_LORA
IN_RWKV_PAD = 896
IN_HY = 3 * HY_WIDTH
ALPHA = float((2 * DEPTH) ** 0.25)
LOG2E = 1.4426950408889634
MAX_UNSHIFTED_SCORE = 40.0

LANES = 128
SUBLANES = 8
TOK_TILE = 256
WKV_CHUNK = 64
WKV_TILE = 256
DFT_N2 = 256
MOE_ROW_BLOCK = 256
VMEM_LIMIT = 48 * 1024 * 1024
VMEM_LIMIT_MOE = 56 * 1024 * 1024


def _cparams(sem, vmem=VMEM_LIMIT):
    return pltpu.CompilerParams(dimension_semantics=sem, vmem_limit_bytes=vmem)


def _pick_tile(n, cands):
    for c in cands:
        if n % c == 0:
            return c
    raise ValueError(f"no tile for {n} in {cands}")


def _layer_norm(x):
    mu = jnp.mean(x, -1, keepdims=True)
    xc = x - mu
    var = jnp.mean(xc * xc, -1, keepdims=True)
    return xc * lax.rsqrt(var + LN_EPS)


def _sel_mod(mod_ref, j, row0, tm, lc):
    rid = row0 + lax.broadcasted_iota(jnp.int32, (tm, 1), 0)
    return jnp.where(rid < lc, mod_ref[0, 0, j:j + 1, :], mod_ref[0, 1, j:j + 1, :])


def _silu(x):
    return x * jax.nn.sigmoid(x)


def _ada_kernel(c_ref, w_ref, b_ref, o_ref):
    s = _silu(c_ref[...])
    o_ref[0] = jnp.dot(s, w_ref[0], precision=HI, preferred_element_type=F32) + b_ref[0]


def _ada_mod(cond8, ada_w, ada_b):
    depth, d, n = ada_w.shape
    tn = _pick_tile(n, (1536, 1024, 512, 256, 128))
    return pl.pallas_call(
        _ada_kernel,
        grid=(depth, n // tn),
        in_specs=[pl.BlockSpec((SUBLANES, d), lambda l, j: (0, 0)),
                  pl.BlockSpec((1, d, tn), lambda l, j: (l, 0, j)),
                  pl.BlockSpec((1, 1, tn), lambda l, j: (l, 0, j))],
        out_specs=pl.BlockSpec((1, SUBLANES, tn), lambda l, j: (l, 0, j)),
        out_shape=jax.ShapeDtypeStruct((depth, SUBLANES, n), F32),
        compiler_params=_cparams(("parallel", "parallel")),
        name="ada_mod",
    )(cond8, ada_w, ada_b.reshape(depth, 1, n))


def _inproj_kernel(x_ref, mod_ref, w_ref, oa_ref, or_ref, oh_ref, *, lc, tm):
    row0 = pl.program_id(1) * tm
    sh = _sel_mod(mod_ref, 0, row0, tm, lc)
    sc = _sel_mod(mod_ref, 1, row0, tm, lc)
    h = (_layer_norm(x_ref[0]) * (1.0 + sc) + sh).astype(BF16)
    u = jnp.dot(h, w_ref[...], preferred_element_type=F32)
    oa_ref[0] = u[:, :IN_ATT]
    or_ref[0] = u[:, IN_ATT:IN_ATT + IN_RWKV_PAD]
    oh_ref[0] = u[:, IN_ATT + IN_RWKV_PAD:]


def _inproj(xx, mod, w_pad, lc):
    b, lt, d = xx.shape
    tm = TOK_TILE
    n = w_pad.shape[1]
    return pl.pallas_call(
        functools.partial(_inproj_kernel, lc=lc, tm=tm),
        grid=(b, lt // tm),
        in_specs=[pl.BlockSpec((1, tm, d), lambda bi, i: (bi, i, 0)),
                  pl.BlockSpec((1, 2, 6, d), lambda bi, i: (bi, 0, 0, 0)),
                  pl.BlockSpec((d, n), lambda bi, i: (0, 0))],
        out_specs=[pl.BlockSpec((1, tm, IN_ATT), lambda bi, i: (bi, i, 0)),
                   pl.BlockSpec((1, tm, IN_RWKV_PAD), lambda bi, i: (bi, i, 0)),
                   pl.BlockSpec((1, tm, IN_HY), lambda bi, i: (bi, i, 0))],
        out_shape=[jax.ShapeDtypeStruct((b, lt, IN_ATT), F32),
                   jax.ShapeDtypeStruct((b, lt, IN_RWKV_PAD), F32),
                   jax.ShapeDtypeStruct((b, lt, IN_HY), F32)],
        compiler_params=_cparams(("parallel", "parallel")),
        name="inproj",
    )(xx, mod, w_pad)


def _attn_prep_kernel(u_ref, cos_ref, sin_ref, qg_ref, kg_ref, bd_ref, sw_ref, qt_ref, k_ref, vt_ref):
    u = u_ref[0]

    def norm_rope(x, g):
        w = x.shape[1]
        tile = lambda t: jnp.concatenate([t] * (w // LANES), -1)
        ms = _dot01(x * x, bd_ref[:w, :w]) * (1.0 / HEAD_DIM)
        xn = x * lax.rsqrt(ms + QK_EPS) * tile(g)
        sw = _dot01(xn, sw_ref[:w, :w])
        return xn * tile(cos_ref[...]) + sw * tile(sin_ref[...])

    q = norm_rope(u[:, :ATT_WIDTH], qg_ref[...]) * (LOG2E * HEAD_DIM ** -0.5)
    qt = q.T
    kx = norm_rope(u[:, ATT_WIDTH:ATT_WIDTH + ATT_KV_WIDTH], kg_ref[...])
    for g in range(ATT_KV_HEADS):
        base = g * ATT_REP * HEAD_DIM
        qt_ref[0, g, 0] = jnp.concatenate(
            [qt[base + r * HEAD_DIM:base + (r + 1) * HEAD_DIM] for r in range(ATT_REP)], -1).astype(BF16)
        k_ref[0, g] = kx[:, g * HEAD_DIM:(g + 1) * HEAD_DIM].astype(BF16)
    v0 = ATT_WIDTH + ATT_KV_WIDTH
    vt = u[:, v0:v0 + ATT_KV_WIDTH].T
    for g in range(ATT_KV_HEADS):
        vt_ref[0, g] = vt[g * HEAD_DIM:(g + 1) * HEAD_DIM].astype(BF16)


def _attn_prep(ua, cos, sin, qg, kg, bd_att, swap_att, lc):
    b, lt, _ = ua.shape
    tm = TOK_TILE
    nct = lc // tm
    nl = lt // tm - nct
    q_pos = lambda i: jnp.where(i < nct, nl + i, i - nct)
    return pl.pallas_call(
        _attn_prep_kernel,
        grid=(b, lt // tm),
        in_specs=[pl.BlockSpec((1, tm, IN_ATT), lambda bi, i: (bi, i, 0)),
                  pl.BlockSpec((tm, LANES), lambda bi, i: (i, 0)),
                  pl.BlockSpec((tm, LANES), lambda bi, i: (i, 0)),
                  pl.BlockSpec((1, LANES), lambda bi, i: (0, 0)),
                  pl.BlockSpec((1, LANES), lambda bi, i: (0, 0)),
                  pl.BlockSpec((ATT_WIDTH, ATT_WIDTH), lambda bi, i: (0, 0)),
                  pl.BlockSpec((ATT_WIDTH, ATT_WIDTH), lambda bi, i: (0, 0))],
        out_specs=[pl.BlockSpec((1, ATT_KV_HEADS, 1, HEAD_DIM, ATT_REP * tm),
                                lambda bi, i: (bi, 0, q_pos(i), 0, 0)),
                   pl.BlockSpec((1, ATT_KV_HEADS, tm, HEAD_DIM), lambda bi, i: (bi, 0, i, 0)),
                   pl.BlockSpec((1, ATT_KV_HEADS, HEAD_DIM, tm), lambda bi, i: (bi, 0, 0, i))],
        out_shape=[jax.ShapeDtypeStruct((b, ATT_KV_HEADS, lt // tm, HEAD_DIM, ATT_REP * tm), BF16),
                   jax.ShapeDtypeStruct((b, ATT_KV_HEADS, lt, HEAD_DIM), BF16),
                   jax.ShapeDtypeStruct((b, ATT_KV_HEADS, HEAD_DIM, lt), BF16)],
        compiler_params=_cparams(("parallel", "parallel")),
        name="attn_prep",
    )(ua, cos, sin, qg, kg, bd_att, swap_att)


def _flash_kernel(qt_ref, k_ref, vt_ref, o_ref, m_scr, l_scr, acc_scr, *, nk, tq, sub, qn):
    j = pl.program_id(3)

    @pl.when(j == 0)
    def _():
        m_scr[...] = jnp.full(m_scr.shape, -jnp.inf, F32)
        l_scr[...] = jnp.zeros(l_scr.shape, F32)
        acc_scr[...] = jnp.zeros(acc_scr.shape, F32)

    qt = jnp.concatenate([qt_ref[0, 0, t] for t in range(qn)], -1)
    nsub = k_ref.shape[2] // sub
    m = m_scr[...]
    l = l_scr[...]
    acc = acc_scr[...]
    scores = lambda c: jnp.dot(k_ref[0, 0, c * sub:(c + 1) * sub, :], qt, preferred_element_type=F32)
    pv = lambda c, p: jnp.dot(vt_ref[0, 0, :, c * sub:(c + 1) * sub], p, preferred_element_type=F32)
    s_next = scores(0)
    pend = None
    for c in range(nsub):
        s = s_next
        if c + 1 < nsub:
            s_next = scores(c + 1)
        if pend is not None:
            acc = pend[0] * acc + pv(c - 1, pend[1])
        m_new = jnp.maximum(m, jnp.max(s, 0, keepdims=True))
        a = jnp.exp2(m - m_new)
        p = jnp.exp2(s - m_new)
        l = a * l + jnp.sum(p, 0, keepdims=True)
        pend = (a, p.astype(BF16))
        m = m_new
    acc = pend[0] * acc + pv(nsub - 1, pend[1])
    m_scr[...] = m
    l_scr[...] = l
    acc_scr[...] = acc

    @pl.when(j == nk - 1)
    def _():
        o = (acc / l).T
        for t in range(qn):
            o_ref[0, t * tq:(t + 1) * tq, :] = jnp.concatenate(
                [o[(t * ATT_REP + r) * tq:(t * ATT_REP + r + 1) * tq] for r in range(ATT_REP)], -1
            ).astype(o_ref.dtype)


def _flash_bounded_kernel(qt_ref, k_ref, vt_ref, o_ref, l_scr, acc_scr, *, nk, tq, sub, qn):
    j = pl.program_id(3)

    @pl.when(j == 0)
    def _():
        l_scr[...] = jnp.zeros(l_scr.shape, F32)
        acc_scr[...] = jnp.zeros(acc_scr.shape, F32)

    qt = jnp.concatenate([qt_ref[0, 0, t] for t in range(qn)], -1)
    nsub = k_ref.shape[2] // sub

    l = l_scr[...]
    acc = acc_scr[...]
    scores = lambda c: jnp.dot(k_ref[0, 0, c * sub:(c + 1) * sub, :], qt, preferred_element_type=F32)
    s_next = scores(0)
    for c in range(nsub):
        s = s_next
        if c + 1 < nsub:
            s_next = scores(c + 1)
        p = jnp.exp2(s)
        l = l + jnp.sum(p.reshape(sub // SUBLANES, SUBLANES, p.shape[1]), 0)
        acc = acc + jnp.dot(vt_ref[0, 0, :, c * sub:(c + 1) * sub], p.astype(BF16), preferred_element_type=F32)
    l_scr[...] = l
    acc_scr[...] = acc

    @pl.when(j == nk - 1)
    def _():
        o = (acc / jnp.sum(l, 0, keepdims=True)).T
        for t in range(qn):
            o_ref[0, t * tq:(t + 1) * tq, :] = jnp.concatenate(
                [o[(t * ATT_REP + r) * tq:(t * ATT_REP + r + 1) * tq] for r in range(ATT_REP)], -1
            ).astype(o_ref.dtype)


def _flash(qt, k, vt, tq, tk, qn, q_blk0, nq, nk, bounded=False):
    b = qt.shape[0]
    lq = nq * qn * tq
    sub = _pick_tile(tk, (256, 128))
    lanes = qn * ATT_REP * tq
    if bounded:
        body = functools.partial(_flash_bounded_kernel, nk=nk, tq=tq, sub=sub, qn=qn)
        scratch = [pltpu.VMEM((SUBLANES, lanes), F32), pltpu.VMEM((HEAD_DIM, lanes), F32)]
    else:
        body = functools.partial(_flash_kernel, nk=nk, tq=tq, sub=sub, qn=qn)
        scratch = [pltpu.VMEM((1, lanes), F32), pltpu.VMEM((1, lanes), F32), pltpu.VMEM((HEAD_DIM, lanes), F32)]
    return pl.pallas_call(
        body,
        grid=(b, ATT_KV_HEADS, nq, nk),
        in_specs=[pl.BlockSpec((1, 1, qn, HEAD_DIM, ATT_REP * tq), lambda bi, g, i, j: (bi, g, i + q_blk0, 0, 0)),
                  pl.BlockSpec((1, 1, tk, HEAD_DIM), lambda bi, g, i, j: (bi, g, j, 0)),
                  pl.BlockSpec((1, 1, HEAD_DIM, tk), lambda bi, g, i, j: (bi, g, 0, j))],
        out_specs=pl.BlockSpec((1, qn * tq, ATT_REP * HEAD_DIM), lambda bi, g, i, j: (bi, i, g)),
        out_shape=jax.ShapeDtypeStruct((b, lq, ATT_WIDTH), BF16),
        scratch_shapes=scratch,
        compiler_params=_cparams(("parallel", "parallel", "parallel", "arbitrary")),
        name="flash_bounded" if bounded else "flash",
    )(qt, k, vt)


def _prev_next(u, up8, un8, i, tm, lc, lt):
    start = i * tm
    p_ok = jnp.logical_and(start != 0, start != lc)
    n_ok = jnp.logical_and(start + tm != lc, start + tm != lt)
    prow = jnp.where(p_ok, up8[SUBLANES - 1:SUBLANES], 0.0)
    nrow = jnp.where(n_ok, un8[0:1], 0.0)
    rid = lax.broadcasted_iota(jnp.int32, u.shape, 0)
    prev = jnp.where(rid == 0, prow, pltpu.roll(u, 1, 0))
    nxt = jnp.where(rid == tm - 1, nrow, pltpu.roll(u, tm - 1, 0))
    return prev, nxt


def _halo_specs(tm, c, lt):
    r = tm // SUBLANES
    last = lt // SUBLANES - 1
    return [pl.BlockSpec((1, tm, c), lambda bi, i: (bi, i, 0)),
            pl.BlockSpec((1, SUBLANES, c), lambda bi, i: (bi, jnp.maximum(i * r - 1, 0), 0)),
            pl.BlockSpec((1, SUBLANES, c), lambda bi, i: (bi, jnp.minimum((i + 1) * r, last), 0))]


def _softplus(z):
    return jnp.maximum(z, 0.0) + jnp.log1p(jnp.exp(-jnp.abs(z)))


def _rwkv_prep_kernel(u_ref, up_ref, un_ref, mu_ref, wl_ref, bd_ref, w0_ref, a0_ref, kkw_ref, ka_ref, rk_ref,
                      r_o, v_o, kk_o, g_o, bon_o, lw_o, be_o, kd_o, *, lc, lt, tm):
    i = pl.program_id(1)
    u = u_ref[0]
    prev, nxt = _prev_next(u, up_ref[0], un_ref[0], i, tm, lc, lt)
    us = u + mu_ref[0:1] * (prev - u) + mu_ref[1:2] * (nxt - u)
    c = RWKV_WIDTH
    r = us[:, 0:c]
    k = us[:, c:2 * c]
    v = us[:, 2 * c:3 * c]
    slab = us[:, 3 * c:3 * c + LANES]
    lane = lax.broadcasted_iota(jnp.int32, slab.shape, 1)
    o_a = 2 * W_LORA
    o_g = o_a + 2 * A_LORA
    act = jnp.where(lane < o_a, jnp.tanh(slab),
                    jnp.where(lane < o_g, slab,
                              jnp.where(lane < o_g + G_LORA, jax.nn.sigmoid(slab), 0.0)))
    lo = _dotp(act, wl_ref[...], 3)
    bd = bd_ref[...]
    kk0 = k * kkw_ref[...]
    kk = kk0 * lax.rsqrt(_dot01(kk0 * kk0, bd) + 1e-12)
    r_o[0] = r
    v_o[0] = v
    kk_o[0] = kk
    g_o[0] = lo[:, 4 * c:5 * c]
    bon = jnp.zeros_like(r)
    for d in range(2):
        w_raw = w0_ref[d:d + 1] + lo[:, d * c:(d + 1) * c]
        lw = -jnp.exp(-_softplus(-w_raw) - 0.5)
        a = jax.nn.sigmoid(a0_ref[d:d + 1] + lo[:, (2 + d) * c:(3 + d) * c])
        kd = k * (1.0 + (a - 1.0) * ka_ref[...])
        lw_o[0, d] = lw
        be_o[0, d] = a * kk
        kd_o[0, d] = kd
        bon = bon + r * kd * rk_ref[...]
    bon_o[0] = _dot01(bon, bd) * v


def _rwkv_prep(ur, mu, wl, bd, w0, a0, kkw, ka, rk, lc):
    b, lt, cp = ur.shape
    tm = TOK_TILE
    c = RWKV_WIDTH
    full = lambda shape: pl.BlockSpec(shape, lambda bi, i: (0,) * len(shape))
    tok = pl.BlockSpec((1, tm, c), lambda bi, i: (bi, i, 0))
    tok2 = pl.BlockSpec((1, 2, tm, c), lambda bi, i: (bi, 0, i, 0))
    s1 = jax.ShapeDtypeStruct((b, lt, c), F32)
    s2 = jax.ShapeDtypeStruct((b, 2, lt, c), F32)
    return pl.pallas_call(
        functools.partial(_rwkv_prep_kernel, lc=lc, lt=lt, tm=tm),
        grid=(b, lt // tm),
        in_specs=_halo_specs(tm, cp, lt) + [full((2, cp)), full((LANES, 5 * c)), full((c, c)), full((2, c)),
                                            full((2, c)), full((1, c)), full((1, c)), full((1, c))],
        out_specs=[tok, tok, tok, tok, tok, tok2, tok2, tok2],
        out_shape=[s1, s1, s1, s1, s1, s2, s2, s2],
        compiler_params=_cparams(("parallel", "parallel")),
        name="rwkv_prep",
    )(ur, ur, ur, mu, wl, bd, w0, a0, kkw, ka, rk)


def _mm(a, b):
    return jnp.dot(a, b, precision=HI, preferred_element_type=F32)


_NN = ((1,), (0,))
_NT = ((1,), (1,))
_TN = ((0,), (0,))


def _split2(a):
    hi = a.astype(BF16)
    return hi, (a - hi.astype(F32)).astype(BF16)


def _dotp(a, b, passes, dims=_NN):
    if a.ndim == 3:
        dn = (((dims[0][0] + 1,), (dims[1][0] + 1,)), ((0,), (0,)))
    else:
        dn = (dims, ((), ()))
    dg = lambda p, q: lax.dot_general(p, q, dn, preferred_element_type=F32)
    if passes == 1:
        return dg(a.astype(BF16), b.astype(BF16))
    ah, al = _split2(a)
    bh, bl = _split2(b)
    return dg(ah, bh) + dg(ah, bl) + dg(al, bh)


def _dot01(a, ones):
    ah, al = _split2(a)
    o = ones.astype(BF16)
    return jnp.dot(ah, o, preferred_element_type=F32) + jnp.dot(al, o, preferred_element_type=F32)


P_M = 1
P_INV = 1
P_W = 1
P_Z = 1
P_STATE = 3
P_DFT = 1


def _unit_tri_inv(a_mat, row, col, eye, passes):
    t = a_mat.shape[-1]
    eye_f = eye.astype(F32)
    base = SUBLANES
    same = (row // base) == (col // base)
    n1 = -jnp.where(same, a_mat, 0.0)
    n2 = _dotp(n1, n1, passes)
    n4 = _dotp(n2, n2, passes)
    x = _dotp(_dotp(eye_f + n1, eye_f + n2, passes), eye_f + n4, passes)
    m = base
    while m < t:
        off = jnp.logical_and((row // (2 * m)) == (col // (2 * m)), (row // m) != (col // m))
        x = x - _dotp(x, _dotp(jnp.where(off, a_mat, 0.0), x, passes), passes)
        m *= 2
    return x


def _wkv_kernel(rf_ref, vf_ref, kkf_ref, rb_ref, vb_ref, kkb_ref, lwf_ref, bef_ref, kdf_ref,
                lwb_ref, beb_ref, kdb_ref, yf_ref, yb_ref, h_scr, pq_scr, ry_scr):
    t = WKV_CHUNK
    n = HEAD_DIM
    g = WKV_TILE // WKV_CHUNK
    tt = WKV_TILE
    nh = RWKV_HEADS
    nd = g * nh
    nu = 2 * nd
    orders = (list(range(g)), list(range(g - 1, -1, -1)))

    @pl.when(pl.program_id(1) == 0)
    def _():
        h_scr[...] = jnp.zeros(h_scr.shape, F32)
        pq_scr[...] = jnp.zeros(pq_scr.shape, F32)
        ry_scr[...] = jnp.zeros(ry_scr.shape, F32)

    hm = h_scr[...]
    for p in range(g):
        both = lambda ref: jnp.concatenate([ref[d * nd + p * nh:d * nd + (p + 1) * nh] for d in range(2)], 0)
        ry = both(ry_scr)
        pq = both(pq_scr)
        y = _dotp(ry[:, :, :n], hm, P_STATE) + ry[:, :, n:]
        hm = _dotp(pq[:, :, :n], hm, P_STATE) + pq[:, :, n:]
        for d, y_ref in enumerate((yf_ref, yb_ref)):
            c = orders[d][p]
            y_ref[0, c * t:(c + 1) * t, :] = jnp.concatenate([y[d * nh + h] for h in range(nh)], -1)
    h_scr[...] = hm

    row = lax.broadcasted_iota(jnp.int32, (tt, tt), 0)
    col = lax.broadcasted_iota(jnp.int32, (tt, tt), 1)
    same = (row // t) == (col // t)

    def scaled(d, r_ref, v_ref, kk_ref, lw_ref, be_ref, kd_ref):
        tri = jnp.logical_and(same, (row >= col) if d == 0 else (row <= col))
        sums = jnp.concatenate([jnp.where(tri, 1.0, 0.0), jnp.where(same, 1.0, 0.0)], 0).astype(BF16)
        lw = lw_ref[0, 0]
        l1 = lw.astype(BF16)
        rem = lw - l1.astype(F32)
        l2 = rem.astype(BF16)
        l3 = (rem - l2.astype(F32)).astype(BF16)
        cc = (jnp.dot(sums, l1, preferred_element_type=F32) + jnp.dot(sums, l2, preferred_element_type=F32)
              + jnp.dot(sums, l3, preferred_element_type=F32))
        cum = cc[:tt]
        ctot = cc[tt:]
        e_neg = jnp.exp(-cum)
        e_end = jnp.exp(ctot - cum)
        be = be_ref[0, 0]
        kd = kd_ref[0, 0]

        def units(x, rows=t):
            return jnp.stack([x[orders[d][p] * t:orders[d][p] * t + rows, h * n:(h + 1) * n]
                              for p in range(g) for h in range(nh)], 0)

        return dict(kap=units(kk_ref[0] * jnp.exp(cum - lw)), rt=units(r_ref[0] * jnp.exp(cum)), vh=units(v_ref[0]),
                    bet=units(be * e_neg), kdt=units(kd * e_neg), beh=units(be * e_end), kdh=units(kd * e_end),
                    gend=units(jnp.exp(ctot), 1))

    parts = (scaled(0, rf_ref, vf_ref, kkf_ref, lwf_ref, bef_ref, kdf_ref),
             scaled(1, rb_ref, vb_ref, kkb_ref, lwb_ref, beb_ref, kdb_ref))
    cat = lambda key: jnp.concatenate([parts[0][key], parts[1][key]], 0)
    kap, rt, vh = cat("kap"), cat("rt"), cat("vh")
    r64 = lax.broadcasted_iota(jnp.int32, (t, t), 0)
    c64 = lax.broadcasted_iota(jnp.int32, (t, t), 1)
    eye = r64 == c64
    unit = lax.broadcasted_iota(jnp.int32, (nu, t, t), 0)
    ahead = (r64 - c64)[None] * jnp.where(unit < nd, 1, -1)
    strict = ahead > 0
    incl = ahead >= 0
    m = _dotp(jnp.concatenate([kap, rt], 1), jnp.concatenate([cat("bet"), cat("kdt")], 1), P_M, _NT)
    a_mat = jnp.where(strict, m[:, :t, :t], 0.0)
    b_mat = jnp.where(strict, m[:, :t, t:], 0.0)
    ab_r = jnp.concatenate([jnp.where(incl, m[:, t:, :t], 0.0), jnp.where(incl, m[:, t:, t:], 0.0)], 2)
    tinv = _unit_tri_inv(a_mat, r64, c64, eye, P_INV)
    w = _dotp(tinv, jnp.concatenate([kap, _dotp(b_mat, vh, P_W)], 2), P_W)
    z = jnp.concatenate([-w, jnp.concatenate([jnp.zeros((nu, t, n), F32), vh], 2)], 1)
    ry = _dotp(ab_r, z, P_Z)
    ry_scr[:, :, :n] = ry[:, :, :n] + rt
    ry_scr[:, :, n:] = ry[:, :, n:]
    pq = _dotp(jnp.concatenate([cat("beh"), cat("kdh")], 1), z, P_Z, _TN)
    gd = jnp.where(eye, jnp.broadcast_to(cat("gend"), (nu, n, n)), 0.0)
    pq_scr[:, :, :n] = pq[:, :, :n] + gd
    pq_scr[:, :, n:] = pq[:, :, n:]


def _wkv_scan(r, v, kk, lw, be, kd, lc):
    b, lt, c = r.shape
    tt = WKV_TILE
    nt = lt // tt
    ntc = lc // tt
    tiles = (lambda i: i, lambda i: jnp.where(i < ntc, ntc - 1 - i, nt - 1 - (i - ntc)))
    t_in = lambda d: (lambda i: tiles[d](jnp.minimum(i, nt - 1)))
    t_out = lambda d: (lambda i: tiles[d](jnp.maximum(i - 1, 0)))
    one = lambda d: pl.BlockSpec((1, tt, c), lambda bi, i: (bi, t_in(d)(i), 0))
    two = lambda d: pl.BlockSpec((1, 1, tt, c), lambda bi, i: (bi, d, t_in(d)(i), 0))
    out = lambda d: pl.BlockSpec((1, tt, c), lambda bi, i: (bi, t_out(d)(i), 0))
    nu = 2 * (tt // WKV_CHUNK) * RWKV_HEADS
    ysh = jax.ShapeDtypeStruct((b, lt, c), F32)
    return pl.pallas_call(
        _wkv_kernel,
        grid=(b, nt + 1),
        in_specs=[one(0)] * 3 + [one(1)] * 3 + [two(0)] * 3 + [two(1)] * 3,
        out_specs=[out(0), out(1)],
        out_shape=[ysh, ysh],
        scratch_shapes=[pltpu.VMEM((2 * RWKV_HEADS, HEAD_DIM, HEAD_DIM), F32),
                        pltpu.VMEM((nu, HEAD_DIM, 2 * HEAD_DIM), F32),
                        pltpu.VMEM((nu, WKV_CHUNK, 2 * HEAD_DIM), F32)],
        compiler_params=_cparams(("parallel", "arbitrary")),
        name="wkv_scan",
    )(r, v, kk, r, v, kk, lw, be, kd, lw, be, kd)


def _rwkv_out_kernel(yf_ref, yb_ref, bon_ref, g_ref, bd_ref, gg_ref, gb_ref, o_ref):
    y = yf_ref[0] + yb_ref[0] + bon_ref[0]
    bd = bd_ref[...]
    mu = _dot01(y, bd) * (1.0 / HEAD_DIM)
    yc = y - mu
    var = _dot01(yc * yc, bd) * (1.0 / HEAD_DIM)
    yn = yc * lax.rsqrt(var + RWKV_GN_EPS) * gg_ref[...] + gb_ref[...]
    o_ref[0] = (yn * g_ref[0]).astype(o_ref.dtype)


def _rwkv_out(yf, yb, bon, g, bd, gg, gb):
    b, lt, c = yf.shape
    tm = TOK_TILE
    tok = pl.BlockSpec((1, tm, c), lambda bi, i: (bi, i, 0))
    full = lambda shape: pl.BlockSpec(shape, lambda bi, i: (0,) * len(shape))
    return pl.pallas_call(
        _rwkv_out_kernel,
        grid=(b, lt // tm),
        in_specs=[tok, tok, tok, tok, full((c, c)), full((1, c)), full((1, c))],
        out_specs=tok,
        out_shape=jax.ShapeDtypeStruct((b, lt, c), BF16),
        compiler_params=_cparams(("parallel", "parallel")),
        name="rwkv_out",
    )(yf, yb, bon, g, bd, gg, gb)


def _hy_prep_kernel(u_ref, up_ref, un_ref, w_ref, b_ref, x1_o, x2_o, v_o, *, lc, lt, tm):
    i = pl.program_id(1)
    u = u_ref[0]
    prev, nxt = _prev_next(u, up_ref[0], un_ref[0], i, tm, lc, lt)
    y = prev * w_ref[0:1] + u * w_ref[1:2] + nxt * w_ref[2:3] + b_ref[...]
    c = HY_WIDTH
    x1_o[0] = y[:, :c]
    x2_o[0] = y[:, c:2 * c]
    v_o[0] = y[:, 2 * c:]


def _hy_prep(uh, w, bias, lc):
    b, lt, cin = uh.shape
    tm = TOK_TILE
    c = HY_WIDTH
    full = lambda shape: pl.BlockSpec(shape, lambda bi, i: (0,) * len(shape))
    tok = pl.BlockSpec((1, tm, c), lambda bi, i: (bi, i, 0))
    s1 = jax.ShapeDtypeStruct((b, lt, c), F32)
    return pl.pallas_call(
        functools.partial(_hy_prep_kernel, lc=lc, lt=lt, tm=tm),
        grid=(b, lt // tm),
        in_specs=_halo_specs(tm, cin, lt) + [full((3, cin)), full((1, cin))],
        out_specs=[tok, tok, tok],
        out_shape=[s1, s1, s1],
        compiler_params=_cparams(("parallel", "parallel")),
        name="hy_prep",
    )(uh, uh, uh, w, bias)


def _hy_filter_kernel(z_ref, w1_ref, b1_ref, f1_ref, w2_ref, b2_ref, f2_ref, w3_ref, dec_ref,
                      fw_o, bw_o, nrm_o, *, tl):
    i = pl.program_id(0)
    z = z_ref[...]
    h = jnp.sin(f1_ref[...] * (_dotp(z, w1_ref[...], 3) + b1_ref[...]))
    h = jnp.sin(f2_ref[...] * (_dotp(h, w2_ref[...], 3) + b2_ref[...]))
    h = _dotp(h, w3_ref[...], 3) * jnp.exp(-z[:, 0:1] * dec_ref[...])
    c = HY_WIDTH
    fw = jnp.concatenate([h[:, 0:c], h[:, 2 * c:3 * c]], 1)
    bw = jnp.concatenate([h[:, c:2 * c], h[:, 3 * c:4 * c]], 1)
    rid = i * tl + lax.broadcasted_iota(jnp.int32, (tl, 1), 0)
    bw = jnp.where(rid == 0, 0.0, bw)
    fw_o[...] = fw
    bw_o[...] = bw

    @pl.when(i == 0)
    def _():
        nrm_o[...] = jnp.zeros(nrm_o.shape, F32)

    nrm_o[...] += jnp.sum(jnp.abs(fw) + jnp.abs(bw), 0, keepdims=True)


def _hy_filter(feat, w1p, b1, f1, w2, b2, f2, w3, dec):
    l, fe = feat.shape
    tl = _pick_tile(l, (512, 256))
    c2 = HY_ORDER * HY_WIDTH
    full = lambda shape: pl.BlockSpec(shape, lambda i: (0,) * len(shape))
    return pl.pallas_call(
        functools.partial(_hy_filter_kernel, tl=tl),
        grid=(l // tl,),
        in_specs=[pl.BlockSpec((tl, fe), lambda i: (i, 0)), full(w1p.shape), full(b1.shape), full(f1.shape),
                  full(w2.shape), full(b2.shape), full(f2.shape), full(w3.shape), full(dec.shape)],
        out_specs=[pl.BlockSpec((tl, c2), lambda i: (i, 0)), pl.BlockSpec((tl, c2), lambda i: (i, 0)),
                   pl.BlockSpec((1, c2), lambda i: (0, 0))],
        out_shape=[jax.ShapeDtypeStruct((l, c2), F32), jax.ShapeDtypeStruct((l, c2), F32),
                   jax.ShapeDtypeStruct((1, c2), F32)],
        compiler_params=_cparams(("arbitrary",)),
        name="hy_filter",
    )(feat, w1p, b1, f1, w2, b2, f2, w3, dec)


def _dft_cols_kernel(f_ref, xa_ref, xb_ref, o_ref, *, n1, pair):
    f = f_ref[...]
    pa = _dotp(f, xa_ref[0], P_DFT)
    pb = _dotp(f, xb_ref[0], P_DFT)
    if pair:
        o_ref[0] = pa[:n1].astype(o_ref.dtype)
        o_ref[1] = pa[n1:].astype(o_ref.dtype)
        o_ref[2] = pb[:n1].astype(o_ref.dtype)
        o_ref[3] = pb[n1:].astype(o_ref.dtype)
    else:
        o_ref[0] = (pa[:n1] - pb[n1:]).astype(o_ref.dtype)
        o_ref[1] = (pb[:n1] + pa[n1:]).astype(o_ref.dtype)


def _dft_cols(fstack, xa, ia, xb, ib, pair):
    n1 = fstack.shape[0] // 2
    _, nh, cols = xa.shape
    tc = _pick_tile(cols, (4096, 2048, 1024, 512, 256, 128))
    no = 4 if pair else 2
    return pl.pallas_call(
        functools.partial(_dft_cols_kernel, n1=n1, pair=pair),
        grid=(cols // tc,),
        in_specs=[pl.BlockSpec(fstack.shape, lambda j: (0, 0)),
                  pl.BlockSpec((1, nh, tc), lambda j: (ia, 0, j)),
                  pl.BlockSpec((1, nh, tc), lambda j: (ib, 0, j))],
        out_specs=pl.BlockSpec((no, n1, tc), lambda j: (0, 0, j)),
        out_shape=jax.ShapeDtypeStruct((no, n1, cols), BF16),
        compiler_params=_cparams(("parallel",)),
        name="dft_cols",
    )(fstack, xa, xb)


def _cplx_left(gs, zr, zi, n):
    c = zr.shape[1]
    p = _dotp(gs, jnp.concatenate([zr, zi], 1), P_DFT)
    return p[:n, :c] - p[n:, c:], p[:n, c:] + p[n:, :c]


def _spec_kernel(a_ref, g_ref, nrm_ref, o_ref, *, n_total):
    n2 = DFT_N2
    gs = jnp.concatenate([g_ref[0, 0], g_ref[0, 1]], 0)
    fr, fi = _cplx_left(gs, a_ref[0, 0], a_ref[1, 0], n2)
    br, bi = _cplx_left(gs, a_ref[2, 0], a_ref[3, 0], n2)
    s = 1.0 / (nrm_ref[...] * n_total)
    o_ref[0, 0] = ((fr + br) * s).astype(o_ref.dtype)
    o_ref[1, 0] = ((fi - bi) * s).astype(o_ref.dtype)


def _spec(a4, g, nrm, n_total):
    _, n1, n2, c2 = a4.shape
    return pl.pallas_call(
        functools.partial(_spec_kernel, n_total=float(n_total)),
        grid=(n1,),
        in_specs=[pl.BlockSpec((4, 1, n2, c2), lambda k: (0, k, 0, 0)),
                  pl.BlockSpec((1, 2, n2, n2), lambda k: (k, 0, 0, 0)),
                  pl.BlockSpec((1, c2), lambda k: (0, 0))],
        out_specs=pl.BlockSpec((2, 1, n2, c2), lambda k: (0, k, 0, 0)),
        out_shape=jax.ShapeDtypeStruct((2, n1, n2, c2), BF16),
        compiler_params=_cparams(("parallel",)),
        name="hy_spec",
    )(a4, g, nrm)


def _conv_mid_kernel(a_ref, g_ref, k_ref, o_ref):
    n2 = DFT_N2
    gs = jnp.concatenate([g_ref[0, 0], g_ref[0, 1]], 0)
    xr, xi = _cplx_left(gs, a_ref[0, 0], a_ref[1, 0], n2)
    c = xr.shape[1]
    kr = k_ref[0, 0].astype(F32)
    ki = k_ref[1, 0].astype(F32)
    zr = xr * kr - xi * ki
    zi = xr * ki + xi * kr
    zst = jnp.concatenate([jnp.concatenate([zr, zi], 1), jnp.concatenate([zi, -zr], 1)], 0)
    y = _dotp(gs, zst, P_DFT, _TN)
    o_ref[0, 0] = y[:, :c].astype(o_ref.dtype)
    o_ref[1, 0] = y[:, c:].astype(o_ref.dtype)


def _conv_mid(a, g, kspec, order):
    _, n1, n2, c = a.shape
    return pl.pallas_call(
        _conv_mid_kernel,
        grid=(n1,),
        in_specs=[pl.BlockSpec((2, 1, n2, c), lambda k: (0, k, 0, 0)),
                  pl.BlockSpec((1, 2, n2, n2), lambda k: (k, 0, 0, 0)),
                  pl.BlockSpec((2, 1, n2, c), lambda k: (0, k, 0, order))],
        out_specs=pl.BlockSpec((2, 1, n2, c), lambda k: (0, k, 0, 0)),
        out_shape=jax.ShapeDtypeStruct((2, n1, n2, c), BF16),
        compiler_params=_cparams(("parallel",)),
        name="hy_conv_mid",
    )(a, g, kspec)


def _idft_cols_kernel(c_ref, b_ref, g0_ref, g1_ref, x0_ref, x1_ref, bias_ref, o_ref, *, nh):
    cs = c_ref[...]
    pr = _dotp(cs, b_ref[0], P_DFT)
    pi = _dotp(cs, b_ref[1], P_DFT)
    yr = pr[:nh] - pi[nh:]
    yi = pi[:nh] + pr[nh:]
    bias = bias_ref[...]
    o_ref[0] = g0_ref[0] * (yr + x0_ref[0] * bias)
    o_ref[1] = g1_ref[0] * (yi + x1_ref[0] * bias)


def _idft_cols(cstack, bv, gate, xin, bias_cols):
    nh2, n1 = cstack.shape
    nh = nh2 // 2
    cols = bv.shape[-1]
    tc = _pick_tile(cols, (4096, 2048, 1024, 512, 256, 128))
    row = lambda bi: pl.BlockSpec((1, nh, tc), lambda j: (bi, 0, j))
    return pl.pallas_call(
        functools.partial(_idft_cols_kernel, nh=nh),
        grid=(cols // tc,),
        in_specs=[pl.BlockSpec((nh2, n1), lambda j: (0, 0)),
                  pl.BlockSpec((2, n1, tc), lambda j: (0, 0, j)),
                  row(0), row(1), row(0), row(1),
                  pl.BlockSpec((1, tc), lambda j: (0, j))],
        out_specs=pl.BlockSpec((2, nh, tc), lambda j: (0, 0, j)),
        out_shape=jax.ShapeDtypeStruct((2, nh, cols), F32),
        compiler_params=_cparams(("parallel",)),
        name="idft_cols",
    )(cstack, bv, gate, gate, xin, xin, bias_cols)


def _hy_ctx_kernel(x1_ref, x2_ref, v_ref, fw_ref, bw_ref, nrm_ref, bias_ref, ff_ref, ci_ref, o_ref, *, lc):
    n = 2 * lc
    c = HY_WIDTH
    ff = ff_ref[...]
    ci = ci_ref[...]
    pf = _mm(ff, fw_ref[...])
    pb = _mm(ff, bw_ref[...])
    s = 1.0 / (nrm_ref[...] * float(n))
    kr = (pf[:n] + pb[:n]) * s
    ki = (pf[n:] - pb[n:]) * s

    def conv(z0, z1, o):
        xr, xi = _cplx_left(ff, z0, z1, n)
        krr = kr[:, o * c:(o + 1) * c]
        kii = ki[:, o * c:(o + 1) * c]
        return _cplx_left(ci, xr * krr - xi * kii, xr * kii + xi * krr, lc)

    v0 = v_ref[0]
    v1 = v_ref[1]
    y0, y1 = conv(v0, v1, 0)
    z0 = x1_ref[0] * (y0 + v0 * bias_ref[0:1])
    z1 = x1_ref[1] * (y1 + v1 * bias_ref[0:1])
    y0, y1 = conv(z0, z1, 1)
    o_ref[0] = x2_ref[0] * (y0 + z0 * bias_ref[1:2])
    o_ref[1] = x2_ref[1] * (y1 + z1 * bias_ref[1:2])


def _hy_ctx(x1, x2, v, fw, bw, nrm, bias, ff, ci):
    b, lc, c = v.shape
    vm = pl.BlockSpec(memory_space=pltpu.VMEM)
    return pl.pallas_call(
        functools.partial(_hy_ctx_kernel, lc=lc),
        in_specs=[vm] * 9,
        out_specs=vm,
        out_shape=jax.ShapeDtypeStruct((b, lc, c), F32),
        compiler_params=pltpu.CompilerParams(vmem_limit_bytes=VMEM_LIMIT),
        name="hy_ctx",
    )(x1, x2, v, fw, bw, nrm, bias, ff, ci)


def _outproj_kernel(attc_ref, attl_ref, rw_ref, hyc_ref, hyl_ref, x_ref, mod_ref, w_ref, lg_ref, lb_ref, o_ref,
                    *, lc, tm):
    row0 = pl.program_id(1) * tm
    a0 = ATT_WIDTH
    a1 = ATT_WIDTH + RWKV_WIDTH
    is_ctx = row0 < lc
    att = jnp.where(is_ctx, attc_ref[0], attl_ref[0])
    hy = jnp.where(is_ctx, hyc_ref[0], hyl_ref[0])
    o = jnp.dot(att, w_ref[:a0], preferred_element_type=F32)
    o += jnp.dot(rw_ref[0], w_ref[a0:a1], preferred_element_type=F32)
    o += jnp.dot(hy.astype(BF16), w_ref[a1:], preferred_element_type=F32)
    g = _sel_mod(mod_ref, 2, row0, tm, lc)
    y = ALPHA * x_ref[0] + g * o
    o_ref[0] = _layer_norm(y) * lg_ref[...] + lb_ref[...]


def _outproj(att_c, att_l, rw, hy_c, hy_l, xx, mod, w, lg, lb, lc):
    b, lt, d = xx.shape
    tm = TOK_TILE
    nct = lc // tm
    tok = lambda c: pl.BlockSpec((1, tm, c), lambda bi, i: (bi, i, 0))
    ctx = lambda c: pl.BlockSpec((1, tm, c), lambda bi, i: (bi, jnp.minimum(i, nct - 1), 0))
    lat = lambda c: pl.BlockSpec((1, tm, c), lambda bi, i: (bi, jnp.maximum(i - nct, 0), 0))
    full = lambda shape: pl.BlockSpec(shape, lambda bi, i: (0,) * len(shape))
    return pl.pallas_call(
        functools.partial(_outproj_kernel, lc=lc, tm=tm),
        grid=(b, lt // tm),
        in_specs=[ctx(ATT_WIDTH), lat(ATT_WIDTH), tok(RWKV_WIDTH), ctx(HY_WIDTH), lat(HY_WIDTH), tok(d),
                  pl.BlockSpec((1, 2, 6, d), lambda bi, i: (bi, 0, 0, 0)),
                  full(w.shape), full((1, d)), full((1, d))],
        out_specs=tok(d),
        out_shape=jax.ShapeDtypeStruct((b, lt, d), F32),
        compiler_params=_cparams(("parallel", "parallel")),
        name="outproj",
    )(att_c, att_l, rw, hy_c, hy_l, xx, mod, w, lg, lb)


def _ffn_kernel(x_ref, mod_ref, w1_ref, w3_ref, w2_ref, lg_ref, lb_ref, o_ref, h_scr, acc_scr, *, lc, tm, nf):
    row0 = pl.program_id(1) * tm
    f = pl.program_id(2)

    @pl.when(f == 0)
    def _():
        sh = _sel_mod(mod_ref, 3, row0, tm, lc)
        sc = _sel_mod(mod_ref, 4, row0, tm, lc)
        h_scr[...] = (_layer_norm(x_ref[0]) * (1.0 + sc) + sh).astype(BF16)
        acc_scr[...] = jnp.zeros(acc_scr.shape, F32)

    h = h_scr[...]
    a = jnp.dot(h, w1_ref[...], preferred_element_type=F32)
    g = jnp.dot(h, w3_ref[...], preferred_element_type=F32)
    acc_scr[...] += jnp.dot((_silu(a) * g).astype(BF16), w2_ref[...], preferred_element_type=F32)

    @pl.when(f == nf - 1)
    def _():
        gate = _sel_mod(mod_ref, 5, row0, tm, lc)
        y = ALPHA * x_ref[0] + gate * acc_scr[...]
        o_ref[0] = _layer_norm(y) * lg_ref[...] + lb_ref[...]


def _ffn(xx, mod, w1, w3, w2, lg, lb, lc):
    b, lt, d = xx.shape
    ff = w1.shape[1]
    tm = _pick_tile(lt, (1280, 768, 512, 256))
    tf = _pick_tile(ff, (256, 128))
    nf = ff // tf
    return pl.pallas_call(
        functools.partial(_ffn_kernel, lc=lc, tm=tm, nf=nf),
        grid=(b, lt // tm, nf),
        in_specs=[pl.BlockSpec((1, tm, d), lambda bi, i, f: (bi, i, 0)),
                  pl.BlockSpec((1, 2, 6, d), lambda bi, i, f: (bi, 0, 0, 0)),
                  pl.BlockSpec((d, tf), lambda bi, i, f: (0, f)),
                  pl.BlockSpec((d, tf), lambda bi, i, f: (0, f)),
                  pl.BlockSpec((tf, d), lambda bi, i, f: (f, 0)),
                  pl.BlockSpec((1, d), lambda bi, i, f: (0, 0)),
                  pl.BlockSpec((1, d), lambda bi, i, f: (0, 0))],
        out_specs=pl.BlockSpec((1, tm, d), lambda bi, i, f: (bi, i, 0)),
        out_shape=jax.ShapeDtypeStruct((b, lt, d), F32),
        scratch_shapes=[pltpu.VMEM((tm, d), BF16), pltpu.VMEM((tm, d), F32)],
        compiler_params=_cparams(("parallel", "parallel", "arbitrary")),
        name="ffn",
    )(xx, mod, w1, w3, w2, lg, lb)


def _route(x_ref, mod_ref, wr_ref, row0, tm, lc):
    sh = _sel_mod(mod_ref, 3, row0, tm, lc)
    sc = _sel_mod(mod_ref, 4, row0, tm, lc)
    h = _layer_norm(x_ref[0]) * (1.0 + sc) + sh
    logits = _dotp(h, wr_ref[...], 3)
    lane = lax.broadcasted_iota(jnp.int32, logits.shape, 1)
    neg = jnp.float32(-jnp.inf)
    lg = jnp.where(lane < N_EXPERTS, logits, neg)
    m1 = jnp.max(lg, -1, keepdims=True)
    i1 = jnp.min(jnp.where(lg == m1, lane, LANES), -1, keepdims=True)
    lg2 = jnp.where(lane == i1, neg, lg)
    m2 = jnp.max(lg2, -1, keepdims=True)
    i2 = jnp.min(jnp.where(lg2 == m2, lane, LANES), -1, keepdims=True)
    e2 = jnp.exp(m2 - m1)
    return h, lane, i1, i2, 1.0 / (1.0 + e2), e2 / (1.0 + e2)


def _moe_sparse_kernel(x_ref, mod_ref, wr_ref, w1_ref, w3_ref, w2_ref, lg_ref, lb_ref, o_ref,
                       h_scr, acc_scr, gate_scr, posc_scr, posr_scr, xs_scr, ye_scr, nblk_scr, *, lc, tm, nf, ns, blk):
    row0 = pl.program_id(1) * tm
    s = pl.program_id(2)
    e = s // nf
    f = s % nf

    @pl.when(s == 0)
    def _():
        h, lane, i1, i2, g1, g2 = _route(x_ref, mod_ref, wr_ref, row0, tm, lc)
        h_scr[...] = h.astype(BF16)
        acc_scr[...] = jnp.zeros(acc_scr.shape, F32)
        routed = jnp.where(jnp.logical_or(lane == i1, lane == i2), 1.0, 0.0)
        r = lax.broadcasted_iota(jnp.int32, (tm, tm), 0)
        c = lax.broadcasted_iota(jnp.int32, (tm, tm), 1)
        before = jnp.where(c < r, 1.0, 0.0).astype(BF16)
        rank_c = jnp.dot(before, routed.astype(BF16), preferred_element_type=F32)
        gate_scr[...] = jnp.where(lane == i1, g1, 0.0) + jnp.where(lane == i2, g2, 0.0)
        posc_scr[...] = jnp.where(routed > 0.0, rank_c, -1.0)
        routed_t = routed.T[:2 * SUBLANES]
        after = jnp.where(r < c, 1.0, 0.0).astype(BF16)
        rank_r = jnp.dot(routed_t.astype(BF16), after, preferred_element_type=F32)
        posr_scr[...] = jnp.where(routed_t > 0.0, rank_r, -1.0)
        counts = jnp.sum(routed, 0, keepdims=True)
        lane1 = lax.broadcasted_iota(jnp.int32, counts.shape, 1)
        for ex in range(N_EXPERTS):
            n_rows = jnp.sum(jnp.where(lane1 == ex, counts, 0.0)).astype(jnp.int32)
            nblk_scr[ex] = (n_rows + (blk - 1)) // blk

    nblk = nblk_scr[e]
    sizes = [(k + 1) * blk for k in range(tm // blk)]

    def per_size(cond, fn):
        for k, m in enumerate(sizes):
            pl.when(jnp.logical_and(cond, nblk == k + 1))(functools.partial(fn, m))

    def gather(m):
        slot = lax.broadcasted_iota(jnp.int32, (m, tm), 0).astype(F32)
        take = jnp.where(slot == posr_scr[pl.ds(e, 1), :], 1.0, 0.0).astype(BF16)
        xs_scr[:m] = jnp.dot(take, h_scr[...], preferred_element_type=F32).astype(BF16)
        ye_scr[:m] = jnp.zeros((m, ye_scr.shape[1]), F32)

    def expert(m):
        xs = xs_scr[:m]
        a = jnp.dot(xs, w1_ref[0], preferred_element_type=F32)
        g = jnp.dot(xs, w3_ref[0], preferred_element_type=F32)
        ye_scr[:m] += jnp.dot((_silu(a) * g).astype(BF16), w2_ref[0], preferred_element_type=F32)

    def scatter(m):
        lane = lax.broadcasted_iota(jnp.int32, (tm, LANES), 1)
        column = lambda ref: jnp.sum(jnp.where(lane == e, ref[...], 0.0), -1, keepdims=True)
        slot = lax.broadcasted_iota(jnp.int32, (tm, m), 1).astype(F32)
        put = jnp.where(slot == column(posc_scr), 1.0, 0.0).astype(BF16)
        y = jnp.dot(put, ye_scr[:m].astype(BF16), preferred_element_type=F32)
        acc_scr[...] += column(gate_scr) * y

    per_size(f == 0, gather)
    per_size(True, expert)
    per_size(f == nf - 1, scatter)

    @pl.when(s == ns - 1)
    def _():
        gate = _sel_mod(mod_ref, 5, row0, tm, lc)
        y = ALPHA * x_ref[0] + gate * acc_scr[...]
        o_ref[0] = _layer_norm(y) * lg_ref[...] + lb_ref[...]


def _moe_sparse(xx, mod, wr, w1, w3, w2, lg, lb, lc):
    b, lt, d = xx.shape
    ne, _, ff = w1.shape
    tm = _pick_tile(lt, (1280, 768, 512, 256))
    tf = _pick_tile(ff, (256, 128))
    nf = ff // tf
    ns = ne * nf
    return pl.pallas_call(
        functools.partial(_moe_sparse_kernel, lc=lc, tm=tm, nf=nf, ns=ns, blk=MOE_ROW_BLOCK),
        grid=(b, lt // tm, ns),
        in_specs=[pl.BlockSpec((1, tm, d), lambda bi, i, s: (bi, i, 0)),
                  pl.BlockSpec((1, 2, 6, d), lambda bi, i, s: (bi, 0, 0, 0)),
                  pl.BlockSpec((d, LANES), lambda bi, i, s: (0, 0)),
                  pl.BlockSpec((1, d, tf), lambda bi, i, s: (s // nf, 0, s % nf)),
                  pl.BlockSpec((1, d, tf), lambda bi, i, s: (s // nf, 0, s % nf)),
                  pl.BlockSpec((1, tf, d), lambda bi, i, s: (s // nf, s % nf, 0)),
                  pl.BlockSpec((1, d), lambda bi, i, s: (0, 0)),
                  pl.BlockSpec((1, d), lambda bi, i, s: (0, 0))],
        out_specs=pl.BlockSpec((1, tm, d), lambda bi, i, s: (bi, i, 0)),
        out_shape=jax.ShapeDtypeStruct((b, lt, d), F32),
        scratch_shapes=[pltpu.VMEM((tm, d), BF16), pltpu.VMEM((tm, d), F32),
                        pltpu.VMEM((tm, LANES), F32), pltpu.VMEM((tm, LANES), F32),
                        pltpu.VMEM((2 * SUBLANES, tm), F32),
                        pltpu.VMEM((tm, d), BF16), pltpu.VMEM((tm, d), F32),
                        pltpu.SMEM((ne,), jnp.int32)],
        compiler_params=_cparams(("parallel", "parallel", "arbitrary"), VMEM_LIMIT_MOE),
        name="moe_sparse",
    )(xx, mod, wr, w1, w3, w2, lg, lb)


def _rope_tables(l, lc):
    rows = l // GRID_W
    row = jnp.repeat(jnp.arange(rows, dtype=F32), GRID_W)
    col = jnp.tile(jnp.arange(GRID_W, dtype=F32), rows)
    n_freq = HEAD_DIM // 4
    inv_freq = ROPE_THETA ** (-jnp.arange(n_freq, dtype=F32) / n_freq)
    ang = jnp.concatenate([row[:, None] * inv_freq, col[:, None] * inv_freq], -1)
    cos, sin = jnp.cos(ang), jnp.sin(ang)
    cos2 = jnp.concatenate([jnp.ones((lc, LANES), F32), jnp.concatenate([cos, cos, cos, cos], -1)], 0)
    sin2 = jnp.concatenate([jnp.zeros((lc, LANES), F32), jnp.concatenate([-sin, sin, -sin, sin], -1)], 0)
    return cos2, sin2


def _hy_features(l):
    bands = (HY_EMB - 1) // 2
    t = jnp.linspace(0.0, 1.0, l, dtype=F32)[:, None]
    f = jnp.linspace(1e-4, bands - 1, bands, dtype=F32)[None, :]
    wt = 2.0 * math.pi * jnp.arange(l, dtype=F32)[:, None] / l
    z = jnp.concatenate([t, jnp.cos(f * wt), -jnp.sin(f * wt)], -1)
    return jnp.pad(z, ((0, 0), (0, LANES - HY_EMB)))


def _angle(idx, n):
    return (2.0 * math.pi / n) * (idx % n).astype(F32)


def _dft_tables(n1):
    nh = n1 // 2
    n2 = DFT_N2
    n = n1 * n2
    k1 = jnp.arange(n1, dtype=jnp.int32)
    a1 = _angle(k1[:, None] * jnp.arange(nh, dtype=jnp.int32)[None, :], n1)
    fstack = jnp.concatenate([jnp.cos(a1), -jnp.sin(a1)], 0)
    cstack = jnp.concatenate([jnp.cos(a1.T), jnp.sin(a1.T)], 0)
    k2 = jnp.arange(n2, dtype=jnp.int32)
    at = _angle(k1[:, None] * k2[None, :], n)
    tr, ti = jnp.cos(at), -jnp.sin(at)
    a2 = _angle(k2[:, None] * k2[None, :], n2)
    fr, fi = jnp.cos(a2), -jnp.sin(a2)
    g = jnp.stack([tr[:, None, :] * fr[None] - ti[:, None, :] * fi[None],
                   tr[:, None, :] * fi[None] + ti[:, None, :] * fr[None]], 1)
    return fstack, cstack, g


def _dense_dft_tables(lc):
    n = 2 * lc
    a = _angle(jnp.arange(n, dtype=jnp.int32)[:, None] * jnp.arange(lc, dtype=jnp.int32)[None, :], n)
    ff = jnp.concatenate([jnp.cos(a), -jnp.sin(a)], 0)
    ci = jnp.concatenate([jnp.cos(a.T), jnp.sin(a.T)], 0)
    return ff, ci


def kernel(x, c, ctx, c_ctx, ada_w, ada_b, w_in, w_out, q_gain, k_gain, rwkv_mu, rwkv_w0, rwkv_wB, rwkv_a0, rwkv_aB, rwkv_gB, rwkv_kk, rwkv_ka, rwkv_rk, rwkv_gn_g, rwkv_gn_b, hy_short_w, hy_short_b, hy_w1, hy_b1, hy_freq1, hy_w2, hy_b2, hy_freq2, hy_w3, hy_decay, hy_bias, ln1_g, ln1_b, ln2_g, ln2_b, ffn_w1, ffn_w3, ffn_w2, moe_router, moe_w1, moe_w3, moe_w2):
    b, l, d = x.shape
    lc = ctx.shape[1]
    lt = lc + l
    depth = ada_w.shape[0]
    assert b == 2, "the long convolution packs the two batch rows as one complex signal"
    assert d == D_MODEL and lc % TOK_TILE == 0 and l % TOK_TILE == 0 and (2 * l) % (2 * DFT_N2) == 0
    cw = RWKV_WIDTH

    xx = jnp.concatenate([ctx, x], 1)
    cond8 = jnp.zeros((SUBLANES, d), F32).at[:b].set(c).at[b].set(c_ctx)
    mod_all = _ada_mod(cond8, ada_w, ada_b)

    cos64, sin64 = _rope_tables(l, lc)
    n1 = 2 * l // DFT_N2
    nh = n1 // 2
    cols = DFT_N2 * HY_WIDTH
    fstack, cstack, g_tab = (t.astype(BF16) for t in _dft_tables(n1))
    ff_c, ci_c = _dense_dft_tables(lc)
    feat_l = _hy_features(l)
    feat_c = _hy_features(lc)
    blk = jnp.arange(cw) // HEAD_DIM
    bd = (blk[:, None] == blk[None, :]).astype(F32)
    ch = jnp.arange(ATT_WIDTH)
    bd_att = (ch[:, None] // HEAD_DIM == ch[None, :] // HEAD_DIM).astype(BF16)
    swap_att = (ch[:, None] == (ch[None, :] + HALF_HD) % HEAD_DIM + (ch[None, :] // HEAD_DIM) * HEAD_DIM
                ).astype(BF16)
    perm64 = jnp.concatenate([jnp.arange(0, HEAD_DIM, 2), jnp.arange(1, HEAD_DIM, 2)])
    perm_att = jnp.concatenate([h * HEAD_DIM + perm64 for h in range(ATT_HEADS + ATT_KV_HEADS)]
                               + [jnp.arange(ATT_WIDTH + ATT_KV_WIDTH, IN_ATT)])
    tq = TOK_TILE
    qn = next(n for n in (4, 2, 1) if (l // tq) % n == 0)
    tk = _pick_tile(lt, (3328, 1280, 1024, 768, 512, 256))

    for li in range(depth):
        ml = mod_all[li]
        mod = jnp.stack([jnp.broadcast_to(ml[b].reshape(1, 6, d), (b, 6, d)), ml[:b].reshape(b, 6, d)], 1)
        wi = w_in[li]
        w_pad = jnp.concatenate([wi[:, :IN_ATT][:, perm_att], wi[:, IN_ATT:IN_ATT + IN_RWKV],
                                 jnp.zeros((d, IN_RWKV_PAD - IN_RWKV), F32), wi[:, IN_ATT + IN_RWKV:]],
                                1).astype(BF16)
        ua, ur, uh = _inproj(xx, mod, w_pad, lc)

        two = lambda gain: jnp.tile(gain[perm64], 2)[None]
        qt, kx, vt = _attn_prep(ua, cos64, sin64, two(q_gain[li]), two(k_gain[li]), bd_att, swap_att, lc)
        att_c = _flash(qt, kx, vt, tq, TOK_TILE, 1, l // tq, lc // tq, lc // TOK_TILE)
        s_bound = (HEAD_DIM ** 0.5) * LOG2E * jnp.max(jnp.abs(q_gain[li])) * jnp.max(jnp.abs(k_gain[li]))
        lat_args = (qt, kx, vt, tq, tk, qn, 0, l // (qn * tq), lt // tk)
        att_l = lax.cond(s_bound <= MAX_UNSHIFTED_SCORE,
                         lambda: _flash(*lat_args, bounded=True), lambda: _flash(*lat_args, bounded=False))

        wl = jnp.zeros((LANES, 5 * cw), F32)
        wl = wl.at[0:W_LORA, 0:cw].set(rwkv_wB[li, 0]).at[W_LORA:2 * W_LORA, cw:2 * cw].set(rwkv_wB[li, 1])
        o_a = 2 * W_LORA
        wl = wl.at[o_a:o_a + A_LORA, 2 * cw:3 * cw].set(rwkv_aB[li, 0])
        wl = wl.at[o_a + A_LORA:o_a + 2 * A_LORA, 3 * cw:4 * cw].set(rwkv_aB[li, 1])
        o_g = o_a + 2 * A_LORA
        wl = wl.at[o_g:o_g + G_LORA, 4 * cw:5 * cw].set(rwkv_gB[li])
        mu = jnp.pad(rwkv_mu[li], ((0, 0), (0, IN_RWKV_PAD - IN_RWKV)))
        r_, v_, kk_, g_, bon_, lw_, be_, kd_ = _rwkv_prep(
            ur, mu, wl, bd, rwkv_w0[li], rwkv_a0[li], rwkv_kk[li][None], rwkv_ka[li][None],
            rwkv_rk[li].reshape(1, cw), lc)
        yf, yb = _wkv_scan(r_, v_, kk_, lw_, be_, kd_, lc)
        rw = _rwkv_out(yf, yb, bon_, g_, bd, rwkv_gn_g[li][None], rwkv_gn_b[li][None])

        x1, x2, vv = _hy_prep(uh, hy_short_w[li], hy_short_b[li][None], lc)
        w1p = jnp.pad(hy_w1[li], ((0, LANES - HY_EMB), (0, 0)))
        fargs = (w1p, hy_b1[li][None], hy_freq1[li][None], hy_w2[li], hy_b2[li][None], hy_freq2[li][None],
                 hy_w3[li], hy_decay[li][None])
        fw, bw, nrm = _hy_filter(feat_l, *fargs)
        c2 = HY_ORDER * HY_WIDTH
        a4 = _dft_cols(fstack, fw.reshape(1, nh, DFT_N2 * c2), 0, bw.reshape(1, nh, DFT_N2 * c2), 0, True)
        kspec = _spec(a4.reshape(4, n1, DFT_N2, c2), g_tab, nrm, n1 * DFT_N2)
        lat = lambda t: t[:, lc:].reshape(b, nh, cols)
        x1l, x2l, zin = lat(x1), lat(x2), lat(vv)
        for o, gate in enumerate((x1l, x2l)):
            a = _dft_cols(fstack, zin, 0, zin, 1, False)
            bv = _conv_mid(a.reshape(2, n1, DFT_N2, HY_WIDTH), g_tab, kspec, o)
            bias_cols = jnp.tile(hy_bias[li, o], DFT_N2)[None]
            zin = _idft_cols(cstack, bv.reshape(2, n1, cols), gate, zin, bias_cols)
        hy_l = zin.reshape(b, l, HY_WIDTH)
        fw_c, bw_c, nrm_c = _hy_filter(feat_c, *fargs)
        hy_c = _hy_ctx(x1[:, :lc], x2[:, :lc], vv[:, :lc], fw_c, bw_c, nrm_c, hy_bias[li], ff_c, ci_c)

        xx = _outproj(att_c, att_l, rw, hy_c, hy_l, xx, mod, w_out[li].astype(BF16),
                      ln1_g[li][None], ln1_b[li][None], lc)

        j = li // 2
        if li % 2 == 0:
            xx = _ffn(xx, mod, ffn_w1[j].astype(BF16), ffn_w3[j].astype(BF16), ffn_w2[j].astype(BF16),
                      ln2_g[li][None], ln2_b[li][None], lc)
        else:
            wr = jnp.pad(moe_router[j], ((0, 0), (0, LANES - N_EXPERTS)))
            xx = _moe_sparse(xx, mod, wr, moe_w1[j].astype(BF16), moe_w3[j].astype(BF16), moe_w2[j].astype(BF16),
                             ln2_g[li][None], ln2_b[li][None], lc)
    return xx[:, lc:]
```

```python
import functools
import math

import jax
import jax.numpy as jnp
from jax import lax
from jax.experimental import pallas as pl
from jax.experimental.pallas import tpu as pltpu

F32 = jnp.float32
BF16 = jnp.bfloat16
HI = lax.Precision.HIGHEST

D_MODEL = 1024
DEPTH = 2
GRID_W = 64
HEAD_DIM = 64
HALF_HD = HEAD_DIM // 2
ATT_WIDTH = 512
RWKV_WIDTH = 256
HY_WIDTH = 256
ATT_HEADS = 8
ATT_KV_HEADS = 2
ATT_REP = 4
ATT_KV_WIDTH = 128
ROPE_THETA = 10000.0
QK_EPS = 1e-6
RWKV_HEADS = 4
W_LORA = 16
A_LORA = 16
G_LORA = 32
RWKV_GN_EPS = 64e-5
HY_ORDER = 2
HY_EMB = 33
HY_FFN = 64
N_EXPERTS = 8
LN_EPS = 1e-6
IN_ATT = ATT_WIDTH + 2 * ATT_KV_WIDTH
IN_RWKV = 3 * RWKV_WIDTH + 2 * W_LORA + 2 * A_LORA + G_LORA
IN_RWKV_PAD = 896
IN_HY = 3 * HY_WIDTH
ALPHA = float((2 * DEPTH) ** 0.25)
LOG2E = 1.4426950408889634
MAX_UNSHIFTED_SCORE = 40.0

LANES = 128
SUBLANES = 8
TOK_TILE = 256
WKV_CHUNK = 64
WKV_TILE = 256
DFT_N2 = 256
MOE_ROW_BLOCK = 128
VMEM_LIMIT = 48 * 1024 * 1024
VMEM_LIMIT_MOE = 56 * 1024 * 1024


def _cparams(sem, vmem=VMEM_LIMIT):
    return pltpu.CompilerParams(dimension_semantics=sem, vmem_limit_bytes=vmem)


def _pick_tile(n, cands):
    for c in cands:
        if n % c == 0:
            return c
    raise ValueError(f"no tile for {n} in {cands}")


def _layer_norm(x):
    mu = jnp.mean(x, -1, keepdims=True)
    xc = x - mu
    var = jnp.mean(xc * xc, -1, keepdims=True)
    return xc * lax.rsqrt(var + LN_EPS)


def _sel_mod(mod_ref, j, row0, tm, lc):
    rid = row0 + lax.broadcasted_iota(jnp.int32, (tm, 1), 0)
    return jnp.where(rid < lc, mod_ref[0, 0, j:j + 1, :], mod_ref[0, 1, j:j + 1, :])


def _silu(x):
    return x * jax.nn.sigmoid(x)


def _ada_kernel(c_ref, w_ref, b_ref, o_ref):
    s = _silu(c_ref[...])
    o_ref[0] = jnp.dot(s, w_ref[0], precision=HI, preferred_element_type=F32) + b_ref[0]


def _ada_mod(cond8, ada_w, ada_b):
    depth, d, n = ada_w.shape
    tn = _pick_tile(n, (1536, 1024, 512, 256, 128))
    return pl.pallas_call(
        _ada_kernel,
        grid=(depth, n // tn),
        in_specs=[pl.BlockSpec((SUBLANES, d), lambda l, j: (0, 0)),
                  pl.BlockSpec((1, d, tn), lambda l, j: (l, 0, j)),
                  pl.BlockSpec((1, 1, tn), lambda l, j: (l, 0, j))],
        out_specs=pl.BlockSpec((1, SUBLANES, tn), lambda l, j: (l, 0, j)),
        out_shape=jax.ShapeDtypeStruct((depth, SUBLANES, n), F32),
        compiler_params=_cparams(("parallel", "parallel")),
        name="ada_mod",
    )(cond8, ada_w, ada_b.reshape(depth, 1, n))


def _inproj_kernel(x_ref, mod_ref, w_ref, oa_ref, or_ref, oh_ref, *, lc, tm):
    row0 = pl.program_id(1) * tm
    sh = _sel_mod(mod_ref, 0, row0, tm, lc)
    sc = _sel_mod(mod_ref, 1, row0, tm, lc)
    h = (_layer_norm(x_ref[0]) * (1.0 + sc) + sh).astype(BF16)
    u = jnp.dot(h, w_ref[...], preferred_element_type=F32)
    oa_ref[0] = u[:, :IN_ATT]
    or_ref[0] = u[:, IN_ATT:IN_ATT + IN_RWKV_PAD]
    oh_ref[0] = u[:, IN_ATT + IN_RWKV_PAD:]


def _inproj(xx, mod, w_pad, lc):
    b, lt, d = xx.shape
    tm = TOK_TILE
    n = w_pad.shape[1]
    return pl.pallas_call(
        functools.partial(_inproj_kernel, lc=lc, tm=tm),
        grid=(b, lt // tm),
        in_specs=[pl.BlockSpec((1, tm, d), lambda bi, i: (bi, i, 0)),
                  pl.BlockSpec((1, 2, 6, d), lambda bi, i: (bi, 0, 0, 0)),
                  pl.BlockSpec((d, n), lambda bi, i: (0, 0))],
        out_specs=[pl.BlockSpec((1, tm, IN_ATT), lambda bi, i: (bi, i, 0)),
                   pl.BlockSpec((1, tm, IN_RWKV_PAD), lambda bi, i: (bi, i, 0)),
                   pl.BlockSpec((1, tm, IN_HY), lambda bi, i: (bi, i, 0))],
        out_shape=[jax.ShapeDtypeStruct((b, lt, IN_ATT), F32),
                   jax.ShapeDtypeStruct((b, lt, IN_RWKV_PAD), F32),
                   jax.ShapeDtypeStruct((b, lt, IN_HY), F32)],
        compiler_params=_cparams(("parallel", "parallel")),
        name="inproj",
    )(xx, mod, w_pad)


def _attn_prep_kernel(u_ref, cos_ref, sin_ref, qg_ref, kg_ref, bd_ref, sw_ref, qt_ref, k_ref, vt_ref):
    u = u_ref[0]

    def norm_rope(x, g):
        w = x.shape[1]
        tile = lambda t: jnp.concatenate([t] * (w // LANES), -1)
        ms = _dot01(x * x, bd_ref[:w, :w]) * (1.0 / HEAD_DIM)
        xn = x * lax.rsqrt(ms + QK_EPS) * tile(g)
        sw = _dot01(xn, sw_ref[:w, :w])
        return xn * tile(cos_ref[...]) + sw * tile(sin_ref[...])

    q = norm_rope(u[:, :ATT_WIDTH], qg_ref[...]) * (LOG2E * HEAD_DIM ** -0.5)
    qt = q.T
    kx = norm_rope(u[:, ATT_WIDTH:ATT_WIDTH + ATT_KV_WIDTH], kg_ref[...])
    for g in range(ATT_KV_HEADS):
        base = g * ATT_REP * HEAD_DIM
        qt_ref[0, g, 0] = jnp.concatenate(
            [qt[base + r * HEAD_DIM:base + (r + 1) * HEAD_DIM] for r in range(ATT_REP)], -1).astype(BF16)
        k_ref[0, g] = kx[:, g * HEAD_DIM:(g + 1) * HEAD_DIM].astype(BF16)
    v0 = ATT_WIDTH + ATT_KV_WIDTH
    vt = u[:, v0:v0 + ATT_KV_WIDTH].T
    for g in range(ATT_KV_HEADS):
        vt_ref[0, g] = vt[g * HEAD_DIM:(g + 1) * HEAD_DIM].astype(BF16)


def _attn_prep(ua, cos, sin, qg, kg, bd_att, swap_att, lc):
    b, lt, _ = ua.shape
    tm = TOK_TILE
    nct = lc // tm
    nl = lt // tm - nct
    q_pos = lambda i: jnp.where(i < nct, nl + i, i - nct)
    return pl.pallas_call(
        _attn_prep_kernel,
        grid=(b, lt // tm),
        in_specs=[pl.BlockSpec((1, tm, IN_ATT), lambda bi, i: (bi, i, 0)),
                  pl.BlockSpec((tm, LANES), lambda bi, i: (i, 0)),
                  pl.BlockSpec((tm, LANES), lambda bi, i: (i, 0)),
                  pl.BlockSpec((1, LANES), lambda bi, i: (0, 0)),
                  pl.BlockSpec((1, LANES), lambda bi, i: (0, 0)),
                  pl.BlockSpec((ATT_WIDTH, ATT_WIDTH), lambda bi, i: (0, 0)),
                  pl.BlockSpec((ATT_WIDTH, ATT_WIDTH), lambda bi, i: (0, 0))],
        out_specs=[pl.BlockSpec((1, ATT_KV_HEADS, 1, HEAD_DIM, ATT_REP * tm),
                                lambda bi, i: (bi, 0, q_pos(i), 0, 0)),
                   pl.BlockSpec((1, ATT_KV_HEADS, tm, HEAD_DIM), lambda bi, i: (bi, 0, i, 0)),
                   pl.BlockSpec((1, ATT_KV_HEADS, HEAD_DIM, tm), lambda bi, i: (bi, 0, 0, i))],
        out_shape=[jax.ShapeDtypeStruct((b, ATT_KV_HEADS, lt // tm, HEAD_DIM, ATT_REP * tm), BF16),
                   jax.ShapeDtypeStruct((b, ATT_KV_HEADS, lt, HEAD_DIM), BF16),
                   jax.ShapeDtypeStruct((b, ATT_KV_HEADS, HEAD_DIM, lt), BF16)],
        compiler_params=_cparams(("parallel", "parallel")),
        name="attn_prep",
    )(ua, cos, sin, qg, kg, bd_att, swap_att)


def _flash_kernel(qt_ref, k_ref, vt_ref, o_ref, m_scr, l_scr, acc_scr, *, nk, tq, sub, qn):
    j = pl.program_id(3)

    @pl.when(j == 0)
    def _():
        m_scr[...] = jnp.full(m_scr.shape, -jnp.inf, F32)
        l_scr[...] = jnp.zeros(l_scr.shape, F32)
        acc_scr[...] = jnp.zeros(acc_scr.shape, F32)

    qt = jnp.concatenate([qt_ref[0, 0, t] for t in range(qn)], -1)
    nsub = k_ref.shape[2] // sub
    m = m_scr[...]
    l = l_scr[...]
    acc = acc_scr[...]
    scores = lambda c: jnp.dot(k_ref[0, 0, c * sub:(c + 1) * sub, :], qt, preferred_element_type=F32)
    pv = lambda c, p: jnp.dot(vt_ref[0, 0, :, c * sub:(c + 1) * sub], p, preferred_element_type=F32)
    s_next = scores(0)
    pend = None
    for c in range(nsub):
        s = s_next
        if c + 1 < nsub:
            s_next = scores(c + 1)
        if pend is not None:
            acc = pend[0] * acc + pv(c - 1, pend[1])
        m_new = jnp.maximum(m, jnp.max(s, 0, keepdims=True))
        a = jnp.exp2(m - m_new)
        p = jnp.exp2(s - m_new)
        l = a * l + jnp.sum(p, 0, keepdims=True)
        pend = (a, p.astype(BF16))
        m = m_new
    acc = pend[0] * acc + pv(nsub - 1, pend[1])
    m_scr[...] = m
    l_scr[...] = l
    acc_scr[...] = acc

    @pl.when(j == nk - 1)
    def _():
        o = (acc / l).T
        for t in range(qn):
            o_ref[0, t * tq:(t + 1) * tq, :] = jnp.concatenate(
                [o[(t * ATT_REP + r) * tq:(t * ATT_REP + r + 1) * tq] for r in range(ATT_REP)], -1
            ).astype(o_ref.dtype)


def _flash_bounded_kernel(qt_ref, k_ref, vt_ref, o_ref, l_scr, acc_scr, *, nk, tq, sub, qn):
    j = pl.program_id(3)

    @pl.when(j == 0)
    def _():
        l_scr[...] = jnp.zeros(l_scr.shape, F32)
        acc_scr[...] = jnp.zeros(acc_scr.shape, F32)

    qt = jnp.concatenate([qt_ref[0, 0, t] for t in range(qn)], -1)
    nsub = k_ref.shape[2] // sub

    l = l_scr[...]
    acc = acc_scr[...]
    scores = lambda c: jnp.dot(k_ref[0, 0, c * sub:(c + 1) * sub, :], qt, preferred_element_type=F32)
    s_next = scores(0)
    for c in range(nsub):
        s = s_next
        if c + 1 < nsub:
            s_next = scores(c + 1)
        p = jnp.exp2(s)
        l = l + jnp.sum(p.reshape(sub // SUBLANES, SUBLANES, p.shape[1]), 0)
        acc = acc + jnp.dot(vt_ref[0, 0, :, c * sub:(c + 1) * sub], p.astype(BF16), preferred_element_type=F32)
    l_scr[...] = l
    acc_scr[...] = acc

    @pl.when(j == nk - 1)
    def _():
        o = (acc / jnp.sum(l, 0, keepdims=True)).T
        for t in range(qn):
            o_ref[0, t * tq:(t + 1) * tq, :] = jnp.concatenate(
                [o[(t * ATT_REP + r) * tq:(t * ATT_REP + r + 1) * tq] for r in range(ATT_REP)], -1
            ).astype(o_ref.dtype)


def _flash(qt, k, vt, tq, tk, qn, q_blk0, nq, nk, bounded=False):
    b = qt.shape[0]
    lq = nq * qn * tq
    sub = _pick_tile(tk, (256, 128))
    lanes = qn * ATT_REP * tq
    if bounded:
        body = functools.partial(_flash_bounded_kernel, nk=nk, tq=tq, sub=sub, qn=qn)
        scratch = [pltpu.VMEM((SUBLANES, lanes), F32), pltpu.VMEM((HEAD_DIM, lanes), F32)]
    else:
        body = functools.partial(_flash_kernel, nk=nk, tq=tq, sub=sub, qn=qn)
        scratch = [pltpu.VMEM((1, lanes), F32), pltpu.VMEM((1, lanes), F32), pltpu.VMEM((HEAD_DIM, lanes), F32)]
    return pl.pallas_call(
        body,
        grid=(b, ATT_KV_HEADS, nq, nk),
        in_specs=[pl.BlockSpec((1, 1, qn, HEAD_DIM, ATT_REP * tq), lambda bi, g, i, j: (bi, g, i + q_blk0, 0, 0)),
                  pl.BlockSpec((1, 1, tk, HEAD_DIM), lambda bi, g, i, j: (bi, g, j, 0)),
                  pl.BlockSpec((1, 1, HEAD_DIM, tk), lambda bi, g, i, j: (bi, g, 0, j))],
        out_specs=pl.BlockSpec((1, qn * tq, ATT_REP * HEAD_DIM), lambda bi, g, i, j: (bi, i, g)),
        out_shape=jax.ShapeDtypeStruct((b, lq, ATT_WIDTH), BF16),
        scratch_shapes=scratch,
        compiler_params=_cparams(("parallel", "parallel", "parallel", "arbitrary")),
        name="flash_bounded" if bounded else "flash",
    )(qt, k, vt)


def _prev_next(u, up8, un8, i, tm, lc, lt):
    start = i * tm
    p_ok = jnp.logical_and(start != 0, start != lc)
    n_ok = jnp.logical_and(start + tm != lc, start + tm != lt)
    prow = jnp.where(p_ok, up8[SUBLANES - 1:SUBLANES], 0.0)
    nrow = jnp.where(n_ok, un8[0:1], 0.0)
    rid = lax.broadcasted_iota(jnp.int32, u.shape, 0)
    prev = jnp.where(rid == 0, prow, pltpu.roll(u, 1, 0))
    nxt = jnp.where(rid == tm - 1, nrow, pltpu.roll(u, tm - 1, 0))
    return prev, nxt


def _halo_specs(tm, c, lt):
    r = tm // SUBLANES
    last = lt // SUBLANES - 1
    return [pl.BlockSpec((1, tm, c), lambda bi, i: (bi, i, 0)),
            pl.BlockSpec((1, SUBLANES, c), lambda bi, i: (bi, jnp.maximum(i * r - 1, 0), 0)),
            pl.BlockSpec((1, SUBLANES, c), lambda bi, i: (bi, jnp.minimum((i + 1) * r, last), 0))]


def _softplus(z):
    return jnp.maximum(z, 0.0) + jnp.log1p(jnp.exp(-jnp.abs(z)))


def _rwkv_prep_kernel(u_ref, up_ref, un_ref, mu_ref, wl_ref, bd_ref, w0_ref, a0_ref, kkw_ref, ka_ref, rk_ref,
                      r_o, v_o, kk_o, g_o, bon_o, lw_o, be_o, kd_o, *, lc, lt, tm):
    i = pl.program_id(1)
    u = u_ref[0]
    prev, nxt = _prev_next(u, up_ref[0], un_ref[0], i, tm, lc, lt)
    us = u + mu_ref[0:1] * (prev - u) + mu_ref[1:2] * (nxt - u)
    c = RWKV_WIDTH
    r = us[:, 0:c]
    k = us[:, c:2 * c]
    v = us[:, 2 * c:3 * c]
    slab = us[:, 3 * c:3 * c + LANES]
    lane = lax.broadcasted_iota(jnp.int32, slab.shape, 1)
    o_a = 2 * W_LORA
    o_g = o_a + 2 * A_LORA
    act = jnp.where(lane < o_a, jnp.tanh(slab),
                    jnp.where(lane < o_g, slab,
                              jnp.where(lane < o_g + G_LORA, jax.nn.sigmoid(slab), 0.0)))
    lo = _dotp(act, wl_ref[...], 3)
    bd = bd_ref[...]
    kk0 = k * kkw_ref[...]
    kk = kk0 * lax.rsqrt(_dot01(kk0 * kk0, bd) + 1e-12)
    r_o[0] = r
    v_o[0] = v
    kk_o[0] = kk
    g_o[0] = lo[:, 4 * c:5 * c]
    bon = jnp.zeros_like(r)
    for d in range(2):
        w_raw = w0_ref[d:d + 1] + lo[:, d * c:(d + 1) * c]
        lw = -jnp.exp(-_softplus(-w_raw) - 0.5)
        a = jax.nn.sigmoid(a0_ref[d:d + 1] + lo[:, (2 + d) * c:(3 + d) * c])
        kd = k * (1.0 + (a - 1.0) * ka_ref[...])
        lw_o[0, d] = lw
        be_o[0, d] = a * kk
        kd_o[0, d] = kd
        bon = bon + r * kd * rk_ref[...]
    bon_o[0] = _dot01(bon, bd) * v


def _rwkv_prep(ur, mu, wl, bd, w0, a0, kkw, ka, rk, lc):
    b, lt, cp = ur.shape
    tm = TOK_TILE
    c = RWKV_WIDTH
    full = lambda shape: pl.BlockSpec(shape, lambda bi, i: (0,) * len(shape))
    tok = pl.BlockSpec((1, tm, c), lambda bi, i: (bi, i, 0))
    tok2 = pl.BlockSpec((1, 2, tm, c), lambda bi, i: (bi, 0, i, 0))
    s1 = jax.ShapeDtypeStruct((b, lt, c), F32)
    s2 = jax.ShapeDtypeStruct((b, 2, lt, c), F32)
    return pl.pallas_call(
        functools.partial(_rwkv_prep_kernel, lc=lc, lt=lt, tm=tm),
        grid=(b, lt // tm),
        in_specs=_halo_specs(tm, cp, lt) + [full((2, cp)), full((LANES, 5 * c)), full((c, c)), full((2, c)),
                                            full((2, c)), full((1, c)), full((1, c)), full((1, c))],
        out_specs=[tok, tok, tok, tok, tok, tok2, tok2, tok2],
        out_shape=[s1, s1, s1, s1, s1, s2, s2, s2],
        compiler_params=_cparams(("parallel", "parallel")),
        name="rwkv_prep",
    )(ur, ur, ur, mu, wl, bd, w0, a0, kkw, ka, rk)


def _mm(a, b):
    return jnp.dot(a, b, precision=HI, preferred_element_type=F32)


_NN = ((1,), (0,))
_NT = ((1,), (1,))
_TN = ((0,), (0,))


def _split2(a):
    hi = a.astype(BF16)
    return hi, (a - hi.astype(F32)).astype(BF16)


def _dotp(a, b, passes, dims=_NN):
    if a.ndim == 3:
        dn = (((dims[0][0] + 1,), (dims[1][0] + 1,)), ((0,), (0,)))
    else:
        dn = (dims, ((), ()))
    dg = lambda p, q: lax.dot_general(p, q, dn, preferred_element_type=F32)
    if passes == 1:
        return dg(a.astype(BF16), b.astype(BF16))
    ah, al = _split2(a)
    bh, bl = _split2(b)
    return dg(ah, bh) + dg(ah, bl) + dg(al, bh)


def _dot01(a, ones):
    ah, al = _split2(a)
    o = ones.astype(BF16)
    return jnp.dot(ah, o, preferred_element_type=F32) + jnp.dot(al, o, preferred_element_type=F32)


P_M = 1
P_INV = 1
P_W = 1
P_Z = 1
P_STATE = 3
P_DFT = 1


def _unit_tri_inv(a_mat, row, col, eye, passes):
    t = a_mat.shape[-1]
    eye_f = eye.astype(F32)
    base = SUBLANES
    same = (row // base) == (col // base)
    n1 = -jnp.where(same, a_mat, 0.0)
    n2 = _dotp(n1, n1, passes)
    n4 = _dotp(n2, n2, passes)
    x = _dotp(_dotp(eye_f + n1, eye_f + n2, passes), eye_f + n4, passes)
    m = base
    while m < t:
        off = jnp.logical_and((row // (2 * m)) == (col // (2 * m)), (row // m) != (col // m))
        x = x - _dotp(x, _dotp(jnp.where(off, a_mat, 0.0), x, passes), passes)
        m *= 2
    return x


def _wkv_kernel(rf_ref, vf_ref, kkf_ref, rb_ref, vb_ref, kkb_ref, lwf_ref, bef_ref, kdf_ref,
                lwb_ref, beb_ref, kdb_ref, yf_ref, yb_ref, h_scr, pq_scr, ry_scr):
    t = WKV_CHUNK
    n = HEAD_DIM
    g = WKV_TILE // WKV_CHUNK
    tt = WKV_TILE
    nh = RWKV_HEADS
    nd = g * nh
    nu = 2 * nd
    orders = (list(range(g)), list(range(g - 1, -1, -1)))

    @pl.when(pl.program_id(1) == 0)
    def _():
        h_scr[...] = jnp.zeros(h_scr.shape, F32)
        pq_scr[...] = jnp.zeros(pq_scr.shape, F32)
        ry_scr[...] = jnp.zeros(ry_scr.shape, F32)

    hm = h_scr[...]
    for p in range(g):
        both = lambda ref: jnp.concatenate([ref[d * nd + p * nh:d * nd + (p + 1) * nh] for d in range(2)], 0)
        ry = both(ry_scr)
        pq = both(pq_scr)
        y = _dotp(ry[:, :, :n], hm, P_STATE) + ry[:, :, n:]
        hm = _dotp(pq[:, :, :n], hm, P_STATE) + pq[:, :, n:]
        for d, y_ref in enumerate((yf_ref, yb_ref)):
            c = orders[d][p]
            y_ref[0, c * t:(c + 1) * t, :] = jnp.concatenate([y[d * nh + h] for h in range(nh)], -1)
    h_scr[...] = hm

    row = lax.broadcasted_iota(jnp.int32, (tt, tt), 0)
    col = lax.broadcasted_iota(jnp.int32, (tt, tt), 1)
    same = (row // t) == (col // t)

    def scaled(d, r_ref, v_ref, kk_ref, lw_ref, be_ref, kd_ref):
        tri = jnp.logical_and(same, (row >= col) if d == 0 else (row <= col))
        sums = jnp.concatenate([jnp.where(tri, 1.0, 0.0), jnp.where(same, 1.0, 0.0)], 0).astype(BF16)
        lw = lw_ref[0, 0]
        l1 = lw.astype(BF16)
        rem = lw - l1.astype(F32)
        l2 = rem.astype(BF16)
        l3 = (rem - l2.astype(F32)).astype(BF16)
        cc = (jnp.dot(sums, l1, preferred_element_type=F32) + jnp.dot(sums, l2, preferred_element_type=F32)
              + jnp.dot(sums, l3, preferred_element_type=F32))
        cum = cc[:tt]
        ctot = cc[tt:]
        e_neg = jnp.exp(-cum)
        e_end = jnp.exp(ctot - cum)
        be = be_ref[0, 0]
        kd = kd_ref[0, 0]

        def units(x, rows=t):
            return jnp.stack([x[orders[d][p] * t:orders[d][p] * t + rows, h * n:(h + 1) * n]
                              for p in range(g) for h in range(nh)], 0)

        return dict(kap=units(kk_ref[0] * jnp.exp(cum - lw)), rt=units(r_ref[0] * jnp.exp(cum)), vh=units(v_ref[0]),
                    bet=units(be * e_neg), kdt=units(kd * e_neg), beh=units(be * e_end), kdh=units(kd * e_end),
                    gend=units(jnp.exp(ctot), 1))

    parts = (scaled(0, rf_ref, vf_ref, kkf_ref, lwf_ref, bef_ref, kdf_ref),
             scaled(1, rb_ref, vb_ref, kkb_ref, lwb_ref, beb_ref, kdb_ref))
    cat = lambda key: jnp.concatenate([parts[0][key], parts[1][key]], 0)
    kap, rt, vh = cat("kap"), cat("rt"), cat("vh")
    r64 = lax.broadcasted_iota(jnp.int32, (t, t), 0)
    c64 = lax.broadcasted_iota(jnp.int32, (t, t), 1)
    eye = r64 == c64
    unit = lax.broadcasted_iota(jnp.int32, (nu, t, t), 0)
    ahead = (r64 - c64)[None] * jnp.where(unit < nd, 1, -1)
    strict = ahead > 0
    incl = ahead >= 0
    m = _dotp(jnp.concatenate([kap, rt], 1), jnp.concatenate([cat("bet"), cat("kdt")], 1), P_M, _NT)
    a_mat = jnp.where(strict, m[:, :t, :t], 0.0)
    b_mat = jnp.where(strict, m[:, :t, t:], 0.0)
    ab_r = jnp.concatenate([jnp.where(incl, m[:, t:, :t], 0.0), jnp.where(incl, m[:, t:, t:], 0.0)], 2)
    tinv = _unit_tri_inv(a_mat, r64, c64, eye, P_INV)
    w = _dotp(tinv, jnp.concatenate([kap, _dotp(b_mat, vh, P_W)], 2), P_W)
    z = jnp.concatenate([-w, jnp.concatenate([jnp.zeros((nu, t, n), F32), vh], 2)], 1)
    ry = _dotp(ab_r, z, P_Z)
    ry_scr[:, :, :n] = ry[:, :, :n] + rt
    ry_scr[:, :, n:] = ry[:, :, n:]
    pq = _dotp(jnp.concatenate([cat("beh"), cat("kdh")], 1), z, P_Z, _TN)
    gd = jnp.where(eye, jnp.broadcast_to(cat("gend"), (nu, n, n)), 0.0)
    pq_scr[:, :, :n] = pq[:, :, :n] + gd
    pq_scr[:, :, n:] = pq[:, :, n:]


def _wkv_scan(r, v, kk, lw, be, kd, lc):
    b, lt, c = r.shape
    tt = WKV_TILE
    nt = lt // tt
    ntc = lc // tt
    tiles = (lambda i: i, lambda i: jnp.where(i < ntc, ntc - 1 - i, nt - 1 - (i - ntc)))
    t_in = lambda d: (lambda i: tiles[d](jnp.minimum(i, nt - 1)))
    t_out = lambda d: (lambda i: tiles[d](jnp.maximum(i - 1, 0)))
    one = lambda d: pl.BlockSpec((1, tt, c), lambda bi, i: (bi, t_in(d)(i), 0))
    two = lambda d: pl.BlockSpec((1, 1, tt, c), lambda bi, i: (bi, d, t_in(d)(i), 0))
    out = lambda d: pl.BlockSpec((1, tt, c), lambda bi, i: (bi, t_out(d)(i), 0))
    nu = 2 * (tt // WKV_CHUNK) * RWKV_HEADS
    ysh = jax.ShapeDtypeStruct((b, lt, c), F32)
    return pl.pallas_call(
        _wkv_kernel,
        grid=(b, nt + 1),
        in_specs=[one(0)] * 3 + [one(1)] * 3 + [two(0)] * 3 + [two(1)] * 3,
        out_specs=[out(0), out(1)],
        out_shape=[ysh, ysh],
        scratch_shapes=[pltpu.VMEM((2 * RWKV_HEADS, HEAD_DIM, HEAD_DIM), F32),
                        pltpu.VMEM((nu, HEAD_DIM, 2 * HEAD_DIM), F32),
                        pltpu.VMEM((nu, WKV_CHUNK, 2 * HEAD_DIM), F32)],
        compiler_params=_cparams(("parallel", "arbitrary")),
        name="wkv_scan",
    )(r, v, kk, r, v, kk, lw, be, kd, lw, be, kd)


def _rwkv_out_kernel(yf_ref, yb_ref, bon_ref, g_ref, bd_ref, gg_ref, gb_ref, o_ref):
    y = yf_ref[0] + yb_ref[0] + bon_ref[0]
    bd = bd_ref[...]
    mu = _dot01(y, bd) * (1.0 / HEAD_DIM)
    yc = y - mu
    var = _dot01(yc * yc, bd) * (1.0 / HEAD_DIM)
    yn = yc * lax.rsqrt(var + RWKV_GN_EPS) * gg_ref[...] + gb_ref[...]
    o_ref[0] = (yn * g_ref[0]).astype(o_ref.dtype)


def _rwkv_out(yf, yb, bon, g, bd, gg, gb):
    b, lt, c = yf.shape
    tm = TOK_TILE
    tok = pl.BlockSpec((1, tm, c), lambda bi, i: (bi, i, 0))
    full = lambda shape: pl.BlockSpec(shape, lambda bi, i: (0,) * len(shape))
    return pl.pallas_call(
        _rwkv_out_kernel,
        grid=(b, lt // tm),
        in_specs=[tok, tok, tok, tok, full((c, c)), full((1, c)), full((1, c))],
        out_specs=tok,
        out_shape=jax.ShapeDtypeStruct((b, lt, c), BF16),
        compiler_params=_cparams(("parallel", "parallel")),
        name="rwkv_out",
    )(yf, yb, bon, g, bd, gg, gb)


def _hy_prep_kernel(u_ref, up_ref, un_ref, w_ref, b_ref, x1_o, x2_o, v_o, *, lc, lt, tm):
    i = pl.program_id(1)
    u = u_ref[0]
    prev, nxt = _prev_next(u, up_ref[0], un_ref[0], i, tm, lc, lt)
    y = prev * w_ref[0:1] + u * w_ref[1:2] + nxt * w_ref[2:3] + b_ref[...]
    c = HY_WIDTH
    x1_o[0] = y[:, :c]
    x2_o[0] = y[:, c:2 * c]
    v_o[0] = y[:, 2 * c:]


def _hy_prep(uh, w, bias, lc):
    b, lt, cin = uh.shape
    tm = TOK_TILE
    c = HY_WIDTH
    full = lambda shape: pl.BlockSpec(shape, lambda bi, i: (0,) * len(shape))
    tok = pl.BlockSpec((1, tm, c), lambda bi, i: (bi, i, 0))
    s1 = jax.ShapeDtypeStruct((b, lt, c), F32)
    return pl.pallas_call(
        functools.partial(_hy_prep_kernel, lc=lc, lt=lt, tm=tm),
        grid=(b, lt // tm),
        in_specs=_halo_specs(tm, cin, lt) + [full((3, cin)), full((1, cin))],
        out_specs=[tok, tok, tok],
        out_shape=[s1, s1, s1],
        compiler_params=_cparams(("parallel", "parallel")),
        name="hy_prep",
    )(uh, uh, uh, w, bias)


def _hy_filter_kernel(z_ref, w1_ref, b1_ref, f1_ref, w2_ref, b2_ref, f2_ref, w3_ref, dec_ref,
                      fw_o, bw_o, nrm_o, *, tl):
    i = pl.program_id(0)
    z = z_ref[...]
    h = jnp.sin(f1_ref[...] * (_dotp(z, w1_ref[...], 3) + b1_ref[...]))
    h = jnp.sin(f2_ref[...] * (_dotp(h, w2_ref[...], 3) + b2_ref[...]))
    h = _dotp(h, w3_ref[...], 3) * jnp.exp(-z[:, 0:1] * dec_ref[...])
    c = HY_WIDTH
    fw = jnp.concatenate([h[:, 0:c], h[:, 2 * c:3 * c]], 1)
    bw = jnp.concatenate([h[:, c:2 * c], h[:, 3 * c:4 * c]], 1)
    rid = i * tl + lax.broadcasted_iota(jnp.int32, (tl, 1), 0)
    bw = jnp.where(rid == 0, 0.0, bw)
    fw_o[...] = fw
    bw_o[...] = bw

    @pl.when(i == 0)
    def _():
        nrm_o[...] = jnp.zeros(nrm_o.shape, F32)

    nrm_o[...] += jnp.sum(jnp.abs(fw) + jnp.abs(bw), 0, keepdims=True)


def _hy_filter(feat, w1p, b1, f1, w2, b2, f2, w3, dec):
    l, fe = feat.shape
    tl = _pick_tile(l, (512, 256))
    c2 = HY_ORDER * HY_WIDTH
    full = lambda shape: pl.BlockSpec(shape, lambda i: (0,) * len(shape))
    return pl.pallas_call(
        functools.partial(_hy_filter_kernel, tl=tl),
        grid=(l // tl,),
        in_specs=[pl.BlockSpec((tl, fe), lambda i: (i, 0)), full(w1p.shape), full(b1.shape), full(f1.shape),
                  full(w2.shape), full(b2.shape), full(f2.shape), full(w3.shape), full(dec.shape)],
        out_specs=[pl.BlockSpec((tl, c2), lambda i: (i, 0)), pl.BlockSpec((tl, c2), lambda i: (i, 0)),
                   pl.BlockSpec((1, c2), lambda i: (0, 0))],
        out_shape=[jax.ShapeDtypeStruct((l, c2), F32), jax.ShapeDtypeStruct((l, c2), F32),
                   jax.ShapeDtypeStruct((1, c2), F32)],
        compiler_params=_cparams(("arbitrary",)),
        name="hy_filter",
    )(feat, w1p, b1, f1, w2, b2, f2, w3, dec)


def _dft_cols_kernel(f_ref, xa_ref, xb_ref, o_ref, *, n1, pair):
    f = f_ref[...]
    pa = _dotp(f, xa_ref[0], P_DFT)
    pb = _dotp(f, xb_ref[0], P_DFT)
    if pair:
        o_ref[0] = pa[:n1].astype(o_ref.dtype)
        o_ref[1] = pa[n1:].astype(o_ref.dtype)
        o_ref[2] = pb[:n1].astype(o_ref.dtype)
        o_ref[3] = pb[n1:].astype(o_ref.dtype)
    else:
        o_ref[0] = (pa[:n1] - pb[n1:]).astype(o_ref.dtype)
        o_ref[1] = (pb[:n1] + pa[n1:]).astype(o_ref.dtype)


def _dft_cols(fstack, xa, ia, xb, ib, pair):
    n1 = fstack.shape[0] // 2
    _, nh, cols = xa.shape
    tc = _pick_tile(cols, (4096, 2048, 1024, 512, 256, 128))
    no = 4 if pair else 2
    return pl.pallas_call(
        functools.partial(_dft_cols_kernel, n1=n1, pair=pair),
        grid=(cols // tc,),
        in_specs=[pl.BlockSpec(fstack.shape, lambda j: (0, 0)),
                  pl.BlockSpec((1, nh, tc), lambda j: (ia, 0, j)),
                  pl.BlockSpec((1, nh, tc), lambda j: (ib, 0, j))],
        out_specs=pl.BlockSpec((no, n1, tc), lambda j: (0, 0, j)),
        out_shape=jax.ShapeDtypeStruct((no, n1, cols), BF16),
        compiler_params=_cparams(("parallel",)),
        name="dft_cols",
    )(fstack, xa, xb)


def _cplx_left(gs, zr, zi, n):
    c = zr.shape[1]
    p = _dotp(gs, jnp.concatenate([zr, zi], 1), P_DFT)
    return p[:n, :c] - p[n:, c:], p[:n, c:] + p[n:, :c]


def _spec_kernel(a_ref, g_ref, nrm_ref, o_ref, *, n_total):
    n2 = DFT_N2
    gs = jnp.concatenate([g_ref[0, 0], g_ref[0, 1]], 0)
    fr, fi = _cplx_left(gs, a_ref[0, 0], a_ref[1, 0], n2)
    br, bi = _cplx_left(gs, a_ref[2, 0], a_ref[3, 0], n2)
    s = 1.0 / (nrm_ref[...] * n_total)
    o_ref[0, 0] = ((fr + br) * s).astype(o_ref.dtype)
    o_ref[1, 0] = ((fi - bi) * s).astype(o_ref.dtype)


def _spec(a4, g, nrm, n_total):
    _, n1, n2, c2 = a4.shape
    return pl.pallas_call(
        functools.partial(_spec_kernel, n_total=float(n_total)),
        grid=(n1,),
        in_specs=[pl.BlockSpec((4, 1, n2, c2), lambda k: (0, k, 0, 0)),
                  pl.BlockSpec((1, 2, n2, n2), lambda k: (k, 0, 0, 0)),
                  pl.BlockSpec((1, c2), lambda k: (0, 0))],
        out_specs=pl.BlockSpec((2, 1, n2, c2), lambda k: (0, k, 0, 0)),
        out_shape=jax.ShapeDtypeStruct((2, n1, n2, c2), BF16),
        compiler_params=_cparams(("parallel",)),
        name="hy_spec",
    )(a4, g, nrm)


def _conv_mid_kernel(a_ref, g_ref, k_ref, o_ref):
    n2 = DFT_N2
    gs = jnp.concatenate([g_ref[0, 0], g_ref[0, 1]], 0)
    xr, xi = _cplx_left(gs, a_ref[0, 0], a_ref[1, 0], n2)
    c = xr.shape[1]
    kr = k_ref[0, 0].astype(F32)
    ki = k_ref[1, 0].astype(F32)
    zr = xr * kr - xi * ki
    zi = xr * ki + xi * kr
    zst = jnp.concatenate([jnp.concatenate([zr, zi], 1), jnp.concatenate([zi, -zr], 1)], 0)
    y = _dotp(gs, zst, P_DFT, _TN)
    o_ref[0, 0] = y[:, :c].astype(o_ref.dtype)
    o_ref[1, 0] = y[:, c:].astype(o_ref.dtype)


def _conv_mid(a, g, kspec, order):
    _, n1, n2, c = a.shape
    return pl.pallas_call(
        _conv_mid_kernel,
        grid=(n1,),
        in_specs=[pl.BlockSpec((2, 1, n2, c), lambda k: (0, k, 0, 0)),
                  pl.BlockSpec((1, 2, n2, n2), lambda k: (k, 0, 0, 0)),
                  pl.BlockSpec((2, 1, n2, c), lambda k: (0, k, 0, order))],
        out_specs=pl.BlockSpec((2, 1, n2, c), lambda k: (0, k, 0, 0)),
        out_shape=jax.ShapeDtypeStruct((2, n1, n2, c), BF16),
        compiler_params=_cparams(("parallel",)),
        name="hy_conv_mid",
    )(a, g, kspec)


def _idft_cols_kernel(c_ref, b_ref, g0_ref, g1_ref, x0_ref, x1_ref, bias_ref, o_ref, *, nh):
    cs = c_ref[...]
    pr = _dotp(cs, b_ref[0], P_DFT)
    pi = _dotp(cs, b_ref[1], P_DFT)
    yr = pr[:nh] - pi[nh:]
    yi = pi[:nh] + pr[nh:]
    bias = bias_ref[...]
    o_ref[0] = g0_ref[0] * (yr + x0_ref[0] * bias)
    o_ref[1] = g1_ref[0] * (yi + x1_ref[0] * bias)


def _idft_cols(cstack, bv, gate, xin, bias_cols):
    nh2, n1 = cstack.shape
    nh = nh2 // 2
    cols = bv.shape[-1]
    tc = _pick_tile(cols, (4096, 2048, 1024, 512, 256, 128))
    row = lambda bi: pl.BlockSpec((1, nh, tc), lambda j: (bi, 0, j))
    return pl.pallas_call(
        functools.partial(_idft_cols_kernel, nh=nh),
        grid=(cols // tc,),
        in_specs=[pl.BlockSpec((nh2, n1), lambda j: (0, 0)),
                  pl.BlockSpec((2, n1, tc), lambda j: (0, 0, j)),
                  row(0), row(1), row(0), row(1),
                  pl.BlockSpec((1, tc), lambda j: (0, j))],
        out_specs=pl.BlockSpec((2, nh, tc), lambda j: (0, 0, j)),
        out_shape=jax.ShapeDtypeStruct((2, nh, cols), F32),
        compiler_params=_cparams(("parallel",)),
        name="idft_cols",
    )(cstack, bv, gate, gate, xin, xin, bias_cols)


def _hy_ctx_kernel(x1_ref, x2_ref, v_ref, fw_ref, bw_ref, nrm_ref, bias_ref, ff_ref, ci_ref, o_ref, *, lc):
    n = 2 * lc
    c = HY_WIDTH
    ff = ff_ref[...]
    ci = ci_ref[...]
    pf = _mm(ff, fw_ref[...])
    pb = _mm(ff, bw_ref[...])
    s = 1.0 / (nrm_ref[...] * float(n))
    kr = (pf[:n] + pb[:n]) * s
    ki = (pf[n:] - pb[n:]) * s

    def conv(z0, z1, o):
        xr, xi = _cplx_left(ff, z0, z1, n)
        krr = kr[:, o * c:(o + 1) * c]
        kii = ki[:, o * c:(o + 1) * c]
        return _cplx_left(ci, xr * krr - xi * kii, xr * kii + xi * krr, lc)

    v0 = v_ref[0]
    v1 = v_ref[1]
    y0, y1 = conv(v0, v1, 0)
    z0 = x1_ref[0] * (y0 + v0 * bias_ref[0:1])
    z1 = x1_ref[1] * (y1 + v1 * bias_ref[0:1])
    y0, y1 = conv(z0, z1, 1)
    o_ref[0] = x2_ref[0] * (y0 + z0 * bias_ref[1:2])
    o_ref[1] = x2_ref[1] * (y1 + z1 * bias_ref[1:2])


def _hy_ctx(x1, x2, v, fw, bw, nrm, bias, ff, ci):
    b, lc, c = v.shape
    vm = pl.BlockSpec(memory_space=pltpu.VMEM)
    return pl.pallas_call(
        functools.partial(_hy_ctx_kernel, lc=lc),
        in_specs=[vm] * 9,
        out_specs=vm,
        out_shape=jax.ShapeDtypeStruct((b, lc, c), F32),
        compiler_params=pltpu.CompilerParams(vmem_limit_bytes=VMEM_LIMIT),
        name="hy_ctx",
    )(x1, x2, v, fw, bw, nrm, bias, ff, ci)


def _outproj_kernel(attc_ref, attl_ref, rw_ref, hyc_ref, hyl_ref, x_ref, mod_ref, w_ref, lg_ref, lb_ref, o_ref,
                    *, lc, tm):
    row0 = pl.program_id(1) * tm
    a0 = ATT_WIDTH
    a1 = ATT_WIDTH + RWKV_WIDTH
    is_ctx = row0 < lc
    att = jnp.where(is_ctx, attc_ref[0], attl_ref[0])
    hy = jnp.where(is_ctx, hyc_ref[0], hyl_ref[0])
    o = jnp.dot(att, w_ref[:a0], preferred_element_type=F32)
    o += jnp.dot(rw_ref[0], w_ref[a0:a1], preferred_element_type=F32)
    o += jnp.dot(hy.astype(BF16), w_ref[a1:], preferred_element_type=F32)
    g = _sel_mod(mod_ref, 2, row0, tm, lc)
    y = ALPHA * x_ref[0] + g * o
    o_ref[0] = _layer_norm(y) * lg_ref[...] + lb_ref[...]


def _outproj(att_c, att_l, rw, hy_c, hy_l, xx, mod, w, lg, lb, lc):
    b, lt, d = xx.shape
    tm = TOK_TILE
    nct = lc // tm
    tok = lambda c: pl.BlockSpec((1, tm, c), lambda bi, i: (bi, i, 0))
    ctx = lambda c: pl.BlockSpec((1, tm, c), lambda bi, i: (bi, jnp.minimum(i, nct - 1), 0))
    lat = lambda c: pl.BlockSpec((1, tm, c), lambda bi, i: (bi, jnp.maximum(i - nct, 0), 0))
    full = lambda shape: pl.BlockSpec(shape, lambda bi, i: (0,) * len(shape))
    return pl.pallas_call(
        functools.partial(_outproj_kernel, lc=lc, tm=tm),
        grid=(b, lt // tm),
        in_specs=[ctx(ATT_WIDTH), lat(ATT_WIDTH), tok(RWKV_WIDTH), ctx(HY_WIDTH), lat(HY_WIDTH), tok(d),
                  pl.BlockSpec((1, 2, 6, d), lambda bi, i: (bi, 0, 0, 0)),
                  full(w.shape), full((1, d)), full((1, d))],
        out_specs=tok(d),
        out_shape=jax.ShapeDtypeStruct((b, lt, d), F32),
        compiler_params=_cparams(("parallel", "parallel")),
        name="outproj",
    )(att_c, att_l, rw, hy_c, hy_l, xx, mod, w, lg, lb)


def _ffn_kernel(x_ref, mod_ref, w1_ref, w3_ref, w2_ref, lg_ref, lb_ref, o_ref, h_scr, acc_scr, *, lc, tm, nf):
    row0 = pl.program_id(1) * tm
    f = pl.program_id(2)

    @pl.when(f == 0)
    def _():
        sh = _sel_mod(mod_ref, 3, row0, tm, lc)
        sc = _sel_mod(mod_ref, 4, row0, tm, lc)
        h_scr[...] = (_layer_norm(x_ref[0]) * (1.0 + sc) + sh).astype(BF16)
        acc_scr[...] = jnp.zeros(acc_scr.shape, F32)

    h = h_scr[...]
    a = jnp.dot(h, w1_ref[...], preferred_element_type=F32)
    g = jnp.dot(h, w3_ref[...], preferred_element_type=F32)
    acc_scr[...] += jnp.dot((_silu(a) * g).astype(BF16), w2_ref[...], preferred_element_type=F32)

    @pl.when(f == nf - 1)
    def _():
        gate = _sel_mod(mod_ref, 5, row0, tm, lc)
        y = ALPHA * x_ref[0] + gate * acc_scr[...]
        o_ref[0] = _layer_norm(y) * lg_ref[...] + lb_ref[...]


def _ffn(xx, mod, w1, w3, w2, lg, lb, lc):
    b, lt, d = xx.shape
    ff = w1.shape[1]
    tm = _pick_tile(lt, (1280, 768, 512, 256))
    tf = _pick_tile(ff, (256, 128))
    nf = ff // tf
    return pl.pallas_call(
        functools.partial(_ffn_kernel, lc=lc, tm=tm, nf=nf),
        grid=(b, lt // tm, nf),
        in_specs=[pl.BlockSpec((1, tm, d), lambda bi, i, f: (bi, i, 0)),
                  pl.BlockSpec((1, 2, 6, d), lambda bi, i, f: (bi, 0, 0, 0)),
                  pl.BlockSpec((d, tf), lambda bi, i, f: (0, f)),
                  pl.BlockSpec((d, tf), lambda bi, i, f: (0, f)),
                  pl.BlockSpec((tf, d), lambda bi, i, f: (f, 0)),
                  pl.BlockSpec((1, d), lambda bi, i, f: (0, 0)),
                  pl.BlockSpec((1, d), lambda bi, i, f: (0, 0))],
        out_specs=pl.BlockSpec((1, tm, d), lambda bi, i, f: (bi, i, 0)),
        out_shape=jax.ShapeDtypeStruct((b, lt, d), F32),
        scratch_shapes=[pltpu.VMEM((tm, d), BF16), pltpu.VMEM((tm, d), F32)],
        compiler_params=_cparams(("parallel", "parallel", "arbitrary")),
        name="ffn",
    )(xx, mod, w1, w3, w2, lg, lb)


def _route(x_ref, mod_ref, wr_ref, row0, tm, lc):
    sh = _sel_mod(mod_ref, 3, row0, tm, lc)
    sc = _sel_mod(mod_ref, 4, row0, tm, lc)
    h = _layer_norm(x_ref[0]) * (1.0 + sc) + sh
    logits = _dotp(h, wr_ref[...], 3)
    lane = lax.broadcasted_iota(jnp.int32, logits.shape, 1)
    neg = jnp.float32(-jnp.inf)
    lg = jnp.where(lane < N_EXPERTS, logits, neg)
    m1 = jnp.max(lg, -1, keepdims=True)
    i1 = jnp.min(jnp.where(lg == m1, lane, LANES), -1, keepdims=True)
    lg2 = jnp.where(lane == i1, neg, lg)
    m2 = jnp.max(lg2, -1, keepdims=True)
    i2 = jnp.min(jnp.where(lg2 == m2, lane, LANES), -1, keepdims=True)
    e2 = jnp.exp(m2 - m1)
    return h, lane, i1, i2, 1.0 / (1.0 + e2), e2 / (1.0 + e2)


def _moe_sparse_kernel(x_ref, mod_ref, wr_ref, w1_ref, w3_ref, w2_ref, lg_ref, lb_ref, o_ref,
                       h_scr, acc_scr, gate_scr, posc_scr, posr_scr, xs_scr, ye_scr, nblk_scr, *, lc, tm, nf, ns, blk):
    row0 = pl.program_id(1) * tm
    s = pl.program_id(2)
    e = s // nf
    f = s % nf

    @pl.when(s == 0)
    def _():
        h, lane, i1, i2, g1, g2 = _route(x_ref, mod_ref, wr_ref, row0, tm, lc)
        h_scr[...] = h.astype(BF16)
        acc_scr[...] = jnp.zeros(acc_scr.shape, F32)
        routed = jnp.where(jnp.logical_or(lane == i1, lane == i2), 1.0, 0.0)
        r = lax.broadcasted_iota(jnp.int32, (tm, tm), 0)
        c = lax.broadcasted_iota(jnp.int32, (tm, tm), 1)
        before = jnp.where(c < r, 1.0, 0.0).astype(BF16)
        rank_c = jnp.dot(before, routed.astype(BF16), preferred_element_type=F32)
        gate_scr[...] = jnp.where(lane == i1, g1, 0.0) + jnp.where(lane == i2, g2, 0.0)
        posc_scr[...] = jnp.where(routed > 0.0, rank_c, -1.0)
        routed_t = routed.T[:2 * SUBLANES]
        after = jnp.where(r < c, 1.0, 0.0).astype(BF16)
        rank_r = jnp.dot(routed_t.astype(BF16), after, preferred_element_type=F32)
        posr_scr[...] = jnp.where(routed_t > 0.0, rank_r, -1.0)
        counts = jnp.sum(routed, 0, keepdims=True)
        lane1 = lax.broadcasted_iota(jnp.int32, counts.shape, 1)
        for ex in range(N_EXPERTS):
            n_rows = jnp.sum(jnp.where(lane1 == ex, counts, 0.0)).astype(jnp.int32)
            nblk_scr[ex] = (n_rows + (blk - 1)) // blk

    nblk = nblk_scr[e]
    sizes = [(k + 1) * blk for k in range(tm // blk)]

    def per_size(cond, fn):
        for k, m in enumerate(sizes):
            pl.when(jnp.logical_and(cond, nblk == k + 1))(functools.partial(fn, m))

    def gather(m):
        slot = lax.broadcasted_iota(jnp.int32, (m, tm), 0).astype(F32)
        take = jnp.where(slot == posr_scr[pl.ds(e, 1), :], 1.0, 0.0).astype(BF16)
        xs_scr[:m] = jnp.dot(take, h_scr[...], preferred_element_type=F32).astype(BF16)
        ye_scr[:m] = jnp.zeros((m, ye_scr.shape[1]), F32)

    def expert(m):
        xs = xs_scr[:m]
        a = jnp.dot(xs, w1_ref[0], preferred_element_type=F32)
        g = jnp.dot(xs, w3_ref[0], preferred_element_type=F32)
        ye_scr[:m] += jnp.dot((_silu(a) * g).astype(BF16), w2_ref[0], preferred_element_type=F32)

    def scatter(m):
        lane = lax.broadcasted_iota(jnp.int32, (tm, LANES), 1)
        column = lambda ref: jnp.sum(jnp.where(lane == e, ref[...], 0.0), -1, keepdims=True)
        slot = lax.broadcasted_iota(jnp.int32, (tm, m), 1).astype(F32)
        put = jnp.where(slot == column(posc_scr), 1.0, 0.0).astype(BF16)
        y = jnp.dot(put, ye_scr[:m].astype(BF16), preferred_element_type=F32)
        acc_scr[...] += column(gate_scr) * y

    per_size(f == 0, gather)
    per_size(True, expert)
    per_size(f == nf - 1, scatter)

    @pl.when(s == ns - 1)
    def _():
        gate = _sel_mod(mod_ref, 5, row0, tm, lc)
        y = ALPHA * x_ref[0] + gate * acc_scr[...]
        o_ref[0] = _layer_norm(y) * lg_ref[...] + lb_ref[...]


def _moe_sparse(xx, mod, wr, w1, w3, w2, lg, lb, lc):
    b, lt, d = xx.shape
    ne, _, ff = w1.shape
    tm = _pick_tile(lt, (1280, 768, 512, 256))
    tf = _pick_tile(ff, (256, 128))
    nf = ff // tf
    ns = ne * nf
    return pl.pallas_call(
        functools.partial(_moe_sparse_kernel, lc=lc, tm=tm, nf=nf, ns=ns, blk=MOE_ROW_BLOCK),
        grid=(b, lt // tm, ns),
        in_specs=[pl.BlockSpec((1, tm, d), lambda bi, i, s: (bi, i, 0)),
                  pl.BlockSpec((1, 2, 6, d), lambda bi, i, s: (bi, 0, 0, 0)),
                  pl.BlockSpec((d, LANES), lambda bi, i, s: (0, 0)),
                  pl.BlockSpec((1, d, tf), lambda bi, i, s: (s // nf, 0, s % nf)),
                  pl.BlockSpec((1, d, tf), lambda bi, i, s: (s // nf, 0, s % nf)),
                  pl.BlockSpec((1, tf, d), lambda bi, i, s: (s // nf, s % nf, 0)),
                  pl.BlockSpec((1, d), lambda bi, i, s: (0, 0)),
                  pl.BlockSpec((1, d), lambda bi, i, s: (0, 0))],
        out_specs=pl.BlockSpec((1, tm, d), lambda bi, i, s: (bi, i, 0)),
        out_shape=jax.ShapeDtypeStruct((b, lt, d), F32),
        scratch_shapes=[pltpu.VMEM((tm, d), BF16), pltpu.VMEM((tm, d), F32),
                        pltpu.VMEM((tm, LANES), F32), pltpu.VMEM((tm, LANES), F32),
                        pltpu.VMEM((2 * SUBLANES, tm), F32),
                        pltpu.VMEM((tm, d), BF16), pltpu.VMEM((tm, d), F32),
                        pltpu.SMEM((ne,), jnp.int32)],
        compiler_params=_cparams(("parallel", "parallel", "arbitrary"), VMEM_LIMIT_MOE),
        name="moe_sparse",
    )(xx, mod, wr, w1, w3, w2, lg, lb)


def _rope_tables(l, lc):
    rows = l // GRID_W
    row = jnp.repeat(jnp.arange(rows, dtype=F32), GRID_W)
    col = jnp.tile(jnp.arange(GRID_W, dtype=F32), rows)
    n_freq = HEAD_DIM // 4
    inv_freq = ROPE_THETA ** (-jnp.arange(n_freq, dtype=F32) / n_freq)
    ang = jnp.concatenate([row[:, None] * inv_freq, col[:, None] * inv_freq], -1)
    cos, sin = jnp.cos(ang), jnp.sin(ang)
    cos2 = jnp.concatenate([jnp.ones((lc, LANES), F32), jnp.concatenate([cos, cos, cos, cos], -1)], 0)
    sin2 = jnp.concatenate([jnp.zeros((lc, LANES), F32), jnp.concatenate([-sin, sin, -sin, sin], -1)], 0)
    return cos2, sin2


def _hy_features(l):
    bands = (HY_EMB - 1) // 2
    t = jnp.linspace(0.0, 1.0, l, dtype=F32)[:, None]
    f = jnp.linspace(1e-4, bands - 1, bands, dtype=F32)[None, :]
    wt = 2.0 * math.pi * jnp.arange(l, dtype=F32)[:, None] / l
    z = jnp.concatenate([t, jnp.cos(f * wt), -jnp.sin(f * wt)], -1)
    return jnp.pad(z, ((0, 0), (0, LANES - HY_EMB)))


def _angle(idx, n):
    return (2.0 * math.pi / n) * (idx % n).astype(F32)


def _dft_tables(n1):
    nh = n1 // 2
    n2 = DFT_N2
    n = n1 * n2
    k1 = jnp.arange(n1, dtype=jnp.int32)
    a1 = _angle(k1[:, None] * jnp.arange(nh, dtype=jnp.int32)[None, :], n1)
    fstack = jnp.concatenate([jnp.cos(a1), -jnp.sin(a1)], 0)
    cstack = jnp.concatenate([jnp.cos(a1.T), jnp.sin(a1.T)], 0)
    k2 = jnp.arange(n2, dtype=jnp.int32)
    at = _angle(k1[:, None] * k2[None, :], n)
    tr, ti = jnp.cos(at), -jnp.sin(at)
    a2 = _angle(k2[:, None] * k2[None, :], n2)
    fr, fi = jnp.cos(a2), -jnp.sin(a2)
    g = jnp.stack([tr[:, None, :] * fr[None] - ti[:, None, :] * fi[None],
                   tr[:, None, :] * fi[None] + ti[:, None, :] * fr[None]], 1)
    return fstack, cstack, g


def _dense_dft_tables(lc):
    n = 2 * lc
    a = _angle(jnp.arange(n, dtype=jnp.int32)[:, None] * jnp.arange(lc, dtype=jnp.int32)[None, :], n)
    ff = jnp.concatenate([jnp.cos(a), -jnp.sin(a)], 0)
    ci = jnp.concatenate([jnp.cos(a.T), jnp.sin(a.T)], 0)
    return ff, ci


def kernel(x, c, ctx, c_ctx, ada_w, ada_b, w_in, w_out, q_gain, k_gain, rwkv_mu, rwkv_w0, rwkv_wB, rwkv_a0, rwkv_aB, rwkv_gB, rwkv_kk, rwkv_ka, rwkv_rk, rwkv_gn_g, rwkv_gn_b, hy_short_w, hy_short_b, hy_w1, hy_b1, hy_freq1, hy_w2, hy_b2, hy_freq2, hy_w3, hy_decay, hy_bias, ln1_g, ln1_b, ln2_g, ln2_b, ffn_w1, ffn_w3, ffn_w2, moe_router, moe_w1, moe_w3, moe_w2):
    b, l, d = x.shape
    lc = ctx.shape[1]
    lt = lc + l
    depth = ada_w.shape[0]
    assert b == 2, "the long convolution packs the two batch rows as one complex signal"
    assert d == D_MODEL and lc % TOK_TILE == 0 and l % TOK_TILE == 0 and (2 * l) % (2 * DFT_N2) == 0
    cw = RWKV_WIDTH

    xx = jnp.concatenate([ctx, x], 1)
    cond8 = jnp.zeros((SUBLANES, d), F32).at[:b].set(c).at[b].set(c_ctx)
    mod_all = _ada_mod(cond8, ada_w, ada_b)

    cos64, sin64 = _rope_tables(l, lc)
    n1 = 2 * l // DFT_N2
    nh = n1 // 2
    cols = DFT_N2 * HY_WIDTH
    fstack, cstack, g_tab = (t.astype(BF16) for t in _dft_tables(n1))
    ff_c, ci_c = _dense_dft_tables(lc)
    feat_l = _hy_features(l)
    feat_c = _hy_features(lc)
    blk = jnp.arange(cw) // HEAD_DIM
    bd = (blk[:, None] == blk[None, :]).astype(F32)
    ch = jnp.arange(ATT_WIDTH)
    bd_att = (ch[:, None] // HEAD_DIM == ch[None, :] // HEAD_DIM).astype(BF16)
    swap_att = (ch[:, None] == (ch[None, :] + HALF_HD) % HEAD_DIM + (ch[None, :] // HEAD_DIM) * HEAD_DIM
                ).astype(BF16)
    perm64 = jnp.concatenate([jnp.arange(0, HEAD_DIM, 2), jnp.arange(1, HEAD_DIM, 2)])
    perm_att = jnp.concatenate([h * HEAD_DIM + perm64 for h in range(ATT_HEADS + ATT_KV_HEADS)]
                               + [jnp.arange(ATT_WIDTH + ATT_KV_WIDTH, IN_ATT)])
    tq = TOK_TILE
    qn = next(n for n in (4, 2, 1) if (l // tq) % n == 0)
    tk = _pick_tile(lt, (1280, 1024, 768, 512, 256))

    for li in range(depth):
        ml = mod_all[li]
        mod = jnp.stack([jnp.broadcast_to(ml[b].reshape(1, 6, d), (b, 6, d)), ml[:b].reshape(b, 6, d)], 1)
        wi = w_in[li]
        w_pad = jnp.concatenate([wi[:, :IN_ATT][:, perm_att], wi[:, IN_ATT:IN_ATT + IN_RWKV],
                                 jnp.zeros((d, IN_RWKV_PAD - IN_RWKV), F32), wi[:, IN_ATT + IN_RWKV:]],
                                1).astype(BF16)
        ua, ur, uh = _inproj(xx, mod, w_pad, lc)

        two = lambda gain: jnp.tile(gain[perm64], 2)[None]
        qt, kx, vt = _attn_prep(ua, cos64, sin64, two(q_gain[li]), two(k_gain[li]), bd_att, swap_att, lc)
        att_c = _flash(qt, kx, vt, tq, TOK_TILE, 1, l // tq, lc // tq, lc // TOK_TILE)
        s_bound = (HEAD_DIM ** 0.5) * LOG2E * jnp.max(jnp.abs(q_gain[li])) * jnp.max(jnp.abs(k_gain[li]))
        lat_args = (qt, kx, vt, tq, tk, qn, 0, l // (qn * tq), lt // tk)
        att_l = lax.cond(s_bound <= MAX_UNSHIFTED_SCORE,
                         lambda: _flash(*lat_args, bounded=True), lambda: _flash(*lat_args, bounded=False))

        wl = jnp.zeros((LANES, 5 * cw), F32)
        wl = wl.at[0:W_LORA, 0:cw].set(rwkv_wB[li, 0]).at[W_LORA:2 * W_LORA, cw:2 * cw].set(rwkv_wB[li, 1])
        o_a = 2 * W_LORA
        wl = wl.at[o_a:o_a + A_LORA, 2 * cw:3 * cw].set(rwkv_aB[li, 0])
        wl = wl.at[o_a + A_LORA:o_a + 2 * A_LORA, 3 * cw:4 * cw].set(rwkv_aB[li, 1])
        o_g = o_a + 2 * A_LORA
        wl = wl.at[o_g:o_g + G_LORA, 4 * cw:5 * cw].set(rwkv_gB[li])
        mu = jnp.pad(rwkv_mu[li], ((0, 0), (0, IN_RWKV_PAD - IN_RWKV)))
        r_, v_, kk_, g_, bon_, lw_, be_, kd_ = _rwkv_prep(
            ur, mu, wl, bd, rwkv_w0[li], rwkv_a0[li], rwkv_kk[li][None], rwkv_ka[li][None],
            rwkv_rk[li].reshape(1, cw), lc)
        yf, yb = _wkv_scan(r_, v_, kk_, lw_, be_, kd_, lc)
        rw = _rwkv_out(yf, yb, bon_, g_, bd, rwkv_gn_g[li][None], rwkv_gn_b[li][None])

        x1, x2, vv = _hy_prep(uh, hy_short_w[li], hy_short_b[li][None], lc)
        w1p = jnp.pad(hy_w1[li], ((0, LANES - HY_EMB), (0, 0)))
        fargs = (w1p, hy_b1[li][None], hy_freq1[li][None], hy_w2[li], hy_b2[li][None], hy_freq2[li][None],
                 hy_w3[li], hy_decay[li][None])
        fw, bw, nrm = _hy_filter(feat_l, *fargs)
        c2 = HY_ORDER * HY_WIDTH
        a4 = _dft_cols(fstack, fw.reshape(1, nh, DFT_N2 * c2), 0, bw.reshape(1, nh, DFT_N2 * c2), 0, True)
        kspec = _spec(a4.reshape(4, n1, DFT_N2, c2), g_tab, nrm, n1 * DFT_N2)
        lat = lambda t: t[:, lc:].reshape(b, nh, cols)
        x1l, x2l, zin = lat(x1), lat(x2), lat(vv)
        for o, gate in enumerate((x1l, x2l)):
            a = _dft_cols(fstack, zin, 0, zin, 1, False)
            bv = _conv_mid(a.reshape(2, n1, DFT_N2, HY_WIDTH), g_tab, kspec, o)
            bias_cols = jnp.tile(hy_bias[li, o], DFT_N2)[None]
            zin = _idft_cols(cstack, bv.reshape(2, n1, cols), gate, zin, bias_cols)
        hy_l = zin.reshape(b, l, HY_WIDTH)
        fw_c, bw_c, nrm_c = _hy_filter(feat_c, *fargs)
        hy_c = _hy_ctx(x1[:, :lc], x2[:, :lc], vv[:, :lc], fw_c, bw_c, nrm_c, hy_bias[li], ff_c, ci_c)

        xx = _outproj(att_c, att_l, rw, hy_c, hy_l, xx, mod, w_out[li].astype(BF16),
                      ln1_g[li][None], ln1_b[li][None], lc)

        j = li // 2
        if li % 2 == 0:
            xx = _ffn(xx, mod, ffn_w1[j].astype(BF16), ffn_w3[j].astype(BF16), ffn_w2[j].astype(BF16),
                      ln2_g[li][None], ln2_b[li][None], lc)
        else:
            wr = jnp.pad(moe_router[j], ((0, 0), (0, LANES - N_EXPERTS)))
            xx = _moe_sparse(xx, mod, wr, moe_w1[j].astype(BF16), moe_w3[j].astype(BF16), moe_w2[j].astype(BF16),
                             ln2_g[li][None], ln2_b[li][None], lc)
    return xx[:, lc:]
```

```python
import functools
import math

import jax
import jax.numpy as jnp
from jax import lax
from jax.experimental import pallas as pl
from jax.experimental.pallas import tpu as pltpu

F32 = jnp.float32
BF16 = jnp.bfloat16
HI = lax.Precision.HIGHEST

D_MODEL = 1024
DEPTH = 2
GRID_W = 64
HEAD_DIM = 64
HALF_HD = HEAD_DIM // 2
ATT_WIDTH = 512
RWKV_WIDTH = 256
HY_WIDTH = 256
ATT_HEADS = 8
ATT_KV_HEADS = 2
ATT_REP = 4
ATT_KV_WIDTH = 128
ROPE_THETA = 10000.0
QK_EPS = 1e-6
RWKV_HEADS = 4
W_LORA = 16
A_LORA = 16
G_LORA = 32
RWKV_GN_EPS = 64e-5
HY_ORDER = 2
HY_EMB = 33
HY_FFN = 64
N_EXPERTS = 8
LN_EPS = 1e-6
IN_ATT = ATT_WIDTH + 2 * ATT_KV_WIDTH
IN_RWKV = 3 * RWKV_WIDTH + 2 * W_LORA + 2 * A_LORA + G_LORA
IN_RWKV_PAD = 896
IN_HY = 3 * HY_WIDTH
ALPHA = float((2 * DEPTH) ** 0.25)
LOG2E = 1.4426950408889634
MAX_UNSHIFTED_SCORE = 40.0

LANES = 128
SUBLANES = 8
TOK_TILE = 256
WKV_CHUNK = 64
WKV_TILE = 256
DFT_N2 = 256
MOE_ROW_BLOCK = 256
VMEM_LIMIT = 48 * 1024 * 1024
VMEM_LIMIT_MOE = 56 * 1024 * 1024


def _cparams(sem, vmem=VMEM_LIMIT):
    return pltpu.CompilerParams(dimension_semantics=sem, vmem_limit_bytes=vmem)


def _pick_tile(n, cands):
    for c in cands:
        if n % c == 0:
            return c
    raise ValueError(f"no tile for {n} in {cands}")


def _layer_norm(x):
    mu = jnp.mean(x, -1, keepdims=True)
    xc = x - mu
    var = jnp.mean(xc * xc, -1, keepdims=True)
    return xc * lax.rsqrt(var + LN_EPS)


def _sel_mod(mod_ref, j, row0, tm, lc):
    rid = row0 + lax.broadcasted_iota(jnp.int32, (tm, 1), 0)
    return jnp.where(rid < lc, mod_ref[0, 0, j:j + 1, :], mod_ref[0, 1, j:j + 1, :])


def _silu(x):
    return x * jax.nn.sigmoid(x)


def _ada_kernel(c_ref, w_ref, b_ref, o_ref):
    s = _silu(c_ref[...])
    o_ref[0] = jnp.dot(s, w_ref[0], precision=HI, preferred_element_type=F32) + b_ref[0]


def _ada_mod(cond8, ada_w, ada_b):
    depth, d, n = ada_w.shape
    tn = _pick_tile(n, (1536, 1024, 512, 256, 128))
    return pl.pallas_call(
        _ada_kernel,
        grid=(depth, n // tn),
        in_specs=[pl.BlockSpec((SUBLANES, d), lambda l, j: (0, 0)),
                  pl.BlockSpec((1, d, tn), lambda l, j: (l, 0, j)),
                  pl.BlockSpec((1, 1, tn), lambda l, j: (l, 0, j))],
        out_specs=pl.BlockSpec((1, SUBLANES, tn), lambda l, j: (l, 0, j)),
        out_shape=jax.ShapeDtypeStruct((depth, SUBLANES, n), F32),
        compiler_params=_cparams(("parallel", "parallel")),
        name="ada_mod",
    )(cond8, ada_w, ada_b.reshape(depth, 1, n))


def _inproj_kernel(x_ref, mod_ref, w_ref, oa_ref, or_ref, oh_ref, *, lc, tm):
    row0 = pl.program_id(1) * tm
    sh = _sel_mod(mod_ref, 0, row0, tm, lc)
    sc = _sel_mod(mod_ref, 1, row0, tm, lc)
    h = (_layer_norm(x_ref[0]) * (1.0 + sc) + sh).astype(BF16)
    u = jnp.dot(h, w_ref[...], preferred_element_type=F32)
    oa_ref[0] = u[:, :IN_ATT]
    or_ref[0] = u[:, IN_ATT:IN_ATT + IN_RWKV_PAD]
    oh_ref[0] = u[:, IN_ATT + IN_RWKV_PAD:]


def _inproj(xx, mod, w_pad, lc):
    b, lt, d = xx.shape
    tm = TOK_TILE
    n = w_pad.shape[1]
    return pl.pallas_call(
        functools.partial(_inproj_kernel, lc=lc, tm=tm),
        grid=(b, lt // tm),
        in_specs=[pl.BlockSpec((1, tm, d), lambda bi, i: (bi, i, 0)),
                  pl.BlockSpec((1, 2, 6, d), lambda bi, i: (bi, 0, 0, 0)),
                  pl.BlockSpec((d, n), lambda bi, i: (0, 0))],
        out_specs=[pl.BlockSpec((1, tm, IN_ATT), lambda bi, i: (bi, i, 0)),
                   pl.BlockSpec((1, tm, IN_RWKV_PAD), lambda bi, i: (bi, i, 0)),
                   pl.BlockSpec((1, tm, IN_HY), lambda bi, i: (bi, i, 0))],
        out_shape=[jax.ShapeDtypeStruct((b, lt, IN_ATT), F32),
                   jax.ShapeDtypeStruct((b, lt, IN_RWKV_PAD), F32),
                   jax.ShapeDtypeStruct((b, lt, IN_HY), F32)],
        compiler_params=_cparams(("parallel", "parallel")),
        name="inproj",
    )(xx, mod, w_pad)


def _attn_prep_kernel(u_ref, cos_ref, sin_ref, qg_ref, kg_ref, bd_ref, sw_ref, qt_ref, k_ref, vt_ref):
    u = u_ref[0]

    def norm_rope(x, g):
        w = x.shape[1]
        tile = lambda t: jnp.concatenate([t] * (w // LANES), -1)
        ms = _dot01(x * x, bd_ref[:w, :w]) * (1.0 / HEAD_DIM)
        xn = x * lax.rsqrt(ms + QK_EPS) * tile(g)
        sw = _dot01(xn, sw_ref[:w, :w])
        return xn * tile(cos_ref[...]) + sw * tile(sin_ref[...])

    q = norm_rope(u[:, :ATT_WIDTH], qg_ref[...]) * (LOG2E * HEAD_DIM ** -0.5)
    qt = q.T
    kx = norm_rope(u[:, ATT_WIDTH:ATT_WIDTH + ATT_KV_WIDTH], kg_ref[...])
    for g in range(ATT_KV_HEADS):
        base = g * ATT_REP * HEAD_DIM
        qt_ref[0, g, 0] = jnp.concatenate(
            [qt[base + r * HEAD_DIM:base + (r + 1) * HEAD_DIM] for r in range(ATT_REP)], -1).astype(BF16)
        k_ref[0, g] = kx[:, g * HEAD_DIM:(g + 1) * HEAD_DIM].astype(BF16)
    v0 = ATT_WIDTH + ATT_KV_WIDTH
    vt = u[:, v0:v0 + ATT_KV_WIDTH].T
    for g in range(ATT_KV_HEADS):
        vt_ref[0, g] = vt[g * HEAD_DIM:(g + 1) * HEAD_DIM].astype(BF16)


def _attn_prep(ua, cos, sin, qg, kg, bd_att, swap_att, lc):
    b, lt, _ = ua.shape
    tm = TOK_TILE
    nct = lc // tm
    nl = lt // tm - nct
    q_pos = lambda i: jnp.where(i < nct, nl + i, i - nct)
    return pl.pallas_call(
        _attn_prep_kernel,
        grid=(b, lt // tm),
        in_specs=[pl.BlockSpec((1, tm, IN_ATT), lambda bi, i: (bi, i, 0)),
                  pl.BlockSpec((tm, LANES), lambda bi, i: (i, 0)),
                  pl.BlockSpec((tm, LANES), lambda bi, i: (i, 0)),
                  pl.BlockSpec((1, LANES), lambda bi, i: (0, 0)),
                  pl.BlockSpec((1, LANES), lambda bi, i: (0, 0)),
                  pl.BlockSpec((ATT_WIDTH, ATT_WIDTH), lambda bi, i: (0, 0)),
                  pl.BlockSpec((ATT_WIDTH, ATT_WIDTH), lambda bi, i: (0, 0))],
        out_specs=[pl.BlockSpec((1, ATT_KV_HEADS, 1, HEAD_DIM, ATT_REP * tm),
                                lambda bi, i: (bi, 0, q_pos(i), 0, 0)),
                   pl.BlockSpec((1, ATT_KV_HEADS, tm, HEAD_DIM), lambda bi, i: (bi, 0, i, 0)),
                   pl.BlockSpec((1, ATT_KV_HEADS, HEAD_DIM, tm), lambda bi, i: (bi, 0, 0, i))],
        out_shape=[jax.ShapeDtypeStruct((b, ATT_KV_HEADS, lt // tm, HEAD_DIM, ATT_REP * tm), BF16),
                   jax.ShapeDtypeStruct((b, ATT_KV_HEADS, lt, HEAD_DIM), BF16),
                   jax.ShapeDtypeStruct((b, ATT_KV_HEADS, HEAD_DIM, lt), BF16)],
        compiler_params=_cparams(("parallel", "parallel")),
        name="attn_prep",
    )(ua, cos, sin, qg, kg, bd_att, swap_att)


def _flash_kernel(qt_ref, k_ref, vt_ref, o_ref, m_scr, l_scr, acc_scr, *, nk, tq, sub, qn):
    j = pl.program_id(3)

    @pl.when(j == 0)
    def _():
        m_scr[...] = jnp.full(m_scr.shape, -jnp.inf, F32)
        l_scr[...] = jnp.zeros(l_scr.shape, F32)
        acc_scr[...] = jnp.zeros(acc_scr.shape, F32)

    qt = jnp.concatenate([qt_ref[0, 0, t] for t in range(qn)], -1)
    nsub = k_ref.shape[2] // sub
    m = m_scr[...]
    l = l_scr[...]
    acc = acc_scr[...]
    scores = lambda c: jnp.dot(k_ref[0, 0, c * sub:(c + 1) * sub, :], qt, preferred_element_type=F32)
    pv = lambda c, p: jnp.dot(vt_ref[0, 0, :, c * sub:(c + 1) * sub], p, preferred_element_type=F32)
    s_next = scores(0)
    pend = None
    for c in range(nsub):
        s = s_next
        if c + 1 < nsub:
            s_next = scores(c + 1)
        if pend is not None:
            acc = pend[0] * acc + pv(c - 1, pend[1])
        m_new = jnp.maximum(m, jnp.max(s, 0, keepdims=True))
        a = jnp.exp2(m - m_new)
        p = jnp.exp2(s - m_new)
        l = a * l + jnp.sum(p, 0, keepdims=True)
        pend = (a, p.astype(BF16))
        m = m_new
    acc = pend[0] * acc + pv(nsub - 1, pend[1])
    m_scr[...] = m
    l_scr[...] = l
    acc_scr[...] = acc

    @pl.when(j == nk - 1)
    def _():
        o = (acc / l).T
        for t in range(qn):
            o_ref[0, t * tq:(t + 1) * tq, :] = jnp.concatenate(
                [o[(t * ATT_REP + r) * tq:(t * ATT_REP + r + 1) * tq] for r in range(ATT_REP)], -1
            ).astype(o_ref.dtype)


def _flash_bounded_kernel(qt_ref, k_ref, vt_ref, o_ref, l_scr, acc_scr, *, nk, tq, sub, qn):
    j = pl.program_id(3)

    @pl.when(j == 0)
    def _():
        l_scr[...] = jnp.zeros(l_scr.shape, F32)
        acc_scr[...] = jnp.zeros(acc_scr.shape, F32)

    qt = jnp.concatenate([qt_ref[0, 0, t] for t in range(qn)], -1)
    nsub = k_ref.shape[2] // sub

    l = l_scr[...]
    acc = acc_scr[...]
    scores = lambda c: jnp.dot(k_ref[0, 0, c * sub:(c + 1) * sub, :], qt, preferred_element_type=F32)
    s_next = scores(0)
    for c in range(nsub):
        s = s_next
        if c + 1 < nsub:
            s_next = scores(c + 1)
        p = jnp.exp2(s)
        l = l + jnp.sum(p.reshape(sub // SUBLANES, SUBLANES, p.shape[1]), 0)
        acc = acc + jnp.dot(vt_ref[0, 0, :, c * sub:(c + 1) * sub], p.astype(BF16), preferred_element_type=F32)
    l_scr[...] = l
    acc_scr[...] = acc

    @pl.when(j == nk - 1)
    def _():
        o = (acc / jnp.sum(l, 0, keepdims=True)).T
        for t in range(qn):
            o_ref[0, t * tq:(t + 1) * tq, :] = jnp.concatenate(
                [o[(t * ATT_REP + r) * tq:(t * ATT_REP + r + 1) * tq] for r in range(ATT_REP)], -1
            ).astype(o_ref.dtype)


def _flash(qt, k, vt, tq, tk, qn, q_blk0, nq, nk, bounded=False):
    b = qt.shape[0]
    lq = nq * qn * tq
    sub = _pick_tile(tk, (256, 128))
    lanes = qn * ATT_REP * tq
    if bounded:
        body = functools.partial(_flash_bounded_kernel, nk=nk, tq=tq, sub=sub, qn=qn)
        scratch = [pltpu.VMEM((SUBLANES, lanes), F32), pltpu.VMEM((HEAD_DIM, lanes), F32)]
    else:
        body = functools.partial(_flash_kernel, nk=nk, tq=tq, sub=sub, qn=qn)
        scratch = [pltpu.VMEM((1, lanes), F32), pltpu.VMEM((1, lanes), F32), pltpu.VMEM((HEAD_DIM, lanes), F32)]
    return pl.pallas_call(
        body,
        grid=(b, ATT_KV_HEADS, nq, nk),
        in_specs=[pl.BlockSpec((1, 1, qn, HEAD_DIM, ATT_REP * tq), lambda bi, g, i, j: (bi, g, i + q_blk0, 0, 0)),
                  pl.BlockSpec((1, 1, tk, HEAD_DIM), lambda bi, g, i, j: (bi, g, j, 0)),
                  pl.BlockSpec((1, 1, HEAD_DIM, tk), lambda bi, g, i, j: (bi, g, 0, j))],
        out_specs=pl.BlockSpec((1, qn * tq, ATT_REP * HEAD_DIM), lambda bi, g, i, j: (bi, i, g)),
        out_shape=jax.ShapeDtypeStruct((b, lq, ATT_WIDTH), BF16),
        scratch_shapes=scratch,
        compiler_params=_cparams(("parallel", "parallel", "parallel", "arbitrary")),
        name="flash_bounded" if bounded else "flash",
    )(qt, k, vt)


def _prev_next(u, up8, un8, i, tm, lc, lt):
    start = i * tm
    p_ok = jnp.logical_and(start != 0, start != lc)
    n_ok = jnp.logical_and(start + tm != lc, start + tm != lt)
    prow = jnp.where(p_ok, up8[SUBLANES - 1:SUBLANES], 0.0)
    nrow = jnp.where(n_ok, un8[0:1], 0.0)
    rid = lax.broadcasted_iota(jnp.int32, u.shape, 0)
    prev = jnp.where(rid == 0, prow, pltpu.roll(u, 1, 0))
    nxt = jnp.where(rid == tm - 1, nrow, pltpu.roll(u, tm - 1, 0))
    return prev, nxt


def _halo_specs(tm, c, lt):
    r = tm // SUBLANES
    last = lt // SUBLANES - 1
    return [pl.BlockSpec((1, tm, c), lambda bi, i: (bi, i, 0)),
            pl.BlockSpec((1, SUBLANES, c), lambda bi, i: (bi, jnp.maximum(i * r - 1, 0), 0)),
            pl.BlockSpec((1, SUBLANES, c), lambda bi, i: (bi, jnp.minimum((i + 1) * r, last), 0))]


def _softplus(z):
    return jnp.maximum(z, 0.0) + jnp.log1p(jnp.exp(-jnp.abs(z)))


def _rwkv_prep_kernel(u_ref, up_ref, un_ref, mu_ref, wl_ref, bd_ref, w0_ref, a0_ref, kkw_ref, ka_ref, rk_ref,
                      r_o, v_o, kk_o, g_o, bon_o, lw_o, be_o, kd_o, *, lc, lt, tm):
    i = pl.program_id(1)
    u = u_ref[0]
    prev, nxt = _prev_next(u, up_ref[0], un_ref[0], i, tm, lc, lt)
    us = u + mu_ref[0:1] * (prev - u) + mu_ref[1:2] * (nxt - u)
    c = RWKV_WIDTH
    r = us[:, 0:c]
    k = us[:, c:2 * c]
    v = us[:, 2 * c:3 * c]
    slab = us[:, 3 * c:3 * c + LANES]
    lane = lax.broadcasted_iota(jnp.int32, slab.shape, 1)
    o_a = 2 * W_LORA
    o_g = o_a + 2 * A_LORA
    act = jnp.where(lane < o_a, jnp.tanh(slab),
                    jnp.where(lane < o_g, slab,
                              jnp.where(lane < o_g + G_LORA, jax.nn.sigmoid(slab), 0.0)))
    lo = _dotp(act, wl_ref[...], 3)
    bd = bd_ref[...]
    kk0 = k * kkw_ref[...]
    kk = kk0 * lax.rsqrt(_dot01(kk0 * kk0, bd) + 1e-12)
    r_o[0] = r
    v_o[0] = v
    kk_o[0] = kk
    g_o[0] = lo[:, 4 * c:5 * c]
    bon = jnp.zeros_like(r)
    for d in range(2):
        w_raw = w0_ref[d:d + 1] + lo[:, d * c:(d + 1) * c]
        lw = -jnp.exp(-_softplus(-w_raw) - 0.5)
        a = jax.nn.sigmoid(a0_ref[d:d + 1] + lo[:, (2 + d) * c:(3 + d) * c])
        kd = k * (1.0 + (a - 1.0) * ka_ref[...])
        lw_o[0, d] = lw
        be_o[0, d] = a * kk
        kd_o[0, d] = kd
        bon = bon + r * kd * rk_ref[...]
    bon_o[0] = _dot01(bon, bd) * v


def _rwkv_prep(ur, mu, wl, bd, w0, a0, kkw, ka, rk, lc):
    b, lt, cp = ur.shape
    tm = TOK_TILE
    c = RWKV_WIDTH
    full = lambda shape: pl.BlockSpec(shape, lambda bi, i: (0,) * len(shape))
    tok = pl.BlockSpec((1, tm, c), lambda bi, i: (bi, i, 0))
    tok2 = pl.BlockSpec((1, 2, tm, c), lambda bi, i: (bi, 0, i, 0))
    s1 = jax.ShapeDtypeStruct((b, lt, c), F32)
    s2 = jax.ShapeDtypeStruct((b, 2, lt, c), F32)
    return pl.pallas_call(
        functools.partial(_rwkv_prep_kernel, lc=lc, lt=lt, tm=tm),
        grid=(b, lt // tm),
        in_specs=_halo_specs(tm, cp, lt) + [full((2, cp)), full((LANES, 5 * c)), full((c, c)), full((2, c)),
                                            full((2, c)), full((1, c)), full((1, c)), full((1, c))],
        out_specs=[tok, tok, tok, tok, tok, tok2, tok2, tok2],
        out_shape=[s1, s1, s1, s1, s1, s2, s2, s2],
        compiler_params=_cparams(("parallel", "parallel")),
        name="rwkv_prep",
    )(ur, ur, ur, mu, wl, bd, w0, a0, kkw, ka, rk)


def _mm(a, b):
    return jnp.dot(a, b, precision=HI, preferred_element_type=F32)


_NN = ((1,), (0,))
_NT = ((1,), (1,))
_TN = ((0,), (0,))


def _split2(a):
    hi = a.astype(BF16)
    return hi, (a - hi.astype(F32)).astype(BF16)


def _dotp(a, b, passes, dims=_NN):
    if a.ndim == 3:
        dn = (((dims[0][0] + 1,), (dims[1][0] + 1,)), ((0,), (0,)))
    else:
        dn = (dims, ((), ()))
    dg = lambda p, q: lax.dot_general(p, q, dn, preferred_element_type=F32)
    if passes == 1:
        return dg(a.astype(BF16), b.astype(BF16))
    ah, al = _split2(a)
    bh, bl = _split2(b)
    return dg(ah, bh) + dg(ah, bl) + dg(al, bh)


def _dot01(a, ones):
    ah, al = _split2(a)
    o = ones.astype(BF16)
    return jnp.dot(ah, o, preferred_element_type=F32) + jnp.dot(al, o, preferred_element_type=F32)


P_M = 1
P_INV = 1
P_W = 1
P_Z = 1
P_STATE = 3
P_DFT = 1


def _unit_tri_inv(a_mat, row, col, eye, passes):
    t = a_mat.shape[-1]
    eye_f = eye.astype(F32)
    base = SUBLANES
    same = (row // base) == (col // base)
    n1 = -jnp.where(same, a_mat, 0.0)
    n2 = _dotp(n1, n1, passes)
    n4 = _dotp(n2, n2, passes)
    x = _dotp(_dotp(eye_f + n1, eye_f + n2, passes), eye_f + n4, passes)
    m = base
    while m < t:
        off = jnp.logical_and((row // (2 * m)) == (col // (2 * m)), (row // m) != (col // m))
        x = x - _dotp(x, _dotp(jnp.where(off, a_mat, 0.0), x, passes), passes)
        m *= 2
    return x


def _wkv_kernel(rf_ref, vf_ref, kkf_ref, rb_ref, vb_ref, kkb_ref, lwf_ref, bef_ref, kdf_ref,
                lwb_ref, beb_ref, kdb_ref, yf_ref, yb_ref, h_scr, pq_scr, ry_scr):
    t = WKV_CHUNK
    n = HEAD_DIM
    g = WKV_TILE // WKV_CHUNK
    tt = WKV_TILE
    nh = RWKV_HEADS
    nd = g * nh
    nu = 2 * nd
    orders = (list(range(g)), list(range(g - 1, -1, -1)))

    @pl.when(pl.program_id(1) == 0)
    def _():
        h_scr[...] = jnp.zeros(h_scr.shape, F32)
        pq_scr[...] = jnp.zeros(pq_scr.shape, F32)
        ry_scr[...] = jnp.zeros(ry_scr.shape, F32)

    hm = h_scr[...]
    for p in range(g):
        both = lambda ref: jnp.concatenate([ref[d * nd + p * nh:d * nd + (p + 1) * nh] for d in range(2)], 0)
        ry = both(ry_scr)
        pq = both(pq_scr)
        y = _dotp(ry[:, :, :n], hm, P_STATE) + ry[:, :, n:]
        hm = _dotp(pq[:, :, :n], hm, P_STATE) + pq[:, :, n:]
        for d, y_ref in enumerate((yf_ref, yb_ref)):
            c = orders[d][p]
            y_ref[0, c * t:(c + 1) * t, :] = jnp.concatenate([y[d * nh + h] for h in range(nh)], -1)
    h_scr[...] = hm

    row = lax.broadcasted_iota(jnp.int32, (tt, tt), 0)
    col = lax.broadcasted_iota(jnp.int32, (tt, tt), 1)
    same = (row // t) == (col // t)

    def scaled(d, r_ref, v_ref, kk_ref, lw_ref, be_ref, kd_ref):
        tri = jnp.logical_and(same, (row >= col) if d == 0 else (row <= col))
        sums = jnp.concatenate([jnp.where(tri, 1.0, 0.0), jnp.where(same, 1.0, 0.0)], 0).astype(BF16)
        lw = lw_ref[0, 0]
        l1 = lw.astype(BF16)
        rem = lw - l1.astype(F32)
        l2 = rem.astype(BF16)
        l3 = (rem - l2.astype(F32)).astype(BF16)
        cc = (jnp.dot(sums, l1, preferred_element_type=F32) + jnp.dot(sums, l2, preferred_element_type=F32)
              + jnp.dot(sums, l3, preferred_element_type=F32))
        cum = cc[:tt]
        ctot = cc[tt:]
        e_neg = jnp.exp(-cum)
        e_end = jnp.exp(ctot - cum)
        be = be_ref[0, 0]
        kd = kd_ref[0, 0]

        def units(x, rows=t):
            return jnp.stack([x[orders[d][p] * t:orders[d][p] * t + rows, h * n:(h + 1) * n]
                              for p in range(g) for h in range(nh)], 0)

        return dict(kap=units(kk_ref[0] * jnp.exp(cum - lw)), rt=units(r_ref[0] * jnp.exp(cum)), vh=units(v_ref[0]),
                    bet=units(be * e_neg), kdt=units(kd * e_neg), beh=units(be * e_end), kdh=units(kd * e_end),
                    gend=units(jnp.exp(ctot), 1))

    parts = (scaled(0, rf_ref, vf_ref, kkf_ref, lwf_ref, bef_ref, kdf_ref),
             scaled(1, rb_ref, vb_ref, kkb_ref, lwb_ref, beb_ref, kdb_ref))
    cat = lambda key: jnp.concatenate([parts[0][key], parts[1][key]], 0)
    kap, rt, vh = cat("kap"), cat("rt"), cat("vh")
    r64 = lax.broadcasted_iota(jnp.int32, (t, t), 0)
    c64 = lax.broadcasted_iota(jnp.int32, (t, t), 1)
    eye = r64 == c64
    unit = lax.broadcasted_iota(jnp.int32, (nu, t, t), 0)
    ahead = (r64 - c64)[None] * jnp.where(unit < nd, 1, -1)
    strict = ahead > 0
    incl = ahead >= 0
    m = _dotp(jnp.concatenate([kap, rt], 1), jnp.concatenate([cat("bet"), cat("kdt")], 1), P_M, _NT)
    a_mat = jnp.where(strict, m[:, :t, :t], 0.0)
    b_mat = jnp.where(strict, m[:, :t, t:], 0.0)
    ab_r = jnp.concatenate([jnp.where(incl, m[:, t:, :t], 0.0), jnp.where(incl, m[:, t:, t:], 0.0)], 2)
    tinv = _unit_tri_inv(a_mat, r64, c64, eye, P_INV)
    w = _dotp(tinv, jnp.concatenate([kap, _dotp(b_mat, vh, P_W)], 2), P_W)
    z = jnp.concatenate([-w, jnp.concatenate([jnp.zeros((nu, t, n), F32), vh], 2)], 1)
    ry = _dotp(ab_r, z, P_Z)
    ry_scr[:, :, :n] = ry[:, :, :n] + rt
    ry_scr[:, :, n:] = ry[:, :, n:]
    pq = _dotp(jnp.concatenate([cat("beh"), cat("kdh")], 1), z, P_Z, _TN)
    gd = jnp.where(eye, jnp.broadcast_to(cat("gend"), (nu, n, n)), 0.0)
    pq_scr[:, :, :n] = pq[:, :, :n] + gd
    pq_scr[:, :, n:] = pq[:, :, n:]


def _wkv_scan(r, v, kk, lw, be, kd, lc):
    b, lt, c = r.shape
    tt = WKV_TILE
    nt = lt // tt
    ntc = lc // tt
    tiles = (lambda i: i, lambda i: jnp.where(i < ntc, ntc - 1 - i, nt - 1 - (i - ntc)))
    t_in = lambda d: (lambda i: tiles[d](jnp.minimum(i, nt - 1)))
    t_out = lambda d: (lambda i: tiles[d](jnp.maximum(i - 1, 0)))
    one = lambda d: pl.BlockSpec((1, tt, c), lambda bi, i: (bi, t_in(d)(i), 0))
    two = lambda d: pl.BlockSpec((1, 1, tt, c), lambda bi, i: (bi, d, t_in(d)(i), 0))
    out = lambda d: pl.BlockSpec((1, tt, c), lambda bi, i: (bi, t_out(d)(i), 0))
    nu = 2 * (tt // WKV_CHUNK) * RWKV_HEADS
    ysh = jax.ShapeDtypeStruct((b, lt, c), F32)
    return pl.pallas_call(
        _wkv_kernel,
        grid=(b, nt + 1),
        in_specs=[one(0)] * 3 + [one(1)] * 3 + [two(0)] * 3 + [two(1)] * 3,
        out_specs=[out(0), out(1)],
        out_shape=[ysh, ysh],
        scratch_shapes=[pltpu.VMEM((2 * RWKV_HEADS, HEAD_DIM, HEAD_DIM), F32),
                        pltpu.VMEM((nu, HEAD_DIM, 2 * HEAD_DIM), F32),
                        pltpu.VMEM((nu, WKV_CHUNK, 2 * HEAD_DIM), F32)],
        compiler_params=_cparams(("parallel", "arbitrary")),
        name="wkv_scan",
    )(r, v, kk, r, v, kk, lw, be, kd, lw, be, kd)


def _rwkv_out_kernel(yf_ref, yb_ref, bon_ref, g_ref, bd_ref, gg_ref, gb_ref, o_ref):
    y = yf_ref[0] + yb_ref[0] + bon_ref[0]
    bd = bd_ref[...]
    mu = _dot01(y, bd) * (1.0 / HEAD_DIM)
    yc = y - mu
    var = _dot01(yc * yc, bd) * (1.0 / HEAD_DIM)
    yn = yc * lax.rsqrt(var + RWKV_GN_EPS) * gg_ref[...] + gb_ref[...]
    o_ref[0] = (yn * g_ref[0]).astype(o_ref.dtype)


def _rwkv_out(yf, yb, bon, g, bd, gg, gb):
    b, lt, c = yf.shape
    tm = TOK_TILE
    tok = pl.BlockSpec((1, tm, c), lambda bi, i: (bi, i, 0))
    full = lambda shape: pl.BlockSpec(shape, lambda bi, i: (0,) * len(shape))
    return pl.pallas_call(
        _rwkv_out_kernel,
        grid=(b, lt // tm),
        in_specs=[tok, tok, tok, tok, full((c, c)), full((1, c)), full((1, c))],
        out_specs=tok,
        out_shape=jax.ShapeDtypeStruct((b, lt, c), BF16),
        compiler_params=_cparams(("parallel", "parallel")),
        name="rwkv_out",
    )(yf, yb, bon, g, bd, gg, gb)


def _hy_prep_kernel(u_ref, up_ref, un_ref, w_ref, b_ref, x1_o, x2_o, v_o, *, lc, lt, tm):
    i = pl.program_id(1)
    u = u_ref[0]
    prev, nxt = _prev_next(u, up_ref[0], un_ref[0], i, tm, lc, lt)
    y = prev * w_ref[0:1] + u * w_ref[1:2] + nxt * w_ref[2:3] + b_ref[...]
    c = HY_WIDTH
    x1_o[0] = y[:, :c]
    x2_o[0] = y[:, c:2 * c]
    v_o[0] = y[:, 2 * c:]


def _hy_prep(uh, w, bias, lc):
    b, lt, cin = uh.shape
    tm = TOK_TILE
    c = HY_WIDTH
    full = lambda shape: pl.BlockSpec(shape, lambda bi, i: (0,) * len(shape))
    tok = pl.BlockSpec((1, tm, c), lambda bi, i: (bi, i, 0))
    s1 = jax.ShapeDtypeStruct((b, lt, c), F32)
    return pl.pallas_call(
        functools.partial(_hy_prep_kernel, lc=lc, lt=lt, tm=tm),
        grid=(b, lt // tm),
        in_specs=_halo_specs(tm, cin, lt) + [full((3, cin)), full((1, cin))],
        out_specs=[tok, tok, tok],
        out_shape=[s1, s1, s1],
        compiler_params=_cparams(("parallel", "parallel")),
        name="hy_prep",
    )(uh, uh, uh, w, bias)


def _hy_filter_kernel(z_ref, w1_ref, b1_ref, f1_ref, w2_ref, b2_ref, f2_ref, w3_ref, dec_ref,
                      fw_o, bw_o, nrm_o, *, tl):
    i = pl.program_id(0)
    z = z_ref[...]
    h = jnp.sin(f1_ref[...] * (_dotp(z, w1_ref[...], 3) + b1_ref[...]))
    h = jnp.sin(f2_ref[...] * (_dotp(h, w2_ref[...], 3) + b2_ref[...]))
    h = _dotp(h, w3_ref[...], 3) * jnp.exp(-z[:, 0:1] * dec_ref[...])
    c = HY_WIDTH
    fw = jnp.concatenate([h[:, 0:c], h[:, 2 * c:3 * c]], 1)
    bw = jnp.concatenate([h[:, c:2 * c], h[:, 3 * c:4 * c]], 1)
    rid = i * tl + lax.broadcasted_iota(jnp.int32, (tl, 1), 0)
    bw = jnp.where(rid == 0, 0.0, bw)
    fw_o[...] = fw
    bw_o[...] = bw

    @pl.when(i == 0)
    def _():
        nrm_o[...] = jnp.zeros(nrm_o.shape, F32)

    nrm_o[...] += jnp.sum(jnp.abs(fw) + jnp.abs(bw), 0, keepdims=True)


def _hy_filter(feat, w1p, b1, f1, w2, b2, f2, w3, dec):
    l, fe = feat.shape
    tl = _pick_tile(l, (512, 256))
    c2 = HY_ORDER * HY_WIDTH
    full = lambda shape: pl.BlockSpec(shape, lambda i: (0,) * len(shape))
    return pl.pallas_call(
        functools.partial(_hy_filter_kernel, tl=tl),
        grid=(l // tl,),
        in_specs=[pl.BlockSpec((tl, fe), lambda i: (i, 0)), full(w1p.shape), full(b1.shape), full(f1.shape),
                  full(w2.shape), full(b2.shape), full(f2.shape), full(w3.shape), full(dec.shape)],
        out_specs=[pl.BlockSpec((tl, c2), lambda i: (i, 0)), pl.BlockSpec((tl, c2), lambda i: (i, 0)),
                   pl.BlockSpec((1, c2), lambda i: (0, 0))],
        out_shape=[jax.ShapeDtypeStruct((l, c2), F32), jax.ShapeDtypeStruct((l, c2), F32),
                   jax.ShapeDtypeStruct((1, c2), F32)],
        compiler_params=_cparams(("arbitrary",)),
        name="hy_filter",
    )(feat, w1p, b1, f1, w2, b2, f2, w3, dec)


def _dft_cols_kernel(f_ref, xa_ref, xb_ref, o_ref, *, n1, pair):
    f = f_ref[...]
    pa = _dotp(f, xa_ref[0], P_DFT)
    pb = _dotp(f, xb_ref[0], P_DFT)
    if pair:
        o_ref[0] = pa[:n1].astype(o_ref.dtype)
        o_ref[1] = pa[n1:].astype(o_ref.dtype)
        o_ref[2] = pb[:n1].astype(o_ref.dtype)
        o_ref[3] = pb[n1:].astype(o_ref.dtype)
    else:
        o_ref[0] = (pa[:n1] - pb[n1:]).astype(o_ref.dtype)
        o_ref[1] = (pb[:n1] + pa[n1:]).astype(o_ref.dtype)


def _dft_cols(fstack, xa, ia, xb, ib, pair):
    n1 = fstack.shape[0] // 2
    _, nh, cols = xa.shape
    tc = _pick_tile(cols, (4096, 2048, 1024, 512, 256, 128))
    no = 4 if pair else 2
    return pl.pallas_call(
        functools.partial(_dft_cols_kernel, n1=n1, pair=pair),
        grid=(cols // tc,),
        in_specs=[pl.BlockSpec(fstack.shape, lambda j: (0, 0)),
                  pl.BlockSpec((1, nh, tc), lambda j: (ia, 0, j)),
                  pl.BlockSpec((1, nh, tc), lambda j: (ib, 0, j))],
        out_specs=pl.BlockSpec((no, n1, tc), lambda j: (0, 0, j)),
        out_shape=jax.ShapeDtypeStruct((no, n1, cols), BF16),
        compiler_params=_cparams(("parallel",)),
        name="dft_cols",
    )(fstack, xa, xb)


def _cplx_left(gs, zr, zi, n):
    c = zr.shape[1]
    p = _dotp(gs, jnp.concatenate([zr, zi], 1), P_DFT)
    return p[:n, :c] - p[n:, c:], p[:n, c:] + p[n:, :c]


def _spec_kernel(a_ref, g_ref, nrm_ref, o_ref, *, n_total):
    n2 = DFT_N2
    gs = jnp.concatenate([g_ref[0, 0], g_ref[0, 1]], 0)
    fr, fi = _cplx_left(gs, a_ref[0, 0], a_ref[1, 0], n2)
    br, bi = _cplx_left(gs, a_ref[2, 0], a_ref[3, 0], n2)
    s = 1.0 / (nrm_ref[...] * n_total)
    o_ref[0, 0] = ((fr + br) * s).astype(o_ref.dtype)
    o_ref[1, 0] = ((fi - bi) * s).astype(o_ref.dtype)


def _spec(a4, g, nrm, n_total):
    _, n1, n2, c2 = a4.shape
    return pl.pallas_call(
        functools.partial(_spec_kernel, n_total=float(n_total)),
        grid=(n1,),
        in_specs=[pl.BlockSpec((4, 1, n2, c2), lambda k: (0, k, 0, 0)),
                  pl.BlockSpec((1, 2, n2, n2), lambda k: (k, 0, 0, 0)),
                  pl.BlockSpec((1, c2), lambda k: (0, 0))],
        out_specs=pl.BlockSpec((2, 1, n2, c2), lambda k: (0, k, 0, 0)),
        out_shape=jax.ShapeDtypeStruct((2, n1, n2, c2), BF16),
        compiler_params=_cparams(("parallel",)),
        name="hy_spec",
    )(a4, g, nrm)


def _conv_mid_kernel(a_ref, g_ref, k_ref, o_ref):
    n2 = DFT_N2
    gs = jnp.concatenate([g_ref[0, 0], g_ref[0, 1]], 0)
    xr, xi = _cplx_left(gs, a_ref[0, 0], a_ref[1, 0], n2)
    c = xr.shape[1]
    kr = k_ref[0, 0].astype(F32)
    ki = k_ref[1, 0].astype(F32)
    zr = xr * kr - xi * ki
    zi = xr * ki + xi * kr
    zst = jnp.concatenate([jnp.concatenate([zr, zi], 1), jnp.concatenate([zi, -zr], 1)], 0)
    y = _dotp(gs, zst, P_DFT, _TN)
    o_ref[0, 0] = y[:, :c].astype(o_ref.dtype)
    o_ref[1, 0] = y[:, c:].astype(o_ref.dtype)


def _conv_mid(a, g, kspec, order):
    _, n1, n2, c = a.shape
    return pl.pallas_call(
        _conv_mid_kernel,
        grid=(n1,),
        in_specs=[pl.BlockSpec((2, 1, n2, c), lambda k: (0, k, 0, 0)),
                  pl.BlockSpec((1, 2, n2, n2), lambda k: (k, 0, 0, 0)),
                  pl.BlockSpec((2, 1, n2, c), lambda k: (0, k, 0, order))],
        out_specs=pl.BlockSpec((2, 1, n2, c), lambda k: (0, k, 0, 0)),
        out_shape=jax.ShapeDtypeStruct((2, n1, n2, c), BF16),
        compiler_params=_cparams(("parallel",)),
        name="hy_conv_mid",
    )(a, g, kspec)


def _idft_cols_kernel(c_ref, b_ref, g0_ref, g1_ref, x0_ref, x1_ref, bias_ref, o_ref, *, nh):
    cs = c_ref[...]
    pr = _dotp(cs, b_ref[0], P_DFT)
    pi = _dotp(cs, b_ref[1], P_DFT)
    yr = pr[:nh] - pi[nh:]
    yi = pi[:nh] + pr[nh:]
    bias = bias_ref[...]
    o_ref[0] = g0_ref[0] * (yr + x0_ref[0] * bias)
    o_ref[1] = g1_ref[0] * (yi + x1_ref[0] * bias)


def _idft_cols(cstack, bv, gate, xin, bias_cols):
    nh2, n1 = cstack.shape
    nh = nh2 // 2
    cols = bv.shape[-1]
    tc = _pick_tile(cols, (4096, 2048, 1024, 512, 256, 128))
    row = lambda bi: pl.BlockSpec((1, nh, tc), lambda j: (bi, 0, j))
    return pl.pallas_call(
        functools.partial(_idft_cols_kernel, nh=nh),
        grid=(cols // tc,),
        in_specs=[pl.BlockSpec((nh2, n1), lambda j: (0, 0)),
                  pl.BlockSpec((2, n1, tc), lambda j: (0, 0, j)),
                  row(0), row(1), row(0), row(1),
                  pl.BlockSpec((1, tc), lambda j: (0, j))],
        out_specs=pl.BlockSpec((2, nh, tc), lambda j: (0, 0, j)),
        out_shape=jax.ShapeDtypeStruct((2, nh, cols), F32),
        compiler_params=_cparams(("parallel",)),
        name="idft_cols",
    )(cstack, bv, gate, gate, xin, xin, bias_cols)


def _hy_ctx_kernel(x1_ref, x2_ref, v_ref, fw_ref, bw_ref, nrm_ref, bias_ref, ff_ref, ci_ref, o_ref, *, lc):
    n = 2 * lc
    c = HY_WIDTH
    ff = ff_ref[...]
    ci = ci_ref[...]
    pf = _mm(ff, fw_ref[...])
    pb = _mm(ff, bw_ref[...])
    s = 1.0 / (nrm_ref[...] * float(n))
    kr = (pf[:n] + pb[:n]) * s
    ki = (pf[n:] - pb[n:]) * s

    def conv(z0, z1, o):
        xr, xi = _cplx_left(ff, z0, z1, n)
        krr = kr[:, o * c:(o + 1) * c]
        kii = ki[:, o * c:(o + 1) * c]
        return _cplx_left(ci, xr * krr - xi * kii, xr * kii + xi * krr, lc)

    v0 = v_ref[0]
    v1 = v_ref[1]
    y0, y1 = conv(v0, v1, 0)
    z0 = x1_ref[0] * (y0 + v0 * bias_ref[0:1])
    z1 = x1_ref[1] * (y1 + v1 * bias_ref[0:1])
    y0, y1 = conv(z0, z1, 1)
    o_ref[0] = x2_ref[0] * (y0 + z0 * bias_ref[1:2])
    o_ref[1] = x2_ref[1] * (y1 + z1 * bias_ref[1:2])


def _hy_ctx(x1, x2, v, fw, bw, nrm, bias, ff, ci):
    b, lc, c = v.shape
    vm = pl.BlockSpec(memory_space=pltpu.VMEM)
    return pl.pallas_call(
        functools.partial(_hy_ctx_kernel, lc=lc),
        in_specs=[vm] * 9,
        out_specs=vm,
        out_shape=jax.ShapeDtypeStruct((b, lc, c), F32),
        compiler_params=pltpu.CompilerParams(vmem_limit_bytes=VMEM_LIMIT),
        name="hy_ctx",
    )(x1, x2, v, fw, bw, nrm, bias, ff, ci)


def _outproj_kernel(attc_ref, attl_ref, rw_ref, hyc_ref, hyl_ref, x_ref, mod_ref, w_ref, lg_ref, lb_ref, o_ref,
                    *, lc, tm):
    row0 = pl.program_id(1) * tm
    a0 = ATT_WIDTH
    a1 = ATT_WIDTH + RWKV_WIDTH
    is_ctx = row0 < lc
    att = jnp.where(is_ctx, attc_ref[0], attl_ref[0])
    hy = jnp.where(is_ctx, hyc_ref[0], hyl_ref[0])
    o = jnp.dot(att, w_ref[:a0], preferred_element_type=F32)
    o += jnp.dot(rw_ref[0], w_ref[a0:a1], preferred_element_type=F32)
    o += jnp.dot(hy.astype(BF16), w_ref[a1:], preferred_element_type=F32)
    g = _sel_mod(mod_ref, 2, row0, tm, lc)
    y = ALPHA * x_ref[0] + g * o
    o_ref[0] = _layer_norm(y) * lg_ref[...] + lb_ref[...]


def _outproj(att_c, att_l, rw, hy_c, hy_l, xx, mod, w, lg, lb, lc):
    b, lt, d = xx.shape
    tm = TOK_TILE
    nct = lc // tm
    tok = lambda c: pl.BlockSpec((1, tm, c), lambda bi, i: (bi, i, 0))
    ctx = lambda c: pl.BlockSpec((1, tm, c), lambda bi, i: (bi, jnp.minimum(i, nct - 1), 0))
    lat = lambda c: pl.BlockSpec((1, tm, c), lambda bi, i: (bi, jnp.maximum(i - nct, 0), 0))
    full = lambda shape: pl.BlockSpec(shape, lambda bi, i: (0,) * len(shape))
    return pl.pallas_call(
        functools.partial(_outproj_kernel, lc=lc, tm=tm),
        grid=(b, lt // tm),
        in_specs=[ctx(ATT_WIDTH), lat(ATT_WIDTH), tok(RWKV_WIDTH), ctx(HY_WIDTH), lat(HY_WIDTH), tok(d),
                  pl.BlockSpec((1, 2, 6, d), lambda bi, i: (bi, 0, 0, 0)),
                  full(w.shape), full((1, d)), full((1, d))],
        out_specs=tok(d),
        out_shape=jax.ShapeDtypeStruct((b, lt, d), F32),
        compiler_params=_cparams(("parallel", "parallel")),
        name="outproj",
    )(att_c, att_l, rw, hy_c, hy_l, xx, mod, w, lg, lb)


def _ffn_kernel(x_ref, mod_ref, w1_ref, w3_ref, w2_ref, lg_ref, lb_ref, o_ref, h_scr, acc_scr, *, lc, tm, nf):
    row0 = pl.program_id(1) * tm
    f = pl.program_id(2)

    @pl.when(f == 0)
    def _():
        sh = _sel_mod(mod_ref, 3, row0, tm, lc)
        sc = _sel_mod(mod_ref, 4, row0, tm, lc)
        h_scr[...] = (_layer_norm(x_ref[0]) * (1.0 + sc) + sh).astype(BF16)
        acc_scr[...] = jnp.zeros(acc_scr.shape, F32)

    h = h_scr[...]
    a = jnp.dot(h, w1_ref[...], preferred_element_type=F32)
    g = jnp.dot(h, w3_ref[...], preferred_element_type=F32)
    acc_scr[...] += jnp.dot((_silu(a) * g).astype(BF16), w2_ref[...], preferred_element_type=F32)

    @pl.when(f == nf - 1)
    def _():
        gate = _sel_mod(mod_ref, 5, row0, tm, lc)
        y = ALPHA * x_ref[0] + gate * acc_scr[...]
        o_ref[0] = _layer_norm(y) * lg_ref[...] + lb_ref[...]


def _ffn(xx, mod, w1, w3, w2, lg, lb, lc):
    b, lt, d = xx.shape
    ff = w1.shape[1]
    tm = _pick_tile(lt, (1280, 768, 512, 256))
    tf = _pick_tile(ff, (256, 128))
    nf = ff // tf
    return pl.pallas_call(
        functools.partial(_ffn_kernel, lc=lc, tm=tm, nf=nf),
        grid=(b, lt // tm, nf),
        in_specs=[pl.BlockSpec((1, tm, d), lambda bi, i, f: (bi, i, 0)),
                  pl.BlockSpec((1, 2, 6, d), lambda bi, i, f: (bi, 0, 0, 0)),
                  pl.BlockSpec((d, tf), lambda bi, i, f: (0, f)),
                  pl.BlockSpec((d, tf), lambda bi, i, f: (0, f)),
                  pl.BlockSpec((tf, d), lambda bi, i, f: (f, 0)),
                  pl.BlockSpec((1, d), lambda bi, i, f: (0, 0)),
                  pl.BlockSpec((1, d), lambda bi, i, f: (0, 0))],
        out_specs=pl.BlockSpec((1, tm, d), lambda bi, i, f: (bi, i, 0)),
        out_shape=jax.ShapeDtypeStruct((b, lt, d), F32),
        scratch_shapes=[pltpu.VMEM((tm, d), BF16), pltpu.VMEM((tm, d), F32)],
        compiler_params=_cparams(("parallel", "parallel", "arbitrary")),
        name="ffn",
    )(xx, mod, w1, w3, w2, lg, lb)


def _route(x_ref, mod_ref, wr_ref, row0, tm, lc):
    sh = _sel_mod(mod_ref, 3, row0, tm, lc)
    sc = _sel_mod(mod_ref, 4, row0, tm, lc)
    h = _layer_norm(x_ref[0]) * (1.0 + sc) + sh
    logits = _dotp(h, wr_ref[...], 3)
    lane = lax.broadcasted_iota(jnp.int32, logits.shape, 1)
    neg = jnp.float32(-jnp.inf)
    lg = jnp.where(lane < N_EXPERTS, logits, neg)
    m1 = jnp.max(lg, -1, keepdims=True)
    i1 = jnp.min(jnp.where(lg == m1, lane, LANES), -1, keepdims=True)
    lg2 = jnp.where(lane == i1, neg, lg)
    m2 = jnp.max(lg2, -1, keepdims=True)
    i2 = jnp.min(jnp.where(lg2 == m2, lane, LANES), -1, keepdims=True)
    e2 = jnp.exp(m2 - m1)
    return h, lane, i1, i2, 1.0 / (1.0 + e2), e2 / (1.0 + e2)


def _moe_sparse_kernel(x_ref, mod_ref, wr_ref, w1_ref, w3_ref, w2_ref, lg_ref, lb_ref, o_ref,
                       h_scr, acc_scr, gate_scr, posc_scr, posr_scr, xs_scr, ye_scr, nblk_scr, *, lc, tm, nf, ns, blk):
    row0 = pl.program_id(1) * tm
    s = pl.program_id(2)
    e = s // nf
    f = s % nf

    @pl.when(s == 0)
    def _():
        h, lane, i1, i2, g1, g2 = _route(x_ref, mod_ref, wr_ref, row0, tm, lc)
        h_scr[...] = h.astype(BF16)
        acc_scr[...] = jnp.zeros(acc_scr.shape, F32)
        routed = jnp.where(jnp.logical_or(lane == i1, lane == i2), 1.0, 0.0)
        r = lax.broadcasted_iota(jnp.int32, (tm, tm), 0)
        c = lax.broadcasted_iota(jnp.int32, (tm, tm), 1)
        before = jnp.where(c < r, 1.0, 0.0).astype(BF16)
        rank_c = jnp.dot(before, routed.astype(BF16), preferred_element_type=F32)
        gate_scr[...] = jnp.where(lane == i1, g1, 0.0) + jnp.where(lane == i2, g2, 0.0)
        posc_scr[...] = jnp.where(routed > 0.0, rank_c, -1.0)
        routed_t = routed.T[:2 * SUBLANES]
        after = jnp.where(r < c, 1.0, 0.0).astype(BF16)
        rank_r = jnp.dot(routed_t.astype(BF16), after, preferred_element_type=F32)
        posr_scr[...] = jnp.where(routed_t > 0.0, rank_r, -1.0)
        counts = jnp.sum(routed, 0, keepdims=True)
        lane1 = lax.broadcasted_iota(jnp.int32, counts.shape, 1)
        for ex in range(N_EXPERTS):
            n_rows = jnp.sum(jnp.where(lane1 == ex, counts, 0.0)).astype(jnp.int32)
            nblk_scr[ex] = (n_rows + (blk - 1)) // blk

    nblk = nblk_scr[e]
    rows = lambda i: pl.ds(pl.multiple_of(i * blk, blk), blk)

    @pl.when(f == 0)
    def _():
        def gather(i, carry):
            slot = (lax.broadcasted_iota(jnp.int32, (blk, tm), 0) + i * blk).astype(F32)
            take = jnp.where(slot == posr_scr[pl.ds(e, 1), :], 1.0, 0.0).astype(BF16)
            xs_scr[rows(i), :] = jnp.dot(take, h_scr[...], preferred_element_type=F32).astype(BF16)
            ye_scr[rows(i), :] = jnp.zeros((blk, ye_scr.shape[1]), F32)
            return carry

        lax.fori_loop(0, nblk, gather, 0)

    def expert(m):
        xs = xs_scr[:m]
        a = jnp.dot(xs, w1_ref[0], preferred_element_type=F32)
        g = jnp.dot(xs, w3_ref[0], preferred_element_type=F32)
        ye_scr[:m] += jnp.dot((_silu(a) * g).astype(BF16), w2_ref[0], preferred_element_type=F32)

    for k in range(tm // blk):
        pl.when(nblk == k + 1)(functools.partial(expert, (k + 1) * blk))

    @pl.when(f == nf - 1)
    def _():
        lane = lax.broadcasted_iota(jnp.int32, (tm, LANES), 1)
        column = lambda ref: jnp.sum(jnp.where(lane == e, ref[...], 0.0), -1, keepdims=True)
        pos = column(posc_scr)
        gate = column(gate_scr)

        def scatter(i, carry):
            slot = (lax.broadcasted_iota(jnp.int32, (tm, blk), 1) + i * blk).astype(F32)
            put = jnp.where(slot == pos, 1.0, 0.0).astype(BF16)
            acc_scr[...] += gate * jnp.dot(put, ye_scr[rows(i), :].astype(BF16), preferred_element_type=F32)
            return carry

        lax.fori_loop(0, nblk, scatter, 0)

    @pl.when(s == ns - 1)
    def _():
        gate = _sel_mod(mod_ref, 5, row0, tm, lc)
        y = ALPHA * x_ref[0] + gate * acc_scr[...]
        o_ref[0] = _layer_norm(y) * lg_ref[...] + lb_ref[...]


def _moe_sparse(xx, mod, wr, w1, w3, w2, lg, lb, lc):
    b, lt, d = xx.shape
    ne, _, ff = w1.shape
    tm = _pick_tile(lt, (1280, 768, 512, 256))
    tf = _pick_tile(ff, (256, 128))
    nf = ff // tf
    ns = ne * nf
    return pl.pallas_call(
        functools.partial(_moe_sparse_kernel, lc=lc, tm=tm, nf=nf, ns=ns, blk=MOE_ROW_BLOCK),
        grid=(b, lt // tm, ns),
        in_specs=[pl.BlockSpec((1, tm, d), lambda bi, i, s: (bi, i, 0)),
                  pl.BlockSpec((1, 2, 6, d), lambda bi, i, s: (bi, 0, 0, 0)),
                  pl.BlockSpec((d, LANES), lambda bi, i, s: (0, 0)),
                  pl.BlockSpec((1, d, tf), lambda bi, i, s: (s // nf, 0, s % nf)),
                  pl.BlockSpec((1, d, tf), lambda bi, i, s: (s // nf, 0, s % nf)),
                  pl.BlockSpec((1, tf, d), lambda bi, i, s: (s // nf, s % nf, 0)),
                  pl.BlockSpec((1, d), lambda bi, i, s: (0, 0)),
                  pl.BlockSpec((1, d), lambda bi, i, s: (0, 0))],
        out_specs=pl.BlockSpec((1, tm, d), lambda bi, i, s: (bi, i, 0)),
        out_shape=jax.ShapeDtypeStruct((b, lt, d), F32),
        scratch_shapes=[pltpu.VMEM((tm, d), BF16), pltpu.VMEM((tm, d), F32),
                        pltpu.VMEM((tm, LANES), F32), pltpu.VMEM((tm, LANES), F32),
                        pltpu.VMEM((2 * SUBLANES, tm), F32),
                        pltpu.VMEM((tm, d), BF16), pltpu.VMEM((tm, d), F32),
                        pltpu.SMEM((ne,), jnp.int32)],
        compiler_params=_cparams(("parallel", "parallel", "arbitrary"), VMEM_LIMIT_MOE),
        name="moe_sparse",
    )(xx, mod, wr, w1, w3, w2, lg, lb)


def _rope_tables(l, lc):
    rows = l // GRID_W
    row = jnp.repeat(jnp.arange(rows, dtype=F32), GRID_W)
    col = jnp.tile(jnp.arange(GRID_W, dtype=F32), rows)
    n_freq = HEAD_DIM // 4
    inv_freq = ROPE_THETA ** (-jnp.arange(n_freq, dtype=F32) / n_freq)
    ang = jnp.concatenate([row[:, None] * inv_freq, col[:, None] * inv_freq], -1)
    cos, sin = jnp.cos(ang), jnp.sin(ang)
    cos2 = jnp.concatenate([jnp.ones((lc, LANES), F32), jnp.concatenate([cos, cos, cos, cos], -1)], 0)
    sin2 = jnp.concatenate([jnp.zeros((lc, LANES), F32), jnp.concatenate([-sin, sin, -sin, sin], -1)], 0)
    return cos2, sin2


def _hy_features(l):
    bands = (HY_EMB - 1) // 2
    t = jnp.linspace(0.0, 1.0, l, dtype=F32)[:, None]
    f = jnp.linspace(1e-4, bands - 1, bands, dtype=F32)[None, :]
    wt = 2.0 * math.pi * jnp.arange(l, dtype=F32)[:, None] / l
    z = jnp.concatenate([t, jnp.cos(f * wt), -jnp.sin(f * wt)], -1)
    return jnp.pad(z, ((0, 0), (0, LANES - HY_EMB)))


def _angle(idx, n):
    return (2.0 * math.pi / n) * (idx % n).astype(F32)


def _dft_tables(n1):
    nh = n1 // 2
    n2 = DFT_N2
    n = n1 * n2
    k1 = jnp.arange(n1, dtype=jnp.int32)
    a1 = _angle(k1[:, None] * jnp.arange(nh, dtype=jnp.int32)[None, :], n1)
    fstack = jnp.concatenate([jnp.cos(a1), -jnp.sin(a1)], 0)
    cstack = jnp.concatenate([jnp.cos(a1.T), jnp.sin(a1.T)], 0)
    k2 = jnp.arange(n2, dtype=jnp.int32)
    at = _angle(k1[:, None] * k2[None, :], n)
    tr, ti = jnp.cos(at), -jnp.sin(at)
    a2 = _angle(k2[:, None] * k2[None, :], n2)
    fr, fi = jnp.cos(a2), -jnp.sin(a2)
    g = jnp.stack([tr[:, None, :] * fr[None] - ti[:, None, :] * fi[None],
                   tr[:, None, :] * fi[None] + ti[:, None, :] * fr[None]], 1)
    return fstack, cstack, g


def _dense_dft_tables(lc):
    n = 2 * lc
    a = _angle(jnp.arange(n, dtype=jnp.int32)[:, None] * jnp.arange(lc, dtype=jnp.int32)[None, :], n)
    ff = jnp.concatenate([jnp.cos(a), -jnp.sin(a)], 0)
    ci = jnp.concatenate([jnp.cos(a.T), jnp.sin(a.T)], 0)
    return ff, ci


def kernel(x, c, ctx, c_ctx, ada_w, ada_b, w_in, w_out, q_gain, k_gain, rwkv_mu, rwkv_w0, rwkv_wB, rwkv_a0, rwkv_aB, rwkv_gB, rwkv_kk, rwkv_ka, rwkv_rk, rwkv_gn_g, rwkv_gn_b, hy_short_w, hy_short_b, hy_w1, hy_b1, hy_freq1, hy_w2, hy_b2, hy_freq2, hy_w3, hy_decay, hy_bias, ln1_g, ln1_b, ln2_g, ln2_b, ffn_w1, ffn_w3, ffn_w2, moe_router, moe_w1, moe_w3, moe_w2):
    b, l, d = x.shape
    lc = ctx.shape[1]
    lt = lc + l
    depth = ada_w.shape[0]
    assert b == 2, "the long convolution packs the two batch rows as one complex signal"
    assert d == D_MODEL and lc % TOK_TILE == 0 and l % TOK_TILE == 0 and (2 * l) % (2 * DFT_N2) == 0
    cw = RWKV_WIDTH

    xx = jnp.concatenate([ctx, x], 1)
    cond8 = jnp.zeros((SUBLANES, d), F32).at[:b].set(c).at[b].set(c_ctx)
    mod_all = _ada_mod(cond8, ada_w, ada_b)

    cos64, sin64 = _rope_tables(l, lc)
    n1 = 2 * l // DFT_N2
    nh = n1 // 2
    cols = DFT_N2 * HY_WIDTH
    fstack, cstack, g_tab = (t.astype(BF16) for t in _dft_tables(n1))
    ff_c, ci_c = _dense_dft_tables(lc)
    feat_l = _hy_features(l)
    feat_c = _hy_features(lc)
    blk = jnp.arange(cw) // HEAD_DIM
    bd = (blk[:, None] == blk[None, :]).astype(F32)
    ch = jnp.arange(ATT_WIDTH)
    bd_att = (ch[:, None] // HEAD_DIM == ch[None, :] // HEAD_DIM).astype(BF16)
    swap_att = (ch[:, None] == (ch[None, :] + HALF_HD) % HEAD_DIM + (ch[None, :] // HEAD_DIM) * HEAD_DIM
                ).astype(BF16)
    perm64 = jnp.concatenate([jnp.arange(0, HEAD_DIM, 2), jnp.arange(1, HEAD_DIM, 2)])
    perm_att = jnp.concatenate([h * HEAD_DIM + perm64 for h in range(ATT_HEADS + ATT_KV_HEADS)]
                               + [jnp.arange(ATT_WIDTH + ATT_KV_WIDTH, IN_ATT)])
    tq = TOK_TILE
    qn = next(n for n in (4, 2, 1) if (l // tq) % n == 0)
    tk = _pick_tile(lt, (3328, 1280, 1024, 768, 512, 256))

    for li in range(depth):
        ml = mod_all[li]
        mod = jnp.stack([jnp.broadcast_to(ml[b].reshape(1, 6, d), (b, 6, d)), ml[:b].reshape(b, 6, d)], 1)
        wi = w_in[li]
        w_pad = jnp.concatenate([wi[:, :IN_ATT][:, perm_att], wi[:, IN_ATT:IN_ATT + IN_RWKV],
                                 jnp.zeros((d, IN_RWKV_PAD - IN_RWKV), F32), wi[:, IN_ATT + IN_RWKV:]],
                                1).astype(BF16)
        ua, ur, uh = _inproj(xx, mod, w_pad, lc)

        two = lambda gain: jnp.tile(gain[perm64], 2)[None]
        qt, kx, vt = _attn_prep(ua, cos64, sin64, two(q_gain[li]), two(k_gain[li]), bd_att, swap_att, lc)
        att_c = _flash(qt, kx, vt, tq, TOK_TILE, 1, l // tq, lc // tq, lc // TOK_TILE)
        s_bound = (HEAD_DIM ** 0.5) * LOG2E * jnp.max(jnp.abs(q_gain[li])) * jnp.max(jnp.abs(k_gain[li]))
        lat_args = (qt, kx, vt, tq, tk, qn, 0, l // (qn * tq), lt // tk)
        att_l = lax.cond(s_bound <= MAX_UNSHIFTED_SCORE,
                         lambda: _flash(*lat_args, bounded=True), lambda: _flash(*lat_args, bounded=False))

        wl = jnp.zeros((LANES, 5 * cw), F32)
        wl = wl.at[0:W_LORA, 0:cw].set(rwkv_wB[li, 0]).at[W_LORA:2 * W_LORA, cw:2 * cw].set(rwkv_wB[li, 1])
        o_a = 2 * W_LORA
        wl = wl.at[o_a:o_a + A_LORA, 2 * cw:3 * cw].set(rwkv_aB[li, 0])
        wl = wl.at[o_a + A_LORA:o_a + 2 * A_LORA, 3 * cw:4 * cw].set(rwkv_aB[li, 1])
        o_g = o_a + 2 * A_LORA
        wl = wl.at[o_g:o_g + G_LORA, 4 * cw:5 * cw].set(rwkv_gB[li])
        mu = jnp.pad(rwkv_mu[li], ((0, 0), (0, IN_RWKV_PAD - IN_RWKV)))
        r_, v_, kk_, g_, bon_, lw_, be_, kd_ = _rwkv_prep(
            ur, mu, wl, bd, rwkv_w0[li], rwkv_a0[li], rwkv_kk[li][None], rwkv_ka[li][None],
            rwkv_rk[li].reshape(1, cw), lc)
        yf, yb = _wkv_scan(r_, v_, kk_, lw_, be_, kd_, lc)
        rw = _rwkv_out(yf, yb, bon_, g_, bd, rwkv_gn_g[li][None], rwkv_gn_b[li][None])

        x1, x2, vv = _hy_prep(uh, hy_short_w[li], hy_short_b[li][None], lc)
        w1p = jnp.pad(hy_w1[li], ((0, LANES - HY_EMB), (0, 0)))
        fargs = (w1p, hy_b1[li][None], hy_freq1[li][None], hy_w2[li], hy_b2[li][None], hy_freq2[li][None],
                 hy_w3[li], hy_decay[li][None])
        fw, bw, nrm = _hy_filter(feat_l, *fargs)
        c2 = HY_ORDER * HY_WIDTH
        a4 = _dft_cols(fstack, fw.reshape(1, nh, DFT_N2 * c2), 0, bw.reshape(1, nh, DFT_N2 * c2), 0, True)
        kspec = _spec(a4.reshape(4, n1, DFT_N2, c2), g_tab, nrm, n1 * DFT_N2)
        lat = lambda t: t[:, lc:].reshape(b, nh, cols)
        x1l, x2l, zin = lat(x1), lat(x2), lat(vv)
        for o, gate in enumerate((x1l, x2l)):
            a = _dft_cols(fstack, zin, 0, zin, 1, False)
            bv = _conv_mid(a.reshape(2, n1, DFT_N2, HY_WIDTH), g_tab, kspec, o)
            bias_cols = jnp.tile(hy_bias[li, o], DFT_N2)[None]
            zin = _idft_cols(cstack, bv.reshape(2, n1, cols), gate, zin, bias_cols)
        hy_l = zin.reshape(b, l, HY_WIDTH)
        fw_c, bw_c, nrm_c = _hy_filter(feat_c, *fargs)
        hy_c = _hy_ctx(x1[:, :lc], x2[:, :lc], vv[:, :lc], fw_c, bw_c, nrm_c, hy_bias[li], ff_c, ci_c)

        xx = _outproj(att_c, att_l, rw, hy_c, hy_l, xx, mod, w_out[li].astype(BF16),
                      ln1_g[li][None], ln1_b[li][None], lc)

        j = li // 2
        if li % 2 == 0:
            xx = _ffn(xx, mod, ffn_w1[j].astype(BF16), ffn_w3[j].astype(BF16), ffn_w2[j].astype(BF16),
                      ln2_g[li][None], ln2_b[li][None], lc)
        else:
            wr = jnp.pad(moe_router[j], ((0, 0), (0, LANES - N_EXPERTS)))
            xx = _moe_sparse(xx, mod, wr, moe_w1[j].astype(BF16), moe_w3[j].astype(BF16), moe_w2[j].astype(BF16),
                             ln2_g[li][None], ln2_b[li][None], lc)
    return xx[:, lc:]
```

```python
import functools
import math

import jax
import jax.numpy as jnp
from jax import lax
from jax.experimental import pallas as pl
from jax.experimental.pallas import tpu as pltpu

F32 = jnp.float32
BF16 = jnp.bfloat16
HI = lax.Precision.HIGHEST

D_MODEL = 1024
DEPTH = 2
GRID_W = 64
HEAD_DIM = 64
HALF_HD = HEAD_DIM // 2
ATT_WIDTH = 512
RWKV_WIDTH = 256
HY_WIDTH = 256
ATT_HEADS = 8
ATT_KV_HEADS = 2
ATT_REP = 4
ATT_KV_WIDTH = 128
ROPE_THETA = 10000.0
QK_EPS = 1e-6
RWKV_HEADS = 4
W_LORA = 16
A_LORA = 16
G_LORA = 32
RWKV_GN_EPS = 64e-5
HY_ORDER = 2
HY_EMB = 33
HY_FFN = 64
N_EXPERTS = 8
LN_EPS = 1e-6
IN_ATT = ATT_WIDTH + 2 * ATT_KV_WIDTH
IN_RWKV = 3 * RWKV_WIDTH + 2 * W_LORA + 2 * A_LORA + G_LORA
IN_RWKV_PAD = 896
IN_HY = 3 * HY_WIDTH
ALPHA = float((2 * DEPTH) ** 0.25)
LOG2E = 1.4426950408889634
MAX_UNSHIFTED_SCORE = 40.0

LANES = 128
SUBLANES = 8
TOK_TILE = 256
WKV_CHUNK = 64
WKV_TILE = 256
DFT_N2 = 256
MOE_ROW_BLOCK = 128
MOE_MOVE_BLOCK = 256
VMEM_LIMIT = 48 * 1024 * 1024
VMEM_LIMIT_MOE = 56 * 1024 * 1024


def _cparams(sem, vmem=VMEM_LIMIT):
    return pltpu.CompilerParams(dimension_semantics=sem, vmem_limit_bytes=vmem)


def _pick_tile(n, cands):
    for c in cands:
        if n % c == 0:
            return c
    raise ValueError(f"no tile for {n} in {cands}")


def _layer_norm(x):
    mu = jnp.mean(x, -1, keepdims=True)
    xc = x - mu
    var = jnp.mean(xc * xc, -1, keepdims=True)
    return xc * lax.rsqrt(var + LN_EPS)


def _sel_mod(mod_ref, j, row0, tm, lc):
    rid = row0 + lax.broadcasted_iota(jnp.int32, (tm, 1), 0)
    return jnp.where(rid < lc, mod_ref[0, 0, j:j + 1, :], mod_ref[0, 1, j:j + 1, :])


def _silu(x):
    return x * jax.nn.sigmoid(x)


def _ada_kernel(c_ref, w_ref, b_ref, o_ref):
    s = _silu(c_ref[...])
    o_ref[0] = jnp.dot(s, w_ref[0], precision=HI, preferred_element_type=F32) + b_ref[0]


def _ada_mod(cond8, ada_w, ada_b):
    depth, d, n = ada_w.shape
    tn = _pick_tile(n, (1536, 1024, 512, 256, 128))
    return pl.pallas_call(
        _ada_kernel,
        grid=(depth, n // tn),
        in_specs=[pl.BlockSpec((SUBLANES, d), lambda l, j: (0, 0)),
                  pl.BlockSpec((1, d, tn), lambda l, j: (l, 0, j)),
                  pl.BlockSpec((1, 1, tn), lambda l, j: (l, 0, j))],
        out_specs=pl.BlockSpec((1, SUBLANES, tn), lambda l, j: (l, 0, j)),
        out_shape=jax.ShapeDtypeStruct((depth, SUBLANES, n), F32),
        compiler_params=_cparams(("parallel", "parallel")),
        name="ada_mod",
    )(cond8, ada_w, ada_b.reshape(depth, 1, n))


def _inproj_kernel(x_ref, mod_ref, w_ref, oa_ref, or_ref, oh_ref, *, lc, tm):
    row0 = pl.program_id(1) * tm
    sh = _sel_mod(mod_ref, 0, row0, tm, lc)
    sc = _sel_mod(mod_ref, 1, row0, tm, lc)
    h = (_layer_norm(x_ref[0]) * (1.0 + sc) + sh).astype(BF16)
    u = jnp.dot(h, w_ref[...], preferred_element_type=F32)
    oa_ref[0] = u[:, :IN_ATT]
    or_ref[0] = u[:, IN_ATT:IN_ATT + IN_RWKV_PAD]
    oh_ref[0] = u[:, IN_ATT + IN_RWKV_PAD:]


def _inproj(xx, mod, w_pad, lc):
    b, lt, d = xx.shape
    tm = TOK_TILE
    n = w_pad.shape[1]
    return pl.pallas_call(
        functools.partial(_inproj_kernel, lc=lc, tm=tm),
        grid=(b, lt // tm),
        in_specs=[pl.BlockSpec((1, tm, d), lambda bi, i: (bi, i, 0)),
                  pl.BlockSpec((1, 2, 6, d), lambda bi, i: (bi, 0, 0, 0)),
                  pl.BlockSpec((d, n), lambda bi, i: (0, 0))],
        out_specs=[pl.BlockSpec((1, tm, IN_ATT), lambda bi, i: (bi, i, 0)),
                   pl.BlockSpec((1, tm, IN_RWKV_PAD), lambda bi, i: (bi, i, 0)),
                   pl.BlockSpec((1, tm, IN_HY), lambda bi, i: (bi, i, 0))],
        out_shape=[jax.ShapeDtypeStruct((b, lt, IN_ATT), F32),
                   jax.ShapeDtypeStruct((b, lt, IN_RWKV_PAD), F32),
                   jax.ShapeDtypeStruct((b, lt, IN_HY), F32)],
        compiler_params=_cparams(("parallel", "parallel")),
        name="inproj",
    )(xx, mod, w_pad)


def _attn_prep_kernel(u_ref, cos_ref, sin_ref, qg_ref, kg_ref, bd_ref, sw_ref, qt_ref, k_ref, vt_ref):
    u = u_ref[0]

    def norm_rope(x, g):
        w = x.shape[1]
        tile = lambda t: jnp.concatenate([t] * (w // LANES), -1)
        ms = _dot01(x * x, bd_ref[:w, :w]) * (1.0 / HEAD_DIM)
        xn = x * lax.rsqrt(ms + QK_EPS) * tile(g)
        sw = _dot01(xn, sw_ref[:w, :w])
        return xn * tile(cos_ref[...]) + sw * tile(sin_ref[...])

    q = norm_rope(u[:, :ATT_WIDTH], qg_ref[...]) * (LOG2E * HEAD_DIM ** -0.5)
    qt = q.T
    kx = norm_rope(u[:, ATT_WIDTH:ATT_WIDTH + ATT_KV_WIDTH], kg_ref[...])
    for g in range(ATT_KV_HEADS):
        base = g * ATT_REP * HEAD_DIM
        qt_ref[0, g, 0] = jnp.concatenate(
            [qt[base + r * HEAD_DIM:base + (r + 1) * HEAD_DIM] for r in range(ATT_REP)], -1).astype(BF16)
        k_ref[0, g] = kx[:, g * HEAD_DIM:(g + 1) * HEAD_DIM].astype(BF16)
    v0 = ATT_WIDTH + ATT_KV_WIDTH
    vt = u[:, v0:v0 + ATT_KV_WIDTH].T
    for g in range(ATT_KV_HEADS):
        vt_ref[0, g] = vt[g * HEAD_DIM:(g + 1) * HEAD_DIM].astype(BF16)


def _attn_prep(ua, cos, sin, qg, kg, bd_att, swap_att, lc):
    b, lt, _ = ua.shape
    tm = TOK_TILE
    nct = lc // tm
    nl = lt // tm - nct
    q_pos = lambda i: jnp.where(i < nct, nl + i, i - nct)
    return pl.pallas_call(
        _attn_prep_kernel,
        grid=(b, lt // tm),
        in_specs=[pl.BlockSpec((1, tm, IN_ATT), lambda bi, i: (bi, i, 0)),
                  pl.BlockSpec((tm, LANES), lambda bi, i: (i, 0)),
                  pl.BlockSpec((tm, LANES), lambda bi, i: (i, 0)),
                  pl.BlockSpec((1, LANES), lambda bi, i: (0, 0)),
                  pl.BlockSpec((1, LANES), lambda bi, i: (0, 0)),
                  pl.BlockSpec((ATT_WIDTH, ATT_WIDTH), lambda bi, i: (0, 0)),
                  pl.BlockSpec((ATT_WIDTH, ATT_WIDTH), lambda bi, i: (0, 0))],
        out_specs=[pl.BlockSpec((1, ATT_KV_HEADS, 1, HEAD_DIM, ATT_REP * tm),
                                lambda bi, i: (bi, 0, q_pos(i), 0, 0)),
                   pl.BlockSpec((1, ATT_KV_HEADS, tm, HEAD_DIM), lambda bi, i: (bi, 0, i, 0)),
                   pl.BlockSpec((1, ATT_KV_HEADS, HEAD_DIM, tm), lambda bi, i: (bi, 0, 0, i))],
        out_shape=[jax.ShapeDtypeStruct((b, ATT_KV_HEADS, lt // tm, HEAD_DIM, ATT_REP * tm), BF16),
                   jax.ShapeDtypeStruct((b, ATT_KV_HEADS, lt, HEAD_DIM), BF16),
                   jax.ShapeDtypeStruct((b, ATT_KV_HEADS, HEAD_DIM, lt), BF16)],
        compiler_params=_cparams(("parallel", "parallel")),
        name="attn_prep",
    )(ua, cos, sin, qg, kg, bd_att, swap_att)


def _flash_kernel(qt_ref, k_ref, vt_ref, o_ref, m_scr, l_scr, acc_scr, *, nk, tq, sub, qn):
    j = pl.program_id(3)

    @pl.when(j == 0)
    def _():
        m_scr[...] = jnp.full(m_scr.shape, -jnp.inf, F32)
        l_scr[...] = jnp.zeros(l_scr.shape, F32)
        acc_scr[...] = jnp.zeros(acc_scr.shape, F32)

    qt = jnp.concatenate([qt_ref[0, 0, t] for t in range(qn)], -1)
    nsub = k_ref.shape[2] // sub
    m = m_scr[...]
    l = l_scr[...]
    acc = acc_scr[...]
    scores = lambda c: jnp.dot(k_ref[0, 0, c * sub:(c + 1) * sub, :], qt, preferred_element_type=F32)
    pv = lambda c, p: jnp.dot(vt_ref[0, 0, :, c * sub:(c + 1) * sub], p, preferred_element_type=F32)
    s_next = scores(0)
    pend = None
    for c in range(nsub):
        s = s_next
        if c + 1 < nsub:
            s_next = scores(c + 1)
        if pend is not None:
            acc = pend[0] * acc + pv(c - 1, pend[1])
        m_new = jnp.maximum(m, jnp.max(s, 0, keepdims=True))
        a = jnp.exp2(m - m_new)
        p = jnp.exp2(s - m_new)
        l = a * l + jnp.sum(p, 0, keepdims=True)
        pend = (a, p.astype(BF16))
        m = m_new
    acc = pend[0] * acc + pv(nsub - 1, pend[1])
    m_scr[...] = m
    l_scr[...] = l
    acc_scr[...] = acc

    @pl.when(j == nk - 1)
    def _():
        o = (acc / l).T
        for t in range(qn):
            o_ref[0, t * tq:(t + 1) * tq, :] = jnp.concatenate(
                [o[(t * ATT_REP + r) * tq:(t * ATT_REP + r + 1) * tq] for r in range(ATT_REP)], -1
            ).astype(o_ref.dtype)


def _flash_bounded_kernel(qt_ref, k_ref, vt_ref, o_ref, l_scr, acc_scr, *, nk, tq, sub, qn):
    j = pl.program_id(3)

    @pl.when(j == 0)
    def _():
        l_scr[...] = jnp.zeros(l_scr.shape, F32)
        acc_scr[...] = jnp.zeros(acc_scr.shape, F32)

    qt = jnp.concatenate([qt_ref[0, 0, t] for t in range(qn)], -1)
    nsub = k_ref.shape[2] // sub

    l = l_scr[...]
    acc = acc_scr[...]
    scores = lambda c: jnp.dot(k_ref[0, 0, c * sub:(c + 1) * sub, :], qt, preferred_element_type=F32)
    s_next = scores(0)
    for c in range(nsub):
        s = s_next
        if c + 1 < nsub:
            s_next = scores(c + 1)
        p = jnp.exp2(s)
        l = l + jnp.sum(p.reshape(sub // SUBLANES, SUBLANES, p.shape[1]), 0)
        acc = acc + jnp.dot(vt_ref[0, 0, :, c * sub:(c + 1) * sub], p.astype(BF16), preferred_element_type=F32)
    l_scr[...] = l
    acc_scr[...] = acc

    @pl.when(j == nk - 1)
    def _():
        o = (acc / jnp.sum(l, 0, keepdims=True)).T
        for t in range(qn):
            o_ref[0, t * tq:(t + 1) * tq, :] = jnp.concatenate(
                [o[(t * ATT_REP + r) * tq:(t * ATT_REP + r + 1) * tq] for r in range(ATT_REP)], -1
            ).astype(o_ref.dtype)


def _flash(qt, k, vt, tq, tk, qn, q_blk0, nq, nk, bounded=False):
    b = qt.shape[0]
    lq = nq * qn * tq
    sub = _pick_tile(tk, (256, 128))
    lanes = qn * ATT_REP * tq
    if bounded:
        body = functools.partial(_flash_bounded_kernel, nk=nk, tq=tq, sub=sub, qn=qn)
        scratch = [pltpu.VMEM((SUBLANES, lanes), F32), pltpu.VMEM((HEAD_DIM, lanes), F32)]
    else:
        body = functools.partial(_flash_kernel, nk=nk, tq=tq, sub=sub, qn=qn)
        scratch = [pltpu.VMEM((1, lanes), F32), pltpu.VMEM((1, lanes), F32), pltpu.VMEM((HEAD_DIM, lanes), F32)]
    return pl.pallas_call(
        body,
        grid=(b, ATT_KV_HEADS, nq, nk),
        in_specs=[pl.BlockSpec((1, 1, qn, HEAD_DIM, ATT_REP * tq), lambda bi, g, i, j: (bi, g, i + q_blk0, 0, 0)),
                  pl.BlockSpec((1, 1, tk, HEAD_DIM), lambda bi, g, i, j: (bi, g, j, 0)),
                  pl.BlockSpec((1, 1, HEAD_DIM, tk), lambda bi, g, i, j: (bi, g, 0, j))],
        out_specs=pl.BlockSpec((1, qn * tq, ATT_REP * HEAD_DIM), lambda bi, g, i, j: (bi, i, g)),
        out_shape=jax.ShapeDtypeStruct((b, lq, ATT_WIDTH), BF16),
        scratch_shapes=scratch,
        compiler_params=_cparams(("parallel", "parallel", "parallel", "arbitrary")),
        name="flash_bounded" if bounded else "flash",
    )(qt, k, vt)


def _prev_next(u, up8, un8, i, tm, lc, lt):
    start = i * tm
    p_ok = jnp.logical_and(start != 0, start != lc)
    n_ok = jnp.logical_and(start + tm != lc, start + tm != lt)
    prow = jnp.where(p_ok, up8[SUBLANES - 1:SUBLANES], 0.0)
    nrow = jnp.where(n_ok, un8[0:1], 0.0)
    rid = lax.broadcasted_iota(jnp.int32, u.shape, 0)
    prev = jnp.where(rid == 0, prow, pltpu.roll(u, 1, 0))
    nxt = jnp.where(rid == tm - 1, nrow, pltpu.roll(u, tm - 1, 0))
    return prev, nxt


def _halo_specs(tm, c, lt):
    r = tm // SUBLANES
    last = lt // SUBLANES - 1
    return [pl.BlockSpec((1, tm, c), lambda bi, i: (bi, i, 0)),
            pl.BlockSpec((1, SUBLANES, c), lambda bi, i: (bi, jnp.maximum(i * r - 1, 0), 0)),
            pl.BlockSpec((1, SUBLANES, c), lambda bi, i: (bi, jnp.minimum((i + 1) * r, last), 0))]


def _softplus(z):
    return jnp.maximum(z, 0.0) + jnp.log1p(jnp.exp(-jnp.abs(z)))


def _rwkv_prep_kernel(u_ref, up_ref, un_ref, mu_ref, wl_ref, bd_ref, w0_ref, a0_ref, kkw_ref, ka_ref, rk_ref,
                      r_o, v_o, kk_o, g_o, bon_o, lw_o, be_o, kd_o, *, lc, lt, tm):
    i = pl.program_id(1)
    u = u_ref[0]
    prev, nxt = _prev_next(u, up_ref[0], un_ref[0], i, tm, lc, lt)
    us = u + mu_ref[0:1] * (prev - u) + mu_ref[1:2] * (nxt - u)
    c = RWKV_WIDTH
    r = us[:, 0:c]
    k = us[:, c:2 * c]
    v = us[:, 2 * c:3 * c]
    slab = us[:, 3 * c:3 * c + LANES]
    lane = lax.broadcasted_iota(jnp.int32, slab.shape, 1)
    o_a = 2 * W_LORA
    o_g = o_a + 2 * A_LORA
    act = jnp.where(lane < o_a, jnp.tanh(slab),
                    jnp.where(lane < o_g, slab,
                              jnp.where(lane < o_g + G_LORA, jax.nn.sigmoid(slab), 0.0)))
    lo = _dotp(act, wl_ref[...], 3)
    bd = bd_ref[...]
    kk0 = k * kkw_ref[...]
    kk = kk0 * lax.rsqrt(_dot01(kk0 * kk0, bd) + 1e-12)
    r_o[0] = r
    v_o[0] = v
    kk_o[0] = kk
    g_o[0] = lo[:, 4 * c:5 * c]
    bon = jnp.zeros_like(r)
    for d in range(2):
        w_raw = w0_ref[d:d + 1] + lo[:, d * c:(d + 1) * c]
        lw = -jnp.exp(-_softplus(-w_raw) - 0.5)
        a = jax.nn.sigmoid(a0_ref[d:d + 1] + lo[:, (2 + d) * c:(3 + d) * c])
        kd = k * (1.0 + (a - 1.0) * ka_ref[...])
        lw_o[0, d] = lw
        be_o[0, d] = a * kk
        kd_o[0, d] = kd
        bon = bon + r * kd * rk_ref[...]
    bon_o[0] = _dot01(bon, bd) * v


def _rwkv_prep(ur, mu, wl, bd, w0, a0, kkw, ka, rk, lc):
    b, lt, cp = ur.shape
    tm = TOK_TILE
    c = RWKV_WIDTH
    full = lambda shape: pl.BlockSpec(shape, lambda bi, i: (0,) * len(shape))
    tok = pl.BlockSpec((1, tm, c), lambda bi, i: (bi, i, 0))
    tok2 = pl.BlockSpec((1, 2, tm, c), lambda bi, i: (bi, 0, i, 0))
    s1 = jax.ShapeDtypeStruct((b, lt, c), F32)
    s2 = jax.ShapeDtypeStruct((b, 2, lt, c), F32)
    return pl.pallas_call(
        functools.partial(_rwkv_prep_kernel, lc=lc, lt=lt, tm=tm),
        grid=(b, lt // tm),
        in_specs=_halo_specs(tm, cp, lt) + [full((2, cp)), full((LANES, 5 * c)), full((c, c)), full((2, c)),
                                            full((2, c)), full((1, c)), full((1, c)), full((1, c))],
        out_specs=[tok, tok, tok, tok, tok, tok2, tok2, tok2],
        out_shape=[s1, s1, s1, s1, s1, s2, s2, s2],
        compiler_params=_cparams(("parallel", "parallel")),
        name="rwkv_prep",
    )(ur, ur, ur, mu, wl, bd, w0, a0, kkw, ka, rk)


def _mm(a, b):
    return jnp.dot(a, b, precision=HI, preferred_element_type=F32)


_NN = ((1,), (0,))
_NT = ((1,), (1,))
_TN = ((0,), (0,))


def _split2(a):
    hi = a.astype(BF16)
    return hi, (a - hi.astype(F32)).astype(BF16)


def _dotp(a, b, passes, dims=_NN):
    if a.ndim == 3:
        dn = (((dims[0][0] + 1,), (dims[1][0] + 1,)), ((0,), (0,)))
    else:
        dn = (dims, ((), ()))
    dg = lambda p, q: lax.dot_general(p, q, dn, preferred_element_type=F32)
    if passes == 1:
        return dg(a.astype(BF16), b.astype(BF16))
    ah, al = _split2(a)
    bh, bl = _split2(b)
    return dg(ah, bh) + dg(ah, bl) + dg(al, bh)


def _dot01(a, ones):
    ah, al = _split2(a)
    o = ones.astype(BF16)
    return jnp.dot(ah, o, preferred_element_type=F32) + jnp.dot(al, o, preferred_element_type=F32)


P_M = 1
P_INV = 1
P_W = 1
P_Z = 1
P_STATE = 3
P_DFT = 1


def _unit_tri_inv(a_mat, row, col, eye, passes):
    t = a_mat.shape[-1]
    eye_f = eye.astype(F32)
    base = SUBLANES
    same = (row // base) == (col // base)
    n1 = -jnp.where(same, a_mat, 0.0)
    n2 = _dotp(n1, n1, passes)
    n4 = _dotp(n2, n2, passes)
    x = _dotp(_dotp(eye_f + n1, eye_f + n2, passes), eye_f + n4, passes)
    m = base
    while m < t:
        off = jnp.logical_and((row // (2 * m)) == (col // (2 * m)), (row // m) != (col // m))
        x = x - _dotp(x, _dotp(jnp.where(off, a_mat, 0.0), x, passes), passes)
        m *= 2
    return x


def _wkv_kernel(rf_ref, vf_ref, kkf_ref, rb_ref, vb_ref, kkb_ref, lwf_ref, bef_ref, kdf_ref,
                lwb_ref, beb_ref, kdb_ref, yf_ref, yb_ref, h_scr, pq_scr, ry_scr):
    t = WKV_CHUNK
    n = HEAD_DIM
    g = WKV_TILE // WKV_CHUNK
    tt = WKV_TILE
    nh = RWKV_HEADS
    nd = g * nh
    nu = 2 * nd
    orders = (list(range(g)), list(range(g - 1, -1, -1)))

    @pl.when(pl.program_id(1) == 0)
    def _():
        h_scr[...] = jnp.zeros(h_scr.shape, F32)
        pq_scr[...] = jnp.zeros(pq_scr.shape, F32)
        ry_scr[...] = jnp.zeros(ry_scr.shape, F32)

    hm = h_scr[...]
    for p in range(g):
        both = lambda ref: jnp.concatenate([ref[d * nd + p * nh:d * nd + (p + 1) * nh] for d in range(2)], 0)
        ry = both(ry_scr)
        pq = both(pq_scr)
        y = _dotp(ry[:, :, :n], hm, P_STATE) + ry[:, :, n:]
        hm = _dotp(pq[:, :, :n], hm, P_STATE) + pq[:, :, n:]
        for d, y_ref in enumerate((yf_ref, yb_ref)):
            c = orders[d][p]
            y_ref[0, c * t:(c + 1) * t, :] = jnp.concatenate([y[d * nh + h] for h in range(nh)], -1)
    h_scr[...] = hm

    row = lax.broadcasted_iota(jnp.int32, (tt, tt), 0)
    col = lax.broadcasted_iota(jnp.int32, (tt, tt), 1)
    same = (row // t) == (col // t)

    def scaled(d, r_ref, v_ref, kk_ref, lw_ref, be_ref, kd_ref):
        tri = jnp.logical_and(same, (row >= col) if d == 0 else (row <= col))
        sums = jnp.concatenate([jnp.where(tri, 1.0, 0.0), jnp.where(same, 1.0, 0.0)], 0).astype(BF16)
        lw = lw_ref[0, 0]
        l1 = lw.astype(BF16)
        rem = lw - l1.astype(F32)
        l2 = rem.astype(BF16)
        l3 = (rem - l2.astype(F32)).astype(BF16)
        cc = (jnp.dot(sums, l1, preferred_element_type=F32) + jnp.dot(sums, l2, preferred_element_type=F32)
              + jnp.dot(sums, l3, preferred_element_type=F32))
        cum = cc[:tt]
        ctot = cc[tt:]
        e_neg = jnp.exp(-cum)
        e_end = jnp.exp(ctot - cum)
        be = be_ref[0, 0]
        kd = kd_ref[0, 0]

        def units(x, rows=t):
            return jnp.stack([x[orders[d][p] * t:orders[d][p] * t + rows, h * n:(h + 1) * n]
                              for p in range(g) for h in range(nh)], 0)

        return dict(kap=units(kk_ref[0] * jnp.exp(cum - lw)), rt=units(r_ref[0] * jnp.exp(cum)), vh=units(v_ref[0]),
                    bet=units(be * e_neg), kdt=units(kd * e_neg), beh=units(be * e_end), kdh=units(kd * e_end),
                    gend=units(jnp.exp(ctot), 1))

    parts = (scaled(0, rf_ref, vf_ref, kkf_ref, lwf_ref, bef_ref, kdf_ref),
             scaled(1, rb_ref, vb_ref, kkb_ref, lwb_ref, beb_ref, kdb_ref))
    cat = lambda key: jnp.concatenate([parts[0][key], parts[1][key]], 0)
    kap, rt, vh = cat("kap"), cat("rt"), cat("vh")
    r64 = lax.broadcasted_iota(jnp.int32, (t, t), 0)
    c64 = lax.broadcasted_iota(jnp.int32, (t, t), 1)
    eye = r64 == c64
    unit = lax.broadcasted_iota(jnp.int32, (nu, t, t), 0)
    ahead = (r64 - c64)[None] * jnp.where(unit < nd, 1, -1)
    strict = ahead > 0
    incl = ahead >= 0
    m = _dotp(jnp.concatenate([kap, rt], 1), jnp.concatenate([cat("bet"), cat("kdt")], 1), P_M, _NT)
    a_mat = jnp.where(strict, m[:, :t, :t], 0.0)
    b_mat = jnp.where(strict, m[:, :t, t:], 0.0)
    ab_r = jnp.concatenate([jnp.where(incl, m[:, t:, :t], 0.0), jnp.where(incl, m[:, t:, t:], 0.0)], 2)
    tinv = _unit_tri_inv(a_mat, r64, c64, eye, P_INV)
    w = _dotp(tinv, jnp.concatenate([kap, _dotp(b_mat, vh, P_W)], 2), P_W)
    z = jnp.concatenate([-w, jnp.concatenate([jnp.zeros((nu, t, n), F32), vh], 2)], 1)
    ry = _dotp(ab_r, z, P_Z)
    ry_scr[:, :, :n] = ry[:, :, :n] + rt
    ry_scr[:, :, n:] = ry[:, :, n:]
    pq = _dotp(jnp.concatenate([cat("beh"), cat("kdh")], 1), z, P_Z, _TN)
    gd = jnp.where(eye, jnp.broadcast_to(cat("gend"), (nu, n, n)), 0.0)
    pq_scr[:, :, :n] = pq[:, :, :n] + gd
    pq_scr[:, :, n:] = pq[:, :, n:]


def _wkv_scan(r, v, kk, lw, be, kd, lc):
    b, lt, c = r.shape
    tt = WKV_TILE
    nt = lt // tt
    ntc = lc // tt
    tiles = (lambda i: i, lambda i: jnp.where(i < ntc, ntc - 1 - i, nt - 1 - (i - ntc)))
    t_in = lambda d: (lambda i: tiles[d](jnp.minimum(i, nt - 1)))
    t_out = lambda d: (lambda i: tiles[d](jnp.maximum(i - 1, 0)))
    one = lambda d: pl.BlockSpec((1, tt, c), lambda bi, i: (bi, t_in(d)(i), 0))
    two = lambda d: pl.BlockSpec((1, 1, tt, c), lambda bi, i: (bi, d, t_in(d)(i), 0))
    out = lambda d: pl.BlockSpec((1, tt, c), lambda bi, i: (bi, t_out(d)(i), 0))
    nu = 2 * (tt // WKV_CHUNK) * RWKV_HEADS
    ysh = jax.ShapeDtypeStruct((b, lt, c), F32)
    return pl.pallas_call(
        _wkv_kernel,
        grid=(b, nt + 1),
        in_specs=[one(0)] * 3 + [one(1)] * 3 + [two(0)] * 3 + [two(1)] * 3,
        out_specs=[out(0), out(1)],
        out_shape=[ysh, ysh],
        scratch_shapes=[pltpu.VMEM((2 * RWKV_HEADS, HEAD_DIM, HEAD_DIM), F32),
                        pltpu.VMEM((nu, HEAD_DIM, 2 * HEAD_DIM), F32),
                        pltpu.VMEM((nu, WKV_CHUNK, 2 * HEAD_DIM), F32)],
        compiler_params=_cparams(("parallel", "arbitrary")),
        name="wkv_scan",
    )(r, v, kk, r, v, kk, lw, be, kd, lw, be, kd)


def _rwkv_out_kernel(yf_ref, yb_ref, bon_ref, g_ref, bd_ref, gg_ref, gb_ref, o_ref):
    y = yf_ref[0] + yb_ref[0] + bon_ref[0]
    bd = bd_ref[...]
    mu = _dot01(y, bd) * (1.0 / HEAD_DIM)
    yc = y - mu
    var = _dot01(yc * yc, bd) * (1.0 / HEAD_DIM)
    yn = yc * lax.rsqrt(var + RWKV_GN_EPS) * gg_ref[...] + gb_ref[...]
    o_ref[0] = (yn * g_ref[0]).astype(o_ref.dtype)


def _rwkv_out(yf, yb, bon, g, bd, gg, gb):
    b, lt, c = yf.shape
    tm = TOK_TILE
    tok = pl.BlockSpec((1, tm, c), lambda bi, i: (bi, i, 0))
    full = lambda shape: pl.BlockSpec(shape, lambda bi, i: (0,) * len(shape))
    return pl.pallas_call(
        _rwkv_out_kernel,
        grid=(b, lt // tm),
        in_specs=[tok, tok, tok, tok, full((c, c)), full((1, c)), full((1, c))],
        out_specs=tok,
        out_shape=jax.ShapeDtypeStruct((b, lt, c), BF16),
        compiler_params=_cparams(("parallel", "parallel")),
        name="rwkv_out",
    )(yf, yb, bon, g, bd, gg, gb)


def _hy_prep_kernel(u_ref, up_ref, un_ref, w_ref, b_ref, x1_o, x2_o, v_o, *, lc, lt, tm):
    i = pl.program_id(1)
    u = u_ref[0]
    prev, nxt = _prev_next(u, up_ref[0], un_ref[0], i, tm, lc, lt)
    y = prev * w_ref[0:1] + u * w_ref[1:2] + nxt * w_ref[2:3] + b_ref[...]
    c = HY_WIDTH
    x1_o[0] = y[:, :c]
    x2_o[0] = y[:, c:2 * c]
    v_o[0] = y[:, 2 * c:]


def _hy_prep(uh, w, bias, lc):
    b, lt, cin = uh.shape
    tm = TOK_TILE
    c = HY_WIDTH
    full = lambda shape: pl.BlockSpec(shape, lambda bi, i: (0,) * len(shape))
    tok = pl.BlockSpec((1, tm, c), lambda bi, i: (bi, i, 0))
    s1 = jax.ShapeDtypeStruct((b, lt, c), F32)
    return pl.pallas_call(
        functools.partial(_hy_prep_kernel, lc=lc, lt=lt, tm=tm),
        grid=(b, lt // tm),
        in_specs=_halo_specs(tm, cin, lt) + [full((3, cin)), full((1, cin))],
        out_specs=[tok, tok, tok],
        out_shape=[s1, s1, s1],
        compiler_params=_cparams(("parallel", "parallel")),
        name="hy_prep",
    )(uh, uh, uh, w, bias)


def _hy_filter_kernel(z_ref, w1_ref, b1_ref, f1_ref, w2_ref, b2_ref, f2_ref, w3_ref, dec_ref,
                      fw_o, bw_o, nrm_o, *, tl):
    i = pl.program_id(0)
    z = z_ref[...]
    h = jnp.sin(f1_ref[...] * (_dotp(z, w1_ref[...], 3) + b1_ref[...]))
    h = jnp.sin(f2_ref[...] * (_dotp(h, w2_ref[...], 3) + b2_ref[...]))
    h = _dotp(h, w3_ref[...], 3) * jnp.exp(-z[:, 0:1] * dec_ref[...])
    c = HY_WIDTH
    fw = jnp.concatenate([h[:, 0:c], h[:, 2 * c:3 * c]], 1)
    bw = jnp.concatenate([h[:, c:2 * c], h[:, 3 * c:4 * c]], 1)
    rid = i * tl + lax.broadcasted_iota(jnp.int32, (tl, 1), 0)
    bw = jnp.where(rid == 0, 0.0, bw)
    fw_o[...] = fw
    bw_o[...] = bw

    @pl.when(i == 0)
    def _():
        nrm_o[...] = jnp.zeros(nrm_o.shape, F32)

    nrm_o[...] += jnp.sum(jnp.abs(fw) + jnp.abs(bw), 0, keepdims=True)


def _hy_filter(feat, w1p, b1, f1, w2, b2, f2, w3, dec):
    l, fe = feat.shape
    tl = _pick_tile(l, (512, 256))
    c2 = HY_ORDER * HY_WIDTH
    full = lambda shape: pl.BlockSpec(shape, lambda i: (0,) * len(shape))
    return pl.pallas_call(
        functools.partial(_hy_filter_kernel, tl=tl),
        grid=(l // tl,),
        in_specs=[pl.BlockSpec((tl, fe), lambda i: (i, 0)), full(w1p.shape), full(b1.shape), full(f1.shape),
                  full(w2.shape), full(b2.shape), full(f2.shape), full(w3.shape), full(dec.shape)],
        out_specs=[pl.BlockSpec((tl, c2), lambda i: (i, 0)), pl.BlockSpec((tl, c2), lambda i: (i, 0)),
                   pl.BlockSpec((1, c2), lambda i: (0, 0))],
        out_shape=[jax.ShapeDtypeStruct((l, c2), F32), jax.ShapeDtypeStruct((l, c2), F32),
                   jax.ShapeDtypeStruct((1, c2), F32)],
        compiler_params=_cparams(("arbitrary",)),
        name="hy_filter",
    )(feat, w1p, b1, f1, w2, b2, f2, w3, dec)


def _dft_cols_kernel(f_ref, xa_ref, xb_ref, o_ref, *, n1, pair):
    f = f_ref[...]
    pa = _dotp(f, xa_ref[0], P_DFT)
    pb = _dotp(f, xb_ref[0], P_DFT)
    if pair:
        o_ref[0] = pa[:n1].astype(o_ref.dtype)
        o_ref[1] = pa[n1:].astype(o_ref.dtype)
        o_ref[2] = pb[:n1].astype(o_ref.dtype)
        o_ref[3] = pb[n1:].astype(o_ref.dtype)
    else:
        o_ref[0] = (pa[:n1] - pb[n1:]).astype(o_ref.dtype)
        o_ref[1] = (pb[:n1] + pa[n1:]).astype(o_ref.dtype)


def _dft_cols(fstack, xa, ia, xb, ib, pair):
    n1 = fstack.shape[0] // 2
    _, nh, cols = xa.shape
    tc = _pick_tile(cols, (4096, 2048, 1024, 512, 256, 128))
    no = 4 if pair else 2
    return pl.pallas_call(
        functools.partial(_dft_cols_kernel, n1=n1, pair=pair),
        grid=(cols // tc,),
        in_specs=[pl.BlockSpec(fstack.shape, lambda j: (0, 0)),
                  pl.BlockSpec((1, nh, tc), lambda j: (ia, 0, j)),
                  pl.BlockSpec((1, nh, tc), lambda j: (ib, 0, j))],
        out_specs=pl.BlockSpec((no, n1, tc), lambda j: (0, 0, j)),
        out_shape=jax.ShapeDtypeStruct((no, n1, cols), BF16),
        compiler_params=_cparams(("parallel",)),
        name="dft_cols",
    )(fstack, xa, xb)


def _cplx_left(gs, zr, zi, n):
    c = zr.shape[1]
    p = _dotp(gs, jnp.concatenate([zr, zi], 1), P_DFT)
    return p[:n, :c] - p[n:, c:], p[:n, c:] + p[n:, :c]


def _spec_kernel(a_ref, g_ref, nrm_ref, o_ref, *, n_total):
    n2 = DFT_N2
    gs = jnp.concatenate([g_ref[0, 0], g_ref[0, 1]], 0)
    fr, fi = _cplx_left(gs, a_ref[0, 0], a_ref[1, 0], n2)
    br, bi = _cplx_left(gs, a_ref[2, 0], a_ref[3, 0], n2)
    s = 1.0 / (nrm_ref[...] * n_total)
    o_ref[0, 0] = ((fr + br) * s).astype(o_ref.dtype)
    o_ref[1, 0] = ((fi - bi) * s).astype(o_ref.dtype)


def _spec(a4, g, nrm, n_total):
    _, n1, n2, c2 = a4.shape
    return pl.pallas_call(
        functools.partial(_spec_kernel, n_total=float(n_total)),
        grid=(n1,),
        in_specs=[pl.BlockSpec((4, 1, n2, c2), lambda k: (0, k, 0, 0)),
                  pl.BlockSpec((1, 2, n2, n2), lambda k: (k, 0, 0, 0)),
                  pl.BlockSpec((1, c2), lambda k: (0, 0))],
        out_specs=pl.BlockSpec((2, 1, n2, c2), lambda k: (0, k, 0, 0)),
        out_shape=jax.ShapeDtypeStruct((2, n1, n2, c2), BF16),
        compiler_params=_cparams(("parallel",)),
        name="hy_spec",
    )(a4, g, nrm)


def _conv_mid_kernel(a_ref, g_ref, k_ref, o_ref):
    n2 = DFT_N2
    gs = jnp.concatenate([g_ref[0, 0], g_ref[0, 1]], 0)
    xr, xi = _cplx_left(gs, a_ref[0, 0], a_ref[1, 0], n2)
    c = xr.shape[1]
    kr = k_ref[0, 0].astype(F32)
    ki = k_ref[1, 0].astype(F32)
    zr = xr * kr - xi * ki
    zi = xr * ki + xi * kr
    zst = jnp.concatenate([jnp.concatenate([zr, zi], 1), jnp.concatenate([zi, -zr], 1)], 0)
    y = _dotp(gs, zst, P_DFT, _TN)
    o_ref[0, 0] = y[:, :c].astype(o_ref.dtype)
    o_ref[1, 0] = y[:, c:].astype(o_ref.dtype)


def _conv_mid(a, g, kspec, order):
    _, n1, n2, c = a.shape
    return pl.pallas_call(
        _conv_mid_kernel,
        grid=(n1,),
        in_specs=[pl.BlockSpec((2, 1, n2, c), lambda k: (0, k, 0, 0)),
                  pl.BlockSpec((1, 2, n2, n2), lambda k: (k, 0, 0, 0)),
                  pl.BlockSpec((2, 1, n2, c), lambda k: (0, k, 0, order))],
        out_specs=pl.BlockSpec((2, 1, n2, c), lambda k: (0, k, 0, 0)),
        out_shape=jax.ShapeDtypeStruct((2, n1, n2, c), BF16),
        compiler_params=_cparams(("parallel",)),
        name="hy_conv_mid",
    )(a, g, kspec)


def _idft_cols_kernel(c_ref, b_ref, g0_ref, g1_ref, x0_ref, x1_ref, bias_ref, o_ref, *, nh):
    cs = c_ref[...]
    pr = _dotp(cs, b_ref[0], P_DFT)
    pi = _dotp(cs, b_ref[1], P_DFT)
    yr = pr[:nh] - pi[nh:]
    yi = pi[:nh] + pr[nh:]
    bias = bias_ref[...]
    o_ref[0] = g0_ref[0] * (yr + x0_ref[0] * bias)
    o_ref[1] = g1_ref[0] * (yi + x1_ref[0] * bias)


def _idft_cols(cstack, bv, gate, xin, bias_cols):
    nh2, n1 = cstack.shape
    nh = nh2 // 2
    cols = bv.shape[-1]
    tc = _pick_tile(cols, (4096, 2048, 1024, 512, 256, 128))
    row = lambda bi: pl.BlockSpec((1, nh, tc), lambda j: (bi, 0, j))
    return pl.pallas_call(
        functools.partial(_idft_cols_kernel, nh=nh),
        grid=(cols // tc,),
        in_specs=[pl.BlockSpec((nh2, n1), lambda j: (0, 0)),
                  pl.BlockSpec((2, n1, tc), lambda j: (0, 0, j)),
                  row(0), row(1), row(0), row(1),
                  pl.BlockSpec((1, tc), lambda j: (0, j))],
        out_specs=pl.BlockSpec((2, nh, tc), lambda j: (0, 0, j)),
        out_shape=jax.ShapeDtypeStruct((2, nh, cols), F32),
        compiler_params=_cparams(("parallel",)),
        name="idft_cols",
    )(cstack, bv, gate, gate, xin, xin, bias_cols)


def _hy_ctx_kernel(x1_ref, x2_ref, v_ref, fw_ref, bw_ref, nrm_ref, bias_ref, ff_ref, ci_ref, o_ref, *, lc):
    n = 2 * lc
    c = HY_WIDTH
    ff = ff_ref[...]
    ci = ci_ref[...]
    pf = _mm(ff, fw_ref[...])
    pb = _mm(ff, bw_ref[...])
    s = 1.0 / (nrm_ref[...] * float(n))
    kr = (pf[:n] + pb[:n]) * s
    ki = (pf[n:] - pb[n:]) * s

    def conv(z0, z1, o):
        xr, xi = _cplx_left(ff, z0, z1, n)
        krr = kr[:, o * c:(o + 1) * c]
        kii = ki[:, o * c:(o + 1) * c]
        return _cplx_left(ci, xr * krr - xi * kii, xr * kii + xi * krr, lc)

    v0 = v_ref[0]
    v1 = v_ref[1]
    y0, y1 = conv(v0, v1, 0)
    z0 = x1_ref[0] * (y0 + v0 * bias_ref[0:1])
    z1 = x1_ref[1] * (y1 + v1 * bias_ref[0:1])
    y0, y1 = conv(z0, z1, 1)
    o_ref[0] = x2_ref[0] * (y0 + z0 * bias_ref[1:2])
    o_ref[1] = x2_ref[1] * (y1 + z1 * bias_ref[1:2])


def _hy_ctx(x1, x2, v, fw, bw, nrm, bias, ff, ci):
    b, lc, c = v.shape
    vm = pl.BlockSpec(memory_space=pltpu.VMEM)
    return pl.pallas_call(
        functools.partial(_hy_ctx_kernel, lc=lc),
        in_specs=[vm] * 9,
        out_specs=vm,
        out_shape=jax.ShapeDtypeStruct((b, lc, c), F32),
        compiler_params=pltpu.CompilerParams(vmem_limit_bytes=VMEM_LIMIT),
        name="hy_ctx",
    )(x1, x2, v, fw, bw, nrm, bias, ff, ci)


def _outproj_kernel(attc_ref, attl_ref, rw_ref, hyc_ref, hyl_ref, x_ref, mod_ref, w_ref, lg_ref, lb_ref, o_ref,
                    *, lc, tm):
    row0 = pl.program_id(1) * tm
    a0 = ATT_WIDTH
    a1 = ATT_WIDTH + RWKV_WIDTH
    is_ctx = row0 < lc
    att = jnp.where(is_ctx, attc_ref[0], attl_ref[0])
    hy = jnp.where(is_ctx, hyc_ref[0], hyl_ref[0])
    o = jnp.dot(att, w_ref[:a0], preferred_element_type=F32)
    o += jnp.dot(rw_ref[0], w_ref[a0:a1], preferred_element_type=F32)
    o += jnp.dot(hy.astype(BF16), w_ref[a1:], preferred_element_type=F32)
    g = _sel_mod(mod_ref, 2, row0, tm, lc)
    y = ALPHA * x_ref[0] + g * o
    o_ref[0] = _layer_norm(y) * lg_ref[...] + lb_ref[...]


def _outproj(att_c, att_l, rw, hy_c, hy_l, xx, mod, w, lg, lb, lc):
    b, lt, d = xx.shape
    tm = TOK_TILE
    nct = lc // tm
    tok = lambda c: pl.BlockSpec((1, tm, c), lambda bi, i: (bi, i, 0))
    ctx = lambda c: pl.BlockSpec((1, tm, c), lambda bi, i: (bi, jnp.minimum(i, nct - 1), 0))
    lat = lambda c: pl.BlockSpec((1, tm, c), lambda bi, i: (bi, jnp.maximum(i - nct, 0), 0))
    full = lambda shape: pl.BlockSpec(shape, lambda bi, i: (0,) * len(shape))
    return pl.pallas_call(
        functools.partial(_outproj_kernel, lc=lc, tm=tm),
        grid=(b, lt // tm),
        in_specs=[ctx(ATT_WIDTH), lat(ATT_WIDTH), tok(RWKV_WIDTH), ctx(HY_WIDTH), lat(HY_WIDTH), tok(d),
                  pl.BlockSpec((1, 2, 6, d), lambda bi, i: (bi, 0, 0, 0)),
                  full(w.shape), full((1, d)), full((1, d))],
        out_specs=tok(d),
        out_shape=jax.ShapeDtypeStruct((b, lt, d), F32),
        compiler_params=_cparams(("parallel", "parallel")),
        name="outproj",
    )(att_c, att_l, rw, hy_c, hy_l, xx, mod, w, lg, lb)


def _ffn_kernel(x_ref, mod_ref, w1_ref, w3_ref, w2_ref, lg_ref, lb_ref, o_ref, h_scr, acc_scr, *, lc, tm, nf):
    row0 = pl.program_id(1) * tm
    f = pl.program_id(2)

    @pl.when(f == 0)
    def _():
        sh = _sel_mod(mod_ref, 3, row0, tm, lc)
        sc = _sel_mod(mod_ref, 4, row0, tm, lc)
        h_scr[...] = (_layer_norm(x_ref[0]) * (1.0 + sc) + sh).astype(BF16)
        acc_scr[...] = jnp.zeros(acc_scr.shape, F32)

    h = h_scr[...]
    a = jnp.dot(h, w1_ref[...], preferred_element_type=F32)
    g = jnp.dot(h, w3_ref[...], preferred_element_type=F32)
    acc_scr[...] += jnp.dot((_silu(a) * g).astype(BF16), w2_ref[...], preferred_element_type=F32)

    @pl.when(f == nf - 1)
    def _():
        gate = _sel_mod(mod_ref, 5, row0, tm, lc)
        y = ALPHA * x_ref[0] + gate * acc_scr[...]
        o_ref[0] = _layer_norm(y) * lg_ref[...] + lb_ref[...]


def _ffn(xx, mod, w1, w3, w2, lg, lb, lc):
    b, lt, d = xx.shape
    ff = w1.shape[1]
    tm = _pick_tile(lt, (1280, 768, 512, 256))
    tf = _pick_tile(ff, (256, 128))
    nf = ff // tf
    return pl.pallas_call(
        functools.partial(_ffn_kernel, lc=lc, tm=tm, nf=nf),
        grid=(b, lt // tm, nf),
        in_specs=[pl.BlockSpec((1, tm, d), lambda bi, i, f: (bi, i, 0)),
                  pl.BlockSpec((1, 2, 6, d), lambda bi, i, f: (bi, 0, 0, 0)),
                  pl.BlockSpec((d, tf), lambda bi, i, f: (0, f)),
                  pl.BlockSpec((d, tf), lambda bi, i, f: (0, f)),
                  pl.BlockSpec((tf, d), lambda bi, i, f: (f, 0)),
                  pl.BlockSpec((1, d), lambda bi, i, f: (0, 0)),
                  pl.BlockSpec((1, d), lambda bi, i, f: (0, 0))],
        out_specs=pl.BlockSpec((1, tm, d), lambda bi, i, f: (bi, i, 0)),
        out_shape=jax.ShapeDtypeStruct((b, lt, d), F32),
        scratch_shapes=[pltpu.VMEM((tm, d), BF16), pltpu.VMEM((tm, d), F32)],
        compiler_params=_cparams(("parallel", "parallel", "arbitrary")),
        name="ffn",
    )(xx, mod, w1, w3, w2, lg, lb)


def _route(x_ref, mod_ref, wr_ref, row0, tm, lc):
    sh = _sel_mod(mod_ref, 3, row0, tm, lc)
    sc = _sel_mod(mod_ref, 4, row0, tm, lc)
    h = _layer_norm(x_ref[0]) * (1.0 + sc) + sh
    logits = _dotp(h, wr_ref[...], 3)
    lane = lax.broadcasted_iota(jnp.int32, logits.shape, 1)
    neg = jnp.float32(-jnp.inf)
    lg = jnp.where(lane < N_EXPERTS, logits, neg)
    m1 = jnp.max(lg, -1, keepdims=True)
    i1 = jnp.min(jnp.where(lg == m1, lane, LANES), -1, keepdims=True)
    lg2 = jnp.where(lane == i1, neg, lg)
    m2 = jnp.max(lg2, -1, keepdims=True)
    i2 = jnp.min(jnp.where(lg2 == m2, lane, LANES), -1, keepdims=True)
    e2 = jnp.exp(m2 - m1)
    return h, lane, i1, i2, 1.0 / (1.0 + e2), e2 / (1.0 + e2)


def _moe_sparse_kernel(x_ref, mod_ref, wr_ref, w1_ref, w3_ref, w2_ref, lg_ref, lb_ref, o_ref,
                       h_scr, acc_scr, gate_scr, posc_scr, posr_scr, xs_scr, ye_scr, nblk_scr, *, lc, tm, nf, ns, blk):
    row0 = pl.program_id(1) * tm
    s = pl.program_id(2)
    e = s // nf
    f = s % nf

    @pl.when(s == 0)
    def _():
        h, lane, i1, i2, g1, g2 = _route(x_ref, mod_ref, wr_ref, row0, tm, lc)
        h_scr[...] = h.astype(BF16)
        acc_scr[...] = jnp.zeros(acc_scr.shape, F32)
        routed = jnp.where(jnp.logical_or(lane == i1, lane == i2), 1.0, 0.0)
        r = lax.broadcasted_iota(jnp.int32, (tm, tm), 0)
        c = lax.broadcasted_iota(jnp.int32, (tm, tm), 1)
        before = jnp.where(c < r, 1.0, 0.0).astype(BF16)
        rank_c = jnp.dot(before, routed.astype(BF16), preferred_element_type=F32)
        gate_scr[...] = jnp.where(lane == i1, g1, 0.0) + jnp.where(lane == i2, g2, 0.0)
        posc_scr[...] = jnp.where(routed > 0.0, rank_c, -1.0)
        routed_t = routed.T[:2 * SUBLANES]
        after = jnp.where(r < c, 1.0, 0.0).astype(BF16)
        rank_r = jnp.dot(routed_t.astype(BF16), after, preferred_element_type=F32)
        posr_scr[...] = jnp.where(routed_t > 0.0, rank_r, -1.0)
        counts = jnp.sum(routed, 0, keepdims=True)
        lane1 = lax.broadcasted_iota(jnp.int32, counts.shape, 1)
        for ex in range(N_EXPERTS):
            n_rows = jnp.sum(jnp.where(lane1 == ex, counts, 0.0)).astype(jnp.int32)
            nblk_scr[ex] = (n_rows + (blk - 1)) // blk

    nblk = nblk_scr[e]
    mv = MOE_MOVE_BLOCK
    nmv = (nblk * blk + (mv - 1)) // mv
    rows = lambda i: pl.ds(pl.multiple_of(i * mv, mv), mv)

    @pl.when(f == 0)
    def _():
        def gather(i, carry):
            slot = (lax.broadcasted_iota(jnp.int32, (mv, tm), 0) + i * mv).astype(F32)
            take = jnp.where(slot == posr_scr[pl.ds(e, 1), :], 1.0, 0.0).astype(BF16)
            xs_scr[rows(i), :] = jnp.dot(take, h_scr[...], preferred_element_type=F32).astype(BF16)
            ye_scr[rows(i), :] = jnp.zeros((mv, ye_scr.shape[1]), F32)
            return carry

        lax.fori_loop(0, nmv, gather, 0)

    def expert(m):
        xs = xs_scr[:m]
        a = jnp.dot(xs, w1_ref[0].astype(BF16), preferred_element_type=F32)
        g = jnp.dot(xs, w3_ref[0].astype(BF16), preferred_element_type=F32)
        ye_scr[:m] += jnp.dot((_silu(a) * g).astype(BF16), w2_ref[0].astype(BF16), preferred_element_type=F32)

    for k in range(tm // blk):
        pl.when(nblk == k + 1)(functools.partial(expert, (k + 1) * blk))

    @pl.when(f == nf - 1)
    def _():
        lane = lax.broadcasted_iota(jnp.int32, (tm, LANES), 1)
        column = lambda ref: jnp.sum(jnp.where(lane == e, ref[...], 0.0), -1, keepdims=True)
        pos = column(posc_scr)
        gate = column(gate_scr)

        def scatter(i, carry):
            slot = (lax.broadcasted_iota(jnp.int32, (tm, mv), 1) + i * mv).astype(F32)
            put = jnp.where(slot == pos, 1.0, 0.0).astype(BF16)
            acc_scr[...] += gate * jnp.dot(put, ye_scr[rows(i), :].astype(BF16), preferred_element_type=F32)
            return carry

        lax.fori_loop(0, nmv, scatter, 0)

    @pl.when(s == ns - 1)
    def _():
        gate = _sel_mod(mod_ref, 5, row0, tm, lc)
        y = ALPHA * x_ref[0] + gate * acc_scr[...]
        o_ref[0] = _layer_norm(y) * lg_ref[...] + lb_ref[...]


def _moe_sparse(xx, mod, wr, w1, w3, w2, lg, lb, lc):
    b, lt, d = xx.shape
    ne, _, ff = w1.shape
    tm = _pick_tile(lt, (1280, 768, 512, 256))
    tf = _pick_tile(ff, (256, 128))
    nf = ff // tf
    ns = ne * nf
    return pl.pallas_call(
        functools.partial(_moe_sparse_kernel, lc=lc, tm=tm, nf=nf, ns=ns, blk=MOE_ROW_BLOCK),
        grid=(b, lt // tm, ns),
        in_specs=[pl.BlockSpec((1, tm, d), lambda bi, i, s: (bi, i, 0)),
                  pl.BlockSpec((1, 2, 6, d), lambda bi, i, s: (bi, 0, 0, 0)),
                  pl.BlockSpec((d, LANES), lambda bi, i, s: (0, 0)),
                  pl.BlockSpec((1, d, tf), lambda bi, i, s: (s // nf, 0, s % nf)),
                  pl.BlockSpec((1, d, tf), lambda bi, i, s: (s // nf, 0, s % nf)),
                  pl.BlockSpec((1, tf, d), lambda bi, i, s: (s // nf, s % nf, 0)),
                  pl.BlockSpec((1, d), lambda bi, i, s: (0, 0)),
                  pl.BlockSpec((1, d), lambda bi, i, s: (0, 0))],
        out_specs=pl.BlockSpec((1, tm, d), lambda bi, i, s: (bi, i, 0)),
        out_shape=jax.ShapeDtypeStruct((b, lt, d), F32),
        scratch_shapes=[pltpu.VMEM((tm, d), BF16), pltpu.VMEM((tm, d), F32),
                        pltpu.VMEM((tm, LANES), F32), pltpu.VMEM((tm, LANES), F32),
                        pltpu.VMEM((2 * SUBLANES, tm), F32),
                        pltpu.VMEM((tm, d), BF16), pltpu.VMEM((tm, d), F32),
                        pltpu.SMEM((ne,), jnp.int32)],
        compiler_params=_cparams(("parallel", "parallel", "arbitrary"), VMEM_LIMIT_MOE),
        name="moe_sparse",
    )(xx, mod, wr, w1, w3, w2, lg, lb)


def _rope_tables(l, lc):
    rows = l // GRID_W
    row = jnp.repeat(jnp.arange(rows, dtype=F32), GRID_W)
    col = jnp.tile(jnp.arange(GRID_W, dtype=F32), rows)
    n_freq = HEAD_DIM // 4
    inv_freq = ROPE_THETA ** (-jnp.arange(n_freq, dtype=F32) / n_freq)
    ang = jnp.concatenate([row[:, None] * inv_freq, col[:, None] * inv_freq], -1)
    cos, sin = jnp.cos(ang), jnp.sin(ang)
    cos2 = jnp.concatenate([jnp.ones((lc, LANES), F32), jnp.concatenate([cos, cos, cos, cos], -1)], 0)
    sin2 = jnp.concatenate([jnp.zeros((lc, LANES), F32), jnp.concatenate([-sin, sin, -sin, sin], -1)], 0)
    return cos2, sin2


def _hy_features(l):
    bands = (HY_EMB - 1) // 2
    t = jnp.linspace(0.0, 1.0, l, dtype=F32)[:, None]
    f = jnp.linspace(1e-4, bands - 1, bands, dtype=F32)[None, :]
    wt = 2.0 * math.pi * jnp.arange(l, dtype=F32)[:, None] / l
    z = jnp.concatenate([t, jnp.cos(f * wt), -jnp.sin(f * wt)], -1)
    return jnp.pad(z, ((0, 0), (0, LANES - HY_EMB)))


def _angle(idx, n):
    return (2.0 * math.pi / n) * (idx % n).astype(F32)


def _dft_tables(n1):
    nh = n1 // 2
    n2 = DFT_N2
    n = n1 * n2
    k1 = jnp.arange(n1, dtype=jnp.int32)
    a1 = _angle(k1[:, None] * jnp.arange(nh, dtype=jnp.int32)[None, :], n1)
    fstack = jnp.concatenate([jnp.cos(a1), -jnp.sin(a1)], 0)
    cstack = jnp.concatenate([jnp.cos(a1.T), jnp.sin(a1.T)], 0)
    k2 = jnp.arange(n2, dtype=jnp.int32)
    at = _angle(k1[:, None] * k2[None, :], n)
    tr, ti = jnp.cos(at), -jnp.sin(at)
    a2 = _angle(k2[:, None] * k2[None, :], n2)
    fr, fi = jnp.cos(a2), -jnp.sin(a2)
    g = jnp.stack([tr[:, None, :] * fr[None] - ti[:, None, :] * fi[None],
                   tr[:, None, :] * fi[None] + ti[:, None, :] * fr[None]], 1)
    return fstack, cstack, g


def _dense_dft_tables(lc):
    n = 2 * lc
    a = _angle(jnp.arange(n, dtype=jnp.int32)[:, None] * jnp.arange(lc, dtype=jnp.int32)[None, :], n)
    ff = jnp.concatenate([jnp.cos(a), -jnp.sin(a)], 0)
    ci = jnp.concatenate([jnp.cos(a.T), jnp.sin(a.T)], 0)
    return ff, ci


def kernel(x, c, ctx, c_ctx, ada_w, ada_b, w_in, w_out, q_gain, k_gain, rwkv_mu, rwkv_w0, rwkv_wB, rwkv_a0, rwkv_aB, rwkv_gB, rwkv_kk, rwkv_ka, rwkv_rk, rwkv_gn_g, rwkv_gn_b, hy_short_w, hy_short_b, hy_w1, hy_b1, hy_freq1, hy_w2, hy_b2, hy_freq2, hy_w3, hy_decay, hy_bias, ln1_g, ln1_b, ln2_g, ln2_b, ffn_w1, ffn_w3, ffn_w2, moe_router, moe_w1, moe_w3, moe_w2):
    b, l, d = x.shape
    lc = ctx.shape[1]
    lt = lc + l
    depth = ada_w.shape[0]
    assert b == 2, "the long convolution packs the two batch rows as one complex signal"
    assert d == D_MODEL and lc % TOK_TILE == 0 and l % TOK_TILE == 0 and (2 * l) % (2 * DFT_N2) == 0
    cw = RWKV_WIDTH

    xx = jnp.concatenate([ctx, x], 1)
    cond8 = jnp.zeros((SUBLANES, d), F32).at[:b].set(c).at[b].set(c_ctx)
    mod_all = _ada_mod(cond8, ada_w, ada_b)

    cos64, sin64 = _rope_tables(l, lc)
    n1 = 2 * l // DFT_N2
    nh = n1 // 2
    cols = DFT_N2 * HY_WIDTH
    fstack, cstack, g_tab = (t.astype(BF16) for t in _dft_tables(n1))
    ff_c, ci_c = _dense_dft_tables(lc)
    feat_l = _hy_features(l)
    feat_c = _hy_features(lc)
    blk = jnp.arange(cw) // HEAD_DIM
    bd = (blk[:, None] == blk[None, :]).astype(F32)
    ch = jnp.arange(ATT_WIDTH)
    bd_att = (ch[:, None] // HEAD_DIM == ch[None, :] // HEAD_DIM).astype(BF16)
    swap_att = (ch[:, None] == (ch[None, :] + HALF_HD) % HEAD_DIM + (ch[None, :] // HEAD_DIM) * HEAD_DIM
                ).astype(BF16)
    perm64 = jnp.concatenate([jnp.arange(0, HEAD_DIM, 2), jnp.arange(1, HEAD_DIM, 2)])
    perm_att = jnp.concatenate([h * HEAD_DIM + perm64 for h in range(ATT_HEADS + ATT_KV_HEADS)]
                               + [jnp.arange(ATT_WIDTH + ATT_KV_WIDTH, IN_ATT)])
    tq = TOK_TILE
    qn = next(n for n in (4, 2, 1) if (l // tq) % n == 0)
    tk = _pick_tile(lt, (3328, 1280, 1024, 768, 512, 256))

    for li in range(depth):
        ml = mod_all[li]
        mod = jnp.stack([jnp.broadcast_to(ml[b].reshape(1, 6, d), (b, 6, d)), ml[:b].reshape(b, 6, d)], 1)
        wi = w_in[li]
        w_pad = jnp.concatenate([wi[:, :IN_ATT][:, perm_att], wi[:, IN_ATT:IN_ATT + IN_RWKV],
                                 jnp.zeros((d, IN_RWKV_PAD - IN_RWKV), F32), wi[:, IN_ATT + IN_RWKV:]],
                                1).astype(BF16)
        ua, ur, uh = _inproj(xx, mod, w_pad, lc)

        two = lambda gain: jnp.tile(gain[perm64], 2)[None]
        qt, kx, vt = _attn_prep(ua, cos64, sin64, two(q_gain[li]), two(k_gain[li]), bd_att, swap_att, lc)
        att_c = _flash(qt, kx, vt, tq, TOK_TILE, 1, l // tq, lc // tq, lc // TOK_TILE)
        s_bound = (HEAD_DIM ** 0.5) * LOG2E * jnp.max(jnp.abs(q_gain[li])) * jnp.max(jnp.abs(k_gain[li]))
        lat_args = (qt, kx, vt, tq, tk, qn, 0, l // (qn * tq), lt // tk)
        att_l = lax.cond(s_bound <= MAX_UNSHIFTED_SCORE,
                         lambda: _flash(*lat_args, bounded=True), lambda: _flash(*lat_args, bounded=False))

        wl = jnp.zeros((LANES, 5 * cw), F32)
        wl = wl.at[0:W_LORA, 0:cw].set(rwkv_wB[li, 0]).at[W_LORA:2 * W_LORA, cw:2 * cw].set(rwkv_wB[li, 1])
        o_a = 2 * W_LORA
        wl = wl.at[o_a:o_a + A_LORA, 2 * cw:3 * cw].set(rwkv_aB[li, 0])
        wl = wl.at[o_a + A_LORA:o_a + 2 * A_LORA, 3 * cw:4 * cw].set(rwkv_aB[li, 1])
        o_g = o_a + 2 * A_LORA
        wl = wl.at[o_g:o_g + G_LORA, 4 * cw:5 * cw].set(rwkv_gB[li])
        mu = jnp.pad(rwkv_mu[li], ((0, 0), (0, IN_RWKV_PAD - IN_RWKV)))
        r_, v_, kk_, g_, bon_, lw_, be_, kd_ = _rwkv_prep(
            ur, mu, wl, bd, rwkv_w0[li], rwkv_a0[li], rwkv_kk[li][None], rwkv_ka[li][None],
            rwkv_rk[li].reshape(1, cw), lc)
        yf, yb = _wkv_scan(r_, v_, kk_, lw_, be_, kd_, lc)
        rw = _rwkv_out(yf, yb, bon_, g_, bd, rwkv_gn_g[li][None], rwkv_gn_b[li][None])

        x1, x2, vv = _hy_prep(uh, hy_short_w[li], hy_short_b[li][None], lc)
        w1p = jnp.pad(hy_w1[li], ((0, LANES - HY_EMB), (0, 0)))
        fargs = (w1p, hy_b1[li][None], hy_freq1[li][None], hy_w2[li], hy_b2[li][None], hy_freq2[li][None],
                 hy_w3[li], hy_decay[li][None])
        fw, bw, nrm = _hy_filter(feat_l, *fargs)
        c2 = HY_ORDER * HY_WIDTH
        a4 = _dft_cols(fstack, fw.reshape(1, nh, DFT_N2 * c2), 0, bw.reshape(1, nh, DFT_N2 * c2), 0, True)
        kspec = _spec(a4.reshape(4, n1, DFT_N2, c2), g_tab, nrm, n1 * DFT_N2)
        lat = lambda t: t[:, lc:].reshape(b, nh, cols)
        x1l, x2l, zin = lat(x1), lat(x2), lat(vv)
        for o, gate in enumerate((x1l, x2l)):
            a = _dft_cols(fstack, zin, 0, zin, 1, False)
            bv = _conv_mid(a.reshape(2, n1, DFT_N2, HY_WIDTH), g_tab, kspec, o)
            bias_cols = jnp.tile(hy_bias[li, o], DFT_N2)[None]
            zin = _idft_cols(cstack, bv.reshape(2, n1, cols), gate, zin, bias_cols)
        hy_l = zin.reshape(b, l, HY_WIDTH)
        fw_c, bw_c, nrm_c = _hy_filter(feat_c, *fargs)
        hy_c = _hy_ctx(x1[:, :lc], x2[:, :lc], vv[:, :lc], fw_c, bw_c, nrm_c, hy_bias[li], ff_c, ci_c)

        xx = _outproj(att_c, att_l, rw, hy_c, hy_l, xx, mod, w_out[li].astype(BF16),
                      ln1_g[li][None], ln1_b[li][None], lc)

        j = li // 2
        if li % 2 == 0:
            xx = _ffn(xx, mod, ffn_w1[j].astype(BF16), ffn_w3[j].astype(BF16), ffn_w2[j].astype(BF16),
                      ln2_g[li][None], ln2_b[li][None], lc)
        else:
            wr = jnp.pad(moe_router[j], ((0, 0), (0, LANES - N_EXPERTS)))
            xx = _moe_sparse(xx, mod, wr, moe_w1[j], moe_w3[j], moe_w2[j],
                             ln2_g[li][None], ln2_b[li][None], lc)
    return xx[:, lc:]
```

```python
import functools
import math

import jax
import jax.numpy as jnp
from jax import lax
from jax.experimental import pallas as pl
from jax.experimental.pallas import tpu as pltpu

F32 = jnp.float32
BF16 = jnp.bfloat16
HI = lax.Precision.HIGHEST

D_MODEL = 1024
DEPTH = 2
GRID_W = 64
HEAD_DIM = 64
HALF_HD = HEAD_DIM // 2
ATT_WIDTH = 512
RWKV_WIDTH = 256
HY_WIDTH = 256
ATT_HEADS = 8
ATT_KV_HEADS = 2
ATT_REP = 4
ATT_KV_WIDTH = 128
ROPE_THETA = 10000.0
QK_EPS = 1e-6
RWKV_HEADS = 4
W_LORA = 16
A_LORA = 16
G_LORA = 32
RWKV_GN_EPS = 64e-5
HY_ORDER = 2
HY_EMB = 33
HY_FFN = 64
N_EXPERTS = 8
LN_EPS = 1e-6
IN_ATT = ATT_WIDTH + 2 * ATT_KV_WIDTH
IN_RWKV = 3 * RWKV_WIDTH + 2 * W_LORA + 2 * A_LORA + G_LORA
IN_RWKV_PAD = 896
IN_HY = 3 * HY_WIDTH
ALPHA = float((2 * DEPTH) ** 0.25)
LOG2E = 1.4426950408889634
MAX_UNSHIFTED_SCORE = 40.0

LANES = 128
SUBLANES = 8
TOK_TILE = 256
WKV_CHUNK = 64
WKV_TILE = 256
DFT_N2 = 256
MOE_ROW_BLOCK = 128
MOE_MOVE_BLOCK = 256
VMEM_LIMIT = 48 * 1024 * 1024
VMEM_LIMIT_MOE = 56 * 1024 * 1024


def _cparams(sem, vmem=VMEM_LIMIT):
    return pltpu.CompilerParams(dimension_semantics=sem, vmem_limit_bytes=vmem)


def _pick_tile(n, cands):
    for c in cands:
        if n % c == 0:
            return c
    raise ValueError(f"no tile for {n} in {cands}")


def _layer_norm(x):
    mu = jnp.mean(x, -1, keepdims=True)
    xc = x - mu
    var = jnp.mean(xc * xc, -1, keepdims=True)
    return xc * lax.rsqrt(var + LN_EPS)


def _sel_mod(mod_ref, j, row0, tm, lc):
    rid = row0 + lax.broadcasted_iota(jnp.int32, (tm, 1), 0)
    return jnp.where(rid < lc, mod_ref[0, 0, j:j + 1, :], mod_ref[0, 1, j:j + 1, :])


def _silu(x):
    return x * jax.nn.sigmoid(x)


def _ada_kernel(c_ref, w_ref, b_ref, o_ref):
    s = _silu(c_ref[...])
    o_ref[0] = jnp.dot(s, w_ref[0], precision=HI, preferred_element_type=F32) + b_ref[0]


def _ada_mod(cond8, ada_w, ada_b):
    depth, d, n = ada_w.shape
    tn = _pick_tile(n, (1536, 1024, 512, 256, 128))
    return pl.pallas_call(
        _ada_kernel,
        grid=(depth, n // tn),
        in_specs=[pl.BlockSpec((SUBLANES, d), lambda l, j: (0, 0)),
                  pl.BlockSpec((1, d, tn), lambda l, j: (l, 0, j)),
                  pl.BlockSpec((1, 1, tn), lambda l, j: (l, 0, j))],
        out_specs=pl.BlockSpec((1, SUBLANES, tn), lambda l, j: (l, 0, j)),
        out_shape=jax.ShapeDtypeStruct((depth, SUBLANES, n), F32),
        compiler_params=_cparams(("parallel", "parallel")),
        name="ada_mod",
    )(cond8, ada_w, ada_b.reshape(depth, 1, n))


def _inproj_kernel(x_ref, mod_ref, w_ref, oa_ref, or_ref, oh_ref, *, lc, tm):
    row0 = pl.program_id(1) * tm
    sh = _sel_mod(mod_ref, 0, row0, tm, lc)
    sc = _sel_mod(mod_ref, 1, row0, tm, lc)
    h = (_layer_norm(x_ref[0]) * (1.0 + sc) + sh).astype(BF16)
    u = jnp.dot(h, w_ref[...], preferred_element_type=F32)
    oa_ref[0] = u[:, :IN_ATT]
    or_ref[0] = u[:, IN_ATT:IN_ATT + IN_RWKV_PAD]
    oh_ref[0] = u[:, IN_ATT + IN_RWKV_PAD:]


def _inproj(xx, mod, w_pad, lc):
    b, lt, d = xx.shape
    tm = TOK_TILE
    n = w_pad.shape[1]
    return pl.pallas_call(
        functools.partial(_inproj_kernel, lc=lc, tm=tm),
        grid=(b, lt // tm),
        in_specs=[pl.BlockSpec((1, tm, d), lambda bi, i: (bi, i, 0)),
                  pl.BlockSpec((1, 2, 6, d), lambda bi, i: (bi, 0, 0, 0)),
                  pl.BlockSpec((d, n), lambda bi, i: (0, 0))],
        out_specs=[pl.BlockSpec((1, tm, IN_ATT), lambda bi, i: (bi, i, 0)),
                   pl.BlockSpec((1, tm, IN_RWKV_PAD), lambda bi, i: (bi, i, 0)),
                   pl.BlockSpec((1, tm, IN_HY), lambda bi, i: (bi, i, 0))],
        out_shape=[jax.ShapeDtypeStruct((b, lt, IN_ATT), F32),
                   jax.ShapeDtypeStruct((b, lt, IN_RWKV_PAD), F32),
                   jax.ShapeDtypeStruct((b, lt, IN_HY), F32)],
        compiler_params=_cparams(("parallel", "parallel")),
        name="inproj",
    )(xx, mod, w_pad)


def _attn_prep_kernel(u_ref, cos_ref, sin_ref, qg_ref, kg_ref, bd_ref, sw_ref, qt_ref, k_ref, vt_ref):
    u = u_ref[0]

    def norm_rope(x, g):
        w = x.shape[1]
        tile = lambda t: jnp.concatenate([t] * (w // LANES), -1)
        ms = _dot01(x * x, bd_ref[:w, :w]) * (1.0 / HEAD_DIM)
        xn = x * lax.rsqrt(ms + QK_EPS) * tile(g)
        sw = _dot01(xn, sw_ref[:w, :w])
        return xn * tile(cos_ref[...]) + sw * tile(sin_ref[...])

    q = norm_rope(u[:, :ATT_WIDTH], qg_ref[...]) * (LOG2E * HEAD_DIM ** -0.5)
    qt = q.T
    kx = norm_rope(u[:, ATT_WIDTH:ATT_WIDTH + ATT_KV_WIDTH], kg_ref[...])
    for g in range(ATT_KV_HEADS):
        base = g * ATT_REP * HEAD_DIM
        qt_ref[0, g, 0] = jnp.concatenate(
            [qt[base + r * HEAD_DIM:base + (r + 1) * HEAD_DIM] for r in range(ATT_REP)], -1).astype(BF16)
        k_ref[0, g] = kx[:, g * HEAD_DIM:(g + 1) * HEAD_DIM].astype(BF16)
    v0 = ATT_WIDTH + ATT_KV_WIDTH
    vt = u[:, v0:v0 + ATT_KV_WIDTH].T
    for g in range(ATT_KV_HEADS):
        vt_ref[0, g] = vt[g * HEAD_DIM:(g + 1) * HEAD_DIM].astype(BF16)


def _attn_prep(ua, cos, sin, qg, kg, bd_att, swap_att, lc):
    b, lt, _ = ua.shape
    tm = TOK_TILE
    nct = lc // tm
    nl = lt // tm - nct
    q_pos = lambda i: jnp.where(i < nct, nl + i, i - nct)
    return pl.pallas_call(
        _attn_prep_kernel,
        grid=(b, lt // tm),
        in_specs=[pl.BlockSpec((1, tm, IN_ATT), lambda bi, i: (bi, i, 0)),
                  pl.BlockSpec((tm, LANES), lambda bi, i: (i, 0)),
                  pl.BlockSpec((tm, LANES), lambda bi, i: (i, 0)),
                  pl.BlockSpec((1, LANES), lambda bi, i: (0, 0)),
                  pl.BlockSpec((1, LANES), lambda bi, i: (0, 0)),
                  pl.BlockSpec((ATT_WIDTH, ATT_WIDTH), lambda bi, i: (0, 0)),
                  pl.BlockSpec((ATT_WIDTH, ATT_WIDTH), lambda bi, i: (0, 0))],
        out_specs=[pl.BlockSpec((1, ATT_KV_HEADS, 1, HEAD_DIM, ATT_REP * tm),
                                lambda bi, i: (bi, 0, q_pos(i), 0, 0)),
                   pl.BlockSpec((1, ATT_KV_HEADS, tm, HEAD_DIM), lambda bi, i: (bi, 0, i, 0)),
                   pl.BlockSpec((1, ATT_KV_HEADS, HEAD_DIM, tm), lambda bi, i: (bi, 0, 0, i))],
        out_shape=[jax.ShapeDtypeStruct((b, ATT_KV_HEADS, lt // tm, HEAD_DIM, ATT_REP * tm), BF16),
                   jax.ShapeDtypeStruct((b, ATT_KV_HEADS, lt, HEAD_DIM), BF16),
                   jax.ShapeDtypeStruct((b, ATT_KV_HEADS, HEAD_DIM, lt), BF16)],
        compiler_params=_cparams(("parallel", "parallel")),
        name="attn_prep",
    )(ua, cos, sin, qg, kg, bd_att, swap_att)


def _flash_kernel(qt_ref, k_ref, vt_ref, o_ref, m_scr, l_scr, acc_scr, *, nk, tq, sub, qn):
    j = pl.program_id(3)

    @pl.when(j == 0)
    def _():
        m_scr[...] = jnp.full(m_scr.shape, -jnp.inf, F32)
        l_scr[...] = jnp.zeros(l_scr.shape, F32)
        acc_scr[...] = jnp.zeros(acc_scr.shape, F32)

    qt = jnp.concatenate([qt_ref[0, 0, t] for t in range(qn)], -1)
    nsub = k_ref.shape[2] // sub
    m = m_scr[...]
    l = l_scr[...]
    acc = acc_scr[...]
    scores = lambda c: jnp.dot(k_ref[0, 0, c * sub:(c + 1) * sub, :], qt, preferred_element_type=F32)
    pv = lambda c, p: jnp.dot(vt_ref[0, 0, :, c * sub:(c + 1) * sub], p, preferred_element_type=F32)
    s_next = scores(0)
    pend = None
    for c in range(nsub):
        s = s_next
        if c + 1 < nsub:
            s_next = scores(c + 1)
        if pend is not None:
            acc = pend[0] * acc + pv(c - 1, pend[1])
        m_new = jnp.maximum(m, jnp.max(s, 0, keepdims=True))
        a = jnp.exp2(m - m_new)
        p = jnp.exp2(s - m_new)
        l = a * l + jnp.sum(p, 0, keepdims=True)
        pend = (a, p.astype(BF16))
        m = m_new
    acc = pend[0] * acc + pv(nsub - 1, pend[1])
    m_scr[...] = m
    l_scr[...] = l
    acc_scr[...] = acc

    @pl.when(j == nk - 1)
    def _():
        o = (acc / l).T
        for t in range(qn):
            o_ref[0, t * tq:(t + 1) * tq, :] = jnp.concatenate(
                [o[(t * ATT_REP + r) * tq:(t * ATT_REP + r + 1) * tq] for r in range(ATT_REP)], -1
            ).astype(o_ref.dtype)


def _flash_bounded_kernel(qt_ref, k_ref, vt_ref, o_ref, l_scr, acc_scr, *, nk, tq, sub, qn):
    j = pl.program_id(3)

    @pl.when(j == 0)
    def _():
        l_scr[...] = jnp.zeros(l_scr.shape, F32)
        acc_scr[...] = jnp.zeros(acc_scr.shape, F32)

    qt = jnp.concatenate([qt_ref[0, 0, t] for t in range(qn)], -1)
    nsub = k_ref.shape[2] // sub

    l = l_scr[...]
    acc = acc_scr[...]
    scores = lambda c: jnp.dot(k_ref[0, 0, c * sub:(c + 1) * sub, :], qt, preferred_element_type=F32)
    s_next = scores(0)
    for c in range(nsub):
        s = s_next
        if c + 1 < nsub:
            s_next = scores(c + 1)
        p = jnp.exp2(s)
        l = l + jnp.sum(p.reshape(sub // SUBLANES, SUBLANES, p.shape[1]), 0)
        acc = acc + jnp.dot(vt_ref[0, 0, :, c * sub:(c + 1) * sub], p.astype(BF16), preferred_element_type=F32)
    l_scr[...] = l
    acc_scr[...] = acc

    @pl.when(j == nk - 1)
    def _():
        o = (acc / jnp.sum(l, 0, keepdims=True)).T
        for t in range(qn):
            o_ref[0, t * tq:(t + 1) * tq, :] = jnp.concatenate(
                [o[(t * ATT_REP + r) * tq:(t * ATT_REP + r + 1) * tq] for r in range(ATT_REP)], -1
            ).astype(o_ref.dtype)


def _flash(qt, k, vt, tq, tk, qn, q_blk0, nq, nk, bounded=False):
    b = qt.shape[0]
    lq = nq * qn * tq
    sub = _pick_tile(tk, (256, 128))
    lanes = qn * ATT_REP * tq
    if bounded:
        body = functools.partial(_flash_bounded_kernel, nk=nk, tq=tq, sub=sub, qn=qn)
        scratch = [pltpu.VMEM((SUBLANES, lanes), F32), pltpu.VMEM((HEAD_DIM, lanes), F32)]
    else:
        body = functools.partial(_flash_kernel, nk=nk, tq=tq, sub=sub, qn=qn)
        scratch = [pltpu.VMEM((1, lanes), F32), pltpu.VMEM((1, lanes), F32), pltpu.VMEM((HEAD_DIM, lanes), F32)]
    return pl.pallas_call(
        body,
        grid=(b, ATT_KV_HEADS, nq, nk),
        in_specs=[pl.BlockSpec((1, 1, qn, HEAD_DIM, ATT_REP * tq), lambda bi, g, i, j: (bi, g, i + q_blk0, 0, 0)),
                  pl.BlockSpec((1, 1, tk, HEAD_DIM), lambda bi, g, i, j: (bi, g, j, 0)),
                  pl.BlockSpec((1, 1, HEAD_DIM, tk), lambda bi, g, i, j: (bi, g, 0, j))],
        out_specs=pl.BlockSpec((1, qn * tq, ATT_REP * HEAD_DIM), lambda bi, g, i, j: (bi, i, g)),
        out_shape=jax.ShapeDtypeStruct((b, lq, ATT_WIDTH), BF16),
        scratch_shapes=scratch,
        compiler_params=_cparams(("parallel", "parallel", "parallel", "arbitrary")),
        name="flash_bounded" if bounded else "flash",
    )(qt, k, vt)


def _prev_next(u, up8, un8, i, tm, lc, lt):
    start = i * tm
    p_ok = jnp.logical_and(start != 0, start != lc)
    n_ok = jnp.logical_and(start + tm != lc, start + tm != lt)
    prow = jnp.where(p_ok, up8[SUBLANES - 1:SUBLANES], 0.0)
    nrow = jnp.where(n_ok, un8[0:1], 0.0)
    rid = lax.broadcasted_iota(jnp.int32, u.shape, 0)
    prev = jnp.where(rid == 0, prow, pltpu.roll(u, 1, 0))
    nxt = jnp.where(rid == tm - 1, nrow, pltpu.roll(u, tm - 1, 0))
    return prev, nxt


def _halo_specs(tm, c, lt):
    r = tm // SUBLANES
    last = lt // SUBLANES - 1
    return [pl.BlockSpec((1, tm, c), lambda bi, i: (bi, i, 0)),
            pl.BlockSpec((1, SUBLANES, c), lambda bi, i: (bi, jnp.maximum(i * r - 1, 0), 0)),
            pl.BlockSpec((1, SUBLANES, c), lambda bi, i: (bi, jnp.minimum((i + 1) * r, last), 0))]


def _softplus(z):
    return jnp.maximum(z, 0.0) + jnp.log1p(jnp.exp(-jnp.abs(z)))


def _rwkv_prep_kernel(u_ref, up_ref, un_ref, mu_ref, wl_ref, bd_ref, w0_ref, a0_ref, kkw_ref, ka_ref, rk_ref,
                      r_o, v_o, kk_o, g_o, bon_o, lw_o, be_o, kd_o, *, lc, lt, tm):
    i = pl.program_id(1)
    u = u_ref[0]
    prev, nxt = _prev_next(u, up_ref[0], un_ref[0], i, tm, lc, lt)
    us = u + mu_ref[0:1] * (prev - u) + mu_ref[1:2] * (nxt - u)
    c = RWKV_WIDTH
    r = us[:, 0:c]
    k = us[:, c:2 * c]
    v = us[:, 2 * c:3 * c]
    slab = us[:, 3 * c:3 * c + LANES]
    lane = lax.broadcasted_iota(jnp.int32, slab.shape, 1)
    o_a = 2 * W_LORA
    o_g = o_a + 2 * A_LORA
    act = jnp.where(lane < o_a, jnp.tanh(slab),
                    jnp.where(lane < o_g, slab,
                              jnp.where(lane < o_g + G_LORA, jax.nn.sigmoid(slab), 0.0)))
    lo = _dotp(act, wl_ref[...], 3)
    bd = bd_ref[...]
    kk0 = k * kkw_ref[...]
    kk = kk0 * lax.rsqrt(_dot01(kk0 * kk0, bd) + 1e-12)
    r_o[0] = r
    v_o[0] = v
    kk_o[0] = kk
    g_o[0] = lo[:, 4 * c:5 * c]
    bon = jnp.zeros_like(r)
    for d in range(2):
        w_raw = w0_ref[d:d + 1] + lo[:, d * c:(d + 1) * c]
        lw = -jnp.exp(-_softplus(-w_raw) - 0.5)
        a = jax.nn.sigmoid(a0_ref[d:d + 1] + lo[:, (2 + d) * c:(3 + d) * c])
        kd = k * (1.0 + (a - 1.0) * ka_ref[...])
        lw_o[0, d] = lw
        be_o[0, d] = a * kk
        kd_o[0, d] = kd
        bon = bon + r * kd * rk_ref[...]
    bon_o[0] = _dot01(bon, bd) * v


def _rwkv_prep(ur, mu, wl, bd, w0, a0, kkw, ka, rk, lc):
    b, lt, cp = ur.shape
    tm = TOK_TILE
    c = RWKV_WIDTH
    full = lambda shape: pl.BlockSpec(shape, lambda bi, i: (0,) * len(shape))
    tok = pl.BlockSpec((1, tm, c), lambda bi, i: (bi, i, 0))
    tok2 = pl.BlockSpec((1, 2, tm, c), lambda bi, i: (bi, 0, i, 0))
    s1 = jax.ShapeDtypeStruct((b, lt, c), F32)
    s2 = jax.ShapeDtypeStruct((b, 2, lt, c), F32)
    return pl.pallas_call(
        functools.partial(_rwkv_prep_kernel, lc=lc, lt=lt, tm=tm),
        grid=(b, lt // tm),
        in_specs=_halo_specs(tm, cp, lt) + [full((2, cp)), full((LANES, 5 * c)), full((c, c)), full((2, c)),
                                            full((2, c)), full((1, c)), full((1, c)), full((1, c))],
        out_specs=[tok, tok, tok, tok, tok, tok2, tok2, tok2],
        out_shape=[s1, s1, s1, s1, s1, s2, s2, s2],
        compiler_params=_cparams(("parallel", "parallel")),
        name="rwkv_prep",
    )(ur, ur, ur, mu, wl, bd, w0, a0, kkw, ka, rk)


def _mm(a, b):
    return jnp.dot(a, b, precision=HI, preferred_element_type=F32)


_NN = ((1,), (0,))
_NT = ((1,), (1,))
_TN = ((0,), (0,))


def _split2(a):
    hi = a.astype(BF16)
    return hi, (a - hi.astype(F32)).astype(BF16)


def _dotp(a, b, passes, dims=_NN):
    if a.ndim == 3:
        dn = (((dims[0][0] + 1,), (dims[1][0] + 1,)), ((0,), (0,)))
    else:
        dn = (dims, ((), ()))
    dg = lambda p, q: lax.dot_general(p, q, dn, preferred_element_type=F32)
    if passes == 1:
        return dg(a.astype(BF16), b.astype(BF16))
    ah, al = _split2(a)
    bh, bl = _split2(b)
    return dg(ah, bh) + dg(ah, bl) + dg(al, bh)


def _dot01(a, ones):
    ah, al = _split2(a)
    o = ones.astype(BF16)
    return jnp.dot(ah, o, preferred_element_type=F32) + jnp.dot(al, o, preferred_element_type=F32)


P_M = 1
P_INV = 1
P_W = 1
P_Z = 1
P_STATE = 3
P_DFT = 1


def _unit_tri_inv(a_mat, row, col, eye, passes):
    t = a_mat.shape[-1]
    eye_f = eye.astype(F32)
    base = SUBLANES
    same = (row // base) == (col // base)
    n1 = -jnp.where(same, a_mat, 0.0)
    n2 = _dotp(n1, n1, passes)
    n4 = _dotp(n2, n2, passes)
    x = _dotp(_dotp(eye_f + n1, eye_f + n2, passes), eye_f + n4, passes)
    m = base
    while m < t:
        off = jnp.logical_and((row // (2 * m)) == (col // (2 * m)), (row // m) != (col // m))
        x = x - _dotp(x, _dotp(jnp.where(off, a_mat, 0.0), x, passes), passes)
        m *= 2
    return x


def _wkv_kernel(rf_ref, vf_ref, kkf_ref, rb_ref, vb_ref, kkb_ref, lwf_ref, bef_ref, kdf_ref,
                lwb_ref, beb_ref, kdb_ref, yf_ref, yb_ref, h_scr, pq_scr, ry_scr):
    t = WKV_CHUNK
    n = HEAD_DIM
    g = WKV_TILE // WKV_CHUNK
    tt = WKV_TILE
    nh = RWKV_HEADS
    nd = g * nh
    nu = 2 * nd
    orders = (list(range(g)), list(range(g - 1, -1, -1)))

    @pl.when(pl.program_id(1) == 0)
    def _():
        h_scr[...] = jnp.zeros(h_scr.shape, F32)
        pq_scr[...] = jnp.zeros(pq_scr.shape, F32)
        ry_scr[...] = jnp.zeros(ry_scr.shape, F32)

    hm = h_scr[...]
    for p in range(g):
        both = lambda ref: jnp.concatenate([ref[d * nd + p * nh:d * nd + (p + 1) * nh] for d in range(2)], 0)
        ry = both(ry_scr)
        pq = both(pq_scr)
        y = _dotp(ry[:, :, :n], hm, P_STATE) + ry[:, :, n:]
        hm = _dotp(pq[:, :, :n], hm, P_STATE) + pq[:, :, n:]
        for d, y_ref in enumerate((yf_ref, yb_ref)):
            c = orders[d][p]
            y_ref[0, c * t:(c + 1) * t, :] = jnp.concatenate([y[d * nh + h] for h in range(nh)], -1)
    h_scr[...] = hm

    row = lax.broadcasted_iota(jnp.int32, (tt, tt), 0)
    col = lax.broadcasted_iota(jnp.int32, (tt, tt), 1)
    same = (row // t) == (col // t)

    def scaled(d, r_ref, v_ref, kk_ref, lw_ref, be_ref, kd_ref):
        tri = jnp.logical_and(same, (row >= col) if d == 0 else (row <= col))
        sums = jnp.concatenate([jnp.where(tri, 1.0, 0.0), jnp.where(same, 1.0, 0.0)], 0).astype(BF16)
        lw = lw_ref[0, 0]
        l1 = lw.astype(BF16)
        rem = lw - l1.astype(F32)
        l2 = rem.astype(BF16)
        l3 = (rem - l2.astype(F32)).astype(BF16)
        cc = (jnp.dot(sums, l1, preferred_element_type=F32) + jnp.dot(sums, l2, preferred_element_type=F32)
              + jnp.dot(sums, l3, preferred_element_type=F32))
        cum = cc[:tt]
        ctot = cc[tt:]
        e_neg = jnp.exp(-cum)
        e_end = jnp.exp(ctot - cum)
        be = be_ref[0, 0]
        kd = kd_ref[0, 0]

        def units(x, rows=t):
            return jnp.stack([x[orders[d][p] * t:orders[d][p] * t + rows, h * n:(h + 1) * n]
                              for p in range(g) for h in range(nh)], 0)

        return dict(kap=units(kk_ref[0] * jnp.exp(cum - lw)), rt=units(r_ref[0] * jnp.exp(cum)), vh=units(v_ref[0]),
                    bet=units(be * e_neg), kdt=units(kd * e_neg), beh=units(be * e_end), kdh=units(kd * e_end),
                    gend=units(jnp.exp(ctot), 1))

    parts = (scaled(0, rf_ref, vf_ref, kkf_ref, lwf_ref, bef_ref, kdf_ref),
             scaled(1, rb_ref, vb_ref, kkb_ref, lwb_ref, beb_ref, kdb_ref))
    cat = lambda key: jnp.concatenate([parts[0][key], parts[1][key]], 0)
    kap, rt, vh = cat("kap"), cat("rt"), cat("vh")
    r64 = lax.broadcasted_iota(jnp.int32, (t, t), 0)
    c64 = lax.broadcasted_iota(jnp.int32, (t, t), 1)
    eye = r64 == c64
    unit = lax.broadcasted_iota(jnp.int32, (nu, t, t), 0)
    ahead = (r64 - c64)[None] * jnp.where(unit < nd, 1, -1)
    strict = ahead > 0
    incl = ahead >= 0
    m = _dotp(jnp.concatenate([kap, rt], 1), jnp.concatenate([cat("bet"), cat("kdt")], 1), P_M, _NT)
    a_mat = jnp.where(strict, m[:, :t, :t], 0.0)
    b_mat = jnp.where(strict, m[:, :t, t:], 0.0)
    ab_r = jnp.concatenate([jnp.where(incl, m[:, t:, :t], 0.0), jnp.where(incl, m[:, t:, t:], 0.0)], 2)
    tinv = _unit_tri_inv(a_mat, r64, c64, eye, P_INV)
    w = _dotp(tinv, jnp.concatenate([kap, _dotp(b_mat, vh, P_W)], 2), P_W)
    z = jnp.concatenate([-w, jnp.concatenate([jnp.zeros((nu, t, n), F32), vh], 2)], 1)
    ry = _dotp(ab_r, z, P_Z)
    ry_scr[:, :, :n] = ry[:, :, :n] + rt
    ry_scr[:, :, n:] = ry[:, :, n:]
    pq = _dotp(jnp.concatenate([cat("beh"), cat("kdh")], 1), z, P_Z, _TN)
    gd = jnp.where(eye, jnp.broadcast_to(cat("gend"), (nu, n, n)), 0.0)
    pq_scr[:, :, :n] = pq[:, :, :n] + gd
    pq_scr[:, :, n:] = pq[:, :, n:]


def _wkv_scan(r, v, kk, lw, be, kd, lc):
    b, lt, c = r.shape
    tt = WKV_TILE
    nt = lt // tt
    ntc = lc // tt
    tiles = (lambda i: i, lambda i: jnp.where(i < ntc, ntc - 1 - i, nt - 1 - (i - ntc)))
    t_in = lambda d: (lambda i: tiles[d](jnp.minimum(i, nt - 1)))
    t_out = lambda d: (lambda i: tiles[d](jnp.maximum(i - 1, 0)))
    one = lambda d: pl.BlockSpec((1, tt, c), lambda bi, i: (bi, t_in(d)(i), 0))
    two = lambda d: pl.BlockSpec((1, 1, tt, c), lambda bi, i: (bi, d, t_in(d)(i), 0))
    out = lambda d: pl.BlockSpec((1, tt, c), lambda bi, i: (bi, t_out(d)(i), 0))
    nu = 2 * (tt // WKV_CHUNK) * RWKV_HEADS
    ysh = jax.ShapeDtypeStruct((b, lt, c), F32)
    return pl.pallas_call(
        _wkv_kernel,
        grid=(b, nt + 1),
        in_specs=[one(0)] * 3 + [one(1)] * 3 + [two(0)] * 3 + [two(1)] * 3,
        out_specs=[out(0), out(1)],
        out_shape=[ysh, ysh],
        scratch_shapes=[pltpu.VMEM((2 * RWKV_HEADS, HEAD_DIM, HEAD_DIM), F32),
                        pltpu.VMEM((nu, HEAD_DIM, 2 * HEAD_DIM), F32),
                        pltpu.VMEM((nu, WKV_CHUNK, 2 * HEAD_DIM), F32)],
        compiler_params=_cparams(("parallel", "arbitrary")),
        name="wkv_scan",
    )(r, v, kk, r, v, kk, lw, be, kd, lw, be, kd)


def _rwkv_out_kernel(yf_ref, yb_ref, bon_ref, g_ref, bd_ref, gg_ref, gb_ref, o_ref):
    y = yf_ref[0] + yb_ref[0] + bon_ref[0]
    bd = bd_ref[...]
    mu = _dot01(y, bd) * (1.0 / HEAD_DIM)
    yc = y - mu
    var = _dot01(yc * yc, bd) * (1.0 / HEAD_DIM)
    yn = yc * lax.rsqrt(var + RWKV_GN_EPS) * gg_ref[...] + gb_ref[...]
    o_ref[0] = (yn * g_ref[0]).astype(o_ref.dtype)


def _rwkv_out(yf, yb, bon, g, bd, gg, gb):
    b, lt, c = yf.shape
    tm = TOK_TILE
    tok = pl.BlockSpec((1, tm, c), lambda bi, i: (bi, i, 0))
    full = lambda shape: pl.BlockSpec(shape, lambda bi, i: (0,) * len(shape))
    return pl.pallas_call(
        _rwkv_out_kernel,
        grid=(b, lt // tm),
        in_specs=[tok, tok, tok, tok, full((c, c)), full((1, c)), full((1, c))],
        out_specs=tok,
        out_shape=jax.ShapeDtypeStruct((b, lt, c), BF16),
        compiler_params=_cparams(("parallel", "parallel")),
        name="rwkv_out",
    )(yf, yb, bon, g, bd, gg, gb)


def _hy_prep_kernel(u_ref, up_ref, un_ref, w_ref, b_ref, x1_o, x2_o, v_o, *, lc, lt, tm):
    i = pl.program_id(1)
    u = u_ref[0]
    prev, nxt = _prev_next(u, up_ref[0], un_ref[0], i, tm, lc, lt)
    y = prev * w_ref[0:1] + u * w_ref[1:2] + nxt * w_ref[2:3] + b_ref[...]
    c = HY_WIDTH
    x1_o[0] = y[:, :c]
    x2_o[0] = y[:, c:2 * c]
    v_o[0] = y[:, 2 * c:]


def _hy_prep(uh, w, bias, lc):
    b, lt, cin = uh.shape
    tm = TOK_TILE
    c = HY_WIDTH
    full = lambda shape: pl.BlockSpec(shape, lambda bi, i: (0,) * len(shape))
    tok = pl.BlockSpec((1, tm, c), lambda bi, i: (bi, i, 0))
    s1 = jax.ShapeDtypeStruct((b, lt, c), F32)
    return pl.pallas_call(
        functools.partial(_hy_prep_kernel, lc=lc, lt=lt, tm=tm),
        grid=(b, lt // tm),
        in_specs=_halo_specs(tm, cin, lt) + [full((3, cin)), full((1, cin))],
        out_specs=[tok, tok, tok],
        out_shape=[s1, s1, s1],
        compiler_params=_cparams(("parallel", "parallel")),
        name="hy_prep",
    )(uh, uh, uh, w, bias)


def _hy_filter_kernel(z_ref, w1_ref, b1_ref, f1_ref, w2_ref, b2_ref, f2_ref, w3_ref, dec_ref,
                      fw_o, bw_o, nrm_o, *, tl):
    i = pl.program_id(0)
    z = z_ref[...]
    h = jnp.sin(f1_ref[...] * (_dotp(z, w1_ref[...], 3) + b1_ref[...]))
    h = jnp.sin(f2_ref[...] * (_dotp(h, w2_ref[...], 3) + b2_ref[...]))
    h = _dotp(h, w3_ref[...], 3) * jnp.exp(-z[:, 0:1] * dec_ref[...])
    c = HY_WIDTH
    fw = jnp.concatenate([h[:, 0:c], h[:, 2 * c:3 * c]], 1)
    bw = jnp.concatenate([h[:, c:2 * c], h[:, 3 * c:4 * c]], 1)
    rid = i * tl + lax.broadcasted_iota(jnp.int32, (tl, 1), 0)
    bw = jnp.where(rid == 0, 0.0, bw)
    fw_o[...] = fw
    bw_o[...] = bw

    @pl.when(i == 0)
    def _():
        nrm_o[...] = jnp.zeros(nrm_o.shape, F32)

    nrm_o[...] += jnp.sum(jnp.abs(fw) + jnp.abs(bw), 0, keepdims=True)


def _hy_filter(feat, w1p, b1, f1, w2, b2, f2, w3, dec):
    l, fe = feat.shape
    tl = _pick_tile(l, (512, 256))
    c2 = HY_ORDER * HY_WIDTH
    full = lambda shape: pl.BlockSpec(shape, lambda i: (0,) * len(shape))
    return pl.pallas_call(
        functools.partial(_hy_filter_kernel, tl=tl),
        grid=(l // tl,),
        in_specs=[pl.BlockSpec((tl, fe), lambda i: (i, 0)), full(w1p.shape), full(b1.shape), full(f1.shape),
                  full(w2.shape), full(b2.shape), full(f2.shape), full(w3.shape), full(dec.shape)],
        out_specs=[pl.BlockSpec((tl, c2), lambda i: (i, 0)), pl.BlockSpec((tl, c2), lambda i: (i, 0)),
                   pl.BlockSpec((1, c2), lambda i: (0, 0))],
        out_shape=[jax.ShapeDtypeStruct((l, c2), F32), jax.ShapeDtypeStruct((l, c2), F32),
                   jax.ShapeDtypeStruct((1, c2), F32)],
        compiler_params=_cparams(("arbitrary",)),
        name="hy_filter",
    )(feat, w1p, b1, f1, w2, b2, f2, w3, dec)


def _dft_cols_kernel(f_ref, xa_ref, xb_ref, o_ref, *, n1, pair):
    f = f_ref[...]
    pa = _dotp(f, xa_ref[0], P_DFT)
    pb = _dotp(f, xb_ref[0], P_DFT)
    if pair:
        o_ref[0] = pa[:n1].astype(o_ref.dtype)
        o_ref[1] = pa[n1:].astype(o_ref.dtype)
        o_ref[2] = pb[:n1].astype(o_ref.dtype)
        o_ref[3] = pb[n1:].astype(o_ref.dtype)
    else:
        o_ref[0] = (pa[:n1] - pb[n1:]).astype(o_ref.dtype)
        o_ref[1] = (pb[:n1] + pa[n1:]).astype(o_ref.dtype)


def _dft_cols(fstack, xa, ia, xb, ib, pair):
    n1 = fstack.shape[0] // 2
    _, nh, cols = xa.shape
    tc = _pick_tile(cols, (4096, 2048, 1024, 512, 256, 128))
    no = 4 if pair else 2
    return pl.pallas_call(
        functools.partial(_dft_cols_kernel, n1=n1, pair=pair),
        grid=(cols // tc,),
        in_specs=[pl.BlockSpec(fstack.shape, lambda j: (0, 0)),
                  pl.BlockSpec((1, nh, tc), lambda j: (ia, 0, j)),
                  pl.BlockSpec((1, nh, tc), lambda j: (ib, 0, j))],
        out_specs=pl.BlockSpec((no, n1, tc), lambda j: (0, 0, j)),
        out_shape=jax.ShapeDtypeStruct((no, n1, cols), BF16),
        compiler_params=_cparams(("parallel",)),
        name="dft_cols",
    )(fstack, xa, xb)


def _cplx_left(gs, zr, zi, n):
    c = zr.shape[1]
    p = _dotp(gs, jnp.concatenate([zr, zi], 1), P_DFT)
    return p[:n, :c] - p[n:, c:], p[:n, c:] + p[n:, :c]


def _spec_kernel(a_ref, g_ref, nrm_ref, o_ref, *, n_total):
    n2 = DFT_N2
    gs = jnp.concatenate([g_ref[0, 0], g_ref[0, 1]], 0)
    fr, fi = _cplx_left(gs, a_ref[0, 0], a_ref[1, 0], n2)
    br, bi = _cplx_left(gs, a_ref[2, 0], a_ref[3, 0], n2)
    s = 1.0 / (nrm_ref[...] * n_total)
    o_ref[0, 0] = ((fr + br) * s).astype(o_ref.dtype)
    o_ref[1, 0] = ((fi - bi) * s).astype(o_ref.dtype)


def _spec(a4, g, nrm, n_total):
    _, n1, n2, c2 = a4.shape
    return pl.pallas_call(
        functools.partial(_spec_kernel, n_total=float(n_total)),
        grid=(n1,),
        in_specs=[pl.BlockSpec((4, 1, n2, c2), lambda k: (0, k, 0, 0)),
                  pl.BlockSpec((1, 2, n2, n2), lambda k: (k, 0, 0, 0)),
                  pl.BlockSpec((1, c2), lambda k: (0, 0))],
        out_specs=pl.BlockSpec((2, 1, n2, c2), lambda k: (0, k, 0, 0)),
        out_shape=jax.ShapeDtypeStruct((2, n1, n2, c2), BF16),
        compiler_params=_cparams(("parallel",)),
        name="hy_spec",
    )(a4, g, nrm)


def _conv_mid_kernel(a_ref, g_ref, k_ref, o_ref):
    n2 = DFT_N2
    gs = jnp.concatenate([g_ref[0, 0], g_ref[0, 1]], 0)
    xr, xi = _cplx_left(gs, a_ref[0, 0], a_ref[1, 0], n2)
    c = xr.shape[1]
    kr = k_ref[0, 0].astype(F32)
    ki = k_ref[1, 0].astype(F32)
    zr = xr * kr - xi * ki
    zi = xr * ki + xi * kr
    zst = jnp.concatenate([jnp.concatenate([zr, zi], 1), jnp.concatenate([zi, -zr], 1)], 0)
    y = _dotp(gs, zst, P_DFT, _TN)
    o_ref[0, 0] = y[:, :c].astype(o_ref.dtype)
    o_ref[1, 0] = y[:, c:].astype(o_ref.dtype)


def _conv_mid(a, g, kspec, order):
    _, n1, n2, c = a.shape
    return pl.pallas_call(
        _conv_mid_kernel,
        grid=(n1,),
        in_specs=[pl.BlockSpec((2, 1, n2, c), lambda k: (0, k, 0, 0)),
                  pl.BlockSpec((1, 2, n2, n2), lambda k: (k, 0, 0, 0)),
                  pl.BlockSpec((2, 1, n2, c), lambda k: (0, k, 0, order))],
        out_specs=pl.BlockSpec((2, 1, n2, c), lambda k: (0, k, 0, 0)),
        out_shape=jax.ShapeDtypeStruct((2, n1, n2, c), BF16),
        compiler_params=_cparams(("parallel",)),
        name="hy_conv_mid",
    )(a, g, kspec)


def _idft_cols_kernel(c_ref, b_ref, g0_ref, g1_ref, x0_ref, x1_ref, bias_ref, o_ref, *, nh):
    cs = c_ref[...]
    pr = _dotp(cs, b_ref[0], P_DFT)
    pi = _dotp(cs, b_ref[1], P_DFT)
    yr = pr[:nh] - pi[nh:]
    yi = pi[:nh] + pr[nh:]
    bias = bias_ref[...]
    o_ref[0] = g0_ref[0] * (yr + x0_ref[0] * bias)
    o_ref[1] = g1_ref[0] * (yi + x1_ref[0] * bias)


def _idft_cols(cstack, bv, gate, xin, bias_cols):
    nh2, n1 = cstack.shape
    nh = nh2 // 2
    cols = bv.shape[-1]
    tc = _pick_tile(cols, (4096, 2048, 1024, 512, 256, 128))
    row = lambda bi: pl.BlockSpec((1, nh, tc), lambda j: (bi, 0, j))
    return pl.pallas_call(
        functools.partial(_idft_cols_kernel, nh=nh),
        grid=(cols // tc,),
        in_specs=[pl.BlockSpec((nh2, n1), lambda j: (0, 0)),
                  pl.BlockSpec((2, n1, tc), lambda j: (0, 0, j)),
                  row(0), row(1), row(0), row(1),
                  pl.BlockSpec((1, tc), lambda j: (0, j))],
        out_specs=pl.BlockSpec((2, nh, tc), lambda j: (0, 0, j)),
        out_shape=jax.ShapeDtypeStruct((2, nh, cols), F32),
        compiler_params=_cparams(("parallel",)),
        name="idft_cols",
    )(cstack, bv, gate, gate, xin, xin, bias_cols)


def _hy_ctx_kernel(x1_ref, x2_ref, v_ref, fw_ref, bw_ref, nrm_ref, bias_ref, ff_ref, ci_ref, o_ref, *, lc):
    n = 2 * lc
    c = HY_WIDTH
    ff = ff_ref[...]
    ci = ci_ref[...]
    pf = _mm(ff, fw_ref[...])
    pb = _mm(ff, bw_ref[...])
    s = 1.0 / (nrm_ref[...] * float(n))
    kr = (pf[:n] + pb[:n]) * s
    ki = (pf[n:] - pb[n:]) * s

    def conv(z0, z1, o):
        xr, xi = _cplx_left(ff, z0, z1, n)
        krr = kr[:, o * c:(o + 1) * c]
        kii = ki[:, o * c:(o + 1) * c]
        return _cplx_left(ci, xr * krr - xi * kii, xr * kii + xi * krr, lc)

    v0 = v_ref[0]
    v1 = v_ref[1]
    y0, y1 = conv(v0, v1, 0)
    z0 = x1_ref[0] * (y0 + v0 * bias_ref[0:1])
    z1 = x1_ref[1] * (y1 + v1 * bias_ref[0:1])
    y0, y1 = conv(z0, z1, 1)
    o_ref[0] = x2_ref[0] * (y0 + z0 * bias_ref[1:2])
    o_ref[1] = x2_ref[1] * (y1 + z1 * bias_ref[1:2])


def _hy_ctx(x1, x2, v, fw, bw, nrm, bias, ff, ci):
    b, lc, c = v.shape
    vm = pl.BlockSpec(memory_space=pltpu.VMEM)
    return pl.pallas_call(
        functools.partial(_hy_ctx_kernel, lc=lc),
        in_specs=[vm] * 9,
        out_specs=vm,
        out_shape=jax.ShapeDtypeStruct((b, lc, c), F32),
        compiler_params=pltpu.CompilerParams(vmem_limit_bytes=VMEM_LIMIT),
        name="hy_ctx",
    )(x1, x2, v, fw, bw, nrm, bias, ff, ci)


def _outproj_kernel(attc_ref, attl_ref, rw_ref, hyc_ref, hyl_ref, x_ref, mod_ref, w_ref, lg_ref, lb_ref, o_ref,
                    *, lc, tm):
    row0 = pl.program_id(1) * tm
    a0 = ATT_WIDTH
    a1 = ATT_WIDTH + RWKV_WIDTH
    is_ctx = row0 < lc
    att = jnp.where(is_ctx, attc_ref[0], attl_ref[0])
    hy = jnp.where(is_ctx, hyc_ref[0], hyl_ref[0])
    o = jnp.dot(att, w_ref[:a0], preferred_element_type=F32)
    o += jnp.dot(rw_ref[0], w_ref[a0:a1], preferred_element_type=F32)
    o += jnp.dot(hy.astype(BF16), w_ref[a1:], preferred_element_type=F32)
    g = _sel_mod(mod_ref, 2, row0, tm, lc)
    y = ALPHA * x_ref[0] + g * o
    o_ref[0] = _layer_norm(y) * lg_ref[...] + lb_ref[...]


def _outproj(att_c, att_l, rw, hy_c, hy_l, xx, mod, w, lg, lb, lc):
    b, lt, d = xx.shape
    tm = TOK_TILE
    nct = lc // tm
    tok = lambda c: pl.BlockSpec((1, tm, c), lambda bi, i: (bi, i, 0))
    ctx = lambda c: pl.BlockSpec((1, tm, c), lambda bi, i: (bi, jnp.minimum(i, nct - 1), 0))
    lat = lambda c: pl.BlockSpec((1, tm, c), lambda bi, i: (bi, jnp.maximum(i - nct, 0), 0))
    full = lambda shape: pl.BlockSpec(shape, lambda bi, i: (0,) * len(shape))
    return pl.pallas_call(
        functools.partial(_outproj_kernel, lc=lc, tm=tm),
        grid=(b, lt // tm),
        in_specs=[ctx(ATT_WIDTH), lat(ATT_WIDTH), tok(RWKV_WIDTH), ctx(HY_WIDTH), lat(HY_WIDTH), tok(d),
                  pl.BlockSpec((1, 2, 6, d), lambda bi, i: (bi, 0, 0, 0)),
                  full(w.shape), full((1, d)), full((1, d))],
        out_specs=tok(d),
        out_shape=jax.ShapeDtypeStruct((b, lt, d), F32),
        compiler_params=_cparams(("parallel", "parallel")),
        name="outproj",
    )(att_c, att_l, rw, hy_c, hy_l, xx, mod, w, lg, lb)


def _ffn_kernel(x_ref, mod_ref, w1_ref, w3_ref, w2_ref, lg_ref, lb_ref, o_ref, h_scr, acc_scr, *, lc, tm, nf):
    row0 = pl.program_id(1) * tm
    f = pl.program_id(2)

    @pl.when(f == 0)
    def _():
        sh = _sel_mod(mod_ref, 3, row0, tm, lc)
        sc = _sel_mod(mod_ref, 4, row0, tm, lc)
        h_scr[...] = (_layer_norm(x_ref[0]) * (1.0 + sc) + sh).astype(BF16)
        acc_scr[...] = jnp.zeros(acc_scr.shape, F32)

    h = h_scr[...]
    a = jnp.dot(h, w1_ref[...], preferred_element_type=F32)
    g = jnp.dot(h, w3_ref[...], preferred_element_type=F32)
    acc_scr[...] += jnp.dot((_silu(a) * g).astype(BF16), w2_ref[...], preferred_element_type=F32)

    @pl.when(f == nf - 1)
    def _():
        gate = _sel_mod(mod_ref, 5, row0, tm, lc)
        y = ALPHA * x_ref[0] + gate * acc_scr[...]
        o_ref[0] = _layer_norm(y) * lg_ref[...] + lb_ref[...]


def _ffn(xx, mod, w1, w3, w2, lg, lb, lc):
    b, lt, d = xx.shape
    ff = w1.shape[1]
    tm = _pick_tile(lt, (1280, 768, 512, 256))
    tf = _pick_tile(ff, (256, 128))
    nf = ff // tf
    return pl.pallas_call(
        functools.partial(_ffn_kernel, lc=lc, tm=tm, nf=nf),
        grid=(b, lt // tm, nf),
        in_specs=[pl.BlockSpec((1, tm, d), lambda bi, i, f: (bi, i, 0)),
                  pl.BlockSpec((1, 2, 6, d), lambda bi, i, f: (bi, 0, 0, 0)),
                  pl.BlockSpec((d, tf), lambda bi, i, f: (0, f)),
                  pl.BlockSpec((d, tf), lambda bi, i, f: (0, f)),
                  pl.BlockSpec((tf, d), lambda bi, i, f: (f, 0)),
                  pl.BlockSpec((1, d), lambda bi, i, f: (0, 0)),
                  pl.BlockSpec((1, d), lambda bi, i, f: (0, 0))],
        out_specs=pl.BlockSpec((1, tm, d), lambda bi, i, f: (bi, i, 0)),
        out_shape=jax.ShapeDtypeStruct((b, lt, d), F32),
        scratch_shapes=[pltpu.VMEM((tm, d), BF16), pltpu.VMEM((tm, d), F32)],
        compiler_params=_cparams(("parallel", "parallel", "arbitrary")),
        name="ffn",
    )(xx, mod, w1, w3, w2, lg, lb)


def _route(x_ref, mod_ref, wr_ref, row0, tm, lc):
    sh = _sel_mod(mod_ref, 3, row0, tm, lc)
    sc = _sel_mod(mod_ref, 4, row0, tm, lc)
    h = _layer_norm(x_ref[0]) * (1.0 + sc) + sh
    logits = _dotp(h, wr_ref[...], 3)
    lane = lax.broadcasted_iota(jnp.int32, logits.shape, 1)
    neg = jnp.float32(-jnp.inf)
    lg = jnp.where(lane < N_EXPERTS, logits, neg)
    m1 = jnp.max(lg, -1, keepdims=True)
    i1 = jnp.min(jnp.where(lg == m1, lane, LANES), -1, keepdims=True)
    lg2 = jnp.where(lane == i1, neg, lg)
    m2 = jnp.max(lg2, -1, keepdims=True)
    i2 = jnp.min(jnp.where(lg2 == m2, lane, LANES), -1, keepdims=True)
    e2 = jnp.exp(m2 - m1)
    return h, lane, i1, i2, 1.0 / (1.0 + e2), e2 / (1.0 + e2)


def _moe_sparse_kernel(x_ref, mod_ref, wr_ref, w1_ref, w3_ref, w2_ref, lg_ref, lb_ref, o_ref,
                       h_scr, acc_scr, gate_scr, posc_scr, posr_scr, xs_scr, ye_scr, nblk_scr, *, lc, tm, nf, ns, blk):
    row0 = pl.program_id(1) * tm
    s = pl.program_id(2)
    e = s // nf
    f = s % nf

    @pl.when(s == 0)
    def _():
        h, lane, i1, i2, g1, g2 = _route(x_ref, mod_ref, wr_ref, row0, tm, lc)
        h_scr[...] = h.astype(BF16)
        acc_scr[...] = jnp.zeros(acc_scr.shape, F32)
        routed = jnp.where(jnp.logical_or(lane == i1, lane == i2), 1.0, 0.0)
        r = lax.broadcasted_iota(jnp.int32, (tm, tm), 0)
        c = lax.broadcasted_iota(jnp.int32, (tm, tm), 1)
        before = jnp.where(c < r, 1.0, 0.0).astype(BF16)
        rank_c = jnp.dot(before, routed.astype(BF16), preferred_element_type=F32)
        gate_scr[...] = jnp.where(lane == i1, g1, 0.0) + jnp.where(lane == i2, g2, 0.0)
        posc_scr[...] = jnp.where(routed > 0.0, rank_c, -1.0)
        routed_t = routed.T[:2 * SUBLANES]
        after = jnp.where(r < c, 1.0, 0.0).astype(BF16)
        rank_r = jnp.dot(routed_t.astype(BF16), after, preferred_element_type=F32)
        posr_scr[...] = jnp.where(routed_t > 0.0, rank_r, -1.0)
        counts = jnp.sum(routed, 0, keepdims=True)
        lane1 = lax.broadcasted_iota(jnp.int32, counts.shape, 1)
        for ex in range(N_EXPERTS):
            n_rows = jnp.sum(jnp.where(lane1 == ex, counts, 0.0)).astype(jnp.int32)
            nblk_scr[ex] = (n_rows + (blk - 1)) // blk

    nblk = nblk_scr[e]
    mv = MOE_MOVE_BLOCK
    nmv = (nblk * blk + (mv - 1)) // mv
    rows = lambda i: pl.ds(pl.multiple_of(i * mv, mv), mv)

    @pl.when(f == 0)
    def _():
        def gather(i, carry):
            slot = (lax.broadcasted_iota(jnp.int32, (mv, tm), 0) + i * mv).astype(F32)
            take = jnp.where(slot == posr_scr[pl.ds(e, 1), :], 1.0, 0.0).astype(BF16)
            xs_scr[rows(i), :] = jnp.dot(take, h_scr[...], preferred_element_type=F32).astype(BF16)
            ye_scr[rows(i), :] = jnp.zeros((mv, ye_scr.shape[1]), F32)
            return carry

        lax.fori_loop(0, nmv, gather, 0)

    def expert(m):
        xs = xs_scr[:m]
        a = jnp.dot(xs, w1_ref[0], preferred_element_type=F32)
        g = jnp.dot(xs, w3_ref[0], preferred_element_type=F32)
        ye_scr[:m] += jnp.dot((_silu(a) * g).astype(BF16), w2_ref[0], preferred_element_type=F32)

    for k in range(tm // blk):
        pl.when(nblk == k + 1)(functools.partial(expert, (k + 1) * blk))

    @pl.when(f == nf - 1)
    def _():
        lane = lax.broadcasted_iota(jnp.int32, (tm, LANES), 1)
        column = lambda ref: jnp.sum(jnp.where(lane == e, ref[...], 0.0), -1, keepdims=True)
        pos = column(posc_scr)
        gate = column(gate_scr)

        def scatter(i, carry):
            slot = (lax.broadcasted_iota(jnp.int32, (tm, mv), 1) + i * mv).astype(F32)
            put = jnp.where(slot == pos, 1.0, 0.0).astype(BF16)
            acc_scr[...] += gate * jnp.dot(put, ye_scr[rows(i), :].astype(BF16), preferred_element_type=F32)
            return carry

        lax.fori_loop(0, nmv, scatter, 0)

    @pl.when(s == ns - 1)
    def _():
        gate = _sel_mod(mod_ref, 5, row0, tm, lc)
        y = ALPHA * x_ref[0] + gate * acc_scr[...]
        o_ref[0] = _layer_norm(y) * lg_ref[...] + lb_ref[...]


def _moe_sparse(xx, mod, wr, w1, w3, w2, lg, lb, lc):
    b, lt, d = xx.shape
    ne, _, ff = w1.shape
    tm = _pick_tile(lt, (1280, 768, 512, 256))
    tf = _pick_tile(ff, (256, 128))
    nf = ff // tf
    ns = ne * nf
    return pl.pallas_call(
        functools.partial(_moe_sparse_kernel, lc=lc, tm=tm, nf=nf, ns=ns, blk=MOE_ROW_BLOCK),
        grid=(b, lt // tm, ns),
        in_specs=[pl.BlockSpec((1, tm, d), lambda bi, i, s: (bi, i, 0)),
                  pl.BlockSpec((1, 2, 6, d), lambda bi, i, s: (bi, 0, 0, 0)),
                  pl.BlockSpec((d, LANES), lambda bi, i, s: (0, 0)),
                  pl.BlockSpec((1, d, tf), lambda bi, i, s: (s // nf, 0, s % nf)),
                  pl.BlockSpec((1, d, tf), lambda bi, i, s: (s // nf, 0, s % nf)),
                  pl.BlockSpec((1, tf, d), lambda bi, i, s: (s // nf, s % nf, 0)),
                  pl.BlockSpec((1, d), lambda bi, i, s: (0, 0)),
                  pl.BlockSpec((1, d), lambda bi, i, s: (0, 0))],
        out_specs=pl.BlockSpec((1, tm, d), lambda bi, i, s: (bi, i, 0)),
        out_shape=jax.ShapeDtypeStruct((b, lt, d), F32),
        scratch_shapes=[pltpu.VMEM((tm, d), BF16), pltpu.VMEM((tm, d), F32),
                        pltpu.VMEM((tm, LANES), F32), pltpu.VMEM((tm, LANES), F32),
                        pltpu.VMEM((2 * SUBLANES, tm), F32),
                        pltpu.VMEM((tm, d), BF16), pltpu.VMEM((tm, d), F32),
                        pltpu.SMEM((ne,), jnp.int32)],
        compiler_params=_cparams(("parallel", "parallel", "arbitrary"), VMEM_LIMIT_MOE),
        name="moe_sparse",
    )(xx, mod, wr, w1, w3, w2, lg, lb)


def _rope_tables(l, lc):
    rows = l // GRID_W
    row = jnp.repeat(jnp.arange(rows, dtype=F32), GRID_W)
    col = jnp.tile(jnp.arange(GRID_W, dtype=F32), rows)
    n_freq = HEAD_DIM // 4
    inv_freq = ROPE_THETA ** (-jnp.arange(n_freq, dtype=F32) / n_freq)
    ang = jnp.concatenate([row[:, None] * inv_freq, col[:, None] * inv_freq], -1)
    cos, sin = jnp.cos(ang), jnp.sin(ang)
    cos2 = jnp.concatenate([jnp.ones((lc, LANES), F32), jnp.concatenate([cos, cos, cos, cos], -1)], 0)
    sin2 = jnp.concatenate([jnp.zeros((lc, LANES), F32), jnp.concatenate([-sin, sin, -sin, sin], -1)], 0)
    return cos2, sin2


def _hy_features(l):
    bands = (HY_EMB - 1) // 2
    t = jnp.linspace(0.0, 1.0, l, dtype=F32)[:, None]
    f = jnp.linspace(1e-4, bands - 1, bands, dtype=F32)[None, :]
    wt = 2.0 * math.pi * jnp.arange(l, dtype=F32)[:, None] / l
    z = jnp.concatenate([t, jnp.cos(f * wt), -jnp.sin(f * wt)], -1)
    return jnp.pad(z, ((0, 0), (0, LANES - HY_EMB)))


def _angle(idx, n):
    return (2.0 * math.pi / n) * (idx % n).astype(F32)


def _dft_tables(n1):
    nh = n1 // 2
    n2 = DFT_N2
    n = n1 * n2
    k1 = jnp.arange(n1, dtype=jnp.int32)
    a1 = _angle(k1[:, None] * jnp.arange(nh, dtype=jnp.int32)[None, :], n1)
    fstack = jnp.concatenate([jnp.cos(a1), -jnp.sin(a1)], 0)
    cstack = jnp.concatenate([jnp.cos(a1.T), jnp.sin(a1.T)], 0)
    k2 = jnp.arange(n2, dtype=jnp.int32)
    at = _angle(k1[:, None] * k2[None, :], n)
    tr, ti = jnp.cos(at), -jnp.sin(at)
    a2 = _angle(k2[:, None] * k2[None, :], n2)
    fr, fi = jnp.cos(a2), -jnp.sin(a2)
    g = jnp.stack([tr[:, None, :] * fr[None] - ti[:, None, :] * fi[None],
                   tr[:, None, :] * fi[None] + ti[:, None, :] * fr[None]], 1)
    return fstack, cstack, g


def _dense_dft_tables(lc):
    n = 2 * lc
    a = _angle(jnp.arange(n, dtype=jnp.int32)[:, None] * jnp.arange(lc, dtype=jnp.int32)[None, :], n)
    ff = jnp.concatenate([jnp.cos(a), -jnp.sin(a)], 0)
    ci = jnp.concatenate([jnp.cos(a.T), jnp.sin(a.T)], 0)
    return ff, ci


def kernel(x, c, ctx, c_ctx, ada_w, ada_b, w_in, w_out, q_gain, k_gain, rwkv_mu, rwkv_w0, rwkv_wB, rwkv_a0, rwkv_aB, rwkv_gB, rwkv_kk, rwkv_ka, rwkv_rk, rwkv_gn_g, rwkv_gn_b, hy_short_w, hy_short_b, hy_w1, hy_b1, hy_freq1, hy_w2, hy_b2, hy_freq2, hy_w3, hy_decay, hy_bias, ln1_g, ln1_b, ln2_g, ln2_b, ffn_w1, ffn_w3, ffn_w2, moe_router, moe_w1, moe_w3, moe_w2):
    b, l, d = x.shape
    lc = ctx.shape[1]
    lt = lc + l
    depth = ada_w.shape[0]
    assert b == 2, "the long convolution packs the two batch rows as one complex signal"
    assert d == D_MODEL and lc % TOK_TILE == 0 and l % TOK_TILE == 0 and (2 * l) % (2 * DFT_N2) == 0
    cw = RWKV_WIDTH

    xx = jnp.concatenate([ctx, x], 1)
    cond8 = jnp.zeros((SUBLANES, d), F32).at[:b].set(c).at[b].set(c_ctx)
    mod_all = _ada_mod(cond8, ada_w, ada_b)

    cos64, sin64 = _rope_tables(l, lc)
    n1 = 2 * l // DFT_N2
    nh = n1 // 2
    cols = DFT_N2 * HY_WIDTH
    fstack, cstack, g_tab = (t.astype(BF16) for t in _dft_tables(n1))
    ff_c, ci_c = _dense_dft_tables(lc)
    feat_l = _hy_features(l)
    feat_c = _hy_features(lc)
    blk = jnp.arange(cw) // HEAD_DIM
    bd = (blk[:, None] == blk[None, :]).astype(F32)
    ch = jnp.arange(ATT_WIDTH)
    bd_att = (ch[:, None] // HEAD_DIM == ch[None, :] // HEAD_DIM).astype(BF16)
    swap_att = (ch[:, None] == (ch[None, :] + HALF_HD) % HEAD_DIM + (ch[None, :] // HEAD_DIM) * HEAD_DIM
                ).astype(BF16)
    perm64 = jnp.concatenate([jnp.arange(0, HEAD_DIM, 2), jnp.arange(1, HEAD_DIM, 2)])
    perm_att = jnp.concatenate([h * HEAD_DIM + perm64 for h in range(ATT_HEADS + ATT_KV_HEADS)]
                               + [jnp.arange(ATT_WIDTH + ATT_KV_WIDTH, IN_ATT)])
    tq = TOK_TILE
    qn = next(n for n in (4, 2, 1) if (l // tq) % n == 0)
    tk = _pick_tile(lt, (3328, 1280, 1024, 768, 512, 256))

    for li in range(depth):
        ml = mod_all[li]
        mod = jnp.stack([jnp.broadcast_to(ml[b].reshape(1, 6, d), (b, 6, d)), ml[:b].reshape(b, 6, d)], 1)
        wi = w_in[li]
        w_pad = jnp.concatenate([wi[:, :IN_ATT][:, perm_att], wi[:, IN_ATT:IN_ATT + IN_RWKV],
                                 jnp.zeros((d, IN_RWKV_PAD - IN_RWKV), F32), wi[:, IN_ATT + IN_RWKV:]],
                                1).astype(BF16)
        ua, ur, uh = _inproj(xx, mod, w_pad, lc)

        two = lambda gain: jnp.tile(gain[perm64], 2)[None]
        qt, kx, vt = _attn_prep(ua, cos64, sin64, two(q_gain[li]), two(k_gain[li]), bd_att, swap_att, lc)
        att_c = _flash(qt, kx, vt, tq, TOK_TILE, 1, l // tq, lc // tq, lc // TOK_TILE)
        s_bound = (HEAD_DIM ** 0.5) * LOG2E * jnp.max(jnp.abs(q_gain[li])) * jnp.max(jnp.abs(k_gain[li]))
        lat_args = (qt, kx, vt, tq, tk, qn, 0, l // (qn * tq), lt // tk)
        att_l = lax.cond(s_bound <= MAX_UNSHIFTED_SCORE,
                         lambda: _flash(*lat_args, bounded=True), lambda: _flash(*lat_args, bounded=False))

        wl = jnp.zeros((LANES, 5 * cw), F32)
        wl = wl.at[0:W_LORA, 0:cw].set(rwkv_wB[li, 0]).at[W_LORA:2 * W_LORA, cw:2 * cw].set(rwkv_wB[li, 1])
        o_a = 2 * W_LORA
        wl = wl.at[o_a:o_a + A_LORA, 2 * cw:3 * cw].set(rwkv_aB[li, 0])
        wl = wl.at[o_a + A_LORA:o_a + 2 * A_LORA, 3 * cw:4 * cw].set(rwkv_aB[li, 1])
        o_g = o_a + 2 * A_LORA
        wl = wl.at[o_g:o_g + G_LORA, 4 * cw:5 * cw].set(rwkv_gB[li])
        mu = jnp.pad(rwkv_mu[li], ((0, 0), (0, IN_RWKV_PAD - IN_RWKV)))
        r_, v_, kk_, g_, bon_, lw_, be_, kd_ = _rwkv_prep(
            ur, mu, wl, bd, rwkv_w0[li], rwkv_a0[li], rwkv_kk[li][None], rwkv_ka[li][None],
            rwkv_rk[li].reshape(1, cw), lc)
        yf, yb = _wkv_scan(r_, v_, kk_, lw_, be_, kd_, lc)
        rw = _rwkv_out(yf, yb, bon_, g_, bd, rwkv_gn_g[li][None], rwkv_gn_b[li][None])

        x1, x2, vv = _hy_prep(uh, hy_short_w[li], hy_short_b[li][None], lc)
        w1p = jnp.pad(hy_w1[li], ((0, LANES - HY_EMB), (0, 0)))
        fargs = (w1p, hy_b1[li][None], hy_freq1[li][None], hy_w2[li], hy_b2[li][None], hy_freq2[li][None],
                 hy_w3[li], hy_decay[li][None])
        fw, bw, nrm = _hy_filter(feat_l, *fargs)
        c2 = HY_ORDER * HY_WIDTH
        a4 = _dft_cols(fstack, fw.reshape(1, nh, DFT_N2 * c2), 0, bw.reshape(1, nh, DFT_N2 * c2), 0, True)
        kspec = _spec(a4.reshape(4, n1, DFT_N2, c2), g_tab, nrm, n1 * DFT_N2)
        lat = lambda t: t[:, lc:].reshape(b, nh, cols)
        x1l, x2l, zin = lat(x1), lat(x2), lat(vv)
        for o, gate in enumerate((x1l, x2l)):
            a = _dft_cols(fstack, zin, 0, zin, 1, False)
            bv = _conv_mid(a.reshape(2, n1, DFT_N2, HY_WIDTH), g_tab, kspec, o)
            bias_cols = jnp.tile(hy_bias[li, o], DFT_N2)[None]
            zin = _idft_cols(cstack, bv.reshape(2, n1, cols), gate, zin, bias_cols)
        hy_l = zin.reshape(b, l, HY_WIDTH)
        fw_c, bw_c, nrm_c = _hy_filter(feat_c, *fargs)
        hy_c = _hy_ctx(x1[:, :lc], x2[:, :lc], vv[:, :lc], fw_c, bw_c, nrm_c, hy_bias[li], ff_c, ci_c)

        xx = _outproj(att_c, att_l, rw, hy_c, hy_l, xx, mod, w_out[li].astype(BF16),
                      ln1_g[li][None], ln1_b[li][None], lc)

        j = li // 2
        if li % 2 == 0:
            xx = _ffn(xx, mod, ffn_w1[j].astype(BF16), ffn_w3[j].astype(BF16), ffn_w2[j].astype(BF16),
                      ln2_g[li][None], ln2_b[li][None], lc)
        else:
            wr = jnp.pad(moe_router[j], ((0, 0), (0, LANES - N_EXPERTS)))
            xx = _moe_sparse(xx, mod, wr, moe_w1[j].astype(BF16), moe_w3[j].astype(BF16), moe_w2[j].astype(BF16),
                             ln2_g[li][None], ln2_b[li][None], lc)
    return xx[:, lc:]
```

```python
import functools
import math

import jax
import jax.numpy as jnp
from jax import lax
from jax.experimental import pallas as pl
from jax.experimental.pallas import tpu as pltpu

F32 = jnp.float32
BF16 = jnp.bfloat16
HI = lax.Precision.HIGHEST

D_MODEL = 1024
DEPTH = 2
GRID_W = 64
HEAD_DIM = 64
HALF_HD = HEAD_DIM // 2
ATT_WIDTH = 512
RWKV_WIDTH = 256
HY_WIDTH = 256
ATT_HEADS = 8
ATT_KV_HEADS = 2
ATT_REP = 4
ATT_KV_WIDTH = 128
ROPE_THETA = 10000.0
QK_EPS = 1e-6
RWKV_HEADS = 4
W_LORA = 16
A_LORA = 16
G_LORA = 32
RWKV_GN_EPS = 64e-5
HY_ORDER = 2
HY_EMB = 33
HY_FFN = 64
N_EXPERTS = 8
LN_EPS = 1e-6
IN_ATT = ATT_WIDTH + 2 * ATT_KV_WIDTH
IN_RWKV = 3 * RWKV_WIDTH + 2 * W_LORA + 2 * A_LORA + G_LORA
IN_RWKV_PAD = 896
IN_HY = 3 * HY_WIDTH
ALPHA = float((2 * DEPTH) ** 0.25)
LOG2E = 1.4426950408889634
MAX_UNSHIFTED_SCORE = 40.0

LANES = 128
SUBLANES = 8
TOK_TILE = 256
WKV_CHUNK = 64
WKV_TILE = 256
DFT_N2 = 256
MOE_ROW_BLOCK = 128
MOE_MOVE_BLOCK = 256
VMEM_LIMIT = 48 * 1024 * 1024
VMEM_LIMIT_MOE = 56 * 1024 * 1024


def _cparams(sem, vmem=VMEM_LIMIT):
    return pltpu.CompilerParams(dimension_semantics=sem, vmem_limit_bytes=vmem)


def _pick_tile(n, cands):
    for c in cands:
        if n % c == 0:
            return c
    raise ValueError(f"no tile for {n} in {cands}")


def _layer_norm(x):
    mu = jnp.mean(x, -1, keepdims=True)
    xc = x - mu
    var = jnp.mean(xc * xc, -1, keepdims=True)
    return xc * lax.rsqrt(var + LN_EPS)


def _sel_mod(mod_ref, j, row0, tm, lc):
    rid = row0 + lax.broadcasted_iota(jnp.int32, (tm, 1), 0)
    return jnp.where(rid < lc, mod_ref[0, 0, j:j + 1, :], mod_ref[0, 1, j:j + 1, :])


def _silu(x):
    return x * jax.nn.sigmoid(x)


def _ada_kernel(c_ref, w_ref, b_ref, o_ref):
    s = _silu(c_ref[...])
    o_ref[0] = jnp.dot(s, w_ref[0], precision=HI, preferred_element_type=F32) + b_ref[0]


def _ada_mod(cond8, ada_w, ada_b):
    depth, d, n = ada_w.shape
    tn = _pick_tile(n, (1536, 1024, 512, 256, 128))
    return pl.pallas_call(
        _ada_kernel,
        grid=(depth, n // tn),
        in_specs=[pl.BlockSpec((SUBLANES, d), lambda l, j: (0, 0)),
                  pl.BlockSpec((1, d, tn), lambda l, j: (l, 0, j)),
                  pl.BlockSpec((1, 1, tn), lambda l, j: (l, 0, j))],
        out_specs=pl.BlockSpec((1, SUBLANES, tn), lambda l, j: (l, 0, j)),
        out_shape=jax.ShapeDtypeStruct((depth, SUBLANES, n), F32),
        compiler_params=_cparams(("parallel", "parallel")),
        name="ada_mod",
    )(cond8, ada_w, ada_b.reshape(depth, 1, n))


def _inproj_kernel(x_ref, mod_ref, w_ref, oa_ref, or_ref, oh_ref, *, lc, tm):
    row0 = pl.program_id(1) * tm
    sh = _sel_mod(mod_ref, 0, row0, tm, lc)
    sc = _sel_mod(mod_ref, 1, row0, tm, lc)
    h = (_layer_norm(x_ref[0]) * (1.0 + sc) + sh).astype(BF16)
    u = jnp.dot(h, w_ref[...], preferred_element_type=F32)
    oa_ref[0] = u[:, :IN_ATT]
    or_ref[0] = u[:, IN_ATT:IN_ATT + IN_RWKV_PAD]
    oh_ref[0] = u[:, IN_ATT + IN_RWKV_PAD:]


def _inproj(xx, mod, w_pad, lc):
    b, lt, d = xx.shape
    tm = TOK_TILE
    n = w_pad.shape[1]
    return pl.pallas_call(
        functools.partial(_inproj_kernel, lc=lc, tm=tm),
        grid=(b, lt // tm),
        in_specs=[pl.BlockSpec((1, tm, d), lambda bi, i: (bi, i, 0)),
                  pl.BlockSpec((1, 2, 6, d), lambda bi, i: (bi, 0, 0, 0)),
                  pl.BlockSpec((d, n), lambda bi, i: (0, 0))],
        out_specs=[pl.BlockSpec((1, tm, IN_ATT), lambda bi, i: (bi, i, 0)),
                   pl.BlockSpec((1, tm, IN_RWKV_PAD), lambda bi, i: (bi, i, 0)),
                   pl.BlockSpec((1, tm, IN_HY), lambda bi, i: (bi, i, 0))],
        out_shape=[jax.ShapeDtypeStruct((b, lt, IN_ATT), F32),
                   jax.ShapeDtypeStruct((b, lt, IN_RWKV_PAD), F32),
                   jax.ShapeDtypeStruct((b, lt, IN_HY), F32)],
        compiler_params=_cparams(("parallel", "parallel")),
        name="inproj",
    )(xx, mod, w_pad)


def _attn_prep_kernel(u_ref, cos_ref, sin_ref, qg_ref, kg_ref, bd_ref, sw_ref, qt_ref, k_ref, vt_ref):
    u = u_ref[0]

    def norm_rope(x, g):
        w = x.shape[1]
        tile = lambda t: jnp.concatenate([t] * (w // LANES), -1)
        ms = _dot01(x * x, bd_ref[:w, :w]) * (1.0 / HEAD_DIM)
        xn = x * lax.rsqrt(ms + QK_EPS) * tile(g)
        sw = _dot01(xn, sw_ref[:w, :w])
        return xn * tile(cos_ref[...]) + sw * tile(sin_ref[...])

    q = norm_rope(u[:, :ATT_WIDTH], qg_ref[...]) * (LOG2E * HEAD_DIM ** -0.5)
    qt = q.T
    kx = norm_rope(u[:, ATT_WIDTH:ATT_WIDTH + ATT_KV_WIDTH], kg_ref[...])
    for g in range(ATT_KV_HEADS):
        base = g * ATT_REP * HEAD_DIM
        qt_ref[0, g, 0] = jnp.concatenate(
            [qt[base + r * HEAD_DIM:base + (r + 1) * HEAD_DIM] for r in range(ATT_REP)], -1).astype(BF16)
        k_ref[0, g] = kx[:, g * HEAD_DIM:(g + 1) * HEAD_DIM].astype(BF16)
    v0 = ATT_WIDTH + ATT_KV_WIDTH
    vt = u[:, v0:v0 + ATT_KV_WIDTH].T
    for g in range(ATT_KV_HEADS):
        vt_ref[0, g] = vt[g * HEAD_DIM:(g + 1) * HEAD_DIM].astype(BF16)


def _attn_prep(ua, cos, sin, qg, kg, bd_att, swap_att, lc):
    b, lt, _ = ua.shape
    tm = TOK_TILE
    nct = lc // tm
    nl = lt // tm - nct
    q_pos = lambda i: jnp.where(i < nct, nl + i, i - nct)
    return pl.pallas_call(
        _attn_prep_kernel,
        grid=(b, lt // tm),
        in_specs=[pl.BlockSpec((1, tm, IN_ATT), lambda bi, i: (bi, i, 0)),
                  pl.BlockSpec((tm, LANES), lambda bi, i: (i, 0)),
                  pl.BlockSpec((tm, LANES), lambda bi, i: (i, 0)),
                  pl.BlockSpec((1, LANES), lambda bi, i: (0, 0)),
                  pl.BlockSpec((1, LANES), lambda bi, i: (0, 0)),
                  pl.BlockSpec((ATT_WIDTH, ATT_WIDTH), lambda bi, i: (0, 0)),
                  pl.BlockSpec((ATT_WIDTH, ATT_WIDTH), lambda bi, i: (0, 0))],
        out_specs=[pl.BlockSpec((1, ATT_KV_HEADS, 1, HEAD_DIM, ATT_REP * tm),
                                lambda bi, i: (bi, 0, q_pos(i), 0, 0)),
                   pl.BlockSpec((1, ATT_KV_HEADS, tm, HEAD_DIM), lambda bi, i: (bi, 0, i, 0)),
                   pl.BlockSpec((1, ATT_KV_HEADS, HEAD_DIM, tm), lambda bi, i: (bi, 0, 0, i))],
        out_shape=[jax.ShapeDtypeStruct((b, ATT_KV_HEADS, lt // tm, HEAD_DIM, ATT_REP * tm), BF16),
                   jax.ShapeDtypeStruct((b, ATT_KV_HEADS, lt, HEAD_DIM), BF16),
                   jax.ShapeDtypeStruct((b, ATT_KV_HEADS, HEAD_DIM, lt), BF16)],
        compiler_params=_cparams(("parallel", "parallel")),
        name="attn_prep",
    )(ua, cos, sin, qg, kg, bd_att, swap_att)


def _flash_kernel(qt_ref, k_ref, vt_ref, o_ref, m_scr, l_scr, acc_scr, *, nk, tq, sub, qn):
    j = pl.program_id(3)

    @pl.when(j == 0)
    def _():
        m_scr[...] = jnp.full(m_scr.shape, -jnp.inf, F32)
        l_scr[...] = jnp.zeros(l_scr.shape, F32)
        acc_scr[...] = jnp.zeros(acc_scr.shape, F32)

    qt = jnp.concatenate([qt_ref[0, 0, t] for t in range(qn)], -1)
    nsub = k_ref.shape[2] // sub
    m = m_scr[...]
    l = l_scr[...]
    acc = acc_scr[...]
    scores = lambda c: jnp.dot(k_ref[0, 0, c * sub:(c + 1) * sub, :], qt, preferred_element_type=F32)
    pv = lambda c, p: jnp.dot(vt_ref[0, 0, :, c * sub:(c + 1) * sub], p, preferred_element_type=F32)
    s_next = scores(0)
    pend = None
    for c in range(nsub):
        s = s_next
        if c + 1 < nsub:
            s_next = scores(c + 1)
        if pend is not None:
            acc = pend[0] * acc + pv(c - 1, pend[1])
        m_new = jnp.maximum(m, jnp.max(s, 0, keepdims=True))
        a = jnp.exp2(m - m_new)
        p = jnp.exp2(s - m_new)
        l = a * l + jnp.sum(p, 0, keepdims=True)
        pend = (a, p.astype(BF16))
        m = m_new
    acc = pend[0] * acc + pv(nsub - 1, pend[1])
    m_scr[...] = m
    l_scr[...] = l
    acc_scr[...] = acc

    @pl.when(j == nk - 1)
    def _():
        o = (acc / l).T
        for t in range(qn):
            o_ref[0, t * tq:(t + 1) * tq, :] = jnp.concatenate(
                [o[(t * ATT_REP + r) * tq:(t * ATT_REP + r + 1) * tq] for r in range(ATT_REP)], -1
            ).astype(o_ref.dtype)


def _flash_bounded_kernel(qt_ref, k_ref, vt_ref, o_ref, l_scr, acc_scr, *, nk, tq, sub, qn):
    j = pl.program_id(3)

    @pl.when(j == 0)
    def _():
        l_scr[...] = jnp.zeros(l_scr.shape, F32)
        acc_scr[...] = jnp.zeros(acc_scr.shape, F32)

    qt = jnp.concatenate([qt_ref[0, 0, t] for t in range(qn)], -1)
    nsub = k_ref.shape[2] // sub

    l = l_scr[...]
    acc = acc_scr[...]
    scores = lambda c: jnp.dot(k_ref[0, 0, c * sub:(c + 1) * sub, :], qt, preferred_element_type=F32)
    s_next = scores(0)
    for c in range(nsub):
        s = s_next
        if c + 1 < nsub:
            s_next = scores(c + 1)
        p = jnp.exp2(s)
        l = l + jnp.sum(p.reshape(sub // SUBLANES, SUBLANES, p.shape[1]), 0)
        acc = acc + jnp.dot(vt_ref[0, 0, :, c * sub:(c + 1) * sub], p.astype(BF16), preferred_element_type=F32)
    l_scr[...] = l
    acc_scr[...] = acc

    @pl.when(j == nk - 1)
    def _():
        o = (acc / jnp.sum(l, 0, keepdims=True)).T
        for t in range(qn):
            o_ref[0, t * tq:(t + 1) * tq, :] = jnp.concatenate(
                [o[(t * ATT_REP + r) * tq:(t * ATT_REP + r + 1) * tq] for r in range(ATT_REP)], -1
            ).astype(o_ref.dtype)


def _flash(qt, k, vt, tq, tk, qn, q_blk0, nq, nk, bounded=False):
    b = qt.shape[0]
    lq = nq * qn * tq
    sub = _pick_tile(tk, (256, 128))
    lanes = qn * ATT_REP * tq
    if bounded:
        body = functools.partial(_flash_bounded_kernel, nk=nk, tq=tq, sub=sub, qn=qn)
        scratch = [pltpu.VMEM((SUBLANES, lanes), F32), pltpu.VMEM((HEAD_DIM, lanes), F32)]
    else:
        body = functools.partial(_flash_kernel, nk=nk, tq=tq, sub=sub, qn=qn)
        scratch = [pltpu.VMEM((1, lanes), F32), pltpu.VMEM((1, lanes), F32), pltpu.VMEM((HEAD_DIM, lanes), F32)]
    return pl.pallas_call(
        body,
        grid=(b, ATT_KV_HEADS, nq, nk),
        in_specs=[pl.BlockSpec((1, 1, qn, HEAD_DIM, ATT_REP * tq), lambda bi, g, i, j: (bi, g, i + q_blk0, 0, 0)),
                  pl.BlockSpec((1, 1, tk, HEAD_DIM), lambda bi, g, i, j: (bi, g, j, 0)),
                  pl.BlockSpec((1, 1, HEAD_DIM, tk), lambda bi, g, i, j: (bi, g, 0, j))],
        out_specs=pl.BlockSpec((1, qn * tq, ATT_REP * HEAD_DIM), lambda bi, g, i, j: (bi, i, g)),
        out_shape=jax.ShapeDtypeStruct((b, lq, ATT_WIDTH), BF16),
        scratch_shapes=scratch,
        compiler_params=_cparams(("parallel", "parallel", "parallel", "arbitrary")),
        name="flash_bounded" if bounded else "flash",
    )(qt, k, vt)


def _prev_next(u, up8, un8, i, tm, lc, lt):
    start = i * tm
    p_ok = jnp.logical_and(start != 0, start != lc)
    n_ok = jnp.logical_and(start + tm != lc, start + tm != lt)
    prow = jnp.where(p_ok, up8[SUBLANES - 1:SUBLANES], 0.0)
    nrow = jnp.where(n_ok, un8[0:1], 0.0)
    rid = lax.broadcasted_iota(jnp.int32, u.shape, 0)
    prev = jnp.where(rid == 0, prow, pltpu.roll(u, 1, 0))
    nxt = jnp.where(rid == tm - 1, nrow, pltpu.roll(u, tm - 1, 0))
    return prev, nxt


def _halo_specs(tm, c, lt):
    r = tm // SUBLANES
    last = lt // SUBLANES - 1
    return [pl.BlockSpec((1, tm, c), lambda bi, i: (bi, i, 0)),
            pl.BlockSpec((1, SUBLANES, c), lambda bi, i: (bi, jnp.maximum(i * r - 1, 0), 0)),
            pl.BlockSpec((1, SUBLANES, c), lambda bi, i: (bi, jnp.minimum((i + 1) * r, last), 0))]


def _softplus(z):
    return jnp.maximum(z, 0.0) + jnp.log1p(jnp.exp(-jnp.abs(z)))


def _rwkv_prep_kernel(u_ref, up_ref, un_ref, mu_ref, wl_ref, bd_ref, w0_ref, a0_ref, kkw_ref, ka_ref, rk_ref,
                      r_o, v_o, kk_o, g_o, bon_o, lw_o, be_o, kd_o, *, lc, lt, tm):
    i = pl.program_id(1)
    u = u_ref[0]
    prev, nxt = _prev_next(u, up_ref[0], un_ref[0], i, tm, lc, lt)
    us = u + mu_ref[0:1] * (prev - u) + mu_ref[1:2] * (nxt - u)
    c = RWKV_WIDTH
    r = us[:, 0:c]
    k = us[:, c:2 * c]
    v = us[:, 2 * c:3 * c]
    slab = us[:, 3 * c:3 * c + LANES]
    lane = lax.broadcasted_iota(jnp.int32, slab.shape, 1)
    o_a = 2 * W_LORA
    o_g = o_a + 2 * A_LORA
    act = jnp.where(lane < o_a, jnp.tanh(slab),
                    jnp.where(lane < o_g, slab,
                              jnp.where(lane < o_g + G_LORA, jax.nn.sigmoid(slab), 0.0)))
    lo = _dotp(act, wl_ref[...], 3)
    bd = bd_ref[...]
    kk0 = k * kkw_ref[...]
    kk = kk0 * lax.rsqrt(_dot01(kk0 * kk0, bd) + 1e-12)
    r_o[0] = r
    v_o[0] = v
    kk_o[0] = kk
    g_o[0] = lo[:, 4 * c:5 * c]
    bon = jnp.zeros_like(r)
    for d in range(2):
        w_raw = w0_ref[d:d + 1] + lo[:, d * c:(d + 1) * c]
        lw = -jnp.exp(-_softplus(-w_raw) - 0.5)
        a = jax.nn.sigmoid(a0_ref[d:d + 1] + lo[:, (2 + d) * c:(3 + d) * c])
        kd = k * (1.0 + (a - 1.0) * ka_ref[...])
        lw_o[0, d] = lw
        be_o[0, d] = a * kk
        kd_o[0, d] = kd
        bon = bon + r * kd * rk_ref[...]
    bon_o[0] = _dot01(bon, bd) * v


def _rwkv_prep(ur, mu, wl, bd, w0, a0, kkw, ka, rk, lc):
    b, lt, cp = ur.shape
    tm = TOK_TILE
    c = RWKV_WIDTH
    full = lambda shape: pl.BlockSpec(shape, lambda bi, i: (0,) * len(shape))
    tok = pl.BlockSpec((1, tm, c), lambda bi, i: (bi, i, 0))
    tok2 = pl.BlockSpec((1, 2, tm, c), lambda bi, i: (bi, 0, i, 0))
    s1 = jax.ShapeDtypeStruct((b, lt, c), F32)
    s2 = jax.ShapeDtypeStruct((b, 2, lt, c), F32)
    return pl.pallas_call(
        functools.partial(_rwkv_prep_kernel, lc=lc, lt=lt, tm=tm),
        grid=(b, lt // tm),
        in_specs=_halo_specs(tm, cp, lt) + [full((2, cp)), full((LANES, 5 * c)), full((c, c)), full((2, c)),
                                            full((2, c)), full((1, c)), full((1, c)), full((1, c))],
        out_specs=[tok, tok, tok, tok, tok, tok2, tok2, tok2],
        out_shape=[s1, s1, s1, s1, s1, s2, s2, s2],
        compiler_params=_cparams(("parallel", "parallel")),
        name="rwkv_prep",
    )(ur, ur, ur, mu, wl, bd, w0, a0, kkw, ka, rk)


def _mm(a, b):
    return jnp.dot(a, b, precision=HI, preferred_element_type=F32)


_NN = ((1,), (0,))
_NT = ((1,), (1,))
_TN = ((0,), (0,))


def _split2(a):
    hi = a.astype(BF16)
    return hi, (a - hi.astype(F32)).astype(BF16)


def _dotp(a, b, passes, dims=_NN):
    if a.ndim == 3:
        dn = (((dims[0][0] + 1,), (dims[1][0] + 1,)), ((0,), (0,)))
    else:
        dn = (dims, ((), ()))
    dg = lambda p, q: lax.dot_general(p, q, dn, preferred_element_type=F32)
    if passes == 1:
        return dg(a.astype(BF16), b.astype(BF16))
    ah, al = _split2(a)
    bh, bl = _split2(b)
    return dg(ah, bh) + dg(ah, bl) + dg(al, bh)


def _dot01(a, ones):
    ah, al = _split2(a)
    o = ones.astype(BF16)
    return jnp.dot(ah, o, preferred_element_type=F32) + jnp.dot(al, o, preferred_element_type=F32)


P_M = 1
P_INV = 1
P_W = 1
P_Z = 1
P_STATE = 3
P_DFT = 1


def _unit_tri_inv(a_mat, row, col, eye, passes):
    t = a_mat.shape[-1]
    eye_f = eye.astype(F32)
    base = SUBLANES
    same = (row // base) == (col // base)
    n1 = -jnp.where(same, a_mat, 0.0)
    n2 = _dotp(n1, n1, passes)
    n4 = _dotp(n2, n2, passes)
    x = _dotp(_dotp(eye_f + n1, eye_f + n2, passes), eye_f + n4, passes)
    m = base
    while m < t:
        off = jnp.logical_and((row // (2 * m)) == (col // (2 * m)), (row // m) != (col // m))
        x = x - _dotp(x, _dotp(jnp.where(off, a_mat, 0.0), x, passes), passes)
        m *= 2
    return x


def _wkv_kernel(rf_ref, vf_ref, kkf_ref, rb_ref, vb_ref, kkb_ref, lwf_ref, bef_ref, kdf_ref,
                lwb_ref, beb_ref, kdb_ref, yf_ref, yb_ref, h_scr, pq_scr, ry_scr):
    t = WKV_CHUNK
    n = HEAD_DIM
    g = WKV_TILE // WKV_CHUNK
    tt = WKV_TILE
    nh = RWKV_HEADS
    nd = g * nh
    nu = 2 * nd
    orders = (list(range(g)), list(range(g - 1, -1, -1)))

    @pl.when(pl.program_id(1) == 0)
    def _():
        h_scr[...] = jnp.zeros(h_scr.shape, F32)
        pq_scr[...] = jnp.zeros(pq_scr.shape, F32)
        ry_scr[...] = jnp.zeros(ry_scr.shape, F32)

    hm = h_scr[...]
    for p in range(g):
        both = lambda ref: jnp.concatenate([ref[d * nd + p * nh:d * nd + (p + 1) * nh] for d in range(2)], 0)
        ry = both(ry_scr)
        pq = both(pq_scr)
        y = _dotp(ry[:, :, :n], hm, P_STATE) + ry[:, :, n:]
        hm = _dotp(pq[:, :, :n], hm, P_STATE) + pq[:, :, n:]
        for d, y_ref in enumerate((yf_ref, yb_ref)):
            c = orders[d][p]
            y_ref[0, c * t:(c + 1) * t, :] = jnp.concatenate([y[d * nh + h] for h in range(nh)], -1)
    h_scr[...] = hm

    row = lax.broadcasted_iota(jnp.int32, (tt, tt), 0)
    col = lax.broadcasted_iota(jnp.int32, (tt, tt), 1)
    same = (row // t) == (col // t)

    def scaled(d, r_ref, v_ref, kk_ref, lw_ref, be_ref, kd_ref):
        tri = jnp.logical_and(same, (row >= col) if d == 0 else (row <= col))
        sums = jnp.concatenate([jnp.where(tri, 1.0, 0.0), jnp.where(same, 1.0, 0.0)], 0).astype(BF16)
        lw = lw_ref[0, 0]
        l1 = lw.astype(BF16)
        rem = lw - l1.astype(F32)
        l2 = rem.astype(BF16)
        l3 = (rem - l2.astype(F32)).astype(BF16)
        cc = (jnp.dot(sums, l1, preferred_element_type=F32) + jnp.dot(sums, l2, preferred_element_type=F32)
              + jnp.dot(sums, l3, preferred_element_type=F32))
        cum = cc[:tt]
        ctot = cc[tt:]
        e_neg = jnp.exp(-cum)
        e_end = jnp.exp(ctot - cum)
        be = be_ref[0, 0]
        kd = kd_ref[0, 0]

        def units(x, rows=t):
            return jnp.stack([x[orders[d][p] * t:orders[d][p] * t + rows, h * n:(h + 1) * n]
                              for p in range(g) for h in range(nh)], 0)

        return dict(kap=units(kk_ref[0] * jnp.exp(cum - lw)), rt=units(r_ref[0] * jnp.exp(cum)), vh=units(v_ref[0]),
                    bet=units(be * e_neg), kdt=units(kd * e_neg), beh=units(be * e_end), kdh=units(kd * e_end),
                    gend=units(jnp.exp(ctot), 1))

    parts = (scaled(0, rf_ref, vf_ref, kkf_ref, lwf_ref, bef_ref, kdf_ref),
             scaled(1, rb_ref, vb_ref, kkb_ref, lwb_ref, beb_ref, kdb_ref))
    cat = lambda key: jnp.concatenate([parts[0][key], parts[1][key]], 0)
    kap, rt, vh = cat("kap"), cat("rt"), cat("vh")
    r64 = lax.broadcasted_iota(jnp.int32, (t, t), 0)
    c64 = lax.broadcasted_iota(jnp.int32, (t, t), 1)
    eye = r64 == c64
    unit = lax.broadcasted_iota(jnp.int32, (nu, t, t), 0)
    ahead = (r64 - c64)[None] * jnp.where(unit < nd, 1, -1)
    strict = ahead > 0
    incl = ahead >= 0
    m = _dotp(jnp.concatenate([kap, rt], 1), jnp.concatenate([cat("bet"), cat("kdt")], 1), P_M, _NT)
    a_mat = jnp.where(strict, m[:, :t, :t], 0.0)
    b_mat = jnp.where(strict, m[:, :t, t:], 0.0)
    ab_r = jnp.concatenate([jnp.where(incl, m[:, t:, :t], 0.0), jnp.where(incl, m[:, t:, t:], 0.0)], 2)
    tinv = _unit_tri_inv(a_mat, r64, c64, eye, P_INV)
    w = _dotp(tinv, jnp.concatenate([kap, _dotp(b_mat, vh, P_W)], 2), P_W)
    z = jnp.concatenate([-w, jnp.concatenate([jnp.zeros((nu, t, n), F32), vh], 2)], 1)
    ry = _dotp(ab_r, z, P_Z)
    ry_scr[:, :, :n] = ry[:, :, :n] + rt
    ry_scr[:, :, n:] = ry[:, :, n:]
    pq = _dotp(jnp.concatenate([cat("beh"), cat("kdh")], 1), z, P_Z, _TN)
    gd = jnp.where(eye, jnp.broadcast_to(cat("gend"), (nu, n, n)), 0.0)
    pq_scr[:, :, :n] = pq[:, :, :n] + gd
    pq_scr[:, :, n:] = pq[:, :, n:]


def _wkv_scan(r, v, kk, lw, be, kd, lc):
    b, lt, c = r.shape
    tt = WKV_TILE
    nt = lt // tt
    ntc = lc // tt
    tiles = (lambda i: i, lambda i: jnp.where(i < ntc, ntc - 1 - i, nt - 1 - (i - ntc)))
    t_in = lambda d: (lambda i: tiles[d](jnp.minimum(i, nt - 1)))
    t_out = lambda d: (lambda i: tiles[d](jnp.maximum(i - 1, 0)))
    one = lambda d: pl.BlockSpec((1, tt, c), lambda bi, i: (bi, t_in(d)(i), 0))
    two = lambda d: pl.BlockSpec((1, 1, tt, c), lambda bi, i: (bi, d, t_in(d)(i), 0))
    out = lambda d: pl.BlockSpec((1, tt, c), lambda bi, i: (bi, t_out(d)(i), 0))
    nu = 2 * (tt // WKV_CHUNK) * RWKV_HEADS
    ysh = jax.ShapeDtypeStruct((b, lt, c), F32)
    return pl.pallas_call(
        _wkv_kernel,
        grid=(b, nt + 1),
        in_specs=[one(0)] * 3 + [one(1)] * 3 + [two(0)] * 3 + [two(1)] * 3,
        out_specs=[out(0), out(1)],
        out_shape=[ysh, ysh],
        scratch_shapes=[pltpu.VMEM((2 * RWKV_HEADS, HEAD_DIM, HEAD_DIM), F32),
                        pltpu.VMEM((nu, HEAD_DIM, 2 * HEAD_DIM), F32),
                        pltpu.VMEM((nu, WKV_CHUNK, 2 * HEAD_DIM), F32)],
        compiler_params=_cparams(("parallel", "arbitrary")),
        name="wkv_scan",
    )(r, v, kk, r, v, kk, lw, be, kd, lw, be, kd)


def _rwkv_out_kernel(yf_ref, yb_ref, bon_ref, g_ref, bd_ref, gg_ref, gb_ref, o_ref):
    y = yf_ref[0] + yb_ref[0] + bon_ref[0]
    bd = bd_ref[...]
    mu = _dot01(y, bd) * (1.0 / HEAD_DIM)
    yc = y - mu
    var = _dot01(yc * yc, bd) * (1.0 / HEAD_DIM)
    yn = yc * lax.rsqrt(var + RWKV_GN_EPS) * gg_ref[...] + gb_ref[...]
    o_ref[0] = (yn * g_ref[0]).astype(o_ref.dtype)


def _rwkv_out(yf, yb, bon, g, bd, gg, gb):
    b, lt, c = yf.shape
    tm = TOK_TILE
    tok = pl.BlockSpec((1, tm, c), lambda bi, i: (bi, i, 0))
    full = lambda shape: pl.BlockSpec(shape, lambda bi, i: (0,) * len(shape))
    return pl.pallas_call(
        _rwkv_out_kernel,
        grid=(b, lt // tm),
        in_specs=[tok, tok, tok, tok, full((c, c)), full((1, c)), full((1, c))],
        out_specs=tok,
        out_shape=jax.ShapeDtypeStruct((b, lt, c), BF16),
        compiler_params=_cparams(("parallel", "parallel")),
        name="rwkv_out",
    )(yf, yb, bon, g, bd, gg, gb)


def _hy_prep_kernel(u_ref, up_ref, un_ref, w_ref, b_ref, x1_o, x2_o, v_o, *, lc, lt, tm):
    i = pl.program_id(1)
    u = u_ref[0]
    prev, nxt = _prev_next(u, up_ref[0], un_ref[0], i, tm, lc, lt)
    y = prev * w_ref[0:1] + u * w_ref[1:2] + nxt * w_ref[2:3] + b_ref[...]
    c = HY_WIDTH
    x1_o[0] = y[:, :c]
    x2_o[0] = y[:, c:2 * c]
    v_o[0] = y[:, 2 * c:]


def _hy_prep(uh, w, bias, lc):
    b, lt, cin = uh.shape
    tm = TOK_TILE
    c = HY_WIDTH
    full = lambda shape: pl.BlockSpec(shape, lambda bi, i: (0,) * len(shape))
    tok = pl.BlockSpec((1, tm, c), lambda bi, i: (bi, i, 0))
    s1 = jax.ShapeDtypeStruct((b, lt, c), F32)
    return pl.pallas_call(
        functools.partial(_hy_prep_kernel, lc=lc, lt=lt, tm=tm),
        grid=(b, lt // tm),
        in_specs=_halo_specs(tm, cin, lt) + [full((3, cin)), full((1, cin))],
        out_specs=[tok, tok, tok],
        out_shape=[s1, s1, s1],
        compiler_params=_cparams(("parallel", "parallel")),
        name="hy_prep",
    )(uh, uh, uh, w, bias)


def _hy_filter_kernel(z_ref, w1_ref, b1_ref, f1_ref, w2_ref, b2_ref, f2_ref, w3_ref, dec_ref,
                      fw_o, bw_o, nrm_o, *, tl):
    i = pl.program_id(0)
    z = z_ref[...]
    h = jnp.sin(f1_ref[...] * (_dotp(z, w1_ref[...], 3) + b1_ref[...]))
    h = jnp.sin(f2_ref[...] * (_dotp(h, w2_ref[...], 3) + b2_ref[...]))
    h = _dotp(h, w3_ref[...], 3) * jnp.exp(-z[:, 0:1] * dec_ref[...])
    c = HY_WIDTH
    fw = jnp.concatenate([h[:, 0:c], h[:, 2 * c:3 * c]], 1)
    bw = jnp.concatenate([h[:, c:2 * c], h[:, 3 * c:4 * c]], 1)
    rid = i * tl + lax.broadcasted_iota(jnp.int32, (tl, 1), 0)
    bw = jnp.where(rid == 0, 0.0, bw)
    fw_o[...] = fw
    bw_o[...] = bw

    @pl.when(i == 0)
    def _():
        nrm_o[...] = jnp.zeros(nrm_o.shape, F32)

    nrm_o[...] += jnp.sum(jnp.abs(fw) + jnp.abs(bw), 0, keepdims=True)


def _hy_filter(feat, w1p, b1, f1, w2, b2, f2, w3, dec):
    l, fe = feat.shape
    tl = _pick_tile(l, (512, 256))
    c2 = HY_ORDER * HY_WIDTH
    full = lambda shape: pl.BlockSpec(shape, lambda i: (0,) * len(shape))
    return pl.pallas_call(
        functools.partial(_hy_filter_kernel, tl=tl),
        grid=(l // tl,),
        in_specs=[pl.BlockSpec((tl, fe), lambda i: (i, 0)), full(w1p.shape), full(b1.shape), full(f1.shape),
                  full(w2.shape), full(b2.shape), full(f2.shape), full(w3.shape), full(dec.shape)],
        out_specs=[pl.BlockSpec((tl, c2), lambda i: (i, 0)), pl.BlockSpec((tl, c2), lambda i: (i, 0)),
                   pl.BlockSpec((1, c2), lambda i: (0, 0))],
        out_shape=[jax.ShapeDtypeStruct((l, c2), F32), jax.ShapeDtypeStruct((l, c2), F32),
                   jax.ShapeDtypeStruct((1, c2), F32)],
        compiler_params=_cparams(("arbitrary",)),
        name="hy_filter",
    )(feat, w1p, b1, f1, w2, b2, f2, w3, dec)


def _dft_cols_kernel(f_ref, xa_ref, xb_ref, o_ref, *, n1, pair):
    f = f_ref[...]
    pa = _dotp(f, xa_ref[0], P_DFT)
    pb = _dotp(f, xb_ref[0], P_DFT)
    if pair:
        o_ref[0] = pa[:n1].astype(o_ref.dtype)
        o_ref[1] = pa[n1:].astype(o_ref.dtype)
        o_ref[2] = pb[:n1].astype(o_ref.dtype)
        o_ref[3] = pb[n1:].astype(o_ref.dtype)
    else:
        o_ref[0] = (pa[:n1] - pb[n1:]).astype(o_ref.dtype)
        o_ref[1] = (pb[:n1] + pa[n1:]).astype(o_ref.dtype)


def _dft_cols(fstack, xa, ia, xb, ib, pair):
    n1 = fstack.shape[0] // 2
    _, nh, cols = xa.shape
    tc = _pick_tile(cols, (4096, 2048, 1024, 512, 256, 128))
    no = 4 if pair else 2
    return pl.pallas_call(
        functools.partial(_dft_cols_kernel, n1=n1, pair=pair),
        grid=(cols // tc,),
        in_specs=[pl.BlockSpec(fstack.shape, lambda j: (0, 0)),
                  pl.BlockSpec((1, nh, tc), lambda j: (ia, 0, j)),
                  pl.BlockSpec((1, nh, tc), lambda j: (ib, 0, j))],
        out_specs=pl.BlockSpec((no, n1, tc), lambda j: (0, 0, j)),
        out_shape=jax.ShapeDtypeStruct((no, n1, cols), BF16),
        compiler_params=_cparams(("parallel",)),
        name="dft_cols",
    )(fstack, xa, xb)


def _cplx_left(gs, zr, zi, n):
    c = zr.shape[1]
    p = _dotp(gs, jnp.concatenate([zr, zi], 1), P_DFT)
    return p[:n, :c] - p[n:, c:], p[:n, c:] + p[n:, :c]


def _spec_kernel(a_ref, g_ref, nrm_ref, o_ref, *, n_total, kp):
    n2 = DFT_N2
    s = 1.0 / (nrm_ref[...] * n_total)
    for q in range(kp):
        gs = jnp.concatenate([g_ref[q, 0], g_ref[q, 1]], 0)
        fr, fi = _cplx_left(gs, a_ref[0, q], a_ref[1, q], n2)
        br, bi = _cplx_left(gs, a_ref[2, q], a_ref[3, q], n2)
        o_ref[0, q] = ((fr + br) * s).astype(o_ref.dtype)
        o_ref[1, q] = ((fi - bi) * s).astype(o_ref.dtype)


def _planes_per_step(n1):
    return next(k for k in (4, 2, 1) if n1 % k == 0)


def _spec(a4, g, nrm, n_total):
    _, n1, n2, c2 = a4.shape
    kp = _planes_per_step(n1)
    return pl.pallas_call(
        functools.partial(_spec_kernel, n_total=float(n_total), kp=kp),
        grid=(n1 // kp,),
        in_specs=[pl.BlockSpec((4, kp, n2, c2), lambda k: (0, k, 0, 0)),
                  pl.BlockSpec((kp, 2, n2, n2), lambda k: (k, 0, 0, 0)),
                  pl.BlockSpec((1, c2), lambda k: (0, 0))],
        out_specs=pl.BlockSpec((2, kp, n2, c2), lambda k: (0, k, 0, 0)),
        out_shape=jax.ShapeDtypeStruct((2, n1, n2, c2), BF16),
        compiler_params=_cparams(("parallel",)),
        name="hy_spec",
    )(a4, g, nrm)


def _conv_mid_kernel(a_ref, g_ref, k_ref, o_ref, *, kp):
    n2 = DFT_N2
    for q in range(kp):
        gs = jnp.concatenate([g_ref[q, 0], g_ref[q, 1]], 0)
        xr, xi = _cplx_left(gs, a_ref[0, q], a_ref[1, q], n2)
        c = xr.shape[1]
        kr = k_ref[0, q].astype(F32)
        ki = k_ref[1, q].astype(F32)
        zr = xr * kr - xi * ki
        zi = xr * ki + xi * kr
        zst = jnp.concatenate([jnp.concatenate([zr, zi], 1), jnp.concatenate([zi, -zr], 1)], 0)
        y = _dotp(gs, zst, P_DFT, _TN)
        o_ref[0, q] = y[:, :c].astype(o_ref.dtype)
        o_ref[1, q] = y[:, c:].astype(o_ref.dtype)


def _conv_mid(a, g, kspec, order):
    _, n1, n2, c = a.shape
    kp = _planes_per_step(n1)
    return pl.pallas_call(
        functools.partial(_conv_mid_kernel, kp=kp),
        grid=(n1 // kp,),
        in_specs=[pl.BlockSpec((2, kp, n2, c), lambda k: (0, k, 0, 0)),
                  pl.BlockSpec((kp, 2, n2, n2), lambda k: (k, 0, 0, 0)),
                  pl.BlockSpec((2, kp, n2, c), lambda k: (0, k, 0, order))],
        out_specs=pl.BlockSpec((2, kp, n2, c), lambda k: (0, k, 0, 0)),
        out_shape=jax.ShapeDtypeStruct((2, n1, n2, c), BF16),
        compiler_params=_cparams(("parallel",)),
        name="hy_conv_mid",
    )(a, g, kspec)


def _idft_cols_kernel(c_ref, b_ref, g0_ref, g1_ref, x0_ref, x1_ref, bias_ref, o_ref, *, nh):
    cs = c_ref[...]
    pr = _dotp(cs, b_ref[0], P_DFT)
    pi = _dotp(cs, b_ref[1], P_DFT)
    yr = pr[:nh] - pi[nh:]
    yi = pi[:nh] + pr[nh:]
    bias = bias_ref[...]
    o_ref[0] = g0_ref[0] * (yr + x0_ref[0] * bias)
    o_ref[1] = g1_ref[0] * (yi + x1_ref[0] * bias)


def _idft_cols(cstack, bv, gate, xin, bias_cols):
    nh2, n1 = cstack.shape
    nh = nh2 // 2
    cols = bv.shape[-1]
    tc = _pick_tile(cols, (4096, 2048, 1024, 512, 256, 128))
    row = lambda bi: pl.BlockSpec((1, nh, tc), lambda j: (bi, 0, j))
    return pl.pallas_call(
        functools.partial(_idft_cols_kernel, nh=nh),
        grid=(cols // tc,),
        in_specs=[pl.BlockSpec((nh2, n1), lambda j: (0, 0)),
                  pl.BlockSpec((2, n1, tc), lambda j: (0, 0, j)),
                  row(0), row(1), row(0), row(1),
                  pl.BlockSpec((1, tc), lambda j: (0, j))],
        out_specs=pl.BlockSpec((2, nh, tc), lambda j: (0, 0, j)),
        out_shape=jax.ShapeDtypeStruct((2, nh, cols), F32),
        compiler_params=_cparams(("parallel",)),
        name="idft_cols",
    )(cstack, bv, gate, gate, xin, xin, bias_cols)


def _hy_ctx_kernel(x1_ref, x2_ref, v_ref, fw_ref, bw_ref, nrm_ref, bias_ref, ff_ref, ci_ref, o_ref, *, lc):
    n = 2 * lc
    c = HY_WIDTH
    ff = ff_ref[...]
    ci = ci_ref[...]
    pf = _mm(ff, fw_ref[...])
    pb = _mm(ff, bw_ref[...])
    s = 1.0 / (nrm_ref[...] * float(n))
    kr = (pf[:n] + pb[:n]) * s
    ki = (pf[n:] - pb[n:]) * s

    def conv(z0, z1, o):
        xr, xi = _cplx_left(ff, z0, z1, n)
        krr = kr[:, o * c:(o + 1) * c]
        kii = ki[:, o * c:(o + 1) * c]
        return _cplx_left(ci, xr * krr - xi * kii, xr * kii + xi * krr, lc)

    v0 = v_ref[0]
    v1 = v_ref[1]
    y0, y1 = conv(v0, v1, 0)
    z0 = x1_ref[0] * (y0 + v0 * bias_ref[0:1])
    z1 = x1_ref[1] * (y1 + v1 * bias_ref[0:1])
    y0, y1 = conv(z0, z1, 1)
    o_ref[0] = x2_ref[0] * (y0 + z0 * bias_ref[1:2])
    o_ref[1] = x2_ref[1] * (y1 + z1 * bias_ref[1:2])


def _hy_ctx(x1, x2, v, fw, bw, nrm, bias, ff, ci):
    b, lc, c = v.shape
    vm = pl.BlockSpec(memory_space=pltpu.VMEM)
    return pl.pallas_call(
        functools.partial(_hy_ctx_kernel, lc=lc),
        in_specs=[vm] * 9,
        out_specs=vm,
        out_shape=jax.ShapeDtypeStruct((b, lc, c), F32),
        compiler_params=pltpu.CompilerParams(vmem_limit_bytes=VMEM_LIMIT),
        name="hy_ctx",
    )(x1, x2, v, fw, bw, nrm, bias, ff, ci)


def _outproj_kernel(attc_ref, attl_ref, rw_ref, hyc_ref, hyl_ref, x_ref, mod_ref, w_ref, lg_ref, lb_ref, o_ref,
                    *, lc, tm):
    row0 = pl.program_id(1) * tm
    a0 = ATT_WIDTH
    a1 = ATT_WIDTH + RWKV_WIDTH
    is_ctx = row0 < lc
    att = jnp.where(is_ctx, attc_ref[0], attl_ref[0])
    hy = jnp.where(is_ctx, hyc_ref[0], hyl_ref[0])
    o = jnp.dot(att, w_ref[:a0], preferred_element_type=F32)
    o += jnp.dot(rw_ref[0], w_ref[a0:a1], preferred_element_type=F32)
    o += jnp.dot(hy.astype(BF16), w_ref[a1:], preferred_element_type=F32)
    g = _sel_mod(mod_ref, 2, row0, tm, lc)
    y = ALPHA * x_ref[0] + g * o
    o_ref[0] = _layer_norm(y) * lg_ref[...] + lb_ref[...]


def _outproj(att_c, att_l, rw, hy_c, hy_l, xx, mod, w, lg, lb, lc):
    b, lt, d = xx.shape
    tm = TOK_TILE
    nct = lc // tm
    tok = lambda c: pl.BlockSpec((1, tm, c), lambda bi, i: (bi, i, 0))
    ctx = lambda c: pl.BlockSpec((1, tm, c), lambda bi, i: (bi, jnp.minimum(i, nct - 1), 0))
    lat = lambda c: pl.BlockSpec((1, tm, c), lambda bi, i: (bi, jnp.maximum(i - nct, 0), 0))
    full = lambda shape: pl.BlockSpec(shape, lambda bi, i: (0,) * len(shape))
    return pl.pallas_call(
        functools.partial(_outproj_kernel, lc=lc, tm=tm),
        grid=(b, lt // tm),
        in_specs=[ctx(ATT_WIDTH), lat(ATT_WIDTH), tok(RWKV_WIDTH), ctx(HY_WIDTH), lat(HY_WIDTH), tok(d),
                  pl.BlockSpec((1, 2, 6, d), lambda bi, i: (bi, 0, 0, 0)),
                  full(w.shape), full((1, d)), full((1, d))],
        out_specs=tok(d),
        out_shape=jax.ShapeDtypeStruct((b, lt, d), F32),
        compiler_params=_cparams(("parallel", "parallel")),
        name="outproj",
    )(att_c, att_l, rw, hy_c, hy_l, xx, mod, w, lg, lb)


def _ffn_kernel(x_ref, mod_ref, w1_ref, w3_ref, w2_ref, lg_ref, lb_ref, o_ref, h_scr, acc_scr, *, lc, tm, nf):
    row0 = pl.program_id(1) * tm
    f = pl.program_id(2)

    @pl.when(f == 0)
    def _():
        sh = _sel_mod(mod_ref, 3, row0, tm, lc)
        sc = _sel_mod(mod_ref, 4, row0, tm, lc)
        h_scr[...] = (_layer_norm(x_ref[0]) * (1.0 + sc) + sh).astype(BF16)
        acc_scr[...] = jnp.zeros(acc_scr.shape, F32)

    h = h_scr[...]
    a = jnp.dot(h, w1_ref[...], preferred_element_type=F32)
    g = jnp.dot(h, w3_ref[...], preferred_element_type=F32)
    acc_scr[...] += jnp.dot((_silu(a) * g).astype(BF16), w2_ref[...], preferred_element_type=F32)

    @pl.when(f == nf - 1)
    def _():
        gate = _sel_mod(mod_ref, 5, row0, tm, lc)
        y = ALPHA * x_ref[0] + gate * acc_scr[...]
        o_ref[0] = _layer_norm(y) * lg_ref[...] + lb_ref[...]


def _ffn(xx, mod, w1, w3, w2, lg, lb, lc):
    b, lt, d = xx.shape
    ff = w1.shape[1]
    tm = _pick_tile(lt, (1280, 768, 512, 256))
    tf = _pick_tile(ff, (256, 128))
    nf = ff // tf
    return pl.pallas_call(
        functools.partial(_ffn_kernel, lc=lc, tm=tm, nf=nf),
        grid=(b, lt // tm, nf),
        in_specs=[pl.BlockSpec((1, tm, d), lambda bi, i, f: (bi, i, 0)),
                  pl.BlockSpec((1, 2, 6, d), lambda bi, i, f: (bi, 0, 0, 0)),
                  pl.BlockSpec((d, tf), lambda bi, i, f: (0, f)),
                  pl.BlockSpec((d, tf), lambda bi, i, f: (0, f)),
                  pl.BlockSpec((tf, d), lambda bi, i, f: (f, 0)),
                  pl.BlockSpec((1, d), lambda bi, i, f: (0, 0)),
                  pl.BlockSpec((1, d), lambda bi, i, f: (0, 0))],
        out_specs=pl.BlockSpec((1, tm, d), lambda bi, i, f: (bi, i, 0)),
        out_shape=jax.ShapeDtypeStruct((b, lt, d), F32),
        scratch_shapes=[pltpu.VMEM((tm, d), BF16), pltpu.VMEM((tm, d), F32)],
        compiler_params=_cparams(("parallel", "parallel", "arbitrary")),
        name="ffn",
    )(xx, mod, w1, w3, w2, lg, lb)


def _route(x_ref, mod_ref, wr_ref, row0, tm, lc):
    sh = _sel_mod(mod_ref, 3, row0, tm, lc)
    sc = _sel_mod(mod_ref, 4, row0, tm, lc)
    h = _layer_norm(x_ref[0]) * (1.0 + sc) + sh
    logits = _dotp(h, wr_ref[...], 3)
    lane = lax.broadcasted_iota(jnp.int32, logits.shape, 1)
    neg = jnp.float32(-jnp.inf)
    lg = jnp.where(lane < N_EXPERTS, logits, neg)
    m1 = jnp.max(lg, -1, keepdims=True)
    i1 = jnp.min(jnp.where(lg == m1, lane, LANES), -1, keepdims=True)
    lg2 = jnp.where(lane == i1, neg, lg)
    m2 = jnp.max(lg2, -1, keepdims=True)
    i2 = jnp.min(jnp.where(lg2 == m2, lane, LANES), -1, keepdims=True)
    e2 = jnp.exp(m2 - m1)
    return h, lane, i1, i2, 1.0 / (1.0 + e2), e2 / (1.0 + e2)


def _moe_sparse_kernel(x_ref, mod_ref, wr_ref, w1_ref, w3_ref, w2_ref, lg_ref, lb_ref, o_ref,
                       h_scr, acc_scr, gate_scr, posc_scr, posr_scr, xs_scr, ye_scr, nblk_scr, *, lc, tm, nf, ns, blk):
    row0 = pl.program_id(1) * tm
    s = pl.program_id(2)
    e = s // nf
    f = s % nf

    @pl.when(s == 0)
    def _():
        h, lane, i1, i2, g1, g2 = _route(x_ref, mod_ref, wr_ref, row0, tm, lc)
        h_scr[...] = h.astype(BF16)
        acc_scr[...] = jnp.zeros(acc_scr.shape, F32)
        routed = jnp.where(jnp.logical_or(lane == i1, lane == i2), 1.0, 0.0)
        r = lax.broadcasted_iota(jnp.int32, (tm, tm), 0)
        c = lax.broadcasted_iota(jnp.int32, (tm, tm), 1)
        before = jnp.where(c < r, 1.0, 0.0).astype(BF16)
        rank_c = jnp.dot(before, routed.astype(BF16), preferred_element_type=F32)
        gate_scr[...] = jnp.where(lane == i1, g1, 0.0) + jnp.where(lane == i2, g2, 0.0)
        posc_scr[...] = jnp.where(routed > 0.0, rank_c, -1.0)
        routed_t = routed.T[:2 * SUBLANES]
        after = jnp.where(r < c, 1.0, 0.0).astype(BF16)
        rank_r = jnp.dot(routed_t.astype(BF16), after, preferred_element_type=F32)
        posr_scr[...] = jnp.where(routed_t > 0.0, rank_r, -1.0)
        counts = jnp.sum(routed, 0, keepdims=True)
        lane1 = lax.broadcasted_iota(jnp.int32, counts.shape, 1)
        for ex in range(N_EXPERTS):
            n_rows = jnp.sum(jnp.where(lane1 == ex, counts, 0.0)).astype(jnp.int32)
            nblk_scr[ex] = (n_rows + (blk - 1)) // blk

    nblk = nblk_scr[e]
    mv = MOE_MOVE_BLOCK
    nmv = (nblk * blk + (mv - 1)) // mv
    rows = lambda i: pl.ds(pl.multiple_of(i * mv, mv), mv)

    @pl.when(f == 0)
    def _():
        def gather(i, carry):
            slot = (lax.broadcasted_iota(jnp.int32, (mv, tm), 0) + i * mv).astype(F32)
            take = jnp.where(slot == posr_scr[pl.ds(e, 1), :], 1.0, 0.0).astype(BF16)
            xs_scr[rows(i), :] = jnp.dot(take, h_scr[...], preferred_element_type=F32).astype(BF16)
            ye_scr[rows(i), :] = jnp.zeros((mv, ye_scr.shape[1]), F32)
            return carry

        lax.fori_loop(0, nmv, gather, 0)

    def expert(m):
        xs = xs_scr[:m]
        a = jnp.dot(xs, w1_ref[0], preferred_element_type=F32)
        g = jnp.dot(xs, w3_ref[0], preferred_element_type=F32)
        ye_scr[:m] += jnp.dot((_silu(a) * g).astype(BF16), w2_ref[0], preferred_element_type=F32)

    for k in range(tm // blk):
        pl.when(nblk == k + 1)(functools.partial(expert, (k + 1) * blk))

    @pl.when(f == nf - 1)
    def _():
        lane = lax.broadcasted_iota(jnp.int32, (tm, LANES), 1)
        column = lambda ref: jnp.sum(jnp.where(lane == e, ref[...], 0.0), -1, keepdims=True)
        pos = column(posc_scr)
        gate = column(gate_scr)

        def scatter(i, carry):
            slot = (lax.broadcasted_iota(jnp.int32, (tm, mv), 1) + i * mv).astype(F32)
            put = jnp.where(slot == pos, 1.0, 0.0).astype(BF16)
            acc_scr[...] += gate * jnp.dot(put, ye_scr[rows(i), :].astype(BF16), preferred_element_type=F32)
            return carry

        lax.fori_loop(0, nmv, scatter, 0)

    @pl.when(s == ns - 1)
    def _():
        gate = _sel_mod(mod_ref, 5, row0, tm, lc)
        y = ALPHA * x_ref[0] + gate * acc_scr[...]
        o_ref[0] = _layer_norm(y) * lg_ref[...] + lb_ref[...]


def _moe_sparse(xx, mod, wr, w1, w3, w2, lg, lb, lc):
    b, lt, d = xx.shape
    ne, _, ff = w1.shape
    tm = _pick_tile(lt, (1280, 768, 512, 256))
    tf = _pick_tile(ff, (256, 128))
    nf = ff // tf
    ns = ne * nf
    return pl.pallas_call(
        functools.partial(_moe_sparse_kernel, lc=lc, tm=tm, nf=nf, ns=ns, blk=MOE_ROW_BLOCK),
        grid=(b, lt // tm, ns),
        in_specs=[pl.BlockSpec((1, tm, d), lambda bi, i, s: (bi, i, 0)),
                  pl.BlockSpec((1, 2, 6, d), lambda bi, i, s: (bi, 0, 0, 0)),
                  pl.BlockSpec((d, LANES), lambda bi, i, s: (0, 0)),
                  pl.BlockSpec((1, d, tf), lambda bi, i, s: (s // nf, 0, s % nf)),
                  pl.BlockSpec((1, d, tf), lambda bi, i, s: (s // nf, 0, s % nf)),
                  pl.BlockSpec((1, tf, d), lambda bi, i, s: (s // nf, s % nf, 0)),
                  pl.BlockSpec((1, d), lambda bi, i, s: (0, 0)),
                  pl.BlockSpec((1, d), lambda bi, i, s: (0, 0))],
        out_specs=pl.BlockSpec((1, tm, d), lambda bi, i, s: (bi, i, 0)),
        out_shape=jax.ShapeDtypeStruct((b, lt, d), F32),
        scratch_shapes=[pltpu.VMEM((tm, d), BF16), pltpu.VMEM((tm, d), F32),
                        pltpu.VMEM((tm, LANES), F32), pltpu.VMEM((tm, LANES), F32),
                        pltpu.VMEM((2 * SUBLANES, tm), F32),
                        pltpu.VMEM((tm, d), BF16), pltpu.VMEM((tm, d), F32),
                        pltpu.SMEM((ne,), jnp.int32)],
        compiler_params=_cparams(("parallel", "parallel", "arbitrary"), VMEM_LIMIT_MOE),
        name="moe_sparse",
    )(xx, mod, wr, w1, w3, w2, lg, lb)


def _rope_tables(l, lc):
    rows = l // GRID_W
    row = jnp.repeat(jnp.arange(rows, dtype=F32), GRID_W)
    col = jnp.tile(jnp.arange(GRID_W, dtype=F32), rows)
    n_freq = HEAD_DIM // 4
    inv_freq = ROPE_THETA ** (-jnp.arange(n_freq, dtype=F32) / n_freq)
    ang = jnp.concatenate([row[:, None] * inv_freq, col[:, None] * inv_freq], -1)
    cos, sin = jnp.cos(ang), jnp.sin(ang)
    cos2 = jnp.concatenate([jnp.ones((lc, LANES), F32), jnp.concatenate([cos, cos, cos, cos], -1)], 0)
    sin2 = jnp.concatenate([jnp.zeros((lc, LANES), F32), jnp.concatenate([-sin, sin, -sin, sin], -1)], 0)
    return cos2, sin2


def _hy_features(l):
    bands = (HY_EMB - 1) // 2
    t = jnp.linspace(0.0, 1.0, l, dtype=F32)[:, None]
    f = jnp.linspace(1e-4, bands - 1, bands, dtype=F32)[None, :]
    wt = 2.0 * math.pi * jnp.arange(l, dtype=F32)[:, None] / l
    z = jnp.concatenate([t, jnp.cos(f * wt), -jnp.sin(f * wt)], -1)
    return jnp.pad(z, ((0, 0), (0, LANES - HY_EMB)))


def _angle(idx, n):
    return (2.0 * math.pi / n) * (idx % n).astype(F32)


def _dft_tables(n1):
    nh = n1 // 2
    n2 = DFT_N2
    n = n1 * n2
    k1 = jnp.arange(n1, dtype=jnp.int32)
    a1 = _angle(k1[:, None] * jnp.arange(nh, dtype=jnp.int32)[None, :], n1)
    fstack = jnp.concatenate([jnp.cos(a1), -jnp.sin(a1)], 0)
    cstack = jnp.concatenate([jnp.cos(a1.T), jnp.sin(a1.T)], 0)
    k2 = jnp.arange(n2, dtype=jnp.int32)
    at = _angle(k1[:, None] * k2[None, :], n)
    tr, ti = jnp.cos(at), -jnp.sin(at)
    a2 = _angle(k2[:, None] * k2[None, :], n2)
    fr, fi = jnp.cos(a2), -jnp.sin(a2)
    g = jnp.stack([tr[:, None, :] * fr[None] - ti[:, None, :] * fi[None],
                   tr[:, None, :] * fi[None] + ti[:, None, :] * fr[None]], 1)
    return fstack, cstack, g


def _dense_dft_tables(lc):
    n = 2 * lc
    a = _angle(jnp.arange(n, dtype=jnp.int32)[:, None] * jnp.arange(lc, dtype=jnp.int32)[None, :], n)
    ff = jnp.concatenate([jnp.cos(a), -jnp.sin(a)], 0)
    ci = jnp.concatenate([jnp.cos(a.T), jnp.sin(a.T)], 0)
    return ff, ci


def kernel(x, c, ctx, c_ctx, ada_w, ada_b, w_in, w_out, q_gain, k_gain, rwkv_mu, rwkv_w0, rwkv_wB, rwkv_a0, rwkv_aB, rwkv_gB, rwkv_kk, rwkv_ka, rwkv_rk, rwkv_gn_g, rwkv_gn_b, hy_short_w, hy_short_b, hy_w1, hy_b1, hy_freq1, hy_w2, hy_b2, hy_freq2, hy_w3, hy_decay, hy_bias, ln1_g, ln1_b, ln2_g, ln2_b, ffn_w1, ffn_w3, ffn_w2, moe_router, moe_w1, moe_w3, moe_w2):
    b, l, d = x.shape
    lc = ctx.shape[1]
    lt = lc + l
    depth = ada_w.shape[0]
    assert b == 2, "the long convolution packs the two batch rows as one complex signal"
    assert d == D_MODEL and lc % TOK_TILE == 0 and l % TOK_TILE == 0 and (2 * l) % (2 * DFT_N2) == 0
    cw = RWKV_WIDTH

    xx = jnp.concatenate([ctx, x], 1)
    cond8 = jnp.zeros((SUBLANES, d), F32).at[:b].set(c).at[b].set(c_ctx)
    mod_all = _ada_mod(cond8, ada_w, ada_b)

    cos64, sin64 = _rope_tables(l, lc)
    n1 = 2 * l // DFT_N2
    nh = n1 // 2
    cols = DFT_N2 * HY_WIDTH
    fstack, cstack, g_tab = (t.astype(BF16) for t in _dft_tables(n1))
    ff_c, ci_c = _dense_dft_tables(lc)
    feat_l = _hy_features(l)
    feat_c = _hy_features(lc)
    blk = jnp.arange(cw) // HEAD_DIM
    bd = (blk[:, None] == blk[None, :]).astype(F32)
    ch = jnp.arange(ATT_WIDTH)
    bd_att = (ch[:, None] // HEAD_DIM == ch[None, :] // HEAD_DIM).astype(BF16)
    swap_att = (ch[:, None] == (ch[None, :] + HALF_HD) % HEAD_DIM + (ch[None, :] // HEAD_DIM) * HEAD_DIM
                ).astype(BF16)
    perm64 = jnp.concatenate([jnp.arange(0, HEAD_DIM, 2), jnp.arange(1, HEAD_DIM, 2)])
    perm_att = jnp.concatenate([h * HEAD_DIM + perm64 for h in range(ATT_HEADS + ATT_KV_HEADS)]
                               + [jnp.arange(ATT_WIDTH + ATT_KV_WIDTH, IN_ATT)])
    tq = TOK_TILE
    qn = next(n for n in (4, 2, 1) if (l // tq) % n == 0)
    tk = _pick_tile(lt, (3328, 1280, 1024, 768, 512, 256))

    for li in range(depth):
        ml = mod_all[li]
        mod = jnp.stack([jnp.broadcast_to(ml[b].reshape(1, 6, d), (b, 6, d)), ml[:b].reshape(b, 6, d)], 1)
        wi = w_in[li]
        w_pad = jnp.concatenate([wi[:, :IN_ATT][:, perm_att], wi[:, IN_ATT:IN_ATT + IN_RWKV],
                                 jnp.zeros((d, IN_RWKV_PAD - IN_RWKV), F32), wi[:, IN_ATT + IN_RWKV:]],
                                1).astype(BF16)
        ua, ur, uh = _inproj(xx, mod, w_pad, lc)

        two = lambda gain: jnp.tile(gain[perm64], 2)[None]
        qt, kx, vt = _attn_prep(ua, cos64, sin64, two(q_gain[li]), two(k_gain[li]), bd_att, swap_att, lc)
        att_c = _flash(qt, kx, vt, tq, TOK_TILE, 1, l // tq, lc // tq, lc // TOK_TILE)
        s_bound = (HEAD_DIM ** 0.5) * LOG2E * jnp.max(jnp.abs(q_gain[li])) * jnp.max(jnp.abs(k_gain[li]))
        lat_args = (qt, kx, vt, tq, tk, qn, 0, l // (qn * tq), lt // tk)
        att_l = lax.cond(s_bound <= MAX_UNSHIFTED_SCORE,
                         lambda: _flash(*lat_args, bounded=True), lambda: _flash(*lat_args, bounded=False))

        wl = jnp.zeros((LANES, 5 * cw), F32)
        wl = wl.at[0:W_LORA, 0:cw].set(rwkv_wB[li, 0]).at[W_LORA:2 * W_LORA, cw:2 * cw].set(rwkv_wB[li, 1])
        o_a = 2 * W_LORA
        wl = wl.at[o_a:o_a + A_LORA, 2 * cw:3 * cw].set(rwkv_aB[li, 0])
        wl = wl.at[o_a + A_LORA:o_a + 2 * A_LORA, 3 * cw:4 * cw].set(rwkv_aB[li, 1])
        o_g = o_a + 2 * A_LORA
        wl = wl.at[o_g:o_g + G_LORA, 4 * cw:5 * cw].set(rwkv_gB[li])
        mu = jnp.pad(rwkv_mu[li], ((0, 0), (0, IN_RWKV_PAD - IN_RWKV)))
        r_, v_, kk_, g_, bon_, lw_, be_, kd_ = _rwkv_prep(
            ur, mu, wl, bd, rwkv_w0[li], rwkv_a0[li], rwkv_kk[li][None], rwkv_ka[li][None],
            rwkv_rk[li].reshape(1, cw), lc)
        yf, yb = _wkv_scan(r_, v_, kk_, lw_, be_, kd_, lc)
        rw = _rwkv_out(yf, yb, bon_, g_, bd, rwkv_gn_g[li][None], rwkv_gn_b[li][None])

        x1, x2, vv = _hy_prep(uh, hy_short_w[li], hy_short_b[li][None], lc)
        w1p = jnp.pad(hy_w1[li], ((0, LANES - HY_EMB), (0, 0)))
        fargs = (w1p, hy_b1[li][None], hy_freq1[li][None], hy_w2[li], hy_b2[li][None], hy_freq2[li][None],
                 hy_w3[li], hy_decay[li][None])
        fw, bw, nrm = _hy_filter(feat_l, *fargs)
        c2 = HY_ORDER * HY_WIDTH
        a4 = _dft_cols(fstack, fw.reshape(1, nh, DFT_N2 * c2), 0, bw.reshape(1, nh, DFT_N2 * c2), 0, True)
        kspec = _spec(a4.reshape(4, n1, DFT_N2, c2), g_tab, nrm, n1 * DFT_N2)
        lat = lambda t: t[:, lc:].reshape(b, nh, cols)
        x1l, x2l, zin = lat(x1), lat(x2), lat(vv)
        for o, gate in enumerate((x1l, x2l)):
            a = _dft_cols(fstack, zin, 0, zin, 1, False)
            bv = _conv_mid(a.reshape(2, n1, DFT_N2, HY_WIDTH), g_tab, kspec, o)
            bias_cols = jnp.tile(hy_bias[li, o], DFT_N2)[None]
            zin = _idft_cols(cstack, bv.reshape(2, n1, cols), gate, zin, bias_cols)
        hy_l = zin.reshape(b, l, HY_WIDTH)
        fw_c, bw_c, nrm_c = _hy_filter(feat_c, *fargs)
        hy_c = _hy_ctx(x1[:, :lc], x2[:, :lc], vv[:, :lc], fw_c, bw_c, nrm_c, hy_bias[li], ff_c, ci_c)

        xx = _outproj(att_c, att_l, rw, hy_c, hy_l, xx, mod, w_out[li].astype(BF16),
                      ln1_g[li][None], ln1_b[li][None], lc)

        j = li // 2
        if li % 2 == 0:
            xx = _ffn(xx, mod, ffn_w1[j].astype(BF16), ffn_w3[j].astype(BF16), ffn_w2[j].astype(BF16),
                      ln2_g[li][None], ln2_b[li][None], lc)
        else:
            wr = jnp.pad(moe_router[j], ((0, 0), (0, LANES - N_EXPERTS)))
            xx = _moe_sparse(xx, mod, wr, moe_w1[j].astype(BF16), moe_w3[j].astype(BF16), moe_w2[j].astype(BF16),
                             ln2_g[li][None], ln2_b[li][None], lc)
    return xx[:, lc:]
```

```python
import functools
import math

import jax
import jax.numpy as jnp
from jax import lax
from jax.experimental import pallas as pl
from jax.experimental.pallas import tpu as pltpu

F32 = jnp.float32
BF16 = jnp.bfloat16
HI = lax.Precision.HIGHEST

D_MODEL = 1024
DEPTH = 2
GRID_W = 64
HEAD_DIM = 64
HALF_HD = HEAD_DIM // 2
ATT_WIDTH = 512
RWKV_WIDTH = 256
HY_WIDTH = 256
ATT_HEADS = 8
ATT_KV_HEADS = 2
ATT_REP = 4
ATT_KV_WIDTH = 128
ROPE_THETA = 10000.0
QK_EPS = 1e-6
RWKV_HEADS = 4
W_LORA = 16
A_LORA = 16
G_LORA = 32
RWKV_GN_EPS = 64e-5
HY_ORDER = 2
HY_EMB = 33
HY_FFN = 64
N_EXPERTS = 8
LN_EPS = 1e-6
IN_ATT = ATT_WIDTH + 2 * ATT_KV_WIDTH
IN_RWKV = 3 * RWKV_WIDTH + 2 * W_LORA + 2 * A_LORA + G_LORA
IN_RWKV_PAD = 896
IN_HY = 3 * HY_WIDTH
ALPHA = float((2 * DEPTH) ** 0.25)
LOG2E = 1.4426950408889634
MAX_UNSHIFTED_SCORE = 40.0

LANES = 128
SUBLANES = 8
TOK_TILE = 256
ROW_TILE_MID = 640
ROW_TILE_WIDE = 1280
WKV_CHUNK = 64
WKV_TILE = 256
DFT_N2 = 256
MOE_ROW_BLOCK = 128
MOE_MOVE_BLOCK = 256
VMEM_LIMIT = 48 * 1024 * 1024
VMEM_LIMIT_MOE = 56 * 1024 * 1024


def _cparams(sem, vmem=VMEM_LIMIT):
    return pltpu.CompilerParams(dimension_semantics=sem, vmem_limit_bytes=vmem)


def _pick_tile(n, cands):
    for c in cands:
        if n % c == 0:
            return c
    raise ValueError(f"no tile for {n} in {cands}")


def _layer_norm(x):
    mu = jnp.mean(x, -1, keepdims=True)
    xc = x - mu
    var = jnp.mean(xc * xc, -1, keepdims=True)
    return xc * lax.rsqrt(var + LN_EPS)


def _sel_mod(mod_ref, j, row0, tm, lc):
    rid = row0 + lax.broadcasted_iota(jnp.int32, (tm, 1), 0)
    return jnp.where(rid < lc, mod_ref[0, 0, j:j + 1, :], mod_ref[0, 1, j:j + 1, :])


def _silu(x):
    return x * jax.nn.sigmoid(x)


def _ada_kernel(c_ref, w_ref, b_ref, o_ref):
    s = _silu(c_ref[...])
    o_ref[0] = jnp.dot(s, w_ref[0], precision=HI, preferred_element_type=F32) + b_ref[0]


def _ada_mod(cond8, ada_w, ada_b):
    depth, d, n = ada_w.shape
    tn = _pick_tile(n, (1536, 1024, 512, 256, 128))
    return pl.pallas_call(
        _ada_kernel,
        grid=(depth, n // tn),
        in_specs=[pl.BlockSpec((SUBLANES, d), lambda l, j: (0, 0)),
                  pl.BlockSpec((1, d, tn), lambda l, j: (l, 0, j)),
                  pl.BlockSpec((1, 1, tn), lambda l, j: (l, 0, j))],
        out_specs=pl.BlockSpec((1, SUBLANES, tn), lambda l, j: (l, 0, j)),
        out_shape=jax.ShapeDtypeStruct((depth, SUBLANES, n), F32),
        compiler_params=_cparams(("parallel", "parallel")),
        name="ada_mod",
    )(cond8, ada_w, ada_b.reshape(depth, 1, n))


def _inproj_kernel(x_ref, mod_ref, w_ref, oa_ref, or_ref, oh_ref, *, lc, tm):
    row0 = pl.program_id(1) * tm
    sh = _sel_mod(mod_ref, 0, row0, tm, lc)
    sc = _sel_mod(mod_ref, 1, row0, tm, lc)
    h = (_layer_norm(x_ref[0]) * (1.0 + sc) + sh).astype(BF16)
    u = jnp.dot(h, w_ref[...], preferred_element_type=F32)
    oa_ref[0] = u[:, :IN_ATT]
    or_ref[0] = u[:, IN_ATT:IN_ATT + IN_RWKV_PAD]
    oh_ref[0] = u[:, IN_ATT + IN_RWKV_PAD:]


def _inproj(xx, mod, w_pad, lc):
    b, lt, d = xx.shape
    tm = _pick_tile(lt, (ROW_TILE_MID, TOK_TILE))
    n = w_pad.shape[1]
    return pl.pallas_call(
        functools.partial(_inproj_kernel, lc=lc, tm=tm),
        grid=(b, lt // tm),
        in_specs=[pl.BlockSpec((1, tm, d), lambda bi, i: (bi, i, 0)),
                  pl.BlockSpec((1, 2, 6, d), lambda bi, i: (bi, 0, 0, 0)),
                  pl.BlockSpec((d, n), lambda bi, i: (0, 0))],
        out_specs=[pl.BlockSpec((1, tm, IN_ATT), lambda bi, i: (bi, i, 0)),
                   pl.BlockSpec((1, tm, IN_RWKV_PAD), lambda bi, i: (bi, i, 0)),
                   pl.BlockSpec((1, tm, IN_HY), lambda bi, i: (bi, i, 0))],
        out_shape=[jax.ShapeDtypeStruct((b, lt, IN_ATT), F32),
                   jax.ShapeDtypeStruct((b, lt, IN_RWKV_PAD), F32),
                   jax.ShapeDtypeStruct((b, lt, IN_HY), F32)],
        compiler_params=_cparams(("parallel", "parallel")),
        name="inproj",
    )(xx, mod, w_pad)


def _attn_prep_kernel(u_ref, cos_ref, sin_ref, qg_ref, kg_ref, bd_ref, sw_ref, qt_ref, k_ref, vt_ref):
    u = u_ref[0]

    def norm_rope(x, g):
        w = x.shape[1]
        tile = lambda t: jnp.concatenate([t] * (w // LANES), -1)
        ms = _dot01(x * x, bd_ref[:w, :w]) * (1.0 / HEAD_DIM)
        xn = x * lax.rsqrt(ms + QK_EPS) * tile(g)
        sw = _dot01(xn, sw_ref[:w, :w])
        return xn * tile(cos_ref[...]) + sw * tile(sin_ref[...])

    q = norm_rope(u[:, :ATT_WIDTH], qg_ref[...]) * (LOG2E * HEAD_DIM ** -0.5)
    qt = q.T
    kx = norm_rope(u[:, ATT_WIDTH:ATT_WIDTH + ATT_KV_WIDTH], kg_ref[...])
    for g in range(ATT_KV_HEADS):
        base = g * ATT_REP * HEAD_DIM
        qt_ref[0, g, 0] = jnp.concatenate(
            [qt[base + r * HEAD_DIM:base + (r + 1) * HEAD_DIM] for r in range(ATT_REP)], -1).astype(BF16)
        k_ref[0, g] = kx[:, g * HEAD_DIM:(g + 1) * HEAD_DIM].astype(BF16)
    v0 = ATT_WIDTH + ATT_KV_WIDTH
    vt = u[:, v0:v0 + ATT_KV_WIDTH].T
    for g in range(ATT_KV_HEADS):
        vt_ref[0, g] = vt[g * HEAD_DIM:(g + 1) * HEAD_DIM].astype(BF16)


def _attn_prep(ua, cos, sin, qg, kg, bd_att, swap_att, lc):
    b, lt, _ = ua.shape
    tm = TOK_TILE
    nct = lc // tm
    nl = lt // tm - nct
    q_pos = lambda i: jnp.where(i < nct, nl + i, i - nct)
    return pl.pallas_call(
        _attn_prep_kernel,
        grid=(b, lt // tm),
        in_specs=[pl.BlockSpec((1, tm, IN_ATT), lambda bi, i: (bi, i, 0)),
                  pl.BlockSpec((tm, LANES), lambda bi, i: (i, 0)),
                  pl.BlockSpec((tm, LANES), lambda bi, i: (i, 0)),
                  pl.BlockSpec((1, LANES), lambda bi, i: (0, 0)),
                  pl.BlockSpec((1, LANES), lambda bi, i: (0, 0)),
                  pl.BlockSpec((ATT_WIDTH, ATT_WIDTH), lambda bi, i: (0, 0)),
                  pl.BlockSpec((ATT_WIDTH, ATT_WIDTH), lambda bi, i: (0, 0))],
        out_specs=[pl.BlockSpec((1, ATT_KV_HEADS, 1, HEAD_DIM, ATT_REP * tm),
                                lambda bi, i: (bi, 0, q_pos(i), 0, 0)),
                   pl.BlockSpec((1, ATT_KV_HEADS, tm, HEAD_DIM), lambda bi, i: (bi, 0, i, 0)),
                   pl.BlockSpec((1, ATT_KV_HEADS, HEAD_DIM, tm), lambda bi, i: (bi, 0, 0, i))],
        out_shape=[jax.ShapeDtypeStruct((b, ATT_KV_HEADS, lt // tm, HEAD_DIM, ATT_REP * tm), BF16),
                   jax.ShapeDtypeStruct((b, ATT_KV_HEADS, lt, HEAD_DIM), BF16),
                   jax.ShapeDtypeStruct((b, ATT_KV_HEADS, HEAD_DIM, lt), BF16)],
        compiler_params=_cparams(("parallel", "parallel")),
        name="attn_prep",
    )(ua, cos, sin, qg, kg, bd_att, swap_att)


def _flash_kernel(qt_ref, k_ref, vt_ref, o_ref, m_scr, l_scr, acc_scr, *, nk, tq, sub, qn):
    j = pl.program_id(3)

    @pl.when(j == 0)
    def _():
        m_scr[...] = jnp.full(m_scr.shape, -jnp.inf, F32)
        l_scr[...] = jnp.zeros(l_scr.shape, F32)
        acc_scr[...] = jnp.zeros(acc_scr.shape, F32)

    qt = jnp.concatenate([qt_ref[0, 0, t] for t in range(qn)], -1)
    nsub = k_ref.shape[2] // sub
    m = m_scr[...]
    l = l_scr[...]
    acc = acc_scr[...]
    scores = lambda c: jnp.dot(k_ref[0, 0, c * sub:(c + 1) * sub, :], qt, preferred_element_type=F32)
    pv = lambda c, p: jnp.dot(vt_ref[0, 0, :, c * sub:(c + 1) * sub], p, preferred_element_type=F32)
    s_next = scores(0)
    pend = None
    for c in range(nsub):
        s = s_next
        if c + 1 < nsub:
            s_next = scores(c + 1)
        if pend is not None:
            acc = pend[0] * acc + pv(c - 1, pend[1])
        m_new = jnp.maximum(m, jnp.max(s, 0, keepdims=True))
        a = jnp.exp2(m - m_new)
        p = jnp.exp2(s - m_new)
        l = a * l + jnp.sum(p, 0, keepdims=True)
        pend = (a, p.astype(BF16))
        m = m_new
    acc = pend[0] * acc + pv(nsub - 1, pend[1])
    m_scr[...] = m
    l_scr[...] = l
    acc_scr[...] = acc

    @pl.when(j == nk - 1)
    def _():
        o = (acc / l).T
        for t in range(qn):
            o_ref[0, t * tq:(t + 1) * tq, :] = jnp.concatenate(
                [o[(t * ATT_REP + r) * tq:(t * ATT_REP + r + 1) * tq] for r in range(ATT_REP)], -1
            ).astype(o_ref.dtype)


def _flash_bounded_kernel(qt_ref, k_ref, vt_ref, o_ref, l_scr, acc_scr, *, nk, tq, sub, qn):
    j = pl.program_id(3)

    @pl.when(j == 0)
    def _():
        l_scr[...] = jnp.zeros(l_scr.shape, F32)
        acc_scr[...] = jnp.zeros(acc_scr.shape, F32)

    qt = jnp.concatenate([qt_ref[0, 0, t] for t in range(qn)], -1)
    nsub = k_ref.shape[2] // sub

    l = l_scr[...]
    acc = acc_scr[...]
    scores = lambda c: jnp.dot(k_ref[0, 0, c * sub:(c + 1) * sub, :], qt, preferred_element_type=F32)
    s_next = scores(0)
    for c in range(nsub):
        s = s_next
        if c + 1 < nsub:
            s_next = scores(c + 1)
        p = jnp.exp2(s)
        l = l + jnp.sum(p.reshape(sub // SUBLANES, SUBLANES, p.shape[1]), 0)
        acc = acc + jnp.dot(vt_ref[0, 0, :, c * sub:(c + 1) * sub], p.astype(BF16), preferred_element_type=F32)
    l_scr[...] = l
    acc_scr[...] = acc

    @pl.when(j == nk - 1)
    def _():
        o = (acc / jnp.sum(l, 0, keepdims=True)).T
        for t in range(qn):
            o_ref[0, t * tq:(t + 1) * tq, :] = jnp.concatenate(
                [o[(t * ATT_REP + r) * tq:(t * ATT_REP + r + 1) * tq] for r in range(ATT_REP)], -1
            ).astype(o_ref.dtype)


def _flash(qt, k, vt, tq, tk, qn, q_blk0, nq, nk, bounded=False):
    b = qt.shape[0]
    lq = nq * qn * tq
    sub = _pick_tile(tk, (256, 128))
    lanes = qn * ATT_REP * tq
    if bounded:
        body = functools.partial(_flash_bounded_kernel, nk=nk, tq=tq, sub=sub, qn=qn)
        scratch = [pltpu.VMEM((SUBLANES, lanes), F32), pltpu.VMEM((HEAD_DIM, lanes), F32)]
    else:
        body = functools.partial(_flash_kernel, nk=nk, tq=tq, sub=sub, qn=qn)
        scratch = [pltpu.VMEM((1, lanes), F32), pltpu.VMEM((1, lanes), F32), pltpu.VMEM((HEAD_DIM, lanes), F32)]
    return pl.pallas_call(
        body,
        grid=(b, ATT_KV_HEADS, nq, nk),
        in_specs=[pl.BlockSpec((1, 1, qn, HEAD_DIM, ATT_REP * tq), lambda bi, g, i, j: (bi, g, i + q_blk0, 0, 0)),
                  pl.BlockSpec((1, 1, tk, HEAD_DIM), lambda bi, g, i, j: (bi, g, j, 0)),
                  pl.BlockSpec((1, 1, HEAD_DIM, tk), lambda bi, g, i, j: (bi, g, 0, j))],
        out_specs=pl.BlockSpec((1, qn * tq, ATT_REP * HEAD_DIM), lambda bi, g, i, j: (bi, i, g)),
        out_shape=jax.ShapeDtypeStruct((b, lq, ATT_WIDTH), BF16),
        scratch_shapes=scratch,
        compiler_params=_cparams(("parallel", "parallel", "parallel", "arbitrary")),
        name="flash_bounded" if bounded else "flash",
    )(qt, k, vt)


def _prev_next(u, up8, un8, i, tm, lc, lt):
    rid = lax.broadcasted_iota(jnp.int32, u.shape, 0)
    gid = rid + i * tm
    prev = jnp.where(rid == 0, up8[SUBLANES - 1:SUBLANES], pltpu.roll(u, 1, 0))
    prev = jnp.where(gid == 0, 0.0, jnp.where(gid == lc, 0.0, prev))
    nxt = jnp.where(rid == tm - 1, un8[0:1], pltpu.roll(u, tm - 1, 0))
    nxt = jnp.where(gid == lc - 1, 0.0, jnp.where(gid == lt - 1, 0.0, nxt))
    return prev, nxt


def _halo_specs(tm, c, lt):
    r = tm // SUBLANES
    last = lt // SUBLANES - 1
    return [pl.BlockSpec((1, tm, c), lambda bi, i: (bi, i, 0)),
            pl.BlockSpec((1, SUBLANES, c), lambda bi, i: (bi, jnp.maximum(i * r - 1, 0), 0)),
            pl.BlockSpec((1, SUBLANES, c), lambda bi, i: (bi, jnp.minimum((i + 1) * r, last), 0))]


def _softplus(z):
    return jnp.maximum(z, 0.0) + jnp.log1p(jnp.exp(-jnp.abs(z)))


def _rwkv_prep_kernel(u_ref, up_ref, un_ref, mu_ref, wl_ref, bd_ref, w0_ref, a0_ref, kkw_ref, ka_ref, rk_ref,
                      r_o, v_o, kk_o, g_o, bon_o, lw_o, be_o, kd_o, *, lc, lt, tm):
    i = pl.program_id(1)
    u = u_ref[0]
    prev, nxt = _prev_next(u, up_ref[0], un_ref[0], i, tm, lc, lt)
    us = u + mu_ref[0:1] * (prev - u) + mu_ref[1:2] * (nxt - u)
    c = RWKV_WIDTH
    r = us[:, 0:c]
    k = us[:, c:2 * c]
    v = us[:, 2 * c:3 * c]
    slab = us[:, 3 * c:3 * c + LANES]
    lane = lax.broadcasted_iota(jnp.int32, slab.shape, 1)
    o_a = 2 * W_LORA
    o_g = o_a + 2 * A_LORA
    act = jnp.where(lane < o_a, jnp.tanh(slab),
                    jnp.where(lane < o_g, slab,
                              jnp.where(lane < o_g + G_LORA, jax.nn.sigmoid(slab), 0.0)))
    lo = _dotp(act, wl_ref[...], 3)
    bd = bd_ref[...]
    kk0 = k * kkw_ref[...]
    kk = kk0 * lax.rsqrt(_dot01(kk0 * kk0, bd) + 1e-12)
    r_o[0] = r
    v_o[0] = v
    kk_o[0] = kk
    g_o[0] = lo[:, 4 * c:5 * c]
    bon = jnp.zeros_like(r)
    for d in range(2):
        w_raw = w0_ref[d:d + 1] + lo[:, d * c:(d + 1) * c]
        lw = -jnp.exp(-_softplus(-w_raw) - 0.5)
        a = jax.nn.sigmoid(a0_ref[d:d + 1] + lo[:, (2 + d) * c:(3 + d) * c])
        kd = k * (1.0 + (a - 1.0) * ka_ref[...])
        lw_o[0, d] = lw
        be_o[0, d] = a * kk
        kd_o[0, d] = kd
        bon = bon + r * kd * rk_ref[...]
    bon_o[0] = _dot01(bon, bd) * v


def _rwkv_prep(ur, mu, wl, bd, w0, a0, kkw, ka, rk, lc):
    b, lt, cp = ur.shape
    tm = _pick_tile(lt, (ROW_TILE_MID, TOK_TILE))
    c = RWKV_WIDTH
    full = lambda shape: pl.BlockSpec(shape, lambda bi, i: (0,) * len(shape))
    tok = pl.BlockSpec((1, tm, c), lambda bi, i: (bi, i, 0))
    tok2 = pl.BlockSpec((1, 2, tm, c), lambda bi, i: (bi, 0, i, 0))
    s1 = jax.ShapeDtypeStruct((b, lt, c), F32)
    s2 = jax.ShapeDtypeStruct((b, 2, lt, c), F32)
    return pl.pallas_call(
        functools.partial(_rwkv_prep_kernel, lc=lc, lt=lt, tm=tm),
        grid=(b, lt // tm),
        in_specs=_halo_specs(tm, cp, lt) + [full((2, cp)), full((LANES, 5 * c)), full((c, c)), full((2, c)),
                                            full((2, c)), full((1, c)), full((1, c)), full((1, c))],
        out_specs=[tok, tok, tok, tok, tok, tok2, tok2, tok2],
        out_shape=[s1, s1, s1, s1, s1, s2, s2, s2],
        compiler_params=_cparams(("parallel", "parallel")),
        name="rwkv_prep",
    )(ur, ur, ur, mu, wl, bd, w0, a0, kkw, ka, rk)


def _mm(a, b):
    return jnp.dot(a, b, precision=HI, preferred_element_type=F32)


_NN = ((1,), (0,))
_NT = ((1,), (1,))
_TN = ((0,), (0,))


def _split2(a):
    hi = a.astype(BF16)
    return hi, (a - hi.astype(F32)).astype(BF16)


def _dotp(a, b, passes, dims=_NN):
    if a.ndim == 3:
        dn = (((dims[0][0] + 1,), (dims[1][0] + 1,)), ((0,), (0,)))
    else:
        dn = (dims, ((), ()))
    dg = lambda p, q: lax.dot_general(p, q, dn, preferred_element_type=F32)
    if passes == 1:
        return dg(a.astype(BF16), b.astype(BF16))
    ah, al = _split2(a)
    bh, bl = _split2(b)
    return dg(ah, bh) + dg(ah, bl) + dg(al, bh)


def _dot01(a, ones):
    ah, al = _split2(a)
    o = ones.astype(BF16)
    return jnp.dot(ah, o, preferred_element_type=F32) + jnp.dot(al, o, preferred_element_type=F32)


P_M = 1
P_INV = 1
P_W = 1
P_Z = 1
P_STATE = 3
P_DFT = 1


def _unit_tri_inv(a_mat, row, col, eye, passes):
    t = a_mat.shape[-1]
    eye_f = eye.astype(F32)
    base = SUBLANES
    same = (row // base) == (col // base)
    n1 = -jnp.where(same, a_mat, 0.0)
    n2 = _dotp(n1, n1, passes)
    n4 = _dotp(n2, n2, passes)
    x = _dotp(_dotp(eye_f + n1, eye_f + n2, passes), eye_f + n4, passes)
    m = base
    while m < t:
        off = jnp.logical_and((row // (2 * m)) == (col // (2 * m)), (row // m) != (col // m))
        x = x - _dotp(x, _dotp(jnp.where(off, a_mat, 0.0), x, passes), passes)
        m *= 2
    return x


def _wkv_kernel(rf_ref, vf_ref, kkf_ref, rb_ref, vb_ref, kkb_ref, lwf_ref, bef_ref, kdf_ref,
                lwb_ref, beb_ref, kdb_ref, yf_ref, yb_ref, h_scr, pq_scr, ry_scr):
    t = WKV_CHUNK
    n = HEAD_DIM
    g = WKV_TILE // WKV_CHUNK
    tt = WKV_TILE
    nh = RWKV_HEADS
    nd = g * nh
    nu = 2 * nd
    orders = (list(range(g)), list(range(g - 1, -1, -1)))

    @pl.when(pl.program_id(1) == 0)
    def _():
        h_scr[...] = jnp.zeros(h_scr.shape, F32)
        pq_scr[...] = jnp.zeros(pq_scr.shape, F32)
        ry_scr[...] = jnp.zeros(ry_scr.shape, F32)

    hm = h_scr[...]
    for p in range(g):
        both = lambda ref: jnp.concatenate([ref[d * nd + p * nh:d * nd + (p + 1) * nh] for d in range(2)], 0)
        ry = both(ry_scr)
        pq = both(pq_scr)
        y = _dotp(ry[:, :, :n], hm, P_STATE) + ry[:, :, n:]
        hm = _dotp(pq[:, :, :n], hm, P_STATE) + pq[:, :, n:]
        for d, y_ref in enumerate((yf_ref, yb_ref)):
            c = orders[d][p]
            y_ref[0, c * t:(c + 1) * t, :] = jnp.concatenate([y[d * nh + h] for h in range(nh)], -1)
    h_scr[...] = hm

    row = lax.broadcasted_iota(jnp.int32, (tt, tt), 0)
    col = lax.broadcasted_iota(jnp.int32, (tt, tt), 1)
    same = (row // t) == (col // t)

    def scaled(d, r_ref, v_ref, kk_ref, lw_ref, be_ref, kd_ref):
        tri = jnp.logical_and(same, (row >= col) if d == 0 else (row <= col))
        sums = jnp.concatenate([jnp.where(tri, 1.0, 0.0), jnp.where(same, 1.0, 0.0)], 0).astype(BF16)
        lw = lw_ref[0, 0]
        l1 = lw.astype(BF16)
        rem = lw - l1.astype(F32)
        l2 = rem.astype(BF16)
        l3 = (rem - l2.astype(F32)).astype(BF16)
        cc = (jnp.dot(sums, l1, preferred_element_type=F32) + jnp.dot(sums, l2, preferred_element_type=F32)
              + jnp.dot(sums, l3, preferred_element_type=F32))
        cum = cc[:tt]
        ctot = cc[tt:]
        e_neg = jnp.exp(-cum)
        e_end = jnp.exp(ctot - cum)
        be = be_ref[0, 0]
        kd = kd_ref[0, 0]

        def units(x, rows=t):
            return jnp.stack([x[orders[d][p] * t:orders[d][p] * t + rows, h * n:(h + 1) * n]
                              for p in range(g) for h in range(nh)], 0)

        return dict(kap=units(kk_ref[0] * jnp.exp(cum - lw)), rt=units(r_ref[0] * jnp.exp(cum)), vh=units(v_ref[0]),
                    bet=units(be * e_neg), kdt=units(kd * e_neg), beh=units(be * e_end), kdh=units(kd * e_end),
                    gend=units(jnp.exp(ctot), 1))

    parts = (scaled(0, rf_ref, vf_ref, kkf_ref, lwf_ref, bef_ref, kdf_ref),
             scaled(1, rb_ref, vb_ref, kkb_ref, lwb_ref, beb_ref, kdb_ref))
    cat = lambda key: jnp.concatenate([parts[0][key], parts[1][key]], 0)
    kap, rt, vh = cat("kap"), cat("rt"), cat("vh")
    r64 = lax.broadcasted_iota(jnp.int32, (t, t), 0)
    c64 = lax.broadcasted_iota(jnp.int32, (t, t), 1)
    eye = r64 == c64
    unit = lax.broadcasted_iota(jnp.int32, (nu, t, t), 0)
    ahead = (r64 - c64)[None] * jnp.where(unit < nd, 1, -1)
    strict = ahead > 0
    incl = ahead >= 0
    m = _dotp(jnp.concatenate([kap, rt], 1), jnp.concatenate([cat("bet"), cat("kdt")], 1), P_M, _NT)
    a_mat = jnp.where(strict, m[:, :t, :t], 0.0)
    b_mat = jnp.where(strict, m[:, :t, t:], 0.0)
    ab_r = jnp.concatenate([jnp.where(incl, m[:, t:, :t], 0.0), jnp.where(incl, m[:, t:, t:], 0.0)], 2)
    tinv = _unit_tri_inv(a_mat, r64, c64, eye, P_INV)
    w = _dotp(tinv, jnp.concatenate([kap, _dotp(b_mat, vh, P_W)], 2), P_W)
    z = jnp.concatenate([-w, jnp.concatenate([jnp.zeros((nu, t, n), F32), vh], 2)], 1)
    ry = _dotp(ab_r, z, P_Z)
    ry_scr[:, :, :n] = ry[:, :, :n] + rt
    ry_scr[:, :, n:] = ry[:, :, n:]
    pq = _dotp(jnp.concatenate([cat("beh"), cat("kdh")], 1), z, P_Z, _TN)
    gd = jnp.where(eye, jnp.broadcast_to(cat("gend"), (nu, n, n)), 0.0)
    pq_scr[:, :, :n] = pq[:, :, :n] + gd
    pq_scr[:, :, n:] = pq[:, :, n:]


def _wkv_scan(r, v, kk, lw, be, kd, lc):
    b, lt, c = r.shape
    tt = WKV_TILE
    nt = lt // tt
    ntc = lc // tt
    tiles = (lambda i: i, lambda i: jnp.where(i < ntc, ntc - 1 - i, nt - 1 - (i - ntc)))
    t_in = lambda d: (lambda i: tiles[d](jnp.minimum(i, nt - 1)))
    t_out = lambda d: (lambda i: tiles[d](jnp.maximum(i - 1, 0)))
    one = lambda d: pl.BlockSpec((1, tt, c), lambda bi, i: (bi, t_in(d)(i), 0))
    two = lambda d: pl.BlockSpec((1, 1, tt, c), lambda bi, i: (bi, d, t_in(d)(i), 0))
    out = lambda d: pl.BlockSpec((1, tt, c), lambda bi, i: (bi, t_out(d)(i), 0))
    nu = 2 * (tt // WKV_CHUNK) * RWKV_HEADS
    ysh = jax.ShapeDtypeStruct((b, lt, c), F32)
    return pl.pallas_call(
        _wkv_kernel,
        grid=(b, nt + 1),
        in_specs=[one(0)] * 3 + [one(1)] * 3 + [two(0)] * 3 + [two(1)] * 3,
        out_specs=[out(0), out(1)],
        out_shape=[ysh, ysh],
        scratch_shapes=[pltpu.VMEM((2 * RWKV_HEADS, HEAD_DIM, HEAD_DIM), F32),
                        pltpu.VMEM((nu, HEAD_DIM, 2 * HEAD_DIM), F32),
                        pltpu.VMEM((nu, WKV_CHUNK, 2 * HEAD_DIM), F32)],
        compiler_params=_cparams(("parallel", "arbitrary")),
        name="wkv_scan",
    )(r, v, kk, r, v, kk, lw, be, kd, lw, be, kd)


def _rwkv_out_kernel(yf_ref, yb_ref, bon_ref, g_ref, bd_ref, gg_ref, gb_ref, o_ref):
    y = yf_ref[0] + yb_ref[0] + bon_ref[0]
    bd = bd_ref[...]
    mu = _dot01(y, bd) * (1.0 / HEAD_DIM)
    yc = y - mu
    var = _dot01(yc * yc, bd) * (1.0 / HEAD_DIM)
    yn = yc * lax.rsqrt(var + RWKV_GN_EPS) * gg_ref[...] + gb_ref[...]
    o_ref[0] = (yn * g_ref[0]).astype(o_ref.dtype)


def _rwkv_out(yf, yb, bon, g, bd, gg, gb):
    b, lt, c = yf.shape
    tm = _pick_tile(lt, (ROW_TILE_WIDE, ROW_TILE_MID, TOK_TILE))
    tok = pl.BlockSpec((1, tm, c), lambda bi, i: (bi, i, 0))
    full = lambda shape: pl.BlockSpec(shape, lambda bi, i: (0,) * len(shape))
    return pl.pallas_call(
        _rwkv_out_kernel,
        grid=(b, lt // tm),
        in_specs=[tok, tok, tok, tok, full((c, c)), full((1, c)), full((1, c))],
        out_specs=tok,
        out_shape=jax.ShapeDtypeStruct((b, lt, c), BF16),
        compiler_params=_cparams(("parallel", "parallel")),
        name="rwkv_out",
    )(yf, yb, bon, g, bd, gg, gb)


def _hy_prep_kernel(u_ref, up_ref, un_ref, w_ref, b_ref, x1_o, x2_o, v_o, *, lc, lt, tm):
    i = pl.program_id(1)
    u = u_ref[0]
    prev, nxt = _prev_next(u, up_ref[0], un_ref[0], i, tm, lc, lt)
    y = prev * w_ref[0:1] + u * w_ref[1:2] + nxt * w_ref[2:3] + b_ref[...]
    c = HY_WIDTH
    x1_o[0] = y[:, :c]
    x2_o[0] = y[:, c:2 * c]
    v_o[0] = y[:, 2 * c:]


def _hy_prep(uh, w, bias, lc):
    b, lt, cin = uh.shape
    tm = _pick_tile(lt, (ROW_TILE_WIDE, ROW_TILE_MID, TOK_TILE))
    c = HY_WIDTH
    full = lambda shape: pl.BlockSpec(shape, lambda bi, i: (0,) * len(shape))
    tok = pl.BlockSpec((1, tm, c), lambda bi, i: (bi, i, 0))
    s1 = jax.ShapeDtypeStruct((b, lt, c), F32)
    return pl.pallas_call(
        functools.partial(_hy_prep_kernel, lc=lc, lt=lt, tm=tm),
        grid=(b, lt // tm),
        in_specs=_halo_specs(tm, cin, lt) + [full((3, cin)), full((1, cin))],
        out_specs=[tok, tok, tok],
        out_shape=[s1, s1, s1],
        compiler_params=_cparams(("parallel", "parallel")),
        name="hy_prep",
    )(uh, uh, uh, w, bias)


def _hy_filter_kernel(z_ref, w1_ref, b1_ref, f1_ref, w2_ref, b2_ref, f2_ref, w3_ref, dec_ref,
                      fw_o, bw_o, nrm_o, *, tl):
    i = pl.program_id(0)
    z = z_ref[...]
    h = jnp.sin(f1_ref[...] * (_dotp(z, w1_ref[...], 3) + b1_ref[...]))
    h = jnp.sin(f2_ref[...] * (_dotp(h, w2_ref[...], 3) + b2_ref[...]))
    h = _dotp(h, w3_ref[...], 3) * jnp.exp(-z[:, 0:1] * dec_ref[...])
    c = HY_WIDTH
    fw = jnp.concatenate([h[:, 0:c], h[:, 2 * c:3 * c]], 1)
    bw = jnp.concatenate([h[:, c:2 * c], h[:, 3 * c:4 * c]], 1)
    rid = i * tl + lax.broadcasted_iota(jnp.int32, (tl, 1), 0)
    bw = jnp.where(rid == 0, 0.0, bw)
    fw_o[...] = fw
    bw_o[...] = bw

    @pl.when(i == 0)
    def _():
        nrm_o[...] = jnp.zeros(nrm_o.shape, F32)

    nrm_o[...] += jnp.sum(jnp.abs(fw) + jnp.abs(bw), 0, keepdims=True)


def _hy_filter(feat, w1p, b1, f1, w2, b2, f2, w3, dec):
    l, fe = feat.shape
    tl = _pick_tile(l, (512, 256))
    c2 = HY_ORDER * HY_WIDTH
    full = lambda shape: pl.BlockSpec(shape, lambda i: (0,) * len(shape))
    return pl.pallas_call(
        functools.partial(_hy_filter_kernel, tl=tl),
        grid=(l // tl,),
        in_specs=[pl.BlockSpec((tl, fe), lambda i: (i, 0)), full(w1p.shape), full(b1.shape), full(f1.shape),
                  full(w2.shape), full(b2.shape), full(f2.shape), full(w3.shape), full(dec.shape)],
        out_specs=[pl.BlockSpec((tl, c2), lambda i: (i, 0)), pl.BlockSpec((tl, c2), lambda i: (i, 0)),
                   pl.BlockSpec((1, c2), lambda i: (0, 0))],
        out_shape=[jax.ShapeDtypeStruct((l, c2), F32), jax.ShapeDtypeStruct((l, c2), F32),
                   jax.ShapeDtypeStruct((1, c2), F32)],
        compiler_params=_cparams(("arbitrary",)),
        name="hy_filter",
    )(feat, w1p, b1, f1, w2, b2, f2, w3, dec)


def _dft_cols_kernel(f_ref, xa_ref, xb_ref, o_ref, *, n1, pair):
    f = f_ref[...]
    pa = _dotp(f, xa_ref[0], P_DFT)
    pb = _dotp(f, xb_ref[0], P_DFT)
    if pair:
        o_ref[0] = pa[:n1].astype(o_ref.dtype)
        o_ref[1] = pa[n1:].astype(o_ref.dtype)
        o_ref[2] = pb[:n1].astype(o_ref.dtype)
        o_ref[3] = pb[n1:].astype(o_ref.dtype)
    else:
        o_ref[0] = (pa[:n1] - pb[n1:]).astype(o_ref.dtype)
        o_ref[1] = (pb[:n1] + pa[n1:]).astype(o_ref.dtype)


def _dft_cols(fstack, xa, ia, xb, ib, pair):
    n1 = fstack.shape[0] // 2
    _, nh, cols = xa.shape
    tc = _pick_tile(cols, (4096, 2048, 1024, 512, 256, 128))
    no = 4 if pair else 2
    return pl.pallas_call(
        functools.partial(_dft_cols_kernel, n1=n1, pair=pair),
        grid=(cols // tc,),
        in_specs=[pl.BlockSpec(fstack.shape, lambda j: (0, 0)),
                  pl.BlockSpec((1, nh, tc), lambda j: (ia, 0, j)),
                  pl.BlockSpec((1, nh, tc), lambda j: (ib, 0, j))],
        out_specs=pl.BlockSpec((no, n1, tc), lambda j: (0, 0, j)),
        out_shape=jax.ShapeDtypeStruct((no, n1, cols), BF16),
        compiler_params=_cparams(("parallel",)),
        name="dft_cols",
    )(fstack, xa, xb)


def _cplx_left(gs, zr, zi, n):
    c = zr.shape[1]
    p = _dotp(gs, jnp.concatenate([zr, zi], 1), P_DFT)
    return p[:n, :c] - p[n:, c:], p[:n, c:] + p[n:, :c]


def _spec_kernel(a_ref, g_ref, nrm_ref, o_ref, *, n_total, kp):
    n2 = DFT_N2
    s = 1.0 / (nrm_ref[...] * n_total)
    for q in range(kp):
        gs = jnp.concatenate([g_ref[q, 0], g_ref[q, 1]], 0)
        fr, fi = _cplx_left(gs, a_ref[0, q], a_ref[1, q], n2)
        br, bi = _cplx_left(gs, a_ref[2, q], a_ref[3, q], n2)
        o_ref[0, q] = ((fr + br) * s).astype(o_ref.dtype)
        o_ref[1, q] = ((fi - bi) * s).astype(o_ref.dtype)


def _planes_per_step(n1):
    return next(k for k in (4, 2, 1) if n1 % k == 0)


def _spec(a4, g, nrm, n_total):
    _, n1, n2, c2 = a4.shape
    kp = _planes_per_step(n1)
    return pl.pallas_call(
        functools.partial(_spec_kernel, n_total=float(n_total), kp=kp),
        grid=(n1 // kp,),
        in_specs=[pl.BlockSpec((4, kp, n2, c2), lambda k: (0, k, 0, 0)),
                  pl.BlockSpec((kp, 2, n2, n2), lambda k: (k, 0, 0, 0)),
                  pl.BlockSpec((1, c2), lambda k: (0, 0))],
        out_specs=pl.BlockSpec((2, kp, n2, c2), lambda k: (0, k, 0, 0)),
        out_shape=jax.ShapeDtypeStruct((2, n1, n2, c2), BF16),
        compiler_params=_cparams(("parallel",)),
        name="hy_spec",
    )(a4, g, nrm)


def _conv_mid_kernel(a_ref, g_ref, k_ref, o_ref, *, kp):
    n2 = DFT_N2
    for q in range(kp):
        gs = jnp.concatenate([g_ref[q, 0], g_ref[q, 1]], 0)
        xr, xi = _cplx_left(gs, a_ref[0, q], a_ref[1, q], n2)
        c = xr.shape[1]
        kr = k_ref[0, q].astype(F32)
        ki = k_ref[1, q].astype(F32)
        zr = xr * kr - xi * ki
        zi = xr * ki + xi * kr
        zst = jnp.concatenate([jnp.concatenate([zr, zi], 1), jnp.concatenate([zi, -zr], 1)], 0)
        y = _dotp(gs, zst, P_DFT, _TN)
        o_ref[0, q] = y[:, :c].astype(o_ref.dtype)
        o_ref[1, q] = y[:, c:].astype(o_ref.dtype)


def _conv_mid(a, g, kspec, order):
    _, n1, n2, c = a.shape
    kp = _planes_per_step(n1)
    return pl.pallas_call(
        functools.partial(_conv_mid_kernel, kp=kp),
        grid=(n1 // kp,),
        in_specs=[pl.BlockSpec((2, kp, n2, c), lambda k: (0, k, 0, 0)),
                  pl.BlockSpec((kp, 2, n2, n2), lambda k: (k, 0, 0, 0)),
                  pl.BlockSpec((2, kp, n2, c), lambda k: (0, k, 0, order))],
        out_specs=pl.BlockSpec((2, kp, n2, c), lambda k: (0, k, 0, 0)),
        out_shape=jax.ShapeDtypeStruct((2, n1, n2, c), BF16),
        compiler_params=_cparams(("parallel",)),
        name="hy_conv_mid",
    )(a, g, kspec)


def _idft_cols_kernel(c_ref, b_ref, g0_ref, g1_ref, x0_ref, x1_ref, bias_ref, o_ref, *, nh):
    cs = c_ref[...]
    pr = _dotp(cs, b_ref[0], P_DFT)
    pi = _dotp(cs, b_ref[1], P_DFT)
    yr = pr[:nh] - pi[nh:]
    yi = pi[:nh] + pr[nh:]
    bias = bias_ref[...]
    o_ref[0] = g0_ref[0] * (yr + x0_ref[0] * bias)
    o_ref[1] = g1_ref[0] * (yi + x1_ref[0] * bias)


def _idft_cols(cstack, bv, gate, xin, bias_cols):
    nh2, n1 = cstack.shape
    nh = nh2 // 2
    cols = bv.shape[-1]
    tc = _pick_tile(cols, (4096, 2048, 1024, 512, 256, 128))
    row = lambda bi: pl.BlockSpec((1, nh, tc), lambda j: (bi, 0, j))
    return pl.pallas_call(
        functools.partial(_idft_cols_kernel, nh=nh),
        grid=(cols // tc,),
        in_specs=[pl.BlockSpec((nh2, n1), lambda j: (0, 0)),
                  pl.BlockSpec((2, n1, tc), lambda j: (0, 0, j)),
                  row(0), row(1), row(0), row(1),
                  pl.BlockSpec((1, tc), lambda j: (0, j))],
        out_specs=pl.BlockSpec((2, nh, tc), lambda j: (0, 0, j)),
        out_shape=jax.ShapeDtypeStruct((2, nh, cols), F32),
        compiler_params=_cparams(("parallel",)),
        name="idft_cols",
    )(cstack, bv, gate, gate, xin, xin, bias_cols)


def _hy_ctx_kernel(x1_ref, x2_ref, v_ref, fw_ref, bw_ref, nrm_ref, bias_ref, ff_ref, ci_ref, o_ref, *, lc):
    n = 2 * lc
    c = HY_WIDTH
    ff = ff_ref[...]
    ci = ci_ref[...]
    pf = _mm(ff, fw_ref[...])
    pb = _mm(ff, bw_ref[...])
    s = 1.0 / (nrm_ref[...] * float(n))
    kr = (pf[:n] + pb[:n]) * s
    ki = (pf[n:] - pb[n:]) * s

    def conv(z0, z1, o):
        xr, xi = _cplx_left(ff, z0, z1, n)
        krr = kr[:, o * c:(o + 1) * c]
        kii = ki[:, o * c:(o + 1) * c]
        return _cplx_left(ci, xr * krr - xi * kii, xr * kii + xi * krr, lc)

    v0 = v_ref[0]
    v1 = v_ref[1]
    y0, y1 = conv(v0, v1, 0)
    z0 = x1_ref[0] * (y0 + v0 * bias_ref[0:1])
    z1 = x1_ref[1] * (y1 + v1 * bias_ref[0:1])
    y0, y1 = conv(z0, z1, 1)
    o_ref[0] = x2_ref[0] * (y0 + z0 * bias_ref[1:2])
    o_ref[1] = x2_ref[1] * (y1 + z1 * bias_ref[1:2])


def _hy_ctx(x1, x2, v, fw, bw, nrm, bias, ff, ci):
    b, lc, c = v.shape
    vm = pl.BlockSpec(memory_space=pltpu.VMEM)
    return pl.pallas_call(
        functools.partial(_hy_ctx_kernel, lc=lc),
        in_specs=[vm] * 9,
        out_specs=vm,
        out_shape=jax.ShapeDtypeStruct((b, lc, c), F32),
        compiler_params=pltpu.CompilerParams(vmem_limit_bytes=VMEM_LIMIT),
        name="hy_ctx",
    )(x1, x2, v, fw, bw, nrm, bias, ff, ci)


def _outproj_kernel(attc_ref, attl_ref, rw_ref, hyc_ref, hyl_ref, x_ref, mod_ref, w_ref, lg_ref, lb_ref, o_ref,
                    *, lc, tm):
    row0 = pl.program_id(1) * tm
    a0 = ATT_WIDTH
    a1 = ATT_WIDTH + RWKV_WIDTH
    is_ctx = row0 < lc
    att = jnp.where(is_ctx, attc_ref[0], attl_ref[0])
    hy = jnp.where(is_ctx, hyc_ref[0], hyl_ref[0])
    o = jnp.dot(att, w_ref[:a0], preferred_element_type=F32)
    o += jnp.dot(rw_ref[0], w_ref[a0:a1], preferred_element_type=F32)
    o += jnp.dot(hy.astype(BF16), w_ref[a1:], preferred_element_type=F32)
    g = _sel_mod(mod_ref, 2, row0, tm, lc)
    y = ALPHA * x_ref[0] + g * o
    o_ref[0] = _layer_norm(y) * lg_ref[...] + lb_ref[...]


def _outproj(att_c, att_l, rw, hy_c, hy_l, xx, mod, w, lg, lb, lc):
    b, lt, d = xx.shape
    tm = TOK_TILE
    nct = lc // tm
    tok = lambda c: pl.BlockSpec((1, tm, c), lambda bi, i: (bi, i, 0))
    ctx = lambda c: pl.BlockSpec((1, tm, c), lambda bi, i: (bi, jnp.minimum(i, nct - 1), 0))
    lat = lambda c: pl.BlockSpec((1, tm, c), lambda bi, i: (bi, jnp.maximum(i - nct, 0), 0))
    full = lambda shape: pl.BlockSpec(shape, lambda bi, i: (0,) * len(shape))
    return pl.pallas_call(
        functools.partial(_outproj_kernel, lc=lc, tm=tm),
        grid=(b, lt // tm),
        in_specs=[ctx(ATT_WIDTH), lat(ATT_WIDTH), tok(RWKV_WIDTH), ctx(HY_WIDTH), lat(HY_WIDTH), tok(d),
                  pl.BlockSpec((1, 2, 6, d), lambda bi, i: (bi, 0, 0, 0)),
                  full(w.shape), full((1, d)), full((1, d))],
        out_specs=tok(d),
        out_shape=jax.ShapeDtypeStruct((b, lt, d), F32),
        compiler_params=_cparams(("parallel", "parallel")),
        name="outproj",
    )(att_c, att_l, rw, hy_c, hy_l, xx, mod, w, lg, lb)


def _ffn_kernel(x_ref, mod_ref, w1_ref, w3_ref, w2_ref, lg_ref, lb_ref, o_ref, h_scr, acc_scr, *, lc, tm, nf):
    row0 = pl.program_id(1) * tm
    f = pl.program_id(2)

    @pl.when(f == 0)
    def _():
        sh = _sel_mod(mod_ref, 3, row0, tm, lc)
        sc = _sel_mod(mod_ref, 4, row0, tm, lc)
        h_scr[...] = (_layer_norm(x_ref[0]) * (1.0 + sc) + sh).astype(BF16)
        acc_scr[...] = jnp.zeros(acc_scr.shape, F32)

    h = h_scr[...]
    a = jnp.dot(h, w1_ref[...], preferred_element_type=F32)
    g = jnp.dot(h, w3_ref[...], preferred_element_type=F32)
    acc_scr[...] += jnp.dot((_silu(a) * g).astype(BF16), w2_ref[...], preferred_element_type=F32)

    @pl.when(f == nf - 1)
    def _():
        gate = _sel_mod(mod_ref, 5, row0, tm, lc)
        y = ALPHA * x_ref[0] + gate * acc_scr[...]
        o_ref[0] = _layer_norm(y) * lg_ref[...] + lb_ref[...]


def _ffn(xx, mod, w1, w3, w2, lg, lb, lc):
    b, lt, d = xx.shape
    ff = w1.shape[1]
    tm = _pick_tile(lt, (1280, 768, 512, 256))
    tf = _pick_tile(ff, (256, 128))
    nf = ff // tf
    return pl.pallas_call(
        functools.partial(_ffn_kernel, lc=lc, tm=tm, nf=nf),
        grid=(b, lt // tm, nf),
        in_specs=[pl.BlockSpec((1, tm, d), lambda bi, i, f: (bi, i, 0)),
                  pl.BlockSpec((1, 2, 6, d), lambda bi, i, f: (bi, 0, 0, 0)),
                  pl.BlockSpec((d, tf), lambda bi, i, f: (0, f)),
                  pl.BlockSpec((d, tf), lambda bi, i, f: (0, f)),
                  pl.BlockSpec((tf, d), lambda bi, i, f: (f, 0)),
                  pl.BlockSpec((1, d), lambda bi, i, f: (0, 0)),
                  pl.BlockSpec((1, d), lambda bi, i, f: (0, 0))],
        out_specs=pl.BlockSpec((1, tm, d), lambda bi, i, f: (bi, i, 0)),
        out_shape=jax.ShapeDtypeStruct((b, lt, d), F32),
        scratch_shapes=[pltpu.VMEM((tm, d), BF16), pltpu.VMEM((tm, d), F32)],
        compiler_params=_cparams(("parallel", "parallel", "arbitrary")),
        name="ffn",
    )(xx, mod, w1, w3, w2, lg, lb)


def _route(x_ref, mod_ref, wr_ref, row0, tm, lc):
    sh = _sel_mod(mod_ref, 3, row0, tm, lc)
    sc = _sel_mod(mod_ref, 4, row0, tm, lc)
    h = _layer_norm(x_ref[0]) * (1.0 + sc) + sh
    logits = _dotp(h, wr_ref[...], 3)
    lane = lax.broadcasted_iota(jnp.int32, logits.shape, 1)
    neg = jnp.float32(-jnp.inf)
    lg = jnp.where(lane < N_EXPERTS, logits, neg)
    m1 = jnp.max(lg, -1, keepdims=True)
    i1 = jnp.min(jnp.where(lg == m1, lane, LANES), -1, keepdims=True)
    lg2 = jnp.where(lane == i1, neg, lg)
    m2 = jnp.max(lg2, -1, keepdims=True)
    i2 = jnp.min(jnp.where(lg2 == m2, lane, LANES), -1, keepdims=True)
    e2 = jnp.exp(m2 - m1)
    return h, lane, i1, i2, 1.0 / (1.0 + e2), e2 / (1.0 + e2)


def _moe_sparse_kernel(x_ref, mod_ref, wr_ref, w1_ref, w3_ref, w2_ref, lg_ref, lb_ref, o_ref,
                       h_scr, acc_scr, gate_scr, posc_scr, posr_scr, xs_scr, ye_scr, nblk_scr, *, lc, tm, nf, ns, blk):
    row0 = pl.program_id(1) * tm
    s = pl.program_id(2)
    e = s // nf
    f = s % nf

    @pl.when(s == 0)
    def _():
        h, lane, i1, i2, g1, g2 = _route(x_ref, mod_ref, wr_ref, row0, tm, lc)
        h_scr[...] = h.astype(BF16)
        acc_scr[...] = jnp.zeros(acc_scr.shape, F32)
        routed = jnp.where(jnp.logical_or(lane == i1, lane == i2), 1.0, 0.0)
        r = lax.broadcasted_iota(jnp.int32, (tm, tm), 0)
        c = lax.broadcasted_iota(jnp.int32, (tm, tm), 1)
        before = jnp.where(c < r, 1.0, 0.0).astype(BF16)
        rank_c = jnp.dot(before, routed.astype(BF16), preferred_element_type=F32)
        gate_scr[...] = jnp.where(lane == i1, g1, 0.0) + jnp.where(lane == i2, g2, 0.0)
        posc_scr[...] = jnp.where(routed > 0.0, rank_c, -1.0)
        routed_t = routed.T[:2 * SUBLANES]
        after = jnp.where(r < c, 1.0, 0.0).astype(BF16)
        rank_r = jnp.dot(routed_t.astype(BF16), after, preferred_element_type=F32)
        posr_scr[...] = jnp.where(routed_t > 0.0, rank_r, -1.0)
        counts = jnp.sum(routed, 0, keepdims=True)
        lane1 = lax.broadcasted_iota(jnp.int32, counts.shape, 1)
        for ex in range(N_EXPERTS):
            n_rows = jnp.sum(jnp.where(lane1 == ex, counts, 0.0)).astype(jnp.int32)
            nblk_scr[ex] = (n_rows + (blk - 1)) // blk

    nblk = nblk_scr[e]
    mv = MOE_MOVE_BLOCK
    nmv = (nblk * blk + (mv - 1)) // mv
    rows = lambda i: pl.ds(pl.multiple_of(i * mv, mv), mv)

    @pl.when(f == 0)
    def _():
        def gather(i, carry):
            slot = (lax.broadcasted_iota(jnp.int32, (mv, tm), 0) + i * mv).astype(F32)
            take = jnp.where(slot == posr_scr[pl.ds(e, 1), :], 1.0, 0.0).astype(BF16)
            xs_scr[rows(i), :] = jnp.dot(take, h_scr[...], preferred_element_type=F32).astype(BF16)
            ye_scr[rows(i), :] = jnp.zeros((mv, ye_scr.shape[1]), F32)
            return carry

        lax.fori_loop(0, nmv, gather, 0)

    def expert(m):
        xs = xs_scr[:m]
        a = jnp.dot(xs, w1_ref[0], preferred_element_type=F32)
        g = jnp.dot(xs, w3_ref[0], preferred_element_type=F32)
        ye_scr[:m] += jnp.dot((_silu(a) * g).astype(BF16), w2_ref[0], preferred_element_type=F32)

    for k in range(tm // blk):
        pl.when(nblk == k + 1)(functools.partial(expert, (k + 1) * blk))

    @pl.when(f == nf - 1)
    def _():
        lane = lax.broadcasted_iota(jnp.int32, (tm, LANES), 1)
        column = lambda ref: jnp.sum(jnp.where(lane == e, ref[...], 0.0), -1, keepdims=True)
        pos = column(posc_scr)
        gate = column(gate_scr)

        def scatter(i, carry):
            slot = (lax.broadcasted_iota(jnp.int32, (tm, mv), 1) + i * mv).astype(F32)
            put = jnp.where(slot == pos, 1.0, 0.0).astype(BF16)
            acc_scr[...] += gate * jnp.dot(put, ye_scr[rows(i), :].astype(BF16), preferred_element_type=F32)
            return carry

        lax.fori_loop(0, nmv, scatter, 0)

    @pl.when(s == ns - 1)
    def _():
        gate = _sel_mod(mod_ref, 5, row0, tm, lc)
        y = ALPHA * x_ref[0] + gate * acc_scr[...]
        o_ref[0] = _layer_norm(y) * lg_ref[...] + lb_ref[...]


def _moe_sparse(xx, mod, wr, w1, w3, w2, lg, lb, lc):
    b, lt, d = xx.shape
    ne, _, ff = w1.shape
    tm = _pick_tile(lt, (1280, 768, 512, 256))
    tf = _pick_tile(ff, (256, 128))
    nf = ff // tf
    ns = ne * nf
    return pl.pallas_call(
        functools.partial(_moe_sparse_kernel, lc=lc, tm=tm, nf=nf, ns=ns, blk=MOE_ROW_BLOCK),
        grid=(b, lt // tm, ns),
        in_specs=[pl.BlockSpec((1, tm, d), lambda bi, i, s: (bi, i, 0)),
                  pl.BlockSpec((1, 2, 6, d), lambda bi, i, s: (bi, 0, 0, 0)),
                  pl.BlockSpec((d, LANES), lambda bi, i, s: (0, 0)),
                  pl.BlockSpec((1, d, tf), lambda bi, i, s: (s // nf, 0, s % nf)),
                  pl.BlockSpec((1, d, tf), lambda bi, i, s: (s // nf, 0, s % nf)),
                  pl.BlockSpec((1, tf, d), lambda bi, i, s: (s // nf, s % nf, 0)),
                  pl.BlockSpec((1, d), lambda bi, i, s: (0, 0)),
                  pl.BlockSpec((1, d), lambda bi, i, s: (0, 0))],
        out_specs=pl.BlockSpec((1, tm, d), lambda bi, i, s: (bi, i, 0)),
        out_shape=jax.ShapeDtypeStruct((b, lt, d), F32),
        scratch_shapes=[pltpu.VMEM((tm, d), BF16), pltpu.VMEM((tm, d), F32),
                        pltpu.VMEM((tm, LANES), F32), pltpu.VMEM((tm, LANES), F32),
                        pltpu.VMEM((2 * SUBLANES, tm), F32),
                        pltpu.VMEM((tm, d), BF16), pltpu.VMEM((tm, d), F32),
                        pltpu.SMEM((ne,), jnp.int32)],
        compiler_params=_cparams(("parallel", "parallel", "arbitrary"), VMEM_LIMIT_MOE),
        name="moe_sparse",
    )(xx, mod, wr, w1, w3, w2, lg, lb)


def _rope_tables(l, lc):
    rows = l // GRID_W
    row = jnp.repeat(jnp.arange(rows, dtype=F32), GRID_W)
    col = jnp.tile(jnp.arange(GRID_W, dtype=F32), rows)
    n_freq = HEAD_DIM // 4
    inv_freq = ROPE_THETA ** (-jnp.arange(n_freq, dtype=F32) / n_freq)
    ang = jnp.concatenate([row[:, None] * inv_freq, col[:, None] * inv_freq], -1)
    cos, sin = jnp.cos(ang), jnp.sin(ang)
    cos2 = jnp.concatenate([jnp.ones((lc, LANES), F32), jnp.concatenate([cos, cos, cos, cos], -1)], 0)
    sin2 = jnp.concatenate([jnp.zeros((lc, LANES), F32), jnp.concatenate([-sin, sin, -sin, sin], -1)], 0)
    return cos2, sin2


def _hy_features(l):
    bands = (HY_EMB - 1) // 2
    t = jnp.linspace(0.0, 1.0, l, dtype=F32)[:, None]
    f = jnp.linspace(1e-4, bands - 1, bands, dtype=F32)[None, :]
    wt = 2.0 * math.pi * jnp.arange(l, dtype=F32)[:, None] / l
    z = jnp.concatenate([t, jnp.cos(f * wt), -jnp.sin(f * wt)], -1)
    return jnp.pad(z, ((0, 0), (0, LANES - HY_EMB)))


def _angle(idx, n):
    return (2.0 * math.pi / n) * (idx % n).astype(F32)


def _dft_tables(n1):
    nh = n1 // 2
    n2 = DFT_N2
    n = n1 * n2
    k1 = jnp.arange(n1, dtype=jnp.int32)
    a1 = _angle(k1[:, None] * jnp.arange(nh, dtype=jnp.int32)[None, :], n1)
    fstack = jnp.concatenate([jnp.cos(a1), -jnp.sin(a1)], 0)
    cstack = jnp.concatenate([jnp.cos(a1.T), jnp.sin(a1.T)], 0)
    k2 = jnp.arange(n2, dtype=jnp.int32)
    at = _angle(k1[:, None] * k2[None, :], n)
    tr, ti = jnp.cos(at), -jnp.sin(at)
    a2 = _angle(k2[:, None] * k2[None, :], n2)
    fr, fi = jnp.cos(a2), -jnp.sin(a2)
    g = jnp.stack([tr[:, None, :] * fr[None] - ti[:, None, :] * fi[None],
                   tr[:, None, :] * fi[None] + ti[:, None, :] * fr[None]], 1)
    return fstack, cstack, g


def _dense_dft_tables(lc):
    n = 2 * lc
    a = _angle(jnp.arange(n, dtype=jnp.int32)[:, None] * jnp.arange(lc, dtype=jnp.int32)[None, :], n)
    ff = jnp.concatenate([jnp.cos(a), -jnp.sin(a)], 0)
    ci = jnp.concatenate([jnp.cos(a.T), jnp.sin(a.T)], 0)
    return ff, ci


def kernel(x, c, ctx, c_ctx, ada_w, ada_b, w_in, w_out, q_gain, k_gain, rwkv_mu, rwkv_w0, rwkv_wB, rwkv_a0, rwkv_aB, rwkv_gB, rwkv_kk, rwkv_ka, rwkv_rk, rwkv_gn_g, rwkv_gn_b, hy_short_w, hy_short_b, hy_w1, hy_b1, hy_freq1, hy_w2, hy_b2, hy_freq2, hy_w3, hy_decay, hy_bias, ln1_g, ln1_b, ln2_g, ln2_b, ffn_w1, ffn_w3, ffn_w2, moe_router, moe_w1, moe_w3, moe_w2):
    b, l, d = x.shape
    lc = ctx.shape[1]
    lt = lc + l
    depth = ada_w.shape[0]
    assert b == 2, "the long convolution packs the two batch rows as one complex signal"
    assert d == D_MODEL and lc % TOK_TILE == 0 and l % TOK_TILE == 0 and (2 * l) % (2 * DFT_N2) == 0
    cw = RWKV_WIDTH

    xx = jnp.concatenate([ctx, x], 1)
    cond8 = jnp.zeros((SUBLANES, d), F32).at[:b].set(c).at[b].set(c_ctx)
    mod_all = _ada_mod(cond8, ada_w, ada_b)

    cos64, sin64 = _rope_tables(l, lc)
    n1 = 2 * l // DFT_N2
    nh = n1 // 2
    cols = DFT_N2 * HY_WIDTH
    fstack, cstack, g_tab = (t.astype(BF16) for t in _dft_tables(n1))
    ff_c, ci_c = _dense_dft_tables(lc)
    feat_l = _hy_features(l)
    feat_c = _hy_features(lc)
    blk = jnp.arange(cw) // HEAD_DIM
    bd = (blk[:, None] == blk[None, :]).astype(F32)
    ch = jnp.arange(ATT_WIDTH)
    bd_att = (ch[:, None] // HEAD_DIM == ch[None, :] // HEAD_DIM).astype(BF16)
    swap_att = (ch[:, None] == (ch[None, :] + HALF_HD) % HEAD_DIM + (ch[None, :] // HEAD_DIM) * HEAD_DIM
                ).astype(BF16)
    perm64 = jnp.concatenate([jnp.arange(0, HEAD_DIM, 2), jnp.arange(1, HEAD_DIM, 2)])
    perm_att = jnp.concatenate([h * HEAD_DIM + perm64 for h in range(ATT_HEADS + ATT_KV_HEADS)]
                               + [jnp.arange(ATT_WIDTH + ATT_KV_WIDTH, IN_ATT)])
    tq = TOK_TILE
    qn = next(n for n in (4, 2, 1) if (l // tq) % n == 0)
    tk = _pick_tile(lt, (3328, 1280, 1024, 768, 512, 256))

    for li in range(depth):
        ml = mod_all[li]
        mod = jnp.stack([jnp.broadcast_to(ml[b].reshape(1, 6, d), (b, 6, d)), ml[:b].reshape(b, 6, d)], 1)
        wi = w_in[li]
        w_pad = jnp.concatenate([wi[:, :IN_ATT][:, perm_att], wi[:, IN_ATT:IN_ATT + IN_RWKV],
                                 jnp.zeros((d, IN_RWKV_PAD - IN_RWKV), F32), wi[:, IN_ATT + IN_RWKV:]],
                                1).astype(BF16)
        ua, ur, uh = _inproj(xx, mod, w_pad, lc)

        two = lambda gain: jnp.tile(gain[perm64], 2)[None]
        qt, kx, vt = _attn_prep(ua, cos64, sin64, two(q_gain[li]), two(k_gain[li]), bd_att, swap_att, lc)
        att_c = _flash(qt, kx, vt, tq, TOK_TILE, 1, l // tq, lc // tq, lc // TOK_TILE)
        s_bound = (HEAD_DIM ** 0.5) * LOG2E * jnp.max(jnp.abs(q_gain[li])) * jnp.max(jnp.abs(k_gain[li]))
        lat_args = (qt, kx, vt, tq, tk, qn, 0, l // (qn * tq), lt // tk)
        att_l = lax.cond(s_bound <= MAX_UNSHIFTED_SCORE,
                         lambda: _flash(*lat_args, bounded=True), lambda: _flash(*lat_args, bounded=False))

        wl = jnp.zeros((LANES, 5 * cw), F32)
        wl = wl.at[0:W_LORA, 0:cw].set(rwkv_wB[li, 0]).at[W_LORA:2 * W_LORA, cw:2 * cw].set(rwkv_wB[li, 1])
        o_a = 2 * W_LORA
        wl = wl.at[o_a:o_a + A_LORA, 2 * cw:3 * cw].set(rwkv_aB[li, 0])
        wl = wl.at[o_a + A_LORA:o_a + 2 * A_LORA, 3 * cw:4 * cw].set(rwkv_aB[li, 1])
        o_g = o_a + 2 * A_LORA
        wl = wl.at[o_g:o_g + G_LORA, 4 * cw:5 * cw].set(rwkv_gB[li])
        mu = jnp.pad(rwkv_mu[li], ((0, 0), (0, IN_RWKV_PAD - IN_RWKV)))
        r_, v_, kk_, g_, bon_, lw_, be_, kd_ = _rwkv_prep(
            ur, mu, wl, bd, rwkv_w0[li], rwkv_a0[li], rwkv_kk[li][None], rwkv_ka[li][None],
            rwkv_rk[li].reshape(1, cw), lc)
        yf, yb = _wkv_scan(r_, v_, kk_, lw_, be_, kd_, lc)
        rw = _rwkv_out(yf, yb, bon_, g_, bd, rwkv_gn_g[li][None], rwkv_gn_b[li][None])

        x1, x2, vv = _hy_prep(uh, hy_short_w[li], hy_short_b[li][None], lc)
        w1p = jnp.pad(hy_w1[li], ((0, LANES - HY_EMB), (0, 0)))
        fargs = (w1p, hy_b1[li][None], hy_freq1[li][None], hy_w2[li], hy_b2[li][None], hy_freq2[li][None],
                 hy_w3[li], hy_decay[li][None])
        fw, bw, nrm = _hy_filter(feat_l, *fargs)
        c2 = HY_ORDER * HY_WIDTH
        a4 = _dft_cols(fstack, fw.reshape(1, nh, DFT_N2 * c2), 0, bw.reshape(1, nh, DFT_N2 * c2), 0, True)
        kspec = _spec(a4.reshape(4, n1, DFT_N2, c2), g_tab, nrm, n1 * DFT_N2)
        lat = lambda t: t[:, lc:].reshape(b, nh, cols)
        x1l, x2l, zin = lat(x1), lat(x2), lat(vv)
        for o, gate in enumerate((x1l, x2l)):
            a = _dft_cols(fstack, zin, 0, zin, 1, False)
            bv = _conv_mid(a.reshape(2, n1, DFT_N2, HY_WIDTH), g_tab, kspec, o)
            bias_cols = jnp.tile(hy_bias[li, o], DFT_N2)[None]
            zin = _idft_cols(cstack, bv.reshape(2, n1, cols), gate, zin, bias_cols)
        hy_l = zin.reshape(b, l, HY_WIDTH)
        fw_c, bw_c, nrm_c = _hy_filter(feat_c, *fargs)
        hy_c = _hy_ctx(x1[:, :lc], x2[:, :lc], vv[:, :lc], fw_c, bw_c, nrm_c, hy_bias[li], ff_c, ci_c)

        xx = _outproj(att_c, att_l, rw, hy_c, hy_l, xx, mod, w_out[li].astype(BF16),
                      ln1_g[li][None], ln1_b[li][None], lc)

        j = li // 2
        if li % 2 == 0:
            xx = _ffn(xx, mod, ffn_w1[j].astype(BF16), ffn_w3[j].astype(BF16), ffn_w2[j].astype(BF16),
                      ln2_g[li][None], ln2_b[li][None], lc)
        else:
            wr = jnp.pad(moe_router[j], ((0, 0), (0, LANES - N_EXPERTS)))
            xx = _moe_sparse(xx, mod, wr, moe_w1[j].astype(BF16), moe_w3[j].astype(BF16), moe_w2[j].astype(BF16),
                             ln2_g[li][None], ln2_b[li][None], lc)
    return xx[:, lc:]
```

```python
import functools
import math

import jax
import jax.numpy as jnp
from jax import lax
from jax.experimental import pallas as pl
from jax.experimental.pallas import tpu as pltpu

F32 = jnp.float32
BF16 = jnp.bfloat16
HI = lax.Precision.HIGHEST

D_MODEL = 1024
DEPTH = 2
GRID_W = 64
HEAD_DIM = 64
HALF_HD = HEAD_DIM // 2
ATT_WIDTH = 512
RWKV_WIDTH = 256
HY_WIDTH = 256
ATT_HEADS = 8
ATT_KV_HEADS = 2
ATT_REP = 4
ATT_KV_WIDTH = 128
ROPE_THETA = 10000.0
QK_EPS = 1e-6
RWKV_HEADS = 4
W_LORA = 16
A_LORA = 16
G_LORA = 32
RWKV_GN_EPS = 64e-5
HY_ORDER = 2
HY_EMB = 33
HY_FFN = 64
N_EXPERTS = 8
LN_EPS = 1e-6
IN_ATT = ATT_WIDTH + 2 * ATT_KV_WIDTH
IN_RWKV = 3 * RWKV_WIDTH + 2 * W_LORA + 2 * A_LORA + G_LORA
IN_RWKV_PAD = 896
IN_HY = 3 * HY_WIDTH
ALPHA = float((2 * DEPTH) ** 0.25)
LOG2E = 1.4426950408889634
MAX_UNSHIFTED_SCORE = 40.0

LANES = 128
SUBLANES = 8
TOK_TILE = 256
ROW_TILE_MID = 640
ROW_TILE_WIDE = 1280
WKV_CHUNK = 64
WKV_TILE = 256
DFT_N2 = 256
MOE_ROW_BLOCK = 128
MOE_MOVE_BLOCK = 256
VMEM_LIMIT = 48 * 1024 * 1024
VMEM_LIMIT_MOE = 56 * 1024 * 1024


def _cparams(sem, vmem=VMEM_LIMIT):
    return pltpu.CompilerParams(dimension_semantics=sem, vmem_limit_bytes=vmem)


def _pick_tile(n, cands):
    for c in cands:
        if n % c == 0:
            return c
    raise ValueError(f"no tile for {n} in {cands}")


def _layer_norm(x):
    mu = jnp.mean(x, -1, keepdims=True)
    xc = x - mu
    var = jnp.mean(xc * xc, -1, keepdims=True)
    return xc * lax.rsqrt(var + LN_EPS)


def _sel_mod(mod_ref, j, row0, tm, lc):
    rid = row0 + lax.broadcasted_iota(jnp.int32, (tm, 1), 0)
    return jnp.where(rid < lc, mod_ref[0, 0, j:j + 1, :], mod_ref[0, 1, j:j + 1, :])


def _silu(x):
    return x * jax.nn.sigmoid(x)


def _ada_kernel(c_ref, w_ref, b_ref, o_ref):
    s = _silu(c_ref[...])
    o_ref[0] = _dotp(s, w_ref[0], 3) + b_ref[0]


def _ada_mod(cond8, ada_w, ada_b):
    depth, d, n = ada_w.shape
    tn = _pick_tile(n, (1536, 1024, 512, 256, 128))
    return pl.pallas_call(
        _ada_kernel,
        grid=(depth, n // tn),
        in_specs=[pl.BlockSpec((SUBLANES, d), lambda l, j: (0, 0)),
                  pl.BlockSpec((1, d, tn), lambda l, j: (l, 0, j)),
                  pl.BlockSpec((1, 1, tn), lambda l, j: (l, 0, j))],
        out_specs=pl.BlockSpec((1, SUBLANES, tn), lambda l, j: (l, 0, j)),
        out_shape=jax.ShapeDtypeStruct((depth, SUBLANES, n), F32),
        compiler_params=_cparams(("parallel", "parallel")),
        name="ada_mod",
    )(cond8, ada_w, ada_b.reshape(depth, 1, n))


def _inproj_kernel(x_ref, mod_ref, w_ref, oa_ref, or_ref, oh_ref, *, lc, tm):
    row0 = pl.program_id(1) * tm
    sh = _sel_mod(mod_ref, 0, row0, tm, lc)
    sc = _sel_mod(mod_ref, 1, row0, tm, lc)
    h = (_layer_norm(x_ref[0]) * (1.0 + sc) + sh).astype(BF16)
    u = jnp.dot(h, w_ref[...], preferred_element_type=F32)
    oa_ref[0] = u[:, :IN_ATT]
    or_ref[0] = u[:, IN_ATT:IN_ATT + IN_RWKV_PAD]
    oh_ref[0] = u[:, IN_ATT + IN_RWKV_PAD:]


def _inproj(xx, mod, w_pad, lc):
    b, lt, d = xx.shape
    tm = _pick_tile(lt, (ROW_TILE_MID, TOK_TILE))
    n = w_pad.shape[1]
    return pl.pallas_call(
        functools.partial(_inproj_kernel, lc=lc, tm=tm),
        grid=(b, lt // tm),
        in_specs=[pl.BlockSpec((1, tm, d), lambda bi, i: (bi, i, 0)),
                  pl.BlockSpec((1, 2, 6, d), lambda bi, i: (bi, 0, 0, 0)),
                  pl.BlockSpec((d, n), lambda bi, i: (0, 0))],
        out_specs=[pl.BlockSpec((1, tm, IN_ATT), lambda bi, i: (bi, i, 0)),
                   pl.BlockSpec((1, tm, IN_RWKV_PAD), lambda bi, i: (bi, i, 0)),
                   pl.BlockSpec((1, tm, IN_HY), lambda bi, i: (bi, i, 0))],
        out_shape=[jax.ShapeDtypeStruct((b, lt, IN_ATT), F32),
                   jax.ShapeDtypeStruct((b, lt, IN_RWKV_PAD), F32),
                   jax.ShapeDtypeStruct((b, lt, IN_HY), F32)],
        compiler_params=_cparams(("parallel", "parallel")),
        name="inproj",
    )(xx, mod, w_pad)


def _attn_prep_kernel(u_ref, cos_ref, sin_ref, qg_ref, kg_ref, bd_ref, sw_ref, qt_ref, k_ref, vt_ref):
    u = u_ref[0]

    def norm_rope(x, g):
        w = x.shape[1]
        tile = lambda t: jnp.concatenate([t] * (w // LANES), -1)
        ms = _dot01(x * x, bd_ref[:w, :w]) * (1.0 / HEAD_DIM)
        xn = x * lax.rsqrt(ms + QK_EPS) * tile(g)
        sw = _dot01(xn, sw_ref[:w, :w])
        return xn * tile(cos_ref[...]) + sw * tile(sin_ref[...])

    q = norm_rope(u[:, :ATT_WIDTH], qg_ref[...]) * (LOG2E * HEAD_DIM ** -0.5)
    qt = q.T
    kx = norm_rope(u[:, ATT_WIDTH:ATT_WIDTH + ATT_KV_WIDTH], kg_ref[...])
    for g in range(ATT_KV_HEADS):
        base = g * ATT_REP * HEAD_DIM
        qt_ref[0, g, 0] = jnp.concatenate(
            [qt[base + r * HEAD_DIM:base + (r + 1) * HEAD_DIM] for r in range(ATT_REP)], -1).astype(BF16)
        k_ref[0, g] = kx[:, g * HEAD_DIM:(g + 1) * HEAD_DIM].astype(BF16)
    v0 = ATT_WIDTH + ATT_KV_WIDTH
    vt = u[:, v0:v0 + ATT_KV_WIDTH].T
    for g in range(ATT_KV_HEADS):
        vt_ref[0, g] = vt[g * HEAD_DIM:(g + 1) * HEAD_DIM].astype(BF16)


def _attn_prep(ua, cos, sin, qg, kg, bd_att, swap_att, lc):
    b, lt, _ = ua.shape
    tm = TOK_TILE
    nct = lc // tm
    nl = lt // tm - nct
    q_pos = lambda i: jnp.where(i < nct, nl + i, i - nct)
    return pl.pallas_call(
        _attn_prep_kernel,
        grid=(b, lt // tm),
        in_specs=[pl.BlockSpec((1, tm, IN_ATT), lambda bi, i: (bi, i, 0)),
                  pl.BlockSpec((tm, LANES), lambda bi, i: (i, 0)),
                  pl.BlockSpec((tm, LANES), lambda bi, i: (i, 0)),
                  pl.BlockSpec((1, LANES), lambda bi, i: (0, 0)),
                  pl.BlockSpec((1, LANES), lambda bi, i: (0, 0)),
                  pl.BlockSpec((ATT_WIDTH, ATT_WIDTH), lambda bi, i: (0, 0)),
                  pl.BlockSpec((ATT_WIDTH, ATT_WIDTH), lambda bi, i: (0, 0))],
        out_specs=[pl.BlockSpec((1, ATT_KV_HEADS, 1, HEAD_DIM, ATT_REP * tm),
                                lambda bi, i: (bi, 0, q_pos(i), 0, 0)),
                   pl.BlockSpec((1, ATT_KV_HEADS, tm, HEAD_DIM), lambda bi, i: (bi, 0, i, 0)),
                   pl.BlockSpec((1, ATT_KV_HEADS, HEAD_DIM, tm), lambda bi, i: (bi, 0, 0, i))],
        out_shape=[jax.ShapeDtypeStruct((b, ATT_KV_HEADS, lt // tm, HEAD_DIM, ATT_REP * tm), BF16),
                   jax.ShapeDtypeStruct((b, ATT_KV_HEADS, lt, HEAD_DIM), BF16),
                   jax.ShapeDtypeStruct((b, ATT_KV_HEADS, HEAD_DIM, lt), BF16)],
        compiler_params=_cparams(("parallel", "parallel")),
        name="attn_prep",
    )(ua, cos, sin, qg, kg, bd_att, swap_att)


def _flash_kernel(qt_ref, k_ref, vt_ref, o_ref, m_scr, l_scr, acc_scr, *, nk, tq, sub, qn):
    j = pl.program_id(3)

    @pl.when(j == 0)
    def _():
        m_scr[...] = jnp.full(m_scr.shape, -jnp.inf, F32)
        l_scr[...] = jnp.zeros(l_scr.shape, F32)
        acc_scr[...] = jnp.zeros(acc_scr.shape, F32)

    qt = jnp.concatenate([qt_ref[0, 0, t] for t in range(qn)], -1)
    nsub = k_ref.shape[2] // sub
    m = m_scr[...]
    l = l_scr[...]
    acc = acc_scr[...]
    scores = lambda c: jnp.dot(k_ref[0, 0, c * sub:(c + 1) * sub, :], qt, preferred_element_type=F32)
    pv = lambda c, p: jnp.dot(vt_ref[0, 0, :, c * sub:(c + 1) * sub], p, preferred_element_type=F32)
    s_next = scores(0)
    pend = None
    for c in range(nsub):
        s = s_next
        if c + 1 < nsub:
            s_next = scores(c + 1)
        if pend is not None:
            acc = pend[0] * acc + pv(c - 1, pend[1])
        m_new = jnp.maximum(m, jnp.max(s, 0, keepdims=True))
        a = jnp.exp2(m - m_new)
        p = jnp.exp2(s - m_new)
        l = a * l + jnp.sum(p, 0, keepdims=True)
        pend = (a, p.astype(BF16))
        m = m_new
    acc = pend[0] * acc + pv(nsub - 1, pend[1])
    m_scr[...] = m
    l_scr[...] = l
    acc_scr[...] = acc

    @pl.when(j == nk - 1)
    def _():
        o = (acc / l).T
        for t in range(qn):
            o_ref[0, t * tq:(t + 1) * tq, :] = jnp.concatenate(
                [o[(t * ATT_REP + r) * tq:(t * ATT_REP + r + 1) * tq] for r in range(ATT_REP)], -1
            ).astype(o_ref.dtype)


def _flash_bounded_kernel(qt_ref, k_ref, vt_ref, o_ref, l_scr, acc_scr, *, nk, tq, sub, qn):
    j = pl.program_id(3)

    @pl.when(j == 0)
    def _():
        l_scr[...] = jnp.zeros(l_scr.shape, F32)
        acc_scr[...] = jnp.zeros(acc_scr.shape, F32)

    qt = jnp.concatenate([qt_ref[0, 0, t] for t in range(qn)], -1)
    nsub = k_ref.shape[2] // sub

    l = l_scr[...]
    acc = acc_scr[...]
    scores = lambda c: jnp.dot(k_ref[0, 0, c * sub:(c + 1) * sub, :], qt, preferred_element_type=F32)
    s_next = scores(0)
    for c in range(nsub):
        s = s_next
        if c + 1 < nsub:
            s_next = scores(c + 1)
        p = jnp.exp2(s)
        l = l + jnp.sum(p.reshape(sub // SUBLANES, SUBLANES, p.shape[1]), 0)
        acc = acc + jnp.dot(vt_ref[0, 0, :, c * sub:(c + 1) * sub], p.astype(BF16), preferred_element_type=F32)
    l_scr[...] = l
    acc_scr[...] = acc

    @pl.when(j == nk - 1)
    def _():
        o = (acc / jnp.sum(l, 0, keepdims=True)).T
        for t in range(qn):
            o_ref[0, t * tq:(t + 1) * tq, :] = jnp.concatenate(
                [o[(t * ATT_REP + r) * tq:(t * ATT_REP + r + 1) * tq] for r in range(ATT_REP)], -1
            ).astype(o_ref.dtype)


def _flash(qt, k, vt, tq, tk, qn, q_blk0, nq, nk, bounded=False):
    b = qt.shape[0]
    lq = nq * qn * tq
    sub = _pick_tile(tk, (256, 128))
    lanes = qn * ATT_REP * tq
    if bounded:
        body = functools.partial(_flash_bounded_kernel, nk=nk, tq=tq, sub=sub, qn=qn)
        scratch = [pltpu.VMEM((SUBLANES, lanes), F32), pltpu.VMEM((HEAD_DIM, lanes), F32)]
    else:
        body = functools.partial(_flash_kernel, nk=nk, tq=tq, sub=sub, qn=qn)
        scratch = [pltpu.VMEM((1, lanes), F32), pltpu.VMEM((1, lanes), F32), pltpu.VMEM((HEAD_DIM, lanes), F32)]
    return pl.pallas_call(
        body,
        grid=(b, ATT_KV_HEADS, nq, nk),
        in_specs=[pl.BlockSpec((1, 1, qn, HEAD_DIM, ATT_REP * tq), lambda bi, g, i, j: (bi, g, i + q_blk0, 0, 0)),
                  pl.BlockSpec((1, 1, tk, HEAD_DIM), lambda bi, g, i, j: (bi, g, j, 0)),
                  pl.BlockSpec((1, 1, HEAD_DIM, tk), lambda bi, g, i, j: (bi, g, 0, j))],
        out_specs=pl.BlockSpec((1, qn * tq, ATT_REP * HEAD_DIM), lambda bi, g, i, j: (bi, i, g)),
        out_shape=jax.ShapeDtypeStruct((b, lq, ATT_WIDTH), BF16),
        scratch_shapes=scratch,
        compiler_params=_cparams(("parallel", "parallel", "parallel", "arbitrary")),
        name="flash_bounded" if bounded else "flash",
    )(qt, k, vt)


def _prev_next(u, up8, un8, i, tm, lc, lt):
    rid = lax.broadcasted_iota(jnp.int32, u.shape, 0)
    gid = rid + i * tm
    prev = jnp.where(rid == 0, up8[SUBLANES - 1:SUBLANES], pltpu.roll(u, 1, 0))
    prev = jnp.where(gid == 0, 0.0, jnp.where(gid == lc, 0.0, prev))
    nxt = jnp.where(rid == tm - 1, un8[0:1], pltpu.roll(u, tm - 1, 0))
    nxt = jnp.where(gid == lc - 1, 0.0, jnp.where(gid == lt - 1, 0.0, nxt))
    return prev, nxt


def _halo_specs(tm, c, lt):
    r = tm // SUBLANES
    last = lt // SUBLANES - 1
    return [pl.BlockSpec((1, tm, c), lambda bi, i: (bi, i, 0)),
            pl.BlockSpec((1, SUBLANES, c), lambda bi, i: (bi, jnp.maximum(i * r - 1, 0), 0)),
            pl.BlockSpec((1, SUBLANES, c), lambda bi, i: (bi, jnp.minimum((i + 1) * r, last), 0))]


def _softplus(z):
    return jnp.maximum(z, 0.0) + jnp.log1p(jnp.exp(-jnp.abs(z)))


def _rwkv_prep_kernel(u_ref, up_ref, un_ref, mu_ref, wl_ref, bd_ref, w0_ref, a0_ref, kkw_ref, ka_ref, rk_ref,
                      r_o, v_o, kk_o, g_o, bon_o, lw_o, be_o, kd_o, *, lc, lt, tm):
    i = pl.program_id(1)
    u = u_ref[0]
    prev, nxt = _prev_next(u, up_ref[0], un_ref[0], i, tm, lc, lt)
    us = u + mu_ref[0:1] * (prev - u) + mu_ref[1:2] * (nxt - u)
    c = RWKV_WIDTH
    r = us[:, 0:c]
    k = us[:, c:2 * c]
    v = us[:, 2 * c:3 * c]
    slab = us[:, 3 * c:3 * c + LANES]
    lane = lax.broadcasted_iota(jnp.int32, slab.shape, 1)
    o_a = 2 * W_LORA
    o_g = o_a + 2 * A_LORA
    act = jnp.where(lane < o_a, jnp.tanh(slab),
                    jnp.where(lane < o_g, slab,
                              jnp.where(lane < o_g + G_LORA, jax.nn.sigmoid(slab), 0.0)))
    lo = _dotp(act, wl_ref[...], 3)
    bd = bd_ref[...]
    kk0 = k * kkw_ref[...]
    kk = kk0 * lax.rsqrt(_dot01(kk0 * kk0, bd) + 1e-12)
    r_o[0] = r
    v_o[0] = v
    kk_o[0] = kk
    g_o[0] = lo[:, 4 * c:5 * c]
    bon = jnp.zeros_like(r)
    for d in range(2):
        w_raw = w0_ref[d:d + 1] + lo[:, d * c:(d + 1) * c]
        lw = -jnp.exp(-_softplus(-w_raw) - 0.5)
        a = jax.nn.sigmoid(a0_ref[d:d + 1] + lo[:, (2 + d) * c:(3 + d) * c])
        kd = k * (1.0 + (a - 1.0) * ka_ref[...])
        lw_o[0, d] = lw
        be_o[0, d] = a * kk
        kd_o[0, d] = kd
        bon = bon + r * kd * rk_ref[...]
    bon_o[0] = _dot01(bon, bd) * v


def _rwkv_prep(ur, mu, wl, bd, w0, a0, kkw, ka, rk, lc):
    b, lt, cp = ur.shape
    tm = _pick_tile(lt, (ROW_TILE_MID, TOK_TILE))
    c = RWKV_WIDTH
    full = lambda shape: pl.BlockSpec(shape, lambda bi, i: (0,) * len(shape))
    tok = pl.BlockSpec((1, tm, c), lambda bi, i: (bi, i, 0))
    tok2 = pl.BlockSpec((1, 2, tm, c), lambda bi, i: (bi, 0, i, 0))
    s1 = jax.ShapeDtypeStruct((b, lt, c), F32)
    s2 = jax.ShapeDtypeStruct((b, 2, lt, c), F32)
    return pl.pallas_call(
        functools.partial(_rwkv_prep_kernel, lc=lc, lt=lt, tm=tm),
        grid=(b, lt // tm),
        in_specs=_halo_specs(tm, cp, lt) + [full((2, cp)), full((LANES, 5 * c)), full((c, c)), full((2, c)),
                                            full((2, c)), full((1, c)), full((1, c)), full((1, c))],
        out_specs=[tok, tok, tok, tok, tok, tok2, tok2, tok2],
        out_shape=[s1, s1, s1, s1, s1, s2, s2, s2],
        compiler_params=_cparams(("parallel", "parallel")),
        name="rwkv_prep",
    )(ur, ur, ur, mu, wl, bd, w0, a0, kkw, ka, rk)


def _mm(a, b):
    return jnp.dot(a, b, precision=HI, preferred_element_type=F32)


_NN = ((1,), (0,))
_NT = ((1,), (1,))
_TN = ((0,), (0,))


def _split2(a):
    hi = a.astype(BF16)
    return hi, (a - hi.astype(F32)).astype(BF16)


def _dotp(a, b, passes, dims=_NN):
    if a.ndim == 3:
        dn = (((dims[0][0] + 1,), (dims[1][0] + 1,)), ((0,), (0,)))
    else:
        dn = (dims, ((), ()))
    dg = lambda p, q: lax.dot_general(p, q, dn, preferred_element_type=F32)
    if passes == 1:
        return dg(a.astype(BF16), b.astype(BF16))
    ah, al = _split2(a)
    bh, bl = _split2(b)
    return dg(ah, bh) + dg(ah, bl) + dg(al, bh)


def _dot01(a, ones):
    ah, al = _split2(a)
    o = ones.astype(BF16)
    return jnp.dot(ah, o, preferred_element_type=F32) + jnp.dot(al, o, preferred_element_type=F32)


P_M = 1
P_INV = 1
P_W = 1
P_Z = 1
P_STATE = 3
P_DFT = 1


def _unit_tri_inv(a_mat, row, col, eye, passes):
    t = a_mat.shape[-1]
    eye_f = eye.astype(F32)
    base = SUBLANES
    same = (row // base) == (col // base)
    n1 = -jnp.where(same, a_mat, 0.0)
    n2 = _dotp(n1, n1, passes)
    n4 = _dotp(n2, n2, passes)
    x = _dotp(_dotp(eye_f + n1, eye_f + n2, passes), eye_f + n4, passes)
    m = base
    while m < t:
        off = jnp.logical_and((row // (2 * m)) == (col // (2 * m)), (row // m) != (col // m))
        x = x - _dotp(x, _dotp(jnp.where(off, a_mat, 0.0), x, passes), passes)
        m *= 2
    return x


def _wkv_kernel(rf_ref, vf_ref, kkf_ref, rb_ref, vb_ref, kkb_ref, lwf_ref, bef_ref, kdf_ref,
                lwb_ref, beb_ref, kdb_ref, yf_ref, yb_ref, h_scr, pq_scr, ry_scr):
    t = WKV_CHUNK
    n = HEAD_DIM
    g = WKV_TILE // WKV_CHUNK
    tt = WKV_TILE
    nh = RWKV_HEADS
    nd = g * nh
    nu = 2 * nd
    orders = (list(range(g)), list(range(g - 1, -1, -1)))

    @pl.when(pl.program_id(1) == 0)
    def _():
        h_scr[...] = jnp.zeros(h_scr.shape, F32)
        pq_scr[...] = jnp.zeros(pq_scr.shape, F32)
        ry_scr[...] = jnp.zeros(ry_scr.shape, F32)

    hm = h_scr[...]
    for p in range(g):
        both = lambda ref: jnp.concatenate([ref[d * nd + p * nh:d * nd + (p + 1) * nh] for d in range(2)], 0)
        ry = both(ry_scr)
        pq = both(pq_scr)
        y = _dotp(ry[:, :, :n], hm, P_STATE) + ry[:, :, n:]
        hm = _dotp(pq[:, :, :n], hm, P_STATE) + pq[:, :, n:]
        for d, y_ref in enumerate((yf_ref, yb_ref)):
            c = orders[d][p]
            y_ref[0, c * t:(c + 1) * t, :] = jnp.concatenate([y[d * nh + h] for h in range(nh)], -1)
    h_scr[...] = hm

    row = lax.broadcasted_iota(jnp.int32, (tt, tt), 0)
    col = lax.broadcasted_iota(jnp.int32, (tt, tt), 1)
    same = (row // t) == (col // t)

    def scaled(d, r_ref, v_ref, kk_ref, lw_ref, be_ref, kd_ref):
        tri = jnp.logical_and(same, (row >= col) if d == 0 else (row <= col))
        sums = jnp.concatenate([jnp.where(tri, 1.0, 0.0), jnp.where(same, 1.0, 0.0)], 0).astype(BF16)
        lw = lw_ref[0, 0]
        l1 = lw.astype(BF16)
        rem = lw - l1.astype(F32)
        l2 = rem.astype(BF16)
        l3 = (rem - l2.astype(F32)).astype(BF16)
        cc = (jnp.dot(sums, l1, preferred_element_type=F32) + jnp.dot(sums, l2, preferred_element_type=F32)
              + jnp.dot(sums, l3, preferred_element_type=F32))
        cum = cc[:tt]
        ctot = cc[tt:]
        e_neg = jnp.exp(-cum)
        e_end = jnp.exp(ctot - cum)
        be = be_ref[0, 0]
        kd = kd_ref[0, 0]

        def units(x, rows=t):
            return jnp.stack([x[orders[d][p] * t:orders[d][p] * t + rows, h * n:(h + 1) * n]
                              for p in range(g) for h in range(nh)], 0)

        return dict(kap=units(kk_ref[0] * jnp.exp(cum - lw)), rt=units(r_ref[0] * jnp.exp(cum)), vh=units(v_ref[0]),
                    bet=units(be * e_neg), kdt=units(kd * e_neg), beh=units(be * e_end), kdh=units(kd * e_end),
                    gend=units(jnp.exp(ctot), 1))

    parts = (scaled(0, rf_ref, vf_ref, kkf_ref, lwf_ref, bef_ref, kdf_ref),
             scaled(1, rb_ref, vb_ref, kkb_ref, lwb_ref, beb_ref, kdb_ref))
    cat = lambda key: jnp.concatenate([parts[0][key], parts[1][key]], 0)
    kap, rt, vh = cat("kap"), cat("rt"), cat("vh")
    r64 = lax.broadcasted_iota(jnp.int32, (t, t), 0)
    c64 = lax.broadcasted_iota(jnp.int32, (t, t), 1)
    eye = r64 == c64
    unit = lax.broadcasted_iota(jnp.int32, (nu, t, t), 0)
    ahead = (r64 - c64)[None] * jnp.where(unit < nd, 1, -1)
    strict = ahead > 0
    incl = ahead >= 0
    m = _dotp(jnp.concatenate([kap, rt], 1), jnp.concatenate([cat("bet"), cat("kdt")], 1), P_M, _NT)
    a_mat = jnp.where(strict, m[:, :t, :t], 0.0)
    b_mat = jnp.where(strict, m[:, :t, t:], 0.0)
    ab_r = jnp.concatenate([jnp.where(incl, m[:, t:, :t], 0.0), jnp.where(incl, m[:, t:, t:], 0.0)], 2)
    tinv = _unit_tri_inv(a_mat, r64, c64, eye, P_INV)
    w = _dotp(tinv, jnp.concatenate([kap, _dotp(b_mat, vh, P_W)], 2), P_W)
    z = jnp.concatenate([-w, jnp.concatenate([jnp.zeros((nu, t, n), F32), vh], 2)], 1)
    ry = _dotp(ab_r, z, P_Z)
    ry_scr[:, :, :n] = ry[:, :, :n] + rt
    ry_scr[:, :, n:] = ry[:, :, n:]
    pq = _dotp(jnp.concatenate([cat("beh"), cat("kdh")], 1), z, P_Z, _TN)
    gd = jnp.where(eye, jnp.broadcast_to(cat("gend"), (nu, n, n)), 0.0)
    pq_scr[:, :, :n] = pq[:, :, :n] + gd
    pq_scr[:, :, n:] = pq[:, :, n:]


def _wkv_scan(r, v, kk, lw, be, kd, lc):
    b, lt, c = r.shape
    tt = WKV_TILE
    nt = lt // tt
    ntc = lc // tt
    tiles = (lambda i: i, lambda i: jnp.where(i < ntc, ntc - 1 - i, nt - 1 - (i - ntc)))
    t_in = lambda d: (lambda i: tiles[d](jnp.minimum(i, nt - 1)))
    t_out = lambda d: (lambda i: tiles[d](jnp.maximum(i - 1, 0)))
    one = lambda d: pl.BlockSpec((1, tt, c), lambda bi, i: (bi, t_in(d)(i), 0))
    two = lambda d: pl.BlockSpec((1, 1, tt, c), lambda bi, i: (bi, d, t_in(d)(i), 0))
    out = lambda d: pl.BlockSpec((1, tt, c), lambda bi, i: (bi, t_out(d)(i), 0))
    nu = 2 * (tt // WKV_CHUNK) * RWKV_HEADS
    ysh = jax.ShapeDtypeStruct((b, lt, c), F32)
    return pl.pallas_call(
        _wkv_kernel,
        grid=(b, nt + 1),
        in_specs=[one(0)] * 3 + [one(1)] * 3 + [two(0)] * 3 + [two(1)] * 3,
        out_specs=[out(0), out(1)],
        out_shape=[ysh, ysh],
        scratch_shapes=[pltpu.VMEM((2 * RWKV_HEADS, HEAD_DIM, HEAD_DIM), F32),
                        pltpu.VMEM((nu, HEAD_DIM, 2 * HEAD_DIM), F32),
                        pltpu.VMEM((nu, WKV_CHUNK, 2 * HEAD_DIM), F32)],
        compiler_params=_cparams(("parallel", "arbitrary")),
        name="wkv_scan",
    )(r, v, kk, r, v, kk, lw, be, kd, lw, be, kd)


def _rwkv_out_kernel(yf_ref, yb_ref, bon_ref, g_ref, bd_ref, gg_ref, gb_ref, o_ref):
    y = yf_ref[0] + yb_ref[0] + bon_ref[0]
    bd = bd_ref[...]
    mu = _dot01(y, bd) * (1.0 / HEAD_DIM)
    yc = y - mu
    var = _dot01(yc * yc, bd) * (1.0 / HEAD_DIM)
    yn = yc * lax.rsqrt(var + RWKV_GN_EPS) * gg_ref[...] + gb_ref[...]
    o_ref[0] = (yn * g_ref[0]).astype(o_ref.dtype)


def _rwkv_out(yf, yb, bon, g, bd, gg, gb):
    b, lt, c = yf.shape
    tm = _pick_tile(lt, (ROW_TILE_WIDE, ROW_TILE_MID, TOK_TILE))
    tok = pl.BlockSpec((1, tm, c), lambda bi, i: (bi, i, 0))
    full = lambda shape: pl.BlockSpec(shape, lambda bi, i: (0,) * len(shape))
    return pl.pallas_call(
        _rwkv_out_kernel,
        grid=(b, lt // tm),
        in_specs=[tok, tok, tok, tok, full((c, c)), full((1, c)), full((1, c))],
        out_specs=tok,
        out_shape=jax.ShapeDtypeStruct((b, lt, c), BF16),
        compiler_params=_cparams(("parallel", "parallel")),
        name="rwkv_out",
    )(yf, yb, bon, g, bd, gg, gb)


def _hy_prep_kernel(u_ref, up_ref, un_ref, w_ref, b_ref, x1_o, x2_o, v_o, *, lc, lt, tm):
    i = pl.program_id(1)
    u = u_ref[0]
    prev, nxt = _prev_next(u, up_ref[0], un_ref[0], i, tm, lc, lt)
    y = prev * w_ref[0:1] + u * w_ref[1:2] + nxt * w_ref[2:3] + b_ref[...]
    c = HY_WIDTH
    x1_o[0] = y[:, :c]
    x2_o[0] = y[:, c:2 * c]
    v_o[0] = y[:, 2 * c:]


def _hy_prep(uh, w, bias, lc):
    b, lt, cin = uh.shape
    tm = _pick_tile(lt, (ROW_TILE_WIDE, ROW_TILE_MID, TOK_TILE))
    c = HY_WIDTH
    full = lambda shape: pl.BlockSpec(shape, lambda bi, i: (0,) * len(shape))
    tok = pl.BlockSpec((1, tm, c), lambda bi, i: (bi, i, 0))
    s1 = jax.ShapeDtypeStruct((b, lt, c), F32)
    return pl.pallas_call(
        functools.partial(_hy_prep_kernel, lc=lc, lt=lt, tm=tm),
        grid=(b, lt // tm),
        in_specs=_halo_specs(tm, cin, lt) + [full((3, cin)), full((1, cin))],
        out_specs=[tok, tok, tok],
        out_shape=[s1, s1, s1],
        compiler_params=_cparams(("parallel", "parallel")),
        name="hy_prep",
    )(uh, uh, uh, w, bias)


def _hy_filter_kernel(z_ref, w1_ref, b1_ref, f1_ref, w2_ref, b2_ref, f2_ref, w3_ref, dec_ref,
                      fw_o, bw_o, nrm_o, *, tl):
    i = pl.program_id(0)
    z = z_ref[...]
    h = jnp.sin(f1_ref[...] * (_dotp(z, w1_ref[...], 3) + b1_ref[...]))
    h = jnp.sin(f2_ref[...] * (_dotp(h, w2_ref[...], 3) + b2_ref[...]))
    h = _dotp(h, w3_ref[...], 3) * jnp.exp(-z[:, 0:1] * dec_ref[...])
    c = HY_WIDTH
    fw = jnp.concatenate([h[:, 0:c], h[:, 2 * c:3 * c]], 1)
    bw = jnp.concatenate([h[:, c:2 * c], h[:, 3 * c:4 * c]], 1)
    rid = i * tl + lax.broadcasted_iota(jnp.int32, (tl, 1), 0)
    bw = jnp.where(rid == 0, 0.0, bw)
    fw_o[...] = fw
    bw_o[...] = bw

    @pl.when(i == 0)
    def _():
        nrm_o[...] = jnp.zeros(nrm_o.shape, F32)

    nrm_o[...] += jnp.sum(jnp.abs(fw) + jnp.abs(bw), 0, keepdims=True)


def _hy_filter(feat, w1p, b1, f1, w2, b2, f2, w3, dec):
    l, fe = feat.shape
    tl = _pick_tile(l, (512, 256))
    c2 = HY_ORDER * HY_WIDTH
    full = lambda shape: pl.BlockSpec(shape, lambda i: (0,) * len(shape))
    return pl.pallas_call(
        functools.partial(_hy_filter_kernel, tl=tl),
        grid=(l // tl,),
        in_specs=[pl.BlockSpec((tl, fe), lambda i: (i, 0)), full(w1p.shape), full(b1.shape), full(f1.shape),
                  full(w2.shape), full(b2.shape), full(f2.shape), full(w3.shape), full(dec.shape)],
        out_specs=[pl.BlockSpec((tl, c2), lambda i: (i, 0)), pl.BlockSpec((tl, c2), lambda i: (i, 0)),
                   pl.BlockSpec((1, c2), lambda i: (0, 0))],
        out_shape=[jax.ShapeDtypeStruct((l, c2), F32), jax.ShapeDtypeStruct((l, c2), F32),
                   jax.ShapeDtypeStruct((1, c2), F32)],
        compiler_params=_cparams(("arbitrary",)),
        name="hy_filter",
    )(feat, w1p, b1, f1, w2, b2, f2, w3, dec)


def _dft_cols_kernel(f_ref, xa_ref, xb_ref, o_ref, *, n1, pair):
    f = f_ref[...]
    pa = _dotp(f, xa_ref[0], P_DFT)
    pb = _dotp(f, xb_ref[0], P_DFT)
    if pair:
        o_ref[0] = pa[:n1].astype(o_ref.dtype)
        o_ref[1] = pa[n1:].astype(o_ref.dtype)
        o_ref[2] = pb[:n1].astype(o_ref.dtype)
        o_ref[3] = pb[n1:].astype(o_ref.dtype)
    else:
        o_ref[0] = (pa[:n1] - pb[n1:]).astype(o_ref.dtype)
        o_ref[1] = (pb[:n1] + pa[n1:]).astype(o_ref.dtype)


def _dft_cols(fstack, xa, ia, xb, ib, pair):
    n1 = fstack.shape[0] // 2
    _, nh, cols = xa.shape
    tc = _pick_tile(cols, (4096, 2048, 1024, 512, 256, 128))
    no = 4 if pair else 2
    return pl.pallas_call(
        functools.partial(_dft_cols_kernel, n1=n1, pair=pair),
        grid=(cols // tc,),
        in_specs=[pl.BlockSpec(fstack.shape, lambda j: (0, 0)),
                  pl.BlockSpec((1, nh, tc), lambda j: (ia, 0, j)),
                  pl.BlockSpec((1, nh, tc), lambda j: (ib, 0, j))],
        out_specs=pl.BlockSpec((no, n1, tc), lambda j: (0, 0, j)),
        out_shape=jax.ShapeDtypeStruct((no, n1, cols), BF16),
        compiler_params=_cparams(("parallel",)),
        name="dft_cols",
    )(fstack, xa, xb)


def _cplx_left(gs, zr, zi, n):
    c = zr.shape[1]
    p = _dotp(gs, jnp.concatenate([zr, zi], 1), P_DFT)
    return p[:n, :c] - p[n:, c:], p[:n, c:] + p[n:, :c]


def _spec_kernel(a_ref, g_ref, nrm_ref, o_ref, *, n_total, kp):
    n2 = DFT_N2
    s = 1.0 / (nrm_ref[...] * n_total)
    for q in range(kp):
        gs = jnp.concatenate([g_ref[q, 0], g_ref[q, 1]], 0)
        fr, fi = _cplx_left(gs, a_ref[0, q], a_ref[1, q], n2)
        br, bi = _cplx_left(gs, a_ref[2, q], a_ref[3, q], n2)
        o_ref[0, q] = ((fr + br) * s).astype(o_ref.dtype)
        o_ref[1, q] = ((fi - bi) * s).astype(o_ref.dtype)


def _planes_per_step(n1):
    return next(k for k in (4, 2, 1) if n1 % k == 0)


def _spec(a4, g, nrm, n_total):
    _, n1, n2, c2 = a4.shape
    kp = _planes_per_step(n1)
    return pl.pallas_call(
        functools.partial(_spec_kernel, n_total=float(n_total), kp=kp),
        grid=(n1 // kp,),
        in_specs=[pl.BlockSpec((4, kp, n2, c2), lambda k: (0, k, 0, 0)),
                  pl.BlockSpec((kp, 2, n2, n2), lambda k: (k, 0, 0, 0)),
                  pl.BlockSpec((1, c2), lambda k: (0, 0))],
        out_specs=pl.BlockSpec((2, kp, n2, c2), lambda k: (0, k, 0, 0)),
        out_shape=jax.ShapeDtypeStruct((2, n1, n2, c2), BF16),
        compiler_params=_cparams(("parallel",)),
        name="hy_spec",
    )(a4, g, nrm)


def _conv_mid_kernel(a_ref, g_ref, k_ref, o_ref, *, kp):
    n2 = DFT_N2
    for q in range(kp):
        gs = jnp.concatenate([g_ref[q, 0], g_ref[q, 1]], 0)
        xr, xi = _cplx_left(gs, a_ref[0, q], a_ref[1, q], n2)
        c = xr.shape[1]
        kr = k_ref[0, q].astype(F32)
        ki = k_ref[1, q].astype(F32)
        zr = xr * kr - xi * ki
        zi = xr * ki + xi * kr
        zst = jnp.concatenate([jnp.concatenate([zr, zi], 1), jnp.concatenate([zi, -zr], 1)], 0)
        y = _dotp(gs, zst, P_DFT, _TN)
        o_ref[0, q] = y[:, :c].astype(o_ref.dtype)
        o_ref[1, q] = y[:, c:].astype(o_ref.dtype)


def _conv_mid(a, g, kspec, order):
    _, n1, n2, c = a.shape
    kp = _planes_per_step(n1)
    return pl.pallas_call(
        functools.partial(_conv_mid_kernel, kp=kp),
        grid=(n1 // kp,),
        in_specs=[pl.BlockSpec((2, kp, n2, c), lambda k: (0, k, 0, 0)),
                  pl.BlockSpec((kp, 2, n2, n2), lambda k: (k, 0, 0, 0)),
                  pl.BlockSpec((2, kp, n2, c), lambda k: (0, k, 0, order))],
        out_specs=pl.BlockSpec((2, kp, n2, c), lambda k: (0, k, 0, 0)),
        out_shape=jax.ShapeDtypeStruct((2, n1, n2, c), BF16),
        compiler_params=_cparams(("parallel",)),
        name="hy_conv_mid",
    )(a, g, kspec)


def _idft_cols_kernel(c_ref, b_ref, g0_ref, g1_ref, x0_ref, x1_ref, bias_ref, o_ref, *, nh):
    cs = c_ref[...]
    pr = _dotp(cs, b_ref[0], P_DFT)
    pi = _dotp(cs, b_ref[1], P_DFT)
    yr = pr[:nh] - pi[nh:]
    yi = pi[:nh] + pr[nh:]
    bias = bias_ref[...]
    o_ref[0] = g0_ref[0] * (yr + x0_ref[0] * bias)
    o_ref[1] = g1_ref[0] * (yi + x1_ref[0] * bias)


def _idft_cols(cstack, bv, gate, xin, bias_cols):
    nh2, n1 = cstack.shape
    nh = nh2 // 2
    cols = bv.shape[-1]
    tc = _pick_tile(cols, (4096, 2048, 1024, 512, 256, 128))
    row = lambda bi: pl.BlockSpec((1, nh, tc), lambda j: (bi, 0, j))
    return pl.pallas_call(
        functools.partial(_idft_cols_kernel, nh=nh),
        grid=(cols // tc,),
        in_specs=[pl.BlockSpec((nh2, n1), lambda j: (0, 0)),
                  pl.BlockSpec((2, n1, tc), lambda j: (0, 0, j)),
                  row(0), row(1), row(0), row(1),
                  pl.BlockSpec((1, tc), lambda j: (0, j))],
        out_specs=pl.BlockSpec((2, nh, tc), lambda j: (0, 0, j)),
        out_shape=jax.ShapeDtypeStruct((2, nh, cols), F32),
        compiler_params=_cparams(("parallel",)),
        name="idft_cols",
    )(cstack, bv, gate, gate, xin, xin, bias_cols)


def _hy_ctx_kernel(x1_ref, x2_ref, v_ref, fw_ref, bw_ref, nrm_ref, bias_ref, ff_ref, ci_ref, o_ref, *, lc):
    n = 2 * lc
    c = HY_WIDTH
    ff = ff_ref[...]
    ci = ci_ref[...]
    pf = _mm(ff, fw_ref[...])
    pb = _mm(ff, bw_ref[...])
    s = 1.0 / (nrm_ref[...] * float(n))
    kr = (pf[:n] + pb[:n]) * s
    ki = (pf[n:] - pb[n:]) * s

    def conv(z0, z1, o):
        xr, xi = _cplx_left(ff, z0, z1, n)
        krr = kr[:, o * c:(o + 1) * c]
        kii = ki[:, o * c:(o + 1) * c]
        return _cplx_left(ci, xr * krr - xi * kii, xr * kii + xi * krr, lc)

    v0 = v_ref[0]
    v1 = v_ref[1]
    y0, y1 = conv(v0, v1, 0)
    z0 = x1_ref[0] * (y0 + v0 * bias_ref[0:1])
    z1 = x1_ref[1] * (y1 + v1 * bias_ref[0:1])
    y0, y1 = conv(z0, z1, 1)
    o_ref[0] = x2_ref[0] * (y0 + z0 * bias_ref[1:2])
    o_ref[1] = x2_ref[1] * (y1 + z1 * bias_ref[1:2])


def _hy_ctx(x1, x2, v, fw, bw, nrm, bias, ff, ci):
    b, lc, c = v.shape
    vm = pl.BlockSpec(memory_space=pltpu.VMEM)
    return pl.pallas_call(
        functools.partial(_hy_ctx_kernel, lc=lc),
        in_specs=[vm] * 9,
        out_specs=vm,
        out_shape=jax.ShapeDtypeStruct((b, lc, c), F32),
        compiler_params=pltpu.CompilerParams(vmem_limit_bytes=VMEM_LIMIT),
        name="hy_ctx",
    )(x1, x2, v, fw, bw, nrm, bias, ff, ci)


def _outproj_kernel(attc_ref, attl_ref, rw_ref, hyc_ref, hyl_ref, x_ref, mod_ref, w_ref, lg_ref, lb_ref, o_ref,
                    *, lc, tm):
    row0 = pl.program_id(1) * tm
    a0 = ATT_WIDTH
    a1 = ATT_WIDTH + RWKV_WIDTH
    is_ctx = row0 < lc
    att = jnp.where(is_ctx, attc_ref[0], attl_ref[0])
    hy = jnp.where(is_ctx, hyc_ref[0], hyl_ref[0])
    o = jnp.dot(att, w_ref[:a0], preferred_element_type=F32)
    o += jnp.dot(rw_ref[0], w_ref[a0:a1], preferred_element_type=F32)
    o += jnp.dot(hy.astype(BF16), w_ref[a1:], preferred_element_type=F32)
    g = _sel_mod(mod_ref, 2, row0, tm, lc)
    y = ALPHA * x_ref[0] + g * o
    o_ref[0] = _layer_norm(y) * lg_ref[...] + lb_ref[...]


def _outproj(att_c, att_l, rw, hy_c, hy_l, xx, mod, w, lg, lb, lc):
    b, lt, d = xx.shape
    tm = TOK_TILE
    nct = lc // tm
    tok = lambda c: pl.BlockSpec((1, tm, c), lambda bi, i: (bi, i, 0))
    ctx = lambda c: pl.BlockSpec((1, tm, c), lambda bi, i: (bi, jnp.minimum(i, nct - 1), 0))
    lat = lambda c: pl.BlockSpec((1, tm, c), lambda bi, i: (bi, jnp.maximum(i - nct, 0), 0))
    full = lambda shape: pl.BlockSpec(shape, lambda bi, i: (0,) * len(shape))
    return pl.pallas_call(
        functools.partial(_outproj_kernel, lc=lc, tm=tm),
        grid=(b, lt // tm),
        in_specs=[ctx(ATT_WIDTH), lat(ATT_WIDTH), tok(RWKV_WIDTH), ctx(HY_WIDTH), lat(HY_WIDTH), tok(d),
                  pl.BlockSpec((1, 2, 6, d), lambda bi, i: (bi, 0, 0, 0)),
                  full(w.shape), full((1, d)), full((1, d))],
        out_specs=tok(d),
        out_shape=jax.ShapeDtypeStruct((b, lt, d), F32),
        compiler_params=_cparams(("parallel", "parallel")),
        name="outproj",
    )(att_c, att_l, rw, hy_c, hy_l, xx, mod, w, lg, lb)


def _ffn_kernel(x_ref, mod_ref, w1_ref, w3_ref, w2_ref, lg_ref, lb_ref, o_ref, h_scr, acc_scr, *, lc, tm, nf):
    row0 = pl.program_id(1) * tm
    f = pl.program_id(2)

    @pl.when(f == 0)
    def _():
        sh = _sel_mod(mod_ref, 3, row0, tm, lc)
        sc = _sel_mod(mod_ref, 4, row0, tm, lc)
        h_scr[...] = (_layer_norm(x_ref[0]) * (1.0 + sc) + sh).astype(BF16)
        acc_scr[...] = jnp.zeros(acc_scr.shape, F32)

    h = h_scr[...]
    a = jnp.dot(h, w1_ref[...], preferred_element_type=F32)
    g = jnp.dot(h, w3_ref[...], preferred_element_type=F32)
    acc_scr[...] += jnp.dot((_silu(a) * g).astype(BF16), w2_ref[...], preferred_element_type=F32)

    @pl.when(f == nf - 1)
    def _():
        gate = _sel_mod(mod_ref, 5, row0, tm, lc)
        y = ALPHA * x_ref[0] + gate * acc_scr[...]
        o_ref[0] = _layer_norm(y) * lg_ref[...] + lb_ref[...]


def _ffn(xx, mod, w1, w3, w2, lg, lb, lc):
    b, lt, d = xx.shape
    ff = w1.shape[1]
    tm = _pick_tile(lt, (1280, 768, 512, 256))
    tf = _pick_tile(ff, (256, 128))
    nf = ff // tf
    return pl.pallas_call(
        functools.partial(_ffn_kernel, lc=lc, tm=tm, nf=nf),
        grid=(b, lt // tm, nf),
        in_specs=[pl.BlockSpec((1, tm, d), lambda bi, i, f: (bi, i, 0)),
                  pl.BlockSpec((1, 2, 6, d), lambda bi, i, f: (bi, 0, 0, 0)),
                  pl.BlockSpec((d, tf), lambda bi, i, f: (0, f)),
                  pl.BlockSpec((d, tf), lambda bi, i, f: (0, f)),
                  pl.BlockSpec((tf, d), lambda bi, i, f: (f, 0)),
                  pl.BlockSpec((1, d), lambda bi, i, f: (0, 0)),
                  pl.BlockSpec((1, d), lambda bi, i, f: (0, 0))],
        out_specs=pl.BlockSpec((1, tm, d), lambda bi, i, f: (bi, i, 0)),
        out_shape=jax.ShapeDtypeStruct((b, lt, d), F32),
        scratch_shapes=[pltpu.VMEM((tm, d), BF16), pltpu.VMEM((tm, d), F32)],
        compiler_params=_cparams(("parallel", "parallel", "arbitrary")),
        name="ffn",
    )(xx, mod, w1, w3, w2, lg, lb)


def _route(x_ref, mod_ref, wr_ref, row0, tm, lc):
    sh = _sel_mod(mod_ref, 3, row0, tm, lc)
    sc = _sel_mod(mod_ref, 4, row0, tm, lc)
    h = _layer_norm(x_ref[0]) * (1.0 + sc) + sh
    logits = _dotp(h, wr_ref[...], 3)
    lane = lax.broadcasted_iota(jnp.int32, logits.shape, 1)
    neg = jnp.float32(-jnp.inf)
    lg = jnp.where(lane < N_EXPERTS, logits, neg)
    m1 = jnp.max(lg, -1, keepdims=True)
    i1 = jnp.min(jnp.where(lg == m1, lane, LANES), -1, keepdims=True)
    lg2 = jnp.where(lane == i1, neg, lg)
    m2 = jnp.max(lg2, -1, keepdims=True)
    i2 = jnp.min(jnp.where(lg2 == m2, lane, LANES), -1, keepdims=True)
    e2 = jnp.exp(m2 - m1)
    return h, lane, i1, i2, 1.0 / (1.0 + e2), e2 / (1.0 + e2)


def _moe_sparse_kernel(x_ref, mod_ref, wr_ref, w1_ref, w3_ref, w2_ref, lg_ref, lb_ref, o_ref,
                       h_scr, acc_scr, gate_scr, posc_scr, posr_scr, xs_scr, ye_scr, nblk_scr, *, lc, tm, nf, ns, blk):
    row0 = pl.program_id(1) * tm
    s = pl.program_id(2)
    e = s // nf
    f = s % nf

    @pl.when(s == 0)
    def _():
        h, lane, i1, i2, g1, g2 = _route(x_ref, mod_ref, wr_ref, row0, tm, lc)
        h_scr[...] = h.astype(BF16)
        acc_scr[...] = jnp.zeros(acc_scr.shape, F32)
        routed = jnp.where(jnp.logical_or(lane == i1, lane == i2), 1.0, 0.0)
        r = lax.broadcasted_iota(jnp.int32, (tm, tm), 0)
        c = lax.broadcasted_iota(jnp.int32, (tm, tm), 1)
        before = jnp.where(c < r, 1.0, 0.0).astype(BF16)
        rank_c = jnp.dot(before, routed.astype(BF16), preferred_element_type=F32)
        gate_scr[...] = jnp.where(lane == i1, g1, 0.0) + jnp.where(lane == i2, g2, 0.0)
        posc_scr[...] = jnp.where(routed > 0.0, rank_c, -1.0)
        routed_t = routed.T[:2 * SUBLANES]
        after = jnp.where(r < c, 1.0, 0.0).astype(BF16)
        rank_r = jnp.dot(routed_t.astype(BF16), after, preferred_element_type=F32)
        posr_scr[...] = jnp.where(routed_t > 0.0, rank_r, -1.0)
        counts = jnp.sum(routed, 0, keepdims=True)
        lane1 = lax.broadcasted_iota(jnp.int32, counts.shape, 1)
        for ex in range(N_EXPERTS):
            n_rows = jnp.sum(jnp.where(lane1 == ex, counts, 0.0)).astype(jnp.int32)
            nblk_scr[ex] = (n_rows + (blk - 1)) // blk

    nblk = nblk_scr[e]
    mv = MOE_MOVE_BLOCK
    nmv = (nblk * blk + (mv - 1)) // mv
    rows = lambda i: pl.ds(pl.multiple_of(i * mv, mv), mv)

    @pl.when(f == 0)
    def _():
        def gather(i, carry):
            slot = (lax.broadcasted_iota(jnp.int32, (mv, tm), 0) + i * mv).astype(F32)
            take = jnp.where(slot == posr_scr[pl.ds(e, 1), :], 1.0, 0.0).astype(BF16)
            xs_scr[rows(i), :] = jnp.dot(take, h_scr[...], preferred_element_type=F32).astype(BF16)
            ye_scr[rows(i), :] = jnp.zeros((mv, ye_scr.shape[1]), F32)
            return carry

        lax.fori_loop(0, nmv, gather, 0)

    def expert(m):
        xs = xs_scr[:m]
        a = jnp.dot(xs, w1_ref[0], preferred_element_type=F32)
        g = jnp.dot(xs, w3_ref[0], preferred_element_type=F32)
        ye_scr[:m] += jnp.dot((_silu(a) * g).astype(BF16), w2_ref[0], preferred_element_type=F32)

    for k in range(tm // blk):
        pl.when(nblk == k + 1)(functools.partial(expert, (k + 1) * blk))

    @pl.when(f == nf - 1)
    def _():
        lane = lax.broadcasted_iota(jnp.int32, (tm, LANES), 1)
        column = lambda ref: jnp.sum(jnp.where(lane == e, ref[...], 0.0), -1, keepdims=True)
        pos = column(posc_scr)
        gate = column(gate_scr)

        def scatter(i, carry):
            slot = (lax.broadcasted_iota(jnp.int32, (tm, mv), 1) + i * mv).astype(F32)
            put = jnp.where(slot == pos, 1.0, 0.0).astype(BF16)
            acc_scr[...] += gate * jnp.dot(put, ye_scr[rows(i), :].astype(BF16), preferred_element_type=F32)
            return carry

        lax.fori_loop(0, nmv, scatter, 0)

    @pl.when(s == ns - 1)
    def _():
        gate = _sel_mod(mod_ref, 5, row0, tm, lc)
        y = ALPHA * x_ref[0] + gate * acc_scr[...]
        o_ref[0] = _layer_norm(y) * lg_ref[...] + lb_ref[...]


def _moe_sparse(xx, mod, wr, w1, w3, w2, lg, lb, lc):
    b, lt, d = xx.shape
    ne, _, ff = w1.shape
    tm = _pick_tile(lt, (1280, 768, 512, 256))
    tf = _pick_tile(ff, (256, 128))
    nf = ff // tf
    ns = ne * nf
    return pl.pallas_call(
        functools.partial(_moe_sparse_kernel, lc=lc, tm=tm, nf=nf, ns=ns, blk=MOE_ROW_BLOCK),
        grid=(b, lt // tm, ns),
        in_specs=[pl.BlockSpec((1, tm, d), lambda bi, i, s: (bi, i, 0)),
                  pl.BlockSpec((1, 2, 6, d), lambda bi, i, s: (bi, 0, 0, 0)),
                  pl.BlockSpec((d, LANES), lambda bi, i, s: (0, 0)),
                  pl.BlockSpec((1, d, tf), lambda bi, i, s: (s // nf, 0, s % nf)),
                  pl.BlockSpec((1, d, tf), lambda bi, i, s: (s // nf, 0, s % nf)),
                  pl.BlockSpec((1, tf, d), lambda bi, i, s: (s // nf, s % nf, 0)),
                  pl.BlockSpec((1, d), lambda bi, i, s: (0, 0)),
                  pl.BlockSpec((1, d), lambda bi, i, s: (0, 0))],
        out_specs=pl.BlockSpec((1, tm, d), lambda bi, i, s: (bi, i, 0)),
        out_shape=jax.ShapeDtypeStruct((b, lt, d), F32),
        scratch_shapes=[pltpu.VMEM((tm, d), BF16), pltpu.VMEM((tm, d), F32),
                        pltpu.VMEM((tm, LANES), F32), pltpu.VMEM((tm, LANES), F32),
                        pltpu.VMEM((2 * SUBLANES, tm), F32),
                        pltpu.VMEM((tm, d), BF16), pltpu.VMEM((tm, d), F32),
                        pltpu.SMEM((ne,), jnp.int32)],
        compiler_params=_cparams(("parallel", "parallel", "arbitrary"), VMEM_LIMIT_MOE),
        name="moe_sparse",
    )(xx, mod, wr, w1, w3, w2, lg, lb)


def _rope_tables(l, lc):
    rows = l // GRID_W
    row = jnp.repeat(jnp.arange(rows, dtype=F32), GRID_W)
    col = jnp.tile(jnp.arange(GRID_W, dtype=F32), rows)
    n_freq = HEAD_DIM // 4
    inv_freq = ROPE_THETA ** (-jnp.arange(n_freq, dtype=F32) / n_freq)
    ang = jnp.concatenate([row[:, None] * inv_freq, col[:, None] * inv_freq], -1)
    cos, sin = jnp.cos(ang), jnp.sin(ang)
    cos2 = jnp.concatenate([jnp.ones((lc, LANES), F32), jnp.concatenate([cos, cos, cos, cos], -1)], 0)
    sin2 = jnp.concatenate([jnp.zeros((lc, LANES), F32), jnp.concatenate([-sin, sin, -sin, sin], -1)], 0)
    return cos2, sin2


def _hy_features(l):
    bands = (HY_EMB - 1) // 2
    t = jnp.linspace(0.0, 1.0, l, dtype=F32)[:, None]
    f = jnp.linspace(1e-4, bands - 1, bands, dtype=F32)[None, :]
    wt = 2.0 * math.pi * jnp.arange(l, dtype=F32)[:, None] / l
    z = jnp.concatenate([t, jnp.cos(f * wt), -jnp.sin(f * wt)], -1)
    return jnp.pad(z, ((0, 0), (0, LANES - HY_EMB)))


def _angle(idx, n):
    return (2.0 * math.pi / n) * (idx % n).astype(F32)


def _dft_tables(n1):
    nh = n1 // 2
    n2 = DFT_N2
    n = n1 * n2
    k1 = jnp.arange(n1, dtype=jnp.int32)
    a1 = _angle(k1[:, None] * jnp.arange(nh, dtype=jnp.int32)[None, :], n1)
    fstack = jnp.concatenate([jnp.cos(a1), -jnp.sin(a1)], 0)
    cstack = jnp.concatenate([jnp.cos(a1.T), jnp.sin(a1.T)], 0)
    k2 = jnp.arange(n2, dtype=jnp.int32)
    at = _angle(k1[:, None] * k2[None, :], n)
    tr, ti = jnp.cos(at), -jnp.sin(at)
    a2 = _angle(k2[:, None] * k2[None, :], n2)
    fr, fi = jnp.cos(a2), -jnp.sin(a2)
    g = jnp.stack([tr[:, None, :] * fr[None] - ti[:, None, :] * fi[None],
                   tr[:, None, :] * fi[None] + ti[:, None, :] * fr[None]], 1)
    return fstack, cstack, g


def _dense_dft_tables(lc):
    n = 2 * lc
    a = _angle(jnp.arange(n, dtype=jnp.int32)[:, None] * jnp.arange(lc, dtype=jnp.int32)[None, :], n)
    ff = jnp.concatenate([jnp.cos(a), -jnp.sin(a)], 0)
    ci = jnp.concatenate([jnp.cos(a.T), jnp.sin(a.T)], 0)
    return ff, ci


def kernel(x, c, ctx, c_ctx, ada_w, ada_b, w_in, w_out, q_gain, k_gain, rwkv_mu, rwkv_w0, rwkv_wB, rwkv_a0, rwkv_aB, rwkv_gB, rwkv_kk, rwkv_ka, rwkv_rk, rwkv_gn_g, rwkv_gn_b, hy_short_w, hy_short_b, hy_w1, hy_b1, hy_freq1, hy_w2, hy_b2, hy_freq2, hy_w3, hy_decay, hy_bias, ln1_g, ln1_b, ln2_g, ln2_b, ffn_w1, ffn_w3, ffn_w2, moe_router, moe_w1, moe_w3, moe_w2):
    b, l, d = x.shape
    lc = ctx.shape[1]
    lt = lc + l
    depth = ada_w.shape[0]
    assert b == 2, "the long convolution packs the two batch rows as one complex signal"
    assert d == D_MODEL and lc % TOK_TILE == 0 and l % TOK_TILE == 0 and (2 * l) % (2 * DFT_N2) == 0
    cw = RWKV_WIDTH

    xx = jnp.concatenate([ctx, x], 1)
    cond8 = jnp.zeros((SUBLANES, d), F32).at[:b].set(c).at[b].set(c_ctx)
    mod_all = _ada_mod(cond8, ada_w, ada_b)

    cos64, sin64 = _rope_tables(l, lc)
    n1 = 2 * l // DFT_N2
    nh = n1 // 2
    cols = DFT_N2 * HY_WIDTH
    fstack, cstack, g_tab = (t.astype(BF16) for t in _dft_tables(n1))
    ff_c, ci_c = _dense_dft_tables(lc)
    feat_l = _hy_features(l)
    feat_c = _hy_features(lc)
    blk = jnp.arange(cw) // HEAD_DIM
    bd = (blk[:, None] == blk[None, :]).astype(F32)
    ch = jnp.arange(ATT_WIDTH)
    bd_att = (ch[:, None] // HEAD_DIM == ch[None, :] // HEAD_DIM).astype(BF16)
    swap_att = (ch[:, None] == (ch[None, :] + HALF_HD) % HEAD_DIM + (ch[None, :] // HEAD_DIM) * HEAD_DIM
                ).astype(BF16)
    perm64 = jnp.concatenate([jnp.arange(0, HEAD_DIM, 2), jnp.arange(1, HEAD_DIM, 2)])
    perm_att = jnp.concatenate([h * HEAD_DIM + perm64 for h in range(ATT_HEADS + ATT_KV_HEADS)]
                               + [jnp.arange(ATT_WIDTH + ATT_KV_WIDTH, IN_ATT)])
    tq = TOK_TILE
    qn = next(n for n in (4, 2, 1) if (l // tq) % n == 0)
    tk = _pick_tile(lt, (3328, 1280, 1024, 768, 512, 256))

    for li in range(depth):
        ml = mod_all[li]
        mod = jnp.stack([jnp.broadcast_to(ml[b].reshape(1, 6, d), (b, 6, d)), ml[:b].reshape(b, 6, d)], 1)
        wi = w_in[li]
        w_pad = jnp.concatenate([wi[:, :IN_ATT][:, perm_att], wi[:, IN_ATT:IN_ATT + IN_RWKV],
                                 jnp.zeros((d, IN_RWKV_PAD - IN_RWKV), F32), wi[:, IN_ATT + IN_RWKV:]],
                                1).astype(BF16)
        ua, ur, uh = _inproj(xx, mod, w_pad, lc)

        two = lambda gain: jnp.tile(gain[perm64], 2)[None]
        qt, kx, vt = _attn_prep(ua, cos64, sin64, two(q_gain[li]), two(k_gain[li]), bd_att, swap_att, lc)
        att_c = _flash(qt, kx, vt, tq, TOK_TILE, 1, l // tq, lc // tq, lc // TOK_TILE)
        s_bound = (HEAD_DIM ** 0.5) * LOG2E * jnp.max(jnp.abs(q_gain[li])) * jnp.max(jnp.abs(k_gain[li]))
        lat_args = (qt, kx, vt, tq, tk, qn, 0, l // (qn * tq), lt // tk)
        att_l = lax.cond(s_bound <= MAX_UNSHIFTED_SCORE,
                         lambda: _flash(*lat_args, bounded=True), lambda: _flash(*lat_args, bounded=False))

        wl = jnp.zeros((LANES, 5 * cw), F32)
        wl = wl.at[0:W_LORA, 0:cw].set(rwkv_wB[li, 0]).at[W_LORA:2 * W_LORA, cw:2 * cw].set(rwkv_wB[li, 1])
        o_a = 2 * W_LORA
        wl = wl.at[o_a:o_a + A_LORA, 2 * cw:3 * cw].set(rwkv_aB[li, 0])
        wl = wl.at[o_a + A_LORA:o_a + 2 * A_LORA, 3 * cw:4 * cw].set(rwkv_aB[li, 1])
        o_g = o_a + 2 * A_LORA
        wl = wl.at[o_g:o_g + G_LORA, 4 * cw:5 * cw].set(rwkv_gB[li])
        mu = jnp.pad(rwkv_mu[li], ((0, 0), (0, IN_RWKV_PAD - IN_RWKV)))
        r_, v_, kk_, g_, bon_, lw_, be_, kd_ = _rwkv_prep(
            ur, mu, wl, bd, rwkv_w0[li], rwkv_a0[li], rwkv_kk[li][None], rwkv_ka[li][None],
            rwkv_rk[li].reshape(1, cw), lc)
        yf, yb = _wkv_scan(r_, v_, kk_, lw_, be_, kd_, lc)
        rw = _rwkv_out(yf, yb, bon_, g_, bd, rwkv_gn_g[li][None], rwkv_gn_b[li][None])

        x1, x2, vv = _hy_prep(uh, hy_short_w[li], hy_short_b[li][None], lc)
        w1p = jnp.pad(hy_w1[li], ((0, LANES - HY_EMB), (0, 0)))
        fargs = (w1p, hy_b1[li][None], hy_freq1[li][None], hy_w2[li], hy_b2[li][None], hy_freq2[li][None],
                 hy_w3[li], hy_decay[li][None])
        fw, bw, nrm = _hy_filter(feat_l, *fargs)
        c2 = HY_ORDER * HY_WIDTH
        a4 = _dft_cols(fstack, fw.reshape(1, nh, DFT_N2 * c2), 0, bw.reshape(1, nh, DFT_N2 * c2), 0, True)
        kspec = _spec(a4.reshape(4, n1, DFT_N2, c2), g_tab, nrm, n1 * DFT_N2)
        lat = lambda t: t[:, lc:].reshape(b, nh, cols)
        x1l, x2l, zin = lat(x1), lat(x2), lat(vv)
        for o, gate in enumerate((x1l, x2l)):
            a = _dft_cols(fstack, zin, 0, zin, 1, False)
            bv = _conv_mid(a.reshape(2, n1, DFT_N2, HY_WIDTH), g_tab, kspec, o)
            bias_cols = jnp.tile(hy_bias[li, o], DFT_N2)[None]
            zin = _idft_cols(cstack, bv.reshape(2, n1, cols), gate, zin, bias_cols)
        hy_l = zin.reshape(b, l, HY_WIDTH)
        fw_c, bw_c, nrm_c = _hy_filter(feat_c, *fargs)
        hy_c = _hy_ctx(x1[:, :lc], x2[:, :lc], vv[:, :lc], fw_c, bw_c, nrm_c, hy_bias[li], ff_c, ci_c)

        xx = _outproj(att_c, att_l, rw, hy_c, hy_l, xx, mod, w_out[li].astype(BF16),
                      ln1_g[li][None], ln1_b[li][None], lc)

        j = li // 2
        if li % 2 == 0:
            xx = _ffn(xx, mod, ffn_w1[j].astype(BF16), ffn_w3[j].astype(BF16), ffn_w2[j].astype(BF16),
                      ln2_g[li][None], ln2_b[li][None], lc)
        else:
            wr = jnp.pad(moe_router[j], ((0, 0), (0, LANES - N_EXPERTS)))
            xx = _moe_sparse(xx, mod, wr, moe_w1[j].astype(BF16), moe_w3[j].astype(BF16), moe_w2[j].astype(BF16),
                             ln2_g[li][None], ln2_b[li][None], lc)
    return xx[:, lc:]
```

```python
import functools
import math

import jax
import jax.numpy as jnp
from jax import lax
from jax.experimental import pallas as pl
from jax.experimental.pallas import tpu as pltpu

F32 = jnp.float32
BF16 = jnp.bfloat16
HI = lax.Precision.HIGHEST

D_MODEL = 1024
DEPTH = 2
GRID_W = 64
HEAD_DIM = 64
HALF_HD = HEAD_DIM // 2
ATT_WIDTH = 512
RWKV_WIDTH = 256
HY_WIDTH = 256
ATT_HEADS = 8
ATT_KV_HEADS = 2
ATT_REP = 4
ATT_KV_WIDTH = 128
ROPE_THETA = 10000.0
QK_EPS = 1e-6
RWKV_HEADS = 4
W_LORA = 16
A_LORA = 16
G_LORA = 32
RWKV_GN_EPS = 64e-5
HY_ORDER = 2
HY_EMB = 33
HY_FFN = 64
N_EXPERTS = 8
LN_EPS = 1e-6
IN_ATT = ATT_WIDTH + 2 * ATT_KV_WIDTH
IN_RWKV = 3 * RWKV_WIDTH + 2 * W_LORA + 2 * A_LORA + G_LORA
IN_RWKV_PAD = 896
IN_HY = 3 * HY_WIDTH
ALPHA = float((2 * DEPTH) ** 0.25)
LOG2E = 1.4426950408889634
MAX_UNSHIFTED_SCORE = 40.0

LANES = 128
SUBLANES = 8
TOK_TILE = 256
ROW_TILE_MID = 640
ROW_TILE_WIDE = 1280
WKV_CHUNK = 64
WKV_TILE = 256
DFT_N2 = 256
MOE_ROW_BLOCK = 128
MOE_MOVE_BLOCK = 256
VMEM_LIMIT = 48 * 1024 * 1024
VMEM_LIMIT_MOE = 56 * 1024 * 1024


def _cparams(sem, vmem=VMEM_LIMIT):
    return pltpu.CompilerParams(dimension_semantics=sem, vmem_limit_bytes=vmem)


def _pick_tile(n, cands):
    for c in cands:
        if n % c == 0:
            return c
    raise ValueError(f"no tile for {n} in {cands}")


def _layer_norm(x):
    mu = jnp.mean(x, -1, keepdims=True)
    xc = x - mu
    var = jnp.mean(xc * xc, -1, keepdims=True)
    return xc * lax.rsqrt(var + LN_EPS)


def _sel_mod(mod_ref, j, row0, tm, lc):
    rid = row0 + lax.broadcasted_iota(jnp.int32, (tm, 1), 0)
    return jnp.where(rid < lc, mod_ref[0, 0, j:j + 1, :], mod_ref[0, 1, j:j + 1, :])


def _silu(x):
    return x * jax.nn.sigmoid(x)


def _ada_kernel(c_ref, w_ref, b_ref, o_ref):
    s = _silu(c_ref[...])
    o_ref[0] = _dotp(s, w_ref[0], 3) + b_ref[0]


def _ada_mod(cond8, ada_w, ada_b):
    depth, d, n = ada_w.shape
    tn = _pick_tile(n, (1536, 1024, 512, 256, 128))
    return pl.pallas_call(
        _ada_kernel,
        grid=(depth, n // tn),
        in_specs=[pl.BlockSpec((SUBLANES, d), lambda l, j: (0, 0)),
                  pl.BlockSpec((1, d, tn), lambda l, j: (l, 0, j)),
                  pl.BlockSpec((1, 1, tn), lambda l, j: (l, 0, j))],
        out_specs=pl.BlockSpec((1, SUBLANES, tn), lambda l, j: (l, 0, j)),
        out_shape=jax.ShapeDtypeStruct((depth, SUBLANES, n), F32),
        compiler_params=_cparams(("parallel", "parallel")),
        name="ada_mod",
    )(cond8, ada_w, ada_b.reshape(depth, 1, n))


def _inproj_kernel(x_ref, mod_ref, w_ref, oa_ref, or_ref, oh_ref, *, lc, tm):
    row0 = pl.program_id(1) * tm
    sh = _sel_mod(mod_ref, 0, row0, tm, lc)
    sc = _sel_mod(mod_ref, 1, row0, tm, lc)
    h = (_layer_norm(x_ref[0]) * (1.0 + sc) + sh).astype(BF16)
    u = jnp.dot(h, w_ref[...], preferred_element_type=F32)
    oa_ref[0] = u[:, :IN_ATT]
    or_ref[0] = u[:, IN_ATT:IN_ATT + IN_RWKV_PAD]
    oh_ref[0] = u[:, IN_ATT + IN_RWKV_PAD:]


def _inproj(xx, mod, w_pad, lc):
    b, lt, d = xx.shape
    tm = _pick_tile(lt, (ROW_TILE_MID, TOK_TILE))
    n = w_pad.shape[1]
    return pl.pallas_call(
        functools.partial(_inproj_kernel, lc=lc, tm=tm),
        grid=(b, lt // tm),
        in_specs=[pl.BlockSpec((1, tm, d), lambda bi, i: (bi, i, 0)),
                  pl.BlockSpec((1, 2, 6, d), lambda bi, i: (bi, 0, 0, 0)),
                  pl.BlockSpec((d, n), lambda bi, i: (0, 0))],
        out_specs=[pl.BlockSpec((1, tm, IN_ATT), lambda bi, i: (bi, i, 0)),
                   pl.BlockSpec((1, tm, IN_RWKV_PAD), lambda bi, i: (bi, i, 0)),
                   pl.BlockSpec((1, tm, IN_HY), lambda bi, i: (bi, i, 0))],
        out_shape=[jax.ShapeDtypeStruct((b, lt, IN_ATT), F32),
                   jax.ShapeDtypeStruct((b, lt, IN_RWKV_PAD), F32),
                   jax.ShapeDtypeStruct((b, lt, IN_HY), F32)],
        compiler_params=_cparams(("parallel", "parallel")),
        name="inproj",
    )(xx, mod, w_pad)


def _attn_prep_kernel(u_ref, cos_ref, sin_ref, qg_ref, kg_ref, bd_ref, sw_ref, qt_ref, k_ref, vt_ref):
    u = u_ref[0]

    def norm_rope(x, g):
        w = x.shape[1]
        tile = lambda t: jnp.concatenate([t] * (w // LANES), -1)
        ms = _dot01(x * x, bd_ref[:w, :w]) * (1.0 / HEAD_DIM)
        xn = x * lax.rsqrt(ms + QK_EPS) * tile(g)
        sw = _dot01(xn, sw_ref[:w, :w])
        return xn * tile(cos_ref[...]) + sw * tile(sin_ref[...])

    q = norm_rope(u[:, :ATT_WIDTH], qg_ref[...]) * (LOG2E * HEAD_DIM ** -0.5)
    qt = q.T
    kx = norm_rope(u[:, ATT_WIDTH:ATT_WIDTH + ATT_KV_WIDTH], kg_ref[...])
    tq = TOK_TILE
    for g in range(ATT_KV_HEADS):
        base = g * ATT_REP * HEAD_DIM
        for t in range(u.shape[0] // tq):
            qt_ref[0, g, t] = jnp.concatenate(
                [qt[base + r * HEAD_DIM:base + (r + 1) * HEAD_DIM, t * tq:(t + 1) * tq] for r in range(ATT_REP)],
                -1).astype(BF16)
        k_ref[0, g] = kx[:, g * HEAD_DIM:(g + 1) * HEAD_DIM].astype(BF16)
    v0 = ATT_WIDTH + ATT_KV_WIDTH
    vt = u[:, v0:v0 + ATT_KV_WIDTH].T
    for g in range(ATT_KV_HEADS):
        vt_ref[0, g] = vt[g * HEAD_DIM:(g + 1) * HEAD_DIM].astype(BF16)


def _attn_prep(ua, cos, sin, qg, kg, bd_att, swap_att, lc):
    b, lt, _ = ua.shape
    tq = TOK_TILE
    tm = _pick_tile(lt, (ROW_TILE_WIDE, ROW_TILE_MID, TOK_TILE))
    return pl.pallas_call(
        _attn_prep_kernel,
        grid=(b, lt // tm),
        in_specs=[pl.BlockSpec((1, tm, IN_ATT), lambda bi, i: (bi, i, 0)),
                  pl.BlockSpec((tm, LANES), lambda bi, i: (i, 0)),
                  pl.BlockSpec((tm, LANES), lambda bi, i: (i, 0)),
                  pl.BlockSpec((1, LANES), lambda bi, i: (0, 0)),
                  pl.BlockSpec((1, LANES), lambda bi, i: (0, 0)),
                  pl.BlockSpec((ATT_WIDTH, ATT_WIDTH), lambda bi, i: (0, 0)),
                  pl.BlockSpec((ATT_WIDTH, ATT_WIDTH), lambda bi, i: (0, 0))],
        out_specs=[pl.BlockSpec((1, ATT_KV_HEADS, tm // tq, HEAD_DIM, ATT_REP * tq),
                                lambda bi, i: (bi, 0, i, 0, 0)),
                   pl.BlockSpec((1, ATT_KV_HEADS, tm, HEAD_DIM), lambda bi, i: (bi, 0, i, 0)),
                   pl.BlockSpec((1, ATT_KV_HEADS, HEAD_DIM, tm), lambda bi, i: (bi, 0, 0, i))],
        out_shape=[jax.ShapeDtypeStruct((b, ATT_KV_HEADS, lt // tq, HEAD_DIM, ATT_REP * tq), BF16),
                   jax.ShapeDtypeStruct((b, ATT_KV_HEADS, lt, HEAD_DIM), BF16),
                   jax.ShapeDtypeStruct((b, ATT_KV_HEADS, HEAD_DIM, lt), BF16)],
        compiler_params=_cparams(("parallel", "parallel")),
        name="attn_prep",
    )(ua, cos, sin, qg, kg, bd_att, swap_att)


def _flash_kernel(qt_refs, k_ref, vt_ref, o_ref, m_scr, l_scr, acc_scr, *, nk, tq, sub, qn):
    j = pl.program_id(3)

    @pl.when(j == 0)
    def _():
        m_scr[...] = jnp.full(m_scr.shape, -jnp.inf, F32)
        l_scr[...] = jnp.zeros(l_scr.shape, F32)
        acc_scr[...] = jnp.zeros(acc_scr.shape, F32)

    qt = jnp.concatenate([r[0, 0, 0] for r in qt_refs], -1)
    nsub = k_ref.shape[2] // sub
    m = m_scr[...]
    l = l_scr[...]
    acc = acc_scr[...]
    scores = lambda c: jnp.dot(k_ref[0, 0, c * sub:(c + 1) * sub, :], qt, preferred_element_type=F32)
    pv = lambda c, p: jnp.dot(vt_ref[0, 0, :, c * sub:(c + 1) * sub], p, preferred_element_type=F32)
    s_next = scores(0)
    pend = None
    for c in range(nsub):
        s = s_next
        if c + 1 < nsub:
            s_next = scores(c + 1)
        if pend is not None:
            acc = pend[0] * acc + pv(c - 1, pend[1])
        m_new = jnp.maximum(m, jnp.max(s, 0, keepdims=True))
        a = jnp.exp2(m - m_new)
        p = jnp.exp2(s - m_new)
        l = a * l + jnp.sum(p, 0, keepdims=True)
        pend = (a, p.astype(BF16))
        m = m_new
    acc = pend[0] * acc + pv(nsub - 1, pend[1])
    m_scr[...] = m
    l_scr[...] = l
    acc_scr[...] = acc

    @pl.when(j == nk - 1)
    def _():
        o = (acc / l).T
        for t in range(qn):
            o_ref[0, t * tq:(t + 1) * tq, :] = jnp.concatenate(
                [o[(t * ATT_REP + r) * tq:(t * ATT_REP + r + 1) * tq] for r in range(ATT_REP)], -1
            ).astype(o_ref.dtype)


def _flash_bounded_kernel(qt_refs, k_ref, vt_ref, o_ref, l_scr, acc_scr, *, nk, tq, sub, qn):
    j = pl.program_id(3)

    @pl.when(j == 0)
    def _():
        l_scr[...] = jnp.zeros(l_scr.shape, F32)
        acc_scr[...] = jnp.zeros(acc_scr.shape, F32)

    qt = jnp.concatenate([r[0, 0, 0] for r in qt_refs], -1)
    nsub = k_ref.shape[2] // sub

    l = l_scr[...]
    acc = acc_scr[...]
    scores = lambda c: jnp.dot(k_ref[0, 0, c * sub:(c + 1) * sub, :], qt, preferred_element_type=F32)
    s_next = scores(0)
    for c in range(nsub):
        s = s_next
        if c + 1 < nsub:
            s_next = scores(c + 1)
        p = jnp.exp2(s)
        l = l + jnp.sum(p.reshape(sub // SUBLANES, SUBLANES, p.shape[1]), 0)
        acc = acc + jnp.dot(vt_ref[0, 0, :, c * sub:(c + 1) * sub], p.astype(BF16), preferred_element_type=F32)
    l_scr[...] = l
    acc_scr[...] = acc

    @pl.when(j == nk - 1)
    def _():
        o = (acc / jnp.sum(l, 0, keepdims=True)).T
        for t in range(qn):
            o_ref[0, t * tq:(t + 1) * tq, :] = jnp.concatenate(
                [o[(t * ATT_REP + r) * tq:(t * ATT_REP + r + 1) * tq] for r in range(ATT_REP)], -1
            ).astype(o_ref.dtype)


def _flash(qt, k, vt, tq, tk, qn, q_tile0, nq, nk, bounded=False):
    b = qt.shape[0]
    lq = nq * qn * tq
    sub = _pick_tile(tk, (256, 128))
    lanes = qn * ATT_REP * tq
    if bounded:
        kern = functools.partial(_flash_bounded_kernel, nk=nk, tq=tq, sub=sub, qn=qn)
        scratch = [pltpu.VMEM((SUBLANES, lanes), F32), pltpu.VMEM((HEAD_DIM, lanes), F32)]
    else:
        kern = functools.partial(_flash_kernel, nk=nk, tq=tq, sub=sub, qn=qn)
        scratch = [pltpu.VMEM((1, lanes), F32), pltpu.VMEM((1, lanes), F32), pltpu.VMEM((HEAD_DIM, lanes), F32)]
    q_spec = lambda t: pl.BlockSpec((1, 1, 1, HEAD_DIM, ATT_REP * tq),
                                    lambda bi, g, i, j: (bi, g, q_tile0 + i * qn + t, 0, 0))
    return pl.pallas_call(
        lambda *refs: kern(refs[:qn], *refs[qn:]),
        grid=(b, ATT_KV_HEADS, nq, nk),
        in_specs=[q_spec(t) for t in range(qn)] + [
                  pl.BlockSpec((1, 1, tk, HEAD_DIM), lambda bi, g, i, j: (bi, g, j, 0)),
                  pl.BlockSpec((1, 1, HEAD_DIM, tk), lambda bi, g, i, j: (bi, g, 0, j))],
        out_specs=pl.BlockSpec((1, qn * tq, ATT_REP * HEAD_DIM), lambda bi, g, i, j: (bi, i, g)),
        out_shape=jax.ShapeDtypeStruct((b, lq, ATT_WIDTH), BF16),
        scratch_shapes=scratch,
        compiler_params=_cparams(("parallel", "parallel", "parallel", "arbitrary")),
        name="flash_bounded" if bounded else "flash",
    )(*([qt] * qn), k, vt)


def _prev_next(u, up8, un8, i, tm, lc, lt):
    rid = lax.broadcasted_iota(jnp.int32, u.shape, 0)
    gid = rid + i * tm
    prev = jnp.where(rid == 0, up8[SUBLANES - 1:SUBLANES], pltpu.roll(u, 1, 0))
    prev = jnp.where(gid == 0, 0.0, jnp.where(gid == lc, 0.0, prev))
    nxt = jnp.where(rid == tm - 1, un8[0:1], pltpu.roll(u, tm - 1, 0))
    nxt = jnp.where(gid == lc - 1, 0.0, jnp.where(gid == lt - 1, 0.0, nxt))
    return prev, nxt


def _halo_specs(tm, c, lt):
    r = tm // SUBLANES
    last = lt // SUBLANES - 1
    return [pl.BlockSpec((1, tm, c), lambda bi, i: (bi, i, 0)),
            pl.BlockSpec((1, SUBLANES, c), lambda bi, i: (bi, jnp.maximum(i * r - 1, 0), 0)),
            pl.BlockSpec((1, SUBLANES, c), lambda bi, i: (bi, jnp.minimum((i + 1) * r, last), 0))]


def _softplus(z):
    return jnp.maximum(z, 0.0) + jnp.log1p(jnp.exp(-jnp.abs(z)))


def _rwkv_prep_kernel(u_ref, up_ref, un_ref, mu_ref, wl_ref, bd_ref, w0_ref, a0_ref, kkw_ref, ka_ref, rk_ref,
                      r_o, v_o, kk_o, g_o, bon_o, lw_o, be_o, kd_o, *, lc, lt, tm):
    i = pl.program_id(1)
    u = u_ref[0]
    prev, nxt = _prev_next(u, up_ref[0], un_ref[0], i, tm, lc, lt)
    us = u + mu_ref[0:1] * (prev - u) + mu_ref[1:2] * (nxt - u)
    c = RWKV_WIDTH
    r = us[:, 0:c]
    k = us[:, c:2 * c]
    v = us[:, 2 * c:3 * c]
    slab = us[:, 3 * c:3 * c + LANES]
    lane = lax.broadcasted_iota(jnp.int32, slab.shape, 1)
    o_a = 2 * W_LORA
    o_g = o_a + 2 * A_LORA
    act = jnp.where(lane < o_a, jnp.tanh(slab),
                    jnp.where(lane < o_g, slab,
                              jnp.where(lane < o_g + G_LORA, jax.nn.sigmoid(slab), 0.0)))
    lo = _dotp(act, wl_ref[...], 3)
    bd = bd_ref[...]
    kk0 = k * kkw_ref[...]
    kk = kk0 * lax.rsqrt(_dot01(kk0 * kk0, bd) + 1e-12)
    r_o[0] = r
    v_o[0] = v
    kk_o[0] = kk
    g_o[0] = lo[:, 4 * c:5 * c]
    bon = jnp.zeros_like(r)
    for d in range(2):
        w_raw = w0_ref[d:d + 1] + lo[:, d * c:(d + 1) * c]
        lw = -jnp.exp(-_softplus(-w_raw) - 0.5)
        a = jax.nn.sigmoid(a0_ref[d:d + 1] + lo[:, (2 + d) * c:(3 + d) * c])
        kd = k * (1.0 + (a - 1.0) * ka_ref[...])
        lw_o[0, d] = lw
        be_o[0, d] = a * kk
        kd_o[0, d] = kd
        bon = bon + r * kd * rk_ref[...]
    bon_o[0] = _dot01(bon, bd) * v


def _rwkv_prep(ur, mu, wl, bd, w0, a0, kkw, ka, rk, lc):
    b, lt, cp = ur.shape
    tm = _pick_tile(lt, (ROW_TILE_MID, TOK_TILE))
    c = RWKV_WIDTH
    full = lambda shape: pl.BlockSpec(shape, lambda bi, i: (0,) * len(shape))
    tok = pl.BlockSpec((1, tm, c), lambda bi, i: (bi, i, 0))
    tok2 = pl.BlockSpec((1, 2, tm, c), lambda bi, i: (bi, 0, i, 0))
    s1 = jax.ShapeDtypeStruct((b, lt, c), F32)
    s2 = jax.ShapeDtypeStruct((b, 2, lt, c), F32)
    return pl.pallas_call(
        functools.partial(_rwkv_prep_kernel, lc=lc, lt=lt, tm=tm),
        grid=(b, lt // tm),
        in_specs=_halo_specs(tm, cp, lt) + [full((2, cp)), full((LANES, 5 * c)), full((c, c)), full((2, c)),
                                            full((2, c)), full((1, c)), full((1, c)), full((1, c))],
        out_specs=[tok, tok, tok, tok, tok, tok2, tok2, tok2],
        out_shape=[s1, s1, s1, s1, s1, s2, s2, s2],
        compiler_params=_cparams(("parallel", "parallel")),
        name="rwkv_prep",
    )(ur, ur, ur, mu, wl, bd, w0, a0, kkw, ka, rk)


def _mm(a, b):
    return jnp.dot(a, b, precision=HI, preferred_element_type=F32)


_NN = ((1,), (0,))
_NT = ((1,), (1,))
_TN = ((0,), (0,))


def _split2(a):
    hi = a.astype(BF16)
    return hi, (a - hi.astype(F32)).astype(BF16)


def _dotp(a, b, passes, dims=_NN):
    if a.ndim == 3:
        dn = (((dims[0][0] + 1,), (dims[1][0] + 1,)), ((0,), (0,)))
    else:
        dn = (dims, ((), ()))
    dg = lambda p, q: lax.dot_general(p, q, dn, preferred_element_type=F32)
    if passes == 1:
        return dg(a.astype(BF16), b.astype(BF16))
    ah, al = _split2(a)
    bh, bl = _split2(b)
    return dg(ah, bh) + dg(ah, bl) + dg(al, bh)


def _dot01(a, ones):
    ah, al = _split2(a)
    o = ones.astype(BF16)
    return jnp.dot(ah, o, preferred_element_type=F32) + jnp.dot(al, o, preferred_element_type=F32)


P_M = 1
P_INV = 1
P_W = 1
P_Z = 1
P_STATE = 3
P_DFT = 1


def _unit_tri_inv(a_mat, row, col, eye, passes):
    t = a_mat.shape[-1]
    eye_f = eye.astype(F32)
    base = SUBLANES
    same = (row // base) == (col // base)
    n1 = -jnp.where(same, a_mat, 0.0)
    n2 = _dotp(n1, n1, passes)
    n4 = _dotp(n2, n2, passes)
    x = _dotp(_dotp(eye_f + n1, eye_f + n2, passes), eye_f + n4, passes)
    m = base
    while m < t:
        off = jnp.logical_and((row // (2 * m)) == (col // (2 * m)), (row // m) != (col // m))
        x = x - _dotp(x, _dotp(jnp.where(off, a_mat, 0.0), x, passes), passes)
        m *= 2
    return x


def _wkv_kernel(rf_ref, vf_ref, kkf_ref, rb_ref, vb_ref, kkb_ref, lwf_ref, bef_ref, kdf_ref,
                lwb_ref, beb_ref, kdb_ref, yf_ref, yb_ref, h_scr, pq_scr, ry_scr):
    t = WKV_CHUNK
    n = HEAD_DIM
    g = WKV_TILE // WKV_CHUNK
    tt = WKV_TILE
    nh = RWKV_HEADS
    nd = g * nh
    nu = 2 * nd
    orders = (list(range(g)), list(range(g - 1, -1, -1)))

    @pl.when(pl.program_id(1) == 0)
    def _():
        h_scr[...] = jnp.zeros(h_scr.shape, F32)
        pq_scr[...] = jnp.zeros(pq_scr.shape, F32)
        ry_scr[...] = jnp.zeros(ry_scr.shape, F32)

    hm = h_scr[...]
    for p in range(g):
        both = lambda ref: jnp.concatenate([ref[d * nd + p * nh:d * nd + (p + 1) * nh] for d in range(2)], 0)
        ry = both(ry_scr)
        pq = both(pq_scr)
        y = _dotp(ry[:, :, :n], hm, P_STATE) + ry[:, :, n:]
        hm = _dotp(pq[:, :, :n], hm, P_STATE) + pq[:, :, n:]
        for d, y_ref in enumerate((yf_ref, yb_ref)):
            c = orders[d][p]
            y_ref[0, c * t:(c + 1) * t, :] = jnp.concatenate([y[d * nh + h] for h in range(nh)], -1)
    h_scr[...] = hm

    row = lax.broadcasted_iota(jnp.int32, (tt, tt), 0)
    col = lax.broadcasted_iota(jnp.int32, (tt, tt), 1)
    same = (row // t) == (col // t)

    def scaled(d, r_ref, v_ref, kk_ref, lw_ref, be_ref, kd_ref):
        tri = jnp.logical_and(same, (row >= col) if d == 0 else (row <= col))
        sums = jnp.concatenate([jnp.where(tri, 1.0, 0.0), jnp.where(same, 1.0, 0.0)], 0).astype(BF16)
        lw = lw_ref[0, 0]
        l1 = lw.astype(BF16)
        rem = lw - l1.astype(F32)
        l2 = rem.astype(BF16)
        l3 = (rem - l2.astype(F32)).astype(BF16)
        cc = (jnp.dot(sums, l1, preferred_element_type=F32) + jnp.dot(sums, l2, preferred_element_type=F32)
              + jnp.dot(sums, l3, preferred_element_type=F32))
        cum = cc[:tt]
        ctot = cc[tt:]
        e_neg = jnp.exp(-cum)
        e_end = jnp.exp(ctot - cum)
        be = be_ref[0, 0]
        kd = kd_ref[0, 0]

        def units(x, rows=t):
            return jnp.stack([x[orders[d][p] * t:orders[d][p] * t + rows, h * n:(h + 1) * n]
                              for p in range(g) for h in range(nh)], 0)

        return dict(kap=units(kk_ref[0] * jnp.exp(cum - lw)), rt=units(r_ref[0] * jnp.exp(cum)), vh=units(v_ref[0]),
                    bet=units(be * e_neg), kdt=units(kd * e_neg), beh=units(be * e_end), kdh=units(kd * e_end),
                    gend=units(jnp.exp(ctot), 1))

    parts = (scaled(0, rf_ref, vf_ref, kkf_ref, lwf_ref, bef_ref, kdf_ref),
             scaled(1, rb_ref, vb_ref, kkb_ref, lwb_ref, beb_ref, kdb_ref))
    cat = lambda key: jnp.concatenate([parts[0][key], parts[1][key]], 0)
    kap, rt, vh = cat("kap"), cat("rt"), cat("vh")
    r64 = lax.broadcasted_iota(jnp.int32, (t, t), 0)
    c64 = lax.broadcasted_iota(jnp.int32, (t, t), 1)
    eye = r64 == c64
    unit = lax.broadcasted_iota(jnp.int32, (nu, t, t), 0)
    ahead = (r64 - c64)[None] * jnp.where(unit < nd, 1, -1)
    strict = ahead > 0
    incl = ahead >= 0
    m = _dotp(jnp.concatenate([kap, rt], 1), jnp.concatenate([cat("bet"), cat("kdt")], 1), P_M, _NT)
    a_mat = jnp.where(strict, m[:, :t, :t], 0.0)
    b_mat = jnp.where(strict, m[:, :t, t:], 0.0)
    ab_r = jnp.concatenate([jnp.where(incl, m[:, t:, :t], 0.0), jnp.where(incl, m[:, t:, t:], 0.0)], 2)
    tinv = _unit_tri_inv(a_mat, r64, c64, eye, P_INV)
    w = _dotp(tinv, jnp.concatenate([kap, _dotp(b_mat, vh, P_W)], 2), P_W)
    z = jnp.concatenate([-w, jnp.concatenate([jnp.zeros((nu, t, n), F32), vh], 2)], 1)
    ry = _dotp(ab_r, z, P_Z)
    ry_scr[:, :, :n] = ry[:, :, :n] + rt
    ry_scr[:, :, n:] = ry[:, :, n:]
    pq = _dotp(jnp.concatenate([cat("beh"), cat("kdh")], 1), z, P_Z, _TN)
    gd = jnp.where(eye, jnp.broadcast_to(cat("gend"), (nu, n, n)), 0.0)
    pq_scr[:, :, :n] = pq[:, :, :n] + gd
    pq_scr[:, :, n:] = pq[:, :, n:]


def _wkv_scan(r, v, kk, lw, be, kd, lc):
    b, lt, c = r.shape
    tt = WKV_TILE
    nt = lt // tt
    ntc = lc // tt
    tiles = (lambda i: i, lambda i: jnp.where(i < ntc, ntc - 1 - i, nt - 1 - (i - ntc)))
    t_in = lambda d: (lambda i: tiles[d](jnp.minimum(i, nt - 1)))
    t_out = lambda d: (lambda i: tiles[d](jnp.maximum(i - 1, 0)))
    one = lambda d: pl.BlockSpec((1, tt, c), lambda bi, i: (bi, t_in(d)(i), 0))
    two = lambda d: pl.BlockSpec((1, 1, tt, c), lambda bi, i: (bi, d, t_in(d)(i), 0))
    out = lambda d: pl.BlockSpec((1, tt, c), lambda bi, i: (bi, t_out(d)(i), 0))
    nu = 2 * (tt // WKV_CHUNK) * RWKV_HEADS
    ysh = jax.ShapeDtypeStruct((b, lt, c), F32)
    return pl.pallas_call(
        _wkv_kernel,
        grid=(b, nt + 1),
        in_specs=[one(0)] * 3 + [one(1)] * 3 + [two(0)] * 3 + [two(1)] * 3,
        out_specs=[out(0), out(1)],
        out_shape=[ysh, ysh],
        scratch_shapes=[pltpu.VMEM((2 * RWKV_HEADS, HEAD_DIM, HEAD_DIM), F32),
                        pltpu.VMEM((nu, HEAD_DIM, 2 * HEAD_DIM), F32),
                        pltpu.VMEM((nu, WKV_CHUNK, 2 * HEAD_DIM), F32)],
        compiler_params=_cparams(("parallel", "arbitrary")),
        name="wkv_scan",
    )(r, v, kk, r, v, kk, lw, be, kd, lw, be, kd)


def _rwkv_out_kernel(yf_ref, yb_ref, bon_ref, g_ref, bd_ref, gg_ref, gb_ref, o_ref):
    y = yf_ref[0] + yb_ref[0] + bon_ref[0]
    bd = bd_ref[...]
    mu = _dot01(y, bd) * (1.0 / HEAD_DIM)
    yc = y - mu
    var = _dot01(yc * yc, bd) * (1.0 / HEAD_DIM)
    yn = yc * lax.rsqrt(var + RWKV_GN_EPS) * gg_ref[...] + gb_ref[...]
    o_ref[0] = (yn * g_ref[0]).astype(o_ref.dtype)


def _rwkv_out(yf, yb, bon, g, bd, gg, gb):
    b, lt, c = yf.shape
    tm = _pick_tile(lt, (ROW_TILE_WIDE, ROW_TILE_MID, TOK_TILE))
    tok = pl.BlockSpec((1, tm, c), lambda bi, i: (bi, i, 0))
    full = lambda shape: pl.BlockSpec(shape, lambda bi, i: (0,) * len(shape))
    return pl.pallas_call(
        _rwkv_out_kernel,
        grid=(b, lt // tm),
        in_specs=[tok, tok, tok, tok, full((c, c)), full((1, c)), full((1, c))],
        out_specs=tok,
        out_shape=jax.ShapeDtypeStruct((b, lt, c), BF16),
        compiler_params=_cparams(("parallel", "parallel")),
        name="rwkv_out",
    )(yf, yb, bon, g, bd, gg, gb)


def _hy_prep_kernel(u_ref, up_ref, un_ref, w_ref, b_ref, x1_o, x2_o, v_o, *, lc, lt, tm):
    i = pl.program_id(1)
    u = u_ref[0]
    prev, nxt = _prev_next(u, up_ref[0], un_ref[0], i, tm, lc, lt)
    y = prev * w_ref[0:1] + u * w_ref[1:2] + nxt * w_ref[2:3] + b_ref[...]
    c = HY_WIDTH
    x1_o[0] = y[:, :c]
    x2_o[0] = y[:, c:2 * c]
    v_o[0] = y[:, 2 * c:]


def _hy_prep(uh, w, bias, lc):
    b, lt, cin = uh.shape
    tm = _pick_tile(lt, (ROW_TILE_WIDE, ROW_TILE_MID, TOK_TILE))
    c = HY_WIDTH
    full = lambda shape: pl.BlockSpec(shape, lambda bi, i: (0,) * len(shape))
    tok = pl.BlockSpec((1, tm, c), lambda bi, i: (bi, i, 0))
    s1 = jax.ShapeDtypeStruct((b, lt, c), F32)
    return pl.pallas_call(
        functools.partial(_hy_prep_kernel, lc=lc, lt=lt, tm=tm),
        grid=(b, lt // tm),
        in_specs=_halo_specs(tm, cin, lt) + [full((3, cin)), full((1, cin))],
        out_specs=[tok, tok, tok],
        out_shape=[s1, s1, s1],
        compiler_params=_cparams(("parallel", "parallel")),
        name="hy_prep",
    )(uh, uh, uh, w, bias)


def _hy_filter_kernel(z_ref, w1_ref, b1_ref, f1_ref, w2_ref, b2_ref, f2_ref, w3_ref, dec_ref,
                      fw_o, bw_o, nrm_o, *, tl):
    i = pl.program_id(0)
    z = z_ref[...]
    h = jnp.sin(f1_ref[...] * (_dotp(z, w1_ref[...], 3) + b1_ref[...]))
    h = jnp.sin(f2_ref[...] * (_dotp(h, w2_ref[...], 3) + b2_ref[...]))
    h = _dotp(h, w3_ref[...], 3) * jnp.exp(-z[:, 0:1] * dec_ref[...])
    c = HY_WIDTH
    fw = jnp.concatenate([h[:, 0:c], h[:, 2 * c:3 * c]], 1)
    bw = jnp.concatenate([h[:, c:2 * c], h[:, 3 * c:4 * c]], 1)
    rid = i * tl + lax.broadcasted_iota(jnp.int32, (tl, 1), 0)
    bw = jnp.where(rid == 0, 0.0, bw)
    fw_o[...] = fw
    bw_o[...] = bw

    @pl.when(i == 0)
    def _():
        nrm_o[...] = jnp.zeros(nrm_o.shape, F32)

    nrm_o[...] += jnp.sum(jnp.abs(fw) + jnp.abs(bw), 0, keepdims=True)


def _hy_filter(feat, w1p, b1, f1, w2, b2, f2, w3, dec):
    l, fe = feat.shape
    tl = _pick_tile(l, (512, 256))
    c2 = HY_ORDER * HY_WIDTH
    full = lambda shape: pl.BlockSpec(shape, lambda i: (0,) * len(shape))
    return pl.pallas_call(
        functools.partial(_hy_filter_kernel, tl=tl),
        grid=(l // tl,),
        in_specs=[pl.BlockSpec((tl, fe), lambda i: (i, 0)), full(w1p.shape), full(b1.shape), full(f1.shape),
                  full(w2.shape), full(b2.shape), full(f2.shape), full(w3.shape), full(dec.shape)],
        out_specs=[pl.BlockSpec((tl, c2), lambda i: (i, 0)), pl.BlockSpec((tl, c2), lambda i: (i, 0)),
                   pl.BlockSpec((1, c2), lambda i: (0, 0))],
        out_shape=[jax.ShapeDtypeStruct((l, c2), F32), jax.ShapeDtypeStruct((l, c2), F32),
                   jax.ShapeDtypeStruct((1, c2), F32)],
        compiler_params=_cparams(("arbitrary",)),
        name="hy_filter",
    )(feat, w1p, b1, f1, w2, b2, f2, w3, dec)


def _dft_cols_kernel(f_ref, xa_ref, xb_ref, o_ref, *, n1, pair):
    f = f_ref[...]
    pa = _dotp(f, xa_ref[0], P_DFT)
    pb = _dotp(f, xb_ref[0], P_DFT)
    if pair:
        o_ref[0] = pa[:n1].astype(o_ref.dtype)
        o_ref[1] = pa[n1:].astype(o_ref.dtype)
        o_ref[2] = pb[:n1].astype(o_ref.dtype)
        o_ref[3] = pb[n1:].astype(o_ref.dtype)
    else:
        o_ref[0] = (pa[:n1] - pb[n1:]).astype(o_ref.dtype)
        o_ref[1] = (pb[:n1] + pa[n1:]).astype(o_ref.dtype)


def _dft_cols(fstack, xa, ia, xb, ib, pair):
    n1 = fstack.shape[0] // 2
    _, nh, cols = xa.shape
    tc = _pick_tile(cols, (4096, 2048, 1024, 512, 256, 128))
    no = 4 if pair else 2
    return pl.pallas_call(
        functools.partial(_dft_cols_kernel, n1=n1, pair=pair),
        grid=(cols // tc,),
        in_specs=[pl.BlockSpec(fstack.shape, lambda j: (0, 0)),
                  pl.BlockSpec((1, nh, tc), lambda j: (ia, 0, j)),
                  pl.BlockSpec((1, nh, tc), lambda j: (ib, 0, j))],
        out_specs=pl.BlockSpec((no, n1, tc), lambda j: (0, 0, j)),
        out_shape=jax.ShapeDtypeStruct((no, n1, cols), BF16),
        compiler_params=_cparams(("parallel",)),
        name="dft_cols",
    )(fstack, xa, xb)


def _cplx_left(gs, zr, zi, n):
    c = zr.shape[1]
    p = _dotp(gs, jnp.concatenate([zr, zi], 1), P_DFT)
    return p[:n, :c] - p[n:, c:], p[:n, c:] + p[n:, :c]


def _spec_kernel(a_ref, g_ref, nrm_ref, o_ref, *, n_total, kp):
    n2 = DFT_N2
    s = 1.0 / (nrm_ref[...] * n_total)
    for q in range(kp):
        gs = jnp.concatenate([g_ref[q, 0], g_ref[q, 1]], 0)
        fr, fi = _cplx_left(gs, a_ref[0, q], a_ref[1, q], n2)
        br, bi = _cplx_left(gs, a_ref[2, q], a_ref[3, q], n2)
        o_ref[0, q] = ((fr + br) * s).astype(o_ref.dtype)
        o_ref[1, q] = ((fi - bi) * s).astype(o_ref.dtype)


def _planes_per_step(n1):
    return next(k for k in (4, 2, 1) if n1 % k == 0)


def _spec(a4, g, nrm, n_total):
    _, n1, n2, c2 = a4.shape
    kp = _planes_per_step(n1)
    return pl.pallas_call(
        functools.partial(_spec_kernel, n_total=float(n_total), kp=kp),
        grid=(n1 // kp,),
        in_specs=[pl.BlockSpec((4, kp, n2, c2), lambda k: (0, k, 0, 0)),
                  pl.BlockSpec((kp, 2, n2, n2), lambda k: (k, 0, 0, 0)),
                  pl.BlockSpec((1, c2), lambda k: (0, 0))],
        out_specs=pl.BlockSpec((2, kp, n2, c2), lambda k: (0, k, 0, 0)),
        out_shape=jax.ShapeDtypeStruct((2, n1, n2, c2), BF16),
        compiler_params=_cparams(("parallel",)),
        name="hy_spec",
    )(a4, g, nrm)


def _conv_mid_kernel(a_ref, g_ref, k_ref, o_ref, *, kp):
    n2 = DFT_N2
    for q in range(kp):
        gs = jnp.concatenate([g_ref[q, 0], g_ref[q, 1]], 0)
        xr, xi = _cplx_left(gs, a_ref[0, q], a_ref[1, q], n2)
        c = xr.shape[1]
        kr = k_ref[0, q].astype(F32)
        ki = k_ref[1, q].astype(F32)
        zr = xr * kr - xi * ki
        zi = xr * ki + xi * kr
        zst = jnp.concatenate([jnp.concatenate([zr, zi], 1), jnp.concatenate([zi, -zr], 1)], 0)
        y = _dotp(gs, zst, P_DFT, _TN)
        o_ref[0, q] = y[:, :c].astype(o_ref.dtype)
        o_ref[1, q] = y[:, c:].astype(o_ref.dtype)


def _conv_mid(a, g, kspec, order):
    _, n1, n2, c = a.shape
    kp = _planes_per_step(n1)
    return pl.pallas_call(
        functools.partial(_conv_mid_kernel, kp=kp),
        grid=(n1 // kp,),
        in_specs=[pl.BlockSpec((2, kp, n2, c), lambda k: (0, k, 0, 0)),
                  pl.BlockSpec((kp, 2, n2, n2), lambda k: (k, 0, 0, 0)),
                  pl.BlockSpec((2, kp, n2, c), lambda k: (0, k, 0, order))],
        out_specs=pl.BlockSpec((2, kp, n2, c), lambda k: (0, k, 0, 0)),
        out_shape=jax.ShapeDtypeStruct((2, n1, n2, c), BF16),
        compiler_params=_cparams(("parallel",)),
        name="hy_conv_mid",
    )(a, g, kspec)


def _idft_cols_kernel(c_ref, b_ref, g0_ref, g1_ref, x0_ref, x1_ref, bias_ref, o_ref, *, nh):
    cs = c_ref[...]
    pr = _dotp(cs, b_ref[0], P_DFT)
    pi = _dotp(cs, b_ref[1], P_DFT)
    yr = pr[:nh] - pi[nh:]
    yi = pi[:nh] + pr[nh:]
    bias = bias_ref[...]
    o_ref[0] = g0_ref[0] * (yr + x0_ref[0] * bias)
    o_ref[1] = g1_ref[0] * (yi + x1_ref[0] * bias)


def _idft_cols(cstack, bv, gate, xin, bias_cols):
    nh2, n1 = cstack.shape
    nh = nh2 // 2
    cols = bv.shape[-1]
    tc = _pick_tile(cols, (4096, 2048, 1024, 512, 256, 128))
    row = lambda bi: pl.BlockSpec((1, nh, tc), lambda j: (bi, 0, j))
    return pl.pallas_call(
        functools.partial(_idft_cols_kernel, nh=nh),
        grid=(cols // tc,),
        in_specs=[pl.BlockSpec((nh2, n1), lambda j: (0, 0)),
                  pl.BlockSpec((2, n1, tc), lambda j: (0, 0, j)),
                  row(0), row(1), row(0), row(1),
                  pl.BlockSpec((1, tc), lambda j: (0, j))],
        out_specs=pl.BlockSpec((2, nh, tc), lambda j: (0, 0, j)),
        out_shape=jax.ShapeDtypeStruct((2, nh, cols), F32),
        compiler_params=_cparams(("parallel",)),
        name="idft_cols",
    )(cstack, bv, gate, gate, xin, xin, bias_cols)


def _hy_ctx_kernel(x1_ref, x2_ref, v_ref, fw_ref, bw_ref, nrm_ref, bias_ref, ff_ref, ci_ref, o_ref, *, lc):
    n = 2 * lc
    c = HY_WIDTH
    ff = ff_ref[...]
    ci = ci_ref[...]
    pf = _mm(ff, fw_ref[...])
    pb = _mm(ff, bw_ref[...])
    s = 1.0 / (nrm_ref[...] * float(n))
    kr = (pf[:n] + pb[:n]) * s
    ki = (pf[n:] - pb[n:]) * s

    def conv(z0, z1, o):
        xr, xi = _cplx_left(ff, z0, z1, n)
        krr = kr[:, o * c:(o + 1) * c]
        kii = ki[:, o * c:(o + 1) * c]
        return _cplx_left(ci, xr * krr - xi * kii, xr * kii + xi * krr, lc)

    v0 = v_ref[0]
    v1 = v_ref[1]
    y0, y1 = conv(v0, v1, 0)
    z0 = x1_ref[0] * (y0 + v0 * bias_ref[0:1])
    z1 = x1_ref[1] * (y1 + v1 * bias_ref[0:1])
    y0, y1 = conv(z0, z1, 1)
    o_ref[0] = x2_ref[0] * (y0 + z0 * bias_ref[1:2])
    o_ref[1] = x2_ref[1] * (y1 + z1 * bias_ref[1:2])


def _hy_ctx(x1, x2, v, fw, bw, nrm, bias, ff, ci):
    b, lc, c = v.shape
    vm = pl.BlockSpec(memory_space=pltpu.VMEM)
    return pl.pallas_call(
        functools.partial(_hy_ctx_kernel, lc=lc),
        in_specs=[vm] * 9,
        out_specs=vm,
        out_shape=jax.ShapeDtypeStruct((b, lc, c), F32),
        compiler_params=pltpu.CompilerParams(vmem_limit_bytes=VMEM_LIMIT),
        name="hy_ctx",
    )(x1, x2, v, fw, bw, nrm, bias, ff, ci)


def _outproj_kernel(attc_ref, attl_ref, rw_ref, hyc_ref, hyl_ref, x_ref, mod_ref, w_ref, lg_ref, lb_ref, o_ref,
                    *, lc, tm):
    row0 = pl.program_id(1) * tm
    a0 = ATT_WIDTH
    a1 = ATT_WIDTH + RWKV_WIDTH
    is_ctx = row0 < lc
    att = jnp.where(is_ctx, attc_ref[0], attl_ref[0])
    hy = jnp.where(is_ctx, hyc_ref[0], hyl_ref[0])
    o = jnp.dot(att, w_ref[:a0], preferred_element_type=F32)
    o += jnp.dot(rw_ref[0], w_ref[a0:a1], preferred_element_type=F32)
    o += jnp.dot(hy.astype(BF16), w_ref[a1:], preferred_element_type=F32)
    g = _sel_mod(mod_ref, 2, row0, tm, lc)
    y = ALPHA * x_ref[0] + g * o
    o_ref[0] = _layer_norm(y) * lg_ref[...] + lb_ref[...]


def _outproj(att_c, att_l, rw, hy_c, hy_l, xx, mod, w, lg, lb, lc):
    b, lt, d = xx.shape
    tm = TOK_TILE
    nct = lc // tm
    tok = lambda c: pl.BlockSpec((1, tm, c), lambda bi, i: (bi, i, 0))
    ctx = lambda c: pl.BlockSpec((1, tm, c), lambda bi, i: (bi, jnp.minimum(i, nct - 1), 0))
    lat = lambda c: pl.BlockSpec((1, tm, c), lambda bi, i: (bi, jnp.maximum(i - nct, 0), 0))
    full = lambda shape: pl.BlockSpec(shape, lambda bi, i: (0,) * len(shape))
    return pl.pallas_call(
        functools.partial(_outproj_kernel, lc=lc, tm=tm),
        grid=(b, lt // tm),
        in_specs=[ctx(ATT_WIDTH), lat(ATT_WIDTH), tok(RWKV_WIDTH), ctx(HY_WIDTH), lat(HY_WIDTH), tok(d),
                  pl.BlockSpec((1, 2, 6, d), lambda bi, i: (bi, 0, 0, 0)),
                  full(w.shape), full((1, d)), full((1, d))],
        out_specs=tok(d),
        out_shape=jax.ShapeDtypeStruct((b, lt, d), F32),
        compiler_params=_cparams(("parallel", "parallel")),
        name="outproj",
    )(att_c, att_l, rw, hy_c, hy_l, xx, mod, w, lg, lb)


def _ffn_kernel(x_ref, mod_ref, w1_ref, w3_ref, w2_ref, lg_ref, lb_ref, o_ref, h_scr, acc_scr, *, lc, tm, nf):
    row0 = pl.program_id(1) * tm
    f = pl.program_id(2)

    @pl.when(f == 0)
    def _():
        sh = _sel_mod(mod_ref, 3, row0, tm, lc)
        sc = _sel_mod(mod_ref, 4, row0, tm, lc)
        h_scr[...] = (_layer_norm(x_ref[0]) * (1.0 + sc) + sh).astype(BF16)
        acc_scr[...] = jnp.zeros(acc_scr.shape, F32)

    h = h_scr[...]
    a = jnp.dot(h, w1_ref[...], preferred_element_type=F32)
    g = jnp.dot(h, w3_ref[...], preferred_element_type=F32)
    acc_scr[...] += jnp.dot((_silu(a) * g).astype(BF16), w2_ref[...], preferred_element_type=F32)

    @pl.when(f == nf - 1)
    def _():
        gate = _sel_mod(mod_ref, 5, row0, tm, lc)
        y = ALPHA * x_ref[0] + gate * acc_scr[...]
        o_ref[0] = _layer_norm(y) * lg_ref[...] + lb_ref[...]


def _ffn(xx, mod, w1, w3, w2, lg, lb, lc):
    b, lt, d = xx.shape
    ff = w1.shape[1]
    tm = _pick_tile(lt, (1280, 768, 512, 256))
    tf = _pick_tile(ff, (256, 128))
    nf = ff // tf
    return pl.pallas_call(
        functools.partial(_ffn_kernel, lc=lc, tm=tm, nf=nf),
        grid=(b, lt // tm, nf),
        in_specs=[pl.BlockSpec((1, tm, d), lambda bi, i, f: (bi, i, 0)),
                  pl.BlockSpec((1, 2, 6, d), lambda bi, i, f: (bi, 0, 0, 0)),
                  pl.BlockSpec((d, tf), lambda bi, i, f: (0, f)),
                  pl.BlockSpec((d, tf), lambda bi, i, f: (0, f)),
                  pl.BlockSpec((tf, d), lambda bi, i, f: (f, 0)),
                  pl.BlockSpec((1, d), lambda bi, i, f: (0, 0)),
                  pl.BlockSpec((1, d), lambda bi, i, f: (0, 0))],
        out_specs=pl.BlockSpec((1, tm, d), lambda bi, i, f: (bi, i, 0)),
        out_shape=jax.ShapeDtypeStruct((b, lt, d), F32),
        scratch_shapes=[pltpu.VMEM((tm, d), BF16), pltpu.VMEM((tm, d), F32)],
        compiler_params=_cparams(("parallel", "parallel", "arbitrary")),
        name="ffn",
    )(xx, mod, w1, w3, w2, lg, lb)


def _route(x_ref, mod_ref, wr_ref, row0, tm, lc):
    sh = _sel_mod(mod_ref, 3, row0, tm, lc)
    sc = _sel_mod(mod_ref, 4, row0, tm, lc)
    h = _layer_norm(x_ref[0]) * (1.0 + sc) + sh
    logits = _dotp(h, wr_ref[...], 3)
    lane = lax.broadcasted_iota(jnp.int32, logits.shape, 1)
    neg = jnp.float32(-jnp.inf)
    lg = jnp.where(lane < N_EXPERTS, logits, neg)
    m1 = jnp.max(lg, -1, keepdims=True)
    i1 = jnp.min(jnp.where(lg == m1, lane, LANES), -1, keepdims=True)
    lg2 = jnp.where(lane == i1, neg, lg)
    m2 = jnp.max(lg2, -1, keepdims=True)
    i2 = jnp.min(jnp.where(lg2 == m2, lane, LANES), -1, keepdims=True)
    e2 = jnp.exp(m2 - m1)
    return h, lane, i1, i2, 1.0 / (1.0 + e2), e2 / (1.0 + e2)


def _moe_sparse_kernel(x_ref, mod_ref, wr_ref, w1_ref, w3_ref, w2_ref, lg_ref, lb_ref, o_ref,
                       h_scr, acc_scr, gate_scr, posc_scr, posr_scr, xs_scr, ye_scr, nblk_scr, *, lc, tm, nf, ns, blk):
    row0 = pl.program_id(1) * tm
    s = pl.program_id(2)
    e = s // nf
    f = s % nf

    @pl.when(s == 0)
    def _():
        h, lane, i1, i2, g1, g2 = _route(x_ref, mod_ref, wr_ref, row0, tm, lc)
        h_scr[...] = h.astype(BF16)
        acc_scr[...] = jnp.zeros(acc_scr.shape, F32)
        routed = jnp.where(jnp.logical_or(lane == i1, lane == i2), 1.0, 0.0)
        r = lax.broadcasted_iota(jnp.int32, (tm, tm), 0)
        c = lax.broadcasted_iota(jnp.int32, (tm, tm), 1)
        before = jnp.where(c < r, 1.0, 0.0).astype(BF16)
        rank_c = jnp.dot(before, routed.astype(BF16), preferred_element_type=F32)
        gate_scr[...] = jnp.where(lane == i1, g1, 0.0) + jnp.where(lane == i2, g2, 0.0)
        posc_scr[...] = jnp.where(routed > 0.0, rank_c, -1.0)
        routed_t = routed.T[:2 * SUBLANES]
        after = jnp.where(r < c, 1.0, 0.0).astype(BF16)
        rank_r = jnp.dot(routed_t.astype(BF16), after, preferred_element_type=F32)
        posr_scr[...] = jnp.where(routed_t > 0.0, rank_r, -1.0)
        counts = jnp.sum(routed, 0, keepdims=True)
        lane1 = lax.broadcasted_iota(jnp.int32, counts.shape, 1)
        for ex in range(N_EXPERTS):
            n_rows = jnp.sum(jnp.where(lane1 == ex, counts, 0.0)).astype(jnp.int32)
            nblk_scr[ex] = (n_rows + (blk - 1)) // blk

    nblk = nblk_scr[e]
    mv = MOE_MOVE_BLOCK
    nmv = (nblk * blk + (mv - 1)) // mv
    rows = lambda i: pl.ds(pl.multiple_of(i * mv, mv), mv)

    @pl.when(f == 0)
    def _():
        def gather(i, carry):
            slot = (lax.broadcasted_iota(jnp.int32, (mv, tm), 0) + i * mv).astype(F32)
            take = jnp.where(slot == posr_scr[pl.ds(e, 1), :], 1.0, 0.0).astype(BF16)
            xs_scr[rows(i), :] = jnp.dot(take, h_scr[...], preferred_element_type=F32).astype(BF16)
            ye_scr[rows(i), :] = jnp.zeros((mv, ye_scr.shape[1]), F32)
            return carry

        lax.fori_loop(0, nmv, gather, 0)

    def expert(m):
        xs = xs_scr[:m]
        a = jnp.dot(xs, w1_ref[0], preferred_element_type=F32)
        g = jnp.dot(xs, w3_ref[0], preferred_element_type=F32)
        ye_scr[:m] += jnp.dot((_silu(a) * g).astype(BF16), w2_ref[0], preferred_element_type=F32)

    for k in range(tm // blk):
        pl.when(nblk == k + 1)(functools.partial(expert, (k + 1) * blk))

    @pl.when(f == nf - 1)
    def _():
        lane = lax.broadcasted_iota(jnp.int32, (tm, LANES), 1)
        column = lambda ref: jnp.sum(jnp.where(lane == e, ref[...], 0.0), -1, keepdims=True)
        pos = column(posc_scr)
        gate = column(gate_scr)

        def scatter(i, carry):
            slot = (lax.broadcasted_iota(jnp.int32, (tm, mv), 1) + i * mv).astype(F32)
            put = jnp.where(slot == pos, 1.0, 0.0).astype(BF16)
            acc_scr[...] += gate * jnp.dot(put, ye_scr[rows(i), :].astype(BF16), preferred_element_type=F32)
            return carry

        lax.fori_loop(0, nmv, scatter, 0)

    @pl.when(s == ns - 1)
    def _():
        gate = _sel_mod(mod_ref, 5, row0, tm, lc)
        y = ALPHA * x_ref[0] + gate * acc_scr[...]
        o_ref[0] = _layer_norm(y) * lg_ref[...] + lb_ref[...]


def _moe_sparse(xx, mod, wr, w1, w3, w2, lg, lb, lc):
    b, lt, d = xx.shape
    ne, _, ff = w1.shape
    tm = _pick_tile(lt, (1280, 768, 512, 256))
    tf = _pick_tile(ff, (256, 128))
    nf = ff // tf
    ns = ne * nf
    return pl.pallas_call(
        functools.partial(_moe_sparse_kernel, lc=lc, tm=tm, nf=nf, ns=ns, blk=MOE_ROW_BLOCK),
        grid=(b, lt // tm, ns),
        in_specs=[pl.BlockSpec((1, tm, d), lambda bi, i, s: (bi, i, 0)),
                  pl.BlockSpec((1, 2, 6, d), lambda bi, i, s: (bi, 0, 0, 0)),
                  pl.BlockSpec((d, LANES), lambda bi, i, s: (0, 0)),
                  pl.BlockSpec((1, d, tf), lambda bi, i, s: (s // nf, 0, s % nf)),
                  pl.BlockSpec((1, d, tf), lambda bi, i, s: (s // nf, 0, s % nf)),
                  pl.BlockSpec((1, tf, d), lambda bi, i, s: (s // nf, s % nf, 0)),
                  pl.BlockSpec((1, d), lambda bi, i, s: (0, 0)),
                  pl.BlockSpec((1, d), lambda bi, i, s: (0, 0))],
        out_specs=pl.BlockSpec((1, tm, d), lambda bi, i, s: (bi, i, 0)),
        out_shape=jax.ShapeDtypeStruct((b, lt, d), F32),
        scratch_shapes=[pltpu.VMEM((tm, d), BF16), pltpu.VMEM((tm, d), F32),
                        pltpu.VMEM((tm, LANES), F32), pltpu.VMEM((tm, LANES), F32),
                        pltpu.VMEM((2 * SUBLANES, tm), F32),
                        pltpu.VMEM((tm, d), BF16), pltpu.VMEM((tm, d), F32),
                        pltpu.SMEM((ne,), jnp.int32)],
        compiler_params=_cparams(("parallel", "parallel", "arbitrary"), VMEM_LIMIT_MOE),
        name="moe_sparse",
    )(xx, mod, wr, w1, w3, w2, lg, lb)


def _rope_tables(l, lc):
    rows = l // GRID_W
    row = jnp.repeat(jnp.arange(rows, dtype=F32), GRID_W)
    col = jnp.tile(jnp.arange(GRID_W, dtype=F32), rows)
    n_freq = HEAD_DIM // 4
    inv_freq = ROPE_THETA ** (-jnp.arange(n_freq, dtype=F32) / n_freq)
    ang = jnp.concatenate([row[:, None] * inv_freq, col[:, None] * inv_freq], -1)
    cos, sin = jnp.cos(ang), jnp.sin(ang)
    cos2 = jnp.concatenate([jnp.ones((lc, LANES), F32), jnp.concatenate([cos, cos, cos, cos], -1)], 0)
    sin2 = jnp.concatenate([jnp.zeros((lc, LANES), F32), jnp.concatenate([-sin, sin, -sin, sin], -1)], 0)
    return cos2, sin2


def _hy_features(l):
    bands = (HY_EMB - 1) // 2
    t = jnp.linspace(0.0, 1.0, l, dtype=F32)[:, None]
    f = jnp.linspace(1e-4, bands - 1, bands, dtype=F32)[None, :]
    wt = 2.0 * math.pi * jnp.arange(l, dtype=F32)[:, None] / l
    z = jnp.concatenate([t, jnp.cos(f * wt), -jnp.sin(f * wt)], -1)
    return jnp.pad(z, ((0, 0), (0, LANES - HY_EMB)))


def _angle(idx, n):
    return (2.0 * math.pi / n) * (idx % n).astype(F32)


def _dft_tables(n1):
    nh = n1 // 2
    n2 = DFT_N2
    n = n1 * n2
    k1 = jnp.arange(n1, dtype=jnp.int32)
    a1 = _angle(k1[:, None] * jnp.arange(nh, dtype=jnp.int32)[None, :], n1)
    fstack = jnp.concatenate([jnp.cos(a1), -jnp.sin(a1)], 0)
    cstack = jnp.concatenate([jnp.cos(a1.T), jnp.sin(a1.T)], 0)
    k2 = jnp.arange(n2, dtype=jnp.int32)
    at = _angle(k1[:, None] * k2[None, :], n)
    tr, ti = jnp.cos(at), -jnp.sin(at)
    a2 = _angle(k2[:, None] * k2[None, :], n2)
    fr, fi = jnp.cos(a2), -jnp.sin(a2)
    g = jnp.stack([tr[:, None, :] * fr[None] - ti[:, None, :] * fi[None],
                   tr[:, None, :] * fi[None] + ti[:, None, :] * fr[None]], 1)
    return fstack, cstack, g


def _dense_dft_tables(lc):
    n = 2 * lc
    a = _angle(jnp.arange(n, dtype=jnp.int32)[:, None] * jnp.arange(lc, dtype=jnp.int32)[None, :], n)
    ff = jnp.concatenate([jnp.cos(a), -jnp.sin(a)], 0)
    ci = jnp.concatenate([jnp.cos(a.T), jnp.sin(a.T)], 0)
    return ff, ci


def kernel(x, c, ctx, c_ctx, ada_w, ada_b, w_in, w_out, q_gain, k_gain, rwkv_mu, rwkv_w0, rwkv_wB, rwkv_a0, rwkv_aB, rwkv_gB, rwkv_kk, rwkv_ka, rwkv_rk, rwkv_gn_g, rwkv_gn_b, hy_short_w, hy_short_b, hy_w1, hy_b1, hy_freq1, hy_w2, hy_b2, hy_freq2, hy_w3, hy_decay, hy_bias, ln1_g, ln1_b, ln2_g, ln2_b, ffn_w1, ffn_w3, ffn_w2, moe_router, moe_w1, moe_w3, moe_w2):
    b, l, d = x.shape
    lc = ctx.shape[1]
    lt = lc + l
    depth = ada_w.shape[0]
    assert b == 2, "the long convolution packs the two batch rows as one complex signal"
    assert d == D_MODEL and lc % TOK_TILE == 0 and l % TOK_TILE == 0 and (2 * l) % (2 * DFT_N2) == 0
    cw = RWKV_WIDTH

    xx = jnp.concatenate([ctx, x], 1)
    cond8 = jnp.zeros((SUBLANES, d), F32).at[:b].set(c).at[b].set(c_ctx)
    mod_all = _ada_mod(cond8, ada_w, ada_b)

    cos64, sin64 = _rope_tables(l, lc)
    n1 = 2 * l // DFT_N2
    nh = n1 // 2
    cols = DFT_N2 * HY_WIDTH
    fstack, cstack, g_tab = (t.astype(BF16) for t in _dft_tables(n1))
    ff_c, ci_c = _dense_dft_tables(lc)
    feat_l = _hy_features(l)
    feat_c = _hy_features(lc)
    blk = jnp.arange(cw) // HEAD_DIM
    bd = (blk[:, None] == blk[None, :]).astype(F32)
    ch = jnp.arange(ATT_WIDTH)
    bd_att = (ch[:, None] // HEAD_DIM == ch[None, :] // HEAD_DIM).astype(BF16)
    swap_att = (ch[:, None] == (ch[None, :] + HALF_HD) % HEAD_DIM + (ch[None, :] // HEAD_DIM) * HEAD_DIM
                ).astype(BF16)
    perm64 = jnp.concatenate([jnp.arange(0, HEAD_DIM, 2), jnp.arange(1, HEAD_DIM, 2)])
    perm_att = jnp.concatenate([h * HEAD_DIM + perm64 for h in range(ATT_HEADS + ATT_KV_HEADS)]
                               + [jnp.arange(ATT_WIDTH + ATT_KV_WIDTH, IN_ATT)])
    tq = TOK_TILE
    qn = next(n for n in (4, 2, 1) if (l // tq) % n == 0)
    tk = _pick_tile(lt, (3328, 1280, 1024, 768, 512, 256))

    for li in range(depth):
        ml = mod_all[li]
        mod = jnp.stack([jnp.broadcast_to(ml[b].reshape(1, 6, d), (b, 6, d)), ml[:b].reshape(b, 6, d)], 1)
        wi = w_in[li]
        w_pad = jnp.concatenate([wi[:, :IN_ATT][:, perm_att], wi[:, IN_ATT:IN_ATT + IN_RWKV],
                                 jnp.zeros((d, IN_RWKV_PAD - IN_RWKV), F32), wi[:, IN_ATT + IN_RWKV:]],
                                1).astype(BF16)
        ua, ur, uh = _inproj(xx, mod, w_pad, lc)

        two = lambda gain: jnp.tile(gain[perm64], 2)[None]
        qt, kx, vt = _attn_prep(ua, cos64, sin64, two(q_gain[li]), two(k_gain[li]), bd_att, swap_att, lc)
        att_c = _flash(qt, kx, vt, tq, TOK_TILE, 1, 0, lc // tq, lc // TOK_TILE)
        s_bound = (HEAD_DIM ** 0.5) * LOG2E * jnp.max(jnp.abs(q_gain[li])) * jnp.max(jnp.abs(k_gain[li]))
        lat_args = (qt, kx, vt, tq, tk, qn, lc // tq, l // (qn * tq), lt // tk)
        att_l = lax.cond(s_bound <= MAX_UNSHIFTED_SCORE,
                         lambda: _flash(*lat_args, bounded=True), lambda: _flash(*lat_args, bounded=False))

        wl = jnp.zeros((LANES, 5 * cw), F32)
        wl = wl.at[0:W_LORA, 0:cw].set(rwkv_wB[li, 0]).at[W_LORA:2 * W_LORA, cw:2 * cw].set(rwkv_wB[li, 1])
        o_a = 2 * W_LORA
        wl = wl.at[o_a:o_a + A_LORA, 2 * cw:3 * cw].set(rwkv_aB[li, 0])
        wl = wl.at[o_a + A_LORA:o_a + 2 * A_LORA, 3 * cw:4 * cw].set(rwkv_aB[li, 1])
        o_g = o_a + 2 * A_LORA
        wl = wl.at[o_g:o_g + G_LORA, 4 * cw:5 * cw].set(rwkv_gB[li])
        mu = jnp.pad(rwkv_mu[li], ((0, 0), (0, IN_RWKV_PAD - IN_RWKV)))
        r_, v_, kk_, g_, bon_, lw_, be_, kd_ = _rwkv_prep(
            ur, mu, wl, bd, rwkv_w0[li], rwkv_a0[li], rwkv_kk[li][None], rwkv_ka[li][None],
            rwkv_rk[li].reshape(1, cw), lc)
        yf, yb = _wkv_scan(r_, v_, kk_, lw_, be_, kd_, lc)
        rw = _rwkv_out(yf, yb, bon_, g_, bd, rwkv_gn_g[li][None], rwkv_gn_b[li][None])

        x1, x2, vv = _hy_prep(uh, hy_short_w[li], hy_short_b[li][None], lc)
        w1p = jnp.pad(hy_w1[li], ((0, LANES - HY_EMB), (0, 0)))
        fargs = (w1p, hy_b1[li][None], hy_freq1[li][None], hy_w2[li], hy_b2[li][None], hy_freq2[li][None],
                 hy_w3[li], hy_decay[li][None])
        fw, bw, nrm = _hy_filter(feat_l, *fargs)
        c2 = HY_ORDER * HY_WIDTH
        a4 = _dft_cols(fstack, fw.reshape(1, nh, DFT_N2 * c2), 0, bw.reshape(1, nh, DFT_N2 * c2), 0, True)
        kspec = _spec(a4.reshape(4, n1, DFT_N2, c2), g_tab, nrm, n1 * DFT_N2)
        lat = lambda t: t[:, lc:].reshape(b, nh, cols)
        x1l, x2l, zin = lat(x1), lat(x2), lat(vv)
        for o, gate in enumerate((x1l, x2l)):
            a = _dft_cols(fstack, zin, 0, zin, 1, False)
            bv = _conv_mid(a.reshape(2, n1, DFT_N2, HY_WIDTH), g_tab, kspec, o)
            bias_cols = jnp.tile(hy_bias[li, o], DFT_N2)[None]
            zin = _idft_cols(cstack, bv.reshape(2, n1, cols), gate, zin, bias_cols)
        hy_l = zin.reshape(b, l, HY_WIDTH)
        fw_c, bw_c, nrm_c = _hy_filter(feat_c, *fargs)
        hy_c = _hy_ctx(x1[:, :lc], x2[:, :lc], vv[:, :lc], fw_c, bw_c, nrm_c, hy_bias[li], ff_c, ci_c)

        xx = _outproj(att_c, att_l, rw, hy_c, hy_l, xx, mod, w_out[li].astype(BF16),
                      ln1_g[li][None], ln1_b[li][None], lc)

        j = li // 2
        if li % 2 == 0:
            xx = _ffn(xx, mod, ffn_w1[j].astype(BF16), ffn_w3[j].astype(BF16), ffn_w2[j].astype(BF16),
                      ln2_g[li][None], ln2_b[li][None], lc)
        else:
            wr = jnp.pad(moe_router[j], ((0, 0), (0, LANES - N_EXPERTS)))
            xx = _moe_sparse(xx, mod, wr, moe_w1[j].astype(BF16), moe_w3[j].astype(BF16), moe_w2[j].astype(BF16),
                             ln2_g[li][None], ln2_b[li][None], lc)
    return xx[:, lc:]
```

```python
import functools
import math

import jax
import jax.numpy as jnp
from jax import lax
from jax.experimental import pallas as pl
from jax.experimental.pallas import tpu as pltpu

F32 = jnp.float32
BF16 = jnp.bfloat16
HI = lax.Precision.HIGHEST

D_MODEL = 1024
DEPTH = 2
GRID_W = 64
HEAD_DIM = 64
HALF_HD = HEAD_DIM // 2
ATT_WIDTH = 512
RWKV_WIDTH = 256
HY_WIDTH = 256
ATT_HEADS = 8
ATT_KV_HEADS = 2
ATT_REP = 4
ATT_KV_WIDTH = 128
ROPE_THETA = 10000.0
QK_EPS = 1e-6
RWKV_HEADS = 4
W_LORA = 16
A_LORA = 16
G_LORA = 32
RWKV_GN_EPS = 64e-5
HY_ORDER = 2
HY_EMB = 33
HY_FFN = 64
N_EXPERTS = 8
LN_EPS = 1e-6
IN_ATT = ATT_WIDTH + 2 * ATT_KV_WIDTH
IN_RWKV = 3 * RWKV_WIDTH + 2 * W_LORA + 2 * A_LORA + G_LORA
IN_RWKV_PAD = 896
IN_HY = 3 * HY_WIDTH
ALPHA = float((2 * DEPTH) ** 0.25)
LOG2E = 1.4426950408889634
MAX_UNSHIFTED_SCORE = 40.0

LANES = 128
SUBLANES = 8
TOK_TILE = 256
ROW_TILE_MID = 640
ROW_TILE_WIDE = 1280
WKV_CHUNK = 64
WKV_TILE = 256
DFT_N2 = 256
MOE_ROW_BLOCK = 128
MOE_MOVE_BLOCK = 256
VMEM_LIMIT = 48 * 1024 * 1024
VMEM_LIMIT_MOE = 56 * 1024 * 1024


def _cparams(sem, vmem=VMEM_LIMIT):
    return pltpu.CompilerParams(dimension_semantics=sem, vmem_limit_bytes=vmem)


def _pick_tile(n, cands):
    for c in cands:
        if n % c == 0:
            return c
    raise ValueError(f"no tile for {n} in {cands}")


def _layer_norm(x):
    mu = jnp.mean(x, -1, keepdims=True)
    xc = x - mu
    var = jnp.mean(xc * xc, -1, keepdims=True)
    return xc * lax.rsqrt(var + LN_EPS)


def _sel_mod(mod_ref, j, row0, tm, lc):
    rid = row0 + lax.broadcasted_iota(jnp.int32, (tm, 1), 0)
    return jnp.where(rid < lc, mod_ref[0, 0, j:j + 1, :], mod_ref[0, 1, j:j + 1, :])


def _silu(x):
    return x * jax.nn.sigmoid(x)


def _ada_kernel(c_ref, w_ref, b_ref, o_ref):
    s = _silu(c_ref[...])
    o_ref[0] = _dotp(s, w_ref[0], 3) + b_ref[0]


def _ada_mod(cond8, ada_w, ada_b):
    depth, d, n = ada_w.shape
    tn = _pick_tile(n, (1536, 1024, 512, 256, 128))
    return pl.pallas_call(
        _ada_kernel,
        grid=(depth, n // tn),
        in_specs=[pl.BlockSpec((SUBLANES, d), lambda l, j: (0, 0)),
                  pl.BlockSpec((1, d, tn), lambda l, j: (l, 0, j)),
                  pl.BlockSpec((1, 1, tn), lambda l, j: (l, 0, j))],
        out_specs=pl.BlockSpec((1, SUBLANES, tn), lambda l, j: (l, 0, j)),
        out_shape=jax.ShapeDtypeStruct((depth, SUBLANES, n), F32),
        compiler_params=_cparams(("parallel", "parallel")),
        name="ada_mod",
    )(cond8, ada_w, ada_b.reshape(depth, 1, n))


def _inproj_kernel(x_ref, mod_ref, w_ref, oa_ref, or_ref, oh_ref, *, lc, tm):
    row0 = pl.program_id(1) * tm
    sh = _sel_mod(mod_ref, 0, row0, tm, lc)
    sc = _sel_mod(mod_ref, 1, row0, tm, lc)
    h = (_layer_norm(x_ref[0]) * (1.0 + sc) + sh).astype(BF16)
    u = jnp.dot(h, w_ref[...], preferred_element_type=F32)
    oa_ref[0] = u[:, :IN_ATT]
    or_ref[0] = u[:, IN_ATT:IN_ATT + IN_RWKV_PAD]
    oh_ref[0] = u[:, IN_ATT + IN_RWKV_PAD:]


def _inproj(xx, mod, w_pad, lc):
    b, lt, d = xx.shape
    tm = _pick_tile(lt, (ROW_TILE_MID, TOK_TILE))
    n = w_pad.shape[1]
    return pl.pallas_call(
        functools.partial(_inproj_kernel, lc=lc, tm=tm),
        grid=(b, lt // tm),
        in_specs=[pl.BlockSpec((1, tm, d), lambda bi, i: (bi, i, 0)),
                  pl.BlockSpec((1, 2, 6, d), lambda bi, i: (bi, 0, 0, 0)),
                  pl.BlockSpec((d, n), lambda bi, i: (0, 0))],
        out_specs=[pl.BlockSpec((1, tm, IN_ATT), lambda bi, i: (bi, i, 0)),
                   pl.BlockSpec((1, tm, IN_RWKV_PAD), lambda bi, i: (bi, i, 0)),
                   pl.BlockSpec((1, tm, IN_HY), lambda bi, i: (bi, i, 0))],
        out_shape=[jax.ShapeDtypeStruct((b, lt, IN_ATT), F32),
                   jax.ShapeDtypeStruct((b, lt, IN_RWKV_PAD), F32),
                   jax.ShapeDtypeStruct((b, lt, IN_HY), F32)],
        compiler_params=_cparams(("parallel", "parallel")),
        name="inproj",
    )(xx, mod, w_pad)


def _attn_prep_kernel(u_ref, cos_ref, sin_ref, qg_ref, kg_ref, bd_ref, sw_ref, qt_ref, k_ref, vt_ref):
    u = u_ref[0]

    def norm_rope(x, g):
        w = x.shape[1]
        tile = lambda t: jnp.concatenate([t] * (w // LANES), -1)
        ms = _dot01(x * x, bd_ref[:w, :w]) * (1.0 / HEAD_DIM)
        xn = x * lax.rsqrt(ms + QK_EPS) * tile(g)
        sw = _dot01(xn, sw_ref[:w, :w])
        return xn * tile(cos_ref[...]) + sw * tile(sin_ref[...])

    q = norm_rope(u[:, :ATT_WIDTH], qg_ref[...]) * (LOG2E * HEAD_DIM ** -0.5)
    qt = q.T
    kx = norm_rope(u[:, ATT_WIDTH:ATT_WIDTH + ATT_KV_WIDTH], kg_ref[...])
    tq = TOK_TILE
    for g in range(ATT_KV_HEADS):
        base = g * ATT_REP * HEAD_DIM
        for t in range(u.shape[0] // tq):
            qt_ref[0, g, t] = jnp.concatenate(
                [qt[base + r * HEAD_DIM:base + (r + 1) * HEAD_DIM, t * tq:(t + 1) * tq] for r in range(ATT_REP)],
                -1).astype(BF16)
        k_ref[0, g] = kx[:, g * HEAD_DIM:(g + 1) * HEAD_DIM].astype(BF16)
    v0 = ATT_WIDTH + ATT_KV_WIDTH
    vt = u[:, v0:v0 + ATT_KV_WIDTH].T
    for g in range(ATT_KV_HEADS):
        vt_ref[0, g] = vt[g * HEAD_DIM:(g + 1) * HEAD_DIM].astype(BF16)


def _attn_prep(ua, cos, sin, qg, kg, bd_att, swap_att, lc):
    b, lt, _ = ua.shape
    tq = TOK_TILE
    tm = _pick_tile(lt, (ROW_TILE_WIDE, ROW_TILE_MID, TOK_TILE))
    return pl.pallas_call(
        _attn_prep_kernel,
        grid=(b, lt // tm),
        in_specs=[pl.BlockSpec((1, tm, IN_ATT), lambda bi, i: (bi, i, 0)),
                  pl.BlockSpec((tm, LANES), lambda bi, i: (i, 0)),
                  pl.BlockSpec((tm, LANES), lambda bi, i: (i, 0)),
                  pl.BlockSpec((1, LANES), lambda bi, i: (0, 0)),
                  pl.BlockSpec((1, LANES), lambda bi, i: (0, 0)),
                  pl.BlockSpec((ATT_WIDTH, ATT_WIDTH), lambda bi, i: (0, 0)),
                  pl.BlockSpec((ATT_WIDTH, ATT_WIDTH), lambda bi, i: (0, 0))],
        out_specs=[pl.BlockSpec((1, ATT_KV_HEADS, tm // tq, HEAD_DIM, ATT_REP * tq),
                                lambda bi, i: (bi, 0, i, 0, 0)),
                   pl.BlockSpec((1, ATT_KV_HEADS, tm, HEAD_DIM), lambda bi, i: (bi, 0, i, 0)),
                   pl.BlockSpec((1, ATT_KV_HEADS, HEAD_DIM, tm), lambda bi, i: (bi, 0, 0, i))],
        out_shape=[jax.ShapeDtypeStruct((b, ATT_KV_HEADS, lt // tq, HEAD_DIM, ATT_REP * tq), BF16),
                   jax.ShapeDtypeStruct((b, ATT_KV_HEADS, lt, HEAD_DIM), BF16),
                   jax.ShapeDtypeStruct((b, ATT_KV_HEADS, HEAD_DIM, lt), BF16)],
        compiler_params=_cparams(("parallel", "parallel")),
        name="attn_prep",
    )(ua, cos, sin, qg, kg, bd_att, swap_att)


def _flash_kernel(qt_refs, k_ref, vt_ref, o_ref, m_scr, l_scr, acc_scr, *, nk, tq, sub, qn):
    j = pl.program_id(3)

    @pl.when(j == 0)
    def _():
        m_scr[...] = jnp.full(m_scr.shape, -jnp.inf, F32)
        l_scr[...] = jnp.zeros(l_scr.shape, F32)
        acc_scr[...] = jnp.zeros(acc_scr.shape, F32)

    qt = jnp.concatenate([r[0, 0, 0] for r in qt_refs], -1)
    nsub = k_ref.shape[2] // sub
    m = m_scr[...]
    l = l_scr[...]
    acc = acc_scr[...]
    scores = lambda c: jnp.dot(k_ref[0, 0, c * sub:(c + 1) * sub, :], qt, preferred_element_type=F32)
    pv = lambda c, p: jnp.dot(vt_ref[0, 0, :, c * sub:(c + 1) * sub], p, preferred_element_type=F32)
    s_next = scores(0)
    pend = None
    for c in range(nsub):
        s = s_next
        if c + 1 < nsub:
            s_next = scores(c + 1)
        if pend is not None:
            acc = pend[0] * acc + pv(c - 1, pend[1])
        m_new = jnp.maximum(m, jnp.max(s, 0, keepdims=True))
        a = jnp.exp2(m - m_new)
        p = jnp.exp2(s - m_new)
        l = a * l + jnp.sum(p, 0, keepdims=True)
        pend = (a, p.astype(BF16))
        m = m_new
    acc = pend[0] * acc + pv(nsub - 1, pend[1])
    m_scr[...] = m
    l_scr[...] = l
    acc_scr[...] = acc

    @pl.when(j == nk - 1)
    def _():
        o = (acc / l).T
        for t in range(qn):
            o_ref[0, t * tq:(t + 1) * tq, :] = jnp.concatenate(
                [o[(t * ATT_REP + r) * tq:(t * ATT_REP + r + 1) * tq] for r in range(ATT_REP)], -1
            ).astype(o_ref.dtype)


def _flash_bounded_kernel(qt_refs, k_ref, vt_ref, o_ref, l_scr, acc_scr, *, nk, tq, sub, qn):
    j = pl.program_id(3)

    @pl.when(j == 0)
    def _():
        l_scr[...] = jnp.zeros(l_scr.shape, F32)
        acc_scr[...] = jnp.zeros(acc_scr.shape, F32)

    qt = jnp.concatenate([r[0, 0, 0] for r in qt_refs], -1)
    nsub = k_ref.shape[2] // sub

    l = l_scr[...]
    acc = acc_scr[...]
    scores = lambda c: jnp.dot(k_ref[0, 0, c * sub:(c + 1) * sub, :], qt, preferred_element_type=F32)
    s_next = scores(0)
    for c in range(nsub):
        s = s_next
        if c + 1 < nsub:
            s_next = scores(c + 1)
        p = jnp.exp2(s)
        l = l + jnp.sum(p.reshape(sub // SUBLANES, SUBLANES, p.shape[1]), 0)
        acc = acc + jnp.dot(vt_ref[0, 0, :, c * sub:(c + 1) * sub], p.astype(BF16), preferred_element_type=F32)
    l_scr[...] = l
    acc_scr[...] = acc

    @pl.when(j == nk - 1)
    def _():
        o = (acc / jnp.sum(l, 0, keepdims=True)).T
        for t in range(qn):
            o_ref[0, t * tq:(t + 1) * tq, :] = jnp.concatenate(
                [o[(t * ATT_REP + r) * tq:(t * ATT_REP + r + 1) * tq] for r in range(ATT_REP)], -1
            ).astype(o_ref.dtype)


def _flash(qt, k, vt, tq, tk, qn, q_tile0, nq, nk, bounded=False):
    b = qt.shape[0]
    lq = nq * qn * tq
    sub = _pick_tile(tk, (256, 128))
    lanes = qn * ATT_REP * tq
    if bounded:
        kern = functools.partial(_flash_bounded_kernel, nk=nk, tq=tq, sub=sub, qn=qn)
        scratch = [pltpu.VMEM((SUBLANES, lanes), F32), pltpu.VMEM((HEAD_DIM, lanes), F32)]
    else:
        kern = functools.partial(_flash_kernel, nk=nk, tq=tq, sub=sub, qn=qn)
        scratch = [pltpu.VMEM((1, lanes), F32), pltpu.VMEM((1, lanes), F32), pltpu.VMEM((HEAD_DIM, lanes), F32)]
    q_spec = lambda t: pl.BlockSpec((1, 1, 1, HEAD_DIM, ATT_REP * tq),
                                    lambda bi, g, i, j: (bi, g, q_tile0 + i * qn + t, 0, 0))
    return pl.pallas_call(
        lambda *refs: kern(refs[:qn], *refs[qn:]),
        grid=(b, ATT_KV_HEADS, nq, nk),
        in_specs=[q_spec(t) for t in range(qn)] + [
                  pl.BlockSpec((1, 1, tk, HEAD_DIM), lambda bi, g, i, j: (bi, g, j, 0)),
                  pl.BlockSpec((1, 1, HEAD_DIM, tk), lambda bi, g, i, j: (bi, g, 0, j))],
        out_specs=pl.BlockSpec((1, qn * tq, ATT_REP * HEAD_DIM), lambda bi, g, i, j: (bi, i, g)),
        out_shape=jax.ShapeDtypeStruct((b, lq, ATT_WIDTH), BF16),
        scratch_shapes=scratch,
        compiler_params=_cparams(("parallel", "parallel", "parallel", "arbitrary")),
        name="flash_bounded" if bounded else "flash",
    )(*([qt] * qn), k, vt)


def _prev_next(u, up8, un8, i, tm, lc, lt):
    rid = lax.broadcasted_iota(jnp.int32, u.shape, 0)
    gid = rid + i * tm
    prev = jnp.where(rid == 0, up8[SUBLANES - 1:SUBLANES], pltpu.roll(u, 1, 0))
    prev = jnp.where(gid == 0, 0.0, jnp.where(gid == lc, 0.0, prev))
    nxt = jnp.where(rid == tm - 1, un8[0:1], pltpu.roll(u, tm - 1, 0))
    nxt = jnp.where(gid == lc - 1, 0.0, jnp.where(gid == lt - 1, 0.0, nxt))
    return prev, nxt


def _halo_specs(tm, c, lt):
    r = tm // SUBLANES
    last = lt // SUBLANES - 1
    return [pl.BlockSpec((1, tm, c), lambda bi, i: (bi, i, 0)),
            pl.BlockSpec((1, SUBLANES, c), lambda bi, i: (bi, jnp.maximum(i * r - 1, 0), 0)),
            pl.BlockSpec((1, SUBLANES, c), lambda bi, i: (bi, jnp.minimum((i + 1) * r, last), 0))]


def _softplus(z):
    return jnp.maximum(z, 0.0) + jnp.log1p(jnp.exp(-jnp.abs(z)))


def _rwkv_prep_kernel(u_ref, up_ref, un_ref, mu_ref, wl_ref, bd_ref, w0_ref, a0_ref, kkw_ref, ka_ref, rk_ref,
                      r_o, v_o, kk_o, g_o, bon_o, lw_o, be_o, kd_o, *, lc, lt, tm):
    i = pl.program_id(1)
    u = u_ref[0]
    prev, nxt = _prev_next(u, up_ref[0], un_ref[0], i, tm, lc, lt)
    us = u + mu_ref[0:1] * (prev - u) + mu_ref[1:2] * (nxt - u)
    c = RWKV_WIDTH
    r = us[:, 0:c]
    k = us[:, c:2 * c]
    v = us[:, 2 * c:3 * c]
    slab = us[:, 3 * c:3 * c + LANES]
    lane = lax.broadcasted_iota(jnp.int32, slab.shape, 1)
    o_a = 2 * W_LORA
    o_g = o_a + 2 * A_LORA
    act = jnp.where(lane < o_a, jnp.tanh(slab),
                    jnp.where(lane < o_g, slab,
                              jnp.where(lane < o_g + G_LORA, jax.nn.sigmoid(slab), 0.0)))
    lo = _dotp(act, wl_ref[...], 3)
    bd = bd_ref[...]
    kk0 = k * kkw_ref[...]
    kk = kk0 * lax.rsqrt(_dot01(kk0 * kk0, bd) + 1e-12)
    r_o[0] = r
    v_o[0] = v
    kk_o[0] = kk
    g_o[0] = lo[:, 4 * c:5 * c]
    bon = jnp.zeros_like(r)
    for d in range(2):
        w_raw = w0_ref[d:d + 1] + lo[:, d * c:(d + 1) * c]
        lw = -jnp.exp(-_softplus(-w_raw) - 0.5)
        a = jax.nn.sigmoid(a0_ref[d:d + 1] + lo[:, (2 + d) * c:(3 + d) * c])
        kd = k * (1.0 + (a - 1.0) * ka_ref[...])
        lw_o[0, d] = lw
        be_o[0, d] = a * kk
        kd_o[0, d] = kd
        bon = bon + r * kd * rk_ref[...]
    bon_o[0] = _dot01(bon, bd) * v


def _rwkv_prep(ur, mu, wl, bd, w0, a0, kkw, ka, rk, lc):
    b, lt, cp = ur.shape
    tm = _pick_tile(lt, (ROW_TILE_MID, TOK_TILE))
    c = RWKV_WIDTH
    full = lambda shape: pl.BlockSpec(shape, lambda bi, i: (0,) * len(shape))
    tok = pl.BlockSpec((1, tm, c), lambda bi, i: (bi, i, 0))
    tok2 = pl.BlockSpec((1, 2, tm, c), lambda bi, i: (bi, 0, i, 0))
    s1 = jax.ShapeDtypeStruct((b, lt, c), F32)
    s2 = jax.ShapeDtypeStruct((b, 2, lt, c), F32)
    return pl.pallas_call(
        functools.partial(_rwkv_prep_kernel, lc=lc, lt=lt, tm=tm),
        grid=(b, lt // tm),
        in_specs=_halo_specs(tm, cp, lt) + [full((2, cp)), full((LANES, 5 * c)), full((c, c)), full((2, c)),
                                            full((2, c)), full((1, c)), full((1, c)), full((1, c))],
        out_specs=[tok, tok, tok, tok, tok, tok2, tok2, tok2],
        out_shape=[s1, s1, s1, s1, s1, s2, s2, s2],
        compiler_params=_cparams(("parallel", "parallel")),
        name="rwkv_prep",
    )(ur, ur, ur, mu, wl, bd, w0, a0, kkw, ka, rk)


def _mm(a, b):
    return jnp.dot(a, b, precision=HI, preferred_element_type=F32)


_NN = ((1,), (0,))
_NT = ((1,), (1,))
_TN = ((0,), (0,))


def _split2(a):
    hi = a.astype(BF16)
    return hi, (a - hi.astype(F32)).astype(BF16)


def _dotp(a, b, passes, dims=_NN):
    if a.ndim == 3:
        dn = (((dims[0][0] + 1,), (dims[1][0] + 1,)), ((0,), (0,)))
    else:
        dn = (dims, ((), ()))
    dg = lambda p, q: lax.dot_general(p, q, dn, preferred_element_type=F32)
    if passes == 1:
        return dg(a.astype(BF16), b.astype(BF16))
    ah, al = _split2(a)
    bh, bl = _split2(b)
    return dg(ah, bh) + dg(ah, bl) + dg(al, bh)


def _dot01(a, ones):
    ah, al = _split2(a)
    o = ones.astype(BF16)
    return jnp.dot(ah, o, preferred_element_type=F32) + jnp.dot(al, o, preferred_element_type=F32)


P_M = 1
P_INV = 1
P_W = 1
P_Z = 1
P_STATE = 1
P_DFT = 1


def _unit_tri_inv(a_mat, row, col, eye, passes):
    t = a_mat.shape[-1]
    eye_f = eye.astype(F32)
    base = SUBLANES
    same = (row // base) == (col // base)
    n1 = -jnp.where(same, a_mat, 0.0)
    n2 = _dotp(n1, n1, passes)
    n4 = _dotp(n2, n2, passes)
    x = _dotp(_dotp(eye_f + n1, eye_f + n2, passes), eye_f + n4, passes)
    m = base
    while m < t:
        off = jnp.logical_and((row // (2 * m)) == (col // (2 * m)), (row // m) != (col // m))
        x = x - _dotp(x, _dotp(jnp.where(off, a_mat, 0.0), x, passes), passes)
        m *= 2
    return x


def _wkv_kernel(rf_ref, vf_ref, kkf_ref, rb_ref, vb_ref, kkb_ref, lwf_ref, bef_ref, kdf_ref,
                lwb_ref, beb_ref, kdb_ref, yf_ref, yb_ref, h_scr, pq_scr, ry_scr):
    t = WKV_CHUNK
    n = HEAD_DIM
    g = WKV_TILE // WKV_CHUNK
    tt = WKV_TILE
    nh = RWKV_HEADS
    nd = g * nh
    nu = 2 * nd
    orders = (list(range(g)), list(range(g - 1, -1, -1)))

    @pl.when(pl.program_id(1) == 0)
    def _():
        h_scr[...] = jnp.zeros(h_scr.shape, F32)
        pq_scr[...] = jnp.zeros(pq_scr.shape, F32)
        ry_scr[...] = jnp.zeros(ry_scr.shape, F32)

    hm = h_scr[...]
    for p in range(g):
        both = lambda ref: jnp.concatenate([ref[d * nd + p * nh:d * nd + (p + 1) * nh] for d in range(2)], 0)
        ry = both(ry_scr)
        pq = both(pq_scr)
        y = _dotp(ry[:, :, :n], hm, P_STATE) + ry[:, :, n:]
        hm = _dotp(pq[:, :, :n], hm, P_STATE) + pq[:, :, n:]
        for d, y_ref in enumerate((yf_ref, yb_ref)):
            c = orders[d][p]
            y_ref[0, c * t:(c + 1) * t, :] = jnp.concatenate([y[d * nh + h] for h in range(nh)], -1)
    h_scr[...] = hm

    row = lax.broadcasted_iota(jnp.int32, (tt, tt), 0)
    col = lax.broadcasted_iota(jnp.int32, (tt, tt), 1)
    same = (row // t) == (col // t)

    def scaled(d, r_ref, v_ref, kk_ref, lw_ref, be_ref, kd_ref):
        tri = jnp.logical_and(same, (row >= col) if d == 0 else (row <= col))
        sums = jnp.concatenate([jnp.where(tri, 1.0, 0.0), jnp.where(same, 1.0, 0.0)], 0).astype(BF16)
        lw = lw_ref[0, 0]
        l1 = lw.astype(BF16)
        rem = lw - l1.astype(F32)
        l2 = rem.astype(BF16)
        l3 = (rem - l2.astype(F32)).astype(BF16)
        cc = (jnp.dot(sums, l1, preferred_element_type=F32) + jnp.dot(sums, l2, preferred_element_type=F32)
              + jnp.dot(sums, l3, preferred_element_type=F32))
        cum = cc[:tt]
        ctot = cc[tt:]
        e_neg = jnp.exp(-cum)
        e_end = jnp.exp(ctot - cum)
        be = be_ref[0, 0]
        kd = kd_ref[0, 0]

        def units(x, rows=t):
            return jnp.stack([x[orders[d][p] * t:orders[d][p] * t + rows, h * n:(h + 1) * n]
                              for p in range(g) for h in range(nh)], 0)

        return dict(kap=units(kk_ref[0] * jnp.exp(cum - lw)), rt=units(r_ref[0] * jnp.exp(cum)), vh=units(v_ref[0]),
                    bet=units(be * e_neg), kdt=units(kd * e_neg), beh=units(be * e_end), kdh=units(kd * e_end),
                    gend=units(jnp.exp(ctot), 1))

    parts = (scaled(0, rf_ref, vf_ref, kkf_ref, lwf_ref, bef_ref, kdf_ref),
             scaled(1, rb_ref, vb_ref, kkb_ref, lwb_ref, beb_ref, kdb_ref))
    cat = lambda key: jnp.concatenate([parts[0][key], parts[1][key]], 0)
    kap, rt, vh = cat("kap"), cat("rt"), cat("vh")
    r64 = lax.broadcasted_iota(jnp.int32, (t, t), 0)
    c64 = lax.broadcasted_iota(jnp.int32, (t, t), 1)
    eye = r64 == c64
    unit = lax.broadcasted_iota(jnp.int32, (nu, t, t), 0)
    ahead = (r64 - c64)[None] * jnp.where(unit < nd, 1, -1)
    strict = ahead > 0
    incl = ahead >= 0
    m = _dotp(jnp.concatenate([kap, rt], 1), jnp.concatenate([cat("bet"), cat("kdt")], 1), P_M, _NT)
    a_mat = jnp.where(strict, m[:, :t, :t], 0.0)
    b_mat = jnp.where(strict, m[:, :t, t:], 0.0)
    ab_r = jnp.concatenate([jnp.where(incl, m[:, t:, :t], 0.0), jnp.where(incl, m[:, t:, t:], 0.0)], 2)
    tinv = _unit_tri_inv(a_mat, r64, c64, eye, P_INV)
    w = _dotp(tinv, jnp.concatenate([kap, _dotp(b_mat, vh, P_W)], 2), P_W)
    z = jnp.concatenate([-w, jnp.concatenate([jnp.zeros((nu, t, n), F32), vh], 2)], 1)
    ry = _dotp(ab_r, z, P_Z)
    ry_scr[:, :, :n] = ry[:, :, :n] + rt
    ry_scr[:, :, n:] = ry[:, :, n:]
    pq = _dotp(jnp.concatenate([cat("beh"), cat("kdh")], 1), z, P_Z, _TN)
    gd = jnp.where(eye, jnp.broadcast_to(cat("gend"), (nu, n, n)), 0.0)
    pq_scr[:, :, :n] = pq[:, :, :n] + gd
    pq_scr[:, :, n:] = pq[:, :, n:]


def _wkv_scan(r, v, kk, lw, be, kd, lc):
    b, lt, c = r.shape
    tt = WKV_TILE
    nt = lt // tt
    ntc = lc // tt
    tiles = (lambda i: i, lambda i: jnp.where(i < ntc, ntc - 1 - i, nt - 1 - (i - ntc)))
    t_in = lambda d: (lambda i: tiles[d](jnp.minimum(i, nt - 1)))
    t_out = lambda d: (lambda i: tiles[d](jnp.maximum(i - 1, 0)))
    one = lambda d: pl.BlockSpec((1, tt, c), lambda bi, i: (bi, t_in(d)(i), 0))
    two = lambda d: pl.BlockSpec((1, 1, tt, c), lambda bi, i: (bi, d, t_in(d)(i), 0))
    out = lambda d: pl.BlockSpec((1, tt, c), lambda bi, i: (bi, t_out(d)(i), 0))
    nu = 2 * (tt // WKV_CHUNK) * RWKV_HEADS
    ysh = jax.ShapeDtypeStruct((b, lt, c), F32)
    return pl.pallas_call(
        _wkv_kernel,
        grid=(b, nt + 1),
        in_specs=[one(0)] * 3 + [one(1)] * 3 + [two(0)] * 3 + [two(1)] * 3,
        out_specs=[out(0), out(1)],
        out_shape=[ysh, ysh],
        scratch_shapes=[pltpu.VMEM((2 * RWKV_HEADS, HEAD_DIM, HEAD_DIM), F32),
                        pltpu.VMEM((nu, HEAD_DIM, 2 * HEAD_DIM), F32),
                        pltpu.VMEM((nu, WKV_CHUNK, 2 * HEAD_DIM), F32)],
        compiler_params=_cparams(("parallel", "arbitrary")),
        name="wkv_scan",
    )(r, v, kk, r, v, kk, lw, be, kd, lw, be, kd)


def _rwkv_out_kernel(yf_ref, yb_ref, bon_ref, g_ref, bd_ref, gg_ref, gb_ref, o_ref):
    y = yf_ref[0] + yb_ref[0] + bon_ref[0]
    bd = bd_ref[...]
    mu = _dot01(y, bd) * (1.0 / HEAD_DIM)
    yc = y - mu
    var = _dot01(yc * yc, bd) * (1.0 / HEAD_DIM)
    yn = yc * lax.rsqrt(var + RWKV_GN_EPS) * gg_ref[...] + gb_ref[...]
    o_ref[0] = (yn * g_ref[0]).astype(o_ref.dtype)


def _rwkv_out(yf, yb, bon, g, bd, gg, gb):
    b, lt, c = yf.shape
    tm = _pick_tile(lt, (ROW_TILE_WIDE, ROW_TILE_MID, TOK_TILE))
    tok = pl.BlockSpec((1, tm, c), lambda bi, i: (bi, i, 0))
    full = lambda shape: pl.BlockSpec(shape, lambda bi, i: (0,) * len(shape))
    return pl.pallas_call(
        _rwkv_out_kernel,
        grid=(b, lt // tm),
        in_specs=[tok, tok, tok, tok, full((c, c)), full((1, c)), full((1, c))],
        out_specs=tok,
        out_shape=jax.ShapeDtypeStruct((b, lt, c), BF16),
        compiler_params=_cparams(("parallel", "parallel")),
        name="rwkv_out",
    )(yf, yb, bon, g, bd, gg, gb)


def _hy_prep_kernel(u_ref, up_ref, un_ref, w_ref, b_ref, x1_o, x2_o, v_o, *, lc, lt, tm):
    i = pl.program_id(1)
    u = u_ref[0]
    prev, nxt = _prev_next(u, up_ref[0], un_ref[0], i, tm, lc, lt)
    y = prev * w_ref[0:1] + u * w_ref[1:2] + nxt * w_ref[2:3] + b_ref[...]
    c = HY_WIDTH
    x1_o[0] = y[:, :c]
    x2_o[0] = y[:, c:2 * c]
    v_o[0] = y[:, 2 * c:]


def _hy_prep(uh, w, bias, lc):
    b, lt, cin = uh.shape
    tm = _pick_tile(lt, (ROW_TILE_WIDE, ROW_TILE_MID, TOK_TILE))
    c = HY_WIDTH
    full = lambda shape: pl.BlockSpec(shape, lambda bi, i: (0,) * len(shape))
    tok = pl.BlockSpec((1, tm, c), lambda bi, i: (bi, i, 0))
    s1 = jax.ShapeDtypeStruct((b, lt, c), F32)
    return pl.pallas_call(
        functools.partial(_hy_prep_kernel, lc=lc, lt=lt, tm=tm),
        grid=(b, lt // tm),
        in_specs=_halo_specs(tm, cin, lt) + [full((3, cin)), full((1, cin))],
        out_specs=[tok, tok, tok],
        out_shape=[s1, s1, s1],
        compiler_params=_cparams(("parallel", "parallel")),
        name="hy_prep",
    )(uh, uh, uh, w, bias)


def _hy_filter_kernel(z_ref, w1_ref, b1_ref, f1_ref, w2_ref, b2_ref, f2_ref, w3_ref, dec_ref,
                      fw_o, bw_o, nrm_o, *, tl):
    i = pl.program_id(0)
    z = z_ref[...]
    h = jnp.sin(f1_ref[...] * (_dotp(z, w1_ref[...], 3) + b1_ref[...]))
    h = jnp.sin(f2_ref[...] * (_dotp(h, w2_ref[...], 3) + b2_ref[...]))
    h = _dotp(h, w3_ref[...], 3) * jnp.exp(-z[:, 0:1] * dec_ref[...])
    c = HY_WIDTH
    fw = jnp.concatenate([h[:, 0:c], h[:, 2 * c:3 * c]], 1)
    bw = jnp.concatenate([h[:, c:2 * c], h[:, 3 * c:4 * c]], 1)
    rid = i * tl + lax.broadcasted_iota(jnp.int32, (tl, 1), 0)
    bw = jnp.where(rid == 0, 0.0, bw)
    fw_o[...] = fw
    bw_o[...] = bw

    @pl.when(i == 0)
    def _():
        nrm_o[...] = jnp.zeros(nrm_o.shape, F32)

    nrm_o[...] += jnp.sum(jnp.abs(fw) + jnp.abs(bw), 0, keepdims=True)


def _hy_filter(feat, w1p, b1, f1, w2, b2, f2, w3, dec):
    l, fe = feat.shape
    tl = _pick_tile(l, (512, 256))
    c2 = HY_ORDER * HY_WIDTH
    full = lambda shape: pl.BlockSpec(shape, lambda i: (0,) * len(shape))
    return pl.pallas_call(
        functools.partial(_hy_filter_kernel, tl=tl),
        grid=(l // tl,),
        in_specs=[pl.BlockSpec((tl, fe), lambda i: (i, 0)), full(w1p.shape), full(b1.shape), full(f1.shape),
                  full(w2.shape), full(b2.shape), full(f2.shape), full(w3.shape), full(dec.shape)],
        out_specs=[pl.BlockSpec((tl, c2), lambda i: (i, 0)), pl.BlockSpec((tl, c2), lambda i: (i, 0)),
                   pl.BlockSpec((1, c2), lambda i: (0, 0))],
        out_shape=[jax.ShapeDtypeStruct((l, c2), F32), jax.ShapeDtypeStruct((l, c2), F32),
                   jax.ShapeDtypeStruct((1, c2), F32)],
        compiler_params=_cparams(("arbitrary",)),
        name="hy_filter",
    )(feat, w1p, b1, f1, w2, b2, f2, w3, dec)


def _dft_cols_kernel(f_ref, xa_ref, xb_ref, o_ref, *, n1, pair):
    f = f_ref[...]
    pa = _dotp(f, xa_ref[0], P_DFT)
    pb = _dotp(f, xb_ref[0], P_DFT)
    if pair:
        o_ref[0] = pa[:n1].astype(o_ref.dtype)
        o_ref[1] = pa[n1:].astype(o_ref.dtype)
        o_ref[2] = pb[:n1].astype(o_ref.dtype)
        o_ref[3] = pb[n1:].astype(o_ref.dtype)
    else:
        o_ref[0] = (pa[:n1] - pb[n1:]).astype(o_ref.dtype)
        o_ref[1] = (pb[:n1] + pa[n1:]).astype(o_ref.dtype)


def _dft_cols(fstack, xa, ia, xb, ib, pair):
    n1 = fstack.shape[0] // 2
    _, nh, cols = xa.shape
    tc = _pick_tile(cols, (4096, 2048, 1024, 512, 256, 128))
    no = 4 if pair else 2
    return pl.pallas_call(
        functools.partial(_dft_cols_kernel, n1=n1, pair=pair),
        grid=(cols // tc,),
        in_specs=[pl.BlockSpec(fstack.shape, lambda j: (0, 0)),
                  pl.BlockSpec((1, nh, tc), lambda j: (ia, 0, j)),
                  pl.BlockSpec((1, nh, tc), lambda j: (ib, 0, j))],
        out_specs=pl.BlockSpec((no, n1, tc), lambda j: (0, 0, j)),
        out_shape=jax.ShapeDtypeStruct((no, n1, cols), BF16),
        compiler_params=_cparams(("parallel",)),
        name="dft_cols",
    )(fstack, xa, xb)


def _cplx_left(gs, zr, zi, n):
    c = zr.shape[1]
    p = _dotp(gs, jnp.concatenate([zr, zi], 1), P_DFT)
    return p[:n, :c] - p[n:, c:], p[:n, c:] + p[n:, :c]


def _spec_kernel(a_ref, g_ref, nrm_ref, o_ref, *, n_total, kp):
    n2 = DFT_N2
    s = 1.0 / (nrm_ref[...] * n_total)
    for q in range(kp):
        gs = jnp.concatenate([g_ref[q, 0], g_ref[q, 1]], 0)
        fr, fi = _cplx_left(gs, a_ref[0, q], a_ref[1, q], n2)
        br, bi = _cplx_left(gs, a_ref[2, q], a_ref[3, q], n2)
        o_ref[0, q] = ((fr + br) * s).astype(o_ref.dtype)
        o_ref[1, q] = ((fi - bi) * s).astype(o_ref.dtype)


def _planes_per_step(n1):
    return next(k for k in (4, 2, 1) if n1 % k == 0)


def _spec(a4, g, nrm, n_total):
    _, n1, n2, c2 = a4.shape
    kp = _planes_per_step(n1)
    return pl.pallas_call(
        functools.partial(_spec_kernel, n_total=float(n_total), kp=kp),
        grid=(n1 // kp,),
        in_specs=[pl.BlockSpec((4, kp, n2, c2), lambda k: (0, k, 0, 0)),
                  pl.BlockSpec((kp, 2, n2, n2), lambda k: (k, 0, 0, 0)),
                  pl.BlockSpec((1, c2), lambda k: (0, 0))],
        out_specs=pl.BlockSpec((2, kp, n2, c2), lambda k: (0, k, 0, 0)),
        out_shape=jax.ShapeDtypeStruct((2, n1, n2, c2), BF16),
        compiler_params=_cparams(("parallel",)),
        name="hy_spec",
    )(a4, g, nrm)


def _conv_mid_kernel(a_ref, g_ref, k_ref, o_ref, *, kp):
    n2 = DFT_N2
    for q in range(kp):
        gs = jnp.concatenate([g_ref[q, 0], g_ref[q, 1]], 0)
        xr, xi = _cplx_left(gs, a_ref[0, q], a_ref[1, q], n2)
        c = xr.shape[1]
        kr = k_ref[0, q].astype(F32)
        ki = k_ref[1, q].astype(F32)
        zr = xr * kr - xi * ki
        zi = xr * ki + xi * kr
        zst = jnp.concatenate([jnp.concatenate([zr, zi], 1), jnp.concatenate([zi, -zr], 1)], 0)
        y = _dotp(gs, zst, P_DFT, _TN)
        o_ref[0, q] = y[:, :c].astype(o_ref.dtype)
        o_ref[1, q] = y[:, c:].astype(o_ref.dtype)


def _conv_mid(a, g, kspec, order):
    _, n1, n2, c = a.shape
    kp = _planes_per_step(n1)
    return pl.pallas_call(
        functools.partial(_conv_mid_kernel, kp=kp),
        grid=(n1 // kp,),
        in_specs=[pl.BlockSpec((2, kp, n2, c), lambda k: (0, k, 0, 0)),
                  pl.BlockSpec((kp, 2, n2, n2), lambda k: (k, 0, 0, 0)),
                  pl.BlockSpec((2, kp, n2, c), lambda k: (0, k, 0, order))],
        out_specs=pl.BlockSpec((2, kp, n2, c), lambda k: (0, k, 0, 0)),
        out_shape=jax.ShapeDtypeStruct((2, n1, n2, c), BF16),
        compiler_params=_cparams(("parallel",)),
        name="hy_conv_mid",
    )(a, g, kspec)


def _idft_cols_kernel(c_ref, b_ref, g0_ref, g1_ref, x0_ref, x1_ref, bias_ref, o_ref, *, nh):
    cs = c_ref[...]
    pr = _dotp(cs, b_ref[0], P_DFT)
    pi = _dotp(cs, b_ref[1], P_DFT)
    yr = pr[:nh] - pi[nh:]
    yi = pi[:nh] + pr[nh:]
    bias = bias_ref[...]
    o_ref[0] = g0_ref[0] * (yr + x0_ref[0] * bias)
    o_ref[1] = g1_ref[0] * (yi + x1_ref[0] * bias)


def _idft_cols(cstack, bv, gate, xin, bias_cols):
    nh2, n1 = cstack.shape
    nh = nh2 // 2
    cols = bv.shape[-1]
    tc = _pick_tile(cols, (4096, 2048, 1024, 512, 256, 128))
    row = lambda bi: pl.BlockSpec((1, nh, tc), lambda j: (bi, 0, j))
    return pl.pallas_call(
        functools.partial(_idft_cols_kernel, nh=nh),
        grid=(cols // tc,),
        in_specs=[pl.BlockSpec((nh2, n1), lambda j: (0, 0)),
                  pl.BlockSpec((2, n1, tc), lambda j: (0, 0, j)),
                  row(0), row(1), row(0), row(1),
                  pl.BlockSpec((1, tc), lambda j: (0, j))],
        out_specs=pl.BlockSpec((2, nh, tc), lambda j: (0, 0, j)),
        out_shape=jax.ShapeDtypeStruct((2, nh, cols), F32),
        compiler_params=_cparams(("parallel",)),
        name="idft_cols",
    )(cstack, bv, gate, gate, xin, xin, bias_cols)


def _hy_ctx_kernel(x1_ref, x2_ref, v_ref, fw_ref, bw_ref, nrm_ref, bias_ref, ff_ref, ci_ref, o_ref, *, lc):
    n = 2 * lc
    c = HY_WIDTH
    ff = ff_ref[...]
    ci = ci_ref[...]
    pf = _mm(ff, fw_ref[...])
    pb = _mm(ff, bw_ref[...])
    s = 1.0 / (nrm_ref[...] * float(n))
    kr = (pf[:n] + pb[:n]) * s
    ki = (pf[n:] - pb[n:]) * s

    def conv(z0, z1, o):
        xr, xi = _cplx_left(ff, z0, z1, n)
        krr = kr[:, o * c:(o + 1) * c]
        kii = ki[:, o * c:(o + 1) * c]
        return _cplx_left(ci, xr * krr - xi * kii, xr * kii + xi * krr, lc)

    v0 = v_ref[0]
    v1 = v_ref[1]
    y0, y1 = conv(v0, v1, 0)
    z0 = x1_ref[0] * (y0 + v0 * bias_ref[0:1])
    z1 = x1_ref[1] * (y1 + v1 * bias_ref[0:1])
    y0, y1 = conv(z0, z1, 1)
    o_ref[0] = x2_ref[0] * (y0 + z0 * bias_ref[1:2])
    o_ref[1] = x2_ref[1] * (y1 + z1 * bias_ref[1:2])


def _hy_ctx(x1, x2, v, fw, bw, nrm, bias, ff, ci):
    b, lc, c = v.shape
    vm = pl.BlockSpec(memory_space=pltpu.VMEM)
    return pl.pallas_call(
        functools.partial(_hy_ctx_kernel, lc=lc),
        in_specs=[vm] * 9,
        out_specs=vm,
        out_shape=jax.ShapeDtypeStruct((b, lc, c), F32),
        compiler_params=pltpu.CompilerParams(vmem_limit_bytes=VMEM_LIMIT),
        name="hy_ctx",
    )(x1, x2, v, fw, bw, nrm, bias, ff, ci)


def _outproj_kernel(attc_ref, attl_ref, rw_ref, hyc_ref, hyl_ref, x_ref, mod_ref, w_ref, lg_ref, lb_ref, o_ref,
                    *, lc, tm):
    row0 = pl.program_id(1) * tm
    a0 = ATT_WIDTH
    a1 = ATT_WIDTH + RWKV_WIDTH
    is_ctx = row0 < lc
    att = jnp.where(is_ctx, attc_ref[0], attl_ref[0])
    hy = jnp.where(is_ctx, hyc_ref[0], hyl_ref[0])
    o = jnp.dot(att, w_ref[:a0], preferred_element_type=F32)
    o += jnp.dot(rw_ref[0], w_ref[a0:a1], preferred_element_type=F32)
    o += jnp.dot(hy.astype(BF16), w_ref[a1:], preferred_element_type=F32)
    g = _sel_mod(mod_ref, 2, row0, tm, lc)
    y = ALPHA * x_ref[0] + g * o
    o_ref[0] = _layer_norm(y) * lg_ref[...] + lb_ref[...]


def _outproj(att_c, att_l, rw, hy_c, hy_l, xx, mod, w, lg, lb, lc):
    b, lt, d = xx.shape
    tm = TOK_TILE
    nct = lc // tm
    tok = lambda c: pl.BlockSpec((1, tm, c), lambda bi, i: (bi, i, 0))
    ctx = lambda c: pl.BlockSpec((1, tm, c), lambda bi, i: (bi, jnp.minimum(i, nct - 1), 0))
    lat = lambda c: pl.BlockSpec((1, tm, c), lambda bi, i: (bi, jnp.maximum(i - nct, 0), 0))
    full = lambda shape: pl.BlockSpec(shape, lambda bi, i: (0,) * len(shape))
    return pl.pallas_call(
        functools.partial(_outproj_kernel, lc=lc, tm=tm),
        grid=(b, lt // tm),
        in_specs=[ctx(ATT_WIDTH), lat(ATT_WIDTH), tok(RWKV_WIDTH), ctx(HY_WIDTH), lat(HY_WIDTH), tok(d),
                  pl.BlockSpec((1, 2, 6, d), lambda bi, i: (bi, 0, 0, 0)),
                  full(w.shape), full((1, d)), full((1, d))],
        out_specs=tok(d),
        out_shape=jax.ShapeDtypeStruct((b, lt, d), F32),
        compiler_params=_cparams(("parallel", "parallel")),
        name="outproj",
    )(att_c, att_l, rw, hy_c, hy_l, xx, mod, w, lg, lb)


def _ffn_kernel(x_ref, mod_ref, w1_ref, w3_ref, w2_ref, lg_ref, lb_ref, o_ref, h_scr, acc_scr, *, lc, tm, nf):
    row0 = pl.program_id(1) * tm
    f = pl.program_id(2)

    @pl.when(f == 0)
    def _():
        sh = _sel_mod(mod_ref, 3, row0, tm, lc)
        sc = _sel_mod(mod_ref, 4, row0, tm, lc)
        h_scr[...] = (_layer_norm(x_ref[0]) * (1.0 + sc) + sh).astype(BF16)
        acc_scr[...] = jnp.zeros(acc_scr.shape, F32)

    h = h_scr[...]
    a = jnp.dot(h, w1_ref[...], preferred_element_type=F32)
    g = jnp.dot(h, w3_ref[...], preferred_element_type=F32)
    acc_scr[...] += jnp.dot((_silu(a) * g).astype(BF16), w2_ref[...], preferred_element_type=F32)

    @pl.when(f == nf - 1)
    def _():
        gate = _sel_mod(mod_ref, 5, row0, tm, lc)
        y = ALPHA * x_ref[0] + gate * acc_scr[...]
        o_ref[0] = _layer_norm(y) * lg_ref[...] + lb_ref[...]


def _ffn(xx, mod, w1, w3, w2, lg, lb, lc):
    b, lt, d = xx.shape
    ff = w1.shape[1]
    tm = _pick_tile(lt, (1280, 768, 512, 256))
    tf = _pick_tile(ff, (256, 128))
    nf = ff // tf
    return pl.pallas_call(
        functools.partial(_ffn_kernel, lc=lc, tm=tm, nf=nf),
        grid=(b, lt // tm, nf),
        in_specs=[pl.BlockSpec((1, tm, d), lambda bi, i, f: (bi, i, 0)),
                  pl.BlockSpec((1, 2, 6, d), lambda bi, i, f: (bi, 0, 0, 0)),
                  pl.BlockSpec((d, tf), lambda bi, i, f: (0, f)),
                  pl.BlockSpec((d, tf), lambda bi, i, f: (0, f)),
                  pl.BlockSpec((tf, d), lambda bi, i, f: (f, 0)),
                  pl.BlockSpec((1, d), lambda bi, i, f: (0, 0)),
                  pl.BlockSpec((1, d), lambda bi, i, f: (0, 0))],
        out_specs=pl.BlockSpec((1, tm, d), lambda bi, i, f: (bi, i, 0)),
        out_shape=jax.ShapeDtypeStruct((b, lt, d), F32),
        scratch_shapes=[pltpu.VMEM((tm, d), BF16), pltpu.VMEM((tm, d), F32)],
        compiler_params=_cparams(("parallel", "parallel", "arbitrary")),
        name="ffn",
    )(xx, mod, w1, w3, w2, lg, lb)


def _route(x_ref, mod_ref, wr_ref, row0, tm, lc):
    sh = _sel_mod(mod_ref, 3, row0, tm, lc)
    sc = _sel_mod(mod_ref, 4, row0, tm, lc)
    h = _layer_norm(x_ref[0]) * (1.0 + sc) + sh
    logits = _dotp(h, wr_ref[...], 3)
    lane = lax.broadcasted_iota(jnp.int32, logits.shape, 1)
    neg = jnp.float32(-jnp.inf)
    lg = jnp.where(lane < N_EXPERTS, logits, neg)
    m1 = jnp.max(lg, -1, keepdims=True)
    i1 = jnp.min(jnp.where(lg == m1, lane, LANES), -1, keepdims=True)
    lg2 = jnp.where(lane == i1, neg, lg)
    m2 = jnp.max(lg2, -1, keepdims=True)
    i2 = jnp.min(jnp.where(lg2 == m2, lane, LANES), -1, keepdims=True)
    e2 = jnp.exp(m2 - m1)
    return h, lane, i1, i2, 1.0 / (1.0 + e2), e2 / (1.0 + e2)


def _moe_sparse_kernel(x_ref, mod_ref, wr_ref, w1_ref, w3_ref, w2_ref, lg_ref, lb_ref, o_ref,
                       h_scr, acc_scr, gate_scr, posc_scr, posr_scr, xs_scr, ye_scr, nblk_scr, *, lc, tm, nf, ns, blk):
    row0 = pl.program_id(1) * tm
    s = pl.program_id(2)
    e = s // nf
    f = s % nf

    @pl.when(s == 0)
    def _():
        h, lane, i1, i2, g1, g2 = _route(x_ref, mod_ref, wr_ref, row0, tm, lc)
        h_scr[...] = h.astype(BF16)
        acc_scr[...] = jnp.zeros(acc_scr.shape, F32)
        routed = jnp.where(jnp.logical_or(lane == i1, lane == i2), 1.0, 0.0)
        r = lax.broadcasted_iota(jnp.int32, (tm, tm), 0)
        c = lax.broadcasted_iota(jnp.int32, (tm, tm), 1)
        before = jnp.where(c < r, 1.0, 0.0).astype(BF16)
        rank_c = jnp.dot(before, routed.astype(BF16), preferred_element_type=F32)
        gate_scr[...] = jnp.where(lane == i1, g1, 0.0) + jnp.where(lane == i2, g2, 0.0)
        posc_scr[...] = jnp.where(routed > 0.0, rank_c, -1.0)
        routed_t = routed.T[:2 * SUBLANES]
        after = jnp.where(r < c, 1.0, 0.0).astype(BF16)
        rank_r = jnp.dot(routed_t.astype(BF16), after, preferred_element_type=F32)
        posr_scr[...] = jnp.where(routed_t > 0.0, rank_r, -1.0)
        counts = jnp.sum(routed, 0, keepdims=True)
        lane1 = lax.broadcasted_iota(jnp.int32, counts.shape, 1)
        for ex in range(N_EXPERTS):
            n_rows = jnp.sum(jnp.where(lane1 == ex, counts, 0.0)).astype(jnp.int32)
            nblk_scr[ex] = (n_rows + (blk - 1)) // blk

    nblk = nblk_scr[e]
    mv = MOE_MOVE_BLOCK
    nmv = (nblk * blk + (mv - 1)) // mv
    rows = lambda i: pl.ds(pl.multiple_of(i * mv, mv), mv)

    @pl.when(f == 0)
    def _():
        def gather(i, carry):
            slot = (lax.broadcasted_iota(jnp.int32, (mv, tm), 0) + i * mv).astype(F32)
            take = jnp.where(slot == posr_scr[pl.ds(e, 1), :], 1.0, 0.0).astype(BF16)
            xs_scr[rows(i), :] = jnp.dot(take, h_scr[...], preferred_element_type=F32).astype(BF16)
            ye_scr[rows(i), :] = jnp.zeros((mv, ye_scr.shape[1]), F32)
            return carry

        lax.fori_loop(0, nmv, gather, 0)

    def expert(m):
        xs = xs_scr[:m]
        a = jnp.dot(xs, w1_ref[0], preferred_element_type=F32)
        g = jnp.dot(xs, w3_ref[0], preferred_element_type=F32)
        ye_scr[:m] += jnp.dot((_silu(a) * g).astype(BF16), w2_ref[0], preferred_element_type=F32)

    for k in range(tm // blk):
        pl.when(nblk == k + 1)(functools.partial(expert, (k + 1) * blk))

    @pl.when(f == nf - 1)
    def _():
        lane = lax.broadcasted_iota(jnp.int32, (tm, LANES), 1)
        column = lambda ref: jnp.sum(jnp.where(lane == e, ref[...], 0.0), -1, keepdims=True)
        pos = column(posc_scr)
        gate = column(gate_scr)

        def scatter(i, carry):
            slot = (lax.broadcasted_iota(jnp.int32, (tm, mv), 1) + i * mv).astype(F32)
            put = jnp.where(slot == pos, 1.0, 0.0).astype(BF16)
            acc_scr[...] += gate * jnp.dot(put, ye_scr[rows(i), :].astype(BF16), preferred_element_type=F32)
            return carry

        lax.fori_loop(0, nmv, scatter, 0)

    @pl.when(s == ns - 1)
    def _():
        gate = _sel_mod(mod_ref, 5, row0, tm, lc)
        y = ALPHA * x_ref[0] + gate * acc_scr[...]
        o_ref[0] = _layer_norm(y) * lg_ref[...] + lb_ref[...]


def _moe_sparse(xx, mod, wr, w1, w3, w2, lg, lb, lc):
    b, lt, d = xx.shape
    ne, _, ff = w1.shape
    tm = _pick_tile(lt, (1280, 768, 512, 256))
    tf = _pick_tile(ff, (256, 128))
    nf = ff // tf
    ns = ne * nf
    return pl.pallas_call(
        functools.partial(_moe_sparse_kernel, lc=lc, tm=tm, nf=nf, ns=ns, blk=MOE_ROW_BLOCK),
        grid=(b, lt // tm, ns),
        in_specs=[pl.BlockSpec((1, tm, d), lambda bi, i, s: (bi, i, 0)),
                  pl.BlockSpec((1, 2, 6, d), lambda bi, i, s: (bi, 0, 0, 0)),
                  pl.BlockSpec((d, LANES), lambda bi, i, s: (0, 0)),
                  pl.BlockSpec((1, d, tf), lambda bi, i, s: (s // nf, 0, s % nf)),
                  pl.BlockSpec((1, d, tf), lambda bi, i, s: (s // nf, 0, s % nf)),
                  pl.BlockSpec((1, tf, d), lambda bi, i, s: (s // nf, s % nf, 0)),
                  pl.BlockSpec((1, d), lambda bi, i, s: (0, 0)),
                  pl.BlockSpec((1, d), lambda bi, i, s: (0, 0))],
        out_specs=pl.BlockSpec((1, tm, d), lambda bi, i, s: (bi, i, 0)),
        out_shape=jax.ShapeDtypeStruct((b, lt, d), F32),
        scratch_shapes=[pltpu.VMEM((tm, d), BF16), pltpu.VMEM((tm, d), F32),
                        pltpu.VMEM((tm, LANES), F32), pltpu.VMEM((tm, LANES), F32),
                        pltpu.VMEM((2 * SUBLANES, tm), F32),
                        pltpu.VMEM((tm, d), BF16), pltpu.VMEM((tm, d), F32),
                        pltpu.SMEM((ne,), jnp.int32)],
        compiler_params=_cparams(("parallel", "parallel", "arbitrary"), VMEM_LIMIT_MOE),
        name="moe_sparse",
    )(xx, mod, wr, w1, w3, w2, lg, lb)


def _rope_tables(l, lc):
    rows = l // GRID_W
    row = jnp.repeat(jnp.arange(rows, dtype=F32), GRID_W)
    col = jnp.tile(jnp.arange(GRID_W, dtype=F32), rows)
    n_freq = HEAD_DIM // 4
    inv_freq = ROPE_THETA ** (-jnp.arange(n_freq, dtype=F32) / n_freq)
    ang = jnp.concatenate([row[:, None] * inv_freq, col[:, None] * inv_freq], -1)
    cos, sin = jnp.cos(ang), jnp.sin(ang)
    cos2 = jnp.concatenate([jnp.ones((lc, LANES), F32), jnp.concatenate([cos, cos, cos, cos], -1)], 0)
    sin2 = jnp.concatenate([jnp.zeros((lc, LANES), F32), jnp.concatenate([-sin, sin, -sin, sin], -1)], 0)
    return cos2, sin2


def _hy_features(l):
    bands = (HY_EMB - 1) // 2
    t = jnp.linspace(0.0, 1.0, l, dtype=F32)[:, None]
    f = jnp.linspace(1e-4, bands - 1, bands, dtype=F32)[None, :]
    wt = 2.0 * math.pi * jnp.arange(l, dtype=F32)[:, None] / l
    z = jnp.concatenate([t, jnp.cos(f * wt), -jnp.sin(f * wt)], -1)
    return jnp.pad(z, ((0, 0), (0, LANES - HY_EMB)))


def _angle(idx, n):
    return (2.0 * math.pi / n) * (idx % n).astype(F32)


def _dft_tables(n1):
    nh = n1 // 2
    n2 = DFT_N2
    n = n1 * n2
    k1 = jnp.arange(n1, dtype=jnp.int32)
    a1 = _angle(k1[:, None] * jnp.arange(nh, dtype=jnp.int32)[None, :], n1)
    fstack = jnp.concatenate([jnp.cos(a1), -jnp.sin(a1)], 0)
    cstack = jnp.concatenate([jnp.cos(a1.T), jnp.sin(a1.T)], 0)
    k2 = jnp.arange(n2, dtype=jnp.int32)
    at = _angle(k1[:, None] * k2[None, :], n)
    tr, ti = jnp.cos(at), -jnp.sin(at)
    a2 = _angle(k2[:, None] * k2[None, :], n2)
    fr, fi = jnp.cos(a2), -jnp.sin(a2)
    g = jnp.stack([tr[:, None, :] * fr[None] - ti[:, None, :] * fi[None],
                   tr[:, None, :] * fi[None] + ti[:, None, :] * fr[None]], 1)
    return fstack, cstack, g


def _dense_dft_tables(lc):
    n = 2 * lc
    a = _angle(jnp.arange(n, dtype=jnp.int32)[:, None] * jnp.arange(lc, dtype=jnp.int32)[None, :], n)
    ff = jnp.concatenate([jnp.cos(a), -jnp.sin(a)], 0)
    ci = jnp.concatenate([jnp.cos(a.T), jnp.sin(a.T)], 0)
    return ff, ci


def kernel(x, c, ctx, c_ctx, ada_w, ada_b, w_in, w_out, q_gain, k_gain, rwkv_mu, rwkv_w0, rwkv_wB, rwkv_a0, rwkv_aB, rwkv_gB, rwkv_kk, rwkv_ka, rwkv_rk, rwkv_gn_g, rwkv_gn_b, hy_short_w, hy_short_b, hy_w1, hy_b1, hy_freq1, hy_w2, hy_b2, hy_freq2, hy_w3, hy_decay, hy_bias, ln1_g, ln1_b, ln2_g, ln2_b, ffn_w1, ffn_w3, ffn_w2, moe_router, moe_w1, moe_w3, moe_w2):
    b, l, d = x.shape
    lc = ctx.shape[1]
    lt = lc + l
    depth = ada_w.shape[0]
    assert b == 2, "the long convolution packs the two batch rows as one complex signal"
    assert d == D_MODEL and lc % TOK_TILE == 0 and l % TOK_TILE == 0 and (2 * l) % (2 * DFT_N2) == 0
    cw = RWKV_WIDTH

    xx = jnp.concatenate([ctx, x], 1)
    cond8 = jnp.zeros((SUBLANES, d), F32).at[:b].set(c).at[b].set(c_ctx)
    mod_all = _ada_mod(cond8, ada_w, ada_b)

    cos64, sin64 = _rope_tables(l, lc)
    n1 = 2 * l // DFT_N2
    nh = n1 // 2
    cols = DFT_N2 * HY_WIDTH
    fstack, cstack, g_tab = (t.astype(BF16) for t in _dft_tables(n1))
    ff_c, ci_c = _dense_dft_tables(lc)
    feat_l = _hy_features(l)
    feat_c = _hy_features(lc)
    blk = jnp.arange(cw) // HEAD_DIM
    bd = (blk[:, None] == blk[None, :]).astype(F32)
    ch = jnp.arange(ATT_WIDTH)
    bd_att = (ch[:, None] // HEAD_DIM == ch[None, :] // HEAD_DIM).astype(BF16)
    swap_att = (ch[:, None] == (ch[None, :] + HALF_HD) % HEAD_DIM + (ch[None, :] // HEAD_DIM) * HEAD_DIM
                ).astype(BF16)
    perm64 = jnp.concatenate([jnp.arange(0, HEAD_DIM, 2), jnp.arange(1, HEAD_DIM, 2)])
    perm_att = jnp.concatenate([h * HEAD_DIM + perm64 for h in range(ATT_HEADS + ATT_KV_HEADS)]
                               + [jnp.arange(ATT_WIDTH + ATT_KV_WIDTH, IN_ATT)])
    tq = TOK_TILE
    qn = next(n for n in (4, 2, 1) if (l // tq) % n == 0)
    tk = _pick_tile(lt, (3328, 1280, 1024, 768, 512, 256))

    for li in range(depth):
        ml = mod_all[li]
        mod = jnp.stack([jnp.broadcast_to(ml[b].reshape(1, 6, d), (b, 6, d)), ml[:b].reshape(b, 6, d)], 1)
        wi = w_in[li]
        w_pad = jnp.concatenate([wi[:, :IN_ATT][:, perm_att], wi[:, IN_ATT:IN_ATT + IN_RWKV],
                                 jnp.zeros((d, IN_RWKV_PAD - IN_RWKV), F32), wi[:, IN_ATT + IN_RWKV:]],
                                1).astype(BF16)
        ua, ur, uh = _inproj(xx, mod, w_pad, lc)

        two = lambda gain: jnp.tile(gain[perm64], 2)[None]
        qt, kx, vt = _attn_prep(ua, cos64, sin64, two(q_gain[li]), two(k_gain[li]), bd_att, swap_att, lc)
        att_c = _flash(qt, kx, vt, tq, TOK_TILE, 1, 0, lc // tq, lc // TOK_TILE)
        s_bound = (HEAD_DIM ** 0.5) * LOG2E * jnp.max(jnp.abs(q_gain[li])) * jnp.max(jnp.abs(k_gain[li]))
        lat_args = (qt, kx, vt, tq, tk, qn, lc // tq, l // (qn * tq), lt // tk)
        att_l = lax.cond(s_bound <= MAX_UNSHIFTED_SCORE,
                         lambda: _flash(*lat_args, bounded=True), lambda: _flash(*lat_args, bounded=False))

        wl = jnp.zeros((LANES, 5 * cw), F32)
        wl = wl.at[0:W_LORA, 0:cw].set(rwkv_wB[li, 0]).at[W_LORA:2 * W_LORA, cw:2 * cw].set(rwkv_wB[li, 1])
        o_a = 2 * W_LORA
        wl = wl.at[o_a:o_a + A_LORA, 2 * cw:3 * cw].set(rwkv_aB[li, 0])
        wl = wl.at[o_a + A_LORA:o_a + 2 * A_LORA, 3 * cw:4 * cw].set(rwkv_aB[li, 1])
        o_g = o_a + 2 * A_LORA
        wl = wl.at[o_g:o_g + G_LORA, 4 * cw:5 * cw].set(rwkv_gB[li])
        mu = jnp.pad(rwkv_mu[li], ((0, 0), (0, IN_RWKV_PAD - IN_RWKV)))
        r_, v_, kk_, g_, bon_, lw_, be_, kd_ = _rwkv_prep(
            ur, mu, wl, bd, rwkv_w0[li], rwkv_a0[li], rwkv_kk[li][None], rwkv_ka[li][None],
            rwkv_rk[li].reshape(1, cw), lc)
        yf, yb = _wkv_scan(r_, v_, kk_, lw_, be_, kd_, lc)
        rw = _rwkv_out(yf, yb, bon_, g_, bd, rwkv_gn_g[li][None], rwkv_gn_b[li][None])

        x1, x2, vv = _hy_prep(uh, hy_short_w[li], hy_short_b[li][None], lc)
        w1p = jnp.pad(hy_w1[li], ((0, LANES - HY_EMB), (0, 0)))
        fargs = (w1p, hy_b1[li][None], hy_freq1[li][None], hy_w2[li], hy_b2[li][None], hy_freq2[li][None],
                 hy_w3[li], hy_decay[li][None])
        fw, bw, nrm = _hy_filter(feat_l, *fargs)
        c2 = HY_ORDER * HY_WIDTH
        a4 = _dft_cols(fstack, fw.reshape(1, nh, DFT_N2 * c2), 0, bw.reshape(1, nh, DFT_N2 * c2), 0, True)
        kspec = _spec(a4.reshape(4, n1, DFT_N2, c2), g_tab, nrm, n1 * DFT_N2)
        lat = lambda t: t[:, lc:].reshape(b, nh, cols)
        x1l, x2l, zin = lat(x1), lat(x2), lat(vv)
        for o, gate in enumerate((x1l, x2l)):
            a = _dft_cols(fstack, zin, 0, zin, 1, False)
            bv = _conv_mid(a.reshape(2, n1, DFT_N2, HY_WIDTH), g_tab, kspec, o)
            bias_cols = jnp.tile(hy_bias[li, o], DFT_N2)[None]
            zin = _idft_cols(cstack, bv.reshape(2, n1, cols), gate, zin, bias_cols)
        hy_l = zin.reshape(b, l, HY_WIDTH)
        fw_c, bw_c, nrm_c = _hy_filter(feat_c, *fargs)
        hy_c = _hy_ctx(x1[:, :lc], x2[:, :lc], vv[:, :lc], fw_c, bw_c, nrm_c, hy_bias[li], ff_c, ci_c)

        xx = _outproj(att_c, att_l, rw, hy_c, hy_l, xx, mod, w_out[li].astype(BF16),
                      ln1_g[li][None], ln1_b[li][None], lc)

        j = li // 2
        if li % 2 == 0:
            xx = _ffn(xx, mod, ffn_w1[j].astype(BF16), ffn_w3[j].astype(BF16), ffn_w2[j].astype(BF16),
                      ln2_g[li][None], ln2_b[li][None], lc)
        else:
            wr = jnp.pad(moe_router[j], ((0, 0), (0, LANES - N_EXPERTS)))
            xx = _moe_sparse(xx, mod, wr, moe_w1[j].astype(BF16), moe_w3[j].astype(BF16), moe_w2[j].astype(BF16),
                             ln2_g[li][None], ln2_b[li][None], lc)
    return xx[:, lc:]
```

```python
import functools
import math

import jax
import jax.numpy as jnp
from jax import lax
from jax.experimental import pallas as pl
from jax.experimental.pallas import tpu as pltpu

F32 = jnp.float32
BF16 = jnp.bfloat16
HI = lax.Precision.HIGHEST

D_MODEL = 1024
DEPTH = 2
GRID_W = 64
HEAD_DIM = 64
HALF_HD = HEAD_DIM // 2
ATT_WIDTH = 512
RWKV_WIDTH = 256
HY_WIDTH = 256
ATT_HEADS = 8
ATT_KV_HEADS = 2
ATT_REP = 4
ATT_KV_WIDTH = 128
ROPE_THETA = 10000.0
QK_EPS = 1e-6
RWKV_HEADS = 4
W_LORA = 16
A_LORA = 16
G_LORA = 32
RWKV_GN_EPS = 64e-5
HY_ORDER = 2
HY_EMB = 33
HY_FFN = 64
N_EXPERTS = 8
LN_EPS = 1e-6
IN_ATT = ATT_WIDTH + 2 * ATT_KV_WIDTH
IN_RWKV = 3 * RWKV_WIDTH + 2 * W_LORA + 2 * A_LORA + G_LORA
IN_RWKV_PAD = 896
IN_HY = 3 * HY_WIDTH
ALPHA = float((2 * DEPTH) ** 0.25)
LOG2E = 1.4426950408889634
MAX_UNSHIFTED_SCORE = 40.0

LANES = 128
SUBLANES = 8
TOK_TILE = 256
ROW_TILE_MID = 640
ROW_TILE_WIDE = 1280
WKV_CHUNK = 64
WKV_TILE = 256
DFT_N2 = 256
MOE_ROW_BLOCK = 128
MOE_MOVE_BLOCK = 256
VMEM_LIMIT = 48 * 1024 * 1024
VMEM_LIMIT_MOE = 56 * 1024 * 1024


def _cparams(sem, vmem=VMEM_LIMIT):
    return pltpu.CompilerParams(dimension_semantics=sem, vmem_limit_bytes=vmem)


def _pick_tile(n, cands):
    for c in cands:
        if n % c == 0:
            return c
    raise ValueError(f"no tile for {n} in {cands}")


def _layer_norm(x):
    mu = jnp.mean(x, -1, keepdims=True)
    xc = x - mu
    var = jnp.mean(xc * xc, -1, keepdims=True)
    return xc * lax.rsqrt(var + LN_EPS)


def _sel_mod(mod_ref, j, row0, tm, lc):
    rid = row0 + lax.broadcasted_iota(jnp.int32, (tm, 1), 0)
    return jnp.where(rid < lc, mod_ref[0, 0, j:j + 1, :], mod_ref[0, 1, j:j + 1, :])


def _silu(x):
    return x * jax.nn.sigmoid(x)


def _ada_kernel(c_ref, w_ref, b_ref, o_ref):
    s = _silu(c_ref[...])
    o_ref[0] = _dotp(s, w_ref[0], 3) + b_ref[0]


def _ada_mod(cond8, ada_w, ada_b):
    depth, d, n = ada_w.shape
    tn = _pick_tile(n, (1536, 1024, 512, 256, 128))
    return pl.pallas_call(
        _ada_kernel,
        grid=(depth, n // tn),
        in_specs=[pl.BlockSpec((SUBLANES, d), lambda l, j: (0, 0)),
                  pl.BlockSpec((1, d, tn), lambda l, j: (l, 0, j)),
                  pl.BlockSpec((1, 1, tn), lambda l, j: (l, 0, j))],
        out_specs=pl.BlockSpec((1, SUBLANES, tn), lambda l, j: (l, 0, j)),
        out_shape=jax.ShapeDtypeStruct((depth, SUBLANES, n), F32),
        compiler_params=_cparams(("parallel", "parallel")),
        name="ada_mod",
    )(cond8, ada_w, ada_b.reshape(depth, 1, n))


def _inproj_kernel(x_ref, mod_ref, w_ref, oa_ref, or_ref, oh_ref, *, lc, tm):
    row0 = pl.program_id(1) * tm
    sh = _sel_mod(mod_ref, 0, row0, tm, lc)
    sc = _sel_mod(mod_ref, 1, row0, tm, lc)
    h = (_layer_norm(x_ref[0]) * (1.0 + sc) + sh).astype(BF16)
    u = jnp.dot(h, w_ref[...], preferred_element_type=F32)
    oa_ref[0] = u[:, :IN_ATT]
    or_ref[0] = u[:, IN_ATT:IN_ATT + IN_RWKV_PAD]
    oh_ref[0] = u[:, IN_ATT + IN_RWKV_PAD:]


def _inproj(xx, mod, w_pad, lc):
    b, lt, d = xx.shape
    tm = _pick_tile(lt, (ROW_TILE_MID, TOK_TILE))
    n = w_pad.shape[1]
    return pl.pallas_call(
        functools.partial(_inproj_kernel, lc=lc, tm=tm),
        grid=(b, lt // tm),
        in_specs=[pl.BlockSpec((1, tm, d), lambda bi, i: (bi, i, 0)),
                  pl.BlockSpec((1, 2, 6, d), lambda bi, i: (bi, 0, 0, 0)),
                  pl.BlockSpec((d, n), lambda bi, i: (0, 0))],
        out_specs=[pl.BlockSpec((1, tm, IN_ATT), lambda bi, i: (bi, i, 0)),
                   pl.BlockSpec((1, tm, IN_RWKV_PAD), lambda bi, i: (bi, i, 0)),
                   pl.BlockSpec((1, tm, IN_HY), lambda bi, i: (bi, i, 0))],
        out_shape=[jax.ShapeDtypeStruct((b, lt, IN_ATT), F32),
                   jax.ShapeDtypeStruct((b, lt, IN_RWKV_PAD), F32),
                   jax.ShapeDtypeStruct((b, lt, IN_HY), F32)],
        compiler_params=_cparams(("parallel", "parallel")),
        name="inproj",
    )(xx, mod, w_pad)


def _attn_prep_kernel(u_ref, cos_ref, sin_ref, qg_ref, kg_ref, bd_ref, sw_ref, qt_ref, k_ref, vt_ref):
    u = u_ref[0]

    def norm_rope(x, g):
        w = x.shape[1]
        tile = lambda t: jnp.concatenate([t] * (w // LANES), -1)
        ms = _dot01(x * x, bd_ref[:w, :w]) * (1.0 / HEAD_DIM)
        xn = x * lax.rsqrt(ms + QK_EPS) * tile(g)
        sw = _dot01(xn, sw_ref[:w, :w])
        return xn * tile(cos_ref[...]) + sw * tile(sin_ref[...])

    q = norm_rope(u[:, :ATT_WIDTH], qg_ref[...]) * (LOG2E * HEAD_DIM ** -0.5)
    qt = q.T
    kx = norm_rope(u[:, ATT_WIDTH:ATT_WIDTH + ATT_KV_WIDTH], kg_ref[...])
    tq = TOK_TILE
    for g in range(ATT_KV_HEADS):
        base = g * ATT_REP * HEAD_DIM
        for t in range(u.shape[0] // tq):
            qt_ref[0, g, t] = jnp.concatenate(
                [qt[base + r * HEAD_DIM:base + (r + 1) * HEAD_DIM, t * tq:(t + 1) * tq] for r in range(ATT_REP)],
                -1).astype(BF16)
        k_ref[0, g] = kx[:, g * HEAD_DIM:(g + 1) * HEAD_DIM].astype(BF16)
    v0 = ATT_WIDTH + ATT_KV_WIDTH
    vt = u[:, v0:v0 + ATT_KV_WIDTH].T
    for g in range(ATT_KV_HEADS):
        vt_ref[0, g] = vt[g * HEAD_DIM:(g + 1) * HEAD_DIM].astype(BF16)


def _attn_prep(ua, cos, sin, qg, kg, bd_att, swap_att, lc):
    b, lt, _ = ua.shape
    tq = TOK_TILE
    tm = _pick_tile(lt, (ROW_TILE_WIDE, ROW_TILE_MID, TOK_TILE))
    return pl.pallas_call(
        _attn_prep_kernel,
        grid=(b, lt // tm),
        in_specs=[pl.BlockSpec((1, tm, IN_ATT), lambda bi, i: (bi, i, 0)),
                  pl.BlockSpec((tm, LANES), lambda bi, i: (i, 0)),
                  pl.BlockSpec((tm, LANES), lambda bi, i: (i, 0)),
                  pl.BlockSpec((1, LANES), lambda bi, i: (0, 0)),
                  pl.BlockSpec((1, LANES), lambda bi, i: (0, 0)),
                  pl.BlockSpec((ATT_WIDTH, ATT_WIDTH), lambda bi, i: (0, 0)),
                  pl.BlockSpec((ATT_WIDTH, ATT_WIDTH), lambda bi, i: (0, 0))],
        out_specs=[pl.BlockSpec((1, ATT_KV_HEADS, tm // tq, HEAD_DIM, ATT_REP * tq),
                                lambda bi, i: (bi, 0, i, 0, 0)),
                   pl.BlockSpec((1, ATT_KV_HEADS, tm, HEAD_DIM), lambda bi, i: (bi, 0, i, 0)),
                   pl.BlockSpec((1, ATT_KV_HEADS, HEAD_DIM, tm), lambda bi, i: (bi, 0, 0, i))],
        out_shape=[jax.ShapeDtypeStruct((b, ATT_KV_HEADS, lt // tq, HEAD_DIM, ATT_REP * tq), BF16),
                   jax.ShapeDtypeStruct((b, ATT_KV_HEADS, lt, HEAD_DIM), BF16),
                   jax.ShapeDtypeStruct((b, ATT_KV_HEADS, HEAD_DIM, lt), BF16)],
        compiler_params=_cparams(("parallel", "parallel")),
        name="attn_prep",
    )(ua, cos, sin, qg, kg, bd_att, swap_att)


def _flash_kernel(qt_refs, k_ref, vt_ref, o_ref, m_scr, l_scr, acc_scr, *, nk, tq, sub, qn):
    j = pl.program_id(3)

    @pl.when(j == 0)
    def _():
        m_scr[...] = jnp.full(m_scr.shape, -jnp.inf, F32)
        l_scr[...] = jnp.zeros(l_scr.shape, F32)
        acc_scr[...] = jnp.zeros(acc_scr.shape, F32)

    qt = jnp.concatenate([r[0, 0, 0] for r in qt_refs], -1)
    nsub = k_ref.shape[2] // sub
    m = m_scr[...]
    l = l_scr[...]
    acc = acc_scr[...]
    scores = lambda c: jnp.dot(k_ref[0, 0, c * sub:(c + 1) * sub, :], qt, preferred_element_type=F32)
    pv = lambda c, p: jnp.dot(vt_ref[0, 0, :, c * sub:(c + 1) * sub], p, preferred_element_type=F32)
    s_next = scores(0)
    pend = None
    for c in range(nsub):
        s = s_next
        if c + 1 < nsub:
            s_next = scores(c + 1)
        if pend is not None:
            acc = pend[0] * acc + pv(c - 1, pend[1])
        m_new = jnp.maximum(m, jnp.max(s, 0, keepdims=True))
        a = jnp.exp2(m - m_new)
        p = jnp.exp2(s - m_new)
        l = a * l + jnp.sum(p, 0, keepdims=True)
        pend = (a, p.astype(BF16))
        m = m_new
    acc = pend[0] * acc + pv(nsub - 1, pend[1])
    m_scr[...] = m
    l_scr[...] = l
    acc_scr[...] = acc

    @pl.when(j == nk - 1)
    def _():
        o = (acc / l).T
        for t in range(qn):
            o_ref[0, t * tq:(t + 1) * tq, :] = jnp.concatenate(
                [o[(t * ATT_REP + r) * tq:(t * ATT_REP + r + 1) * tq] for r in range(ATT_REP)], -1
            ).astype(o_ref.dtype)


def _flash_bounded_kernel(qt_refs, k_ref, vt_ref, o_ref, l_scr, acc_scr, *, nk, tq, sub, qn):
    j = pl.program_id(3)

    @pl.when(j == 0)
    def _():
        l_scr[...] = jnp.zeros(l_scr.shape, F32)
        acc_scr[...] = jnp.zeros(acc_scr.shape, F32)

    qt = jnp.concatenate([r[0, 0, 0] for r in qt_refs], -1)
    nsub = k_ref.shape[2] // sub

    l = l_scr[...]
    acc = acc_scr[...]
    scores = lambda c: jnp.dot(k_ref[0, 0, c * sub:(c + 1) * sub, :], qt, preferred_element_type=F32)
    s_next = scores(0)
    for c in range(nsub):
        s = s_next
        if c + 1 < nsub:
            s_next = scores(c + 1)
        p = jnp.exp2(s)
        l = l + jnp.sum(p.reshape(sub // SUBLANES, SUBLANES, p.shape[1]), 0)
        acc = acc + jnp.dot(vt_ref[0, 0, :, c * sub:(c + 1) * sub], p.astype(BF16), preferred_element_type=F32)
    l_scr[...] = l
    acc_scr[...] = acc

    @pl.when(j == nk - 1)
    def _():
        o = (acc / jnp.sum(l, 0, keepdims=True)).T
        for t in range(qn):
            o_ref[0, t * tq:(t + 1) * tq, :] = jnp.concatenate(
                [o[(t * ATT_REP + r) * tq:(t * ATT_REP + r + 1) * tq] for r in range(ATT_REP)], -1
            ).astype(o_ref.dtype)


def _flash(qt, k, vt, tq, tk, qn, q_tile0, nq, nk, bounded=False):
    b = qt.shape[0]
    lq = nq * qn * tq
    sub = _pick_tile(tk, (256, 128))
    lanes = qn * ATT_REP * tq
    if bounded:
        kern = functools.partial(_flash_bounded_kernel, nk=nk, tq=tq, sub=sub, qn=qn)
        scratch = [pltpu.VMEM((SUBLANES, lanes), F32), pltpu.VMEM((HEAD_DIM, lanes), F32)]
    else:
        kern = functools.partial(_flash_kernel, nk=nk, tq=tq, sub=sub, qn=qn)
        scratch = [pltpu.VMEM((1, lanes), F32), pltpu.VMEM((1, lanes), F32), pltpu.VMEM((HEAD_DIM, lanes), F32)]
    q_spec = lambda t: pl.BlockSpec((1, 1, 1, HEAD_DIM, ATT_REP * tq),
                                    lambda bi, g, i, j: (bi, g, q_tile0 + i * qn + t, 0, 0))
    return pl.pallas_call(
        lambda *refs: kern(refs[:qn], *refs[qn:]),
        grid=(b, ATT_KV_HEADS, nq, nk),
        in_specs=[q_spec(t) for t in range(qn)] + [
                  pl.BlockSpec((1, 1, tk, HEAD_DIM), lambda bi, g, i, j: (bi, g, j, 0)),
                  pl.BlockSpec((1, 1, HEAD_DIM, tk), lambda bi, g, i, j: (bi, g, 0, j))],
        out_specs=pl.BlockSpec((1, qn * tq, ATT_REP * HEAD_DIM), lambda bi, g, i, j: (bi, i, g)),
        out_shape=jax.ShapeDtypeStruct((b, lq, ATT_WIDTH), BF16),
        scratch_shapes=scratch,
        compiler_params=_cparams(("parallel", "parallel", "parallel", "arbitrary")),
        name="flash_bounded" if bounded else "flash",
    )(*([qt] * qn), k, vt)


def _prev_next(u, up8, un8, i, tm, lc, lt):
    rid = lax.broadcasted_iota(jnp.int32, u.shape, 0)
    gid = rid + i * tm
    prev = jnp.where(rid == 0, up8[SUBLANES - 1:SUBLANES], pltpu.roll(u, 1, 0))
    prev = jnp.where(gid == 0, 0.0, jnp.where(gid == lc, 0.0, prev))
    nxt = jnp.where(rid == tm - 1, un8[0:1], pltpu.roll(u, tm - 1, 0))
    nxt = jnp.where(gid == lc - 1, 0.0, jnp.where(gid == lt - 1, 0.0, nxt))
    return prev, nxt


def _halo_specs(tm, c, lt):
    r = tm // SUBLANES
    last = lt // SUBLANES - 1
    return [pl.BlockSpec((1, tm, c), lambda bi, i: (bi, i, 0)),
            pl.BlockSpec((1, SUBLANES, c), lambda bi, i: (bi, jnp.maximum(i * r - 1, 0), 0)),
            pl.BlockSpec((1, SUBLANES, c), lambda bi, i: (bi, jnp.minimum((i + 1) * r, last), 0))]


def _softplus(z):
    return jnp.maximum(z, 0.0) + jnp.log1p(jnp.exp(-jnp.abs(z)))


def _rwkv_prep_kernel(u_ref, up_ref, un_ref, mu_ref, wl_ref, bd_ref, w0_ref, a0_ref, kkw_ref, ka_ref, rk_ref,
                      r_o, v_o, kk_o, g_o, bon_o, lw_o, be_o, kd_o, *, lc, lt, tm):
    i = pl.program_id(1)
    u = u_ref[0]
    prev, nxt = _prev_next(u, up_ref[0], un_ref[0], i, tm, lc, lt)
    us = u + mu_ref[0:1] * (prev - u) + mu_ref[1:2] * (nxt - u)
    c = RWKV_WIDTH
    r = us[:, 0:c]
    k = us[:, c:2 * c]
    v = us[:, 2 * c:3 * c]
    slab = us[:, 3 * c:3 * c + LANES]
    lane = lax.broadcasted_iota(jnp.int32, slab.shape, 1)
    o_a = 2 * W_LORA
    o_g = o_a + 2 * A_LORA
    act = jnp.where(lane < o_a, jnp.tanh(slab),
                    jnp.where(lane < o_g, slab,
                              jnp.where(lane < o_g + G_LORA, jax.nn.sigmoid(slab), 0.0)))
    lo = _dotp(act, wl_ref[...], 3)
    bd = bd_ref[...]
    kk0 = k * kkw_ref[...]
    kk = kk0 * lax.rsqrt(_dot01(kk0 * kk0, bd) + 1e-12)
    r_o[0] = r
    v_o[0] = v
    kk_o[0] = kk
    g_o[0] = lo[:, 4 * c:5 * c]
    bon = jnp.zeros_like(r)
    for d in range(2):
        w_raw = w0_ref[d:d + 1] + lo[:, d * c:(d + 1) * c]
        lw = -jnp.exp(-_softplus(-w_raw) - 0.5)
        a = jax.nn.sigmoid(a0_ref[d:d + 1] + lo[:, (2 + d) * c:(3 + d) * c])
        kd = k * (1.0 + (a - 1.0) * ka_ref[...])
        lw_o[0, d] = lw
        be_o[0, d] = a * kk
        kd_o[0, d] = kd
        bon = bon + r * kd * rk_ref[...]
    bon_o[0] = _dot01(bon, bd) * v


def _rwkv_prep(ur, mu, wl, bd, w0, a0, kkw, ka, rk, lc):
    b, lt, cp = ur.shape
    tm = _pick_tile(lt, (ROW_TILE_MID, TOK_TILE))
    c = RWKV_WIDTH
    full = lambda shape: pl.BlockSpec(shape, lambda bi, i: (0,) * len(shape))
    tok = pl.BlockSpec((1, tm, c), lambda bi, i: (bi, i, 0))
    tok2 = pl.BlockSpec((1, 2, tm, c), lambda bi, i: (bi, 0, i, 0))
    s1 = jax.ShapeDtypeStruct((b, lt, c), F32)
    s2 = jax.ShapeDtypeStruct((b, 2, lt, c), F32)
    return pl.pallas_call(
        functools.partial(_rwkv_prep_kernel, lc=lc, lt=lt, tm=tm),
        grid=(b, lt // tm),
        in_specs=_halo_specs(tm, cp, lt) + [full((2, cp)), full((LANES, 5 * c)), full((c, c)), full((2, c)),
                                            full((2, c)), full((1, c)), full((1, c)), full((1, c))],
        out_specs=[tok, tok, tok, tok, tok, tok2, tok2, tok2],
        out_shape=[s1, s1, s1, s1, s1, s2, s2, s2],
        compiler_params=_cparams(("parallel", "parallel")),
        name="rwkv_prep",
    )(ur, ur, ur, mu, wl, bd, w0, a0, kkw, ka, rk)


def _mm(a, b):
    return jnp.dot(a, b, precision=HI, preferred_element_type=F32)


_NN = ((1,), (0,))
_NT = ((1,), (1,))
_TN = ((0,), (0,))


def _split2(a):
    hi = a.astype(BF16)
    return hi, (a - hi.astype(F32)).astype(BF16)


def _dotp(a, b, passes, dims=_NN):
    if a.ndim == 3:
        dn = (((dims[0][0] + 1,), (dims[1][0] + 1,)), ((0,), (0,)))
    else:
        dn = (dims, ((), ()))
    dg = lambda p, q: lax.dot_general(p, q, dn, preferred_element_type=F32)
    if passes == 1:
        return dg(a.astype(BF16), b.astype(BF16))
    ah, al = _split2(a)
    bh, bl = _split2(b)
    return dg(ah, bh) + dg(ah, bl) + dg(al, bh)


def _dot01(a, ones):
    ah, al = _split2(a)
    o = ones.astype(BF16)
    return jnp.dot(ah, o, preferred_element_type=F32) + jnp.dot(al, o, preferred_element_type=F32)


P_M = 1
P_INV = 1
P_W = 1
P_Z = 1
P_STATE = 1
P_DFT = 1


def _unit_tri_inv(a_mat, row, col, eye, passes):
    t = a_mat.shape[-1]
    eye_f = eye.astype(F32)
    base = SUBLANES
    same = (row // base) == (col // base)
    n1 = -jnp.where(same, a_mat, 0.0)
    n2 = _dotp(n1, n1, passes)
    n4 = _dotp(n2, n2, passes)
    x = _dotp(_dotp(eye_f + n1, eye_f + n2, passes), eye_f + n4, passes)
    m = base
    while m < t:
        off = jnp.logical_and((row // (2 * m)) == (col // (2 * m)), (row // m) != (col // m))
        x = x - _dotp(x, _dotp(jnp.where(off, a_mat, 0.0), x, passes), passes)
        m *= 2
    return x


def _wkv_kernel(rf_ref, vf_ref, kkf_ref, rb_ref, vb_ref, kkb_ref, lwf_ref, bef_ref, kdf_ref,
                lwb_ref, beb_ref, kdb_ref, yf_ref, yb_ref, h_scr, pq_scr, ry_scr):
    t = WKV_CHUNK
    n = HEAD_DIM
    g = WKV_TILE // WKV_CHUNK
    tt = WKV_TILE
    nh = RWKV_HEADS
    nd = g * nh
    nu = 2 * nd
    orders = (list(range(g)), list(range(g - 1, -1, -1)))

    @pl.when(pl.program_id(1) == 0)
    def _():
        h_scr[...] = jnp.zeros(h_scr.shape, F32)
        pq_scr[...] = jnp.zeros(pq_scr.shape, F32)
        ry_scr[...] = jnp.zeros(ry_scr.shape, F32)

    hm = h_scr[...]
    for p in range(g):
        both = lambda ref: jnp.concatenate([ref[d * nd + p * nh:d * nd + (p + 1) * nh] for d in range(2)], 0)
        ry = both(ry_scr)
        pq = both(pq_scr)
        y = _dotp(ry[:, :, :n], hm, P_STATE) + ry[:, :, n:]
        hm = _dotp(pq[:, :, :n], hm, P_STATE) + pq[:, :, n:]
        for d, y_ref in enumerate((yf_ref, yb_ref)):
            c = orders[d][p]
            y_ref[0, c * t:(c + 1) * t, :] = jnp.concatenate([y[d * nh + h] for h in range(nh)], -1)
    h_scr[...] = hm

    row = lax.broadcasted_iota(jnp.int32, (tt, tt), 0)
    col = lax.broadcasted_iota(jnp.int32, (tt, tt), 1)
    same = (row // t) == (col // t)

    def scaled(d, r_ref, v_ref, kk_ref, lw_ref, be_ref, kd_ref):
        tri = jnp.logical_and(same, (row >= col) if d == 0 else (row <= col))
        sums = jnp.concatenate([jnp.where(tri, 1.0, 0.0), jnp.where(same, 1.0, 0.0)], 0).astype(BF16)
        lw = lw_ref[0, 0]
        l1 = lw.astype(BF16)
        rem = lw - l1.astype(F32)
        l2 = rem.astype(BF16)
        l3 = (rem - l2.astype(F32)).astype(BF16)
        cc = (jnp.dot(sums, l1, preferred_element_type=F32) + jnp.dot(sums, l2, preferred_element_type=F32)
              + jnp.dot(sums, l3, preferred_element_type=F32))
        cum = cc[:tt]
        ctot = cc[tt:]
        e_neg = jnp.exp(-cum)
        e_end = jnp.exp(ctot - cum)
        be = be_ref[0, 0]
        kd = kd_ref[0, 0]

        def units(x, rows=t):
            return jnp.stack([x[orders[d][p] * t:orders[d][p] * t + rows, h * n:(h + 1) * n]
                              for p in range(g) for h in range(nh)], 0)

        return dict(kap=units(kk_ref[0] * jnp.exp(cum - lw)), rt=units(r_ref[0] * jnp.exp(cum)), vh=units(v_ref[0]),
                    bet=units(be * e_neg), kdt=units(kd * e_neg), beh=units(be * e_end), kdh=units(kd * e_end),
                    gend=units(jnp.exp(ctot), 1))

    parts = (scaled(0, rf_ref, vf_ref, kkf_ref, lwf_ref, bef_ref, kdf_ref),
             scaled(1, rb_ref, vb_ref, kkb_ref, lwb_ref, beb_ref, kdb_ref))
    cat = lambda key: jnp.concatenate([parts[0][key], parts[1][key]], 0)
    kap, rt, vh = cat("kap"), cat("rt"), cat("vh")
    r64 = lax.broadcasted_iota(jnp.int32, (t, t), 0)
    c64 = lax.broadcasted_iota(jnp.int32, (t, t), 1)
    eye = r64 == c64
    unit = lax.broadcasted_iota(jnp.int32, (nu, t, t), 0)
    ahead = (r64 - c64)[None] * jnp.where(unit < nd, 1, -1)
    strict = ahead > 0
    incl = ahead >= 0
    m = _dotp(jnp.concatenate([kap, rt], 1), jnp.concatenate([cat("bet"), cat("kdt")], 1), P_M, _NT)
    a_mat = jnp.where(strict, m[:, :t, :t], 0.0)
    b_mat = jnp.where(strict, m[:, :t, t:], 0.0)
    ab_r = jnp.concatenate([jnp.where(incl, m[:, t:, :t], 0.0), jnp.where(incl, m[:, t:, t:], 0.0)], 2)
    tinv = _unit_tri_inv(a_mat, r64, c64, eye, P_INV)
    w = _dotp(tinv, jnp.concatenate([kap, _dotp(b_mat, vh, P_W)], 2), P_W)
    z = jnp.concatenate([-w, jnp.concatenate([jnp.zeros((nu, t, n), F32), vh], 2)], 1)
    ry = _dotp(ab_r, z, P_Z)
    ry_scr[:, :, :n] = ry[:, :, :n] + rt
    ry_scr[:, :, n:] = ry[:, :, n:]
    pq = _dotp(jnp.concatenate([cat("beh"), cat("kdh")], 1), z, P_Z, _TN)
    gd = jnp.where(eye, jnp.broadcast_to(cat("gend"), (nu, n, n)), 0.0)
    pq_scr[:, :, :n] = pq[:, :, :n] + gd
    pq_scr[:, :, n:] = pq[:, :, n:]


def _wkv_scan(r, v, kk, lw, be, kd, lc):
    b, lt, c = r.shape
    tt = WKV_TILE
    nt = lt // tt
    ntc = lc // tt
    tiles = (lambda i: i, lambda i: jnp.where(i < ntc, ntc - 1 - i, nt - 1 - (i - ntc)))
    t_in = lambda d: (lambda i: tiles[d](jnp.minimum(i, nt - 1)))
    t_out = lambda d: (lambda i: tiles[d](jnp.maximum(i - 1, 0)))
    one = lambda d: pl.BlockSpec((1, tt, c), lambda bi, i: (bi, t_in(d)(i), 0))
    two = lambda d: pl.BlockSpec((1, 1, tt, c), lambda bi, i: (bi, d, t_in(d)(i), 0))
    out = lambda d: pl.BlockSpec((1, tt, c), lambda bi, i: (bi, t_out(d)(i), 0))
    nu = 2 * (tt // WKV_CHUNK) * RWKV_HEADS
    ysh = jax.ShapeDtypeStruct((b, lt, c), F32)
    return pl.pallas_call(
        _wkv_kernel,
        grid=(b, nt + 1),
        in_specs=[one(0)] * 3 + [one(1)] * 3 + [two(0)] * 3 + [two(1)] * 3,
        out_specs=[out(0), out(1)],
        out_shape=[ysh, ysh],
        scratch_shapes=[pltpu.VMEM((2 * RWKV_HEADS, HEAD_DIM, HEAD_DIM), F32),
                        pltpu.VMEM((nu, HEAD_DIM, 2 * HEAD_DIM), F32),
                        pltpu.VMEM((nu, WKV_CHUNK, 2 * HEAD_DIM), F32)],
        compiler_params=_cparams(("parallel", "arbitrary")),
        name="wkv_scan",
    )(r, v, kk, r, v, kk, lw, be, kd, lw, be, kd)


def _rwkv_out_kernel(yf_ref, yb_ref, bon_ref, g_ref, bd_ref, gg_ref, gb_ref, o_ref):
    y = yf_ref[0] + yb_ref[0] + bon_ref[0]
    bd = bd_ref[...]
    mu = _dot01(y, bd) * (1.0 / HEAD_DIM)
    yc = y - mu
    var = _dot01(yc * yc, bd) * (1.0 / HEAD_DIM)
    yn = yc * lax.rsqrt(var + RWKV_GN_EPS) * gg_ref[...] + gb_ref[...]
    o_ref[0] = (yn * g_ref[0]).astype(o_ref.dtype)


def _rwkv_out(yf, yb, bon, g, bd, gg, gb):
    b, lt, c = yf.shape
    tm = _pick_tile(lt, (ROW_TILE_WIDE, ROW_TILE_MID, TOK_TILE))
    tok = pl.BlockSpec((1, tm, c), lambda bi, i: (bi, i, 0))
    full = lambda shape: pl.BlockSpec(shape, lambda bi, i: (0,) * len(shape))
    return pl.pallas_call(
        _rwkv_out_kernel,
        grid=(b, lt // tm),
        in_specs=[tok, tok, tok, tok, full((c, c)), full((1, c)), full((1, c))],
        out_specs=tok,
        out_shape=jax.ShapeDtypeStruct((b, lt, c), BF16),
        compiler_params=_cparams(("parallel", "parallel")),
        name="rwkv_out",
    )(yf, yb, bon, g, bd, gg, gb)


def _hy_prep_kernel(u_ref, up_ref, un_ref, w_ref, b_ref, x1_o, x2_o, v_o, *, lc, lt, tm):
    i = pl.program_id(1)
    u = u_ref[0]
    prev, nxt = _prev_next(u, up_ref[0], un_ref[0], i, tm, lc, lt)
    y = prev * w_ref[0:1] + u * w_ref[1:2] + nxt * w_ref[2:3] + b_ref[...]
    c = HY_WIDTH
    x1_o[0] = y[:, :c]
    x2_o[0] = y[:, c:2 * c]
    v_o[0] = y[:, 2 * c:]


def _hy_prep(uh, w, bias, lc):
    b, lt, cin = uh.shape
    tm = _pick_tile(lt, (ROW_TILE_WIDE, ROW_TILE_MID, TOK_TILE))
    c = HY_WIDTH
    full = lambda shape: pl.BlockSpec(shape, lambda bi, i: (0,) * len(shape))
    tok = pl.BlockSpec((1, tm, c), lambda bi, i: (bi, i, 0))
    s1 = jax.ShapeDtypeStruct((b, lt, c), F32)
    return pl.pallas_call(
        functools.partial(_hy_prep_kernel, lc=lc, lt=lt, tm=tm),
        grid=(b, lt // tm),
        in_specs=_halo_specs(tm, cin, lt) + [full((3, cin)), full((1, cin))],
        out_specs=[tok, tok, tok],
        out_shape=[s1, s1, s1],
        compiler_params=_cparams(("parallel", "parallel")),
        name="hy_prep",
    )(uh, uh, uh, w, bias)


def _hy_filter_kernel(z_ref, w1_ref, b1_ref, f1_ref, w2_ref, b2_ref, f2_ref, w3_ref, dec_ref,
                      fw_o, bw_o, nrm_o, *, tl):
    i = pl.program_id(0)
    z = z_ref[...]
    h = jnp.sin(f1_ref[...] * (_dotp(z, w1_ref[...], 3) + b1_ref[...]))
    h = jnp.sin(f2_ref[...] * (_dotp(h, w2_ref[...], 3) + b2_ref[...]))
    h = _dotp(h, w3_ref[...], 3) * jnp.exp(-z[:, 0:1] * dec_ref[...])
    c = HY_WIDTH
    fw = jnp.concatenate([h[:, 0:c], h[:, 2 * c:3 * c]], 1)
    bw = jnp.concatenate([h[:, c:2 * c], h[:, 3 * c:4 * c]], 1)
    rid = i * tl + lax.broadcasted_iota(jnp.int32, (tl, 1), 0)
    bw = jnp.where(rid == 0, 0.0, bw)
    fw_o[...] = fw
    bw_o[...] = bw

    @pl.when(i == 0)
    def _():
        nrm_o[...] = jnp.zeros(nrm_o.shape, F32)

    nrm_o[...] += jnp.sum(jnp.abs(fw) + jnp.abs(bw), 0, keepdims=True)


def _hy_filter(feat, w1p, b1, f1, w2, b2, f2, w3, dec):
    l, fe = feat.shape
    tl = _pick_tile(l, (512, 256))
    c2 = HY_ORDER * HY_WIDTH
    full = lambda shape: pl.BlockSpec(shape, lambda i: (0,) * len(shape))
    return pl.pallas_call(
        functools.partial(_hy_filter_kernel, tl=tl),
        grid=(l // tl,),
        in_specs=[pl.BlockSpec((tl, fe), lambda i: (i, 0)), full(w1p.shape), full(b1.shape), full(f1.shape),
                  full(w2.shape), full(b2.shape), full(f2.shape), full(w3.shape), full(dec.shape)],
        out_specs=[pl.BlockSpec((tl, c2), lambda i: (i, 0)), pl.BlockSpec((tl, c2), lambda i: (i, 0)),
                   pl.BlockSpec((1, c2), lambda i: (0, 0))],
        out_shape=[jax.ShapeDtypeStruct((l, c2), F32), jax.ShapeDtypeStruct((l, c2), F32),
                   jax.ShapeDtypeStruct((1, c2), F32)],
        compiler_params=_cparams(("arbitrary",)),
        name="hy_filter",
    )(feat, w1p, b1, f1, w2, b2, f2, w3, dec)


def _dft_cols_kernel(f_ref, xa_ref, xb_ref, o_ref, *, n1, pair):
    f = f_ref[...]
    pa = _dotp(f, xa_ref[0], P_DFT)
    pb = _dotp(f, xb_ref[0], P_DFT)
    if pair:
        o_ref[0] = pa[:n1].astype(o_ref.dtype)
        o_ref[1] = pa[n1:].astype(o_ref.dtype)
        o_ref[2] = pb[:n1].astype(o_ref.dtype)
        o_ref[3] = pb[n1:].astype(o_ref.dtype)
    else:
        o_ref[0] = (pa[:n1] - pb[n1:]).astype(o_ref.dtype)
        o_ref[1] = (pb[:n1] + pa[n1:]).astype(o_ref.dtype)


def _dft_cols(fstack, xa, ia, xb, ib, pair):
    n1 = fstack.shape[0] // 2
    _, nh, cols = xa.shape
    tc = _pick_tile(cols, (4096, 2048, 1024, 512, 256, 128))
    no = 4 if pair else 2
    return pl.pallas_call(
        functools.partial(_dft_cols_kernel, n1=n1, pair=pair),
        grid=(cols // tc,),
        in_specs=[pl.BlockSpec(fstack.shape, lambda j: (0, 0)),
                  pl.BlockSpec((1, nh, tc), lambda j: (ia, 0, j)),
                  pl.BlockSpec((1, nh, tc), lambda j: (ib, 0, j))],
        out_specs=pl.BlockSpec((no, n1, tc), lambda j: (0, 0, j)),
        out_shape=jax.ShapeDtypeStruct((no, n1, cols), BF16),
        compiler_params=_cparams(("parallel",)),
        name="dft_cols",
    )(fstack, xa, xb)


def _cplx_left(gs, zr, zi, n):
    c = zr.shape[1]
    p = _dotp(gs, jnp.concatenate([zr, zi], 1), P_DFT)
    return p[:n, :c] - p[n:, c:], p[:n, c:] + p[n:, :c]


def _spec_kernel(a_ref, g_ref, nrm_ref, o_ref, *, n_total, kp):
    n2 = DFT_N2
    s = 1.0 / (nrm_ref[...] * n_total)
    for q in range(kp):
        gs = jnp.concatenate([g_ref[q, 0], g_ref[q, 1]], 0)
        fr, fi = _cplx_left(gs, a_ref[0, q], a_ref[1, q], n2)
        br, bi = _cplx_left(gs, a_ref[2, q], a_ref[3, q], n2)
        o_ref[0, q] = ((fr + br) * s).astype(o_ref.dtype)
        o_ref[1, q] = ((fi - bi) * s).astype(o_ref.dtype)


def _planes_per_step(n1):
    return next(k for k in (4, 2, 1) if n1 % k == 0)


def _spec(a4, g, nrm, n_total):
    _, n1, n2, c2 = a4.shape
    kp = _planes_per_step(n1)
    return pl.pallas_call(
        functools.partial(_spec_kernel, n_total=float(n_total), kp=kp),
        grid=(n1 // kp,),
        in_specs=[pl.BlockSpec((4, kp, n2, c2), lambda k: (0, k, 0, 0)),
                  pl.BlockSpec((kp, 2, n2, n2), lambda k: (k, 0, 0, 0)),
                  pl.BlockSpec((1, c2), lambda k: (0, 0))],
        out_specs=pl.BlockSpec((2, kp, n2, c2), lambda k: (0, k, 0, 0)),
        out_shape=jax.ShapeDtypeStruct((2, n1, n2, c2), BF16),
        compiler_params=_cparams(("parallel",)),
        name="hy_spec",
    )(a4, g, nrm)


def _conv_mid_kernel(a_ref, g_ref, k_ref, o_ref, *, kp):
    n2 = DFT_N2
    for q in range(kp):
        gs = jnp.concatenate([g_ref[q, 0], g_ref[q, 1]], 0)
        xr, xi = _cplx_left(gs, a_ref[0, q], a_ref[1, q], n2)
        c = xr.shape[1]
        kr = k_ref[0, q].astype(F32)
        ki = k_ref[1, q].astype(F32)
        zr = xr * kr - xi * ki
        zi = xr * ki + xi * kr
        zst = jnp.concatenate([jnp.concatenate([zr, zi], 1), jnp.concatenate([zi, -zr], 1)], 0)
        y = _dotp(gs, zst, P_DFT, _TN)
        o_ref[0, q] = y[:, :c].astype(o_ref.dtype)
        o_ref[1, q] = y[:, c:].astype(o_ref.dtype)


def _conv_mid(a, g, kspec, order):
    _, n1, n2, c = a.shape
    kp = _planes_per_step(n1)
    return pl.pallas_call(
        functools.partial(_conv_mid_kernel, kp=kp),
        grid=(n1 // kp,),
        in_specs=[pl.BlockSpec((2, kp, n2, c), lambda k: (0, k, 0, 0)),
                  pl.BlockSpec((kp, 2, n2, n2), lambda k: (k, 0, 0, 0)),
                  pl.BlockSpec((2, kp, n2, c), lambda k: (0, k, 0, order))],
        out_specs=pl.BlockSpec((2, kp, n2, c), lambda k: (0, k, 0, 0)),
        out_shape=jax.ShapeDtypeStruct((2, n1, n2, c), BF16),
        compiler_params=_cparams(("parallel",)),
        name="hy_conv_mid",
    )(a, g, kspec)


def _idft_cols_kernel(c_ref, b_ref, g0_ref, g1_ref, x0_ref, x1_ref, bias_ref, o_ref, *, nh):
    cs = c_ref[...]
    pr = _dotp(cs, b_ref[0], P_DFT)
    pi = _dotp(cs, b_ref[1], P_DFT)
    yr = pr[:nh] - pi[nh:]
    yi = pi[:nh] + pr[nh:]
    bias = bias_ref[...]
    o_ref[0] = g0_ref[0] * (yr + x0_ref[0] * bias)
    o_ref[1] = g1_ref[0] * (yi + x1_ref[0] * bias)


def _idft_cols(cstack, bv, gate, xin, bias_cols):
    nh2, n1 = cstack.shape
    nh = nh2 // 2
    cols = bv.shape[-1]
    tc = _pick_tile(cols, (4096, 2048, 1024, 512, 256, 128))
    row = lambda bi: pl.BlockSpec((1, nh, tc), lambda j: (bi, 0, j))
    return pl.pallas_call(
        functools.partial(_idft_cols_kernel, nh=nh),
        grid=(cols // tc,),
        in_specs=[pl.BlockSpec((nh2, n1), lambda j: (0, 0)),
                  pl.BlockSpec((2, n1, tc), lambda j: (0, 0, j)),
                  row(0), row(1), row(0), row(1),
                  pl.BlockSpec((1, tc), lambda j: (0, j))],
        out_specs=pl.BlockSpec((2, nh, tc), lambda j: (0, 0, j)),
        out_shape=jax.ShapeDtypeStruct((2, nh, cols), F32),
        compiler_params=_cparams(("parallel",)),
        name="idft_cols",
    )(cstack, bv, gate, gate, xin, xin, bias_cols)


def _hy_ctx_kernel(x1_ref, x2_ref, v_ref, fw_ref, bw_ref, nrm_ref, bias_ref, ff_ref, ci_ref, o_ref, *, lc):
    n = 2 * lc
    c = HY_WIDTH
    ff = ff_ref[...]
    ci = ci_ref[...]
    pf = _mm(ff, fw_ref[...])
    pb = _mm(ff, bw_ref[...])
    s = 1.0 / (nrm_ref[...] * float(n))
    kr = (pf[:n] + pb[:n]) * s
    ki = (pf[n:] - pb[n:]) * s

    def conv(z0, z1, o):
        xr, xi = _cplx_left(ff, z0, z1, n)
        krr = kr[:, o * c:(o + 1) * c]
        kii = ki[:, o * c:(o + 1) * c]
        return _cplx_left(ci, xr * krr - xi * kii, xr * kii + xi * krr, lc)

    v0 = v_ref[0]
    v1 = v_ref[1]
    y0, y1 = conv(v0, v1, 0)
    z0 = x1_ref[0] * (y0 + v0 * bias_ref[0:1])
    z1 = x1_ref[1] * (y1 + v1 * bias_ref[0:1])
    y0, y1 = conv(z0, z1, 1)
    o_ref[0] = x2_ref[0] * (y0 + z0 * bias_ref[1:2])
    o_ref[1] = x2_ref[1] * (y1 + z1 * bias_ref[1:2])


def _hy_ctx(x1, x2, v, fw, bw, nrm, bias, ff, ci):
    b, lc, c = v.shape
    vm = pl.BlockSpec(memory_space=pltpu.VMEM)
    return pl.pallas_call(
        functools.partial(_hy_ctx_kernel, lc=lc),
        in_specs=[vm] * 9,
        out_specs=vm,
        out_shape=jax.ShapeDtypeStruct((b, lc, c), F32),
        compiler_params=pltpu.CompilerParams(vmem_limit_bytes=VMEM_LIMIT),
        name="hy_ctx",
    )(x1, x2, v, fw, bw, nrm, bias, ff, ci)


def _outproj_kernel(att_refs, hy_refs, rw_ref, x_ref, mod_ref, w_ref, lg_ref, lb_ref, o_ref, *, lc, tm):
    i = pl.program_id(1)
    row0 = i * tm
    a0 = ATT_WIDTH
    a1 = ATT_WIDTH + RWKV_WIDTH
    ns = tm // TOK_TILE

    def rows(refs):
        return jnp.concatenate([jnp.where(row0 + s * TOK_TILE < lc, refs[2 * s][0], refs[2 * s + 1][0])
                                for s in range(ns)], 0)

    att = rows(att_refs)
    hy = rows(hy_refs)
    o = jnp.dot(att, w_ref[:a0], preferred_element_type=F32)
    o += jnp.dot(rw_ref[0], w_ref[a0:a1], preferred_element_type=F32)
    o += jnp.dot(hy.astype(BF16), w_ref[a1:], preferred_element_type=F32)
    g = _sel_mod(mod_ref, 2, row0, tm, lc)
    y = ALPHA * x_ref[0] + g * o
    o_ref[0] = _layer_norm(y) * lg_ref[...] + lb_ref[...]


def _outproj(att_c, att_l, rw, hy_c, hy_l, xx, mod, w, lg, lb, lc):
    b, lt, d = xx.shape
    ts = TOK_TILE
    tm = _pick_tile(lt, (ROW_TILE_WIDE, ROW_TILE_MID, TOK_TILE))
    ns = tm // ts
    nct = lc // ts
    tok = lambda c: pl.BlockSpec((1, tm, c), lambda bi, i: (bi, i, 0))
    ctx = lambda c, s: pl.BlockSpec((1, ts, c), lambda bi, i: (bi, jnp.minimum(i * ns + s, nct - 1), 0))
    lat = lambda c, s: pl.BlockSpec((1, ts, c), lambda bi, i: (bi, jnp.maximum(i * ns + s - nct, 0), 0))
    pairs = lambda c: [spec(c, s) for s in range(ns) for spec in (ctx, lat)]
    full = lambda shape: pl.BlockSpec(shape, lambda bi, i: (0,) * len(shape))
    kern = functools.partial(_outproj_kernel, lc=lc, tm=tm)
    return pl.pallas_call(
        lambda *refs: kern(refs[:2 * ns], refs[2 * ns:4 * ns], *refs[4 * ns:]),
        grid=(b, lt // tm),
        in_specs=pairs(ATT_WIDTH) + pairs(HY_WIDTH) + [
                  tok(RWKV_WIDTH), tok(d), pl.BlockSpec((1, 2, 6, d), lambda bi, i: (bi, 0, 0, 0)),
                  full(w.shape), full((1, d)), full((1, d))],
        out_specs=tok(d),
        out_shape=jax.ShapeDtypeStruct((b, lt, d), F32),
        compiler_params=_cparams(("parallel", "parallel")),
        name="outproj",
    )(*([att_c, att_l] * ns), *([hy_c, hy_l] * ns), rw, xx, mod, w, lg, lb)


def _ffn_kernel(x_ref, mod_ref, w1_ref, w3_ref, w2_ref, lg_ref, lb_ref, o_ref, h_scr, acc_scr, *, lc, tm, nf):
    row0 = pl.program_id(1) * tm
    f = pl.program_id(2)

    @pl.when(f == 0)
    def _():
        sh = _sel_mod(mod_ref, 3, row0, tm, lc)
        sc = _sel_mod(mod_ref, 4, row0, tm, lc)
        h_scr[...] = (_layer_norm(x_ref[0]) * (1.0 + sc) + sh).astype(BF16)
        acc_scr[...] = jnp.zeros(acc_scr.shape, F32)

    h = h_scr[...]
    a = jnp.dot(h, w1_ref[...], preferred_element_type=F32)
    g = jnp.dot(h, w3_ref[...], preferred_element_type=F32)
    acc_scr[...] += jnp.dot((_silu(a) * g).astype(BF16), w2_ref[...], preferred_element_type=F32)

    @pl.when(f == nf - 1)
    def _():
        gate = _sel_mod(mod_ref, 5, row0, tm, lc)
        y = ALPHA * x_ref[0] + gate * acc_scr[...]
        o_ref[0] = _layer_norm(y) * lg_ref[...] + lb_ref[...]


def _ffn(xx, mod, w1, w3, w2, lg, lb, lc):
    b, lt, d = xx.shape
    ff = w1.shape[1]
    tm = _pick_tile(lt, (1280, 768, 512, 256))
    tf = _pick_tile(ff, (256, 128))
    nf = ff // tf
    return pl.pallas_call(
        functools.partial(_ffn_kernel, lc=lc, tm=tm, nf=nf),
        grid=(b, lt // tm, nf),
        in_specs=[pl.BlockSpec((1, tm, d), lambda bi, i, f: (bi, i, 0)),
                  pl.BlockSpec((1, 2, 6, d), lambda bi, i, f: (bi, 0, 0, 0)),
                  pl.BlockSpec((d, tf), lambda bi, i, f: (0, f)),
                  pl.BlockSpec((d, tf), lambda bi, i, f: (0, f)),
                  pl.BlockSpec((tf, d), lambda bi, i, f: (f, 0)),
                  pl.BlockSpec((1, d), lambda bi, i, f: (0, 0)),
                  pl.BlockSpec((1, d), lambda bi, i, f: (0, 0))],
        out_specs=pl.BlockSpec((1, tm, d), lambda bi, i, f: (bi, i, 0)),
        out_shape=jax.ShapeDtypeStruct((b, lt, d), F32),
        scratch_shapes=[pltpu.VMEM((tm, d), BF16), pltpu.VMEM((tm, d), F32)],
        compiler_params=_cparams(("parallel", "parallel", "arbitrary")),
        name="ffn",
    )(xx, mod, w1, w3, w2, lg, lb)


def _route(x_ref, mod_ref, wr_ref, row0, tm, lc):
    sh = _sel_mod(mod_ref, 3, row0, tm, lc)
    sc = _sel_mod(mod_ref, 4, row0, tm, lc)
    h = _layer_norm(x_ref[0]) * (1.0 + sc) + sh
    logits = _dotp(h, wr_ref[...], 3)
    lane = lax.broadcasted_iota(jnp.int32, logits.shape, 1)
    neg = jnp.float32(-jnp.inf)
    lg = jnp.where(lane < N_EXPERTS, logits, neg)
    m1 = jnp.max(lg, -1, keepdims=True)
    i1 = jnp.min(jnp.where(lg == m1, lane, LANES), -1, keepdims=True)
    lg2 = jnp.where(lane == i1, neg, lg)
    m2 = jnp.max(lg2, -1, keepdims=True)
    i2 = jnp.min(jnp.where(lg2 == m2, lane, LANES), -1, keepdims=True)
    e2 = jnp.exp(m2 - m1)
    return h, lane, i1, i2, 1.0 / (1.0 + e2), e2 / (1.0 + e2)


def _moe_sparse_kernel(x_ref, mod_ref, wr_ref, w1_ref, w3_ref, w2_ref, lg_ref, lb_ref, o_ref,
                       h_scr, acc_scr, gate_scr, posc_scr, posr_scr, xs_scr, ye_scr, nblk_scr, *, lc, tm, nf, ns, blk):
    row0 = pl.program_id(1) * tm
    s = pl.program_id(2)
    e = s // nf
    f = s % nf

    @pl.when(s == 0)
    def _():
        h, lane, i1, i2, g1, g2 = _route(x_ref, mod_ref, wr_ref, row0, tm, lc)
        h_scr[...] = h.astype(BF16)
        acc_scr[...] = jnp.zeros(acc_scr.shape, F32)
        routed = jnp.where(jnp.logical_or(lane == i1, lane == i2), 1.0, 0.0)
        r = lax.broadcasted_iota(jnp.int32, (tm, tm), 0)
        c = lax.broadcasted_iota(jnp.int32, (tm, tm), 1)
        before = jnp.where(c < r, 1.0, 0.0).astype(BF16)
        rank_c = jnp.dot(before, routed.astype(BF16), preferred_element_type=F32)
        gate_scr[...] = jnp.where(lane == i1, g1, 0.0) + jnp.where(lane == i2, g2, 0.0)
        posc_scr[...] = jnp.where(routed > 0.0, rank_c, -1.0)
        routed_t = routed.T[:2 * SUBLANES]
        after = jnp.where(r < c, 1.0, 0.0).astype(BF16)
        rank_r = jnp.dot(routed_t.astype(BF16), after, preferred_element_type=F32)
        posr_scr[...] = jnp.where(routed_t > 0.0, rank_r, -1.0)
        counts = jnp.sum(routed, 0, keepdims=True)
        lane1 = lax.broadcasted_iota(jnp.int32, counts.shape, 1)
        for ex in range(N_EXPERTS):
            n_rows = jnp.sum(jnp.where(lane1 == ex, counts, 0.0)).astype(jnp.int32)
            nblk_scr[ex] = (n_rows + (blk - 1)) // blk

    nblk = nblk_scr[e]
    mv = MOE_MOVE_BLOCK
    nmv = (nblk * blk + (mv - 1)) // mv
    rows = lambda i: pl.ds(pl.multiple_of(i * mv, mv), mv)

    @pl.when(f == 0)
    def _():
        def gather(i, carry):
            slot = (lax.broadcasted_iota(jnp.int32, (mv, tm), 0) + i * mv).astype(F32)
            take = jnp.where(slot == posr_scr[pl.ds(e, 1), :], 1.0, 0.0).astype(BF16)
            xs_scr[rows(i), :] = jnp.dot(take, h_scr[...], preferred_element_type=F32).astype(BF16)
            ye_scr[rows(i), :] = jnp.zeros((mv, ye_scr.shape[1]), F32)
            return carry

        lax.fori_loop(0, nmv, gather, 0)

    def expert(m):
        xs = xs_scr[:m]
        a = jnp.dot(xs, w1_ref[0], preferred_element_type=F32)
        g = jnp.dot(xs, w3_ref[0], preferred_element_type=F32)
        ye_scr[:m] += jnp.dot((_silu(a) * g).astype(BF16), w2_ref[0], preferred_element_type=F32)

    for k in range(tm // blk):
        pl.when(nblk == k + 1)(functools.partial(expert, (k + 1) * blk))

    @pl.when(f == nf - 1)
    def _():
        lane = lax.broadcasted_iota(jnp.int32, (tm, LANES), 1)
        column = lambda ref: jnp.sum(jnp.where(lane == e, ref[...], 0.0), -1, keepdims=True)
        pos = column(posc_scr)
        gate = column(gate_scr)

        def scatter(i, carry):
            slot = (lax.broadcasted_iota(jnp.int32, (tm, mv), 1) + i * mv).astype(F32)
            put = jnp.where(slot == pos, 1.0, 0.0).astype(BF16)
            acc_scr[...] += gate * jnp.dot(put, ye_scr[rows(i), :].astype(BF16), preferred_element_type=F32)
            return carry

        lax.fori_loop(0, nmv, scatter, 0)

    @pl.when(s == ns - 1)
    def _():
        gate = _sel_mod(mod_ref, 5, row0, tm, lc)
        y = ALPHA * x_ref[0] + gate * acc_scr[...]
        o_ref[0] = _layer_norm(y) * lg_ref[...] + lb_ref[...]


def _moe_sparse(xx, mod, wr, w1, w3, w2, lg, lb, lc):
    b, lt, d = xx.shape
    ne, _, ff = w1.shape
    tm = _pick_tile(lt, (1280, 768, 512, 256))
    tf = _pick_tile(ff, (256, 128))
    nf = ff // tf
    ns = ne * nf
    return pl.pallas_call(
        functools.partial(_moe_sparse_kernel, lc=lc, tm=tm, nf=nf, ns=ns, blk=MOE_ROW_BLOCK),
        grid=(b, lt // tm, ns),
        in_specs=[pl.BlockSpec((1, tm, d), lambda bi, i, s: (bi, i, 0)),
                  pl.BlockSpec((1, 2, 6, d), lambda bi, i, s: (bi, 0, 0, 0)),
                  pl.BlockSpec((d, LANES), lambda bi, i, s: (0, 0)),
                  pl.BlockSpec((1, d, tf), lambda bi, i, s: (s // nf, 0, s % nf)),
                  pl.BlockSpec((1, d, tf), lambda bi, i, s: (s // nf, 0, s % nf)),
                  pl.BlockSpec((1, tf, d), lambda bi, i, s: (s // nf, s % nf, 0)),
                  pl.BlockSpec((1, d), lambda bi, i, s: (0, 0)),
                  pl.BlockSpec((1, d), lambda bi, i, s: (0, 0))],
        out_specs=pl.BlockSpec((1, tm, d), lambda bi, i, s: (bi, i, 0)),
        out_shape=jax.ShapeDtypeStruct((b, lt, d), F32),
        scratch_shapes=[pltpu.VMEM((tm, d), BF16), pltpu.VMEM((tm, d), F32),
                        pltpu.VMEM((tm, LANES), F32), pltpu.VMEM((tm, LANES), F32),
                        pltpu.VMEM((2 * SUBLANES, tm), F32),
                        pltpu.VMEM((tm, d), BF16), pltpu.VMEM((tm, d), F32),
                        pltpu.SMEM((ne,), jnp.int32)],
        compiler_params=_cparams(("parallel", "parallel", "arbitrary"), VMEM_LIMIT_MOE),
        name="moe_sparse",
    )(xx, mod, wr, w1, w3, w2, lg, lb)


def _rope_tables(l, lc):
    rows = l // GRID_W
    row = jnp.repeat(jnp.arange(rows, dtype=F32), GRID_W)
    col = jnp.tile(jnp.arange(GRID_W, dtype=F32), rows)
    n_freq = HEAD_DIM // 4
    inv_freq = ROPE_THETA ** (-jnp.arange(n_freq, dtype=F32) / n_freq)
    ang = jnp.concatenate([row[:, None] * inv_freq, col[:, None] * inv_freq], -1)
    cos, sin = jnp.cos(ang), jnp.sin(ang)
    cos2 = jnp.concatenate([jnp.ones((lc, LANES), F32), jnp.concatenate([cos, cos, cos, cos], -1)], 0)
    sin2 = jnp.concatenate([jnp.zeros((lc, LANES), F32), jnp.concatenate([-sin, sin, -sin, sin], -1)], 0)
    return cos2, sin2


def _hy_features(l):
    bands = (HY_EMB - 1) // 2
    t = jnp.linspace(0.0, 1.0, l, dtype=F32)[:, None]
    f = jnp.linspace(1e-4, bands - 1, bands, dtype=F32)[None, :]
    wt = 2.0 * math.pi * jnp.arange(l, dtype=F32)[:, None] / l
    z = jnp.concatenate([t, jnp.cos(f * wt), -jnp.sin(f * wt)], -1)
    return jnp.pad(z, ((0, 0), (0, LANES - HY_EMB)))


def _angle(idx, n):
    return (2.0 * math.pi / n) * (idx % n).astype(F32)


def _dft_tables(n1):
    nh = n1 // 2
    n2 = DFT_N2
    n = n1 * n2
    k1 = jnp.arange(n1, dtype=jnp.int32)
    a1 = _angle(k1[:, None] * jnp.arange(nh, dtype=jnp.int32)[None, :], n1)
    fstack = jnp.concatenate([jnp.cos(a1), -jnp.sin(a1)], 0)
    cstack = jnp.concatenate([jnp.cos(a1.T), jnp.sin(a1.T)], 0)
    k2 = jnp.arange(n2, dtype=jnp.int32)
    at = _angle(k1[:, None] * k2[None, :], n)
    tr, ti = jnp.cos(at), -jnp.sin(at)
    a2 = _angle(k2[:, None] * k2[None, :], n2)
    fr, fi = jnp.cos(a2), -jnp.sin(a2)
    g = jnp.stack([tr[:, None, :] * fr[None] - ti[:, None, :] * fi[None],
                   tr[:, None, :] * fi[None] + ti[:, None, :] * fr[None]], 1)
    return fstack, cstack, g


def _dense_dft_tables(lc):
    n = 2 * lc
    a = _angle(jnp.arange(n, dtype=jnp.int32)[:, None] * jnp.arange(lc, dtype=jnp.int32)[None, :], n)
    ff = jnp.concatenate([jnp.cos(a), -jnp.sin(a)], 0)
    ci = jnp.concatenate([jnp.cos(a.T), jnp.sin(a.T)], 0)
    return ff, ci


def kernel(x, c, ctx, c_ctx, ada_w, ada_b, w_in, w_out, q_gain, k_gain, rwkv_mu, rwkv_w0, rwkv_wB, rwkv_a0, rwkv_aB, rwkv_gB, rwkv_kk, rwkv_ka, rwkv_rk, rwkv_gn_g, rwkv_gn_b, hy_short_w, hy_short_b, hy_w1, hy_b1, hy_freq1, hy_w2, hy_b2, hy_freq2, hy_w3, hy_decay, hy_bias, ln1_g, ln1_b, ln2_g, ln2_b, ffn_w1, ffn_w3, ffn_w2, moe_router, moe_w1, moe_w3, moe_w2):
    b, l, d = x.shape
    lc = ctx.shape[1]
    lt = lc + l
    depth = ada_w.shape[0]
    assert b == 2, "the long convolution packs the two batch rows as one complex signal"
    assert d == D_MODEL and lc % TOK_TILE == 0 and l % TOK_TILE == 0 and (2 * l) % (2 * DFT_N2) == 0
    cw = RWKV_WIDTH

    xx = jnp.concatenate([ctx, x], 1)
    cond8 = jnp.zeros((SUBLANES, d), F32).at[:b].set(c).at[b].set(c_ctx)
    mod_all = _ada_mod(cond8, ada_w, ada_b)

    cos64, sin64 = _rope_tables(l, lc)
    n1 = 2 * l // DFT_N2
    nh = n1 // 2
    cols = DFT_N2 * HY_WIDTH
    fstack, cstack, g_tab = (t.astype(BF16) for t in _dft_tables(n1))
    ff_c, ci_c = _dense_dft_tables(lc)
    feat_l = _hy_features(l)
    feat_c = _hy_features(lc)
    blk = jnp.arange(cw) // HEAD_DIM
    bd = (blk[:, None] == blk[None, :]).astype(F32)
    ch = jnp.arange(ATT_WIDTH)
    bd_att = (ch[:, None] // HEAD_DIM == ch[None, :] // HEAD_DIM).astype(BF16)
    swap_att = (ch[:, None] == (ch[None, :] + HALF_HD) % HEAD_DIM + (ch[None, :] // HEAD_DIM) * HEAD_DIM
                ).astype(BF16)
    perm64 = jnp.concatenate([jnp.arange(0, HEAD_DIM, 2), jnp.arange(1, HEAD_DIM, 2)])
    perm_att = jnp.concatenate([h * HEAD_DIM + perm64 for h in range(ATT_HEADS + ATT_KV_HEADS)]
                               + [jnp.arange(ATT_WIDTH + ATT_KV_WIDTH, IN_ATT)])
    tq = TOK_TILE
    qn = next(n for n in (4, 2, 1) if (l // tq) % n == 0)
    tk = _pick_tile(lt, (3328, 1280, 1024, 768, 512, 256))

    for li in range(depth):
        ml = mod_all[li]
        mod = jnp.stack([jnp.broadcast_to(ml[b].reshape(1, 6, d), (b, 6, d)), ml[:b].reshape(b, 6, d)], 1)
        wi = w_in[li]
        w_pad = jnp.concatenate([wi[:, :IN_ATT][:, perm_att], wi[:, IN_ATT:IN_ATT + IN_RWKV],
                                 jnp.zeros((d, IN_RWKV_PAD - IN_RWKV), F32), wi[:, IN_ATT + IN_RWKV:]],
                                1).astype(BF16)
        ua, ur, uh = _inproj(xx, mod, w_pad, lc)

        two = lambda gain: jnp.tile(gain[perm64], 2)[None]
        qt, kx, vt = _attn_prep(ua, cos64, sin64, two(q_gain[li]), two(k_gain[li]), bd_att, swap_att, lc)
        att_c = _flash(qt, kx, vt, tq, TOK_TILE, 1, 0, lc // tq, lc // TOK_TILE)
        s_bound = (HEAD_DIM ** 0.5) * LOG2E * jnp.max(jnp.abs(q_gain[li])) * jnp.max(jnp.abs(k_gain[li]))
        lat_args = (qt, kx, vt, tq, tk, qn, lc // tq, l // (qn * tq), lt // tk)
        att_l = lax.cond(s_bound <= MAX_UNSHIFTED_SCORE,
                         lambda: _flash(*lat_args, bounded=True), lambda: _flash(*lat_args, bounded=False))

        wl = jnp.zeros((LANES, 5 * cw), F32)
        wl = wl.at[0:W_LORA, 0:cw].set(rwkv_wB[li, 0]).at[W_LORA:2 * W_LORA, cw:2 * cw].set(rwkv_wB[li, 1])
        o_a = 2 * W_LORA
        wl = wl.at[o_a:o_a + A_LORA, 2 * cw:3 * cw].set(rwkv_aB[li, 0])
        wl = wl.at[o_a + A_LORA:o_a + 2 * A_LORA, 3 * cw:4 * cw].set(rwkv_aB[li, 1])
        o_g = o_a + 2 * A_LORA
        wl = wl.at[o_g:o_g + G_LORA, 4 * cw:5 * cw].set(rwkv_gB[li])
        mu = jnp.pad(rwkv_mu[li], ((0, 0), (0, IN_RWKV_PAD - IN_RWKV)))
        r_, v_, kk_, g_, bon_, lw_, be_, kd_ = _rwkv_prep(
            ur, mu, wl, bd, rwkv_w0[li], rwkv_a0[li], rwkv_kk[li][None], rwkv_ka[li][None],
            rwkv_rk[li].reshape(1, cw), lc)
        yf, yb = _wkv_scan(r_, v_, kk_, lw_, be_, kd_, lc)
        rw = _rwkv_out(yf, yb, bon_, g_, bd, rwkv_gn_g[li][None], rwkv_gn_b[li][None])

        x1, x2, vv = _hy_prep(uh, hy_short_w[li], hy_short_b[li][None], lc)
        w1p = jnp.pad(hy_w1[li], ((0, LANES - HY_EMB), (0, 0)))
        fargs = (w1p, hy_b1[li][None], hy_freq1[li][None], hy_w2[li], hy_b2[li][None], hy_freq2[li][None],
                 hy_w3[li], hy_decay[li][None])
        fw, bw, nrm = _hy_filter(feat_l, *fargs)
        c2 = HY_ORDER * HY_WIDTH
        a4 = _dft_cols(fstack, fw.reshape(1, nh, DFT_N2 * c2), 0, bw.reshape(1, nh, DFT_N2 * c2), 0, True)
        kspec = _spec(a4.reshape(4, n1, DFT_N2, c2), g_tab, nrm, n1 * DFT_N2)
        lat = lambda t: t[:, lc:].reshape(b, nh, cols)
        x1l, x2l, zin = lat(x1), lat(x2), lat(vv)
        for o, gate in enumerate((x1l, x2l)):
            a = _dft_cols(fstack, zin, 0, zin, 1, False)
            bv = _conv_mid(a.reshape(2, n1, DFT_N2, HY_WIDTH), g_tab, kspec, o)
            bias_cols = jnp.tile(hy_bias[li, o], DFT_N2)[None]
            zin = _idft_cols(cstack, bv.reshape(2, n1, cols), gate, zin, bias_cols)
        hy_l = zin.reshape(b, l, HY_WIDTH)
        fw_c, bw_c, nrm_c = _hy_filter(feat_c, *fargs)
        hy_c = _hy_ctx(x1[:, :lc], x2[:, :lc], vv[:, :lc], fw_c, bw_c, nrm_c, hy_bias[li], ff_c, ci_c)

        xx = _outproj(att_c, att_l, rw, hy_c, hy_l, xx, mod, w_out[li].astype(BF16),
                      ln1_g[li][None], ln1_b[li][None], lc)

        j = li // 2
        if li % 2 == 0:
            xx = _ffn(xx, mod, ffn_w1[j].astype(BF16), ffn_w3[j].astype(BF16), ffn_w2[j].astype(BF16),
                      ln2_g[li][None], ln2_b[li][None], lc)
        else:
            wr = jnp.pad(moe_router[j], ((0, 0), (0, LANES - N_EXPERTS)))
            xx = _moe_sparse(xx, mod, wr, moe_w1[j].astype(BF16), moe_w3[j].astype(BF16), moe_w2[j].astype(BF16),
                             ln2_g[li][None], ln2_b[li][None], lc)
    return xx[:, lc:]
```

```python
import functools
import math

import jax
import jax.numpy as jnp
from jax import lax
from jax.experimental import pallas as pl
from jax.experimental.pallas import tpu as pltpu

F32 = jnp.float32
BF16 = jnp.bfloat16
HI = lax.Precision.HIGHEST

D_MODEL = 1024
DEPTH = 2
GRID_W = 64
HEAD_DIM = 64
HALF_HD = HEAD_DIM // 2
ATT_WIDTH = 512
RWKV_WIDTH = 256
HY_WIDTH = 256
ATT_HEADS = 8
ATT_KV_HEADS = 2
ATT_REP = 4
ATT_KV_WIDTH = 128
ROPE_THETA = 10000.0
QK_EPS = 1e-6
RWKV_HEADS = 4
W_LORA = 16
A_LORA = 16
G_LORA = 32
RWKV_GN_EPS = 64e-5
HY_ORDER = 2
HY_EMB = 33
HY_FFN = 64
N_EXPERTS = 8
LN_EPS = 1e-6
IN_ATT = ATT_WIDTH + 2 * ATT_KV_WIDTH
IN_RWKV = 3 * RWKV_WIDTH + 2 * W_LORA + 2 * A_LORA + G_LORA
IN_RWKV_PAD = 896
IN_HY = 3 * HY_WIDTH
ALPHA = float((2 * DEPTH) ** 0.25)
LOG2E = 1.4426950408889634
MAX_UNSHIFTED_SCORE = 40.0

LANES = 128
SUBLANES = 8
TOK_TILE = 256
ROW_TILE_MID = 640
ROW_TILE_WIDE = 1280
WKV_CHUNK = 64
WKV_TILE = 256
DFT_N2 = 256
MOE_ROW_BLOCK = 128
MOE_MOVE_BLOCK = 256
MOE_GATHER_BLOCK = 512
VMEM_LIMIT = 48 * 1024 * 1024
VMEM_LIMIT_MOE = 56 * 1024 * 1024


def _cparams(sem, vmem=VMEM_LIMIT):
    return pltpu.CompilerParams(dimension_semantics=sem, vmem_limit_bytes=vmem)


def _pick_tile(n, cands):
    for c in cands:
        if n % c == 0:
            return c
    raise ValueError(f"no tile for {n} in {cands}")


def _layer_norm(x):
    mu = jnp.mean(x, -1, keepdims=True)
    xc = x - mu
    var = jnp.mean(xc * xc, -1, keepdims=True)
    return xc * lax.rsqrt(var + LN_EPS)


def _sel_mod(mod_ref, j, row0, tm, lc):
    rid = row0 + lax.broadcasted_iota(jnp.int32, (tm, 1), 0)
    return jnp.where(rid < lc, mod_ref[0, 0, j:j + 1, :], mod_ref[0, 1, j:j + 1, :])


def _silu(x):
    return x * jax.nn.sigmoid(x)


def _ada_kernel(c_ref, w_ref, b_ref, o_ref):
    s = _silu(c_ref[...])
    o_ref[0] = _dotp(s, w_ref[0], 3) + b_ref[0]


def _ada_mod(cond8, ada_w, ada_b):
    depth, d, n = ada_w.shape
    tn = _pick_tile(n, (1536, 1024, 512, 256, 128))
    return pl.pallas_call(
        _ada_kernel,
        grid=(depth, n // tn),
        in_specs=[pl.BlockSpec((SUBLANES, d), lambda l, j: (0, 0)),
                  pl.BlockSpec((1, d, tn), lambda l, j: (l, 0, j)),
                  pl.BlockSpec((1, 1, tn), lambda l, j: (l, 0, j))],
        out_specs=pl.BlockSpec((1, SUBLANES, tn), lambda l, j: (l, 0, j)),
        out_shape=jax.ShapeDtypeStruct((depth, SUBLANES, n), F32),
        compiler_params=_cparams(("parallel", "parallel")),
        name="ada_mod",
    )(cond8, ada_w, ada_b.reshape(depth, 1, n))


def _inproj_kernel(x_ref, mod_ref, w_ref, oa_ref, or_ref, oh_ref, *, lc, tm):
    row0 = pl.program_id(1) * tm
    sh = _sel_mod(mod_ref, 0, row0, tm, lc)
    sc = _sel_mod(mod_ref, 1, row0, tm, lc)
    h = (_layer_norm(x_ref[0]) * (1.0 + sc) + sh).astype(BF16)
    u = jnp.dot(h, w_ref[...], preferred_element_type=F32)
    oa_ref[0] = u[:, :IN_ATT]
    or_ref[0] = u[:, IN_ATT:IN_ATT + IN_RWKV_PAD]
    oh_ref[0] = u[:, IN_ATT + IN_RWKV_PAD:]


def _inproj(xx, mod, w_pad, lc):
    b, lt, d = xx.shape
    tm = _pick_tile(lt, (ROW_TILE_MID, TOK_TILE))
    n = w_pad.shape[1]
    return pl.pallas_call(
        functools.partial(_inproj_kernel, lc=lc, tm=tm),
        grid=(b, lt // tm),
        in_specs=[pl.BlockSpec((1, tm, d), lambda bi, i: (bi, i, 0)),
                  pl.BlockSpec((1, 2, 6, d), lambda bi, i: (bi, 0, 0, 0)),
                  pl.BlockSpec((d, n), lambda bi, i: (0, 0))],
        out_specs=[pl.BlockSpec((1, tm, IN_ATT), lambda bi, i: (bi, i, 0)),
                   pl.BlockSpec((1, tm, IN_RWKV_PAD), lambda bi, i: (bi, i, 0)),
                   pl.BlockSpec((1, tm, IN_HY), lambda bi, i: (bi, i, 0))],
        out_shape=[jax.ShapeDtypeStruct((b, lt, IN_ATT), F32),
                   jax.ShapeDtypeStruct((b, lt, IN_RWKV_PAD), F32),
                   jax.ShapeDtypeStruct((b, lt, IN_HY), F32)],
        compiler_params=_cparams(("parallel", "parallel")),
        name="inproj",
    )(xx, mod, w_pad)


def _attn_prep_kernel(u_ref, cos_ref, sin_ref, qg_ref, kg_ref, bd_ref, sw_ref, qt_ref, k_ref, vt_ref):
    u = u_ref[0]

    def norm_rope(x, g):
        w = x.shape[1]
        tile = lambda t: jnp.concatenate([t] * (w // LANES), -1)
        ms = _dot01(x * x, bd_ref[:w, :w]) * (1.0 / HEAD_DIM)
        xn = x * lax.rsqrt(ms + QK_EPS) * tile(g)
        sw = _dot01(xn, sw_ref[:w, :w])
        return xn * tile(cos_ref[...]) + sw * tile(sin_ref[...])

    q = norm_rope(u[:, :ATT_WIDTH], qg_ref[...]) * (LOG2E * HEAD_DIM ** -0.5)
    qt = q.T
    kx = norm_rope(u[:, ATT_WIDTH:ATT_WIDTH + ATT_KV_WIDTH], kg_ref[...])
    tq = TOK_TILE
    for g in range(ATT_KV_HEADS):
        base = g * ATT_REP * HEAD_DIM
        for t in range(u.shape[0] // tq):
            qt_ref[0, g, t] = jnp.concatenate(
                [qt[base + r * HEAD_DIM:base + (r + 1) * HEAD_DIM, t * tq:(t + 1) * tq] for r in range(ATT_REP)],
                -1).astype(BF16)
        k_ref[0, g] = kx[:, g * HEAD_DIM:(g + 1) * HEAD_DIM].astype(BF16)
    v0 = ATT_WIDTH + ATT_KV_WIDTH
    vt = u[:, v0:v0 + ATT_KV_WIDTH].T
    for g in range(ATT_KV_HEADS):
        vt_ref[0, g] = vt[g * HEAD_DIM:(g + 1) * HEAD_DIM].astype(BF16)


def _attn_prep(ua, cos, sin, qg, kg, bd_att, swap_att, lc):
    b, lt, _ = ua.shape
    tq = TOK_TILE
    tm = _pick_tile(lt, (ROW_TILE_WIDE, ROW_TILE_MID, TOK_TILE))
    return pl.pallas_call(
        _attn_prep_kernel,
        grid=(b, lt // tm),
        in_specs=[pl.BlockSpec((1, tm, IN_ATT), lambda bi, i: (bi, i, 0)),
                  pl.BlockSpec((tm, LANES), lambda bi, i: (i, 0)),
                  pl.BlockSpec((tm, LANES), lambda bi, i: (i, 0)),
                  pl.BlockSpec((1, LANES), lambda bi, i: (0, 0)),
                  pl.BlockSpec((1, LANES), lambda bi, i: (0, 0)),
                  pl.BlockSpec((ATT_WIDTH, ATT_WIDTH), lambda bi, i: (0, 0)),
                  pl.BlockSpec((ATT_WIDTH, ATT_WIDTH), lambda bi, i: (0, 0))],
        out_specs=[pl.BlockSpec((1, ATT_KV_HEADS, tm // tq, HEAD_DIM, ATT_REP * tq),
                                lambda bi, i: (bi, 0, i, 0, 0)),
                   pl.BlockSpec((1, ATT_KV_HEADS, tm, HEAD_DIM), lambda bi, i: (bi, 0, i, 0)),
                   pl.BlockSpec((1, ATT_KV_HEADS, HEAD_DIM, tm), lambda bi, i: (bi, 0, 0, i))],
        out_shape=[jax.ShapeDtypeStruct((b, ATT_KV_HEADS, lt // tq, HEAD_DIM, ATT_REP * tq), BF16),
                   jax.ShapeDtypeStruct((b, ATT_KV_HEADS, lt, HEAD_DIM), BF16),
                   jax.ShapeDtypeStruct((b, ATT_KV_HEADS, HEAD_DIM, lt), BF16)],
        compiler_params=_cparams(("parallel", "parallel")),
        name="attn_prep",
    )(ua, cos, sin, qg, kg, bd_att, swap_att)


def _flash_kernel(qt_refs, k_ref, vt_ref, o_ref, m_scr, l_scr, acc_scr, *, nk, tq, sub, qn):
    j = pl.program_id(3)

    @pl.when(j == 0)
    def _():
        m_scr[...] = jnp.full(m_scr.shape, -jnp.inf, F32)
        l_scr[...] = jnp.zeros(l_scr.shape, F32)
        acc_scr[...] = jnp.zeros(acc_scr.shape, F32)

    qt = jnp.concatenate([r[0, 0, 0] for r in qt_refs], -1)
    nsub = k_ref.shape[2] // sub
    m = m_scr[...]
    l = l_scr[...]
    acc = acc_scr[...]
    scores = lambda c: jnp.dot(k_ref[0, 0, c * sub:(c + 1) * sub, :], qt, preferred_element_type=F32)
    pv = lambda c, p: jnp.dot(vt_ref[0, 0, :, c * sub:(c + 1) * sub], p, preferred_element_type=F32)
    s_next = scores(0)
    pend = None
    for c in range(nsub):
        s = s_next
        if c + 1 < nsub:
            s_next = scores(c + 1)
        if pend is not None:
            acc = pend[0] * acc + pv(c - 1, pend[1])
        m_new = jnp.maximum(m, jnp.max(s, 0, keepdims=True))
        a = jnp.exp2(m - m_new)
        p = jnp.exp2(s - m_new)
        l = a * l + jnp.sum(p, 0, keepdims=True)
        pend = (a, p.astype(BF16))
        m = m_new
    acc = pend[0] * acc + pv(nsub - 1, pend[1])
    m_scr[...] = m
    l_scr[...] = l
    acc_scr[...] = acc

    @pl.when(j == nk - 1)
    def _():
        o = (acc / l).T
        for t in range(qn):
            o_ref[0, t * tq:(t + 1) * tq, :] = jnp.concatenate(
                [o[(t * ATT_REP + r) * tq:(t * ATT_REP + r + 1) * tq] for r in range(ATT_REP)], -1
            ).astype(o_ref.dtype)


def _flash_bounded_kernel(qt_refs, k_ref, vt_ref, o_ref, l_scr, acc_scr, *, nk, tq, sub, qn):
    j = pl.program_id(3)

    @pl.when(j == 0)
    def _():
        l_scr[...] = jnp.zeros(l_scr.shape, F32)
        acc_scr[...] = jnp.zeros(acc_scr.shape, F32)

    qt = jnp.concatenate([r[0, 0, 0] for r in qt_refs], -1)
    nsub = k_ref.shape[2] // sub

    l = l_scr[...]
    acc = acc_scr[...]
    scores = lambda c: jnp.dot(k_ref[0, 0, c * sub:(c + 1) * sub, :], qt, preferred_element_type=F32)
    s_next = scores(0)
    for c in range(nsub):
        s = s_next
        if c + 1 < nsub:
            s_next = scores(c + 1)
        p = jnp.exp2(s)
        l = l + jnp.sum(p.reshape(sub // SUBLANES, SUBLANES, p.shape[1]), 0)
        acc = acc + jnp.dot(vt_ref[0, 0, :, c * sub:(c + 1) * sub], p.astype(BF16), preferred_element_type=F32)
    l_scr[...] = l
    acc_scr[...] = acc

    @pl.when(j == nk - 1)
    def _():
        o = (acc / jnp.sum(l, 0, keepdims=True)).T
        for t in range(qn):
            o_ref[0, t * tq:(t + 1) * tq, :] = jnp.concatenate(
                [o[(t * ATT_REP + r) * tq:(t * ATT_REP + r + 1) * tq] for r in range(ATT_REP)], -1
            ).astype(o_ref.dtype)


def _flash(qt, k, vt, tq, tk, qn, q_tile0, nq, nk, bounded=False):
    b = qt.shape[0]
    lq = nq * qn * tq
    sub = _pick_tile(tk, (256, 128))
    lanes = qn * ATT_REP * tq
    if bounded:
        kern = functools.partial(_flash_bounded_kernel, nk=nk, tq=tq, sub=sub, qn=qn)
        scratch = [pltpu.VMEM((SUBLANES, lanes), F32), pltpu.VMEM((HEAD_DIM, lanes), F32)]
    else:
        kern = functools.partial(_flash_kernel, nk=nk, tq=tq, sub=sub, qn=qn)
        scratch = [pltpu.VMEM((1, lanes), F32), pltpu.VMEM((1, lanes), F32), pltpu.VMEM((HEAD_DIM, lanes), F32)]
    q_spec = lambda t: pl.BlockSpec((1, 1, 1, HEAD_DIM, ATT_REP * tq),
                                    lambda bi, g, i, j: (bi, g, q_tile0 + i * qn + t, 0, 0))
    return pl.pallas_call(
        lambda *refs: kern(refs[:qn], *refs[qn:]),
        grid=(b, ATT_KV_HEADS, nq, nk),
        in_specs=[q_spec(t) for t in range(qn)] + [
                  pl.BlockSpec((1, 1, tk, HEAD_DIM), lambda bi, g, i, j: (bi, g, j, 0)),
                  pl.BlockSpec((1, 1, HEAD_DIM, tk), lambda bi, g, i, j: (bi, g, 0, j))],
        out_specs=pl.BlockSpec((1, qn * tq, ATT_REP * HEAD_DIM), lambda bi, g, i, j: (bi, i, g)),
        out_shape=jax.ShapeDtypeStruct((b, lq, ATT_WIDTH), BF16),
        scratch_shapes=scratch,
        compiler_params=_cparams(("parallel", "parallel", "parallel", "arbitrary")),
        name="flash_bounded" if bounded else "flash",
    )(*([qt] * qn), k, vt)


def _prev_next(u, up8, un8, i, tm, lc, lt):
    rid = lax.broadcasted_iota(jnp.int32, u.shape, 0)
    gid = rid + i * tm
    prev = jnp.where(rid == 0, up8[SUBLANES - 1:SUBLANES], pltpu.roll(u, 1, 0))
    prev = jnp.where(gid == 0, 0.0, jnp.where(gid == lc, 0.0, prev))
    nxt = jnp.where(rid == tm - 1, un8[0:1], pltpu.roll(u, tm - 1, 0))
    nxt = jnp.where(gid == lc - 1, 0.0, jnp.where(gid == lt - 1, 0.0, nxt))
    return prev, nxt


def _halo_specs(tm, c, lt):
    r = tm // SUBLANES
    last = lt // SUBLANES - 1
    return [pl.BlockSpec((1, tm, c), lambda bi, i: (bi, i, 0)),
            pl.BlockSpec((1, SUBLANES, c), lambda bi, i: (bi, jnp.maximum(i * r - 1, 0), 0)),
            pl.BlockSpec((1, SUBLANES, c), lambda bi, i: (bi, jnp.minimum((i + 1) * r, last), 0))]


def _softplus(z):
    return jnp.maximum(z, 0.0) + jnp.log1p(jnp.exp(-jnp.abs(z)))


def _rwkv_prep_kernel(u_ref, up_ref, un_ref, mu_ref, wl_ref, bd_ref, w0_ref, a0_ref, kkw_ref, ka_ref, rk_ref,
                      r_o, v_o, kk_o, g_o, bon_o, lw_o, be_o, kd_o, *, lc, lt, tm):
    i = pl.program_id(1)
    u = u_ref[0]
    prev, nxt = _prev_next(u, up_ref[0], un_ref[0], i, tm, lc, lt)
    us = u + mu_ref[0:1] * (prev - u) + mu_ref[1:2] * (nxt - u)
    c = RWKV_WIDTH
    r = us[:, 0:c]
    k = us[:, c:2 * c]
    v = us[:, 2 * c:3 * c]
    slab = us[:, 3 * c:3 * c + LANES]
    lane = lax.broadcasted_iota(jnp.int32, slab.shape, 1)
    o_a = 2 * W_LORA
    o_g = o_a + 2 * A_LORA
    act = jnp.where(lane < o_a, jnp.tanh(slab),
                    jnp.where(lane < o_g, slab,
                              jnp.where(lane < o_g + G_LORA, jax.nn.sigmoid(slab), 0.0)))
    lo = _dotp(act, wl_ref[...], 3)
    bd = bd_ref[...]
    kk0 = k * kkw_ref[...]
    kk = kk0 * lax.rsqrt(_dot01(kk0 * kk0, bd) + 1e-12)
    r_o[0] = r
    v_o[0] = v
    kk_o[0] = kk
    g_o[0] = lo[:, 4 * c:5 * c]
    bon = jnp.zeros_like(r)
    for d in range(2):
        w_raw = w0_ref[d:d + 1] + lo[:, d * c:(d + 1) * c]
        lw = -jnp.exp(-_softplus(-w_raw) - 0.5)
        a = jax.nn.sigmoid(a0_ref[d:d + 1] + lo[:, (2 + d) * c:(3 + d) * c])
        kd = k * (1.0 + (a - 1.0) * ka_ref[...])
        lw_o[0, d] = lw
        be_o[0, d] = a * kk
        kd_o[0, d] = kd
        bon = bon + r * kd * rk_ref[...]
    bon_o[0] = _dot01(bon, bd) * v


def _rwkv_prep(ur, mu, wl, bd, w0, a0, kkw, ka, rk, lc):
    b, lt, cp = ur.shape
    tm = _pick_tile(lt, (ROW_TILE_MID, TOK_TILE))
    c = RWKV_WIDTH
    full = lambda shape: pl.BlockSpec(shape, lambda bi, i: (0,) * len(shape))
    tok = pl.BlockSpec((1, tm, c), lambda bi, i: (bi, i, 0))
    tok2 = pl.BlockSpec((1, 2, tm, c), lambda bi, i: (bi, 0, i, 0))
    s1 = jax.ShapeDtypeStruct((b, lt, c), F32)
    s2 = jax.ShapeDtypeStruct((b, 2, lt, c), F32)
    return pl.pallas_call(
        functools.partial(_rwkv_prep_kernel, lc=lc, lt=lt, tm=tm),
        grid=(b, lt // tm),
        in_specs=_halo_specs(tm, cp, lt) + [full((2, cp)), full((LANES, 5 * c)), full((c, c)), full((2, c)),
                                            full((2, c)), full((1, c)), full((1, c)), full((1, c))],
        out_specs=[tok, tok, tok, tok, tok, tok2, tok2, tok2],
        out_shape=[s1, s1, s1, s1, s1, s2, s2, s2],
        compiler_params=_cparams(("parallel", "parallel")),
        name="rwkv_prep",
    )(ur, ur, ur, mu, wl, bd, w0, a0, kkw, ka, rk)


def _mm(a, b):
    return jnp.dot(a, b, precision=HI, preferred_element_type=F32)


_NN = ((1,), (0,))
_NT = ((1,), (1,))
_TN = ((0,), (0,))


def _split2(a):
    hi = a.astype(BF16)
    return hi, (a - hi.astype(F32)).astype(BF16)


def _dotp(a, b, passes, dims=_NN):
    if a.ndim == 3:
        dn = (((dims[0][0] + 1,), (dims[1][0] + 1,)), ((0,), (0,)))
    else:
        dn = (dims, ((), ()))
    dg = lambda p, q: lax.dot_general(p, q, dn, preferred_element_type=F32)
    if passes == 1:
        return dg(a.astype(BF16), b.astype(BF16))
    ah, al = _split2(a)
    bh, bl = _split2(b)
    return dg(ah, bh) + dg(ah, bl) + dg(al, bh)


def _dot01(a, ones):
    ah, al = _split2(a)
    o = ones.astype(BF16)
    return jnp.dot(ah, o, preferred_element_type=F32) + jnp.dot(al, o, preferred_element_type=F32)


P_M = 1
P_INV = 1
P_W = 1
P_Z = 1
P_STATE = 1
P_DFT = 1


def _unit_tri_inv(a_mat, row, col, eye, passes):
    t = a_mat.shape[-1]
    eye_f = eye.astype(F32)
    base = SUBLANES
    same = (row // base) == (col // base)
    n1 = -jnp.where(same, a_mat, 0.0)
    n2 = _dotp(n1, n1, passes)
    n4 = _dotp(n2, n2, passes)
    x = _dotp(_dotp(eye_f + n1, eye_f + n2, passes), eye_f + n4, passes)
    m = base
    while m < t:
        off = jnp.logical_and((row // (2 * m)) == (col // (2 * m)), (row // m) != (col // m))
        x = x - _dotp(x, _dotp(jnp.where(off, a_mat, 0.0), x, passes), passes)
        m *= 2
    return x


def _wkv_kernel(rf_ref, vf_ref, kkf_ref, rb_ref, vb_ref, kkb_ref, lwf_ref, bef_ref, kdf_ref,
                lwb_ref, beb_ref, kdb_ref, yf_ref, yb_ref, h_scr, pq_scr, ry_scr):
    t = WKV_CHUNK
    n = HEAD_DIM
    g = WKV_TILE // WKV_CHUNK
    tt = WKV_TILE
    nh = RWKV_HEADS
    nd = g * nh
    nu = 2 * nd
    orders = (list(range(g)), list(range(g - 1, -1, -1)))

    @pl.when(pl.program_id(1) == 0)
    def _():
        h_scr[...] = jnp.zeros(h_scr.shape, F32)
        pq_scr[...] = jnp.zeros(pq_scr.shape, F32)
        ry_scr[...] = jnp.zeros(ry_scr.shape, F32)

    hm = h_scr[...]
    for p in range(g):
        both = lambda ref: jnp.concatenate([ref[d * nd + p * nh:d * nd + (p + 1) * nh] for d in range(2)], 0)
        ry = both(ry_scr)
        pq = both(pq_scr)
        y = _dotp(ry[:, :, :n], hm, P_STATE) + ry[:, :, n:]
        hm = _dotp(pq[:, :, :n], hm, P_STATE) + pq[:, :, n:]
        for d, y_ref in enumerate((yf_ref, yb_ref)):
            c = orders[d][p]
            y_ref[0, c * t:(c + 1) * t, :] = jnp.concatenate([y[d * nh + h] for h in range(nh)], -1)
    h_scr[...] = hm

    row = lax.broadcasted_iota(jnp.int32, (tt, tt), 0)
    col = lax.broadcasted_iota(jnp.int32, (tt, tt), 1)
    same = (row // t) == (col // t)

    def scaled(d, r_ref, v_ref, kk_ref, lw_ref, be_ref, kd_ref):
        tri = jnp.logical_and(same, (row >= col) if d == 0 else (row <= col))
        sums = jnp.concatenate([jnp.where(tri, 1.0, 0.0), jnp.where(same, 1.0, 0.0)], 0).astype(BF16)
        lw = lw_ref[0, 0]
        l1 = lw.astype(BF16)
        rem = lw - l1.astype(F32)
        l2 = rem.astype(BF16)
        l3 = (rem - l2.astype(F32)).astype(BF16)
        cc = (jnp.dot(sums, l1, preferred_element_type=F32) + jnp.dot(sums, l2, preferred_element_type=F32)
              + jnp.dot(sums, l3, preferred_element_type=F32))
        cum = cc[:tt]
        ctot = cc[tt:]
        e_neg = jnp.exp(-cum)
        e_end = jnp.exp(ctot - cum)
        be = be_ref[0, 0]
        kd = kd_ref[0, 0]

        def units(x, rows=t):
            return jnp.stack([x[orders[d][p] * t:orders[d][p] * t + rows, h * n:(h + 1) * n]
                              for p in range(g) for h in range(nh)], 0)

        return dict(kap=units(kk_ref[0] * jnp.exp(cum - lw)), rt=units(r_ref[0] * jnp.exp(cum)), vh=units(v_ref[0]),
                    bet=units(be * e_neg), kdt=units(kd * e_neg), beh=units(be * e_end), kdh=units(kd * e_end),
                    gend=units(jnp.exp(ctot), 1))

    parts = (scaled(0, rf_ref, vf_ref, kkf_ref, lwf_ref, bef_ref, kdf_ref),
             scaled(1, rb_ref, vb_ref, kkb_ref, lwb_ref, beb_ref, kdb_ref))
    cat = lambda key: jnp.concatenate([parts[0][key], parts[1][key]], 0)
    kap, rt, vh = cat("kap"), cat("rt"), cat("vh")
    r64 = lax.broadcasted_iota(jnp.int32, (t, t), 0)
    c64 = lax.broadcasted_iota(jnp.int32, (t, t), 1)
    eye = r64 == c64
    unit = lax.broadcasted_iota(jnp.int32, (nu, t, t), 0)
    ahead = (r64 - c64)[None] * jnp.where(unit < nd, 1, -1)
    strict = ahead > 0
    incl = ahead >= 0
    m = _dotp(jnp.concatenate([kap, rt], 1), jnp.concatenate([cat("bet"), cat("kdt")], 1), P_M, _NT)
    a_mat = jnp.where(strict, m[:, :t, :t], 0.0)
    b_mat = jnp.where(strict, m[:, :t, t:], 0.0)
    ab_r = jnp.concatenate([jnp.where(incl, m[:, t:, :t], 0.0), jnp.where(incl, m[:, t:, t:], 0.0)], 2)
    tinv = _unit_tri_inv(a_mat, r64, c64, eye, P_INV)
    w = _dotp(tinv, jnp.concatenate([kap, _dotp(b_mat, vh, P_W)], 2), P_W)
    z = jnp.concatenate([-w, jnp.concatenate([jnp.zeros((nu, t, n), F32), vh], 2)], 1)
    ry = _dotp(ab_r, z, P_Z)
    ry_scr[:, :, :n] = ry[:, :, :n] + rt
    ry_scr[:, :, n:] = ry[:, :, n:]
    pq = _dotp(jnp.concatenate([cat("beh"), cat("kdh")], 1), z, P_Z, _TN)
    gd = jnp.where(eye, jnp.broadcast_to(cat("gend"), (nu, n, n)), 0.0)
    pq_scr[:, :, :n] = pq[:, :, :n] + gd
    pq_scr[:, :, n:] = pq[:, :, n:]


def _wkv_scan(r, v, kk, lw, be, kd, lc):
    b, lt, c = r.shape
    tt = WKV_TILE
    nt = lt // tt
    ntc = lc // tt
    tiles = (lambda i: i, lambda i: jnp.where(i < ntc, ntc - 1 - i, nt - 1 - (i - ntc)))
    t_in = lambda d: (lambda i: tiles[d](jnp.minimum(i, nt - 1)))
    t_out = lambda d: (lambda i: tiles[d](jnp.maximum(i - 1, 0)))
    one = lambda d: pl.BlockSpec((1, tt, c), lambda bi, i: (bi, t_in(d)(i), 0))
    two = lambda d: pl.BlockSpec((1, 1, tt, c), lambda bi, i: (bi, d, t_in(d)(i), 0))
    out = lambda d: pl.BlockSpec((1, tt, c), lambda bi, i: (bi, t_out(d)(i), 0))
    nu = 2 * (tt // WKV_CHUNK) * RWKV_HEADS
    ysh = jax.ShapeDtypeStruct((b, lt, c), F32)
    return pl.pallas_call(
        _wkv_kernel,
        grid=(b, nt + 1),
        in_specs=[one(0)] * 3 + [one(1)] * 3 + [two(0)] * 3 + [two(1)] * 3,
        out_specs=[out(0), out(1)],
        out_shape=[ysh, ysh],
        scratch_shapes=[pltpu.VMEM((2 * RWKV_HEADS, HEAD_DIM, HEAD_DIM), F32),
                        pltpu.VMEM((nu, HEAD_DIM, 2 * HEAD_DIM), F32),
                        pltpu.VMEM((nu, WKV_CHUNK, 2 * HEAD_DIM), F32)],
        compiler_params=_cparams(("parallel", "arbitrary")),
        name="wkv_scan",
    )(r, v, kk, r, v, kk, lw, be, kd, lw, be, kd)


def _rwkv_out_kernel(yf_ref, yb_ref, bon_ref, g_ref, bd_ref, gg_ref, gb_ref, o_ref):
    y = yf_ref[0] + yb_ref[0] + bon_ref[0]
    bd = bd_ref[...]
    mu = _dot01(y, bd) * (1.0 / HEAD_DIM)
    yc = y - mu
    var = _dot01(yc * yc, bd) * (1.0 / HEAD_DIM)
    yn = yc * lax.rsqrt(var + RWKV_GN_EPS) * gg_ref[...] + gb_ref[...]
    o_ref[0] = (yn * g_ref[0]).astype(o_ref.dtype)


def _rwkv_out(yf, yb, bon, g, bd, gg, gb):
    b, lt, c = yf.shape
    tm = _pick_tile(lt, (ROW_TILE_WIDE, ROW_TILE_MID, TOK_TILE))
    tok = pl.BlockSpec((1, tm, c), lambda bi, i: (bi, i, 0))
    full = lambda shape: pl.BlockSpec(shape, lambda bi, i: (0,) * len(shape))
    return pl.pallas_call(
        _rwkv_out_kernel,
        grid=(b, lt // tm),
        in_specs=[tok, tok, tok, tok, full((c, c)), full((1, c)), full((1, c))],
        out_specs=tok,
        out_shape=jax.ShapeDtypeStruct((b, lt, c), BF16),
        compiler_params=_cparams(("parallel", "parallel")),
        name="rwkv_out",
    )(yf, yb, bon, g, bd, gg, gb)


def _hy_prep_kernel(u_ref, up_ref, un_ref, w_ref, b_ref, x1_o, x2_o, v_o, *, lc, lt, tm):
    i = pl.program_id(1)
    u = u_ref[0]
    prev, nxt = _prev_next(u, up_ref[0], un_ref[0], i, tm, lc, lt)
    y = prev * w_ref[0:1] + u * w_ref[1:2] + nxt * w_ref[2:3] + b_ref[...]
    c = HY_WIDTH
    x1_o[0] = y[:, :c]
    x2_o[0] = y[:, c:2 * c]
    v_o[0] = y[:, 2 * c:]


def _hy_prep(uh, w, bias, lc):
    b, lt, cin = uh.shape
    tm = _pick_tile(lt, (ROW_TILE_WIDE, ROW_TILE_MID, TOK_TILE))
    c = HY_WIDTH
    full = lambda shape: pl.BlockSpec(shape, lambda bi, i: (0,) * len(shape))
    tok = pl.BlockSpec((1, tm, c), lambda bi, i: (bi, i, 0))
    s1 = jax.ShapeDtypeStruct((b, lt, c), F32)
    return pl.pallas_call(
        functools.partial(_hy_prep_kernel, lc=lc, lt=lt, tm=tm),
        grid=(b, lt // tm),
        in_specs=_halo_specs(tm, cin, lt) + [full((3, cin)), full((1, cin))],
        out_specs=[tok, tok, tok],
        out_shape=[s1, s1, s1],
        compiler_params=_cparams(("parallel", "parallel")),
        name="hy_prep",
    )(uh, uh, uh, w, bias)


def _hy_filter_kernel(z_ref, w1_ref, b1_ref, f1_ref, w2_ref, b2_ref, f2_ref, w3_ref, dec_ref,
                      fw_o, bw_o, nrm_o, *, tl):
    i = pl.program_id(0)
    z = z_ref[...]
    h = jnp.sin(f1_ref[...] * (_dotp(z, w1_ref[...], 3) + b1_ref[...]))
    h = jnp.sin(f2_ref[...] * (_dotp(h, w2_ref[...], 3) + b2_ref[...]))
    h = _dotp(h, w3_ref[...], 3) * jnp.exp(-z[:, 0:1] * dec_ref[...])
    c = HY_WIDTH
    fw = jnp.concatenate([h[:, 0:c], h[:, 2 * c:3 * c]], 1)
    bw = jnp.concatenate([h[:, c:2 * c], h[:, 3 * c:4 * c]], 1)
    rid = i * tl + lax.broadcasted_iota(jnp.int32, (tl, 1), 0)
    bw = jnp.where(rid == 0, 0.0, bw)
    fw_o[...] = fw
    bw_o[...] = bw

    @pl.when(i == 0)
    def _():
        nrm_o[...] = jnp.zeros(nrm_o.shape, F32)

    nrm_o[...] += jnp.sum(jnp.abs(fw) + jnp.abs(bw), 0, keepdims=True)


def _hy_filter(feat, w1p, b1, f1, w2, b2, f2, w3, dec):
    l, fe = feat.shape
    tl = _pick_tile(l, (512, 256))
    c2 = HY_ORDER * HY_WIDTH
    full = lambda shape: pl.BlockSpec(shape, lambda i: (0,) * len(shape))
    return pl.pallas_call(
        functools.partial(_hy_filter_kernel, tl=tl),
        grid=(l // tl,),
        in_specs=[pl.BlockSpec((tl, fe), lambda i: (i, 0)), full(w1p.shape), full(b1.shape), full(f1.shape),
                  full(w2.shape), full(b2.shape), full(f2.shape), full(w3.shape), full(dec.shape)],
        out_specs=[pl.BlockSpec((tl, c2), lambda i: (i, 0)), pl.BlockSpec((tl, c2), lambda i: (i, 0)),
                   pl.BlockSpec((1, c2), lambda i: (0, 0))],
        out_shape=[jax.ShapeDtypeStruct((l, c2), F32), jax.ShapeDtypeStruct((l, c2), F32),
                   jax.ShapeDtypeStruct((1, c2), F32)],
        compiler_params=_cparams(("arbitrary",)),
        name="hy_filter",
    )(feat, w1p, b1, f1, w2, b2, f2, w3, dec)


def _dft_cols_kernel(f_ref, xa_ref, xb_ref, o_ref, *, n1, pair):
    f = f_ref[...]
    pa = _dotp(f, xa_ref[0], P_DFT)
    pb = _dotp(f, xb_ref[0], P_DFT)
    if pair:
        o_ref[0] = pa[:n1].astype(o_ref.dtype)
        o_ref[1] = pa[n1:].astype(o_ref.dtype)
        o_ref[2] = pb[:n1].astype(o_ref.dtype)
        o_ref[3] = pb[n1:].astype(o_ref.dtype)
    else:
        o_ref[0] = (pa[:n1] - pb[n1:]).astype(o_ref.dtype)
        o_ref[1] = (pb[:n1] + pa[n1:]).astype(o_ref.dtype)


def _dft_cols(fstack, xa, ia, xb, ib, pair):
    n1 = fstack.shape[0] // 2
    _, nh, cols = xa.shape
    tc = _pick_tile(cols, (4096, 2048, 1024, 512, 256, 128))
    no = 4 if pair else 2
    return pl.pallas_call(
        functools.partial(_dft_cols_kernel, n1=n1, pair=pair),
        grid=(cols // tc,),
        in_specs=[pl.BlockSpec(fstack.shape, lambda j: (0, 0)),
                  pl.BlockSpec((1, nh, tc), lambda j: (ia, 0, j)),
                  pl.BlockSpec((1, nh, tc), lambda j: (ib, 0, j))],
        out_specs=pl.BlockSpec((no, n1, tc), lambda j: (0, 0, j)),
        out_shape=jax.ShapeDtypeStruct((no, n1, cols), BF16),
        compiler_params=_cparams(("parallel",)),
        name="dft_cols",
    )(fstack, xa, xb)


def _cplx_left(gs, zr, zi, n):
    c = zr.shape[1]
    p = _dotp(gs, jnp.concatenate([zr, zi], 1), P_DFT)
    return p[:n, :c] - p[n:, c:], p[:n, c:] + p[n:, :c]


def _spec_kernel(a_ref, g_ref, nrm_ref, o_ref, *, n_total, kp):
    n2 = DFT_N2
    s = 1.0 / (nrm_ref[...] * n_total)
    for q in range(kp):
        gs = jnp.concatenate([g_ref[q, 0], g_ref[q, 1]], 0)
        fr, fi = _cplx_left(gs, a_ref[0, q], a_ref[1, q], n2)
        br, bi = _cplx_left(gs, a_ref[2, q], a_ref[3, q], n2)
        o_ref[0, q] = ((fr + br) * s).astype(o_ref.dtype)
        o_ref[1, q] = ((fi - bi) * s).astype(o_ref.dtype)


def _planes_per_step(n1):
    return next(k for k in (4, 2, 1) if n1 % k == 0)


def _spec(a4, g, nrm, n_total):
    _, n1, n2, c2 = a4.shape
    kp = _planes_per_step(n1)
    return pl.pallas_call(
        functools.partial(_spec_kernel, n_total=float(n_total), kp=kp),
        grid=(n1 // kp,),
        in_specs=[pl.BlockSpec((4, kp, n2, c2), lambda k: (0, k, 0, 0)),
                  pl.BlockSpec((kp, 2, n2, n2), lambda k: (k, 0, 0, 0)),
                  pl.BlockSpec((1, c2), lambda k: (0, 0))],
        out_specs=pl.BlockSpec((2, kp, n2, c2), lambda k: (0, k, 0, 0)),
        out_shape=jax.ShapeDtypeStruct((2, n1, n2, c2), BF16),
        compiler_params=_cparams(("parallel",)),
        name="hy_spec",
    )(a4, g, nrm)


def _conv_mid_kernel(a_ref, g_ref, k_ref, o_ref, *, kp):
    n2 = DFT_N2
    for q in range(kp):
        gs = jnp.concatenate([g_ref[q, 0], g_ref[q, 1]], 0)
        xr, xi = _cplx_left(gs, a_ref[0, q], a_ref[1, q], n2)
        c = xr.shape[1]
        kr = k_ref[0, q].astype(F32)
        ki = k_ref[1, q].astype(F32)
        zr = xr * kr - xi * ki
        zi = xr * ki + xi * kr
        zst = jnp.concatenate([jnp.concatenate([zr, zi], 1), jnp.concatenate([zi, -zr], 1)], 0)
        y = _dotp(gs, zst, P_DFT, _TN)
        o_ref[0, q] = y[:, :c].astype(o_ref.dtype)
        o_ref[1, q] = y[:, c:].astype(o_ref.dtype)


def _conv_mid(a, g, kspec, order):
    _, n1, n2, c = a.shape
    kp = _planes_per_step(n1)
    return pl.pallas_call(
        functools.partial(_conv_mid_kernel, kp=kp),
        grid=(n1 // kp,),
        in_specs=[pl.BlockSpec((2, kp, n2, c), lambda k: (0, k, 0, 0)),
                  pl.BlockSpec((kp, 2, n2, n2), lambda k: (k, 0, 0, 0)),
                  pl.BlockSpec((2, kp, n2, c), lambda k: (0, k, 0, order))],
        out_specs=pl.BlockSpec((2, kp, n2, c), lambda k: (0, k, 0, 0)),
        out_shape=jax.ShapeDtypeStruct((2, n1, n2, c), BF16),
        compiler_params=_cparams(("parallel",)),
        name="hy_conv_mid",
    )(a, g, kspec)


def _idft_cols_kernel(c_ref, b_ref, g0_ref, g1_ref, x0_ref, x1_ref, bias_ref, o_ref, *, nh):
    cs = c_ref[...]
    pr = _dotp(cs, b_ref[0], P_DFT)
    pi = _dotp(cs, b_ref[1], P_DFT)
    yr = pr[:nh] - pi[nh:]
    yi = pi[:nh] + pr[nh:]
    bias = bias_ref[...]
    o_ref[0] = g0_ref[0] * (yr + x0_ref[0] * bias)
    o_ref[1] = g1_ref[0] * (yi + x1_ref[0] * bias)


def _idft_cols(cstack, bv, gate, xin, bias_cols):
    nh2, n1 = cstack.shape
    nh = nh2 // 2
    cols = bv.shape[-1]
    tc = _pick_tile(cols, (4096, 2048, 1024, 512, 256, 128))
    row = lambda bi: pl.BlockSpec((1, nh, tc), lambda j: (bi, 0, j))
    return pl.pallas_call(
        functools.partial(_idft_cols_kernel, nh=nh),
        grid=(cols // tc,),
        in_specs=[pl.BlockSpec((nh2, n1), lambda j: (0, 0)),
                  pl.BlockSpec((2, n1, tc), lambda j: (0, 0, j)),
                  row(0), row(1), row(0), row(1),
                  pl.BlockSpec((1, tc), lambda j: (0, j))],
        out_specs=pl.BlockSpec((2, nh, tc), lambda j: (0, 0, j)),
        out_shape=jax.ShapeDtypeStruct((2, nh, cols), F32),
        compiler_params=_cparams(("parallel",)),
        name="idft_cols",
    )(cstack, bv, gate, gate, xin, xin, bias_cols)


def _hy_ctx_kernel(x1_ref, x2_ref, v_ref, fw_ref, bw_ref, nrm_ref, bias_ref, ff_ref, ci_ref, o_ref, *, lc):
    n = 2 * lc
    c = HY_WIDTH
    ff = ff_ref[...]
    ci = ci_ref[...]
    pf = _mm(ff, fw_ref[...])
    pb = _mm(ff, bw_ref[...])
    s = 1.0 / (nrm_ref[...] * float(n))
    kr = (pf[:n] + pb[:n]) * s
    ki = (pf[n:] - pb[n:]) * s

    def conv(z0, z1, o):
        xr, xi = _cplx_left(ff, z0, z1, n)
        krr = kr[:, o * c:(o + 1) * c]
        kii = ki[:, o * c:(o + 1) * c]
        return _cplx_left(ci, xr * krr - xi * kii, xr * kii + xi * krr, lc)

    v0 = v_ref[0]
    v1 = v_ref[1]
    y0, y1 = conv(v0, v1, 0)
    z0 = x1_ref[0] * (y0 + v0 * bias_ref[0:1])
    z1 = x1_ref[1] * (y1 + v1 * bias_ref[0:1])
    y0, y1 = conv(z0, z1, 1)
    o_ref[0] = x2_ref[0] * (y0 + z0 * bias_ref[1:2])
    o_ref[1] = x2_ref[1] * (y1 + z1 * bias_ref[1:2])


def _hy_ctx(x1, x2, v, fw, bw, nrm, bias, ff, ci):
    b, lc, c = v.shape
    vm = pl.BlockSpec(memory_space=pltpu.VMEM)
    return pl.pallas_call(
        functools.partial(_hy_ctx_kernel, lc=lc),
        in_specs=[vm] * 9,
        out_specs=vm,
        out_shape=jax.ShapeDtypeStruct((b, lc, c), F32),
        compiler_params=pltpu.CompilerParams(vmem_limit_bytes=VMEM_LIMIT),
        name="hy_ctx",
    )(x1, x2, v, fw, bw, nrm, bias, ff, ci)


def _outproj_kernel(att_refs, hy_refs, rw_ref, x_ref, mod_ref, w_ref, lg_ref, lb_ref, o_ref, *, lc, tm):
    i = pl.program_id(1)
    row0 = i * tm
    a0 = ATT_WIDTH
    a1 = ATT_WIDTH + RWKV_WIDTH
    ns = tm // TOK_TILE

    def rows(refs):
        return jnp.concatenate([jnp.where(row0 + s * TOK_TILE < lc, refs[2 * s][0], refs[2 * s + 1][0])
                                for s in range(ns)], 0)

    att = rows(att_refs)
    hy = rows(hy_refs)
    o = jnp.dot(att, w_ref[:a0], preferred_element_type=F32)
    o += jnp.dot(rw_ref[0], w_ref[a0:a1], preferred_element_type=F32)
    o += jnp.dot(hy.astype(BF16), w_ref[a1:], preferred_element_type=F32)
    g = _sel_mod(mod_ref, 2, row0, tm, lc)
    y = ALPHA * x_ref[0] + g * o
    o_ref[0] = _layer_norm(y) * lg_ref[...] + lb_ref[...]


def _outproj(att_c, att_l, rw, hy_c, hy_l, xx, mod, w, lg, lb, lc):
    b, lt, d = xx.shape
    ts = TOK_TILE
    tm = _pick_tile(lt, (ROW_TILE_WIDE, ROW_TILE_MID, TOK_TILE))
    ns = tm // ts
    nct = lc // ts
    tok = lambda c: pl.BlockSpec((1, tm, c), lambda bi, i: (bi, i, 0))
    ctx = lambda c, s: pl.BlockSpec((1, ts, c), lambda bi, i: (bi, jnp.minimum(i * ns + s, nct - 1), 0))
    lat = lambda c, s: pl.BlockSpec((1, ts, c), lambda bi, i: (bi, jnp.maximum(i * ns + s - nct, 0), 0))
    pairs = lambda c: [spec(c, s) for s in range(ns) for spec in (ctx, lat)]
    full = lambda shape: pl.BlockSpec(shape, lambda bi, i: (0,) * len(shape))
    kern = functools.partial(_outproj_kernel, lc=lc, tm=tm)
    return pl.pallas_call(
        lambda *refs: kern(refs[:2 * ns], refs[2 * ns:4 * ns], *refs[4 * ns:]),
        grid=(b, lt // tm),
        in_specs=pairs(ATT_WIDTH) + pairs(HY_WIDTH) + [
                  tok(RWKV_WIDTH), tok(d), pl.BlockSpec((1, 2, 6, d), lambda bi, i: (bi, 0, 0, 0)),
                  full(w.shape), full((1, d)), full((1, d))],
        out_specs=tok(d),
        out_shape=jax.ShapeDtypeStruct((b, lt, d), F32),
        compiler_params=_cparams(("parallel", "parallel")),
        name="outproj",
    )(*([att_c, att_l] * ns), *([hy_c, hy_l] * ns), rw, xx, mod, w, lg, lb)


def _ffn_kernel(x_ref, mod_ref, w1_ref, w3_ref, w2_ref, lg_ref, lb_ref, o_ref, h_scr, acc_scr, *, lc, tm, nf):
    row0 = pl.program_id(1) * tm
    f = pl.program_id(2)

    @pl.when(f == 0)
    def _():
        sh = _sel_mod(mod_ref, 3, row0, tm, lc)
        sc = _sel_mod(mod_ref, 4, row0, tm, lc)
        h_scr[...] = (_layer_norm(x_ref[0]) * (1.0 + sc) + sh).astype(BF16)
        acc_scr[...] = jnp.zeros(acc_scr.shape, F32)

    h = h_scr[...]
    a = jnp.dot(h, w1_ref[...], preferred_element_type=F32)
    g = jnp.dot(h, w3_ref[...], preferred_element_type=F32)
    acc_scr[...] += jnp.dot((_silu(a) * g).astype(BF16), w2_ref[...], preferred_element_type=F32)

    @pl.when(f == nf - 1)
    def _():
        gate = _sel_mod(mod_ref, 5, row0, tm, lc)
        y = ALPHA * x_ref[0] + gate * acc_scr[...]
        o_ref[0] = _layer_norm(y) * lg_ref[...] + lb_ref[...]


def _ffn(xx, mod, w1, w3, w2, lg, lb, lc):
    b, lt, d = xx.shape
    ff = w1.shape[1]
    tm = _pick_tile(lt, (1280, 768, 512, 256))
    tf = _pick_tile(ff, (256, 128))
    nf = ff // tf
    return pl.pallas_call(
        functools.partial(_ffn_kernel, lc=lc, tm=tm, nf=nf),
        grid=(b, lt // tm, nf),
        in_specs=[pl.BlockSpec((1, tm, d), lambda bi, i, f: (bi, i, 0)),
                  pl.BlockSpec((1, 2, 6, d), lambda bi, i, f: (bi, 0, 0, 0)),
                  pl.BlockSpec((d, tf), lambda bi, i, f: (0, f)),
                  pl.BlockSpec((d, tf), lambda bi, i, f: (0, f)),
                  pl.BlockSpec((tf, d), lambda bi, i, f: (f, 0)),
                  pl.BlockSpec((1, d), lambda bi, i, f: (0, 0)),
                  pl.BlockSpec((1, d), lambda bi, i, f: (0, 0))],
        out_specs=pl.BlockSpec((1, tm, d), lambda bi, i, f: (bi, i, 0)),
        out_shape=jax.ShapeDtypeStruct((b, lt, d), F32),
        scratch_shapes=[pltpu.VMEM((tm, d), BF16), pltpu.VMEM((tm, d), F32)],
        compiler_params=_cparams(("parallel", "parallel", "arbitrary")),
        name="ffn",
    )(xx, mod, w1, w3, w2, lg, lb)


def _route(x_ref, mod_ref, wr_ref, row0, tm, lc):
    sh = _sel_mod(mod_ref, 3, row0, tm, lc)
    sc = _sel_mod(mod_ref, 4, row0, tm, lc)
    h = _layer_norm(x_ref[0]) * (1.0 + sc) + sh
    logits = _dotp(h, wr_ref[...], 3)
    lane = lax.broadcasted_iota(jnp.int32, logits.shape, 1)
    neg = jnp.float32(-jnp.inf)
    lg = jnp.where(lane < N_EXPERTS, logits, neg)
    m1 = jnp.max(lg, -1, keepdims=True)
    i1 = jnp.min(jnp.where(lg == m1, lane, LANES), -1, keepdims=True)
    lg2 = jnp.where(lane == i1, neg, lg)
    m2 = jnp.max(lg2, -1, keepdims=True)
    i2 = jnp.min(jnp.where(lg2 == m2, lane, LANES), -1, keepdims=True)
    e2 = jnp.exp(m2 - m1)
    return h, lane, i1, i2, 1.0 / (1.0 + e2), e2 / (1.0 + e2)


def _moe_sparse_kernel(x_ref, mod_ref, wr_ref, w1_ref, w3_ref, w2_ref, lg_ref, lb_ref, o_ref,
                       h_scr, acc_scr, gate_scr, posc_scr, posr_scr, xs_scr, ye_scr, nblk_scr, *, lc, tm, nf, ns, blk):
    row0 = pl.program_id(1) * tm
    s = pl.program_id(2)
    e = s // nf
    f = s % nf

    @pl.when(s == 0)
    def _():
        h, lane, i1, i2, g1, g2 = _route(x_ref, mod_ref, wr_ref, row0, tm, lc)
        h_scr[...] = h.astype(BF16)
        acc_scr[...] = jnp.zeros(acc_scr.shape, F32)
        routed = jnp.where(jnp.logical_or(lane == i1, lane == i2), 1.0, 0.0)
        r = lax.broadcasted_iota(jnp.int32, (tm, tm), 0)
        c = lax.broadcasted_iota(jnp.int32, (tm, tm), 1)
        before = jnp.where(c < r, 1.0, 0.0).astype(BF16)
        rank_c = jnp.dot(before, routed.astype(BF16), preferred_element_type=F32)
        gate_scr[...] = jnp.where(lane == i1, g1, 0.0) + jnp.where(lane == i2, g2, 0.0)
        posc_scr[...] = jnp.where(routed > 0.0, rank_c, -1.0)
        routed_t = routed.T[:2 * SUBLANES]
        after = jnp.where(r < c, 1.0, 0.0).astype(BF16)
        rank_r = jnp.dot(routed_t.astype(BF16), after, preferred_element_type=F32)
        posr_scr[...] = jnp.where(routed_t > 0.0, rank_r, -1.0)
        counts = jnp.sum(routed, 0, keepdims=True)
        lane1 = lax.broadcasted_iota(jnp.int32, counts.shape, 1)
        for ex in range(N_EXPERTS):
            n_rows = jnp.sum(jnp.where(lane1 == ex, counts, 0.0)).astype(jnp.int32)
            nblk_scr[ex] = (n_rows + (blk - 1)) // blk

    nblk = nblk_scr[e]
    mv = MOE_MOVE_BLOCK
    nmv = (nblk * blk + (mv - 1)) // mv
    rows = lambda i: pl.ds(pl.multiple_of(i * mv, mv), mv)

    @pl.when(f == 0)
    def _():
        gv = MOE_GATHER_BLOCK

        def gather(i, carry):
            blk_rows = pl.ds(pl.multiple_of(i * gv, gv), gv)
            slot = (lax.broadcasted_iota(jnp.int32, (gv, tm), 0) + i * gv).astype(F32)
            take = jnp.where(slot == posr_scr[pl.ds(e, 1), :], 1.0, 0.0).astype(BF16)
            xs_scr[blk_rows, :] = jnp.dot(take, h_scr[...], preferred_element_type=F32).astype(BF16)
            ye_scr[blk_rows, :] = jnp.zeros((gv, ye_scr.shape[1]), F32)
            return carry

        lax.fori_loop(0, (nblk * blk + (gv - 1)) // gv, gather, 0)

    def expert(m):
        xs = xs_scr[:m]
        a = jnp.dot(xs, w1_ref[0], preferred_element_type=F32)
        g = jnp.dot(xs, w3_ref[0], preferred_element_type=F32)
        ye_scr[:m] += jnp.dot((_silu(a) * g).astype(BF16), w2_ref[0], preferred_element_type=F32)

    for k in range(tm // blk):
        pl.when(nblk == k + 1)(functools.partial(expert, (k + 1) * blk))

    @pl.when(f == nf - 1)
    def _():
        lane = lax.broadcasted_iota(jnp.int32, (tm, LANES), 1)
        column = lambda ref: jnp.sum(jnp.where(lane == e, ref[...], 0.0), -1, keepdims=True)
        pos = column(posc_scr)
        gate = column(gate_scr)

        def scatter(i, carry):
            slot = (lax.broadcasted_iota(jnp.int32, (tm, mv), 1) + i * mv).astype(F32)
            put = jnp.where(slot == pos, 1.0, 0.0).astype(BF16)
            acc_scr[...] += gate * jnp.dot(put, ye_scr[rows(i), :].astype(BF16), preferred_element_type=F32)
            return carry

        lax.fori_loop(0, nmv, scatter, 0)

    @pl.when(s == ns - 1)
    def _():
        gate = _sel_mod(mod_ref, 5, row0, tm, lc)
        y = ALPHA * x_ref[0] + gate * acc_scr[...]
        o_ref[0] = _layer_norm(y) * lg_ref[...] + lb_ref[...]


def _moe_sparse(xx, mod, wr, w1, w3, w2, lg, lb, lc):
    b, lt, d = xx.shape
    ne, _, ff = w1.shape
    tm = _pick_tile(lt, (1280, 768, 512, 256))
    tf = _pick_tile(ff, (256, 128))
    nf = ff // tf
    ns = ne * nf
    cap = -(-tm // MOE_GATHER_BLOCK) * MOE_GATHER_BLOCK
    return pl.pallas_call(
        functools.partial(_moe_sparse_kernel, lc=lc, tm=tm, nf=nf, ns=ns, blk=MOE_ROW_BLOCK),
        grid=(b, lt // tm, ns),
        in_specs=[pl.BlockSpec((1, tm, d), lambda bi, i, s: (bi, i, 0)),
                  pl.BlockSpec((1, 2, 6, d), lambda bi, i, s: (bi, 0, 0, 0)),
                  pl.BlockSpec((d, LANES), lambda bi, i, s: (0, 0)),
                  pl.BlockSpec((1, d, tf), lambda bi, i, s: (s // nf, 0, s % nf)),
                  pl.BlockSpec((1, d, tf), lambda bi, i, s: (s // nf, 0, s % nf)),
                  pl.BlockSpec((1, tf, d), lambda bi, i, s: (s // nf, s % nf, 0)),
                  pl.BlockSpec((1, d), lambda bi, i, s: (0, 0)),
                  pl.BlockSpec((1, d), lambda bi, i, s: (0, 0))],
        out_specs=pl.BlockSpec((1, tm, d), lambda bi, i, s: (bi, i, 0)),
        out_shape=jax.ShapeDtypeStruct((b, lt, d), F32),
        scratch_shapes=[pltpu.VMEM((tm, d), BF16), pltpu.VMEM((tm, d), F32),
                        pltpu.VMEM((tm, LANES), F32), pltpu.VMEM((tm, LANES), F32),
                        pltpu.VMEM((2 * SUBLANES, tm), F32),
                        pltpu.VMEM((cap, d), BF16), pltpu.VMEM((cap, d), F32),
                        pltpu.SMEM((ne,), jnp.int32)],
        compiler_params=_cparams(("parallel", "parallel", "arbitrary"), VMEM_LIMIT_MOE),
        name="moe_sparse",
    )(xx, mod, wr, w1, w3, w2, lg, lb)


def _rope_tables(l, lc):
    rows = l // GRID_W
    row = jnp.repeat(jnp.arange(rows, dtype=F32), GRID_W)
    col = jnp.tile(jnp.arange(GRID_W, dtype=F32), rows)
    n_freq = HEAD_DIM // 4
    inv_freq = ROPE_THETA ** (-jnp.arange(n_freq, dtype=F32) / n_freq)
    ang = jnp.concatenate([row[:, None] * inv_freq, col[:, None] * inv_freq], -1)
    cos, sin = jnp.cos(ang), jnp.sin(ang)
    cos2 = jnp.concatenate([jnp.ones((lc, LANES), F32), jnp.concatenate([cos, cos, cos, cos], -1)], 0)
    sin2 = jnp.concatenate([jnp.zeros((lc, LANES), F32), jnp.concatenate([-sin, sin, -sin, sin], -1)], 0)
    return cos2, sin2


def _hy_features(l):
    bands = (HY_EMB - 1) // 2
    t = jnp.linspace(0.0, 1.0, l, dtype=F32)[:, None]
    f = jnp.linspace(1e-4, bands - 1, bands, dtype=F32)[None, :]
    wt = 2.0 * math.pi * jnp.arange(l, dtype=F32)[:, None] / l
    z = jnp.concatenate([t, jnp.cos(f * wt), -jnp.sin(f * wt)], -1)
    return jnp.pad(z, ((0, 0), (0, LANES - HY_EMB)))


def _angle(idx, n):
    return (2.0 * math.pi / n) * (idx % n).astype(F32)


def _dft_tables(n1):
    nh = n1 // 2
    n2 = DFT_N2
    n = n1 * n2
    k1 = jnp.arange(n1, dtype=jnp.int32)
    a1 = _angle(k1[:, None] * jnp.arange(nh, dtype=jnp.int32)[None, :], n1)
    fstack = jnp.concatenate([jnp.cos(a1), -jnp.sin(a1)], 0)
    cstack = jnp.concatenate([jnp.cos(a1.T), jnp.sin(a1.T)], 0)
    k2 = jnp.arange(n2, dtype=jnp.int32)
    at = _angle(k1[:, None] * k2[None, :], n)
    tr, ti = jnp.cos(at), -jnp.sin(at)
    a2 = _angle(k2[:, None] * k2[None, :], n2)
    fr, fi = jnp.cos(a2), -jnp.sin(a2)
    g = jnp.stack([tr[:, None, :] * fr[None] - ti[:, None, :] * fi[None],
                   tr[:, None, :] * fi[None] + ti[:, None, :] * fr[None]], 1)
    return fstack, cstack, g


def _dense_dft_tables(lc):
    n = 2 * lc
    a = _angle(jnp.arange(n, dtype=jnp.int32)[:, None] * jnp.arange(lc, dtype=jnp.int32)[None, :], n)
    ff = jnp.concatenate([jnp.cos(a), -jnp.sin(a)], 0)
    ci = jnp.concatenate([jnp.cos(a.T), jnp.sin(a.T)], 0)
    return ff, ci


def kernel(x, c, ctx, c_ctx, ada_w, ada_b, w_in, w_out, q_gain, k_gain, rwkv_mu, rwkv_w0, rwkv_wB, rwkv_a0, rwkv_aB, rwkv_gB, rwkv_kk, rwkv_ka, rwkv_rk, rwkv_gn_g, rwkv_gn_b, hy_short_w, hy_short_b, hy_w1, hy_b1, hy_freq1, hy_w2, hy_b2, hy_freq2, hy_w3, hy_decay, hy_bias, ln1_g, ln1_b, ln2_g, ln2_b, ffn_w1, ffn_w3, ffn_w2, moe_router, moe_w1, moe_w3, moe_w2):
    b, l, d = x.shape
    lc = ctx.shape[1]
    lt = lc + l
    depth = ada_w.shape[0]
    assert b == 2, "the long convolution packs the two batch rows as one complex signal"
    assert d == D_MODEL and lc % TOK_TILE == 0 and l % TOK_TILE == 0 and (2 * l) % (2 * DFT_N2) == 0
    cw = RWKV_WIDTH

    xx = jnp.concatenate([ctx, x], 1)
    cond8 = jnp.zeros((SUBLANES, d), F32).at[:b].set(c).at[b].set(c_ctx)
    mod_all = _ada_mod(cond8, ada_w, ada_b)

    cos64, sin64 = _rope_tables(l, lc)
    n1 = 2 * l // DFT_N2
    nh = n1 // 2
    cols = DFT_N2 * HY_WIDTH
    fstack, cstack, g_tab = (t.astype(BF16) for t in _dft_tables(n1))
    ff_c, ci_c = _dense_dft_tables(lc)
    feat_l = _hy_features(l)
    feat_c = _hy_features(lc)
    blk = jnp.arange(cw) // HEAD_DIM
    bd = (blk[:, None] == blk[None, :]).astype(F32)
    ch = jnp.arange(ATT_WIDTH)
    bd_att = (ch[:, None] // HEAD_DIM == ch[None, :] // HEAD_DIM).astype(BF16)
    swap_att = (ch[:, None] == (ch[None, :] + HALF_HD) % HEAD_DIM + (ch[None, :] // HEAD_DIM) * HEAD_DIM
                ).astype(BF16)
    perm64 = jnp.concatenate([jnp.arange(0, HEAD_DIM, 2), jnp.arange(1, HEAD_DIM, 2)])
    perm_att = jnp.concatenate([h * HEAD_DIM + perm64 for h in range(ATT_HEADS + ATT_KV_HEADS)]
                               + [jnp.arange(ATT_WIDTH + ATT_KV_WIDTH, IN_ATT)])
    tq = TOK_TILE
    qn = next(n for n in (4, 2, 1) if (l // tq) % n == 0)
    tk = _pick_tile(lt, (3328, 1280, 1024, 768, 512, 256))

    for li in range(depth):
        ml = mod_all[li]
        mod = jnp.stack([jnp.broadcast_to(ml[b].reshape(1, 6, d), (b, 6, d)), ml[:b].reshape(b, 6, d)], 1)
        wi = w_in[li]
        w_pad = jnp.concatenate([wi[:, :IN_ATT][:, perm_att], wi[:, IN_ATT:IN_ATT + IN_RWKV],
                                 jnp.zeros((d, IN_RWKV_PAD - IN_RWKV), F32), wi[:, IN_ATT + IN_RWKV:]],
                                1).astype(BF16)
        ua, ur, uh = _inproj(xx, mod, w_pad, lc)

        two = lambda gain: jnp.tile(gain[perm64], 2)[None]
        qt, kx, vt = _attn_prep(ua, cos64, sin64, two(q_gain[li]), two(k_gain[li]), bd_att, swap_att, lc)
        att_c = _flash(qt, kx, vt, tq, TOK_TILE, 1, 0, lc // tq, lc // TOK_TILE)
        s_bound = (HEAD_DIM ** 0.5) * LOG2E * jnp.max(jnp.abs(q_gain[li])) * jnp.max(jnp.abs(k_gain[li]))
        lat_args = (qt, kx, vt, tq, tk, qn, lc // tq, l // (qn * tq), lt // tk)
        att_l = lax.cond(s_bound <= MAX_UNSHIFTED_SCORE,
                         lambda: _flash(*lat_args, bounded=True), lambda: _flash(*lat_args, bounded=False))

        wl = jnp.zeros((LANES, 5 * cw), F32)
        wl = wl.at[0:W_LORA, 0:cw].set(rwkv_wB[li, 0]).at[W_LORA:2 * W_LORA, cw:2 * cw].set(rwkv_wB[li, 1])
        o_a = 2 * W_LORA
        wl = wl.at[o_a:o_a + A_LORA, 2 * cw:3 * cw].set(rwkv_aB[li, 0])
        wl = wl.at[o_a + A_LORA:o_a + 2 * A_LORA, 3 * cw:4 * cw].set(rwkv_aB[li, 1])
        o_g = o_a + 2 * A_LORA
        wl = wl.at[o_g:o_g + G_LORA, 4 * cw:5 * cw].set(rwkv_gB[li])
        mu = jnp.pad(rwkv_mu[li], ((0, 0), (0, IN_RWKV_PAD - IN_RWKV)))
        r_, v_, kk_, g_, bon_, lw_, be_, kd_ = _rwkv_prep(
            ur, mu, wl, bd, rwkv_w0[li], rwkv_a0[li], rwkv_kk[li][None], rwkv_ka[li][None],
            rwkv_rk[li].reshape(1, cw), lc)
        yf, yb = _wkv_scan(r_, v_, kk_, lw_, be_, kd_, lc)
        rw = _rwkv_out(yf, yb, bon_, g_, bd, rwkv_gn_g[li][None], rwkv_gn_b[li][None])

        x1, x2, vv = _hy_prep(uh, hy_short_w[li], hy_short_b[li][None], lc)
        w1p = jnp.pad(hy_w1[li], ((0, LANES - HY_EMB), (0, 0)))
        fargs = (w1p, hy_b1[li][None], hy_freq1[li][None], hy_w2[li], hy_b2[li][None], hy_freq2[li][None],
                 hy_w3[li], hy_decay[li][None])
        fw, bw, nrm = _hy_filter(feat_l, *fargs)
        c2 = HY_ORDER * HY_WIDTH
        a4 = _dft_cols(fstack, fw.reshape(1, nh, DFT_N2 * c2), 0, bw.reshape(1, nh, DFT_N2 * c2), 0, True)
        kspec = _spec(a4.reshape(4, n1, DFT_N2, c2), g_tab, nrm, n1 * DFT_N2)
        lat = lambda t: t[:, lc:].reshape(b, nh, cols)
        x1l, x2l, zin = lat(x1), lat(x2), lat(vv)
        for o, gate in enumerate((x1l, x2l)):
            a = _dft_cols(fstack, zin, 0, zin, 1, False)
            bv = _conv_mid(a.reshape(2, n1, DFT_N2, HY_WIDTH), g_tab, kspec, o)
            bias_cols = jnp.tile(hy_bias[li, o], DFT_N2)[None]
            zin = _idft_cols(cstack, bv.reshape(2, n1, cols), gate, zin, bias_cols)
        hy_l = zin.reshape(b, l, HY_WIDTH)
        fw_c, bw_c, nrm_c = _hy_filter(feat_c, *fargs)
        hy_c = _hy_ctx(x1[:, :lc], x2[:, :lc], vv[:, :lc], fw_c, bw_c, nrm_c, hy_bias[li], ff_c, ci_c)

        xx = _outproj(att_c, att_l, rw, hy_c, hy_l, xx, mod, w_out[li].astype(BF16),
                      ln1_g[li][None], ln1_b[li][None], lc)

        j = li // 2
        if li % 2 == 0:
            xx = _ffn(xx, mod, ffn_w1[j].astype(BF16), ffn_w3[j].astype(BF16), ffn_w2[j].astype(BF16),
                      ln2_g[li][None], ln2_b[li][None], lc)
        else:
            wr = jnp.pad(moe_router[j], ((0, 0), (0, LANES - N_EXPERTS)))
            xx = _moe_sparse(xx, mod, wr, moe_w1[j].astype(BF16), moe_w3[j].astype(BF16), moe_w2[j].astype(BF16),
                             ln2_g[li][None], ln2_b[li][None], lc)
    return xx[:, lc:]
```
